```python
import math
import jax, jax.numpy as jnp
from jax import lax
import numpy as np

D_MODEL = 1024
BATCH = 8
SEQ = 2048
DEPTH = 4

SG_WIDTH = 512
SG_GROUPS = 8
SG_GROUP_DIM = SG_WIDTH // SG_GROUPS
CHUNK = 128
CV_WIDTH = 512
CV_KERNEL = 31
SB_HEADS = 8
SB_HEAD_DIM = 64
SB_WIDTH = SB_HEADS * SB_HEAD_DIM
Q_BLOCK = 128
N_BRANCH = 3
D_FF = 2816
FFN_KERNEL = 3

IN_COLS = 2 * SG_WIDTH + 2 * CV_WIDTH + 3 * SB_WIDTH + N_BRANCH * D_MODEL
EPS = 1e-6

kernel_name = "hybrid_sgu_conformer_stickbreak_block"


def rms_norm(x, g):
    xf = x.astype(jnp.float32)
    y = xf * lax.rsqrt(jnp.mean(xf * xf, axis=-1, keepdims=True) + EPS)
    return (y * g.astype(jnp.float32)).astype(x.dtype)


def layer_norm(x, g, b):
    xf = x.astype(jnp.float32)
    mu = jnp.mean(xf, axis=-1, keepdims=True)
    xc = xf - mu
    y = xc * lax.rsqrt(jnp.mean(xc * xc, axis=-1, keepdims=True) + EPS)
    return (y * g.astype(jnp.float32) + b.astype(jnp.float32)).astype(x.dtype)


def causal_depthwise_conv(x, w, b):
    k_width, ch = w.shape
    y = lax.conv_general_dilated(
        x, w[:, None, :].astype(x.dtype), window_strides=(1,), padding=[(k_width - 1, 0)],
        dimension_numbers=("NWC", "WIO", "NWC"), feature_group_count=ch)
    return y + b.astype(x.dtype)


def chunked_spatial_gating(v, w_s, b_s):
    bsz, t, _ = v.shape
    vc = v.reshape(bsz, t // CHUNK, CHUNK, SG_GROUPS, SG_GROUP_DIM)
    tril = jnp.tril(jnp.ones((CHUNK, CHUNK), dtype=bool))
    w_m = jnp.where(tril[None], w_s, jnp.zeros_like(w_s)).astype(v.dtype)
    out = jnp.einsum("gts,bcsgd->bctgd", w_m, vc) + b_s.T.astype(v.dtype)[:, :, None]
    return out.reshape(bsz, t, SG_WIDTH)


def stick_breaking_attention(q, k, v):
    bsz, t, h, dh = q.shape
    q = q.transpose(0, 2, 1, 3)
    k = k.transpose(0, 2, 1, 3)
    v = v.transpose(0, 2, 1, 3)
    scale = 1.0 / math.sqrt(dh)
    outs = []
    for i in range(t // Q_BLOCK):
        ctx = (i + 1) * Q_BLOCK
        q_blk = q[:, :, i * Q_BLOCK:ctx]
        z = jnp.einsum("bhqd,bhkd->bhqk", q_blk, k[:, :, :ctx]).astype(jnp.float32) * scale
        q_pos = i * Q_BLOCK + jnp.arange(Q_BLOCK)
        k_pos = jnp.arange(ctx)
        mask = k_pos[None, :] < q_pos[:, None]
        log_1m_beta = jnp.where(mask, jax.nn.log_sigmoid(-z), 0.0)
        cs = lax.cumsum(log_1m_beta, axis=3)
        log_a = jax.nn.log_sigmoid(z) + (cs[..., -1:] - cs)
        a = jnp.where(mask, jnp.exp(log_a), 0.0)
        outs.append(jnp.einsum("bhqk,bhkd->bhqd", a.astype(v.dtype), v[:, :, :ctx]))
    o = jnp.concatenate(outs, axis=2)
    return o.transpose(0, 2, 1, 3).reshape(bsz, t, h * dh)


def _fwd_setup_inputs(seed: int = 0) -> dict:
    key = jax.random.key(seed)
    ks = jax.random.split(key, 24)
    f32 = jnp.float32
    L, D = DEPTH, D_MODEL

    def nrm(k, shape, scale):
        return jax.random.normal(k, shape, f32) * scale

    return {
        "x": jax.random.normal(ks[0], (BATCH, SEQ, D), f32),
        "ln1_g": 1.0 + nrm(ks[1], (L, D), 0.05),
        "w_in": nrm(ks[2], (L, D, IN_COLS), D ** -0.5),
        "b_gate": nrm(ks[3], (L, N_BRANCH, D), 0.1),
        "sg_ln_g": 1.0 + nrm(ks[4], (L, SG_WIDTH), 0.05),
        "sg_ln_b": nrm(ks[5], (L, SG_WIDTH), 0.05),
        "sg_w": nrm(ks[6], (L, SG_GROUPS, CHUNK, CHUNK), CHUNK ** -0.5),
        "sg_b": 1.0 + nrm(ks[7], (L, SG_GROUPS, CHUNK), 0.1),
        "w_a_out": nrm(ks[8], (L, SG_WIDTH, D), SG_WIDTH ** -0.5),
        "cv_w": nrm(ks[9], (L, CV_KERNEL, CV_WIDTH), CV_KERNEL ** -0.5),
        "cv_b": nrm(ks[10], (L, CV_WIDTH), 0.05),
        "cv_ln_g": 1.0 + nrm(ks[11], (L, CV_WIDTH), 0.05),
        "cv_ln_b": nrm(ks[12], (L, CV_WIDTH), 0.05),
        "w_b_out": nrm(ks[13], (L, CV_WIDTH, D), CV_WIDTH ** -0.5),
        "q_norm_g": 1.0 + nrm(ks[14], (L, SB_HEAD_DIM), 0.05),
        "k_norm_g": 1.0 + nrm(ks[15], (L, SB_HEAD_DIM), 0.05),
        "w_c_out": nrm(ks[16], (L, SB_WIDTH, D), SB_WIDTH ** -0.5),
        "w_out": nrm(ks[17], (L, D, D), D ** -0.5),
        "ln2_g": 1.0 + nrm(ks[18], (L, D), 0.05),
        "w_up": nrm(ks[19], (L, D, 2 * D_FF), D ** -0.5),
        "ffn_conv_w": nrm(ks[20], (L, FFN_KERNEL, 2 * D_FF), FFN_KERNEL ** -0.5),
        "ffn_conv_b": nrm(ks[21], (L, 2 * D_FF), 0.05),
        "w_down": nrm(ks[22], (L, D_FF, D), D_FF ** -0.5),
    }


def _fwd_reference(x, ln1_g, w_in, b_gate, sg_ln_g, sg_ln_b, sg_w, sg_b, w_a_out,
              cv_w, cv_b, cv_ln_g, cv_ln_b, w_b_out, q_norm_g, k_norm_g, w_c_out,
              w_out, ln2_g, w_up, ffn_conv_w, ffn_conv_b, w_down):
    bsz, t, d = x.shape
    split_at = np.cumsum([2 * SG_WIDTH, 2 * CV_WIDTH, 3 * SB_WIDTH]).tolist()
    for l in range(DEPTH):
        h = rms_norm(x, ln1_g[l])
        z = h @ w_in[l]
        z_a, z_b, z_c, z_g = jnp.split(z, split_at, axis=-1)

        a = jax.nn.gelu(z_a)
        u, v = jnp.split(a, 2, axis=-1)
        v = layer_norm(v, sg_ln_g[l], sg_ln_b[l])
        y_a = (u * chunked_spatial_gating(v, sg_w[l], sg_b[l])) @ w_a_out[l]

        p, g_lin = jnp.split(z_b, 2, axis=-1)
        c = p * jax.nn.sigmoid(g_lin)
        c = causal_depthwise_conv(c, cv_w[l], cv_b[l])
        c = jax.nn.silu(layer_norm(c, cv_ln_g[l], cv_ln_b[l]))
        y_b = c @ w_b_out[l]

        q, k, vv = jnp.split(z_c, 3, axis=-1)
        q = rms_norm(q.reshape(bsz, t, SB_HEADS, SB_HEAD_DIM), q_norm_g[l])
        k = rms_norm(k.reshape(bsz, t, SB_HEADS, SB_HEAD_DIM), k_norm_g[l])
        vv = vv.reshape(bsz, t, SB_HEADS, SB_HEAD_DIM)
        y_c = stick_breaking_attention(q, k, vv) @ w_c_out[l]

        gates = jax.nn.sigmoid(z_g.reshape(bsz, t, N_BRANCH, d) + b_gate[l].astype(z_g.dtype))
        merged = gates[:, :, 0] * y_a + gates[:, :, 1] * y_b + gates[:, :, 2] * y_c
        x = x + merged @ w_out[l]

        h2 = rms_norm(x, ln2_g[l])
        up = causal_depthwise_conv(h2 @ w_up[l], ffn_conv_w[l], ffn_conv_b[l])
        gate, val = jnp.split(up, 2, axis=-1)
        x = x + (jax.nn.silu(gate) * val) @ w_down[l]
    return x


import jax as _jax
import jax.numpy as _jnp

TWIN_FORMAT = 'train_step'
FWD_PARAMS = ['x', 'ln1_g', 'w_in', 'b_gate', 'sg_ln_g', 'sg_ln_b', 'sg_w', 'sg_b', 'w_a_out', 'cv_w', 'cv_b', 'cv_ln_g', 'cv_ln_b', 'w_b_out', 'q_norm_g', 'k_norm_g', 'w_c_out', 'w_out', 'ln2_g', 'w_up', 'ffn_conv_w', 'ffn_conv_b', 'w_down']
TWIN_WEIGHTS = ['ln1_g', 'w_in', 'b_gate', 'sg_ln_g', 'sg_ln_b', 'sg_w', 'sg_b', 'w_a_out', 'cv_w', 'cv_b', 'cv_ln_g', 'cv_ln_b', 'w_b_out', 'q_norm_g', 'k_norm_g', 'w_c_out', 'w_out', 'ln2_g', 'w_up', 'ffn_conv_w', 'ffn_conv_b', 'w_down']
TWIN_DIFF_INPUT = 'x'
TWIN_INPUTS = ['x', 'ln1_g', 'w_in', 'b_gate', 'sg_ln_g', 'sg_ln_b', 'sg_w', 'sg_b', 'w_a_out', 'cv_w', 'cv_b', 'cv_ln_g', 'cv_ln_b', 'w_b_out', 'q_norm_g', 'k_norm_g', 'w_c_out', 'w_out', 'ln2_g', 'w_up', 'ffn_conv_w', 'ffn_conv_b', 'w_down', 'loss_target', 'm_ln1_g', 'm_w_in', 'm_b_gate', 'm_sg_ln_g', 'm_sg_ln_b', 'm_sg_w', 'm_sg_b', 'm_w_a_out', 'm_cv_w', 'm_cv_b', 'm_cv_ln_g', 'm_cv_ln_b', 'm_w_b_out', 'm_q_norm_g', 'm_k_norm_g', 'm_w_c_out', 'm_w_out', 'm_ln2_g', 'm_w_up', 'm_ffn_conv_w', 'm_ffn_conv_b', 'm_w_down', 'v_ln1_g', 'v_w_in', 'v_b_gate', 'v_sg_ln_g', 'v_sg_ln_b', 'v_sg_w', 'v_sg_b', 'v_w_a_out', 'v_cv_w', 'v_cv_b', 'v_cv_ln_g', 'v_cv_ln_b', 'v_w_b_out', 'v_q_norm_g', 'v_k_norm_g', 'v_w_c_out', 'v_w_out', 'v_ln2_g', 'v_w_up', 'v_ffn_conv_w', 'v_ffn_conv_b', 'v_w_down']
TWIN_OUTPUTS = ['loss', 'grad_x', 'grad_ln1_g', 'grad_w_in', 'grad_b_gate', 'grad_sg_ln_g', 'grad_sg_ln_b', 'grad_sg_w', 'grad_sg_b', 'grad_w_a_out', 'grad_cv_w', 'grad_cv_b', 'grad_cv_ln_g', 'grad_cv_ln_b', 'grad_w_b_out', 'grad_q_norm_g', 'grad_k_norm_g', 'grad_w_c_out', 'grad_w_out', 'grad_ln2_g', 'grad_w_up', 'grad_ffn_conv_w', 'grad_ffn_conv_b', 'grad_w_down', 'delta_ln1_g', 'delta_w_in', 'delta_b_gate', 'delta_sg_ln_g', 'delta_sg_ln_b', 'delta_sg_w', 'delta_sg_b', 'delta_w_a_out', 'delta_cv_w', 'delta_cv_b', 'delta_cv_ln_g', 'delta_cv_ln_b', 'delta_w_b_out', 'delta_q_norm_g', 'delta_k_norm_g', 'delta_w_c_out', 'delta_w_out', 'delta_ln2_g', 'delta_w_up', 'delta_ffn_conv_w', 'delta_ffn_conv_b', 'delta_w_down', 'new_m_ln1_g', 'new_m_w_in', 'new_m_b_gate', 'new_m_sg_ln_g', 'new_m_sg_ln_b', 'new_m_sg_w', 'new_m_sg_b', 'new_m_w_a_out', 'new_m_cv_w', 'new_m_cv_b', 'new_m_cv_ln_g', 'new_m_cv_ln_b', 'new_m_w_b_out', 'new_m_q_norm_g', 'new_m_k_norm_g', 'new_m_w_c_out', 'new_m_w_out', 'new_m_ln2_g', 'new_m_w_up', 'new_m_ffn_conv_w', 'new_m_ffn_conv_b', 'new_m_w_down', 'new_v_ln1_g', 'new_v_w_in', 'new_v_b_gate', 'new_v_sg_ln_g', 'new_v_sg_ln_b', 'new_v_sg_w', 'new_v_sg_b', 'new_v_w_a_out', 'new_v_cv_w', 'new_v_cv_b', 'new_v_cv_ln_g', 'new_v_cv_ln_b', 'new_v_w_b_out', 'new_v_q_norm_g', 'new_v_k_norm_g', 'new_v_w_c_out', 'new_v_w_out', 'new_v_ln2_g', 'new_v_w_up', 'new_v_ffn_conv_w', 'new_v_ffn_conv_b', 'new_v_w_down']
TWIN_LEAF_KINDS = {'loss': 'loss', 'grad_x': 'grad_x', 'grad_ln1_g': 'grad_w', 'grad_w_in': 'grad_w', 'grad_b_gate': 'grad_w', 'grad_sg_ln_g': 'grad_w', 'grad_sg_ln_b': 'grad_w', 'grad_sg_w': 'grad_w', 'grad_sg_b': 'grad_w', 'grad_w_a_out': 'grad_w', 'grad_cv_w': 'grad_w', 'grad_cv_b': 'grad_w', 'grad_cv_ln_g': 'grad_w', 'grad_cv_ln_b': 'grad_w', 'grad_w_b_out': 'grad_w', 'grad_q_norm_g': 'grad_w', 'grad_k_norm_g': 'grad_w', 'grad_w_c_out': 'grad_w', 'grad_w_out': 'grad_w', 'grad_ln2_g': 'grad_w', 'grad_w_up': 'grad_w', 'grad_ffn_conv_w': 'grad_w', 'grad_ffn_conv_b': 'grad_w', 'grad_w_down': 'grad_w', 'delta_ln1_g': 'delta_w', 'delta_w_in': 'delta_w', 'delta_b_gate': 'delta_w', 'delta_sg_ln_g': 'delta_w', 'delta_sg_ln_b': 'delta_w', 'delta_sg_w': 'delta_w', 'delta_sg_b': 'delta_w', 'delta_w_a_out': 'delta_w', 'delta_cv_w': 'delta_w', 'delta_cv_b': 'delta_w', 'delta_cv_ln_g': 'delta_w', 'delta_cv_ln_b': 'delta_w', 'delta_w_b_out': 'delta_w', 'delta_q_norm_g': 'delta_w', 'delta_k_norm_g': 'delta_w', 'delta_w_c_out': 'delta_w', 'delta_w_out': 'delta_w', 'delta_ln2_g': 'delta_w', 'delta_w_up': 'delta_w', 'delta_ffn_conv_w': 'delta_w', 'delta_ffn_conv_b': 'delta_w', 'delta_w_down': 'delta_w', 'new_m_ln1_g': 'new_m', 'new_m_w_in': 'new_m', 'new_m_b_gate': 'new_m', 'new_m_sg_ln_g': 'new_m', 'new_m_sg_ln_b': 'new_m', 'new_m_sg_w': 'new_m', 'new_m_sg_b': 'new_m', 'new_m_w_a_out': 'new_m', 'new_m_cv_w': 'new_m', 'new_m_cv_b': 'new_m', 'new_m_cv_ln_g': 'new_m', 'new_m_cv_ln_b': 'new_m', 'new_m_w_b_out': 'new_m', 'new_m_q_norm_g': 'new_m', 'new_m_k_norm_g': 'new_m', 'new_m_w_c_out': 'new_m', 'new_m_w_out': 'new_m', 'new_m_ln2_g': 'new_m', 'new_m_w_up': 'new_m', 'new_m_ffn_conv_w': 'new_m', 'new_m_ffn_conv_b': 'new_m', 'new_m_w_down': 'new_m', 'new_v_ln1_g': 'new_v', 'new_v_w_in': 'new_v', 'new_v_b_gate': 'new_v', 'new_v_sg_ln_g': 'new_v', 'new_v_sg_ln_b': 'new_v', 'new_v_sg_w': 'new_v', 'new_v_sg_b': 'new_v', 'new_v_w_a_out': 'new_v', 'new_v_cv_w': 'new_v', 'new_v_cv_b': 'new_v', 'new_v_cv_ln_g': 'new_v', 'new_v_cv_ln_b': 'new_v', 'new_v_w_b_out': 'new_v', 'new_v_q_norm_g': 'new_v', 'new_v_k_norm_g': 'new_v', 'new_v_w_c_out': 'new_v', 'new_v_w_out': 'new_v', 'new_v_ln2_g': 'new_v', 'new_v_w_up': 'new_v', 'new_v_ffn_conv_w': 'new_v', 'new_v_ffn_conv_b': 'new_v', 'new_v_w_down': 'new_v'}


def _forward(args):
    return _fwd_reference(*[args[k] for k in FWD_PARAMS])


def _output_shape():
    out = _jax.eval_shape(lambda: _forward(_fwd_setup_inputs(0)))
    return out.shape, out.dtype

N_MICROBATCH = 1
ADAM_LR = 0.001
ADAM_B1 = 0.9
ADAM_B2 = 0.999
ADAM_EPS = 1e-08
ADAM_WD = 0.01
ADAM_STEP = 10
PER_EXAMPLE_BATCH_AXIS = {'x': 0, 'loss_target': 0}
SHARED_INPUTS = []
_WEIGHT_DTYPES = {'ln1_g': _jnp.float32, 'w_in': _jnp.float32, 'b_gate': _jnp.float32, 'sg_ln_g': _jnp.float32, 'sg_ln_b': _jnp.float32, 'sg_w': _jnp.float32, 'sg_b': _jnp.float32, 'w_a_out': _jnp.float32, 'cv_w': _jnp.float32, 'cv_b': _jnp.float32, 'cv_ln_g': _jnp.float32, 'cv_ln_b': _jnp.float32, 'w_b_out': _jnp.float32, 'q_norm_g': _jnp.float32, 'k_norm_g': _jnp.float32, 'w_c_out': _jnp.float32, 'w_out': _jnp.float32, 'ln2_g': _jnp.float32, 'w_up': _jnp.float32, 'ffn_conv_w': _jnp.float32, 'ffn_conv_b': _jnp.float32, 'w_down': _jnp.float32}
MOMENT_SCALE = {'ln1_g': 6.378163e+00, 'w_in': 3.923560e-01, 'b_gate': 9.689109e-01, 'sg_ln_g': 2.050134e+00, 'sg_ln_b': 3.606847e-01, 'sg_w': 1.866585e-01, 'sg_b': 1.990341e+00, 'w_a_out': 1.571729e+00, 'cv_w': 6.293175e-01, 'cv_b': 6.910417e+00, 'cv_ln_g': 5.279533e+00, 'cv_ln_b': 4.879399e+00, 'w_b_out': 1.203267e+00, 'q_norm_g': 4.445305e+00, 'k_norm_g': 4.474940e+00, 'w_c_out': 6.204104e-01, 'w_out': 1.945898e+00, 'ln2_g': 1.325089e+01, 'w_up': 4.761133e-01, 'ffn_conv_w': 1.939702e+00, 'ffn_conv_b': 2.046025e+00, 'w_down': 5.065996e-01}


def _to_microbatches(a, axis):
    t = _jnp.moveaxis(a, axis, 0)
    t = t.reshape((N_MICROBATCH, t.shape[0] // N_MICROBATCH) + t.shape[1:])
    return _jnp.moveaxis(t, 1, axis + 1)


def setup_inputs(seed: int = 0) -> dict:
    inp = _fwd_setup_inputs(seed)
    key = _jax.random.fold_in(_jax.random.key(seed), 7919)
    shape, _ = _output_shape()
    out = dict(inp)
    out["loss_target"] = _jax.random.normal(_jax.random.fold_in(key, 0), shape, _jnp.float32)
    for i, name in enumerate(TWIN_WEIGHTS):
        w = inp[name].astype(_jnp.float32)
        if MOMENT_SCALE is None:
            s = _jnp.sqrt(_jnp.mean(_jnp.square(w)) + 1e-30)
        else:
            s = MOMENT_SCALE[name]
        km, kv = _jax.random.split(_jax.random.fold_in(key, i + 1))
        out[name] = w
        out["m_" + name] = s * _jax.random.normal(km, w.shape, _jnp.float32)
        out["v_" + name] = (s * s) * _jax.random.uniform(kv, w.shape, _jnp.float32, 0.5, 1.5)
    if N_MICROBATCH > 1:
        for name, axis in PER_EXAMPLE_BATCH_AXIS.items():
            out[name] = _to_microbatches(out[name], axis)
    return {'x': out['x'], 'ln1_g': out['ln1_g'], 'w_in': out['w_in'], 'b_gate': out['b_gate'], 'sg_ln_g': out['sg_ln_g'], 'sg_ln_b': out['sg_ln_b'], 'sg_w': out['sg_w'], 'sg_b': out['sg_b'], 'w_a_out': out['w_a_out'], 'cv_w': out['cv_w'], 'cv_b': out['cv_b'], 'cv_ln_g': out['cv_ln_g'], 'cv_ln_b': out['cv_ln_b'], 'w_b_out': out['w_b_out'], 'q_norm_g': out['q_norm_g'], 'k_norm_g': out['k_norm_g'], 'w_c_out': out['w_c_out'], 'w_out': out['w_out'], 'ln2_g': out['ln2_g'], 'w_up': out['w_up'], 'ffn_conv_w': out['ffn_conv_w'], 'ffn_conv_b': out['ffn_conv_b'], 'w_down': out['w_down'], 'loss_target': out['loss_target'], 'm_ln1_g': out['m_ln1_g'], 'm_w_in': out['m_w_in'], 'm_b_gate': out['m_b_gate'], 'm_sg_ln_g': out['m_sg_ln_g'], 'm_sg_ln_b': out['m_sg_ln_b'], 'm_sg_w': out['m_sg_w'], 'm_sg_b': out['m_sg_b'], 'm_w_a_out': out['m_w_a_out'], 'm_cv_w': out['m_cv_w'], 'm_cv_b': out['m_cv_b'], 'm_cv_ln_g': out['m_cv_ln_g'], 'm_cv_ln_b': out['m_cv_ln_b'], 'm_w_b_out': out['m_w_b_out'], 'm_q_norm_g': out['m_q_norm_g'], 'm_k_norm_g': out['m_k_norm_g'], 'm_w_c_out': out['m_w_c_out'], 'm_w_out': out['m_w_out'], 'm_ln2_g': out['m_ln2_g'], 'm_w_up': out['m_w_up'], 'm_ffn_conv_w': out['m_ffn_conv_w'], 'm_ffn_conv_b': out['m_ffn_conv_b'], 'm_w_down': out['m_w_down'], 'v_ln1_g': out['v_ln1_g'], 'v_w_in': out['v_w_in'], 'v_b_gate': out['v_b_gate'], 'v_sg_ln_g': out['v_sg_ln_g'], 'v_sg_ln_b': out['v_sg_ln_b'], 'v_sg_w': out['v_sg_w'], 'v_sg_b': out['v_sg_b'], 'v_w_a_out': out['v_w_a_out'], 'v_cv_w': out['v_cv_w'], 'v_cv_b': out['v_cv_b'], 'v_cv_ln_g': out['v_cv_ln_g'], 'v_cv_ln_b': out['v_cv_ln_b'], 'v_w_b_out': out['v_w_b_out'], 'v_q_norm_g': out['v_q_norm_g'], 'v_k_norm_g': out['v_k_norm_g'], 'v_w_c_out': out['v_w_c_out'], 'v_w_out': out['v_w_out'], 'v_ln2_g': out['v_ln2_g'], 'v_w_up': out['v_w_up'], 'v_ffn_conv_w': out['v_ffn_conv_w'], 'v_ffn_conv_b': out['v_ffn_conv_b'], 'v_w_down': out['v_w_down']}


def _loss(weights, diff, rest, loss_target):
    with _jax.named_scope("forward"):
        args = {**rest, TWIN_DIFF_INPUT: diff, **{k: w.astype(_WEIGHT_DTYPES[k]) for k, w in weights.items()}}
        y = _forward(args)
    with _jax.named_scope("loss_head"):
        err = _jnp.square(y.astype(_jnp.float32) - loss_target)
        return 0.5 * _jnp.sum(_jnp.mean(err, axis=-1)) if err.ndim else 0.5 * err


def _adamw(w, g, m, v):
    m = ADAM_B1 * m + (1.0 - ADAM_B1) * g
    v = ADAM_B2 * v + (1.0 - ADAM_B2) * _jnp.square(g)
    m_hat = m / (1.0 - ADAM_B1 ** ADAM_STEP)
    v_hat = v / (1.0 - ADAM_B2 ** ADAM_STEP)
    delta = -ADAM_LR * (m_hat / (_jnp.sqrt(v_hat) + ADAM_EPS) + ADAM_WD * w)
    return delta, m, v


def reference(x, ln1_g, w_in, b_gate, sg_ln_g, sg_ln_b, sg_w, sg_b, w_a_out, cv_w, cv_b, cv_ln_g, cv_ln_b, w_b_out, q_norm_g, k_norm_g, w_c_out, w_out, ln2_g, w_up, ffn_conv_w, ffn_conv_b, w_down, loss_target, m_ln1_g, m_w_in, m_b_gate, m_sg_ln_g, m_sg_ln_b, m_sg_w, m_sg_b, m_w_a_out, m_cv_w, m_cv_b, m_cv_ln_g, m_cv_ln_b, m_w_b_out, m_q_norm_g, m_k_norm_g, m_w_c_out, m_w_out, m_ln2_g, m_w_up, m_ffn_conv_w, m_ffn_conv_b, m_w_down, v_ln1_g, v_w_in, v_b_gate, v_sg_ln_g, v_sg_ln_b, v_sg_w, v_sg_b, v_w_a_out, v_cv_w, v_cv_b, v_cv_ln_g, v_cv_ln_b, v_w_b_out, v_q_norm_g, v_k_norm_g, v_w_c_out, v_w_out, v_ln2_g, v_w_up, v_ffn_conv_w, v_ffn_conv_b, v_w_down):
    given = dict(x=x, ln1_g=ln1_g, w_in=w_in, b_gate=b_gate, sg_ln_g=sg_ln_g, sg_ln_b=sg_ln_b, sg_w=sg_w, sg_b=sg_b, w_a_out=w_a_out, cv_w=cv_w, cv_b=cv_b, cv_ln_g=cv_ln_g, cv_ln_b=cv_ln_b, w_b_out=w_b_out, q_norm_g=q_norm_g, k_norm_g=k_norm_g, w_c_out=w_c_out, w_out=w_out, ln2_g=ln2_g, w_up=w_up, ffn_conv_w=ffn_conv_w, ffn_conv_b=ffn_conv_b, w_down=w_down, loss_target=loss_target, m_ln1_g=m_ln1_g, m_w_in=m_w_in, m_b_gate=m_b_gate, m_sg_ln_g=m_sg_ln_g, m_sg_ln_b=m_sg_ln_b, m_sg_w=m_sg_w, m_sg_b=m_sg_b, m_w_a_out=m_w_a_out, m_cv_w=m_cv_w, m_cv_b=m_cv_b, m_cv_ln_g=m_cv_ln_g, m_cv_ln_b=m_cv_ln_b, m_w_b_out=m_w_b_out, m_q_norm_g=m_q_norm_g, m_k_norm_g=m_k_norm_g, m_w_c_out=m_w_c_out, m_w_out=m_w_out, m_ln2_g=m_ln2_g, m_w_up=m_w_up, m_ffn_conv_w=m_ffn_conv_w, m_ffn_conv_b=m_ffn_conv_b, m_w_down=m_w_down, v_ln1_g=v_ln1_g, v_w_in=v_w_in, v_b_gate=v_b_gate, v_sg_ln_g=v_sg_ln_g, v_sg_ln_b=v_sg_ln_b, v_sg_w=v_sg_w, v_sg_b=v_sg_b, v_w_a_out=v_w_a_out, v_cv_w=v_cv_w, v_cv_b=v_cv_b, v_cv_ln_g=v_cv_ln_g, v_cv_ln_b=v_cv_ln_b, v_w_b_out=v_w_b_out, v_q_norm_g=v_q_norm_g, v_k_norm_g=v_k_norm_g, v_w_c_out=v_w_c_out, v_w_out=v_w_out, v_ln2_g=v_ln2_g, v_w_up=v_w_up, v_ffn_conv_w=v_ffn_conv_w, v_ffn_conv_b=v_ffn_conv_b, v_w_down=v_w_down)
    weights = {n: given[n] for n in TWIN_WEIGHTS}
    shared = {n: given[n] for n in SHARED_INPUTS}
    per_example = {n: given[n] for n in ['x']}
    grad_fn = _jax.value_and_grad(_loss, argnums=(0, 1))

    def one_microbatch(ex, loss_target):
        ex = dict(ex)
        diff = ex.pop(TWIN_DIFF_INPUT)
        return grad_fn(weights, diff, {**shared, **ex}, loss_target)

    if N_MICROBATCH == 1:
        loss, (grad_w, grad_x) = one_microbatch(per_example, given["loss_target"])
    else:
        def body(carry, xs):
            loss_sum, grad_sum = carry
            l_k, (gw_k, gx_k) = one_microbatch(xs[0], xs[1])
            with _jax.named_scope("update"):
                return (loss_sum + l_k, _jax.tree.map(_jnp.add, grad_sum, gw_k)), gx_k

        init = (_jnp.zeros((), _jnp.float32), _jax.tree.map(_jnp.zeros_like, weights))
        (loss, grad_w), grad_x = _jax.lax.scan(body, init, (per_example, given["loss_target"]))
    with _jax.named_scope("update"):
        delta_w, new_m, new_v = {}, {}, {}
        for n in TWIN_WEIGHTS:
            delta_w[n], new_m[n], new_v[n] = _adamw(weights[n], grad_w[n], given["m_" + n], given["v_" + n])
    return (loss, grad_x, *[grad_w[n] for n in TWIN_WEIGHTS], *[delta_w[n] for n in TWIN_WEIGHTS],
            *[new_m[n] for n in TWIN_WEIGHTS], *[new_v[n] for n in TWIN_WEIGHTS])
```

```python
import functools

import jax
import jax.numpy as jnp
from jax import lax
from jax.experimental import pallas as pl
from jax.experimental.pallas import tpu as pltpu

f32 = jnp.float32
bf16 = jnp.bfloat16

EPS = 1e-6
N_DEV = 8
DEPTH = 4
CHUNK = 128
HEAD_DIM = 64
N_HEADS = 8
CV_KERNEL = 31
VMEM_LIMIT_BYTES = 56 * 2 ** 20

ADAM_LR = 0.001
ADAM_B1 = 0.9
ADAM_B2 = 0.999
ADAM_EPS = 1e-08
ADAM_WD = 0.01
ADAM_STEP = 10

MESH = pl.DeviceIdType.MESH


def _cparams(n_grid):
    return pltpu.CompilerParams(dimension_semantics=("arbitrary",) * n_grid, vmem_limit_bytes=VMEM_LIMIT_BYTES)


def exchange(name, arrays, modes):
    n = len(arrays)

    def body(*refs):
        ins, outs = refs[:n], refs[n:2 * n]
        send_sems, recv_sems, local_sems = refs[2 * n:]
        x, y, c = lax.axis_index("x"), lax.axis_index("y"), lax.axis_index("c")
        me = 4 * x + 2 * y + c
        copies = []
        for k in range(n):
            src_mine = ins[k] if modes[k] == "gather" else ins[k].at[me]
            local = pltpu.make_async_copy(src_mine, outs[k].at[me], local_sems.at[k])
            local.start()
            copies.append(local)
            for r in range(1, N_DEV):
                px = 1 - x if r & 4 else x
                py = 1 - y if r & 2 else y
                pc = 1 - c if r & 1 else c
                peer = 4 * px + 2 * py + pc
                src = ins[k] if modes[k] == "gather" else ins[k].at[peer]
                cp = pltpu.make_async_remote_copy(
                    src_ref=src, dst_ref=outs[k].at[me], send_sem=send_sems.at[k, r - 1], recv_sem=recv_sems.at[k, r - 1],
                    device_id=(px, py, pc), device_id_type=MESH)
                cp.start()
                copies.append(cp)
        for cp in copies:
            cp.wait()

    out_shape = []
    for a, m in zip(arrays, modes):
        shp = (N_DEV,) + tuple(a.shape) if m == "gather" else tuple(a.shape)
        out_shape.append(jax.ShapeDtypeStruct(shp, a.dtype))
    any_spec = pl.BlockSpec(memory_space=pl.ANY)
    return pl.pallas_call(
        body, name=name, out_shape=out_shape,
        in_specs=[any_spec] * n, out_specs=[any_spec] * n,
        scratch_shapes=[pltpu.SemaphoreType.DMA((n, N_DEV - 1)), pltpu.SemaphoreType.DMA((n, N_DEV - 1)),
                        pltpu.SemaphoreType.DMA((n,))],
    )(*arrays)


def assemble(name, slabs):
    _, K, Ns = slabs.shape
    g = N_DEV if Ns % 128 == 0 else 2

    def body(w_ref, o_ref):
        o_ref[...] = jnp.concatenate([w_ref[s] for s in range(g)], axis=1)

    return pl.pallas_call(
        body, name=name, grid=(N_DEV // g,),
        in_specs=[pl.BlockSpec((g, K, Ns), lambda m: (m, 0, 0))],
        out_specs=pl.BlockSpec((K, g * Ns), lambda m: (0, m)),
        out_shape=jax.ShapeDtypeStruct((K, N_DEV * Ns), slabs.dtype),
        compiler_params=_cparams(1),
    )(slabs)


def mm_nn(name, a, b, out_dtype, add=None, tm=1024, tn=512):
    M, K = a.shape
    N = b.shape[1]
    tm, tn = min(tm, M), min(tn, N)
    has_add = add is not None

    def body(a_ref, b_ref, *rest):
        o_ref = rest[-1]
        acc = jnp.dot(a_ref[...].astype(bf16), b_ref[...].astype(bf16), preferred_element_type=f32)
        if has_add:
            acc = acc + rest[0][...]
        o_ref[...] = acc.astype(o_ref.dtype)

    in_specs = [pl.BlockSpec((tm, K), lambda i, j: (i, 0)), pl.BlockSpec((K, tn), lambda i, j: (0, j))]
    ops = [a, b]
    if has_add:
        in_specs.append(pl.BlockSpec((tm, tn), lambda i, j: (i, j)))
        ops.append(add)
    return pl.pallas_call(
        body, name=name, grid=(M // tm, N // tn), in_specs=in_specs,
        out_specs=pl.BlockSpec((tm, tn), lambda i, j: (i, j)),
        out_shape=jax.ShapeDtypeStruct((M, N), out_dtype), compiler_params=_cparams(2),
    )(*ops)


def mm_nt(name, a, b, out_dtype, tm=1024, tko=None, tc=512):
    M, C = a.shape
    Ko = b.shape[0]
    tm, tc = min(tm, M), min(tc, C)
    tko = Ko if tko is None else tko
    nc = C // tc

    def body(a_ref, b_ref, o_ref, acc_ref):
        c = pl.program_id(2)
        part = lax.dot_general(a_ref[...].astype(bf16), b_ref[...].astype(bf16), (((1,), (1,)), ((), ())),
                               preferred_element_type=f32)

        @pl.when(c == 0)
        def _():
            acc_ref[...] = part

        @pl.when(c > 0)
        def _():
            acc_ref[...] += part

        @pl.when(c == nc - 1)
        def _():
            o_ref[...] = acc_ref[...].astype(o_ref.dtype)

    return pl.pallas_call(
        body, name=name, grid=(M // tm, Ko // tko, nc),
        in_specs=[pl.BlockSpec((tm, tc), lambda i, j, c: (i, c)), pl.BlockSpec((tko, tc), lambda i, j, c: (j, c))],
        out_specs=pl.BlockSpec((tm, tko), lambda i, j, c: (i, j)),
        out_shape=jax.ShapeDtypeStruct((M, Ko), out_dtype),
        scratch_shapes=[pltpu.VMEM((tm, tko), f32)], compiler_params=_cparams(3),
    )(a, b)


def mm_tn(name, a, b, n_slab=None, tk=512, tn=512):
    T, K = a.shape
    N = b.shape[1]
    tk = min(tk, K)
    if n_slab is None:
        tn = min(tn, N)

        def body(a_ref, b_ref, o_ref):
            o_ref[...] = lax.dot_general(a_ref[...].astype(bf16), b_ref[...].astype(bf16), (((0,), (0,)), ((), ())),
                                         preferred_element_type=f32).astype(o_ref.dtype)

        return pl.pallas_call(
            body, name=name, grid=(K // tk, N // tn),
            in_specs=[pl.BlockSpec((T, tk), lambda i, j: (0, i)), pl.BlockSpec((T, tn), lambda i, j: (0, j))],
            out_specs=pl.BlockSpec((tk, tn), lambda i, j: (i, j)),
            out_shape=jax.ShapeDtypeStruct((K, N), bf16), compiler_params=_cparams(2),
        )(a, b)

    Ns = n_slab
    g = N_DEV if Ns % 128 == 0 else 2
    tn = g * Ns

    def body(a_ref, b_ref, o_ref):
        val = lax.dot_general(a_ref[...].astype(bf16), b_ref[...].astype(bf16), (((0,), (0,)), ((), ())),
                              preferred_element_type=f32)
        for s in range(g):
            o_ref[s] = val[:, s * Ns:(s + 1) * Ns].astype(o_ref.dtype)

    return pl.pallas_call(
        body, name=name, grid=(K // tk, N_DEV // g),
        in_specs=[pl.BlockSpec((T, tk), lambda i, j: (0, i)), pl.BlockSpec((T, tn), lambda i, j: (0, j))],
        out_specs=pl.BlockSpec((g, tk, Ns), lambda i, j: (j, i, 0)),
        out_shape=jax.ShapeDtypeStruct((N_DEV, K, Ns), bf16), compiler_params=_cparams(2),
    )(a, b)


def _tile_spec(w, tm, cb):
    if callable(cb):
        return pl.BlockSpec((tm, w), lambda c, i: (i, cb(c)))
    return pl.BlockSpec((tm, w), lambda c, i: (i, cb))


def _param_spec(block, cb):
    nd = len(block)
    if cb is None:
        return pl.BlockSpec(block, lambda c, i: (0,) * nd)
    return pl.BlockSpec(block, lambda c, i: (0,) * (nd - 1) + (cb(c),))


def rowwise(name, fn, tiled, params, outs, T, tm, ncol=1):
    n_in = len(tiled) + len(params)
    n_t = len(tiled)

    def body(*refs):
        ins = [r[...].astype(f32) for r in refs[:n_t]] + [r[...] for r in refs[n_t:n_in]]
        res = fn(*ins)
        for r, o in zip(refs[n_in:], res):
            r[...] = o.astype(r.dtype)

    return pl.pallas_call(
        body, name=name, grid=(ncol, T // tm),
        in_specs=[_tile_spec(w, tm, cb) for _, w, cb in tiled] + [_param_spec(blk, cb) for _, blk, cb in params],
        out_specs=[_tile_spec(w, tm, cb) for _, w, cb, _ in outs],
        out_shape=[jax.ShapeDtypeStruct((T, cols), dt) for cols, _, _, dt in outs],
        compiler_params=_cparams(2),
    )(*[a for a, _, _ in tiled], *[a for a, _, _ in params])


def rowwise_bwd(name, fn, tiled, params, cts, grads, T, tm, ncol=1, adds=None):
    n_t, n_p, n_c = len(tiled), len(params), len(cts)
    adds = adds or [None] * n_t
    add_list = [(k, a) for k, a in enumerate(adds) if a is not None]
    want = [k for k, g in enumerate(grads) if g is not None]
    n_a = len(add_list)

    def body(*refs):
        pos = 0
        t_refs = refs[pos:pos + n_t]; pos += n_t
        p_refs = refs[pos:pos + n_p]; pos += n_p
        c_refs = refs[pos:pos + n_c]; pos += n_c
        a_refs = refs[pos:pos + n_a]; pos += n_a
        g_refs = refs[pos:pos + len(want)]; pos += len(want)
        pg_refs = refs[pos:pos + n_p]
        primals = [r[...].astype(f32) for r in t_refs] + [r[...] for r in p_refs]
        _, vjp = jax.vjp(fn, *primals)
        g = vjp(tuple(r[...].astype(f32) for r in c_refs))
        add_of = {k: a_refs[n][...] for n, (k, _) in enumerate(add_list)}
        for n, k in enumerate(want):
            val = g[k]
            if k in add_of:
                val = val + add_of[k]
            g_refs[n][...] = val.astype(g_refs[n].dtype)
        i = pl.program_id(1)
        for k in range(n_p):
            @pl.when(i == 0)
            def _(k=k):
                pg_refs[k][...] = g[n_t + k]

            @pl.when(i > 0)
            def _(k=k):
                pg_refs[k][...] += g[n_t + k]

    in_specs = ([_tile_spec(w, tm, cb) for _, w, cb in tiled] + [_param_spec(blk, cb) for _, blk, cb in params]
                + [_tile_spec(w, tm, cb) for _, w, cb in cts] + [_tile_spec(w, tm, cb) for _, (_, w, cb) in add_list])
    ops = ([a for a, _, _ in tiled] + [a for a, _, _ in params] + [a for a, _, _ in cts]
           + [a for _, (a, _, _) in add_list])
    out_specs, out_shape = [], []
    for k in want:
        w = tiled[k][1]
        out_specs.append(_tile_spec(w, tm, lambda c: c))
        out_shape.append(jax.ShapeDtypeStruct((T, ncol * w), grads[k]))
    for a, blk, cb in params:
        out_specs.append(_param_spec(blk, cb))
        out_shape.append(jax.ShapeDtypeStruct(a.shape, f32))
    return pl.pallas_call(
        body, name=name, grid=(ncol, T // tm), in_specs=in_specs, out_specs=out_specs, out_shape=out_shape,
        compiler_params=_cparams(2),
    )(*ops)


@jax.custom_vjp
def _bdot(a, b):
    return jnp.dot(a.astype(bf16), b.astype(bf16), preferred_element_type=f32)


def _bdot_fwd(a, b):
    return _bdot(a, b), (a, b)


def _bdot_bwd(res, ct):
    a, b = res
    ctb = ct.astype(bf16)
    da = lax.dot_general(ctb, b.astype(bf16), (((1,), (1,)), ((), ())), preferred_element_type=f32)
    db = lax.dot_general(a.astype(bf16), ctb, (((0,), (0,)), ((), ())), preferred_element_type=f32)
    return da, db


_bdot.defvjp(_bdot_fwd, _bdot_bwd)


def _layer_norm(x, g, b):
    mu = jnp.mean(x, axis=-1, keepdims=True)
    xc = x - mu
    y = xc * lax.rsqrt(jnp.mean(xc * xc, axis=-1, keepdims=True) + EPS)
    return y * g + b


def f_rms(x, g):
    y = x * lax.rsqrt(jnp.mean(x * x, axis=-1, keepdims=True) + EPS)
    return (y * g,)


def f_sgu(zu, zv, ln_g, ln_b, wm, sgb_t):
    u = jax.nn.gelu(zu)
    vn = _layer_norm(jax.nn.gelu(zv), ln_g, ln_b)
    row = lax.broadcasted_iota(jnp.int32, (CHUNK, CHUNK), 0)
    col = lax.broadcasted_iota(jnp.int32, (CHUNK, CHUNK), 1)
    tril = col <= row
    low = col < HEAD_DIM
    parts = []
    for p in range(4):
        vp = vn[:, CHUNK * p:CHUNK * (p + 1)]
        w0 = jnp.where(tril, wm[2 * p], 0.0)
        w1 = jnp.where(tril, wm[2 * p + 1], 0.0)
        parts.append(_bdot(w0, jnp.where(low, vp, 0.0)) + _bdot(w1, jnp.where(low, 0.0, vp)))
    s = jnp.concatenate(parts, axis=1)
    lane_g = lax.shift_right_logical(lax.broadcasted_iota(jnp.int32, s.shape, 1), 6)
    bias = jnp.zeros_like(s)
    for g in range(8):
        bias = jnp.where(lane_g == g, sgb_t[:, g:g + 1], bias)
    return (u * (s + bias),)


def f_glu(p, gl):
    return (p * jax.nn.sigmoid(gl),)


def f_lnsilu(c1, g, b):
    return (jax.nn.silu(_layer_norm(c1, g, b)),)


def _head_norm(x, g64):
    g = jnp.concatenate([g64] * N_HEADS, axis=1)
    lane_h = lax.shift_right_logical(lax.broadcasted_iota(jnp.int32, x.shape, 1), 6)
    x2 = x * x
    r = jnp.zeros_like(x)
    for h in range(N_HEADS):
        mh = lane_h == h
        ms = jnp.sum(jnp.where(mh, x2, 0.0), axis=-1, keepdims=True) * (1.0 / HEAD_DIM)
        r = jnp.where(mh, lax.rsqrt(ms + EPS), r)
    return (x * r) * g


def f_qkv(zq, zk, zv, qg, kg):
    return (_head_norm(zq, qg) * 0.125, _head_norm(zk, kg), zv)


def f_merge(g0, g1, g2, ya, yb, yc, b0, b1, b2):
    return (jax.nn.sigmoid(g0 + b0) * ya + jax.nn.sigmoid(g1 + b1) * yb + jax.nn.sigmoid(g2 + b2) * yc,)


HALO = 32


def conv31_fwd(x, w, b, T, tt=256):
    C = x.shape[1]
    r = tt // HALO

    def body(x_ref, h_ref, w_ref, b_ref, y_ref, buf):
        i = pl.program_id(0)
        halo = h_ref[...]
        buf[0:HALO, :] = jnp.where(i > 0, halo, jnp.zeros_like(halo))
        buf[HALO:HALO + tt, :] = x_ref[...]
        acc = jnp.zeros((tt, C), f32) + b_ref[...]
        for k in range(CV_KERNEL):
            acc = acc + w_ref[k:k + 1, :] * buf[pl.ds(HALO - (CV_KERNEL - 1) + k, tt), :]
        y_ref[...] = acc

    return pl.pallas_call(
        body, name="conv31_fwd", grid=(T // tt,),
        in_specs=[pl.BlockSpec((tt, C), lambda i: (i, 0)),
                  pl.BlockSpec((HALO, C), lambda i: (jnp.maximum(i * r - 1, 0), 0)),
                  pl.BlockSpec((CV_KERNEL, C), lambda i: (0, 0)), pl.BlockSpec((1, C), lambda i: (0, 0))],
        out_specs=pl.BlockSpec((tt, C), lambda i: (i, 0)),
        out_shape=jax.ShapeDtypeStruct((T, C), f32),
        scratch_shapes=[pltpu.VMEM((HALO + tt, C), f32)], compiler_params=_cparams(1),
    )(x, x, w, b)


def conv31_bwd(x, w, dy, T, tt=256):
    C = x.shape[1]
    r = tt // HALO
    n = T // tt

    def body(x_ref, h_ref, w_ref, dy_ref, dyn_ref, dx_ref, dw_ref, db_ref, xbuf, dbuf):
        i = pl.program_id(0)
        halo = h_ref[...]
        xbuf[0:HALO, :] = jnp.where(i > 0, halo, jnp.zeros_like(halo))
        xbuf[HALO:HALO + tt, :] = x_ref[...]
        nxt = dyn_ref[...]
        dy = dy_ref[...]
        dbuf[0:tt, :] = dy
        dbuf[tt:tt + HALO, :] = jnp.where(i < n - 1, nxt, jnp.zeros_like(nxt))

        @pl.when(i == 0)
        def _():
            dw_ref[...] = jnp.zeros_like(dw_ref)
            db_ref[...] = jnp.zeros_like(db_ref)

        acc = jnp.zeros((tt, C), f32)
        for k in range(CV_KERNEL):
            acc = acc + w_ref[k:k + 1, :] * dbuf[pl.ds(CV_KERNEL - 1 - k, tt), :]
            xs = xbuf[pl.ds(HALO - (CV_KERNEL - 1) + k, tt), :]
            dw_ref[k:k + 1, :] += jnp.sum(dy * xs, axis=0, keepdims=True)
        dx_ref[...] = acc
        db_ref[...] += jnp.sum(dy, axis=0, keepdims=True)

    return pl.pallas_call(
        body, name="conv31_bwd", grid=(n,),
        in_specs=[pl.BlockSpec((tt, C), lambda i: (i, 0)),
                  pl.BlockSpec((HALO, C), lambda i: (jnp.maximum(i * r - 1, 0), 0)),
                  pl.BlockSpec((CV_KERNEL, C), lambda i: (0, 0)),
                  pl.BlockSpec((tt, C), lambda i: (i, 0)),
                  pl.BlockSpec((HALO, C), lambda i: (jnp.minimum((i + 1) * r, n * r - 1), 0))],
        out_specs=[pl.BlockSpec((tt, C), lambda i: (i, 0)), pl.BlockSpec((CV_KERNEL, C), lambda i: (0, 0)),
                   pl.BlockSpec((1, C), lambda i: (0, 0))],
        out_shape=[jax.ShapeDtypeStruct((T, C), f32), jax.ShapeDtypeStruct((CV_KERNEL, C), f32),
                   jax.ShapeDtypeStruct((1, C), f32)],
        scratch_shapes=[pltpu.VMEM((HALO + tt, C), f32), pltpu.VMEM((HALO + tt, C), f32)], compiler_params=_cparams(1),
    )(x, x, w, dy, dy)


FFN_TC = 128


def _shift_down(x, s, row):
    return jnp.where(row >= s, pltpu.roll(x, s, 0), 0.0)


def _shift_up(x, s, row, T):
    return jnp.where(row < T - s, pltpu.roll(x, T - s, 0), 0.0)


def _conv3(x, w0, w1, w2, b, row):
    return w0 * _shift_down(x, 2, row) + w1 * _shift_down(x, 1, row) + w2 * x + b


def ffn_act_fwd(up, cw, cb, T):
    F = up.shape[1] // 2
    nj = F // FFN_TC

    def body(g_ref, v_ref, g0, g1, g2, gb, v0, v1, v2, vb, o_ref):
        row = lax.broadcasted_iota(jnp.int32, (T, FFN_TC), 0)
        gc = _conv3(g_ref[...], g0[...], g1[...], g2[...], gb[...], row)
        vc = _conv3(v_ref[...], v0[...], v1[...], v2[...], vb[...], row)
        o_ref[...] = (jax.nn.silu(gc) * vc).astype(o_ref.dtype)

    gspec = pl.BlockSpec((T, FFN_TC), lambda j: (0, j))
    vspec = pl.BlockSpec((T, FFN_TC), lambda j: (0, j + nj))
    pg = pl.BlockSpec((1, FFN_TC), lambda j: (0, j))
    pv = pl.BlockSpec((1, FFN_TC), lambda j: (0, j + nj))
    return pl.pallas_call(
        body, name="ffn_act_fwd", grid=(nj,),
        in_specs=[gspec, vspec, pg, pg, pg, pg, pv, pv, pv, pv], out_specs=gspec,
        out_shape=jax.ShapeDtypeStruct((T, F), bf16), compiler_params=_cparams(1),
    )(up, up, cw[0], cw[1], cw[2], cb, cw[0], cw[1], cw[2], cb)


def ffn_act_bwd(up, cw, cb, dact, T):
    F = up.shape[1] // 2
    nj = F // FFN_TC

    def body(g_ref, v_ref, g0, g1, g2, gb, v0, v1, v2, vb, d_ref, dup_ref, dw0_ref, dw1_ref, dw2_ref, db_ref):
        side = pl.program_id(0)
        row = lax.broadcasted_iota(jnp.int32, (T, FFN_TC), 0)
        xg, xv = g_ref[...], v_ref[...]
        gc = _conv3(xg, g0[...], g1[...], g2[...], gb[...], row)
        vc = _conv3(xv, v0[...], v1[...], v2[...], vb[...], row)
        d = d_ref[...]
        sg = jax.nn.sigmoid(gc)
        silu = gc * sg
        d_gc = d * vc * (sg * (1.0 + gc * (1.0 - sg)))
        d_vc = d * silu
        is_gate = side == 0
        dc = jnp.where(is_gate, d_gc, d_vc)
        x = jnp.where(is_gate, xg, xv)
        w0 = jnp.where(is_gate, g0[...], v0[...])
        w1 = jnp.where(is_gate, g1[...], v1[...])
        w2 = jnp.where(is_gate, g2[...], v2[...])
        dup = w2 * dc + w1 * _shift_up(dc, 1, row, T) + w0 * _shift_up(dc, 2, row, T)
        dup_ref[...] = dup.astype(dup_ref.dtype)
        dw0_ref[...] = jnp.sum(dc * _shift_down(x, 2, row), axis=0, keepdims=True)
        dw1_ref[...] = jnp.sum(dc * _shift_down(x, 1, row), axis=0, keepdims=True)
        dw2_ref[...] = jnp.sum(dc * x, axis=0, keepdims=True)
        db_ref[...] = jnp.sum(dc, axis=0, keepdims=True)

    gspec = pl.BlockSpec((T, FFN_TC), lambda s, j: (0, j))
    vspec = pl.BlockSpec((T, FFN_TC), lambda s, j: (0, j + nj))
    pg = pl.BlockSpec((1, FFN_TC), lambda s, j: (0, j))
    pv = pl.BlockSpec((1, FFN_TC), lambda s, j: (0, j + nj))
    ospec = pl.BlockSpec((T, FFN_TC), lambda s, j: (0, j + s * nj))
    pospec = pl.BlockSpec((1, FFN_TC), lambda s, j: (0, j + s * nj))
    return pl.pallas_call(
        body, name="ffn_act_bwd", grid=(2, nj),
        in_specs=[gspec, vspec, pg, pg, pg, pg, pv, pv, pv, pv, gspec],
        out_specs=[ospec, pospec, pospec, pospec, pospec],
        out_shape=[jax.ShapeDtypeStruct((T, 2 * F), bf16)] + [jax.ShapeDtypeStruct((1, 2 * F), f32)] * 4,
        compiler_params=_cparams(2),
    )(up, up, cw[0], cw[1], cw[2], cb, cw[0], cw[1], cw[2], cb, dact)


def _split_dot(x, u):
    x1 = x.astype(bf16)
    r1 = x - x1.astype(f32)
    x2 = r1.astype(bf16)
    x3 = (r1 - x2.astype(f32)).astype(bf16)
    y = jnp.dot(jnp.concatenate([x1, x2, x3], axis=0), u, preferred_element_type=f32)
    n = x.shape[0]
    return y[0:n] + y[n:2 * n] + y[2 * n:3 * n]


def _log_sigmoids(z):
    sp = jnp.log1p(jnp.exp(-jnp.abs(z)))
    return jnp.minimum(z, 0.0) - sp, -(jnp.maximum(z, 0.0) + sp)


def attn_fwd(q, k, v, T):
    B = CHUNK
    nq = T // B
    nt = (((1,), (1,)), ((), ()))

    def body(q_ref, k_ref, v_ref, o_ref, tot_ref):
        i = pl.program_id(1)
        row = lax.broadcasted_iota(jnp.int32, (B, B), 0)
        col = lax.broadcasted_iota(jnp.int32, (B, B), 1)
        head1 = col >= HEAD_DIM
        qv = q_ref[...]
        zero = jnp.zeros_like(qv)
        qm = (jnp.where(head1, zero, qv), jnp.where(head1, qv, zero))
        u_after = (row > col).astype(bf16)

        def step(t, carry):
            j = i - t
            ks = pl.multiple_of(j * B, B)
            kb = k_ref[pl.ds(ks, B), :]
            vb = v_ref[pl.ds(ks, B), :]
            m = (col + j * B) < (row + i * B)
            out = []
            for hh in range(2):
                acc, c = carry[2 * hh], carry[2 * hh + 1]
                z = lax.dot_general(qm[hh], kb, nt, preferred_element_type=f32)
                lsp, lsn = _log_sigmoids(z)
                lmask = jnp.where(m, lsn, 0.0)
                a = jnp.where(m, jnp.exp(lsp + _split_dot(lmask, u_after) + c), 0.0)
                acc = acc + jnp.dot(a.astype(bf16), vb, preferred_element_type=f32)
                c = c + jnp.sum(lmask, axis=-1, keepdims=True)
                out += [acc, c]
            return tuple(out)

        init = (jnp.zeros((B, B), f32), jnp.zeros((B, 1), f32), jnp.zeros((B, B), f32), jnp.zeros((B, 1), f32))
        acc0, c0, acc1, c1 = lax.fori_loop(0, i + 1, step, init)
        o_ref[...] = jnp.where(head1, acc1, acc0).astype(o_ref.dtype)
        tot_ref[...] = jnp.where(head1, c1, c0)

    blk = pl.BlockSpec((B, B), lambda p, i: (i, p))
    full = pl.BlockSpec((T, B), lambda p, i: (0, p))
    return pl.pallas_call(
        body, name="attn_fwd", grid=(4, nq), in_specs=[blk, full, full], out_specs=[blk, blk],
        out_shape=[jax.ShapeDtypeStruct((T, 512), bf16), jax.ShapeDtypeStruct((T, 512), f32)],
        compiler_params=_cparams(2),
    )(q, k, v)


def attn_bwd(q, k, v, do, tot, T):
    B = CHUNK
    nq = T // B
    nt = (((1,), (1,)), ((), ()))
    tn = (((0,), (0,)), ((), ()))

    def body(q_ref, k_ref, v_ref, do_ref, tot_ref, dq_ref, dk_ref, dv_ref):
        i = pl.program_id(1)

        @pl.when(i == 0)
        def _():
            dk_ref[...] = jnp.zeros_like(dk_ref)
            dv_ref[...] = jnp.zeros_like(dv_ref)

        row = lax.broadcasted_iota(jnp.int32, (B, B), 0)
        col = lax.broadcasted_iota(jnp.int32, (B, B), 1)
        head1 = col >= HEAD_DIM
        qv = q_ref[...]
        dov = do_ref[...].astype(bf16)
        zero = jnp.zeros_like(qv)
        qm = (jnp.where(head1, zero, qv), jnp.where(head1, qv, zero))
        dom = (jnp.where(head1, zero, dov), jnp.where(head1, dov, zero))
        totv = tot_ref[...]
        tots = (totv[:, 0:1], totv[:, HEAD_DIM:HEAD_DIM + 1])
        u_upto = (row <= col).astype(bf16)
        u_before = (row < col).astype(bf16)

        def step(j, carry):
            ks = pl.multiple_of(j * B, B)
            kb = k_ref[pl.ds(ks, B), :]
            vb = v_ref[pl.ds(ks, B), :]
            m = (col + j * B) < (row + i * B)
            out = []
            dk_add = jnp.zeros((B, B), f32)
            dv_add = jnp.zeros((B, B), f32)
            for hh in range(2):
                dq, cl, cg = carry[3 * hh], carry[3 * hh + 1], carry[3 * hh + 2]
                z = lax.dot_general(qm[hh], kb, nt, preferred_element_type=f32)
                lsp, lsn = _log_sigmoids(z)
                lmask = jnp.where(m, lsn, 0.0)
                cs = cl + _split_dot(lmask, u_upto)
                a = jnp.where(m, jnp.exp(lsp + (tots[hh] - cs)), 0.0)
                da = lax.dot_general(dom[hh], vb, nt, preferred_element_type=f32)
                g = a * da
                big_g = cg + _split_dot(g, u_before)
                dz = jnp.where(m, g * jnp.exp(lsn) - big_g * jnp.exp(lsp), 0.0).astype(bf16)
                dq = dq + jnp.dot(dz, kb, preferred_element_type=f32)
                dk_add = dk_add + lax.dot_general(dz, qm[hh], tn, preferred_element_type=f32)
                dv_add = dv_add + lax.dot_general(a.astype(bf16), dom[hh], tn, preferred_element_type=f32)
                cl = cl + jnp.sum(lmask, axis=-1, keepdims=True)
                cg = cg + jnp.sum(g, axis=-1, keepdims=True)
                out += [dq, cl, cg]
            dk_ref[pl.ds(ks, B), :] += dk_add
            dv_ref[pl.ds(ks, B), :] += dv_add
            return tuple(out)

        zc = jnp.zeros((B, 1), f32)
        init = (jnp.zeros((B, B), f32), zc, zc, jnp.zeros((B, B), f32), zc, zc)
        res = lax.fori_loop(0, i + 1, step, init)
        dq_ref[...] = jnp.where(head1, res[3], res[0])

    blk = pl.BlockSpec((B, B), lambda p, i: (i, p))
    full = pl.BlockSpec((T, B), lambda p, i: (0, p))
    shp = jax.ShapeDtypeStruct((T, 512), f32)
    return pl.pallas_call(
        body, name="attn_bwd", grid=(4, nq), in_specs=[blk, full, full, blk, blk], out_specs=[blk, full, full],
        out_shape=[shp, shp, shp], compiler_params=_cparams(2),
    )(q, k, v, do, tot)


def loss_head(y, target, T, tm=256):
    D = y.shape[1]

    def body(y_ref, t_ref, dy_ref, l_ref):
        i = pl.program_id(0)
        err = y_ref[...] - t_ref[...]
        dy_ref[...] = err * (1.0 / D)
        part = 0.5 * jnp.sum(jnp.sum(err * err, axis=-1, keepdims=True) * (1.0 / D), axis=0, keepdims=True)

        @pl.when(i == 0)
        def _():
            l_ref[...] = jnp.zeros_like(l_ref)

        l_ref[...] += jnp.broadcast_to(part, l_ref.shape)

    spec = pl.BlockSpec((tm, D), lambda i: (i, 0))
    return pl.pallas_call(
        body, name="loss_head", grid=(T // tm,), in_specs=[spec, spec],
        out_specs=[spec, pl.BlockSpec((1, 128), lambda i: (0, 0))],
        out_shape=[jax.ShapeDtypeStruct((T, D), f32), jax.ShapeDtypeStruct((1, 128), f32)],
        compiler_params=_cparams(1),
    )(y, target)


def _row_tile(rows, cols):
    best = None
    for t in range(16, rows + 1, 16):
        if rows % t == 0 and t * cols * 4 <= 512 * 1024:
            best = t
    return best if best is not None else rows


def adamw(name, w, m, v, parts):
    R, C = w.shape
    tr = _row_tile(R, C)

    def body(w_ref, m_ref, v_ref, p_ref, g_ref, d_ref, nm_ref, nv_ref):
        g = p_ref[0].astype(f32)
        for s in range(1, N_DEV):
            g = g + p_ref[s].astype(f32)
        wv = w_ref[...]
        mn = ADAM_B1 * m_ref[...] + (1.0 - ADAM_B1) * g
        vn = ADAM_B2 * v_ref[...] + (1.0 - ADAM_B2) * jnp.square(g)
        m_hat = mn / (1.0 - ADAM_B1 ** ADAM_STEP)
        v_hat = vn / (1.0 - ADAM_B2 ** ADAM_STEP)
        g_ref[...] = g
        d_ref[...] = -ADAM_LR * (m_hat / (jnp.sqrt(v_hat) + ADAM_EPS) + ADAM_WD * wv)
        nm_ref[...] = mn
        nv_ref[...] = vn

    spec = pl.BlockSpec((tr, C), lambda i: (i, 0))
    shp = jax.ShapeDtypeStruct((R, C), f32)
    return pl.pallas_call(
        body, name=name, grid=(R // tr,),
        in_specs=[spec, spec, spec, pl.BlockSpec((N_DEV, tr, C), lambda i: (0, i, 0))],
        out_specs=[spec] * 4, out_shape=[shp] * 4, compiler_params=_cparams(1),
    )(w, m, v, parts)


REPL = ["ln1_g", "sg_ln_g", "sg_ln_b", "sg_w", "sg_b", "cv_b", "cv_ln_g", "cv_ln_b", "q_norm_g", "k_norm_g", "ln2_g",
        "ffn_conv_b"]
SMALL_SHARDED = ["b_gate", "cv_w", "ffn_conv_w"]
BIG = ["w_in", "w_a_out", "w_b_out", "w_c_out", "w_up", "w_out", "w_down"]
COL_SHARDED = ["w_in", "w_a_out", "w_b_out", "w_c_out", "w_up"]


def _pack(arrs, rows):
    flat = jnp.concatenate([a.reshape(-1) for a in arrs])
    return jnp.pad(flat, (0, rows * 128 - flat.shape[0])).reshape(rows, 128)


def _unpack(packed, shapes):
    flat = packed.reshape(-1)
    out, pos = [], 0
    for s in shapes:
        n = 1
        for d in s:
            n *= d
        out.append(flat[pos:pos + n].reshape(s))
        pos += n
    return out


def _rows_for(shapes, mult):
    n = 0
    for s in shapes:
        k = 1
        for d in s:
            k *= d
        n += k
    rows = -(-n // 128)
    return -(-rows // mult) * mult


W_NAMES = ['ln1_g', 'w_in', 'b_gate', 'sg_ln_g', 'sg_ln_b', 'sg_w', 'sg_b', 'w_a_out', 'cv_w', 'cv_b', 'cv_ln_g', 'cv_ln_b',
           'w_b_out', 'q_norm_g', 'k_norm_g', 'w_c_out', 'w_out', 'ln2_g', 'w_up', 'ffn_conv_w', 'ffn_conv_b', 'w_down']


def _forward_layer(x, P, T):
    D = x.shape[1]
    sv = {"x0": x}
    (h1,) = rowwise("rms_fwd", f_rms, [(x, D, 0)], [(P["ln1_g"], (1, D), None)], [(D, D, 0, bf16)], T, 256)
    z = mm_nn("in_proj", h1, P["w_in"], f32)
    sv["h1"], sv["z"] = h1, z
    (ya_in,) = rowwise("sgu_fwd", f_sgu, [(z, 512, 0), (z, 512, 1)],
                       [(P["sg_ln_g"], (1, 512), None), (P["sg_ln_b"], (1, 512), None),
                        (P["sg_w"], (8, CHUNK, CHUNK), None), (P["sg_b_t"], (CHUNK, 8), None)],
                       [(512, 512, 0, bf16)], T, CHUNK)
    ya = mm_nn("a_out", ya_in, P["w_a_out"], f32)
    (c0,) = rowwise("glu_fwd", f_glu, [(z, 512, 2), (z, 512, 3)], [], [(512, 512, 0, f32)], T, 256)
    c1 = conv31_fwd(c0, P["cv_w"], P["cv_b"], T)
    (c3,) = rowwise("lnsilu_fwd", f_lnsilu, [(c1, 512, 0)], [(P["cv_ln_g"], (1, 512), None), (P["cv_ln_b"], (1, 512), None)],
                    [(512, 512, 0, bf16)], T, 256)
    yb = mm_nn("b_out", c3, P["w_b_out"], f32)
    q8, kn, vb = rowwise("qkv_fwd", f_qkv, [(z, 512, 4), (z, 512, 5), (z, 512, 6)],
                         [(P["q_norm_g"], (1, HEAD_DIM), None), (P["k_norm_g"], (1, HEAD_DIM), None)],
                         [(512, 512, 0, bf16)] * 3, T, 256)
    o, tot = attn_fwd(q8, kn, vb, T)
    yc = mm_nn("c_out", o, P["w_c_out"], f32)
    (merged,) = rowwise("merge_fwd", f_merge,
                        [(z, 512, lambda c: 7 + c), (z, 512, lambda c: 9 + c), (z, 512, lambda c: 11 + c),
                         (ya, 512, lambda c: c), (yb, 512, lambda c: c), (yc, 512, lambda c: c)],
                        [(P["b_gate"][k], (1, 512), lambda c: c) for k in range(3)],
                        [(D, 512, lambda c: c, bf16)], T, 256, ncol=2)
    x1 = mm_nn("out_proj", merged, P["w_out"], f32, add=x)
    sv.update(ya_in=ya_in, ya=ya, c0=c0, c1=c1, c3=c3, yb=yb, q8=q8, kn=kn, vb=vb, o=o, tot=tot, yc=yc, merged=merged, x1=x1)
    (h2,) = rowwise("rms_fwd", f_rms, [(x1, D, 0)], [(P["ln2_g"], (1, D), None)], [(D, D, 0, bf16)], T, 256)
    up = mm_nn("up_proj", h2, P["w_up"], f32)
    act = ffn_act_fwd(up, P["ffn_conv_w"], P["ffn_conv_b"], T)
    x2 = mm_nn("down_proj", act, P["w_down"], f32, add=x1, tn=256)
    sv.update(h2=h2, up=up, act=act)
    return x2, sv


def _backward_layer(dx2, P, sv, T):
    D = dx2.shape[1]
    G = {}
    G["w_down"] = mm_tn("dw_down", sv["act"], dx2, tk=256)
    dact = mm_nt("d_act", dx2, P["w_down"], f32, tko=256, tc=D)
    dup, dcw0, dcw1, dcw2, G["ffn_conv_b"] = ffn_act_bwd(sv["up"], P["ffn_conv_w"], P["ffn_conv_b"], dact, T)
    G["ffn_conv_w"] = jnp.concatenate([dcw0, dcw1, dcw2], axis=0)
    G["w_up"] = mm_tn("dw_up", sv["h2"], dup, n_slab=P["w_up"].shape[1] // N_DEV)
    dh2 = mm_nt("d_h2", dup, P["w_up"], f32)
    dx1, G["ln2_g"] = rowwise_bwd("rms_bwd", f_rms, [(sv["x1"], D, 0)], [(P["ln2_g"], (1, D), None)], [(dh2, D, 0)],
                                  [f32], T, 256, adds=[(dx2, D, 0)])
    G["w_out"] = mm_tn("dw_out", sv["merged"], dx1)
    dmerged = mm_nt("d_merged", dx1, P["w_out"], f32)
    z = sv["z"]
    dg0, dg1, dg2, dya, dyb, dyc, db0, db1, db2 = rowwise_bwd(
        "merge_bwd", f_merge,
        [(z, 512, lambda c: 7 + c), (z, 512, lambda c: 9 + c), (z, 512, lambda c: 11 + c),
         (sv["ya"], 512, lambda c: c), (sv["yb"], 512, lambda c: c), (sv["yc"], 512, lambda c: c)],
        [(P["b_gate"][k], (1, 512), lambda c: c) for k in range(3)],
        [(dmerged, 512, lambda c: c)], [bf16] * 6, T, 256, ncol=2)
    G["b_gate"] = jnp.concatenate([db0, db1, db2], axis=0)
    G["w_c_out"] = mm_tn("dw_c_out", sv["o"], dyc, n_slab=CHUNK)
    do = mm_nt("d_o", dyc, P["w_c_out"], f32)
    dq8, dkn, dvb = attn_bwd(sv["q8"], sv["kn"], sv["vb"], do, sv["tot"], T)
    dzq, dzk, dzv, G["q_norm_g"], G["k_norm_g"] = rowwise_bwd(
        "qkv_bwd", f_qkv, [(z, 512, 4), (z, 512, 5), (z, 512, 6)],
        [(P["q_norm_g"], (1, HEAD_DIM), None), (P["k_norm_g"], (1, HEAD_DIM), None)],
        [(dq8, 512, 0), (dkn, 512, 0), (dvb, 512, 0)], [bf16] * 3, T, 256)
    G["w_b_out"] = mm_tn("dw_b_out", sv["c3"], dyb, n_slab=CHUNK)
    dc3 = mm_nt("d_c3", dyb, P["w_b_out"], f32)
    dc1, G["cv_ln_g"], G["cv_ln_b"] = rowwise_bwd(
        "lnsilu_bwd", f_lnsilu, [(sv["c1"], 512, 0)], [(P["cv_ln_g"], (1, 512), None), (P["cv_ln_b"], (1, 512), None)],
        [(dc3, 512, 0)], [f32], T, 256)
    dc0, G["cv_w"], G["cv_b"] = conv31_bwd(sv["c0"], P["cv_w"], dc1, T)
    dzp, dzgl = rowwise_bwd("glu_bwd", f_glu, [(z, 512, 2), (z, 512, 3)], [], [(dc0, 512, 0)], [bf16] * 2, T, 256)
    G["w_a_out"] = mm_tn("dw_a_out", sv["ya_in"], dya, n_slab=CHUNK)
    dya_in = mm_nt("d_ya_in", dya, P["w_a_out"], f32)
    dzu, dzv_a, G["sg_ln_g"], G["sg_ln_b"], G["sg_w"], dsbt = rowwise_bwd(
        "sgu_bwd", f_sgu, [(z, 512, 0), (z, 512, 1)],
        [(P["sg_ln_g"], (1, 512), None), (P["sg_ln_b"], (1, 512), None), (P["sg_w"], (8, CHUNK, CHUNK), None),
         (P["sg_b_t"], (CHUNK, 8), None)],
        [(dya_in, 512, 0)], [bf16] * 2, T, CHUNK)
    G["sg_b"] = dsbt.T
    dz = jnp.concatenate([dzu, dzv_a, dzp, dzgl, dzq, dzk, dzv, dg0, dg1, dg2], axis=1)
    G["w_in"] = mm_tn("dw_in", sv["h1"], dz, n_slab=P["w_in"].shape[1] // N_DEV)
    dh1 = mm_nt("d_h1", dz, P["w_in"], f32)
    dx0, G["ln1_g"] = rowwise_bwd("rms_bwd", f_rms, [(sv["x0"], D, 0)], [(P["ln1_g"], (1, D), None)], [(dh1, D, 0)],
                                  [f32], T, 256, adds=[(dx1, D, 0)])
    return dx0, G


def kernel(x, ln1_g, w_in, b_gate, sg_ln_g, sg_ln_b, sg_w, sg_b, w_a_out, cv_w, cv_b, cv_ln_g, cv_ln_b, w_b_out, q_norm_g, k_norm_g, w_c_out, w_out, ln2_g, w_up, ffn_conv_w, ffn_conv_b, w_down, loss_target, m_ln1_g, m_w_in, m_b_gate, m_sg_ln_g, m_sg_ln_b, m_sg_w, m_sg_b, m_w_a_out, m_cv_w, m_cv_b, m_cv_ln_g, m_cv_ln_b, m_w_b_out, m_q_norm_g, m_k_norm_g, m_w_c_out, m_w_out, m_ln2_g, m_w_up, m_ffn_conv_w, m_ffn_conv_b, m_w_down, v_ln1_g, v_w_in, v_b_gate, v_sg_ln_g, v_sg_ln_b, v_sg_w, v_sg_b, v_w_a_out, v_cv_w, v_cv_b, v_cv_ln_g, v_cv_ln_b, v_w_b_out, v_q_norm_g, v_k_norm_g, v_w_c_out, v_w_out, v_ln2_g, v_w_up, v_ffn_conv_w, v_ffn_conv_b, v_w_down):
    W = dict(ln1_g=ln1_g, w_in=w_in, b_gate=b_gate, sg_ln_g=sg_ln_g, sg_ln_b=sg_ln_b, sg_w=sg_w, sg_b=sg_b, w_a_out=w_a_out,
             cv_w=cv_w, cv_b=cv_b, cv_ln_g=cv_ln_g, cv_ln_b=cv_ln_b, w_b_out=w_b_out, q_norm_g=q_norm_g, k_norm_g=k_norm_g,
             w_c_out=w_c_out, w_out=w_out, ln2_g=ln2_g, w_up=w_up, ffn_conv_w=ffn_conv_w, ffn_conv_b=ffn_conv_b, w_down=w_down)
    M = dict(ln1_g=m_ln1_g, w_in=m_w_in, b_gate=m_b_gate, sg_ln_g=m_sg_ln_g, sg_ln_b=m_sg_ln_b, sg_w=m_sg_w, sg_b=m_sg_b,
             w_a_out=m_w_a_out, cv_w=m_cv_w, cv_b=m_cv_b, cv_ln_g=m_cv_ln_g, cv_ln_b=m_cv_ln_b, w_b_out=m_w_b_out,
             q_norm_g=m_q_norm_g, k_norm_g=m_k_norm_g, w_c_out=m_w_c_out, w_out=m_w_out, ln2_g=m_ln2_g, w_up=m_w_up,
             ffn_conv_w=m_ffn_conv_w, ffn_conv_b=m_ffn_conv_b, w_down=m_w_down)
    V = dict(ln1_g=v_ln1_g, w_in=v_w_in, b_gate=v_b_gate, sg_ln_g=v_sg_ln_g, sg_ln_b=v_sg_ln_b, sg_w=v_sg_w, sg_b=v_sg_b,
             w_a_out=v_w_a_out, cv_w=v_cv_w, cv_b=v_cv_b, cv_ln_g=v_cv_ln_g, cv_ln_b=v_cv_ln_b, w_b_out=v_w_b_out,
             q_norm_g=v_q_norm_g, k_norm_g=v_k_norm_g, w_c_out=v_w_c_out, w_out=v_w_out, ln2_g=v_ln2_g, w_up=v_w_up,
             ffn_conv_w=v_ffn_conv_w, ffn_conv_b=v_ffn_conv_b, w_down=v_w_down)
    T, D = x.shape[1], x.shape[2]
    L = DEPTH
    xs = x.reshape(T, D)
    target = loss_target.reshape(T, D)

    ss_shapes = [W[n].shape for n in SMALL_SHARDED]
    ss_rows = _rows_for(ss_shapes, 8)
    (ss_all,) = exchange("gather_small", [_pack([W[n] for n in SMALL_SHARDED], ss_rows)], ["gather"])
    full_small = {}
    pos = 0
    for n in SMALL_SHARDED:
        s = W[n].shape
        cnt = s[0] * s[1] * s[2]
        part = ss_all.reshape(N_DEV, -1)[:, pos:pos + cnt].reshape((N_DEV,) + s)
        full_small[n] = jnp.transpose(part, (1, 2, 0, 3)).reshape(s[0], s[1], N_DEV * s[2])
        pos += cnt

    params, saved = [], []
    cur = xs
    for l in range(L):
        slabs = exchange("gather_big", [W[n][l].astype(bf16) for n in BIG], ["gather"] * len(BIG))
        P = {}
        for n, s in zip(BIG, slabs):
            P[n] = assemble("assemble_" + n, s) if n in COL_SHARDED else s.reshape(-1, s.shape[-1])
        for n in REPL:
            P[n] = W[n][l]
        for n in ("ln1_g", "sg_ln_g", "sg_ln_b", "cv_b", "cv_ln_g", "cv_ln_b", "q_norm_g", "k_norm_g", "ln2_g", "ffn_conv_b"):
            P[n] = P[n].reshape(1, -1)
        P["sg_b_t"] = P["sg_b"].T
        P["cv_w"] = full_small["cv_w"][l]
        P["b_gate"] = [full_small["b_gate"][l][k:k + 1] for k in range(3)]
        P["ffn_conv_w"] = [full_small["ffn_conv_w"][l][k:k + 1] for k in range(3)]
        cur, sv = _forward_layer(cur, P, T)
        params.append(P)
        saved.append(sv)

    dy, loss_row = loss_head(cur, target, T)
    loss = lax.psum(loss_row[0, 0], ("x", "y", "c"))

    small_grads = [None] * L
    big_out = {n: [None] * L for n in BIG}
    dcur = dy
    for l in reversed(range(L)):
        dcur, G = _backward_layer(dcur, params[l], saved[l], T)
        small_grads[l] = G
        send = [G[n] if n in COL_SHARDED else G[n].reshape(N_DEV, -1, G[n].shape[-1]) for n in BIG]
        recv = exchange("scatter_big", send, ["scatter"] * len(BIG))
        for n, parts in zip(BIG, recv):
            big_out[n][l] = adamw("adamw_" + n, W[n][l], M[n][l], V[n][l], parts)

    repl_shapes = [W[n].shape for n in REPL]
    repl_rows = _rows_for(repl_shapes, 512)
    repl_grads = [jnp.stack([small_grads[l][n].reshape(W[n].shape[1:]) for l in range(L)]) for n in REPL]
    ss_grads = []
    for n in SMALL_SHARDED:
        g = jnp.stack([small_grads[l][n] for l in range(L)])
        s = W[n].shape
        ss_grads.append(jnp.transpose(g.reshape(s[0], s[1], N_DEV, s[2]), (2, 0, 1, 3)).reshape(N_DEV, -1))
    ss_send = jnp.concatenate(ss_grads, axis=1)
    ss_send = jnp.pad(ss_send, ((0, 0), (0, ss_rows * 128 - ss_send.shape[1]))).reshape(N_DEV, ss_rows, 128)
    repl_parts, ss_parts = exchange("exchange_small", [_pack(repl_grads, repl_rows), ss_send], ["gather", "scatter"])
    repl_res = adamw("adamw_repl", _pack([W[n] for n in REPL], repl_rows), _pack([M[n] for n in REPL], repl_rows),
                     _pack([V[n] for n in REPL], repl_rows), repl_parts)
    ss_res = adamw("adamw_small", _pack([W[n] for n in SMALL_SHARDED], ss_rows), _pack([M[n] for n in SMALL_SHARDED], ss_rows),
                   _pack([V[n] for n in SMALL_SHARDED], ss_rows), ss_parts)

    results = [{}, {}, {}, {}]
    for k in range(4):
        for n, a in zip(REPL, _unpack(repl_res[k], repl_shapes)):
            results[k][n] = a
        for n, a in zip(SMALL_SHARDED, _unpack(ss_res[k], ss_shapes)):
            results[k][n] = a
        for n in BIG:
            results[k][n] = jnp.stack([big_out[n][l][k] for l in range(L)])
    out = [loss, dcur.reshape(1, T, D)]
    for k in range(4):
        out += [results[k][n] for n in W_NAMES]
    return tuple(out)
```

```python
import functools

import jax
import jax.numpy as jnp
from jax import lax
from jax.experimental import pallas as pl
from jax.experimental.pallas import tpu as pltpu

f32 = jnp.float32
bf16 = jnp.bfloat16

EPS = 1e-6
N_DEV = 8
DEPTH = 4
CHUNK = 128
HEAD_DIM = 64
N_HEADS = 8
CV_KERNEL = 31
VMEM_LIMIT_BYTES = 56 * 2 ** 20

ADAM_LR = 0.001
ADAM_B1 = 0.9
ADAM_B2 = 0.999
ADAM_EPS = 1e-08
ADAM_WD = 0.01
ADAM_STEP = 10

MESH = pl.DeviceIdType.MESH


def _cparams(n_grid):
    return pltpu.CompilerParams(dimension_semantics=("arbitrary",) * n_grid, vmem_limit_bytes=VMEM_LIMIT_BYTES)


def exchange(name, arrays, modes):
    n = len(arrays)

    def body(*refs):
        copies = _direct_copies(refs[:n], refs[n:2 * n], modes, *refs[2 * n:])
        for cp in copies:
            cp.start()
        for cp in copies:
            cp.wait()

    return pl.pallas_call(
        body, name=name, out_shape=_exchange_out_shapes(arrays, modes),
        in_specs=[_ANY] * n, out_specs=[_ANY] * n, scratch_shapes=_exchange_sems(n),
    )(*arrays)


_ANY = pl.BlockSpec(memory_space=pl.ANY)


def _exchange_out_shapes(arrays, modes):
    return [jax.ShapeDtypeStruct((N_DEV,) + tuple(a.shape) if m == "gather" else tuple(a.shape), a.dtype)
            for a, m in zip(arrays, modes)]


def _exchange_sems(n):
    return [pltpu.SemaphoreType.DMA((n, N_DEV - 1)), pltpu.SemaphoreType.DMA((n, N_DEV - 1)), pltpu.SemaphoreType.DMA((n,))]


def _direct_copies(ins, outs, modes, send_sems, recv_sems, local_sems):
    x, y, c = lax.axis_index("x"), lax.axis_index("y"), lax.axis_index("c")
    me = 4 * x + 2 * y + c
    copies = []
    for k in range(len(ins)):
        src_mine = ins[k] if modes[k] == "gather" else ins[k].at[me]
        copies.append(pltpu.make_async_copy(src_mine, outs[k].at[me], local_sems.at[k]))
        for r in range(1, N_DEV):
            px = 1 - x if r & 4 else x
            py = 1 - y if r & 2 else y
            pc = 1 - c if r & 1 else c
            src = ins[k] if modes[k] == "gather" else ins[k].at[4 * px + 2 * py + pc]
            copies.append(pltpu.make_async_remote_copy(
                src_ref=src, dst_ref=outs[k].at[me], send_sem=send_sems.at[k, r - 1], recv_sem=recv_sems.at[k, r - 1],
                device_id=(px, py, pc), device_id_type=MESH))
    return copies


def _two_level_gather(ins, outs, send_sems, recv_sems, local_sems):
    x, y, c = lax.axis_index("x"), lax.axis_index("y"), lax.axis_index("c")
    me = 4 * x + 2 * y + c
    sibling = (x, y, 1 - c)
    chips = [(1 - x, y), (x, 1 - y), (1 - x, 1 - y)]

    def slot(px, py, pc):
        return 4 * px + 2 * py + pc

    def copy(k, sem, block, to, src=None):
        dst = outs[k].at[block]
        return pltpu.make_async_remote_copy(
            src_ref=dst if src is None else src, dst_ref=dst, send_sem=send_sems.at[k, sem], recv_sem=recv_sems.at[k, sem],
            device_id=to, device_id_type=MESH)

    def local(k):
        return pltpu.make_async_copy(ins[k], outs[k].at[me], local_sems.at[k])

    def first(k):
        return [copy(k, 0, me, sibling, src=ins[k])] + [copy(k, 1 + j, me, (*chip, c), src=ins[k]) for j, chip in enumerate(chips)]

    def passed_on(k, j):
        return copy(k, 4 + j, slot(*chips[j], c), sibling)

    def passed(k):
        return [passed_on(k, j) for j in range(3)]

    def start():
        for k in range(len(ins)):
            local(k).start()
            for cp in first(k):
                cp.start()

    def forward():
        for k in range(len(ins)):
            for j, chip in enumerate(chips):
                copy(k, 1 + j, slot(*chip, c), sibling).wait_recv()
                passed_on(k, j).start()

    def finish():
        for k in range(len(ins)):
            copy(k, 0, slot(x, y, 1 - c), sibling).wait_recv()
            for j, chip in enumerate(chips):
                copy(k, 4 + j, slot(*chip, 1 - c), sibling).wait_recv()
            for cp in first(k) + passed(k):
                cp.wait_send()
            local(k).wait()

    return start, forward, finish


def gather2(name, arrays):
    n = len(arrays)

    def body(*refs):
        start, forward, finish = _two_level_gather(refs[:n], refs[n:2 * n], *refs[2 * n:])
        start()
        forward()
        finish()

    return pl.pallas_call(
        body, name=name, out_shape=_exchange_out_shapes(arrays, ["gather"] * n),
        in_specs=[_ANY] * n, out_specs=[_ANY] * n, scratch_shapes=_exchange_sems(n),
    )(*arrays)


def assemble(name, slabs):
    _, K, Ns = slabs.shape
    g = N_DEV if Ns % 128 == 0 else 2

    def body(w_ref, o_ref):
        o_ref[...] = jnp.concatenate([w_ref[s] for s in range(g)], axis=1)

    return pl.pallas_call(
        body, name=name, grid=(N_DEV // g,),
        in_specs=[pl.BlockSpec((g, K, Ns), lambda m: (m, 0, 0))],
        out_specs=pl.BlockSpec((K, g * Ns), lambda m: (0, m)),
        out_shape=jax.ShapeDtypeStruct((K, N_DEV * Ns), slabs.dtype),
        compiler_params=_cparams(1),
    )(slabs)


def mm_nn(name, a, b, out_dtype, add=None, tm=1024, tn=512):
    M, K = a.shape
    N = b.shape[1]
    tm, tn = min(tm, M), min(tn, N)
    has_add = add is not None

    def body(a_ref, b_ref, *rest):
        o_ref = rest[-1]
        acc = jnp.dot(a_ref[...].astype(bf16), b_ref[...].astype(bf16), preferred_element_type=f32)
        if has_add:
            acc = acc + rest[0][...]
        o_ref[...] = acc.astype(o_ref.dtype)

    in_specs = [pl.BlockSpec((tm, K), lambda i, j: (i, 0)), pl.BlockSpec((K, tn), lambda i, j: (0, j))]
    ops = [a, b]
    if has_add:
        in_specs.append(pl.BlockSpec((tm, tn), lambda i, j: (i, j)))
        ops.append(add)
    return pl.pallas_call(
        body, name=name, grid=(M // tm, N // tn), in_specs=in_specs,
        out_specs=pl.BlockSpec((tm, tn), lambda i, j: (i, j)),
        out_shape=jax.ShapeDtypeStruct((M, N), out_dtype), compiler_params=_cparams(2),
    )(*ops)


def mm_nt(name, a, b, out_dtype, tm=1024, tko=None, tc=512):
    M, C = a.shape
    Ko = b.shape[0]
    tm, tc = min(tm, M), min(tc, C)
    tko = Ko if tko is None else tko
    nc = C // tc

    def body(a_ref, b_ref, o_ref, acc_ref):
        c = pl.program_id(2)
        part = lax.dot_general(a_ref[...].astype(bf16), b_ref[...].astype(bf16), (((1,), (1,)), ((), ())),
                               preferred_element_type=f32)

        @pl.when(c == 0)
        def _():
            acc_ref[...] = part

        @pl.when(c > 0)
        def _():
            acc_ref[...] += part

        @pl.when(c == nc - 1)
        def _():
            o_ref[...] = acc_ref[...].astype(o_ref.dtype)

    return pl.pallas_call(
        body, name=name, grid=(M // tm, Ko // tko, nc),
        in_specs=[pl.BlockSpec((tm, tc), lambda i, j, c: (i, c)), pl.BlockSpec((tko, tc), lambda i, j, c: (j, c))],
        out_specs=pl.BlockSpec((tm, tko), lambda i, j, c: (i, j)),
        out_shape=jax.ShapeDtypeStruct((M, Ko), out_dtype),
        scratch_shapes=[pltpu.VMEM((tm, tko), f32)], compiler_params=_cparams(3),
    )(a, b)


def mm_tn(name, a, b, n_slab=None, tk=512, tn=512):
    T, K = a.shape
    N = b.shape[1]
    tk = min(tk, K)
    if n_slab is None:
        tn = min(tn, N)

        def body(a_ref, b_ref, o_ref):
            o_ref[...] = lax.dot_general(a_ref[...].astype(bf16), b_ref[...].astype(bf16), (((0,), (0,)), ((), ())),
                                         preferred_element_type=f32).astype(o_ref.dtype)

        return pl.pallas_call(
            body, name=name, grid=(K // tk, N // tn),
            in_specs=[pl.BlockSpec((T, tk), lambda i, j: (0, i)), pl.BlockSpec((T, tn), lambda i, j: (0, j))],
            out_specs=pl.BlockSpec((tk, tn), lambda i, j: (i, j)),
            out_shape=jax.ShapeDtypeStruct((K, N), bf16), compiler_params=_cparams(2),
        )(a, b)

    Ns = n_slab
    g = N_DEV if Ns % 128 == 0 else 2
    tn = g * Ns

    def body(a_ref, b_ref, o_ref):
        val = lax.dot_general(a_ref[...].astype(bf16), b_ref[...].astype(bf16), (((0,), (0,)), ((), ())),
                              preferred_element_type=f32)
        for s in range(g):
            o_ref[s] = val[:, s * Ns:(s + 1) * Ns].astype(o_ref.dtype)

    return pl.pallas_call(
        body, name=name, grid=(K // tk, N_DEV // g),
        in_specs=[pl.BlockSpec((T, tk), lambda i, j: (0, i)), pl.BlockSpec((T, tn), lambda i, j: (0, j))],
        out_specs=pl.BlockSpec((g, tk, Ns), lambda i, j: (j, i, 0)),
        out_shape=jax.ShapeDtypeStruct((N_DEV, K, Ns), bf16), compiler_params=_cparams(2),
    )(a, b)


def _tile_spec(w, tm, cb):
    if callable(cb):
        return pl.BlockSpec((tm, w), lambda c, i: (i, cb(c)))
    return pl.BlockSpec((tm, w), lambda c, i: (i, cb))


def _param_spec(block, cb):
    nd = len(block)
    if cb is None:
        return pl.BlockSpec(block, lambda c, i: (0,) * nd)
    return pl.BlockSpec(block, lambda c, i: (0,) * (nd - 1) + (cb(c),))


def rowwise(name, fn, tiled, params, outs, T, tm, ncol=1):
    n_in = len(tiled) + len(params)
    n_t = len(tiled)

    def body(*refs):
        ins = [r[...].astype(f32) for r in refs[:n_t]] + [r[...] for r in refs[n_t:n_in]]
        res = fn(*ins)
        for r, o in zip(refs[n_in:], res):
            r[...] = o.astype(r.dtype)

    return pl.pallas_call(
        body, name=name, grid=(ncol, T // tm),
        in_specs=[_tile_spec(w, tm, cb) for _, w, cb in tiled] + [_param_spec(blk, cb) for _, blk, cb in params],
        out_specs=[_tile_spec(w, tm, cb) for _, w, cb, _ in outs],
        out_shape=[jax.ShapeDtypeStruct((T, cols), dt) for cols, _, _, dt in outs],
        compiler_params=_cparams(2),
    )(*[a for a, _, _ in tiled], *[a for a, _, _ in params])


def rowwise_bwd(name, fn, tiled, params, cts, grads, T, tm, ncol=1, adds=None):
    n_t, n_p, n_c = len(tiled), len(params), len(cts)
    adds = adds or [None] * n_t
    add_list = [(k, a) for k, a in enumerate(adds) if a is not None]
    want = [k for k, g in enumerate(grads) if g is not None]
    n_a = len(add_list)

    def body(*refs):
        pos = 0
        t_refs = refs[pos:pos + n_t]; pos += n_t
        p_refs = refs[pos:pos + n_p]; pos += n_p
        c_refs = refs[pos:pos + n_c]; pos += n_c
        a_refs = refs[pos:pos + n_a]; pos += n_a
        g_refs = refs[pos:pos + len(want)]; pos += len(want)
        pg_refs = refs[pos:pos + n_p]
        primals = [r[...].astype(f32) for r in t_refs] + [r[...] for r in p_refs]
        _, vjp = jax.vjp(fn, *primals)
        g = vjp(tuple(r[...].astype(f32) for r in c_refs))
        add_of = {k: a_refs[n][...] for n, (k, _) in enumerate(add_list)}
        for n, k in enumerate(want):
            val = g[k]
            if k in add_of:
                val = val + add_of[k]
            g_refs[n][...] = val.astype(g_refs[n].dtype)
        i = pl.program_id(1)
        for k in range(n_p):
            @pl.when(i == 0)
            def _(k=k):
                pg_refs[k][...] = g[n_t + k]

            @pl.when(i > 0)
            def _(k=k):
                pg_refs[k][...] += g[n_t + k]

    in_specs = ([_tile_spec(w, tm, cb) for _, w, cb in tiled] + [_param_spec(blk, cb) for _, blk, cb in params]
                + [_tile_spec(w, tm, cb) for _, w, cb in cts] + [_tile_spec(w, tm, cb) for _, (_, w, cb) in add_list])
    ops = ([a for a, _, _ in tiled] + [a for a, _, _ in params] + [a for a, _, _ in cts]
           + [a for _, (a, _, _) in add_list])
    out_specs, out_shape = [], []
    for k in want:
        w = tiled[k][1]
        out_specs.append(_tile_spec(w, tm, lambda c: c))
        out_shape.append(jax.ShapeDtypeStruct((T, ncol * w), grads[k]))
    for a, blk, cb in params:
        out_specs.append(_param_spec(blk, cb))
        out_shape.append(jax.ShapeDtypeStruct(a.shape, f32))
    return pl.pallas_call(
        body, name=name, grid=(ncol, T // tm), in_specs=in_specs, out_specs=out_specs, out_shape=out_shape,
        compiler_params=_cparams(2),
    )(*ops)


@jax.custom_vjp
def _bdot(a, b):
    return jnp.dot(a.astype(bf16), b.astype(bf16), preferred_element_type=f32)


def _bdot_fwd(a, b):
    return _bdot(a, b), (a, b)


def _bdot_bwd(res, ct):
    a, b = res
    ctb = ct.astype(bf16)
    da = lax.dot_general(ctb, b.astype(bf16), (((1,), (1,)), ((), ())), preferred_element_type=f32)
    db = lax.dot_general(a.astype(bf16), ctb, (((0,), (0,)), ((), ())), preferred_element_type=f32)
    return da, db


_bdot.defvjp(_bdot_fwd, _bdot_bwd)


def _layer_norm(x, g, b):
    mu = jnp.mean(x, axis=-1, keepdims=True)
    xc = x - mu
    y = xc * lax.rsqrt(jnp.mean(xc * xc, axis=-1, keepdims=True) + EPS)
    return y * g + b


def f_rms(x, g):
    y = x * lax.rsqrt(jnp.mean(x * x, axis=-1, keepdims=True) + EPS)
    return (y * g,)


def f_sgu(zu, zv, ln_g, ln_b, wm, sgb_t):
    u = jax.nn.gelu(zu)
    vn = _layer_norm(jax.nn.gelu(zv), ln_g, ln_b)
    row = lax.broadcasted_iota(jnp.int32, (CHUNK, CHUNK), 0)
    col = lax.broadcasted_iota(jnp.int32, (CHUNK, CHUNK), 1)
    tril = col <= row
    low = col < HEAD_DIM
    parts = []
    for p in range(4):
        vp = vn[:, CHUNK * p:CHUNK * (p + 1)]
        w0 = jnp.where(tril, wm[2 * p], 0.0)
        w1 = jnp.where(tril, wm[2 * p + 1], 0.0)
        parts.append(_bdot(w0, jnp.where(low, vp, 0.0)) + _bdot(w1, jnp.where(low, 0.0, vp)))
    s = jnp.concatenate(parts, axis=1)
    lane_g = lax.shift_right_logical(lax.broadcasted_iota(jnp.int32, s.shape, 1), 6)
    bias = jnp.zeros_like(s)
    for g in range(8):
        bias = jnp.where(lane_g == g, sgb_t[:, g:g + 1], bias)
    return (u * (s + bias),)


def f_glu(p, gl):
    return (p * jax.nn.sigmoid(gl),)


def f_lnsilu(c1, g, b):
    return (jax.nn.silu(_layer_norm(c1, g, b)),)


def _head_norm(x, g64):
    g = jnp.concatenate([g64] * N_HEADS, axis=1)
    lane_h = lax.shift_right_logical(lax.broadcasted_iota(jnp.int32, x.shape, 1), 6)
    x2 = x * x
    r = jnp.zeros_like(x)
    for h in range(N_HEADS):
        mh = lane_h == h
        ms = jnp.sum(jnp.where(mh, x2, 0.0), axis=-1, keepdims=True) * (1.0 / HEAD_DIM)
        r = jnp.where(mh, lax.rsqrt(ms + EPS), r)
    return (x * r) * g


def f_qkv(zq, zk, zv, qg, kg):
    return (_head_norm(zq, qg) * 0.125, _head_norm(zk, kg), zv)


def f_merge(g0, g1, g2, ya, yb, yc, b0, b1, b2):
    return (jax.nn.sigmoid(g0 + b0) * ya + jax.nn.sigmoid(g1 + b1) * yb + jax.nn.sigmoid(g2 + b2) * yc,)


HALO = 32


def conv31_fwd(x, w, b, T, tt=256):
    C = x.shape[1]
    r = tt // HALO

    def body(x_ref, h_ref, w_ref, b_ref, y_ref, buf):
        i = pl.program_id(0)
        halo = h_ref[...]
        buf[0:HALO, :] = jnp.where(i > 0, halo, jnp.zeros_like(halo))
        buf[HALO:HALO + tt, :] = x_ref[...]
        acc = jnp.zeros((tt, C), f32) + b_ref[...]
        for k in range(CV_KERNEL):
            acc = acc + w_ref[k:k + 1, :] * buf[pl.ds(HALO - (CV_KERNEL - 1) + k, tt), :]
        y_ref[...] = acc

    return pl.pallas_call(
        body, name="conv31_fwd", grid=(T // tt,),
        in_specs=[pl.BlockSpec((tt, C), lambda i: (i, 0)),
                  pl.BlockSpec((HALO, C), lambda i: (jnp.maximum(i * r - 1, 0), 0)),
                  pl.BlockSpec((CV_KERNEL, C), lambda i: (0, 0)), pl.BlockSpec((1, C), lambda i: (0, 0))],
        out_specs=pl.BlockSpec((tt, C), lambda i: (i, 0)),
        out_shape=jax.ShapeDtypeStruct((T, C), f32),
        scratch_shapes=[pltpu.VMEM((HALO + tt, C), f32)], compiler_params=_cparams(1),
    )(x, x, w, b)


def conv31_bwd(x, w, dy, T, tt=256):
    C = x.shape[1]
    r = tt // HALO
    n = T // tt

    def body(x_ref, h_ref, w_ref, dy_ref, dyn_ref, dx_ref, dw_ref, db_ref, xbuf, dbuf):
        i = pl.program_id(0)
        halo = h_ref[...]
        xbuf[0:HALO, :] = jnp.where(i > 0, halo, jnp.zeros_like(halo))
        xbuf[HALO:HALO + tt, :] = x_ref[...]
        nxt = dyn_ref[...]
        dy = dy_ref[...]
        dbuf[0:tt, :] = dy
        dbuf[tt:tt + HALO, :] = jnp.where(i < n - 1, nxt, jnp.zeros_like(nxt))

        @pl.when(i == 0)
        def _():
            dw_ref[...] = jnp.zeros_like(dw_ref)
            db_ref[...] = jnp.zeros_like(db_ref)

        acc = jnp.zeros((tt, C), f32)
        for k in range(CV_KERNEL):
            acc = acc + w_ref[k:k + 1, :] * dbuf[pl.ds(CV_KERNEL - 1 - k, tt), :]
            xs = xbuf[pl.ds(HALO - (CV_KERNEL - 1) + k, tt), :]
            dw_ref[k:k + 1, :] += jnp.sum(dy * xs, axis=0, keepdims=True)
        dx_ref[...] = acc
        db_ref[...] += jnp.sum(dy, axis=0, keepdims=True)

    return pl.pallas_call(
        body, name="conv31_bwd", grid=(n,),
        in_specs=[pl.BlockSpec((tt, C), lambda i: (i, 0)),
                  pl.BlockSpec((HALO, C), lambda i: (jnp.maximum(i * r - 1, 0), 0)),
                  pl.BlockSpec((CV_KERNEL, C), lambda i: (0, 0)),
                  pl.BlockSpec((tt, C), lambda i: (i, 0)),
                  pl.BlockSpec((HALO, C), lambda i: (jnp.minimum((i + 1) * r, n * r - 1), 0))],
        out_specs=[pl.BlockSpec((tt, C), lambda i: (i, 0)), pl.BlockSpec((CV_KERNEL, C), lambda i: (0, 0)),
                   pl.BlockSpec((1, C), lambda i: (0, 0))],
        out_shape=[jax.ShapeDtypeStruct((T, C), f32), jax.ShapeDtypeStruct((CV_KERNEL, C), f32),
                   jax.ShapeDtypeStruct((1, C), f32)],
        scratch_shapes=[pltpu.VMEM((HALO + tt, C), f32), pltpu.VMEM((HALO + tt, C), f32)], compiler_params=_cparams(1),
    )(x, x, w, dy, dy)


FFN_TC = 128


def _shift_down(x, s, row):
    return jnp.where(row >= s, pltpu.roll(x, s, 0), 0.0)


def _shift_up(x, s, row, T):
    return jnp.where(row < T - s, pltpu.roll(x, T - s, 0), 0.0)


def _conv3(x, w0, w1, w2, b, row):
    return w0 * _shift_down(x, 2, row) + w1 * _shift_down(x, 1, row) + w2 * x + b


def ffn_act_fwd(up, cw, cb, T):
    F = up.shape[1] // 2
    nj = F // FFN_TC

    def body(g_ref, v_ref, g0, g1, g2, gb, v0, v1, v2, vb, o_ref):
        row = lax.broadcasted_iota(jnp.int32, (T, FFN_TC), 0)
        gc = _conv3(g_ref[...], g0[...], g1[...], g2[...], gb[...], row)
        vc = _conv3(v_ref[...], v0[...], v1[...], v2[...], vb[...], row)
        o_ref[...] = (jax.nn.silu(gc) * vc).astype(o_ref.dtype)

    gspec = pl.BlockSpec((T, FFN_TC), lambda j: (0, j))
    vspec = pl.BlockSpec((T, FFN_TC), lambda j: (0, j + nj))
    pg = pl.BlockSpec((1, FFN_TC), lambda j: (0, j))
    pv = pl.BlockSpec((1, FFN_TC), lambda j: (0, j + nj))
    return pl.pallas_call(
        body, name="ffn_act_fwd", grid=(nj,),
        in_specs=[gspec, vspec, pg, pg, pg, pg, pv, pv, pv, pv], out_specs=gspec,
        out_shape=jax.ShapeDtypeStruct((T, F), bf16), compiler_params=_cparams(1),
    )(up, up, cw[0], cw[1], cw[2], cb, cw[0], cw[1], cw[2], cb)


def ffn_act_bwd(up, cw, cb, dact, T):
    F = up.shape[1] // 2
    nj = F // FFN_TC

    def side_grads(dc, x, w0, w1, w2, row, dup_ref, p_refs):
        dup = w2 * dc + w1 * _shift_up(dc, 1, row, T) + w0 * _shift_up(dc, 2, row, T)
        dup_ref[...] = dup.astype(dup_ref.dtype)
        p_refs[0][...] = jnp.sum(dc * _shift_down(x, 2, row), axis=0, keepdims=True)
        p_refs[1][...] = jnp.sum(dc * _shift_down(x, 1, row), axis=0, keepdims=True)
        p_refs[2][...] = jnp.sum(dc * x, axis=0, keepdims=True)
        p_refs[3][...] = jnp.sum(dc, axis=0, keepdims=True)

    def body(g_ref, v_ref, g0, g1, g2, gb, v0, v1, v2, vb, d_ref, dupg_ref, dupv_ref, *p_refs):
        row = lax.broadcasted_iota(jnp.int32, (T, FFN_TC), 0)
        xg, xv = g_ref[...], v_ref[...]
        gc = _conv3(xg, g0[...], g1[...], g2[...], gb[...], row)
        vc = _conv3(xv, v0[...], v1[...], v2[...], vb[...], row)
        d = d_ref[...]
        sg = jax.nn.sigmoid(gc)
        side_grads(d * vc * (sg * (1.0 + gc * (1.0 - sg))), xg, g0[...], g1[...], g2[...], row, dupg_ref, p_refs[:4])
        side_grads(d * (gc * sg), xv, v0[...], v1[...], v2[...], row, dupv_ref, p_refs[4:])

    gspec = pl.BlockSpec((T, FFN_TC), lambda j: (0, j))
    vspec = pl.BlockSpec((T, FFN_TC), lambda j: (0, j + nj))
    pg = pl.BlockSpec((1, FFN_TC), lambda j: (0, j))
    pv = pl.BlockSpec((1, FFN_TC), lambda j: (0, j + nj))
    res = pl.pallas_call(
        body, name="ffn_act_bwd", grid=(nj,),
        in_specs=[gspec, vspec, pg, pg, pg, pg, pv, pv, pv, pv, gspec],
        out_specs=[gspec, gspec] + [pg] * 8,
        out_shape=[jax.ShapeDtypeStruct((T, F), bf16)] * 2 + [jax.ShapeDtypeStruct((1, F), f32)] * 8,
        compiler_params=_cparams(1),
    )(up, up, cw[0], cw[1], cw[2], cb, cw[0], cw[1], cw[2], cb, dact)
    dup = jnp.concatenate([res[0], res[1]], axis=1)
    return (dup,) + tuple(jnp.concatenate([res[2 + k], res[6 + k]], axis=1) for k in range(4))


BQ = 128
BK = 256
NT_DIMS = (((1,), (1,)), ((), ()))
TN_DIMS = (((0,), (0,)), ((), ()))


def _split_dot(x, u):
    x1 = x.astype(bf16)
    x2 = (x - x1.astype(f32)).astype(bf16)
    n = x.shape[0]
    y = jnp.dot(jnp.concatenate([x1, x2], axis=0), u, preferred_element_type=f32)
    return y[0:n] + y[n:2 * n]


def _log_sigmoids(z):
    sp = jnp.log(1.0 + jnp.exp(-jnp.abs(z)))
    lsp = jnp.minimum(z, 0.0) - sp
    return lsp, lsp - z


def _stack_heads(x):
    head1 = lax.broadcasted_iota(jnp.int32, x.shape, 1) >= HEAD_DIM
    zero = jnp.zeros_like(x)
    return jnp.concatenate([jnp.where(head1, zero, x), jnp.where(head1, x, zero)], axis=0)


def _unstack_heads(y):
    head1 = lax.broadcasted_iota(jnp.int32, (BQ, y.shape[1]), 1) >= HEAD_DIM
    return jnp.where(head1, y[BQ:2 * BQ], y[0:BQ])


def _attn_masks():
    row = lax.broadcasted_iota(jnp.int32, (2 * BQ, BK), 0)
    col = lax.broadcasted_iota(jnp.int32, (2 * BQ, BK), 1)
    ur = lax.broadcasted_iota(jnp.int32, (BK, BK), 0)
    uc = lax.broadcasted_iota(jnp.int32, (BK, BK), 1)
    return (row & (BQ - 1)) - col, (ur > uc).astype(bf16), (ur < uc).astype(bf16)


def attn_fwd(q, k, v, T, gather=None):
    nq = T // BQ
    n_g = 0 if gather is None else len(gather)

    def body(*refs):
        q_ref, k_ref, v_ref = refs[:3]
        o_ref, tot_ref = refs[3 + n_g:5 + n_g]
        p, i = pl.program_id(0), pl.program_id(1)
        if n_g:
            start, forward, finish = _two_level_gather(refs[3:3 + n_g], refs[5 + n_g:5 + 2 * n_g], *refs[5 + 2 * n_g:])
            pl.when(jnp.logical_and(p == 0, i == 0))(start)
            pl.when(jnp.logical_and(p == 2, i == 0))(forward)
        qs = _stack_heads(q_ref[...])
        diff, u_after, _ = _attn_masks()

        def step(t, carry):
            acc, c = carry
            jb = i // 2 - t
            ks = pl.multiple_of(jb * BK, BK)
            kb = k_ref[pl.ds(ks, BK), :]
            vb = v_ref[pl.ds(ks, BK), :]
            m = diff > jb * BK - i * BQ
            z = lax.dot_general(qs, kb, NT_DIMS, preferred_element_type=f32)
            lsp, lsn = _log_sigmoids(z)
            lm = jnp.where(m, lsn, 0.0)
            a = jnp.where(m, jnp.exp(lsp + _split_dot(lm, u_after)), 0.0)
            acc = acc + jnp.exp(c) * jnp.dot(a.astype(bf16), vb, preferred_element_type=f32)
            return acc, c + jnp.sum(lm, axis=-1, keepdims=True)

        acc, c = lax.fori_loop(0, i // 2 + 1, step, (jnp.zeros((2 * BQ, 128), f32), jnp.zeros((2 * BQ, 1), f32)))
        o_ref[...] = _unstack_heads(acc).astype(o_ref.dtype)
        tot_ref[...] = _unstack_heads(jnp.broadcast_to(c, (2 * BQ, 128)))
        if n_g:
            pl.when(jnp.logical_and(p == 3, i == nq - 1))(finish)

    blk = pl.BlockSpec((BQ, 128), lambda p, i: (i, p))
    full = pl.BlockSpec((T, 128), lambda p, i: (0, p))
    g_list = [] if gather is None else list(gather)
    res = pl.pallas_call(
        body, name="attn_fwd_gather" if n_g else "attn_fwd", grid=(4, nq),
        in_specs=[blk, full, full] + [_ANY] * n_g, out_specs=[blk, blk] + [_ANY] * n_g,
        out_shape=[jax.ShapeDtypeStruct((T, 512), bf16), jax.ShapeDtypeStruct((T, 512), f32)]
        + _exchange_out_shapes(g_list, ["gather"] * n_g),
        scratch_shapes=_exchange_sems(n_g) if n_g else [], compiler_params=_cparams(2),
    )(q, k, v, *g_list)
    return res[0], res[1], list(res[2:])


def attn_bwd(q, k, v, do, tot, T, scatter=None):
    nq = T // BQ
    n_s = 0 if scatter is None else len(scatter)
    modes = ["scatter"] * n_s

    def body(*refs):
        q_ref, k_ref, v_ref, do_ref, tot_ref = refs[:5]
        dq_ref, dk_ref, dv_ref = refs[5 + n_s:8 + n_s]
        p, i = pl.program_id(0), pl.program_id(1)
        if n_s:
            def copies():
                return _direct_copies(refs[5:5 + n_s], refs[8 + n_s:8 + 2 * n_s], modes, *refs[8 + 2 * n_s:])

            @pl.when(jnp.logical_and(p == 0, i == 0))
            def _():
                for cp in copies():
                    cp.start()

        @pl.when(i == 0)
        def _():
            dk_ref[...] = jnp.zeros_like(dk_ref)
            dv_ref[...] = jnp.zeros_like(dv_ref)

        qs = _stack_heads(q_ref[...])
        dos = _stack_heads(do_ref[...].astype(bf16))
        totv = tot_ref[...]
        tots = jnp.concatenate([totv[:, 0:1], totv[:, HEAD_DIM:HEAD_DIM + 1]], axis=0)
        diff, u_after, u_before = _attn_masks()

        def step(jb, carry):
            dq, cl, cg = carry
            ks = pl.multiple_of(jb * BK, BK)
            kb = k_ref[pl.ds(ks, BK), :]
            vb = v_ref[pl.ds(ks, BK), :]
            m = diff > jb * BK - i * BQ
            z = lax.dot_general(qs, kb, NT_DIMS, preferred_element_type=f32)
            lsp, lsn = _log_sigmoids(z)
            lm = jnp.where(m, lsn, 0.0)
            a = jnp.where(m, jnp.exp(lsp + _split_dot(lm, u_after)), 0.0)
            g = a * lax.dot_general(dos, vb, NT_DIMS, preferred_element_type=f32)
            beta = jnp.exp(lsp)
            aa = jnp.where(m, g * jnp.exp(lsn) - _split_dot(g, u_before) * beta, 0.0)
            bb = jnp.where(m, beta, 0.0)
            cl = cl + jnp.sum(lm, axis=-1, keepdims=True)
            f = jnp.exp(tots - cl)
            dz = (f * aa - cg * bb).astype(bf16)
            cg = cg + f * jnp.sum(g, axis=-1, keepdims=True)
            dq = dq + jnp.dot(dz, kb, preferred_element_type=f32)
            dk_ref[pl.ds(ks, BK), :] += lax.dot_general(dz, qs, TN_DIMS, preferred_element_type=f32)
            dv_ref[pl.ds(ks, BK), :] += lax.dot_general((f * a).astype(bf16), dos, TN_DIMS, preferred_element_type=f32)
            return dq, cl, cg

        zc = jnp.zeros((2 * BQ, 1), f32)
        dq, _, _ = lax.fori_loop(0, i // 2 + 1, step, (jnp.zeros((2 * BQ, 128), f32), zc, zc))
        dq_ref[...] = _unstack_heads(dq)
        if n_s:
            @pl.when(jnp.logical_and(p == 3, i == nq - 1))
            def _():
                for cp in copies():
                    cp.wait()

    blk = pl.BlockSpec((BQ, 128), lambda p, i: (i, p))
    full = pl.BlockSpec((T, 128), lambda p, i: (0, p))
    shp = jax.ShapeDtypeStruct((T, 512), f32)
    s_list = [] if scatter is None else list(scatter)
    res = pl.pallas_call(
        body, name="attn_bwd_scatter" if n_s else "attn_bwd", grid=(4, nq),
        in_specs=[blk, full, full, blk, blk] + [_ANY] * n_s, out_specs=[blk, full, full] + [_ANY] * n_s,
        out_shape=[shp, shp, shp] + _exchange_out_shapes(s_list, modes),
        scratch_shapes=_exchange_sems(n_s) if n_s else [], compiler_params=_cparams(2),
    )(q, k, v, do, tot, *s_list)
    return res[0], res[1], res[2], list(res[3:])


def loss_head(y, target, T, tm=256):
    D = y.shape[1]

    def body(y_ref, t_ref, dy_ref, l_ref):
        i = pl.program_id(0)
        err = y_ref[...] - t_ref[...]
        dy_ref[...] = err * (1.0 / D)
        part = 0.5 * jnp.sum(jnp.sum(err * err, axis=-1, keepdims=True) * (1.0 / D), axis=0, keepdims=True)

        @pl.when(i == 0)
        def _():
            l_ref[...] = jnp.zeros_like(l_ref)

        l_ref[...] += jnp.broadcast_to(part, l_ref.shape)

    spec = pl.BlockSpec((tm, D), lambda i: (i, 0))
    return pl.pallas_call(
        body, name="loss_head", grid=(T // tm,), in_specs=[spec, spec],
        out_specs=[spec, pl.BlockSpec((1, 128), lambda i: (0, 0))],
        out_shape=[jax.ShapeDtypeStruct((T, D), f32), jax.ShapeDtypeStruct((1, 128), f32)],
        compiler_params=_cparams(1),
    )(y, target)


def _row_tile(rows, cols):
    best = None
    for t in range(16, rows + 1, 16):
        if rows % t == 0 and t * cols * 4 <= 512 * 1024:
            best = t
    return best if best is not None else rows


def adamw(name, w, m, v, parts):
    R, C = w.shape
    tr = _row_tile(R, C)

    def body(w_ref, m_ref, v_ref, p_ref, g_ref, d_ref, nm_ref, nv_ref):
        g = p_ref[0].astype(f32)
        for s in range(1, N_DEV):
            g = g + p_ref[s].astype(f32)
        wv = w_ref[...]
        mn = ADAM_B1 * m_ref[...] + (1.0 - ADAM_B1) * g
        vn = ADAM_B2 * v_ref[...] + (1.0 - ADAM_B2) * jnp.square(g)
        m_hat = mn / (1.0 - ADAM_B1 ** ADAM_STEP)
        v_hat = vn / (1.0 - ADAM_B2 ** ADAM_STEP)
        g_ref[...] = g
        d_ref[...] = -ADAM_LR * (m_hat / (jnp.sqrt(v_hat) + ADAM_EPS) + ADAM_WD * wv)
        nm_ref[...] = mn
        nv_ref[...] = vn

    spec = pl.BlockSpec((tr, C), lambda i: (i, 0))
    shp = jax.ShapeDtypeStruct((R, C), f32)
    return pl.pallas_call(
        body, name=name, grid=(R // tr,),
        in_specs=[spec, spec, spec, pl.BlockSpec((N_DEV, tr, C), lambda i: (0, i, 0))],
        out_specs=[spec] * 4, out_shape=[shp] * 4, compiler_params=_cparams(1),
    )(w, m, v, parts)


REPL = ["ln1_g", "sg_ln_g", "sg_ln_b", "sg_w", "sg_b", "cv_b", "cv_ln_g", "cv_ln_b", "q_norm_g", "k_norm_g", "ln2_g",
        "ffn_conv_b"]
SMALL_SHARDED = ["b_gate", "cv_w", "ffn_conv_w"]
BIG = ["w_in", "w_a_out", "w_b_out", "w_c_out", "w_up", "w_out", "w_down"]
COL_SHARDED = ["w_in", "w_a_out", "w_b_out", "w_c_out", "w_up"]


def _pack(arrs, rows):
    flat = jnp.concatenate([a.reshape(-1) for a in arrs])
    return jnp.pad(flat, (0, rows * 128 - flat.shape[0])).reshape(rows, 128)


def _unpack(packed, shapes):
    flat = packed.reshape(-1)
    out, pos = [], 0
    for s in shapes:
        n = 1
        for d in s:
            n *= d
        out.append(flat[pos:pos + n].reshape(s))
        pos += n
    return out


def _rows_for(shapes, mult):
    n = 0
    for s in shapes:
        k = 1
        for d in s:
            k *= d
        n += k
    rows = -(-n // 128)
    return -(-rows // mult) * mult


W_NAMES = ['ln1_g', 'w_in', 'b_gate', 'sg_ln_g', 'sg_ln_b', 'sg_w', 'sg_b', 'w_a_out', 'cv_w', 'cv_b', 'cv_ln_g', 'cv_ln_b',
           'w_b_out', 'q_norm_g', 'k_norm_g', 'w_c_out', 'w_out', 'ln2_g', 'w_up', 'ffn_conv_w', 'ffn_conv_b', 'w_down']


def _forward_layer(x, P, T, gather=None):
    D = x.shape[1]
    sv = {"x0": x}
    (h1,) = rowwise("rms_fwd", f_rms, [(x, D, 0)], [(P["ln1_g"], (1, D), None)], [(D, D, 0, bf16)], T, 256)
    z = mm_nn("in_proj", h1, P["w_in"], f32)
    sv["h1"], sv["z"] = h1, z
    (ya_in,) = rowwise("sgu_fwd", f_sgu, [(z, 512, 0), (z, 512, 1)],
                       [(P["sg_ln_g"], (1, 512), None), (P["sg_ln_b"], (1, 512), None),
                        (P["sg_w"], (8, CHUNK, CHUNK), None), (P["sg_b_t"], (CHUNK, 8), None)],
                       [(512, 512, 0, bf16)], T, CHUNK)
    ya = mm_nn("a_out", ya_in, P["w_a_out"], f32)
    (c0,) = rowwise("glu_fwd", f_glu, [(z, 512, 2), (z, 512, 3)], [], [(512, 512, 0, f32)], T, 256)
    c1 = conv31_fwd(c0, P["cv_w"], P["cv_b"], T)
    (c3,) = rowwise("lnsilu_fwd", f_lnsilu, [(c1, 512, 0)], [(P["cv_ln_g"], (1, 512), None), (P["cv_ln_b"], (1, 512), None)],
                    [(512, 512, 0, bf16)], T, 256)
    yb = mm_nn("b_out", c3, P["w_b_out"], f32)
    q8, kn, vb = rowwise("qkv_fwd", f_qkv, [(z, 512, 4), (z, 512, 5), (z, 512, 6)],
                         [(P["q_norm_g"], (1, HEAD_DIM), None), (P["k_norm_g"], (1, HEAD_DIM), None)],
                         [(512, 512, 0, bf16)] * 3, T, 256)
    o, tot, gathered = attn_fwd(q8, kn, vb, T, gather=gather)
    yc = mm_nn("c_out", o, P["w_c_out"], f32)
    (merged,) = rowwise("merge_fwd", f_merge,
                        [(z, 512, lambda c: 7 + c), (z, 512, lambda c: 9 + c), (z, 512, lambda c: 11 + c),
                         (ya, 512, lambda c: c), (yb, 512, lambda c: c), (yc, 512, lambda c: c)],
                        [(P["b_gate"][k], (1, 512), lambda c: c) for k in range(3)],
                        [(D, 512, lambda c: c, bf16)], T, 256, ncol=2)
    x1 = mm_nn("out_proj", merged, P["w_out"], f32, add=x)
    sv.update(ya_in=ya_in, ya=ya, c0=c0, c1=c1, c3=c3, yb=yb, q8=q8, kn=kn, vb=vb, o=o, tot=tot, yc=yc, merged=merged, x1=x1)
    (h2,) = rowwise("rms_fwd", f_rms, [(x1, D, 0)], [(P["ln2_g"], (1, D), None)], [(D, D, 0, bf16)], T, 256)
    up = mm_nn("up_proj", h2, P["w_up"], f32)
    act = ffn_act_fwd(up, P["ffn_conv_w"], P["ffn_conv_b"], T)
    x2 = mm_nn("down_proj", act, P["w_down"], f32, add=x1, tn=256)
    sv.update(h2=h2, up=up, act=act)
    return x2, sv, gathered


def _backward_layer(dx2, P, sv, T, scatter=None):
    D = dx2.shape[1]
    G = {}
    G["w_down"] = mm_tn("dw_down", sv["act"], dx2, tk=256)
    dact = mm_nt("d_act", dx2, P["w_down"], f32, tko=256, tc=D)
    dup, dcw0, dcw1, dcw2, G["ffn_conv_b"] = ffn_act_bwd(sv["up"], P["ffn_conv_w"], P["ffn_conv_b"], dact, T)
    G["ffn_conv_w"] = jnp.concatenate([dcw0, dcw1, dcw2], axis=0)
    G["w_up"] = mm_tn("dw_up", sv["h2"], dup, n_slab=P["w_up"].shape[1] // N_DEV)
    dh2 = mm_nt("d_h2", dup, P["w_up"], f32)
    dx1, G["ln2_g"] = rowwise_bwd("rms_bwd", f_rms, [(sv["x1"], D, 0)], [(P["ln2_g"], (1, D), None)], [(dh2, D, 0)],
                                  [f32], T, 256, adds=[(dx2, D, 0)])
    G["w_out"] = mm_tn("dw_out", sv["merged"], dx1)
    dmerged = mm_nt("d_merged", dx1, P["w_out"], f32)
    z = sv["z"]
    dg0, dg1, dg2, dya, dyb, dyc, db0, db1, db2 = rowwise_bwd(
        "merge_bwd", f_merge,
        [(z, 512, lambda c: 7 + c), (z, 512, lambda c: 9 + c), (z, 512, lambda c: 11 + c),
         (sv["ya"], 512, lambda c: c), (sv["yb"], 512, lambda c: c), (sv["yc"], 512, lambda c: c)],
        [(P["b_gate"][k], (1, 512), lambda c: c) for k in range(3)],
        [(dmerged, 512, lambda c: c)], [bf16] * 6, T, 256, ncol=2)
    G["b_gate"] = jnp.concatenate([db0, db1, db2], axis=0)
    G["w_c_out"] = mm_tn("dw_c_out", sv["o"], dyc, n_slab=CHUNK)
    do = mm_nt("d_o", dyc, P["w_c_out"], f32)
    dq8, dkn, dvb, received = attn_bwd(sv["q8"], sv["kn"], sv["vb"], do, sv["tot"], T, scatter=scatter)
    dzq, dzk, dzv, G["q_norm_g"], G["k_norm_g"] = rowwise_bwd(
        "qkv_bwd", f_qkv, [(z, 512, 4), (z, 512, 5), (z, 512, 6)],
        [(P["q_norm_g"], (1, HEAD_DIM), None), (P["k_norm_g"], (1, HEAD_DIM), None)],
        [(dq8, 512, 0), (dkn, 512, 0), (dvb, 512, 0)], [bf16] * 3, T, 256)
    G["w_b_out"] = mm_tn("dw_b_out", sv["c3"], dyb, n_slab=CHUNK)
    dc3 = mm_nt("d_c3", dyb, P["w_b_out"], f32)
    dc1, G["cv_ln_g"], G["cv_ln_b"] = rowwise_bwd(
        "lnsilu_bwd", f_lnsilu, [(sv["c1"], 512, 0)], [(P["cv_ln_g"], (1, 512), None), (P["cv_ln_b"], (1, 512), None)],
        [(dc3, 512, 0)], [f32], T, 256)
    dc0, G["cv_w"], G["cv_b"] = conv31_bwd(sv["c0"], P["cv_w"], dc1, T)
    dzp, dzgl = rowwise_bwd("glu_bwd", f_glu, [(z, 512, 2), (z, 512, 3)], [], [(dc0, 512, 0)], [bf16] * 2, T, 256)
    G["w_a_out"] = mm_tn("dw_a_out", sv["ya_in"], dya, n_slab=CHUNK)
    dya_in = mm_nt("d_ya_in", dya, P["w_a_out"], f32)
    dzu, dzv_a, G["sg_ln_g"], G["sg_ln_b"], G["sg_w"], dsbt = rowwise_bwd(
        "sgu_bwd", f_sgu, [(z, 512, 0), (z, 512, 1)],
        [(P["sg_ln_g"], (1, 512), None), (P["sg_ln_b"], (1, 512), None), (P["sg_w"], (8, CHUNK, CHUNK), None),
         (P["sg_b_t"], (CHUNK, 8), None)],
        [(dya_in, 512, 0)], [bf16] * 2, T, CHUNK)
    G["sg_b"] = dsbt.T
    dz = jnp.concatenate([dzu, dzv_a, dzp, dzgl, dzq, dzk, dzv, dg0, dg1, dg2], axis=1)
    G["w_in"] = mm_tn("dw_in", sv["h1"], dz, n_slab=P["w_in"].shape[1] // N_DEV)
    dh1 = mm_nt("d_h1", dz, P["w_in"], f32)
    dx0, G["ln1_g"] = rowwise_bwd("rms_bwd", f_rms, [(sv["x0"], D, 0)], [(P["ln1_g"], (1, D), None)], [(dh1, D, 0)],
                                  [f32], T, 256, adds=[(dx1, D, 0)])
    return dx0, G, received


def kernel(x, ln1_g, w_in, b_gate, sg_ln_g, sg_ln_b, sg_w, sg_b, w_a_out, cv_w, cv_b, cv_ln_g, cv_ln_b, w_b_out, q_norm_g, k_norm_g, w_c_out, w_out, ln2_g, w_up, ffn_conv_w, ffn_conv_b, w_down, loss_target, m_ln1_g, m_w_in, m_b_gate, m_sg_ln_g, m_sg_ln_b, m_sg_w, m_sg_b, m_w_a_out, m_cv_w, m_cv_b, m_cv_ln_g, m_cv_ln_b, m_w_b_out, m_q_norm_g, m_k_norm_g, m_w_c_out, m_w_out, m_ln2_g, m_w_up, m_ffn_conv_w, m_ffn_conv_b, m_w_down, v_ln1_g, v_w_in, v_b_gate, v_sg_ln_g, v_sg_ln_b, v_sg_w, v_sg_b, v_w_a_out, v_cv_w, v_cv_b, v_cv_ln_g, v_cv_ln_b, v_w_b_out, v_q_norm_g, v_k_norm_g, v_w_c_out, v_w_out, v_ln2_g, v_w_up, v_ffn_conv_w, v_ffn_conv_b, v_w_down):
    W = dict(ln1_g=ln1_g, w_in=w_in, b_gate=b_gate, sg_ln_g=sg_ln_g, sg_ln_b=sg_ln_b, sg_w=sg_w, sg_b=sg_b, w_a_out=w_a_out,
             cv_w=cv_w, cv_b=cv_b, cv_ln_g=cv_ln_g, cv_ln_b=cv_ln_b, w_b_out=w_b_out, q_norm_g=q_norm_g, k_norm_g=k_norm_g,
             w_c_out=w_c_out, w_out=w_out, ln2_g=ln2_g, w_up=w_up, ffn_conv_w=ffn_conv_w, ffn_conv_b=ffn_conv_b, w_down=w_down)
    M = dict(ln1_g=m_ln1_g, w_in=m_w_in, b_gate=m_b_gate, sg_ln_g=m_sg_ln_g, sg_ln_b=m_sg_ln_b, sg_w=m_sg_w, sg_b=m_sg_b,
             w_a_out=m_w_a_out, cv_w=m_cv_w, cv_b=m_cv_b, cv_ln_g=m_cv_ln_g, cv_ln_b=m_cv_ln_b, w_b_out=m_w_b_out,
             q_norm_g=m_q_norm_g, k_norm_g=m_k_norm_g, w_c_out=m_w_c_out, w_out=m_w_out, ln2_g=m_ln2_g, w_up=m_w_up,
             ffn_conv_w=m_ffn_conv_w, ffn_conv_b=m_ffn_conv_b, w_down=m_w_down)
    V = dict(ln1_g=v_ln1_g, w_in=v_w_in, b_gate=v_b_gate, sg_ln_g=v_sg_ln_g, sg_ln_b=v_sg_ln_b, sg_w=v_sg_w, sg_b=v_sg_b,
             w_a_out=v_w_a_out, cv_w=v_cv_w, cv_b=v_cv_b, cv_ln_g=v_cv_ln_g, cv_ln_b=v_cv_ln_b, w_b_out=v_w_b_out,
             q_norm_g=v_q_norm_g, k_norm_g=v_k_norm_g, w_c_out=v_w_c_out, w_out=v_w_out, ln2_g=v_ln2_g, w_up=v_w_up,
             ffn_conv_w=v_ffn_conv_w, ffn_conv_b=v_ffn_conv_b, w_down=v_w_down)
    T, D = x.shape[1], x.shape[2]
    L = DEPTH
    xs = x.reshape(T, D)
    target = loss_target.reshape(T, D)

    ss_shapes = [W[n].shape for n in SMALL_SHARDED]
    ss_rows = _rows_for(ss_shapes, 8)
    (ss_all,) = exchange("gather_small", [_pack([W[n] for n in SMALL_SHARDED], ss_rows)], ["gather"])
    full_small = {}
    pos = 0
    for n in SMALL_SHARDED:
        s = W[n].shape
        cnt = s[0] * s[1] * s[2]
        part = ss_all.reshape(N_DEV, -1)[:, pos:pos + cnt].reshape((N_DEV,) + s)
        full_small[n] = jnp.transpose(part, (1, 2, 0, 3)).reshape(s[0], s[1], N_DEV * s[2])
        pos += cnt

    params, saved = [], []
    cur = xs
    slabs = gather2("gather_big", [W[n][0].astype(bf16) for n in BIG])
    for l in range(L):
        P = {}
        for n, s in zip(BIG, slabs):
            P[n] = assemble("assemble_" + n, s) if n in COL_SHARDED else s.reshape(-1, s.shape[-1])
        for n in REPL:
            P[n] = W[n][l]
        for n in ("ln1_g", "sg_ln_g", "sg_ln_b", "cv_b", "cv_ln_g", "cv_ln_b", "q_norm_g", "k_norm_g", "ln2_g", "ffn_conv_b"):
            P[n] = P[n].reshape(1, -1)
        P["sg_b_t"] = P["sg_b"].T
        P["cv_w"] = full_small["cv_w"][l]
        P["b_gate"] = [full_small["b_gate"][l][k:k + 1] for k in range(3)]
        P["ffn_conv_w"] = [full_small["ffn_conv_w"][l][k:k + 1] for k in range(3)]
        nxt = [W[n][l + 1].astype(bf16) for n in BIG] if l + 1 < L else None
        cur, sv, slabs = _forward_layer(cur, P, T, gather=nxt)
        params.append(P)
        saved.append(sv)

    dy, loss_row = loss_head(cur, target, T)
    loss = lax.psum(loss_row[0, 0], ("x", "y", "c"))

    small_grads = [None] * L
    big_out = {n: [None] * L for n in BIG}
    dcur = dy
    send = None
    for l in reversed(range(L)):
        dcur, G, recv = _backward_layer(dcur, params[l], saved[l], T, scatter=send)
        if send is not None:
            for n, parts in zip(BIG, recv):
                big_out[n][l + 1] = adamw("adamw_" + n, W[n][l + 1], M[n][l + 1], V[n][l + 1], parts)
        small_grads[l] = G
        send = [G[n] if n in COL_SHARDED else G[n].reshape(N_DEV, -1, G[n].shape[-1]) for n in BIG]
    recv = exchange("scatter_big", send, ["scatter"] * len(BIG))
    for n, parts in zip(BIG, recv):
        big_out[n][0] = adamw("adamw_" + n, W[n][0], M[n][0], V[n][0], parts)

    repl_shapes = [W[n].shape for n in REPL]
    repl_rows = _rows_for(repl_shapes, 512)
    repl_grads = [jnp.stack([small_grads[l][n].reshape(W[n].shape[1:]) for l in range(L)]) for n in REPL]
    ss_grads = []
    for n in SMALL_SHARDED:
        g = jnp.stack([small_grads[l][n] for l in range(L)])
        s = W[n].shape
        ss_grads.append(jnp.transpose(g.reshape(s[0], s[1], N_DEV, s[2]), (2, 0, 1, 3)).reshape(N_DEV, -1))
    ss_send = jnp.concatenate(ss_grads, axis=1)
    ss_send = jnp.pad(ss_send, ((0, 0), (0, ss_rows * 128 - ss_send.shape[1]))).reshape(N_DEV, ss_rows, 128)
    repl_parts, ss_parts = exchange("exchange_small", [_pack(repl_grads, repl_rows), ss_send], ["gather", "scatter"])
    repl_res = adamw("adamw_repl", _pack([W[n] for n in REPL], repl_rows), _pack([M[n] for n in REPL], repl_rows),
                     _pack([V[n] for n in REPL], repl_rows), repl_parts)
    ss_res = adamw("adamw_small", _pack([W[n] for n in SMALL_SHARDED], ss_rows), _pack([M[n] for n in SMALL_SHARDED], ss_rows),
                   _pack([V[n] for n in SMALL_SHARDED], ss_rows), ss_parts)

    results = [{}, {}, {}, {}]
    for k in range(4):
        for n, a in zip(REPL, _unpack(repl_res[k], repl_shapes)):
            results[k][n] = a
        for n, a in zip(SMALL_SHARDED, _unpack(ss_res[k], ss_shapes)):
            results[k][n] = a
        for n in BIG:
            results[k][n] = jnp.stack([big_out[n][l][k] for l in range(L)])
    out = [loss, dcur.reshape(1, T, D)]
    for k in range(4):
        out += [results[k][n] for n in W_NAMES]
    return tuple(out)
```

```python
import functools

import jax
import jax.numpy as jnp
from jax import lax
from jax.experimental import pallas as pl
from jax.experimental.pallas import tpu as pltpu

f32 = jnp.float32
bf16 = jnp.bfloat16

EPS = 1e-6
N_DEV = 8
DEPTH = 4
CHUNK = 128
HEAD_DIM = 64
N_HEADS = 8
CV_KERNEL = 31
VMEM_LIMIT_BYTES = 56 * 2 ** 20

ADAM_LR = 0.001
ADAM_B1 = 0.9
ADAM_B2 = 0.999
ADAM_EPS = 1e-08
ADAM_WD = 0.01
ADAM_STEP = 10

MESH = pl.DeviceIdType.MESH


def _cparams(n_grid):
    return pltpu.CompilerParams(dimension_semantics=("arbitrary",) * n_grid, vmem_limit_bytes=VMEM_LIMIT_BYTES)


def exchange(name, arrays, modes):
    n = len(arrays)

    def body(*refs):
        copies = _direct_copies(refs[:n], refs[n:2 * n], modes, *refs[2 * n:])
        for cp in copies:
            cp.start()
        for cp in copies:
            cp.wait()

    return pl.pallas_call(
        body, name=name, out_shape=_exchange_out_shapes(arrays, modes),
        in_specs=[_ANY] * n, out_specs=[_ANY] * n, scratch_shapes=_exchange_sems(n),
    )(*arrays)


_ANY = pl.BlockSpec(memory_space=pl.ANY)


def _exchange_out_shapes(arrays, modes):
    return [jax.ShapeDtypeStruct((N_DEV,) + tuple(a.shape) if m == "gather" else tuple(a.shape), a.dtype)
            for a, m in zip(arrays, modes)]


def _exchange_sems(n):
    return [pltpu.SemaphoreType.DMA((n, N_DEV - 1)), pltpu.SemaphoreType.DMA((n, N_DEV - 1)), pltpu.SemaphoreType.DMA((n,))]


def _direct_copies(ins, outs, modes, send_sems, recv_sems, local_sems):
    x, y, c = lax.axis_index("x"), lax.axis_index("y"), lax.axis_index("c")
    me = 4 * x + 2 * y + c
    copies = []
    for k in range(len(ins)):
        src_mine = ins[k] if modes[k] == "gather" else ins[k].at[me]
        copies.append(pltpu.make_async_copy(src_mine, outs[k].at[me], local_sems.at[k]))
        for r in range(1, N_DEV):
            px = 1 - x if r & 4 else x
            py = 1 - y if r & 2 else y
            pc = 1 - c if r & 1 else c
            src = ins[k] if modes[k] == "gather" else ins[k].at[4 * px + 2 * py + pc]
            copies.append(pltpu.make_async_remote_copy(
                src_ref=src, dst_ref=outs[k].at[me], send_sem=send_sems.at[k, r - 1], recv_sem=recv_sems.at[k, r - 1],
                device_id=(px, py, pc), device_id_type=MESH))
    return copies


def _two_level_gather(ins, outs, send_sems, recv_sems, local_sems):
    x, y, c = lax.axis_index("x"), lax.axis_index("y"), lax.axis_index("c")
    me = 4 * x + 2 * y + c
    sibling = (x, y, 1 - c)
    chips = [(1 - x, y), (x, 1 - y), (1 - x, 1 - y)]

    def slot(px, py, pc):
        return 4 * px + 2 * py + pc

    def copy(k, sem, block, to, src=None):
        dst = outs[k].at[block]
        return pltpu.make_async_remote_copy(
            src_ref=dst if src is None else src, dst_ref=dst, send_sem=send_sems.at[k, sem], recv_sem=recv_sems.at[k, sem],
            device_id=to, device_id_type=MESH)

    def local(k):
        return pltpu.make_async_copy(ins[k], outs[k].at[me], local_sems.at[k])

    def first(k):
        return [copy(k, 0, me, sibling, src=ins[k])] + [copy(k, 1 + j, me, (*chip, c), src=ins[k]) for j, chip in enumerate(chips)]

    def passed_on(k, j):
        return copy(k, 4 + j, slot(*chips[j], c), sibling)

    def passed(k):
        return [passed_on(k, j) for j in range(3)]

    def start():
        for k in range(len(ins)):
            local(k).start()
            for cp in first(k):
                cp.start()

    def forward():
        for k in range(len(ins)):
            for j, chip in enumerate(chips):
                copy(k, 1 + j, slot(*chip, c), sibling).wait_recv()
                passed_on(k, j).start()

    def finish():
        for k in range(len(ins)):
            copy(k, 0, slot(x, y, 1 - c), sibling).wait_recv()
            for j, chip in enumerate(chips):
                copy(k, 4 + j, slot(*chip, 1 - c), sibling).wait_recv()
            for cp in first(k) + passed(k):
                cp.wait_send()
            local(k).wait()

    return start, forward, finish


def gather2(name, arrays):
    n = len(arrays)

    def body(*refs):
        start, forward, finish = _two_level_gather(refs[:n], refs[n:2 * n], *refs[2 * n:])
        start()
        forward()
        finish()

    return pl.pallas_call(
        body, name=name, out_shape=_exchange_out_shapes(arrays, ["gather"] * n),
        in_specs=[_ANY] * n, out_specs=[_ANY] * n, scratch_shapes=_exchange_sems(n),
    )(*arrays)


def assemble(name, slabs):
    _, K, Ns = slabs.shape
    g = N_DEV if Ns % 128 == 0 else 2

    def body(w_ref, o_ref):
        o_ref[...] = jnp.concatenate([w_ref[s] for s in range(g)], axis=1)

    return pl.pallas_call(
        body, name=name, grid=(N_DEV // g,),
        in_specs=[pl.BlockSpec((g, K, Ns), lambda m: (m, 0, 0))],
        out_specs=pl.BlockSpec((K, g * Ns), lambda m: (0, m)),
        out_shape=jax.ShapeDtypeStruct((K, N_DEV * Ns), slabs.dtype),
        compiler_params=_cparams(1),
    )(slabs)


def mm_nn(name, a, b, out_dtype, add=None, tm=1024, tn=512):
    M, K = a.shape
    N = b.shape[1]
    tm, tn = min(tm, M), min(tn, N)
    has_add = add is not None

    def body(a_ref, b_ref, *rest):
        o_ref = rest[-1]
        acc = jnp.dot(a_ref[...].astype(bf16), b_ref[...].astype(bf16), preferred_element_type=f32)
        if has_add:
            acc = acc + rest[0][...]
        o_ref[...] = acc.astype(o_ref.dtype)

    in_specs = [pl.BlockSpec((tm, K), lambda i, j: (i, 0)), pl.BlockSpec((K, tn), lambda i, j: (0, j))]
    ops = [a, b]
    if has_add:
        in_specs.append(pl.BlockSpec((tm, tn), lambda i, j: (i, j)))
        ops.append(add)
    return pl.pallas_call(
        body, name=name, grid=(M // tm, N // tn), in_specs=in_specs,
        out_specs=pl.BlockSpec((tm, tn), lambda i, j: (i, j)),
        out_shape=jax.ShapeDtypeStruct((M, N), out_dtype), compiler_params=_cparams(2),
    )(*ops)


def mm_nt(name, a, b, out_dtype, tm=1024, tko=None, tc=512):
    M, C = a.shape
    Ko = b.shape[0]
    tm, tc = min(tm, M), min(tc, C)
    tko = Ko if tko is None else tko
    nc = C // tc

    def body(a_ref, b_ref, o_ref, acc_ref):
        c = pl.program_id(2)
        part = lax.dot_general(a_ref[...].astype(bf16), b_ref[...].astype(bf16), (((1,), (1,)), ((), ())),
                               preferred_element_type=f32)

        @pl.when(c == 0)
        def _():
            acc_ref[...] = part

        @pl.when(c > 0)
        def _():
            acc_ref[...] += part

        @pl.when(c == nc - 1)
        def _():
            o_ref[...] = acc_ref[...].astype(o_ref.dtype)

    return pl.pallas_call(
        body, name=name, grid=(M // tm, Ko // tko, nc),
        in_specs=[pl.BlockSpec((tm, tc), lambda i, j, c: (i, c)), pl.BlockSpec((tko, tc), lambda i, j, c: (j, c))],
        out_specs=pl.BlockSpec((tm, tko), lambda i, j, c: (i, j)),
        out_shape=jax.ShapeDtypeStruct((M, Ko), out_dtype),
        scratch_shapes=[pltpu.VMEM((tm, tko), f32)], compiler_params=_cparams(3),
    )(a, b)


def mm_tn(name, a, b, n_slab=None, tk=512, tn=512):
    T, K = a.shape
    N = b.shape[1]
    tk = min(tk, K)
    if n_slab is None:
        tn = min(tn, N)

        def body(a_ref, b_ref, o_ref):
            o_ref[...] = lax.dot_general(a_ref[...].astype(bf16), b_ref[...].astype(bf16), (((0,), (0,)), ((), ())),
                                         preferred_element_type=f32).astype(o_ref.dtype)

        return pl.pallas_call(
            body, name=name, grid=(N // tn, K // tk),
            in_specs=[pl.BlockSpec((T, tk), lambda j, i: (0, i)), pl.BlockSpec((T, tn), lambda j, i: (0, j))],
            out_specs=pl.BlockSpec((tk, tn), lambda j, i: (i, j)),
            out_shape=jax.ShapeDtypeStruct((K, N), bf16), compiler_params=_cparams(2),
        )(a, b)

    Ns = n_slab
    g = N_DEV if Ns % 128 == 0 else 2
    tn = g * Ns

    def body(a_ref, b_ref, o_ref):
        val = lax.dot_general(a_ref[...].astype(bf16), b_ref[...].astype(bf16), (((0,), (0,)), ((), ())),
                              preferred_element_type=f32)
        for s in range(g):
            o_ref[s] = val[:, s * Ns:(s + 1) * Ns].astype(o_ref.dtype)

    return pl.pallas_call(
        body, name=name, grid=(K // tk, N_DEV // g),
        in_specs=[pl.BlockSpec((T, tk), lambda i, j: (0, i)), pl.BlockSpec((T, tn), lambda i, j: (0, j))],
        out_specs=pl.BlockSpec((g, tk, Ns), lambda i, j: (j, i, 0)),
        out_shape=jax.ShapeDtypeStruct((N_DEV, K, Ns), bf16), compiler_params=_cparams(2),
    )(a, b)


def _tile_spec(w, tm, cb):
    if callable(cb):
        return pl.BlockSpec((tm, w), lambda c, i: (i, cb(c)))
    return pl.BlockSpec((tm, w), lambda c, i: (i, cb))


def _param_spec(block, cb):
    nd = len(block)
    if cb is None:
        return pl.BlockSpec(block, lambda c, i: (0,) * nd)
    return pl.BlockSpec(block, lambda c, i: (0,) * (nd - 1) + (cb(c),))


def rowwise(name, fn, tiled, params, outs, T, tm, ncol=1):
    n_in = len(tiled) + len(params)
    n_t = len(tiled)

    def body(*refs):
        ins = [r[...].astype(f32) for r in refs[:n_t]] + [r[...] for r in refs[n_t:n_in]]
        res = fn(*ins)
        for r, o in zip(refs[n_in:], res):
            r[...] = o.astype(r.dtype)

    return pl.pallas_call(
        body, name=name, grid=(ncol, T // tm),
        in_specs=[_tile_spec(w, tm, cb) for _, w, cb in tiled] + [_param_spec(blk, cb) for _, blk, cb in params],
        out_specs=[_tile_spec(w, tm, cb) for _, w, cb, _ in outs],
        out_shape=[jax.ShapeDtypeStruct((T, cols), dt) for cols, _, _, dt in outs],
        compiler_params=_cparams(2),
    )(*[a for a, _, _ in tiled], *[a for a, _, _ in params])


def rowwise_bwd(name, fn, tiled, params, cts, grads, T, tm, ncol=1, adds=None):
    n_t, n_p, n_c = len(tiled), len(params), len(cts)
    adds = adds or [None] * n_t
    add_list = [(k, a) for k, a in enumerate(adds) if a is not None]
    want = [k for k, g in enumerate(grads) if g is not None]
    n_a = len(add_list)

    def body(*refs):
        pos = 0
        t_refs = refs[pos:pos + n_t]; pos += n_t
        p_refs = refs[pos:pos + n_p]; pos += n_p
        c_refs = refs[pos:pos + n_c]; pos += n_c
        a_refs = refs[pos:pos + n_a]; pos += n_a
        g_refs = refs[pos:pos + len(want)]; pos += len(want)
        pg_refs = refs[pos:pos + n_p]
        primals = [r[...].astype(f32) for r in t_refs] + [r[...] for r in p_refs]
        _, vjp = jax.vjp(fn, *primals)
        g = vjp(tuple(r[...].astype(f32) for r in c_refs))
        add_of = {k: a_refs[n][...] for n, (k, _) in enumerate(add_list)}
        for n, k in enumerate(want):
            val = g[k]
            if k in add_of:
                val = val + add_of[k]
            g_refs[n][...] = val.astype(g_refs[n].dtype)
        i = pl.program_id(1)
        for k in range(n_p):
            @pl.when(i == 0)
            def _(k=k):
                pg_refs[k][...] = g[n_t + k]

            @pl.when(i > 0)
            def _(k=k):
                pg_refs[k][...] += g[n_t + k]

    in_specs = ([_tile_spec(w, tm, cb) for _, w, cb in tiled] + [_param_spec(blk, cb) for _, blk, cb in params]
                + [_tile_spec(w, tm, cb) for _, w, cb in cts] + [_tile_spec(w, tm, cb) for _, (_, w, cb) in add_list])
    ops = ([a for a, _, _ in tiled] + [a for a, _, _ in params] + [a for a, _, _ in cts]
           + [a for _, (a, _, _) in add_list])
    out_specs, out_shape = [], []
    for k in want:
        w = tiled[k][1]
        out_specs.append(_tile_spec(w, tm, lambda c: c))
        out_shape.append(jax.ShapeDtypeStruct((T, ncol * w), grads[k]))
    for a, blk, cb in params:
        out_specs.append(_param_spec(blk, cb))
        out_shape.append(jax.ShapeDtypeStruct(a.shape, f32))
    return pl.pallas_call(
        body, name=name, grid=(ncol, T // tm), in_specs=in_specs, out_specs=out_specs, out_shape=out_shape,
        compiler_params=_cparams(2),
    )(*ops)


@jax.custom_vjp
def _bdot(a, b):
    return jnp.dot(a.astype(bf16), b.astype(bf16), preferred_element_type=f32)


def _bdot_fwd(a, b):
    return _bdot(a, b), (a, b)


def _bdot_bwd(res, ct):
    a, b = res
    ctb = ct.astype(bf16)
    da = lax.dot_general(ctb, b.astype(bf16), (((1,), (1,)), ((), ())), preferred_element_type=f32)
    db = lax.dot_general(a.astype(bf16), ctb, (((0,), (0,)), ((), ())), preferred_element_type=f32)
    return da, db


_bdot.defvjp(_bdot_fwd, _bdot_bwd)


def _layer_norm(x, g, b):
    mu = jnp.mean(x, axis=-1, keepdims=True)
    xc = x - mu
    y = xc * lax.rsqrt(jnp.mean(xc * xc, axis=-1, keepdims=True) + EPS)
    return y * g + b


def f_rms(x, g):
    y = x * lax.rsqrt(jnp.mean(x * x, axis=-1, keepdims=True) + EPS)
    return (y * g,)


def f_sgu(zu, zv, ln_g, ln_b, wm, sgb_t):
    u = jax.nn.gelu(zu)
    vn = _layer_norm(jax.nn.gelu(zv), ln_g, ln_b)
    row = lax.broadcasted_iota(jnp.int32, (CHUNK, CHUNK), 0)
    col = lax.broadcasted_iota(jnp.int32, (CHUNK, CHUNK), 1)
    tril = col <= row
    low = col < HEAD_DIM
    parts = []
    for p in range(4):
        vp = vn[:, CHUNK * p:CHUNK * (p + 1)]
        w0 = jnp.where(tril, wm[2 * p], 0.0)
        w1 = jnp.where(tril, wm[2 * p + 1], 0.0)
        parts.append(_bdot(w0, jnp.where(low, vp, 0.0)) + _bdot(w1, jnp.where(low, 0.0, vp)))
    s = jnp.concatenate(parts, axis=1)
    lane_g = lax.shift_right_logical(lax.broadcasted_iota(jnp.int32, s.shape, 1), 6)
    bias = jnp.zeros_like(s)
    for g in range(8):
        bias = jnp.where(lane_g == g, sgb_t[:, g:g + 1], bias)
    return (u * (s + bias),)


def f_glu(p, gl):
    return (p * jax.nn.sigmoid(gl),)


def f_lnsilu(c1, g, b):
    return (jax.nn.silu(_layer_norm(c1, g, b)),)


def _head_norm(x, g64):
    g = jnp.concatenate([g64] * N_HEADS, axis=1)
    lane_h = lax.shift_right_logical(lax.broadcasted_iota(jnp.int32, x.shape, 1), 6)
    x2 = x * x
    r = jnp.zeros_like(x)
    for h in range(N_HEADS):
        mh = lane_h == h
        ms = jnp.sum(jnp.where(mh, x2, 0.0), axis=-1, keepdims=True) * (1.0 / HEAD_DIM)
        r = jnp.where(mh, lax.rsqrt(ms + EPS), r)
    return (x * r) * g


def f_qkv(zq, zk, zv, qg, kg):
    return (_head_norm(zq, qg) * 0.125, _head_norm(zk, kg), zv)


def f_merge(g0, g1, g2, ya, yb, yc, b0, b1, b2):
    return (jax.nn.sigmoid(g0 + b0) * ya + jax.nn.sigmoid(g1 + b1) * yb + jax.nn.sigmoid(g2 + b2) * yc,)


HALO = 32


def conv31_fwd(x, w, b, T, tt=256):
    C = x.shape[1]
    r = tt // HALO

    def body(x_ref, h_ref, w_ref, b_ref, y_ref, buf):
        i = pl.program_id(0)
        halo = h_ref[...]
        buf[0:HALO, :] = jnp.where(i > 0, halo, jnp.zeros_like(halo))
        buf[HALO:HALO + tt, :] = x_ref[...]
        acc = jnp.zeros((tt, C), f32) + b_ref[...]
        for k in range(CV_KERNEL):
            acc = acc + w_ref[k:k + 1, :] * buf[pl.ds(HALO - (CV_KERNEL - 1) + k, tt), :]
        y_ref[...] = acc

    return pl.pallas_call(
        body, name="conv31_fwd", grid=(T // tt,),
        in_specs=[pl.BlockSpec((tt, C), lambda i: (i, 0)),
                  pl.BlockSpec((HALO, C), lambda i: (jnp.maximum(i * r - 1, 0), 0)),
                  pl.BlockSpec((CV_KERNEL, C), lambda i: (0, 0)), pl.BlockSpec((1, C), lambda i: (0, 0))],
        out_specs=pl.BlockSpec((tt, C), lambda i: (i, 0)),
        out_shape=jax.ShapeDtypeStruct((T, C), f32),
        scratch_shapes=[pltpu.VMEM((HALO + tt, C), f32)], compiler_params=_cparams(1),
    )(x, x, w, b)


def conv31_bwd(x, w, dy, T, tt=256):
    C = x.shape[1]
    r = tt // HALO
    n = T // tt

    def body(x_ref, h_ref, w_ref, dy_ref, dyn_ref, dx_ref, dw_ref, db_ref, xbuf, dbuf):
        i = pl.program_id(0)
        halo = h_ref[...]
        xbuf[0:HALO, :] = jnp.where(i > 0, halo, jnp.zeros_like(halo))
        xbuf[HALO:HALO + tt, :] = x_ref[...]
        nxt = dyn_ref[...]
        dy = dy_ref[...]
        dbuf[0:tt, :] = dy
        dbuf[tt:tt + HALO, :] = jnp.where(i < n - 1, nxt, jnp.zeros_like(nxt))

        @pl.when(i == 0)
        def _():
            dw_ref[...] = jnp.zeros_like(dw_ref)
            db_ref[...] = jnp.zeros_like(db_ref)

        acc = jnp.zeros((tt, C), f32)
        for k in range(CV_KERNEL):
            acc = acc + w_ref[k:k + 1, :] * dbuf[pl.ds(CV_KERNEL - 1 - k, tt), :]
            xs = xbuf[pl.ds(HALO - (CV_KERNEL - 1) + k, tt), :]
            dw_ref[k:k + 1, :] += jnp.sum(dy * xs, axis=0, keepdims=True)
        dx_ref[...] = acc
        db_ref[...] += jnp.sum(dy, axis=0, keepdims=True)

    return pl.pallas_call(
        body, name="conv31_bwd", grid=(n,),
        in_specs=[pl.BlockSpec((tt, C), lambda i: (i, 0)),
                  pl.BlockSpec((HALO, C), lambda i: (jnp.maximum(i * r - 1, 0), 0)),
                  pl.BlockSpec((CV_KERNEL, C), lambda i: (0, 0)),
                  pl.BlockSpec((tt, C), lambda i: (i, 0)),
                  pl.BlockSpec((HALO, C), lambda i: (jnp.minimum((i + 1) * r, n * r - 1), 0))],
        out_specs=[pl.BlockSpec((tt, C), lambda i: (i, 0)), pl.BlockSpec((CV_KERNEL, C), lambda i: (0, 0)),
                   pl.BlockSpec((1, C), lambda i: (0, 0))],
        out_shape=[jax.ShapeDtypeStruct((T, C), f32), jax.ShapeDtypeStruct((CV_KERNEL, C), f32),
                   jax.ShapeDtypeStruct((1, C), f32)],
        scratch_shapes=[pltpu.VMEM((HALO + tt, C), f32), pltpu.VMEM((HALO + tt, C), f32)], compiler_params=_cparams(1),
    )(x, x, w, dy, dy)


FFN_TC = 128


def _shift_down(x, s, row):
    return jnp.where(row >= s, pltpu.roll(x, s, 0), 0.0)


def _shift_up(x, s, row, T):
    return jnp.where(row < T - s, pltpu.roll(x, T - s, 0), 0.0)


def _conv3(x, w0, w1, w2, b, row):
    return w0 * _shift_down(x, 2, row) + w1 * _shift_down(x, 1, row) + w2 * x + b


def ffn_act_fwd(up, cw, cb, T):
    F = up.shape[1] // 2
    nj = F // FFN_TC

    def body(g_ref, v_ref, g0, g1, g2, gb, v0, v1, v2, vb, o_ref):
        row = lax.broadcasted_iota(jnp.int32, (T, FFN_TC), 0)
        gc = _conv3(g_ref[...], g0[...], g1[...], g2[...], gb[...], row)
        vc = _conv3(v_ref[...], v0[...], v1[...], v2[...], vb[...], row)
        o_ref[...] = (jax.nn.silu(gc) * vc).astype(o_ref.dtype)

    gspec = pl.BlockSpec((T, FFN_TC), lambda j: (0, j))
    vspec = pl.BlockSpec((T, FFN_TC), lambda j: (0, j + nj))
    pg = pl.BlockSpec((1, FFN_TC), lambda j: (0, j))
    pv = pl.BlockSpec((1, FFN_TC), lambda j: (0, j + nj))
    return pl.pallas_call(
        body, name="ffn_act_fwd", grid=(nj,),
        in_specs=[gspec, vspec, pg, pg, pg, pg, pv, pv, pv, pv], out_specs=gspec,
        out_shape=jax.ShapeDtypeStruct((T, F), bf16), compiler_params=_cparams(1),
    )(up, up, cw[0], cw[1], cw[2], cb, cw[0], cw[1], cw[2], cb)


def ffn_act_bwd(up, cw, cb, dact, T):
    F = up.shape[1] // 2
    nj = F // FFN_TC

    def side_grads(dc, x, w0, w1, w2, row, dup_ref, p_refs):
        dup = w2 * dc + w1 * _shift_up(dc, 1, row, T) + w0 * _shift_up(dc, 2, row, T)
        dup_ref[...] = dup.astype(dup_ref.dtype)
        p_refs[0][...] = jnp.sum(dc * _shift_down(x, 2, row), axis=0, keepdims=True)
        p_refs[1][...] = jnp.sum(dc * _shift_down(x, 1, row), axis=0, keepdims=True)
        p_refs[2][...] = jnp.sum(dc * x, axis=0, keepdims=True)
        p_refs[3][...] = jnp.sum(dc, axis=0, keepdims=True)

    def body(g_ref, v_ref, g0, g1, g2, gb, v0, v1, v2, vb, d_ref, dupg_ref, dupv_ref, *p_refs):
        row = lax.broadcasted_iota(jnp.int32, (T, FFN_TC), 0)
        xg, xv = g_ref[...], v_ref[...]
        gc = _conv3(xg, g0[...], g1[...], g2[...], gb[...], row)
        vc = _conv3(xv, v0[...], v1[...], v2[...], vb[...], row)
        d = d_ref[...]
        sg = jax.nn.sigmoid(gc)
        side_grads(d * vc * (sg * (1.0 + gc * (1.0 - sg))), xg, g0[...], g1[...], g2[...], row, dupg_ref, p_refs[:4])
        side_grads(d * (gc * sg), xv, v0[...], v1[...], v2[...], row, dupv_ref, p_refs[4:])

    gspec = pl.BlockSpec((T, FFN_TC), lambda j: (0, j))
    vspec = pl.BlockSpec((T, FFN_TC), lambda j: (0, j + nj))
    pg = pl.BlockSpec((1, FFN_TC), lambda j: (0, j))
    pv = pl.BlockSpec((1, FFN_TC), lambda j: (0, j + nj))
    res = pl.pallas_call(
        body, name="ffn_act_bwd", grid=(nj,),
        in_specs=[gspec, vspec, pg, pg, pg, pg, pv, pv, pv, pv, gspec],
        out_specs=[gspec, gspec] + [pg] * 8,
        out_shape=[jax.ShapeDtypeStruct((T, F), bf16)] * 2 + [jax.ShapeDtypeStruct((1, F), f32)] * 8,
        compiler_params=_cparams(1),
    )(up, up, cw[0], cw[1], cw[2], cb, cw[0], cw[1], cw[2], cb, dact)
    dup = jnp.concatenate([res[0], res[1]], axis=1)
    return (dup,) + tuple(jnp.concatenate([res[2 + k], res[6 + k]], axis=1) for k in range(4))


BQ = 128
BK = 256
NT_DIMS = (((1,), (1,)), ((), ()))
TN_DIMS = (((0,), (0,)), ((), ()))


def _split_dot(x, u):
    x1 = x.astype(bf16)
    x2 = (x - x1.astype(f32)).astype(bf16)
    n = x.shape[0]
    y = jnp.dot(jnp.concatenate([x1, x2], axis=0), u, preferred_element_type=f32)
    return y[0:n] + y[n:2 * n]


def _log_sigmoids(z):
    sp = jnp.log(1.0 + jnp.exp(-jnp.abs(z)))
    lsp = jnp.minimum(z, 0.0) - sp
    return lsp, lsp - z


def _stack_heads(x):
    head1 = lax.broadcasted_iota(jnp.int32, x.shape, 1) >= HEAD_DIM
    zero = jnp.zeros_like(x)
    return jnp.concatenate([jnp.where(head1, zero, x), jnp.where(head1, x, zero)], axis=0)


def _unstack_heads(y):
    head1 = lax.broadcasted_iota(jnp.int32, (BQ, y.shape[1]), 1) >= HEAD_DIM
    return jnp.where(head1, y[BQ:2 * BQ], y[0:BQ])


def _attn_masks():
    row = lax.broadcasted_iota(jnp.int32, (2 * BQ, BK), 0)
    col = lax.broadcasted_iota(jnp.int32, (2 * BQ, BK), 1)
    ur = lax.broadcasted_iota(jnp.int32, (BK, BK), 0)
    uc = lax.broadcasted_iota(jnp.int32, (BK, BK), 1)
    return (row & (BQ - 1)) - col, (ur > uc).astype(bf16), (ur < uc).astype(bf16)


def attn_fwd(q, k, v, T, gather=None):
    nq = T // BQ
    n_g = 0 if gather is None else len(gather)

    def body(*refs):
        q_ref, k_ref, v_ref = refs[:3]
        o_ref, tot_ref = refs[3 + n_g:5 + n_g]
        p, i = pl.program_id(0), pl.program_id(1)
        if n_g:
            start, forward, finish = _two_level_gather(refs[3:3 + n_g], refs[5 + n_g:5 + 2 * n_g], *refs[5 + 2 * n_g:])
            pl.when(jnp.logical_and(p == 0, i == 0))(start)
        qs = _stack_heads(q_ref[...])
        diff, u_after, _ = _attn_masks()

        def step(t, carry):
            acc, c = carry
            jb = i // 2 - t
            ks = pl.multiple_of(jb * BK, BK)
            kb = k_ref[pl.ds(ks, BK), :]
            vb = v_ref[pl.ds(ks, BK), :]
            m = diff > jb * BK - i * BQ
            z = lax.dot_general(qs, kb, NT_DIMS, preferred_element_type=f32)
            lsp, lsn = _log_sigmoids(z)
            lm = jnp.where(m, lsn, 0.0)
            a = jnp.where(m, jnp.exp(lsp + _split_dot(lm, u_after)), 0.0)
            acc = acc + jnp.exp(c) * jnp.dot(a.astype(bf16), vb, preferred_element_type=f32)
            return acc, c + jnp.sum(lm, axis=-1, keepdims=True)

        acc, c = lax.fori_loop(0, i // 2 + 1, step, (jnp.zeros((2 * BQ, 128), f32), jnp.zeros((2 * BQ, 1), f32)))
        o_ref[...] = _unstack_heads(acc).astype(o_ref.dtype)
        tot_ref[...] = _unstack_heads(jnp.broadcast_to(c, (2 * BQ, 128)))
        if n_g:
            @pl.when(jnp.logical_and(p == 3, i == nq - 1))
            def _():
                forward()
                finish()

    blk = pl.BlockSpec((BQ, 128), lambda p, i: (i, p))
    full = pl.BlockSpec((T, 128), lambda p, i: (0, p))
    g_list = [] if gather is None else list(gather)
    res = pl.pallas_call(
        body, name="attn_fwd_gather" if n_g else "attn_fwd", grid=(4, nq),
        in_specs=[blk, full, full] + [_ANY] * n_g, out_specs=[blk, blk] + [_ANY] * n_g,
        out_shape=[jax.ShapeDtypeStruct((T, 512), bf16), jax.ShapeDtypeStruct((T, 512), f32)]
        + _exchange_out_shapes(g_list, ["gather"] * n_g),
        scratch_shapes=_exchange_sems(n_g) if n_g else [], compiler_params=_cparams(2),
    )(q, k, v, *g_list)
    return res[0], res[1], list(res[2:])


def attn_bwd(q, k, v, do, tot, T, scatter=None):
    nq = T // BQ
    n_s = 0 if scatter is None else len(scatter[0])
    modes = [] if scatter is None else list(scatter[1])

    def body(*refs):
        q_ref, k_ref, v_ref, do_ref, tot_ref = refs[:5]
        dq_ref, dk_ref, dv_ref = refs[5 + n_s:8 + n_s]
        p, i = pl.program_id(0), pl.program_id(1)
        if n_s:
            def copies():
                return _direct_copies(refs[5:5 + n_s], refs[8 + n_s:8 + 2 * n_s], modes, *refs[8 + 2 * n_s:])

            @pl.when(jnp.logical_and(p == 0, i == 0))
            def _():
                for cp in copies():
                    cp.start()

        @pl.when(i == 0)
        def _():
            dk_ref[...] = jnp.zeros_like(dk_ref)
            dv_ref[...] = jnp.zeros_like(dv_ref)

        qs = _stack_heads(q_ref[...])
        dos = _stack_heads(do_ref[...].astype(bf16))
        totv = tot_ref[...]
        tots = jnp.concatenate([totv[:, 0:1], totv[:, HEAD_DIM:HEAD_DIM + 1]], axis=0)
        diff, u_after, u_before = _attn_masks()

        def step(jb, carry):
            dq, cl, cg = carry
            ks = pl.multiple_of(jb * BK, BK)
            kb = k_ref[pl.ds(ks, BK), :]
            vb = v_ref[pl.ds(ks, BK), :]
            m = diff > jb * BK - i * BQ
            z = lax.dot_general(qs, kb, NT_DIMS, preferred_element_type=f32)
            lsp, lsn = _log_sigmoids(z)
            lm = jnp.where(m, lsn, 0.0)
            a = jnp.where(m, jnp.exp(lsp + _split_dot(lm, u_after)), 0.0)
            g = a * lax.dot_general(dos, vb, NT_DIMS, preferred_element_type=f32)
            beta = jnp.exp(lsp)
            aa = jnp.where(m, g * jnp.exp(lsn) - _split_dot(g, u_before) * beta, 0.0)
            bb = jnp.where(m, beta, 0.0)
            cl = cl + jnp.sum(lm, axis=-1, keepdims=True)
            f = jnp.exp(tots - cl)
            dz = (f * aa - cg * bb).astype(bf16)
            cg = cg + f * jnp.sum(g, axis=-1, keepdims=True)
            dq = dq + jnp.dot(dz, kb, preferred_element_type=f32)
            dk_ref[pl.ds(ks, BK), :] += lax.dot_general(dz, qs, TN_DIMS, preferred_element_type=f32)
            dv_ref[pl.ds(ks, BK), :] += lax.dot_general((f * a).astype(bf16), dos, TN_DIMS, preferred_element_type=f32)
            return dq, cl, cg

        zc = jnp.zeros((2 * BQ, 1), f32)
        dq, _, _ = lax.fori_loop(0, i // 2 + 1, step, (jnp.zeros((2 * BQ, 128), f32), zc, zc))
        dq_ref[...] = _unstack_heads(dq)
        if n_s:
            @pl.when(jnp.logical_and(p == 3, i == nq - 1))
            def _():
                for cp in copies():
                    cp.wait()

    blk = pl.BlockSpec((BQ, 128), lambda p, i: (i, p))
    full = pl.BlockSpec((T, 128), lambda p, i: (0, p))
    shp = jax.ShapeDtypeStruct((T, 512), f32)
    s_list = [] if scatter is None else list(scatter[0])
    res = pl.pallas_call(
        body, name="attn_bwd_scatter" if n_s else "attn_bwd", grid=(4, nq),
        in_specs=[blk, full, full, blk, blk] + [_ANY] * n_s, out_specs=[blk, full, full] + [_ANY] * n_s,
        out_shape=[shp, shp, shp] + _exchange_out_shapes(s_list, modes),
        scratch_shapes=_exchange_sems(n_s) if n_s else [], compiler_params=_cparams(2),
    )(q, k, v, do, tot, *s_list)
    return res[0], res[1], res[2], list(res[3:])


def loss_head(y, target, T, tm=256):
    D = y.shape[1]

    def body(y_ref, t_ref, dy_ref, l_ref):
        i = pl.program_id(0)
        err = y_ref[...] - t_ref[...]
        dy_ref[...] = err * (1.0 / D)
        part = 0.5 * jnp.sum(jnp.sum(err * err, axis=-1, keepdims=True) * (1.0 / D), axis=0, keepdims=True)

        @pl.when(i == 0)
        def _():
            l_ref[...] = jnp.zeros_like(l_ref)

        l_ref[...] += jnp.broadcast_to(part, l_ref.shape)

    spec = pl.BlockSpec((tm, D), lambda i: (i, 0))
    return pl.pallas_call(
        body, name="loss_head", grid=(T // tm,), in_specs=[spec, spec],
        out_specs=[spec, pl.BlockSpec((1, 128), lambda i: (0, 0))],
        out_shape=[jax.ShapeDtypeStruct((T, D), f32), jax.ShapeDtypeStruct((1, 128), f32)],
        compiler_params=_cparams(1),
    )(y, target)


def _row_tile(rows, cols):
    best = None
    for t in range(16, rows + 1, 16):
        if rows % t == 0 and t * cols * 4 <= 512 * 1024:
            best = t
    return best if best is not None else rows


def adamw(name, w, m, v, parts, layer, prev=None):
    L, R, C = w.shape
    tr = _row_tile(R, C)

    def body(w_ref, m_ref, v_ref, p_ref, *rest):
        g_ref, d_ref, nm_ref, nv_ref = rest[-4:]
        g = p_ref[0].astype(f32)
        for s in range(1, N_DEV):
            g = g + p_ref[s].astype(f32)
        wv = w_ref[...]
        mn = ADAM_B1 * m_ref[...] + (1.0 - ADAM_B1) * g
        vn = ADAM_B2 * v_ref[...] + (1.0 - ADAM_B2) * jnp.square(g)
        m_hat = mn / (1.0 - ADAM_B1 ** ADAM_STEP)
        v_hat = vn / (1.0 - ADAM_B2 ** ADAM_STEP)
        g_ref[...] = g
        d_ref[...] = -ADAM_LR * (m_hat / (jnp.sqrt(v_hat) + ADAM_EPS) + ADAM_WD * wv)
        nm_ref[...] = mn
        nv_ref[...] = vn

    spec = pl.BlockSpec((None, tr, C), lambda i: (layer, i, 0))
    shp = jax.ShapeDtypeStruct((L, R, C), f32)
    n_prev = 0 if prev is None else 4
    return pl.pallas_call(
        body, name=name, grid=(R // tr,),
        in_specs=[spec, spec, spec, pl.BlockSpec((N_DEV, tr, C), lambda i: (0, i, 0))] + [_ANY] * n_prev,
        out_specs=[spec] * 4, out_shape=[shp] * 4, input_output_aliases={4 + k: k for k in range(n_prev)},
        compiler_params=_cparams(1),
    )(w, m, v, parts, *(prev or ()))


REPL = ["ln1_g", "sg_ln_g", "sg_ln_b", "sg_w", "sg_b", "cv_b", "cv_ln_g", "cv_ln_b", "q_norm_g", "k_norm_g", "ln2_g",
        "ffn_conv_b"]
SMALL_SHARDED = ["b_gate", "cv_w", "ffn_conv_w"]
BIG = ["w_in", "w_a_out", "w_b_out", "w_c_out", "w_up", "w_out", "w_down"]
COL_SHARDED = ["w_in", "w_a_out", "w_b_out", "w_c_out", "w_up"]


def _pack(arrs, rows):
    flat = jnp.concatenate([a.reshape(-1) for a in arrs])
    return jnp.pad(flat, (0, rows * 128 - flat.shape[0])).reshape(rows, 128)


def _pack_layers(arrs, rows):
    flat = jnp.concatenate([a.reshape(a.shape[0], -1) for a in arrs], axis=1)
    return jnp.pad(flat, ((0, 0), (0, rows * 128 - flat.shape[1]))).reshape(flat.shape[0], rows, 128)


def _unpack_layers(packed, shapes):
    flat = packed.reshape(packed.shape[0], -1)
    out, pos = [], 0
    for s in shapes:
        n = 1
        for d in s[1:]:
            n *= d
        out.append(flat[:, pos:pos + n].reshape(s))
        pos += n
    return out


def _rows_for(shapes, mult):
    n = 0
    for s in shapes:
        k = 1
        for d in s:
            k *= d
        n += k
    rows = -(-n // 128)
    return -(-rows // mult) * mult


W_NAMES = ['ln1_g', 'w_in', 'b_gate', 'sg_ln_g', 'sg_ln_b', 'sg_w', 'sg_b', 'w_a_out', 'cv_w', 'cv_b', 'cv_ln_g', 'cv_ln_b',
           'w_b_out', 'q_norm_g', 'k_norm_g', 'w_c_out', 'w_out', 'ln2_g', 'w_up', 'ffn_conv_w', 'ffn_conv_b', 'w_down']


def _forward_layer(x, P, T, gather=None):
    D = x.shape[1]
    sv = {"x0": x}
    (h1,) = rowwise("rms_fwd", f_rms, [(x, D, 0)], [(P["ln1_g"], (1, D), None)], [(D, D, 0, bf16)], T, 256)
    z = mm_nn("in_proj", h1, P["w_in"], f32, tn=1664)
    sv["h1"], sv["z"] = h1, z
    (ya_in,) = rowwise("sgu_fwd", f_sgu, [(z, 512, 0), (z, 512, 1)],
                       [(P["sg_ln_g"], (1, 512), None), (P["sg_ln_b"], (1, 512), None),
                        (P["sg_w"], (8, CHUNK, CHUNK), None), (P["sg_b_t"], (CHUNK, 8), None)],
                       [(512, 512, 0, bf16)], T, CHUNK)
    ya = mm_nn("a_out", ya_in, P["w_a_out"], f32)
    (c0,) = rowwise("glu_fwd", f_glu, [(z, 512, 2), (z, 512, 3)], [], [(512, 512, 0, f32)], T, 256)
    c1 = conv31_fwd(c0, P["cv_w"], P["cv_b"], T)
    (c3,) = rowwise("lnsilu_fwd", f_lnsilu, [(c1, 512, 0)], [(P["cv_ln_g"], (1, 512), None), (P["cv_ln_b"], (1, 512), None)],
                    [(512, 512, 0, bf16)], T, 256)
    yb = mm_nn("b_out", c3, P["w_b_out"], f32)
    q8, kn, vb = rowwise("qkv_fwd", f_qkv, [(z, 512, 4), (z, 512, 5), (z, 512, 6)],
                         [(P["q_norm_g"], (1, HEAD_DIM), None), (P["k_norm_g"], (1, HEAD_DIM), None)],
                         [(512, 512, 0, bf16)] * 3, T, 256)
    o, tot, gathered = attn_fwd(q8, kn, vb, T, gather=gather)
    yc = mm_nn("c_out", o, P["w_c_out"], f32)
    (merged,) = rowwise("merge_fwd", f_merge,
                        [(z, 512, lambda c: 7 + c), (z, 512, lambda c: 9 + c), (z, 512, lambda c: 11 + c),
                         (ya, 512, lambda c: c), (yb, 512, lambda c: c), (yc, 512, lambda c: c)],
                        [(P["b_gate"][k], (1, 512), lambda c: c) for k in range(3)],
                        [(D, 512, lambda c: c, bf16)], T, 256, ncol=2)
    x1 = mm_nn("out_proj", merged, P["w_out"], f32, add=x)
    sv.update(ya_in=ya_in, ya=ya, c0=c0, c1=c1, c3=c3, yb=yb, q8=q8, kn=kn, vb=vb, o=o, tot=tot, yc=yc, merged=merged, x1=x1)
    (h2,) = rowwise("rms_fwd", f_rms, [(x1, D, 0)], [(P["ln2_g"], (1, D), None)], [(D, D, 0, bf16)], T, 256)
    up = mm_nn("up_proj", h2, P["w_up"], f32, tn=1408)
    act = ffn_act_fwd(up, P["ffn_conv_w"], P["ffn_conv_b"], T)
    x2 = mm_nn("down_proj", act, P["w_down"], f32, add=x1)
    sv.update(h2=h2, up=up, act=act)
    return x2, sv, gathered


def _backward_layer(dx2, P, sv, T, scatter=None):
    D = dx2.shape[1]
    G = {}
    G["w_down"] = mm_tn("dw_down", sv["act"], dx2, tk=1408)
    dact = mm_nt("d_act", dx2, P["w_down"], f32, tko=1408, tc=D)
    dup, dcw0, dcw1, dcw2, G["ffn_conv_b"] = ffn_act_bwd(sv["up"], P["ffn_conv_w"], P["ffn_conv_b"], dact, T)
    G["ffn_conv_w"] = jnp.concatenate([dcw0, dcw1, dcw2], axis=0)
    G["w_up"] = mm_tn("dw_up", sv["h2"], dup, n_slab=P["w_up"].shape[1] // N_DEV)
    dh2 = mm_nt("d_h2", dup, P["w_up"], f32, tc=1408)
    dx1, G["ln2_g"] = rowwise_bwd("rms_bwd", f_rms, [(sv["x1"], D, 0)], [(P["ln2_g"], (1, D), None)], [(dh2, D, 0)],
                                  [f32], T, 256, adds=[(dx2, D, 0)])
    G["w_out"] = mm_tn("dw_out", sv["merged"], dx1, tk=1024)
    dmerged = mm_nt("d_merged", dx1, P["w_out"], f32)
    z = sv["z"]
    dg0, dg1, dg2, dya, dyb, dyc, db0, db1, db2 = rowwise_bwd(
        "merge_bwd", f_merge,
        [(z, 512, lambda c: 7 + c), (z, 512, lambda c: 9 + c), (z, 512, lambda c: 11 + c),
         (sv["ya"], 512, lambda c: c), (sv["yb"], 512, lambda c: c), (sv["yc"], 512, lambda c: c)],
        [(P["b_gate"][k], (1, 512), lambda c: c) for k in range(3)],
        [(dmerged, 512, lambda c: c)], [bf16] * 6, T, 256, ncol=2)
    G["b_gate"] = jnp.concatenate([db0, db1, db2], axis=0)
    G["w_c_out"] = mm_tn("dw_c_out", sv["o"], dyc, n_slab=CHUNK)
    do = mm_nt("d_o", dyc, P["w_c_out"], f32)
    dq8, dkn, dvb, received = attn_bwd(sv["q8"], sv["kn"], sv["vb"], do, sv["tot"], T, scatter=scatter)
    dzq, dzk, dzv, G["q_norm_g"], G["k_norm_g"] = rowwise_bwd(
        "qkv_bwd", f_qkv, [(z, 512, 4), (z, 512, 5), (z, 512, 6)],
        [(P["q_norm_g"], (1, HEAD_DIM), None), (P["k_norm_g"], (1, HEAD_DIM), None)],
        [(dq8, 512, 0), (dkn, 512, 0), (dvb, 512, 0)], [bf16] * 3, T, 256)
    G["w_b_out"] = mm_tn("dw_b_out", sv["c3"], dyb, n_slab=CHUNK)
    dc3 = mm_nt("d_c3", dyb, P["w_b_out"], f32)
    dc1, G["cv_ln_g"], G["cv_ln_b"] = rowwise_bwd(
        "lnsilu_bwd", f_lnsilu, [(sv["c1"], 512, 0)], [(P["cv_ln_g"], (1, 512), None), (P["cv_ln_b"], (1, 512), None)],
        [(dc3, 512, 0)], [f32], T, 256)
    dc0, G["cv_w"], G["cv_b"] = conv31_bwd(sv["c0"], P["cv_w"], dc1, T)
    dzp, dzgl = rowwise_bwd("glu_bwd", f_glu, [(z, 512, 2), (z, 512, 3)], [], [(dc0, 512, 0)], [bf16] * 2, T, 256)
    G["w_a_out"] = mm_tn("dw_a_out", sv["ya_in"], dya, n_slab=CHUNK)
    dya_in = mm_nt("d_ya_in", dya, P["w_a_out"], f32)
    dzu, dzv_a, G["sg_ln_g"], G["sg_ln_b"], G["sg_w"], dsbt = rowwise_bwd(
        "sgu_bwd", f_sgu, [(z, 512, 0), (z, 512, 1)],
        [(P["sg_ln_g"], (1, 512), None), (P["sg_ln_b"], (1, 512), None), (P["sg_w"], (8, CHUNK, CHUNK), None),
         (P["sg_b_t"], (CHUNK, 8), None)],
        [(dya_in, 512, 0)], [bf16] * 2, T, CHUNK)
    G["sg_b"] = dsbt.T
    dz = jnp.concatenate([dzu, dzv_a, dzp, dzgl, dzq, dzk, dzv, dg0, dg1, dg2], axis=1)
    G["w_in"] = mm_tn("dw_in", sv["h1"], dz, n_slab=P["w_in"].shape[1] // N_DEV)
    dh1 = mm_nt("d_h1", dz, P["w_in"], f32, tc=1664)
    dx0, G["ln1_g"] = rowwise_bwd("rms_bwd", f_rms, [(sv["x0"], D, 0)], [(P["ln1_g"], (1, D), None)], [(dh1, D, 0)],
                                  [f32], T, 256, adds=[(dx1, D, 0)])
    return dx0, G, received


def kernel(x, ln1_g, w_in, b_gate, sg_ln_g, sg_ln_b, sg_w, sg_b, w_a_out, cv_w, cv_b, cv_ln_g, cv_ln_b, w_b_out, q_norm_g, k_norm_g, w_c_out, w_out, ln2_g, w_up, ffn_conv_w, ffn_conv_b, w_down, loss_target, m_ln1_g, m_w_in, m_b_gate, m_sg_ln_g, m_sg_ln_b, m_sg_w, m_sg_b, m_w_a_out, m_cv_w, m_cv_b, m_cv_ln_g, m_cv_ln_b, m_w_b_out, m_q_norm_g, m_k_norm_g, m_w_c_out, m_w_out, m_ln2_g, m_w_up, m_ffn_conv_w, m_ffn_conv_b, m_w_down, v_ln1_g, v_w_in, v_b_gate, v_sg_ln_g, v_sg_ln_b, v_sg_w, v_sg_b, v_w_a_out, v_cv_w, v_cv_b, v_cv_ln_g, v_cv_ln_b, v_w_b_out, v_q_norm_g, v_k_norm_g, v_w_c_out, v_w_out, v_ln2_g, v_w_up, v_ffn_conv_w, v_ffn_conv_b, v_w_down):
    W = dict(ln1_g=ln1_g, w_in=w_in, b_gate=b_gate, sg_ln_g=sg_ln_g, sg_ln_b=sg_ln_b, sg_w=sg_w, sg_b=sg_b, w_a_out=w_a_out,
             cv_w=cv_w, cv_b=cv_b, cv_ln_g=cv_ln_g, cv_ln_b=cv_ln_b, w_b_out=w_b_out, q_norm_g=q_norm_g, k_norm_g=k_norm_g,
             w_c_out=w_c_out, w_out=w_out, ln2_g=ln2_g, w_up=w_up, ffn_conv_w=ffn_conv_w, ffn_conv_b=ffn_conv_b, w_down=w_down)
    M = dict(ln1_g=m_ln1_g, w_in=m_w_in, b_gate=m_b_gate, sg_ln_g=m_sg_ln_g, sg_ln_b=m_sg_ln_b, sg_w=m_sg_w, sg_b=m_sg_b,
             w_a_out=m_w_a_out, cv_w=m_cv_w, cv_b=m_cv_b, cv_ln_g=m_cv_ln_g, cv_ln_b=m_cv_ln_b, w_b_out=m_w_b_out,
             q_norm_g=m_q_norm_g, k_norm_g=m_k_norm_g, w_c_out=m_w_c_out, w_out=m_w_out, ln2_g=m_ln2_g, w_up=m_w_up,
             ffn_conv_w=m_ffn_conv_w, ffn_conv_b=m_ffn_conv_b, w_down=m_w_down)
    V = dict(ln1_g=v_ln1_g, w_in=v_w_in, b_gate=v_b_gate, sg_ln_g=v_sg_ln_g, sg_ln_b=v_sg_ln_b, sg_w=v_sg_w, sg_b=v_sg_b,
             w_a_out=v_w_a_out, cv_w=v_cv_w, cv_b=v_cv_b, cv_ln_g=v_cv_ln_g, cv_ln_b=v_cv_ln_b, w_b_out=v_w_b_out,
             q_norm_g=v_q_norm_g, k_norm_g=v_k_norm_g, w_c_out=v_w_c_out, w_out=v_w_out, ln2_g=v_ln2_g, w_up=v_w_up,
             ffn_conv_w=v_ffn_conv_w, ffn_conv_b=v_ffn_conv_b, w_down=v_w_down)
    T, D = x.shape[1], x.shape[2]
    L = DEPTH
    xs = x.reshape(T, D)
    target = loss_target.reshape(T, D)

    ss_shapes = [W[n].shape for n in SMALL_SHARDED]
    ss_rows = _rows_for(ss_shapes, 8)
    (ss_all,) = exchange("gather_small", [_pack([W[n] for n in SMALL_SHARDED], ss_rows)], ["gather"])
    full_small = {}
    pos = 0
    for n in SMALL_SHARDED:
        s = W[n].shape
        cnt = s[0] * s[1] * s[2]
        part = ss_all.reshape(N_DEV, -1)[:, pos:pos + cnt].reshape((N_DEV,) + s)
        full_small[n] = jnp.transpose(part, (1, 2, 0, 3)).reshape(s[0], s[1], N_DEV * s[2])
        pos += cnt

    params, saved = [], []
    cur = xs
    slabs = gather2("gather_big", [W[n][0].astype(bf16) for n in BIG])
    for l in range(L):
        P = {}
        for n, s in zip(BIG, slabs):
            P[n] = assemble("assemble_" + n, s) if n in COL_SHARDED else s.reshape(-1, s.shape[-1])
        for n in REPL:
            P[n] = W[n][l]
        for n in ("ln1_g", "sg_ln_g", "sg_ln_b", "cv_b", "cv_ln_g", "cv_ln_b", "q_norm_g", "k_norm_g", "ln2_g", "ffn_conv_b"):
            P[n] = P[n].reshape(1, -1)
        P["sg_b_t"] = P["sg_b"].T
        P["cv_w"] = full_small["cv_w"][l]
        P["b_gate"] = [full_small["b_gate"][l][k:k + 1] for k in range(3)]
        P["ffn_conv_w"] = [full_small["ffn_conv_w"][l][k:k + 1] for k in range(3)]
        nxt = [W[n][l + 1].astype(bf16) for n in BIG] if l + 1 < L else None
        cur, sv, slabs = _forward_layer(cur, P, T, gather=nxt)
        params.append(P)
        saved.append(sv)

    dy, loss_row = loss_head(cur, target, T)
    loss = lax.psum(loss_row[0, 0], ("x", "y", "c"))

    repl_shapes = [W[n].shape for n in REPL]
    repl_rows = _rows_for([s[1:] for s in repl_shapes], 16)
    ssl_rows = _rows_for([s[1:] for s in ss_shapes], 16)
    state = {"repl": [_pack_layers([X[n] for n in REPL], repl_rows) for X in (W, M, V)],
             "ss": [_pack_layers([X[n] for n in SMALL_SHARDED], ssl_rows) for X in (W, M, V)]}
    for n in BIG:
        state[n] = [W[n], M[n], V[n]]
    done = {n: None for n in state}

    def update(layer, received):
        for n, parts in zip(BIG + ["repl", "ss"], received):
            done[n] = adamw("adamw_" + n, *state[n], parts, layer, done[n])

    dcur = dy
    send = None
    for l in reversed(range(L)):
        dcur, G, recv = _backward_layer(dcur, params[l], saved[l], T, scatter=send)
        if send is not None:
            update(l + 1, recv)
        arrays = [G[n] if n in COL_SHARDED else G[n].reshape(N_DEV, -1, G[n].shape[-1]) for n in BIG]
        ss_parts = []
        for n in SMALL_SHARDED:
            k, c = W[n].shape[1:]
            ss_parts.append(jnp.transpose(G[n].reshape(k, N_DEV, c), (1, 0, 2)).reshape(N_DEV, k * c))
        ss_send = jnp.concatenate(ss_parts, axis=1)
        ss_send = jnp.pad(ss_send, ((0, 0), (0, ssl_rows * 128 - ss_send.shape[1]))).reshape(N_DEV, ssl_rows, 128)
        arrays += [_pack([G[n] for n in REPL], repl_rows), ss_send]
        send = (arrays, ["scatter"] * len(BIG) + ["gather", "scatter"])
    update(0, exchange("exchange_last", send[0], send[1]))

    results = [{}, {}, {}, {}]
    for k in range(4):
        for n, a in zip(REPL, _unpack_layers(done["repl"][k], repl_shapes)):
            results[k][n] = a
        for n, a in zip(SMALL_SHARDED, _unpack_layers(done["ss"][k], ss_shapes)):
            results[k][n] = a
        for n in BIG:
            results[k][n] = done[n][k]
    out = [loss, dcur.reshape(1, T, D)]
    for k in range(4):
        out += [results[k][n] for n in W_NAMES]
    return tuple(out)
```

```python
import functools

import jax
import jax.numpy as jnp
from jax import lax
from jax.experimental import pallas as pl
from jax.experimental.pallas import tpu as pltpu

f32 = jnp.float32
bf16 = jnp.bfloat16

EPS = 1e-6
N_DEV = 8
DEPTH = 4
CHUNK = 128
HEAD_DIM = 64
N_HEADS = 8
CV_KERNEL = 31
VMEM_LIMIT_BYTES = 56 * 2 ** 20

ADAM_LR = 0.001
ADAM_B1 = 0.9
ADAM_B2 = 0.999
ADAM_EPS = 1e-08
ADAM_WD = 0.01
ADAM_STEP = 10

MESH = pl.DeviceIdType.MESH


def _cparams(n_grid):
    return pltpu.CompilerParams(dimension_semantics=("arbitrary",) * n_grid, vmem_limit_bytes=VMEM_LIMIT_BYTES)


def exchange(name, arrays, modes):
    n = len(arrays)

    def body(*refs):
        copies = _direct_copies(refs[:n], refs[n:2 * n], modes, *refs[2 * n:])
        for cp in copies:
            cp.start()
        for cp in copies:
            cp.wait()

    return pl.pallas_call(
        body, name=name, out_shape=_exchange_out_shapes(arrays, modes),
        in_specs=[_ANY] * n, out_specs=[_ANY] * n, scratch_shapes=_exchange_sems(n),
    )(*arrays)


_ANY = pl.BlockSpec(memory_space=pl.ANY)


def _exchange_out_shapes(arrays, modes):
    return [jax.ShapeDtypeStruct((N_DEV,) + tuple(a.shape) if m == "gather" else tuple(a.shape), a.dtype)
            for a, m in zip(arrays, modes)]


def _exchange_sems(n):
    return [pltpu.SemaphoreType.DMA((n, N_DEV - 1)), pltpu.SemaphoreType.DMA((n, N_DEV - 1)), pltpu.SemaphoreType.DMA((n,))]


def _direct_copies(ins, outs, modes, send_sems, recv_sems, local_sems):
    x, y, c = lax.axis_index("x"), lax.axis_index("y"), lax.axis_index("c")
    me = 4 * x + 2 * y + c
    copies = []
    for k in range(len(ins)):
        src_mine = ins[k] if modes[k] == "gather" else ins[k].at[me]
        copies.append(pltpu.make_async_copy(src_mine, outs[k].at[me], local_sems.at[k]))
        for r in range(1, N_DEV):
            px = 1 - x if r & 4 else x
            py = 1 - y if r & 2 else y
            pc = 1 - c if r & 1 else c
            src = ins[k] if modes[k] == "gather" else ins[k].at[4 * px + 2 * py + pc]
            copies.append(pltpu.make_async_remote_copy(
                src_ref=src, dst_ref=outs[k].at[me], send_sem=send_sems.at[k, r - 1], recv_sem=recv_sems.at[k, r - 1],
                device_id=(px, py, pc), device_id_type=MESH))
    return copies


def _two_level_gather(ins, outs, send_sems, recv_sems, local_sems):
    x, y, c = lax.axis_index("x"), lax.axis_index("y"), lax.axis_index("c")
    me = 4 * x + 2 * y + c
    sibling = (x, y, 1 - c)
    chips = [(1 - x, y), (x, 1 - y), (1 - x, 1 - y)]

    def slot(px, py, pc):
        return 4 * px + 2 * py + pc

    def copy(k, sem, block, to, src=None):
        dst = outs[k].at[block]
        return pltpu.make_async_remote_copy(
            src_ref=dst if src is None else src, dst_ref=dst, send_sem=send_sems.at[k, sem], recv_sem=recv_sems.at[k, sem],
            device_id=to, device_id_type=MESH)

    def local(k):
        return pltpu.make_async_copy(ins[k], outs[k].at[me], local_sems.at[k])

    def first(k):
        return [copy(k, 0, me, sibling, src=ins[k])] + [copy(k, 1 + j, me, (*chip, c), src=ins[k]) for j, chip in enumerate(chips)]

    def passed_on(k, j):
        return copy(k, 4 + j, slot(*chips[j], c), sibling)

    def passed(k):
        return [passed_on(k, j) for j in range(3)]

    def start():
        for k in range(len(ins)):
            local(k).start()
            for cp in first(k):
                cp.start()

    def forward():
        for k in range(len(ins)):
            for j, chip in enumerate(chips):
                copy(k, 1 + j, slot(*chip, c), sibling).wait_recv()
                passed_on(k, j).start()

    def finish():
        for k in range(len(ins)):
            copy(k, 0, slot(x, y, 1 - c), sibling).wait_recv()
            for j, chip in enumerate(chips):
                copy(k, 4 + j, slot(*chip, 1 - c), sibling).wait_recv()
            for cp in first(k) + passed(k):
                cp.wait_send()
            local(k).wait()

    return start, forward, finish


def pair_exchange(name, big, smalls, modes):
    nb, ns = len(big), len(smalls)
    rows = [a.shape[2] for a in big]
    total, width = sum(rows), big[0].shape[3]
    n = nb + ns

    def body(*refs):
        ins = refs[:n]
        kept_big, got_big = refs[n:n + 2]
        small_outs = refs[n + 2:len(refs) - 3]
        send_sems, recv_sems, local_sems = refs[-3:]
        x, y, c = lax.axis_index("x"), lax.axis_index("y"), lax.axis_index("c")
        sibling = (x, y, 1 - c)
        copies = []

        def remote(k, src, dst):
            copies.append(pltpu.make_async_remote_copy(src_ref=src, dst_ref=dst, send_sem=send_sems.at[k],
                                                       recv_sem=recv_sems.at[k], device_id=sibling, device_id_type=MESH))

        off = 0
        for k in range(nb):
            part = pl.ds(off, rows[k])
            copies.append(pltpu.make_async_copy(ins[k].at[:, c], kept_big.at[:, part, :], local_sems.at[k]))
            remote(k, ins[k].at[:, 1 - c], got_big.at[:, part, :])
            off += rows[k]
        pos = 0
        for j in range(ns):
            k = nb + j
            if modes[j] == "scatter":
                copies.append(pltpu.make_async_copy(ins[k].at[:, c], small_outs[pos], local_sems.at[k]))
                remote(k, ins[k].at[:, 1 - c], small_outs[pos + 1])
                pos += 2
            else:
                remote(k, ins[k], small_outs[pos])
                pos += 1
        for cp in copies:
            cp.start()
        for cp in copies:
            cp.wait()

    big_shape = jax.ShapeDtypeStruct((4, total, width), big[0].dtype)
    out_shape = [big_shape, big_shape]
    for a, m in zip(smalls, modes):
        if m == "scatter":
            out_shape += [jax.ShapeDtypeStruct((4,) + tuple(a.shape[2:]), a.dtype)] * 2
        else:
            out_shape.append(jax.ShapeDtypeStruct(tuple(a.shape), a.dtype))
    return pl.pallas_call(
        body, name=name, out_shape=out_shape, in_specs=[_ANY] * n, out_specs=[_ANY] * len(out_shape),
        scratch_shapes=[pltpu.SemaphoreType.DMA((n,))] * 3,
    )(*big, *smalls)


def add2(name, a, b):
    shape = a.shape
    a2, b2 = a.reshape(-1, shape[-1]), b.reshape(-1, shape[-1])
    R, C = a2.shape
    tr = _row_tile(R, C)

    def body(a_ref, b_ref, o_ref):
        o_ref[...] = (a_ref[...].astype(f32) + b_ref[...].astype(f32)).astype(o_ref.dtype)

    spec = pl.BlockSpec((tr, C), lambda i: (i, 0))
    return pl.pallas_call(
        body, name=name, grid=(R // tr,), in_specs=[spec, spec], out_specs=spec,
        out_shape=jax.ShapeDtypeStruct((R, C), a.dtype), compiler_params=_cparams(1),
    )(a2, b2).reshape(shape)


def _chip_out_shapes(arrays, modes):
    return [jax.ShapeDtypeStruct(tuple(a.shape) if m == "scatter" else (4,) + tuple(a.shape), a.dtype)
            for a, m in zip(arrays, modes)]


def _chip_sems(n):
    return [pltpu.SemaphoreType.DMA((n, 3)), pltpu.SemaphoreType.DMA((n, 3)), pltpu.SemaphoreType.DMA((n,))]


def _chip_copies(ins, outs, modes, send_sems, recv_sems, local_sems):
    x, y, c = lax.axis_index("x"), lax.axis_index("y"), lax.axis_index("c")
    mine = 2 * x + y
    copies = []
    for k in range(len(ins)):
        copies.append(pltpu.make_async_copy(ins[k].at[mine] if modes[k] == "scatter" else ins[k], outs[k].at[mine],
                                            local_sems.at[k]))
        for r in range(1, 4):
            px = 1 - x if r & 2 else x
            py = 1 - y if r & 1 else y
            src = ins[k].at[2 * px + py] if modes[k] == "scatter" else ins[k]
            copies.append(pltpu.make_async_remote_copy(
                src_ref=src, dst_ref=outs[k].at[mine], send_sem=send_sems.at[k, r - 1], recv_sem=recv_sems.at[k, r - 1],
                device_id=(px, py, c), device_id_type=MESH))
    return copies


def chip_exchange(name, arrays, modes):
    n = len(arrays)

    def body(*refs):
        copies = _chip_copies(refs[:n], refs[n:2 * n], modes, *refs[2 * n:])
        for cp in copies:
            cp.start()
        for cp in copies:
            cp.wait()

    return pl.pallas_call(
        body, name=name, out_shape=_chip_out_shapes(arrays, modes),
        in_specs=[_ANY] * n, out_specs=[_ANY] * n, scratch_shapes=_chip_sems(n),
    )(*arrays)


def gather2(name, arrays):
    n = len(arrays)

    def body(*refs):
        start, forward, finish = _two_level_gather(refs[:n], refs[n:2 * n], *refs[2 * n:])
        start()
        forward()
        finish()

    return pl.pallas_call(
        body, name=name, out_shape=_exchange_out_shapes(arrays, ["gather"] * n),
        in_specs=[_ANY] * n, out_specs=[_ANY] * n, scratch_shapes=_exchange_sems(n),
    )(*arrays)


def assemble(name, slabs):
    _, K, Ns = slabs.shape
    g = N_DEV if Ns % 128 == 0 else 2

    def body(w_ref, o_ref):
        o_ref[...] = jnp.concatenate([w_ref[s] for s in range(g)], axis=1)

    return pl.pallas_call(
        body, name=name, grid=(N_DEV // g,),
        in_specs=[pl.BlockSpec((g, K, Ns), lambda m: (m, 0, 0))],
        out_specs=pl.BlockSpec((K, g * Ns), lambda m: (0, m)),
        out_shape=jax.ShapeDtypeStruct((K, N_DEV * Ns), slabs.dtype),
        compiler_params=_cparams(1),
    )(slabs)


def mm_nn(name, a, b, out_dtype, add=None, tm=1024, tn=512, tk=None):
    M, K = a.shape
    N = b.shape[1]
    tm, tn = min(tm, M), min(tn, N)
    tk = K if tk is None else tk
    nk = K // tk
    has_add = add is not None

    def body(a_ref, b_ref, *rest):
        o_ref = rest[1] if has_add else rest[0]
        part = jnp.dot(a_ref[...].astype(bf16), b_ref[...].astype(bf16), preferred_element_type=f32)
        if nk == 1:
            o_ref[...] = (part + rest[0][...] if has_add else part).astype(o_ref.dtype)
            return
        acc_ref = rest[-1]
        k = pl.program_id(2)

        @pl.when(k == 0)
        def _():
            acc_ref[...] = part + rest[0][...] if has_add else part

        @pl.when(k > 0)
        def _():
            acc_ref[...] += part

        @pl.when(k == nk - 1)
        def _():
            o_ref[...] = acc_ref[...].astype(o_ref.dtype)

    in_specs = [pl.BlockSpec((tm, tk), lambda i, j, k: (i, k)), pl.BlockSpec((tk, tn), lambda i, j, k: (k, j))]
    ops = [a, b]
    if has_add:
        in_specs.append(pl.BlockSpec((tm, tn), lambda i, j, k: (i, j)))
        ops.append(add)
    return pl.pallas_call(
        body, name=name, grid=(M // tm, N // tn, nk), in_specs=in_specs,
        out_specs=pl.BlockSpec((tm, tn), lambda i, j, k: (i, j)),
        out_shape=jax.ShapeDtypeStruct((M, N), out_dtype),
        scratch_shapes=[pltpu.VMEM((tm, tn), f32)] if nk > 1 else [], compiler_params=_cparams(3),
    )(*ops)


def concat_cols(name, pieces, T, tm=512):
    widths = [p.shape[1] for p in pieces]
    total = sum(widths)
    tm = min(tm, T)

    def body(*refs):
        o_ref = refs[-1]
        off = 0
        for r, w in zip(refs[:-1], widths):
            o_ref[:, off:off + w] = r[...]
            off += w

    return pl.pallas_call(
        body, name=name, grid=(T // tm,),
        in_specs=[pl.BlockSpec((tm, w), lambda i: (i, 0)) for w in widths],
        out_specs=pl.BlockSpec((tm, total), lambda i: (i, 0)),
        out_shape=jax.ShapeDtypeStruct((T, total), pieces[0].dtype), compiler_params=_cparams(1),
    )(*pieces)


def mm_nt(name, a, b, out_dtype, tm=1024, tko=None, tc=512):
    M, C = a.shape
    Ko = b.shape[0]
    tm, tc = min(tm, M), min(tc, C)
    tko = Ko if tko is None else tko
    nc = C // tc

    def body(a_ref, b_ref, o_ref, acc_ref):
        c = pl.program_id(2)
        part = lax.dot_general(a_ref[...].astype(bf16), b_ref[...].astype(bf16), (((1,), (1,)), ((), ())),
                               preferred_element_type=f32)

        @pl.when(c == 0)
        def _():
            acc_ref[...] = part

        @pl.when(c > 0)
        def _():
            acc_ref[...] += part

        @pl.when(c == nc - 1)
        def _():
            o_ref[...] = acc_ref[...].astype(o_ref.dtype)

    return pl.pallas_call(
        body, name=name, grid=(M // tm, Ko // tko, nc),
        in_specs=[pl.BlockSpec((tm, tc), lambda i, j, c: (i, c)), pl.BlockSpec((tko, tc), lambda i, j, c: (j, c))],
        out_specs=pl.BlockSpec((tm, tko), lambda i, j, c: (i, j)),
        out_shape=jax.ShapeDtypeStruct((M, Ko), out_dtype),
        scratch_shapes=[pltpu.VMEM((tm, tko), f32)], compiler_params=_cparams(3),
    )(a, b)


def mm_tn(name, a, b, n_slab=None, tk=512, tn=512):
    T, K = a.shape
    N = b.shape[1]
    tk = min(tk, K)
    if n_slab is None:
        tn = min(tn, N)

        def body(a_ref, b_ref, o_ref):
            o_ref[...] = lax.dot_general(a_ref[...].astype(bf16), b_ref[...].astype(bf16), (((0,), (0,)), ((), ())),
                                         preferred_element_type=f32).astype(o_ref.dtype)

        return pl.pallas_call(
            body, name=name, grid=(N // tn, K // tk),
            in_specs=[pl.BlockSpec((T, tk), lambda j, i: (0, i)), pl.BlockSpec((T, tn), lambda j, i: (0, j))],
            out_specs=pl.BlockSpec((tk, tn), lambda j, i: (i, j)),
            out_shape=jax.ShapeDtypeStruct((K, N), bf16), compiler_params=_cparams(2),
        )(a, b)

    Ns = n_slab
    g = N_DEV if Ns % 128 == 0 else 2
    tn = g * Ns

    def body(a_ref, b_ref, o_ref):
        val = lax.dot_general(a_ref[...].astype(bf16), b_ref[...].astype(bf16), (((0,), (0,)), ((), ())),
                              preferred_element_type=f32)
        for s in range(g):
            o_ref[s] = val[:, s * Ns:(s + 1) * Ns].astype(o_ref.dtype)

    return pl.pallas_call(
        body, name=name, grid=(K // tk, N_DEV // g),
        in_specs=[pl.BlockSpec((T, tk), lambda i, j: (0, i)), pl.BlockSpec((T, tn), lambda i, j: (0, j))],
        out_specs=pl.BlockSpec((g, tk, Ns), lambda i, j: (j, i, 0)),
        out_shape=jax.ShapeDtypeStruct((N_DEV, K, Ns), bf16), compiler_params=_cparams(2),
    )(a, b)


def _tile_spec(w, tm, cb):
    if callable(cb):
        return pl.BlockSpec((tm, w), lambda c, i: (i, cb(c)))
    return pl.BlockSpec((tm, w), lambda c, i: (i, cb))


def _param_spec(block, cb):
    nd = len(block)
    if cb is None:
        return pl.BlockSpec(block, lambda c, i: (0,) * nd)
    return pl.BlockSpec(block, lambda c, i: (0,) * (nd - 1) + (cb(c),))


def rowwise(name, fn, tiled, params, outs, T, tm, ncol=1):
    n_in = len(tiled) + len(params)
    n_t = len(tiled)

    def body(*refs):
        ins = [r[...].astype(f32) for r in refs[:n_t]] + [r[...] for r in refs[n_t:n_in]]
        res = fn(*ins)
        for r, o in zip(refs[n_in:], res):
            r[...] = o.astype(r.dtype)

    return pl.pallas_call(
        body, name=name, grid=(ncol, T // tm),
        in_specs=[_tile_spec(w, tm, cb) for _, w, cb in tiled] + [_param_spec(blk, cb) for _, blk, cb in params],
        out_specs=[_tile_spec(w, tm, cb) for _, w, cb, _ in outs],
        out_shape=[jax.ShapeDtypeStruct((T, cols), dt) for cols, _, _, dt in outs],
        compiler_params=_cparams(2),
    )(*[a for a, _, _ in tiled], *[a for a, _, _ in params])


def rowwise_bwd(name, fn, tiled, params, cts, grads, T, tm, ncol=1, adds=None):
    n_t, n_p, n_c = len(tiled), len(params), len(cts)
    adds = adds or [None] * n_t
    add_list = [(k, a) for k, a in enumerate(adds) if a is not None]
    want = [k for k, g in enumerate(grads) if g is not None]
    n_a = len(add_list)

    def body(*refs):
        pos = 0
        t_refs = refs[pos:pos + n_t]; pos += n_t
        p_refs = refs[pos:pos + n_p]; pos += n_p
        c_refs = refs[pos:pos + n_c]; pos += n_c
        a_refs = refs[pos:pos + n_a]; pos += n_a
        g_refs = refs[pos:pos + len(want)]; pos += len(want)
        pg_refs = refs[pos:pos + n_p]
        primals = [r[...].astype(f32) for r in t_refs] + [r[...] for r in p_refs]
        _, vjp = jax.vjp(fn, *primals)
        g = vjp(tuple(r[...].astype(f32) for r in c_refs))
        add_of = {k: a_refs[n][...] for n, (k, _) in enumerate(add_list)}
        for n, k in enumerate(want):
            val = g[k]
            if k in add_of:
                val = val + add_of[k]
            g_refs[n][...] = val.astype(g_refs[n].dtype)
        i = pl.program_id(1)
        for k in range(n_p):
            @pl.when(i == 0)
            def _(k=k):
                pg_refs[k][...] = g[n_t + k]

            @pl.when(i > 0)
            def _(k=k):
                pg_refs[k][...] += g[n_t + k]

    in_specs = ([_tile_spec(w, tm, cb) for _, w, cb in tiled] + [_param_spec(blk, cb) for _, blk, cb in params]
                + [_tile_spec(w, tm, cb) for _, w, cb in cts] + [_tile_spec(w, tm, cb) for _, (_, w, cb) in add_list])
    ops = ([a for a, _, _ in tiled] + [a for a, _, _ in params] + [a for a, _, _ in cts]
           + [a for _, (a, _, _) in add_list])
    out_specs, out_shape = [], []
    for k in want:
        w = tiled[k][1]
        out_specs.append(_tile_spec(w, tm, lambda c: c))
        out_shape.append(jax.ShapeDtypeStruct((T, ncol * w), grads[k]))
    for a, blk, cb in params:
        out_specs.append(_param_spec(blk, cb))
        out_shape.append(jax.ShapeDtypeStruct(a.shape, f32))
    return pl.pallas_call(
        body, name=name, grid=(ncol, T // tm), in_specs=in_specs, out_specs=out_specs, out_shape=out_shape,
        compiler_params=_cparams(2),
    )(*ops)


@jax.custom_vjp
def _bdot(a, b):
    return jnp.dot(a.astype(bf16), b.astype(bf16), preferred_element_type=f32)


def _bdot_fwd(a, b):
    return _bdot(a, b), (a, b)


def _bdot_bwd(res, ct):
    a, b = res
    ctb = ct.astype(bf16)
    da = lax.dot_general(ctb, b.astype(bf16), (((1,), (1,)), ((), ())), preferred_element_type=f32)
    db = lax.dot_general(a.astype(bf16), ctb, (((0,), (0,)), ((), ())), preferred_element_type=f32)
    return da, db


_bdot.defvjp(_bdot_fwd, _bdot_bwd)


def _layer_norm(x, g, b):
    mu = jnp.mean(x, axis=-1, keepdims=True)
    xc = x - mu
    y = xc * lax.rsqrt(jnp.mean(xc * xc, axis=-1, keepdims=True) + EPS)
    return y * g + b


def f_rms(x, g):
    y = x * lax.rsqrt(jnp.mean(x * x, axis=-1, keepdims=True) + EPS)
    return (y * g,)


def f_sgu(zu, zv, ln_g, ln_b, wm, sgb_t):
    u = jax.nn.gelu(zu)
    vn = _layer_norm(jax.nn.gelu(zv), ln_g, ln_b)
    row = lax.broadcasted_iota(jnp.int32, (CHUNK, CHUNK), 0)
    col = lax.broadcasted_iota(jnp.int32, (CHUNK, CHUNK), 1)
    tril = col <= row
    low = col < HEAD_DIM
    parts = []
    for p in range(4):
        vp = vn[:, CHUNK * p:CHUNK * (p + 1)]
        w0 = jnp.where(tril, wm[2 * p], 0.0)
        w1 = jnp.where(tril, wm[2 * p + 1], 0.0)
        parts.append(_bdot(w0, jnp.where(low, vp, 0.0)) + _bdot(w1, jnp.where(low, 0.0, vp)))
    s = jnp.concatenate(parts, axis=1)
    lane_g = lax.shift_right_logical(lax.broadcasted_iota(jnp.int32, s.shape, 1), 6)
    bias = jnp.zeros_like(s)
    for g in range(8):
        bias = jnp.where(lane_g == g, sgb_t[:, g:g + 1], bias)
    return (u * (s + bias),)


def f_glu(p, gl):
    return (p * jax.nn.sigmoid(gl),)


def f_lnsilu(c1, g, b):
    return (jax.nn.silu(_layer_norm(c1, g, b)),)


def _head_norm(x, g64):
    g = jnp.concatenate([g64] * N_HEADS, axis=1)
    lane_h = lax.shift_right_logical(lax.broadcasted_iota(jnp.int32, x.shape, 1), 6)
    x2 = x * x
    r = jnp.zeros_like(x)
    for h in range(N_HEADS):
        mh = lane_h == h
        ms = jnp.sum(jnp.where(mh, x2, 0.0), axis=-1, keepdims=True) * (1.0 / HEAD_DIM)
        r = jnp.where(mh, lax.rsqrt(ms + EPS), r)
    return (x * r) * g


def f_qkv(zq, zk, zv, qg, kg):
    return (_head_norm(zq, qg) * 0.125, _head_norm(zk, kg), zv)


def f_merge(g0, g1, g2, ya, yb, yc, b0, b1, b2):
    return (jax.nn.sigmoid(g0 + b0) * ya + jax.nn.sigmoid(g1 + b1) * yb + jax.nn.sigmoid(g2 + b2) * yc,)


HALO = 32


def conv31_fwd(x, w, b, T, tt=256):
    C = x.shape[1]
    r = tt // HALO

    def body(x_ref, h_ref, w_ref, b_ref, y_ref, buf):
        i = pl.program_id(0)
        halo = h_ref[...]
        buf[0:HALO, :] = jnp.where(i > 0, halo, jnp.zeros_like(halo))
        buf[HALO:HALO + tt, :] = x_ref[...]
        acc = jnp.zeros((tt, C), f32) + b_ref[...]
        for k in range(CV_KERNEL):
            acc = acc + w_ref[k:k + 1, :] * buf[pl.ds(HALO - (CV_KERNEL - 1) + k, tt), :]
        y_ref[...] = acc

    return pl.pallas_call(
        body, name="conv31_fwd", grid=(T // tt,),
        in_specs=[pl.BlockSpec((tt, C), lambda i: (i, 0)),
                  pl.BlockSpec((HALO, C), lambda i: (jnp.maximum(i * r - 1, 0), 0)),
                  pl.BlockSpec((CV_KERNEL, C), lambda i: (0, 0)), pl.BlockSpec((1, C), lambda i: (0, 0))],
        out_specs=pl.BlockSpec((tt, C), lambda i: (i, 0)),
        out_shape=jax.ShapeDtypeStruct((T, C), f32),
        scratch_shapes=[pltpu.VMEM((HALO + tt, C), f32)], compiler_params=_cparams(1),
    )(x, x, w, b)


def conv31_bwd(x, w, dy, T, tt=256):
    C = x.shape[1]
    r = tt // HALO
    n = T // tt

    def body(x_ref, h_ref, w_ref, dy_ref, dyn_ref, dx_ref, dw_ref, db_ref, xbuf, dbuf):
        i = pl.program_id(0)
        halo = h_ref[...]
        xbuf[0:HALO, :] = jnp.where(i > 0, halo, jnp.zeros_like(halo))
        xbuf[HALO:HALO + tt, :] = x_ref[...]
        nxt = dyn_ref[...]
        dy = dy_ref[...]
        dbuf[0:tt, :] = dy
        dbuf[tt:tt + HALO, :] = jnp.where(i < n - 1, nxt, jnp.zeros_like(nxt))

        @pl.when(i == 0)
        def _():
            dw_ref[...] = jnp.zeros_like(dw_ref)
            db_ref[...] = jnp.zeros_like(db_ref)

        acc = jnp.zeros((tt, C), f32)
        for k in range(CV_KERNEL):
            acc = acc + w_ref[k:k + 1, :] * dbuf[pl.ds(CV_KERNEL - 1 - k, tt), :]
            xs = xbuf[pl.ds(HALO - (CV_KERNEL - 1) + k, tt), :]
            dw_ref[k:k + 1, :] += jnp.sum(dy * xs, axis=0, keepdims=True)
        dx_ref[...] = acc
        db_ref[...] += jnp.sum(dy, axis=0, keepdims=True)

    return pl.pallas_call(
        body, name="conv31_bwd", grid=(n,),
        in_specs=[pl.BlockSpec((tt, C), lambda i: (i, 0)),
                  pl.BlockSpec((HALO, C), lambda i: (jnp.maximum(i * r - 1, 0), 0)),
                  pl.BlockSpec((CV_KERNEL, C), lambda i: (0, 0)),
                  pl.BlockSpec((tt, C), lambda i: (i, 0)),
                  pl.BlockSpec((HALO, C), lambda i: (jnp.minimum((i + 1) * r, n * r - 1), 0))],
        out_specs=[pl.BlockSpec((tt, C), lambda i: (i, 0)), pl.BlockSpec((CV_KERNEL, C), lambda i: (0, 0)),
                   pl.BlockSpec((1, C), lambda i: (0, 0))],
        out_shape=[jax.ShapeDtypeStruct((T, C), f32), jax.ShapeDtypeStruct((CV_KERNEL, C), f32),
                   jax.ShapeDtypeStruct((1, C), f32)],
        scratch_shapes=[pltpu.VMEM((HALO + tt, C), f32), pltpu.VMEM((HALO + tt, C), f32)], compiler_params=_cparams(1),
    )(x, x, w, dy, dy)


FFN_TC = 128


def _shift_down(x, s, row):
    return jnp.where(row >= s, pltpu.roll(x, s, 0), 0.0)


def _shift_up(x, s, row, T):
    return jnp.where(row < T - s, pltpu.roll(x, T - s, 0), 0.0)


def _conv3(x, w0, w1, w2, b, row):
    return w0 * _shift_down(x, 2, row) + w1 * _shift_down(x, 1, row) + w2 * x + b


def ffn_act_fwd(up, cw, cb, T):
    F = up.shape[1] // 2
    nj = F // FFN_TC

    def body(g_ref, v_ref, g0, g1, g2, gb, v0, v1, v2, vb, o_ref):
        row = lax.broadcasted_iota(jnp.int32, (T, FFN_TC), 0)
        gc = _conv3(g_ref[...], g0[...], g1[...], g2[...], gb[...], row)
        vc = _conv3(v_ref[...], v0[...], v1[...], v2[...], vb[...], row)
        o_ref[...] = (jax.nn.silu(gc) * vc).astype(o_ref.dtype)

    gspec = pl.BlockSpec((T, FFN_TC), lambda j: (0, j))
    vspec = pl.BlockSpec((T, FFN_TC), lambda j: (0, j + nj))
    pg = pl.BlockSpec((1, FFN_TC), lambda j: (0, j))
    pv = pl.BlockSpec((1, FFN_TC), lambda j: (0, j + nj))
    return pl.pallas_call(
        body, name="ffn_act_fwd", grid=(nj,),
        in_specs=[gspec, vspec, pg, pg, pg, pg, pv, pv, pv, pv], out_specs=gspec,
        out_shape=jax.ShapeDtypeStruct((T, F), bf16), compiler_params=_cparams(1),
    )(up, up, cw[0], cw[1], cw[2], cb, cw[0], cw[1], cw[2], cb)


def ffn_act_bwd(up, cw, cb, dact, T):
    F = up.shape[1] // 2
    nj = F // FFN_TC

    def side_grads(dc, x, w0, w1, w2, row, dup_ref, p_refs):
        dup = w2 * dc + w1 * _shift_up(dc, 1, row, T) + w0 * _shift_up(dc, 2, row, T)
        dup_ref[...] = dup.astype(dup_ref.dtype)
        p_refs[0][...] = jnp.sum(dc * _shift_down(x, 2, row), axis=0, keepdims=True)
        p_refs[1][...] = jnp.sum(dc * _shift_down(x, 1, row), axis=0, keepdims=True)
        p_refs[2][...] = jnp.sum(dc * x, axis=0, keepdims=True)
        p_refs[3][...] = jnp.sum(dc, axis=0, keepdims=True)

    def body(g_ref, v_ref, g0, g1, g2, gb, v0, v1, v2, vb, d_ref, dupg_ref, dupv_ref, *p_refs):
        row = lax.broadcasted_iota(jnp.int32, (T, FFN_TC), 0)
        xg, xv = g_ref[...], v_ref[...]
        gc = _conv3(xg, g0[...], g1[...], g2[...], gb[...], row)
        vc = _conv3(xv, v0[...], v1[...], v2[...], vb[...], row)
        d = d_ref[...]
        sg = jax.nn.sigmoid(gc)
        side_grads(d * vc * (sg * (1.0 + gc * (1.0 - sg))), xg, g0[...], g1[...], g2[...], row, dupg_ref, p_refs[:4])
        side_grads(d * (gc * sg), xv, v0[...], v1[...], v2[...], row, dupv_ref, p_refs[4:])

    gspec = pl.BlockSpec((T, FFN_TC), lambda j: (0, j))
    vspec = pl.BlockSpec((T, FFN_TC), lambda j: (0, j + nj))
    pg = pl.BlockSpec((1, FFN_TC), lambda j: (0, j))
    pv = pl.BlockSpec((1, FFN_TC), lambda j: (0, j + nj))
    res = pl.pallas_call(
        body, name="ffn_act_bwd", grid=(nj,),
        in_specs=[gspec, vspec, pg, pg, pg, pg, pv, pv, pv, pv, gspec],
        out_specs=[gspec, gspec] + [pg] * 8,
        out_shape=[jax.ShapeDtypeStruct((T, F), bf16)] * 2 + [jax.ShapeDtypeStruct((1, F), f32)] * 8,
        compiler_params=_cparams(1),
    )(up, up, cw[0], cw[1], cw[2], cb, cw[0], cw[1], cw[2], cb, dact)
    dup = concat_cols("dup_concat", [res[0], res[1]], T)
    return (dup,) + tuple(jnp.concatenate([res[2 + k], res[6 + k]], axis=1) for k in range(4))


BQ = 256
BK = 256
assert BQ % BK == 0
NT_DIMS = (((1,), (1,)), ((), ()))
TN_DIMS = (((0,), (0,)), ((), ()))


def _split_dot(x, u):
    x1 = x.astype(bf16)
    x2 = (x - x1.astype(f32)).astype(bf16)
    n = x.shape[0]
    y = jnp.dot(jnp.concatenate([x1, x2], axis=0), u, preferred_element_type=f32)
    return y[0:n] + y[n:2 * n]


def _log_sigmoids(z):
    sp = jnp.log(1.0 + jnp.exp(-jnp.abs(z)))
    lsp = jnp.minimum(z, 0.0) - sp
    return lsp, lsp - z


def _stack_heads(x):
    head1 = lax.broadcasted_iota(jnp.int32, x.shape, 1) >= HEAD_DIM
    zero = jnp.zeros_like(x)
    return jnp.concatenate([jnp.where(head1, zero, x), jnp.where(head1, x, zero)], axis=0)


def _unstack_heads(y):
    head1 = lax.broadcasted_iota(jnp.int32, (BQ, y.shape[1]), 1) >= HEAD_DIM
    return jnp.where(head1, y[BQ:2 * BQ], y[0:BQ])


def last_block(i):
    return ((i + 1) * BQ - 1) // BK


def _attn_masks():
    row = lax.broadcasted_iota(jnp.int32, (2 * BQ, BK), 0)
    col = lax.broadcasted_iota(jnp.int32, (2 * BQ, BK), 1)
    ur = lax.broadcasted_iota(jnp.int32, (BK, BK), 0)
    uc = lax.broadcasted_iota(jnp.int32, (BK, BK), 1)
    return (row & (BQ - 1)) - col, (ur > uc).astype(bf16), (ur < uc).astype(bf16)


def attn_fwd(q, k, v, T, gather=None):
    nq = T // BQ
    n_g = 0 if gather is None else len(gather)

    def body(*refs):
        q_ref, k_ref, v_ref = refs[:3]
        o_ref, tot_ref = refs[3 + n_g:5 + n_g]
        p, i = pl.program_id(0), pl.program_id(1)
        if n_g:
            start, forward, finish = _two_level_gather(refs[3:3 + n_g], refs[5 + n_g:5 + 2 * n_g], *refs[5 + 2 * n_g:])
            pl.when(jnp.logical_and(p == 0, i == 0))(start)
        qs = _stack_heads(q_ref[...])
        diff, u_after, _ = _attn_masks()

        def step(t, carry):
            acc, c = carry
            jb = last_block(i) - t
            ks = pl.multiple_of(jb * BK, BK)
            kb = k_ref[pl.ds(ks, BK), :]
            vb = v_ref[pl.ds(ks, BK), :]
            m = diff > jb * BK - i * BQ
            z = lax.dot_general(qs, kb, NT_DIMS, preferred_element_type=f32)
            lsp, lsn = _log_sigmoids(z)
            lm = jnp.where(m, lsn, 0.0)
            a = jnp.where(m, jnp.exp(lsp + _split_dot(lm, u_after)), 0.0)
            acc = acc + jnp.exp(c) * jnp.dot(a.astype(bf16), vb, preferred_element_type=f32)
            return acc, c + jnp.sum(lm, axis=-1, keepdims=True)

        acc, c = lax.fori_loop(0, last_block(i) + 1, step, (jnp.zeros((2 * BQ, 128), f32), jnp.zeros((2 * BQ, 1), f32)))
        o_ref[...] = _unstack_heads(acc).astype(o_ref.dtype)
        tot_ref[...] = _unstack_heads(jnp.broadcast_to(c, (2 * BQ, 128)))
        if n_g:
            @pl.when(jnp.logical_and(p == 3, i == nq - 1))
            def _():
                forward()
                finish()

    blk = pl.BlockSpec((BQ, 128), lambda p, i: (i, p))
    full = pl.BlockSpec((T, 128), lambda p, i: (0, p))
    g_list = [] if gather is None else list(gather)
    res = pl.pallas_call(
        body, name="attn_fwd_gather" if n_g else "attn_fwd", grid=(4, nq),
        in_specs=[blk, full, full] + [_ANY] * n_g, out_specs=[blk, blk] + [_ANY] * n_g,
        out_shape=[jax.ShapeDtypeStruct((T, 512), bf16), jax.ShapeDtypeStruct((T, 512), f32)]
        + _exchange_out_shapes(g_list, ["gather"] * n_g),
        scratch_shapes=_exchange_sems(n_g) if n_g else [], compiler_params=_cparams(2),
    )(q, k, v, *g_list)
    return res[0], res[1], list(res[2:])


def attn_bwd(q, k, v, do, tot, T, scatter=None):
    nq = T // BQ
    n_s = 0 if scatter is None else len(scatter[0])
    modes = [] if scatter is None else list(scatter[1])

    def body(*refs):
        q_ref, k_ref, v_ref, do_ref, tot_ref = refs[:5]
        dq_ref, dk_ref, dv_ref = refs[5 + n_s:8 + n_s]
        p, i = pl.program_id(0), pl.program_id(1)
        if n_s:
            def copies():
                return _chip_copies(refs[5:5 + n_s], refs[8 + n_s:8 + 2 * n_s], modes, *refs[8 + 2 * n_s:])

            @pl.when(jnp.logical_and(p == 0, i == 0))
            def _():
                for cp in copies():
                    cp.start()

        @pl.when(i == 0)
        def _():
            dk_ref[...] = jnp.zeros_like(dk_ref)
            dv_ref[...] = jnp.zeros_like(dv_ref)

        qs = _stack_heads(q_ref[...])
        dos = _stack_heads(do_ref[...].astype(bf16))
        totv = tot_ref[...]
        tots = jnp.concatenate([totv[:, 0:1], totv[:, HEAD_DIM:HEAD_DIM + 1]], axis=0)
        diff, u_after, u_before = _attn_masks()

        def step(jb, carry):
            dq, cl, cg = carry
            ks = pl.multiple_of(jb * BK, BK)
            kb = k_ref[pl.ds(ks, BK), :]
            vb = v_ref[pl.ds(ks, BK), :]
            m = diff > jb * BK - i * BQ
            z = lax.dot_general(qs, kb, NT_DIMS, preferred_element_type=f32)
            lsp, lsn = _log_sigmoids(z)
            lm = jnp.where(m, lsn, 0.0)
            a = jnp.where(m, jnp.exp(lsp + _split_dot(lm, u_after)), 0.0)
            g = a * lax.dot_general(dos, vb, NT_DIMS, preferred_element_type=f32)
            beta = jnp.exp(lsp)
            aa = jnp.where(m, g * jnp.exp(lsn) - _split_dot(g, u_before) * beta, 0.0)
            bb = jnp.where(m, beta, 0.0)
            cl = cl + jnp.sum(lm, axis=-1, keepdims=True)
            f = jnp.exp(tots - cl)
            dz = (f * aa - cg * bb).astype(bf16)
            cg = cg + f * jnp.sum(g, axis=-1, keepdims=True)
            dq = dq + jnp.dot(dz, kb, preferred_element_type=f32)
            dk_ref[pl.ds(ks, BK), :] += lax.dot_general(dz, qs, TN_DIMS, preferred_element_type=f32)
            dv_ref[pl.ds(ks, BK), :] += lax.dot_general((f * a).astype(bf16), dos, TN_DIMS, preferred_element_type=f32)
            return dq, cl, cg

        zc = jnp.zeros((2 * BQ, 1), f32)
        dq, _, _ = lax.fori_loop(0, last_block(i) + 1, step, (jnp.zeros((2 * BQ, 128), f32), zc, zc))
        dq_ref[...] = _unstack_heads(dq)
        if n_s:
            @pl.when(jnp.logical_and(p == 3, i == nq - 1))
            def _():
                for cp in copies():
                    cp.wait()

    blk = pl.BlockSpec((BQ, 128), lambda p, i: (i, p))
    full = pl.BlockSpec((T, 128), lambda p, i: (0, p))
    shp = jax.ShapeDtypeStruct((T, 512), f32)
    s_list = [] if scatter is None else list(scatter[0])
    res = pl.pallas_call(
        body, name="attn_bwd_scatter" if n_s else "attn_bwd", grid=(4, nq),
        in_specs=[blk, full, full, blk, blk] + [_ANY] * n_s, out_specs=[blk, full, full] + [_ANY] * n_s,
        out_shape=[shp, shp, shp] + _chip_out_shapes(s_list, modes),
        scratch_shapes=_chip_sems(n_s) if n_s else [], compiler_params=_cparams(2),
    )(q, k, v, do, tot, *s_list)
    return res[0], res[1], res[2], list(res[3:])


def loss_head(y, target, T, tm=256):
    D = y.shape[1]

    def body(y_ref, t_ref, dy_ref, l_ref):
        i = pl.program_id(0)
        err = y_ref[...] - t_ref[...]
        dy_ref[...] = err * (1.0 / D)
        part = 0.5 * jnp.sum(jnp.sum(err * err, axis=-1, keepdims=True) * (1.0 / D), axis=0, keepdims=True)

        @pl.when(i == 0)
        def _():
            l_ref[...] = jnp.zeros_like(l_ref)

        l_ref[...] += jnp.broadcast_to(part, l_ref.shape)

    spec = pl.BlockSpec((tm, D), lambda i: (i, 0))
    return pl.pallas_call(
        body, name="loss_head", grid=(T // tm,), in_specs=[spec, spec],
        out_specs=[spec, pl.BlockSpec((1, 128), lambda i: (0, 0))],
        out_shape=[jax.ShapeDtypeStruct((T, D), f32), jax.ShapeDtypeStruct((1, 128), f32)],
        compiler_params=_cparams(1),
    )(y, target)


def _row_tile(rows, cols, offset=0):
    best = None
    for t in range(16, rows + 1, 16):
        if rows % t == 0 and offset % t == 0 and t * cols * 4 <= 512 * 1024:
            best = t
    return best if best is not None else rows


def adamw(name, w, m, v, parts, layer, prev=None, row_off=0):
    L, R, C = w.shape
    n_parts = parts.shape[0]
    tr = _row_tile(R, C, row_off)
    assert R % tr == 0 and row_off % tr == 0, (name, R, row_off, tr)

    def body(w_ref, m_ref, v_ref, p_ref, *rest):
        g_ref, d_ref, nm_ref, nv_ref = rest[-4:]
        g = p_ref[0].astype(f32)
        for s in range(1, n_parts):
            g = g + p_ref[s].astype(f32)
        wv = w_ref[...]
        mn = ADAM_B1 * m_ref[...] + (1.0 - ADAM_B1) * g
        vn = ADAM_B2 * v_ref[...] + (1.0 - ADAM_B2) * jnp.square(g)
        m_hat = mn / (1.0 - ADAM_B1 ** ADAM_STEP)
        v_hat = vn / (1.0 - ADAM_B2 ** ADAM_STEP)
        g_ref[...] = g
        d_ref[...] = -ADAM_LR * (m_hat / (jnp.sqrt(v_hat) + ADAM_EPS) + ADAM_WD * wv)
        nm_ref[...] = mn
        nv_ref[...] = vn

    spec = pl.BlockSpec((None, tr, C), lambda i: (layer, i, 0))
    shp = jax.ShapeDtypeStruct((L, R, C), f32)
    n_prev = 0 if prev is None else 4
    return pl.pallas_call(
        body, name=name, grid=(R // tr,),
        in_specs=[spec, spec, spec, pl.BlockSpec((n_parts, tr, C), lambda i: (0, row_off // tr + i, 0))] + [_ANY] * n_prev,
        out_specs=[spec] * 4, out_shape=[shp] * 4, input_output_aliases={4 + k: k for k in range(n_prev)},
        compiler_params=_cparams(1),
    )(w, m, v, parts, *(prev or ()))


REPL = ["ln1_g", "sg_ln_g", "sg_ln_b", "sg_w", "sg_b", "cv_b", "cv_ln_g", "cv_ln_b", "q_norm_g", "k_norm_g", "ln2_g",
        "ffn_conv_b"]
SMALL_SHARDED = ["b_gate", "cv_w", "ffn_conv_w"]
BIG = ["w_in", "w_a_out", "w_b_out", "w_c_out", "w_up", "w_out", "w_down"]
COL_SHARDED = ["w_a_out", "w_b_out", "w_c_out"]
TRANSPOSED = ["w_in", "w_up"]


def _pack(arrs, rows):
    flat = jnp.concatenate([a.reshape(-1) for a in arrs])
    return jnp.pad(flat, (0, rows * 128 - flat.shape[0])).reshape(rows, 128)


def _pack_layers(arrs, rows):
    flat = jnp.concatenate([a.reshape(a.shape[0], -1) for a in arrs], axis=1)
    return jnp.pad(flat, ((0, 0), (0, rows * 128 - flat.shape[1]))).reshape(flat.shape[0], rows, 128)


def _unpack_layers(packed, shapes):
    flat = packed.reshape(packed.shape[0], -1)
    out, pos = [], 0
    for s in shapes:
        n = 1
        for d in s[1:]:
            n *= d
        out.append(flat[:, pos:pos + n].reshape(s))
        pos += n
    return out


def _rows_for(shapes, mult):
    n = 0
    for s in shapes:
        k = 1
        for d in s:
            k *= d
        n += k
    rows = -(-n // 128)
    return -(-rows // mult) * mult


W_NAMES = ['ln1_g', 'w_in', 'b_gate', 'sg_ln_g', 'sg_ln_b', 'sg_w', 'sg_b', 'w_a_out', 'cv_w', 'cv_b', 'cv_ln_g', 'cv_ln_b',
           'w_b_out', 'q_norm_g', 'k_norm_g', 'w_c_out', 'w_out', 'ln2_g', 'w_up', 'ffn_conv_w', 'ffn_conv_b', 'w_down']


def _forward_layer(x, P, T, gather=None):
    D = x.shape[1]
    sv = {"x0": x}
    (h1,) = rowwise("rms_fwd", f_rms, [(x, D, 0)], [(P["ln1_g"], (1, D), None)], [(D, D, 0, bf16)], T, 256)
    z = mm_nt("in_proj", h1, P["w_in_t"], f32, tko=1664, tc=D)
    sv["h1"], sv["z"] = h1, z
    (ya_in,) = rowwise("sgu_fwd", f_sgu, [(z, 512, 0), (z, 512, 1)],
                       [(P["sg_ln_g"], (1, 512), None), (P["sg_ln_b"], (1, 512), None),
                        (P["sg_w"], (8, CHUNK, CHUNK), None), (P["sg_b_t"], (CHUNK, 8), None)],
                       [(512, 512, 0, bf16)], T, CHUNK)
    ya = mm_nn("a_out", ya_in, P["w_a_out"], f32)
    (c0,) = rowwise("glu_fwd", f_glu, [(z, 512, 2), (z, 512, 3)], [], [(512, 512, 0, f32)], T, 256)
    c1 = conv31_fwd(c0, P["cv_w"], P["cv_b"], T)
    (c3,) = rowwise("lnsilu_fwd", f_lnsilu, [(c1, 512, 0)], [(P["cv_ln_g"], (1, 512), None), (P["cv_ln_b"], (1, 512), None)],
                    [(512, 512, 0, bf16)], T, 256)
    yb = mm_nn("b_out", c3, P["w_b_out"], f32)
    q8, kn, vb = rowwise("qkv_fwd", f_qkv, [(z, 512, 4), (z, 512, 5), (z, 512, 6)],
                         [(P["q_norm_g"], (1, HEAD_DIM), None), (P["k_norm_g"], (1, HEAD_DIM), None)],
                         [(512, 512, 0, bf16)] * 3, T, 256)
    o, tot, gathered = attn_fwd(q8, kn, vb, T, gather=gather)
    yc = mm_nn("c_out", o, P["w_c_out"], f32)
    (merged,) = rowwise("merge_fwd", f_merge,
                        [(z, 512, lambda c: 7 + c), (z, 512, lambda c: 9 + c), (z, 512, lambda c: 11 + c),
                         (ya, 512, lambda c: c), (yb, 512, lambda c: c), (yc, 512, lambda c: c)],
                        [(P["b_gate"][k], (1, 512), lambda c: c) for k in range(3)],
                        [(D, 512, lambda c: c, bf16)], T, 256, ncol=2)
    x1 = mm_nn("out_proj", merged, P["w_out"], f32, add=x)
    sv.update(ya_in=ya_in, ya=ya, c0=c0, c1=c1, c3=c3, yb=yb, q8=q8, kn=kn, vb=vb, o=o, tot=tot, yc=yc, merged=merged, x1=x1)
    (h2,) = rowwise("rms_fwd", f_rms, [(x1, D, 0)], [(P["ln2_g"], (1, D), None)], [(D, D, 0, bf16)], T, 256)
    up = mm_nt("up_proj", h2, P["w_up_t"], f32, tko=1408, tc=D)
    act = ffn_act_fwd(up, P["ffn_conv_w"], P["ffn_conv_b"], T)
    x2 = mm_nn("down_proj", act, P["w_down"], f32, add=x1)
    sv.update(h2=h2, up=up, act=act)
    return x2, sv, gathered


def _backward_layer(dx2, P, sv, T, scatter=None):
    D = dx2.shape[1]
    G = {}
    G["w_down"] = mm_tn("dw_down", sv["act"], dx2, tk=1408)
    dact = mm_nt("d_act", dx2, P["w_down"], f32, tko=1408, tc=D)
    dup, dcw0, dcw1, dcw2, G["ffn_conv_b"] = ffn_act_bwd(sv["up"], P["ffn_conv_w"], P["ffn_conv_b"], dact, T)
    G["ffn_conv_w"] = jnp.concatenate([dcw0, dcw1, dcw2], axis=0)
    G["w_up"] = mm_tn("dw_up", dup, sv["h2"], tk=1408)
    dh2 = mm_nn("d_h2", dup, P["w_up_t"], f32, tk=1408)
    dx1, G["ln2_g"] = rowwise_bwd("rms_bwd", f_rms, [(sv["x1"], D, 0)], [(P["ln2_g"], (1, D), None)], [(dh2, D, 0)],
                                  [f32], T, 256, adds=[(dx2, D, 0)])
    G["w_out"] = mm_tn("dw_out", sv["merged"], dx1, tk=1024)
    dmerged = mm_nt("d_merged", dx1, P["w_out"], f32)
    z = sv["z"]
    dg0, dg1, dg2, dya, dyb, dyc, db0, db1, db2 = rowwise_bwd(
        "merge_bwd", f_merge,
        [(z, 512, lambda c: 7 + c), (z, 512, lambda c: 9 + c), (z, 512, lambda c: 11 + c),
         (sv["ya"], 512, lambda c: c), (sv["yb"], 512, lambda c: c), (sv["yc"], 512, lambda c: c)],
        [(P["b_gate"][k], (1, 512), lambda c: c) for k in range(3)],
        [(dmerged, 512, lambda c: c)], [bf16] * 6, T, 256, ncol=2)
    G["b_gate"] = jnp.concatenate([db0, db1, db2], axis=0)
    G["w_c_out"] = mm_tn("dw_c_out", sv["o"], dyc, n_slab=CHUNK)
    do = mm_nt("d_o", dyc, P["w_c_out"], f32)
    dq8, dkn, dvb, received = attn_bwd(sv["q8"], sv["kn"], sv["vb"], do, sv["tot"], T, scatter=scatter)
    dzq, dzk, dzv, G["q_norm_g"], G["k_norm_g"] = rowwise_bwd(
        "qkv_bwd", f_qkv, [(z, 512, 4), (z, 512, 5), (z, 512, 6)],
        [(P["q_norm_g"], (1, HEAD_DIM), None), (P["k_norm_g"], (1, HEAD_DIM), None)],
        [(dq8, 512, 0), (dkn, 512, 0), (dvb, 512, 0)], [bf16] * 3, T, 256)
    G["w_b_out"] = mm_tn("dw_b_out", sv["c3"], dyb, n_slab=CHUNK)
    dc3 = mm_nt("d_c3", dyb, P["w_b_out"], f32)
    dc1, G["cv_ln_g"], G["cv_ln_b"] = rowwise_bwd(
        "lnsilu_bwd", f_lnsilu, [(sv["c1"], 512, 0)], [(P["cv_ln_g"], (1, 512), None), (P["cv_ln_b"], (1, 512), None)],
        [(dc3, 512, 0)], [f32], T, 256)
    dc0, G["cv_w"], G["cv_b"] = conv31_bwd(sv["c0"], P["cv_w"], dc1, T)
    dzp, dzgl = rowwise_bwd("glu_bwd", f_glu, [(z, 512, 2), (z, 512, 3)], [], [(dc0, 512, 0)], [bf16] * 2, T, 256)
    G["w_a_out"] = mm_tn("dw_a_out", sv["ya_in"], dya, n_slab=CHUNK)
    dya_in = mm_nt("d_ya_in", dya, P["w_a_out"], f32)
    dzu, dzv_a, G["sg_ln_g"], G["sg_ln_b"], G["sg_w"], dsbt = rowwise_bwd(
        "sgu_bwd", f_sgu, [(z, 512, 0), (z, 512, 1)],
        [(P["sg_ln_g"], (1, 512), None), (P["sg_ln_b"], (1, 512), None), (P["sg_w"], (8, CHUNK, CHUNK), None),
         (P["sg_b_t"], (CHUNK, 8), None)],
        [(dya_in, 512, 0)], [bf16] * 2, T, CHUNK)
    G["sg_b"] = dsbt.T
    dz = concat_cols("dz_concat", [dzu, dzv_a, dzp, dzgl, dzq, dzk, dzv, dg0, dg1, dg2], T)
    G["w_in"] = mm_tn("dw_in", dz, sv["h1"], tk=1664)
    dh1 = mm_nn("d_h1", dz, P["w_in_t"], f32, tk=1664)
    dx0, G["ln1_g"] = rowwise_bwd("rms_bwd", f_rms, [(sv["x0"], D, 0)], [(P["ln1_g"], (1, D), None)], [(dh1, D, 0)],
                                  [f32], T, 256, adds=[(dx1, D, 0)])
    return dx0, G, received


def kernel(x, ln1_g, w_in, b_gate, sg_ln_g, sg_ln_b, sg_w, sg_b, w_a_out, cv_w, cv_b, cv_ln_g, cv_ln_b, w_b_out, q_norm_g, k_norm_g, w_c_out, w_out, ln2_g, w_up, ffn_conv_w, ffn_conv_b, w_down, loss_target, m_ln1_g, m_w_in, m_b_gate, m_sg_ln_g, m_sg_ln_b, m_sg_w, m_sg_b, m_w_a_out, m_cv_w, m_cv_b, m_cv_ln_g, m_cv_ln_b, m_w_b_out, m_q_norm_g, m_k_norm_g, m_w_c_out, m_w_out, m_ln2_g, m_w_up, m_ffn_conv_w, m_ffn_conv_b, m_w_down, v_ln1_g, v_w_in, v_b_gate, v_sg_ln_g, v_sg_ln_b, v_sg_w, v_sg_b, v_w_a_out, v_cv_w, v_cv_b, v_cv_ln_g, v_cv_ln_b, v_w_b_out, v_q_norm_g, v_k_norm_g, v_w_c_out, v_w_out, v_ln2_g, v_w_up, v_ffn_conv_w, v_ffn_conv_b, v_w_down):
    W = dict(ln1_g=ln1_g, w_in=w_in, b_gate=b_gate, sg_ln_g=sg_ln_g, sg_ln_b=sg_ln_b, sg_w=sg_w, sg_b=sg_b, w_a_out=w_a_out,
             cv_w=cv_w, cv_b=cv_b, cv_ln_g=cv_ln_g, cv_ln_b=cv_ln_b, w_b_out=w_b_out, q_norm_g=q_norm_g, k_norm_g=k_norm_g,
             w_c_out=w_c_out, w_out=w_out, ln2_g=ln2_g, w_up=w_up, ffn_conv_w=ffn_conv_w, ffn_conv_b=ffn_conv_b, w_down=w_down)
    M = dict(ln1_g=m_ln1_g, w_in=m_w_in, b_gate=m_b_gate, sg_ln_g=m_sg_ln_g, sg_ln_b=m_sg_ln_b, sg_w=m_sg_w, sg_b=m_sg_b,
             w_a_out=m_w_a_out, cv_w=m_cv_w, cv_b=m_cv_b, cv_ln_g=m_cv_ln_g, cv_ln_b=m_cv_ln_b, w_b_out=m_w_b_out,
             q_norm_g=m_q_norm_g, k_norm_g=m_k_norm_g, w_c_out=m_w_c_out, w_out=m_w_out, ln2_g=m_ln2_g, w_up=m_w_up,
             ffn_conv_w=m_ffn_conv_w, ffn_conv_b=m_ffn_conv_b, w_down=m_w_down)
    V = dict(ln1_g=v_ln1_g, w_in=v_w_in, b_gate=v_b_gate, sg_ln_g=v_sg_ln_g, sg_ln_b=v_sg_ln_b, sg_w=v_sg_w, sg_b=v_sg_b,
             w_a_out=v_w_a_out, cv_w=v_cv_w, cv_b=v_cv_b, cv_ln_g=v_cv_ln_g, cv_ln_b=v_cv_ln_b, w_b_out=v_w_b_out,
             q_norm_g=v_q_norm_g, k_norm_g=v_k_norm_g, w_c_out=v_w_c_out, w_out=v_w_out, ln2_g=v_ln2_g, w_up=v_w_up,
             ffn_conv_w=v_ffn_conv_w, ffn_conv_b=v_ffn_conv_b, w_down=v_w_down)
    T, D = x.shape[1], x.shape[2]
    L = DEPTH
    xs = x.reshape(T, D)
    target = loss_target.reshape(T, D)

    ss_shapes = [W[n].shape for n in SMALL_SHARDED]
    ss_rows = _rows_for(ss_shapes, 8)
    (ss_all,) = exchange("gather_small", [_pack([W[n] for n in SMALL_SHARDED], ss_rows)], ["gather"])
    full_small = {}
    pos = 0
    for n in SMALL_SHARDED:
        s = W[n].shape
        cnt = s[0] * s[1] * s[2]
        part = ss_all.reshape(N_DEV, -1)[:, pos:pos + cnt].reshape((N_DEV,) + s)
        full_small[n] = jnp.transpose(part, (1, 2, 0, 3)).reshape(s[0], s[1], N_DEV * s[2])
        pos += cnt

    params, saved = [], []
    cur = xs
    Wt = {n: (jnp.transpose(W[n], (0, 2, 1)) if n in TRANSPOSED else W[n]) for n in BIG}

    def local_slabs(l):
        return [Wt[n][l].astype(bf16) for n in BIG]

    slabs = gather2("gather_big", local_slabs(0))
    for l in range(L):
        P = {}
        for n, s in zip(BIG, slabs):
            if n in TRANSPOSED:
                P[n + "_t"] = s.reshape(-1, s.shape[-1])
            else:
                P[n] = assemble("assemble_" + n, s) if n in COL_SHARDED else s.reshape(-1, s.shape[-1])
        for n in REPL:
            P[n] = W[n][l]
        for n in ("ln1_g", "sg_ln_g", "sg_ln_b", "cv_b", "cv_ln_g", "cv_ln_b", "q_norm_g", "k_norm_g", "ln2_g", "ffn_conv_b"):
            P[n] = P[n].reshape(1, -1)
        P["sg_b_t"] = P["sg_b"].T
        P["cv_w"] = full_small["cv_w"][l]
        P["b_gate"] = [full_small["b_gate"][l][k:k + 1] for k in range(3)]
        P["ffn_conv_w"] = [full_small["ffn_conv_w"][l][k:k + 1] for k in range(3)]
        nxt = local_slabs(l + 1) if l + 1 < L else None
        cur, sv, slabs = _forward_layer(cur, P, T, gather=nxt)
        params.append(P)
        saved.append(sv)

    dy, loss_row = loss_head(cur, target, T)
    loss = lax.psum(loss_row[0, 0], ("x", "y", "c"))

    repl_shapes = [W[n].shape for n in REPL]
    repl_rows = _rows_for([s[1:] for s in repl_shapes], 16)
    ssl_rows = _rows_for([s[1:] for s in ss_shapes], 16)
    state = {"repl": [_pack_layers([X[n] for n in REPL], repl_rows) for X in (W, M, V)],
             "ss": [_pack_layers([X[n] for n in SMALL_SHARDED], ssl_rows) for X in (W, M, V)]}
    big_rows, big_off = {}, {}
    for n in BIG:
        big_off[n] = sum(big_rows.values())
        big_rows[n] = W[n].shape[1] * W[n].shape[2] // D
        state[n] = [(jnp.transpose(X[n], (0, 2, 1)) if n in TRANSPOSED else X[n]).reshape(L, big_rows[n], D)
                    for X in (W, M, V)]
    done = {n: None for n in state}

    def update(layer, received):
        for n in BIG:
            done[n] = adamw("adamw_" + n, *state[n], received[0], layer, done[n], row_off=big_off[n])
        done["repl"] = adamw("adamw_repl", *state["repl"], received[1], layer, done["repl"])
        done["ss"] = adamw("adamw_ss", *state["ss"], received[2], layer, done["ss"])

    def reduce_in_chip(G):
        big = [G[n].reshape(4, 2, big_rows[n], D) for n in BIG]
        ss_parts = []
        for n in SMALL_SHARDED:
            k, c = W[n].shape[1:]
            ss_parts.append(jnp.transpose(G[n].reshape(k, N_DEV, c), (1, 0, 2)).reshape(N_DEV, k * c))
        ss_send = jnp.concatenate(ss_parts, axis=1)
        ss_send = jnp.pad(ss_send, ((0, 0), (0, ssl_rows * 128 - ss_send.shape[1]))).reshape(4, 2, ssl_rows, 128)
        repl = _pack([G[n] for n in REPL], repl_rows)
        kept, got, repl_got, ss_kept, ss_got = pair_exchange("pair_exchange", big, [repl, ss_send], ["gather", "scatter"])
        return [add2("pair_add_big", kept, got), add2("pair_add_repl", repl, repl_got), add2("pair_add_ss", ss_kept, ss_got)]

    chip_modes = ["scatter", "gather", "scatter"]
    dcur = dy
    send = None
    for l in reversed(range(L)):
        dcur, G, recv = _backward_layer(dcur, params[l], saved[l], T, scatter=send)
        if send is not None:
            update(l + 1, recv)
        send = (reduce_in_chip(G), chip_modes)
    update(0, chip_exchange("chip_exchange_last", send[0], send[1]))

    results = [{}, {}, {}, {}]
    for k in range(4):
        for n, a in zip(REPL, _unpack_layers(done["repl"][k], repl_shapes)):
            results[k][n] = a
        for n, a in zip(SMALL_SHARDED, _unpack_layers(done["ss"][k], ss_shapes)):
            results[k][n] = a
        for n in BIG:
            results[k][n] = jnp.transpose(done[n][k], (0, 2, 1)) if n in TRANSPOSED else done[n][k].reshape(W[n].shape)
    out = [loss, dcur.reshape(1, T, D)]
    for k in range(4):
        out += [results[k][n] for n in W_NAMES]
    return tuple(out)
```

```python
import functools

import jax
import jax.numpy as jnp
from jax import lax
from jax.experimental import pallas as pl
from jax.experimental.pallas import tpu as pltpu

f32 = jnp.float32
bf16 = jnp.bfloat16

EPS = 1e-6
N_DEV = 8
DEPTH = 4
CHUNK = 128
HEAD_DIM = 64
N_HEADS = 8
CV_KERNEL = 31
VMEM_LIMIT_BYTES = 56 * 2 ** 20

ADAM_LR = 0.001
ADAM_B1 = 0.9
ADAM_B2 = 0.999
ADAM_EPS = 1e-08
ADAM_WD = 0.01
ADAM_STEP = 10

MESH = pl.DeviceIdType.MESH


def _cparams(n_grid):
    return pltpu.CompilerParams(dimension_semantics=("arbitrary",) * n_grid, vmem_limit_bytes=VMEM_LIMIT_BYTES)


def exchange(name, arrays, modes):
    n = len(arrays)

    def body(*refs):
        copies = _direct_copies(refs[:n], refs[n:2 * n], modes, *refs[2 * n:])
        for cp in copies:
            cp.start()
        for cp in copies:
            cp.wait()

    return pl.pallas_call(
        body, name=name, out_shape=_exchange_out_shapes(arrays, modes),
        in_specs=[_ANY] * n, out_specs=[_ANY] * n, scratch_shapes=_exchange_sems(n),
    )(*arrays)


_ANY = pl.BlockSpec(memory_space=pl.ANY)


def _exchange_out_shapes(arrays, modes):
    return [jax.ShapeDtypeStruct((N_DEV,) + tuple(a.shape) if m == "gather" else tuple(a.shape), a.dtype)
            for a, m in zip(arrays, modes)]


def _exchange_sems(n):
    return [pltpu.SemaphoreType.DMA((n, N_DEV - 1)), pltpu.SemaphoreType.DMA((n, N_DEV - 1)), pltpu.SemaphoreType.DMA((n,))]


def _direct_copies(ins, outs, modes, send_sems, recv_sems, local_sems):
    x, y, c = lax.axis_index("x"), lax.axis_index("y"), lax.axis_index("c")
    me = 4 * x + 2 * y + c
    copies = []
    for k in range(len(ins)):
        src_mine = ins[k] if modes[k] == "gather" else ins[k].at[me]
        copies.append(pltpu.make_async_copy(src_mine, outs[k].at[me], local_sems.at[k]))
        for r in range(1, N_DEV):
            px = 1 - x if r & 4 else x
            py = 1 - y if r & 2 else y
            pc = 1 - c if r & 1 else c
            src = ins[k] if modes[k] == "gather" else ins[k].at[4 * px + 2 * py + pc]
            copies.append(pltpu.make_async_remote_copy(
                src_ref=src, dst_ref=outs[k].at[me], send_sem=send_sems.at[k, r - 1], recv_sem=recv_sems.at[k, r - 1],
                device_id=(px, py, pc), device_id_type=MESH))
    return copies


def own_slot_buffer(block, n_slots, index):
    return lax.dynamic_update_slice(lax.empty((n_slots,) + tuple(block.shape), block.dtype), block[None],
                                    (index,) + (0,) * block.ndim)


def _sems(n, m):
    return [pltpu.SemaphoreType.DMA((n, m)), pltpu.SemaphoreType.DMA((n, m))]


def _two_level_gather(bufs, send_sems, recv_sems):
    x, y, c = lax.axis_index("x"), lax.axis_index("y"), lax.axis_index("c")
    me = 4 * x + 2 * y + c
    sibling = (x, y, 1 - c)
    chips = [(1 - x, y), (x, 1 - y), (1 - x, 1 - y)]

    def slot(px, py, pc):
        return 4 * px + 2 * py + pc

    def copy(k, sem, block, to):
        rows = bufs[k].at[block]
        return pltpu.make_async_remote_copy(src_ref=rows, dst_ref=rows, send_sem=send_sems.at[k, sem],
                                            recv_sem=recv_sems.at[k, sem], device_id=to, device_id_type=MESH)

    def first(k):
        return [copy(k, 0, me, sibling)] + [copy(k, 1 + j, me, (*chip, c)) for j, chip in enumerate(chips)]

    def passed_on(k, j):
        return copy(k, 4 + j, slot(*chips[j], c), sibling)

    def start():
        for k in range(len(bufs)):
            for cp in first(k):
                cp.start()

    def forward():
        for k in range(len(bufs)):
            for j in range(3):
                copy(k, 1 + j, slot(*chips[j], c), sibling).wait_recv()
                passed_on(k, j).start()

    def finish():
        for k in range(len(bufs)):
            copy(k, 0, slot(x, y, 1 - c), sibling).wait_recv()
            for j, chip in enumerate(chips):
                copy(k, 4 + j, slot(*chip, 1 - c), sibling).wait_recv()
            for cp in first(k) + [passed_on(k, j) for j in range(3)]:
                cp.wait_send()

    return start, forward, finish


def pair_exchange(name, big, smalls, modes):
    nb, ns = len(big), len(smalls)
    rows = [a.shape[2] for a in big]
    total, width = sum(rows), big[0].shape[3]
    n = nb + ns

    def body(*refs):
        ins, outs = refs[:n], refs[n:n + 1 + ns]
        send_sems, recv_sems = refs[-2:]
        x, y, c = lax.axis_index("x"), lax.axis_index("y"), lax.axis_index("c")
        copies = []

        def remote(k, src, dst):
            copies.append(pltpu.make_async_remote_copy(src_ref=src, dst_ref=dst, send_sem=send_sems.at[k, 0],
                                                       recv_sem=recv_sems.at[k, 0], device_id=(x, y, 1 - c),
                                                       device_id_type=MESH))

        off = 0
        for k in range(nb):
            remote(k, ins[k].at[:, 1 - c], outs[0].at[:, pl.ds(off, rows[k]), :])
            off += rows[k]
        for j in range(ns):
            remote(nb + j, ins[nb + j].at[:, 1 - c] if modes[j] == "scatter" else ins[nb + j], outs[1 + j])
        for cp in copies:
            cp.start()
        for cp in copies:
            cp.wait()

    out_shape = [jax.ShapeDtypeStruct((4, total, width), big[0].dtype)]
    for a, m in zip(smalls, modes):
        out_shape.append(jax.ShapeDtypeStruct((4,) + tuple(a.shape[2:]) if m == "scatter" else tuple(a.shape), a.dtype))
    return pl.pallas_call(
        body, name=name, out_shape=out_shape, in_specs=[_ANY] * n, out_specs=[_ANY] * len(out_shape),
        scratch_shapes=_sems(n, 1),
    )(*big, *smalls)


def add2(name, a, b):
    shape = a.shape
    a2, b2 = a.reshape(-1, shape[-1]), b.reshape(-1, shape[-1])
    R, C = a2.shape
    tr = _row_tile(R, C, max_elems=1200 * 1024)

    def body(a_ref, b_ref, o_ref):
        o_ref[...] = (a_ref[...].astype(f32) + b_ref[...].astype(f32)).astype(o_ref.dtype)

    spec = pl.BlockSpec((tr, C), lambda i: (i, 0))
    return pl.pallas_call(
        body, name=name, grid=(R // tr,), in_specs=[spec, spec], out_specs=spec,
        out_shape=jax.ShapeDtypeStruct((R, C), a.dtype), compiler_params=_cparams(1),
    )(a2, b2).reshape(shape)


def chip_buffers(arrays, modes):
    mine = 2 * lax.axis_index("x") + lax.axis_index("y")
    return [own_slot_buffer(lax.dynamic_index_in_dim(a, mine, 0, keepdims=False) if m == "scatter" else a, 4, mine)
            for a, m in zip(arrays, modes)]


def _chip_copies(ins, bufs, modes, send_sems, recv_sems):
    x, y, c = lax.axis_index("x"), lax.axis_index("y"), lax.axis_index("c")
    mine = 2 * x + y
    copies = []
    for k in range(len(ins)):
        for r in range(1, 4):
            px = 1 - x if r & 2 else x
            py = 1 - y if r & 1 else y
            src = ins[k].at[2 * px + py] if modes[k] == "scatter" else ins[k]
            copies.append(pltpu.make_async_remote_copy(
                src_ref=src, dst_ref=bufs[k].at[mine], send_sem=send_sems.at[k, r - 1], recv_sem=recv_sems.at[k, r - 1],
                device_id=(px, py, c), device_id_type=MESH))
    return copies


def chip_exchange(name, arrays, modes):
    n = len(arrays)

    def body(*refs):
        copies = _chip_copies(refs[:n], refs[2 * n:3 * n], modes, *refs[3 * n:])
        for cp in copies:
            cp.start()
        for cp in copies:
            cp.wait()

    bufs = chip_buffers(arrays, modes)
    return pl.pallas_call(
        body, name=name, out_shape=[jax.ShapeDtypeStruct(b.shape, b.dtype) for b in bufs],
        in_specs=[_ANY] * (2 * n), out_specs=[_ANY] * n, input_output_aliases={n + k: k for k in range(n)},
        scratch_shapes=_sems(n, 3),
    )(*arrays, *bufs)


def gather_buffers(blocks):
    me = 4 * lax.axis_index("x") + 2 * lax.axis_index("y") + lax.axis_index("c")
    return [own_slot_buffer(b, N_DEV, me) for b in blocks]


def gather2(name, arrays):
    n = len(arrays)

    def body(*refs):
        start, forward, finish = _two_level_gather(refs[n:2 * n], *refs[2 * n:])
        start()
        forward()
        finish()

    bufs = gather_buffers(arrays)
    return pl.pallas_call(
        body, name=name, out_shape=[jax.ShapeDtypeStruct(b.shape, b.dtype) for b in bufs],
        in_specs=[_ANY] * n, out_specs=[_ANY] * n, input_output_aliases={k: k for k in range(n)},
        scratch_shapes=_sems(n, N_DEV - 1),
    )(*bufs)


def assemble(name, slabs):
    _, K, Ns = slabs.shape
    g = N_DEV if Ns % 128 == 0 else 2

    def body(w_ref, o_ref):
        o_ref[...] = jnp.concatenate([w_ref[s] for s in range(g)], axis=1)

    return pl.pallas_call(
        body, name=name, grid=(N_DEV // g,),
        in_specs=[pl.BlockSpec((g, K, Ns), lambda m: (m, 0, 0))],
        out_specs=pl.BlockSpec((K, g * Ns), lambda m: (0, m)),
        out_shape=jax.ShapeDtypeStruct((K, N_DEV * Ns), slabs.dtype),
        compiler_params=_cparams(1),
    )(slabs)


def mm_nn(name, a, b, out_dtype, add=None, tm=1024, tn=512, tk=None):
    M, K = a.shape
    N = b.shape[1]
    tm, tn = min(tm, M), min(tn, N)
    tk = K if tk is None else tk
    nk = K // tk
    has_add = add is not None

    def body(a_ref, b_ref, *rest):
        o_ref = rest[1] if has_add else rest[0]
        part = jnp.dot(a_ref[...].astype(bf16), b_ref[...].astype(bf16), preferred_element_type=f32)
        if nk == 1:
            o_ref[...] = (part + rest[0][...] if has_add else part).astype(o_ref.dtype)
            return
        acc_ref = rest[-1]
        k = pl.program_id(2)

        @pl.when(k == 0)
        def _():
            acc_ref[...] = part + rest[0][...] if has_add else part

        @pl.when(k > 0)
        def _():
            acc_ref[...] += part

        @pl.when(k == nk - 1)
        def _():
            o_ref[...] = acc_ref[...].astype(o_ref.dtype)

    in_specs = [pl.BlockSpec((tm, tk), lambda i, j, k: (i, k)), pl.BlockSpec((tk, tn), lambda i, j, k: (k, j))]
    ops = [a, b]
    if has_add:
        in_specs.append(pl.BlockSpec((tm, tn), lambda i, j, k: (i, j)))
        ops.append(add)
    return pl.pallas_call(
        body, name=name, grid=(M // tm, N // tn, nk), in_specs=in_specs,
        out_specs=pl.BlockSpec((tm, tn), lambda i, j, k: (i, j)),
        out_shape=jax.ShapeDtypeStruct((M, N), out_dtype),
        scratch_shapes=[pltpu.VMEM((tm, tn), f32)] if nk > 1 else [], compiler_params=_cparams(3),
    )(*ops)


def concat_cols(name, pieces, T, tm=512):
    widths = [p.shape[1] for p in pieces]
    total = sum(widths)
    tm = min(tm, T)

    def body(*refs):
        o_ref = refs[-1]
        off = 0
        for r, w in zip(refs[:-1], widths):
            o_ref[:, off:off + w] = r[...]
            off += w

    return pl.pallas_call(
        body, name=name, grid=(T // tm,),
        in_specs=[pl.BlockSpec((tm, w), lambda i: (i, 0)) for w in widths],
        out_specs=pl.BlockSpec((tm, total), lambda i: (i, 0)),
        out_shape=jax.ShapeDtypeStruct((T, total), pieces[0].dtype), compiler_params=_cparams(1),
    )(*pieces)


def mm_nt(name, a, b, out_dtype, tm=1024, tko=None, tc=512):
    M, C = a.shape
    Ko = b.shape[0]
    tm, tc = min(tm, M), min(tc, C)
    tko = Ko if tko is None else tko
    nc = C // tc

    def body(a_ref, b_ref, o_ref, acc_ref):
        c = pl.program_id(2)
        part = lax.dot_general(a_ref[...].astype(bf16), b_ref[...].astype(bf16), (((1,), (1,)), ((), ())),
                               preferred_element_type=f32)

        @pl.when(c == 0)
        def _():
            acc_ref[...] = part

        @pl.when(c > 0)
        def _():
            acc_ref[...] += part

        @pl.when(c == nc - 1)
        def _():
            o_ref[...] = acc_ref[...].astype(o_ref.dtype)

    return pl.pallas_call(
        body, name=name, grid=(M // tm, Ko // tko, nc),
        in_specs=[pl.BlockSpec((tm, tc), lambda i, j, c: (i, c)), pl.BlockSpec((tko, tc), lambda i, j, c: (j, c))],
        out_specs=pl.BlockSpec((tm, tko), lambda i, j, c: (i, j)),
        out_shape=jax.ShapeDtypeStruct((M, Ko), out_dtype),
        scratch_shapes=[pltpu.VMEM((tm, tko), f32)], compiler_params=_cparams(3),
    )(a, b)


def mm_tn(name, a, b, n_slab=None, tk=512, tn=512):
    T, K = a.shape
    N = b.shape[1]
    tk = min(tk, K)
    if n_slab is None:
        tn = min(tn, N)

        def body(a_ref, b_ref, o_ref):
            o_ref[...] = lax.dot_general(a_ref[...].astype(bf16), b_ref[...].astype(bf16), (((0,), (0,)), ((), ())),
                                         preferred_element_type=f32).astype(o_ref.dtype)

        return pl.pallas_call(
            body, name=name, grid=(N // tn, K // tk),
            in_specs=[pl.BlockSpec((T, tk), lambda j, i: (0, i)), pl.BlockSpec((T, tn), lambda j, i: (0, j))],
            out_specs=pl.BlockSpec((tk, tn), lambda j, i: (i, j)),
            out_shape=jax.ShapeDtypeStruct((K, N), bf16), compiler_params=_cparams(2),
        )(a, b)

    Ns = n_slab
    g = N_DEV if Ns % 128 == 0 else 2
    tn = g * Ns

    def body(a_ref, b_ref, o_ref):
        val = lax.dot_general(a_ref[...].astype(bf16), b_ref[...].astype(bf16), (((0,), (0,)), ((), ())),
                              preferred_element_type=f32)
        for s in range(g):
            o_ref[s] = val[:, s * Ns:(s + 1) * Ns].astype(o_ref.dtype)

    return pl.pallas_call(
        body, name=name, grid=(K // tk, N_DEV // g),
        in_specs=[pl.BlockSpec((T, tk), lambda i, j: (0, i)), pl.BlockSpec((T, tn), lambda i, j: (0, j))],
        out_specs=pl.BlockSpec((g, tk, Ns), lambda i, j: (j, i, 0)),
        out_shape=jax.ShapeDtypeStruct((N_DEV, K, Ns), bf16), compiler_params=_cparams(2),
    )(a, b)


def _tile_spec(w, tm, cb):
    if callable(cb):
        return pl.BlockSpec((tm, w), lambda c, i: (i, cb(c)))
    return pl.BlockSpec((tm, w), lambda c, i: (i, cb))


def _param_spec(block, cb):
    nd = len(block)
    if cb is None:
        return pl.BlockSpec(block, lambda c, i: (0,) * nd)
    return pl.BlockSpec(block, lambda c, i: (0,) * (nd - 1) + (cb(c),))


def rowwise(name, fn, tiled, params, outs, T, tm, ncol=1):
    n_in = len(tiled) + len(params)
    n_t = len(tiled)

    def body(*refs):
        ins = [r[...].astype(f32) for r in refs[:n_t]] + [r[...] for r in refs[n_t:n_in]]
        res = fn(*ins)
        for r, o in zip(refs[n_in:], res):
            r[...] = o.astype(r.dtype)

    return pl.pallas_call(
        body, name=name, grid=(ncol, T // tm),
        in_specs=[_tile_spec(w, tm, cb) for _, w, cb in tiled] + [_param_spec(blk, cb) for _, blk, cb in params],
        out_specs=[_tile_spec(w, tm, cb) for _, w, cb, _ in outs],
        out_shape=[jax.ShapeDtypeStruct((T, cols), dt) for cols, _, _, dt in outs],
        compiler_params=_cparams(2),
    )(*[a for a, _, _ in tiled], *[a for a, _, _ in params])


def rowwise_bwd(name, fn, tiled, params, cts, grads, T, tm, ncol=1, adds=None):
    n_t, n_p, n_c = len(tiled), len(params), len(cts)
    adds = adds or [None] * n_t
    add_list = [(k, a) for k, a in enumerate(adds) if a is not None]
    want = [k for k, g in enumerate(grads) if g is not None]
    n_a = len(add_list)

    def body(*refs):
        pos = 0
        t_refs = refs[pos:pos + n_t]; pos += n_t
        p_refs = refs[pos:pos + n_p]; pos += n_p
        c_refs = refs[pos:pos + n_c]; pos += n_c
        a_refs = refs[pos:pos + n_a]; pos += n_a
        g_refs = refs[pos:pos + len(want)]; pos += len(want)
        pg_refs = refs[pos:pos + n_p]
        primals = [r[...].astype(f32) for r in t_refs] + [r[...] for r in p_refs]
        _, vjp = jax.vjp(fn, *primals)
        g = vjp(tuple(r[...].astype(f32) for r in c_refs))
        add_of = {k: a_refs[n][...] for n, (k, _) in enumerate(add_list)}
        for n, k in enumerate(want):
            val = g[k]
            if k in add_of:
                val = val + add_of[k]
            g_refs[n][...] = val.astype(g_refs[n].dtype)
        i = pl.program_id(1)
        for k in range(n_p):
            @pl.when(i == 0)
            def _(k=k):
                pg_refs[k][...] = g[n_t + k]

            @pl.when(i > 0)
            def _(k=k):
                pg_refs[k][...] += g[n_t + k]

    in_specs = ([_tile_spec(w, tm, cb) for _, w, cb in tiled] + [_param_spec(blk, cb) for _, blk, cb in params]
                + [_tile_spec(w, tm, cb) for _, w, cb in cts] + [_tile_spec(w, tm, cb) for _, (_, w, cb) in add_list])
    ops = ([a for a, _, _ in tiled] + [a for a, _, _ in params] + [a for a, _, _ in cts]
           + [a for _, (a, _, _) in add_list])
    out_specs, out_shape = [], []
    for k in want:
        w = tiled[k][1]
        out_specs.append(_tile_spec(w, tm, lambda c: c))
        out_shape.append(jax.ShapeDtypeStruct((T, ncol * w), grads[k]))
    for a, blk, cb in params:
        out_specs.append(_param_spec(blk, cb))
        out_shape.append(jax.ShapeDtypeStruct(a.shape, f32))
    return pl.pallas_call(
        body, name=name, grid=(ncol, T // tm), in_specs=in_specs, out_specs=out_specs, out_shape=out_shape,
        compiler_params=_cparams(2),
    )(*ops)


@jax.custom_vjp
def _bdot(a, b):
    return jnp.dot(a.astype(bf16), b.astype(bf16), preferred_element_type=f32)


def _bdot_fwd(a, b):
    return _bdot(a, b), (a, b)


def _bdot_bwd(res, ct):
    a, b = res
    ctb = ct.astype(bf16)
    da = lax.dot_general(ctb, b.astype(bf16), (((1,), (1,)), ((), ())), preferred_element_type=f32)
    db = lax.dot_general(a.astype(bf16), ctb, (((0,), (0,)), ((), ())), preferred_element_type=f32)
    return da, db


_bdot.defvjp(_bdot_fwd, _bdot_bwd)


def _layer_norm(x, g, b):
    mu = jnp.mean(x, axis=-1, keepdims=True)
    xc = x - mu
    y = xc * lax.rsqrt(jnp.mean(xc * xc, axis=-1, keepdims=True) + EPS)
    return y * g + b


def f_rms(x, g):
    y = x * lax.rsqrt(jnp.mean(x * x, axis=-1, keepdims=True) + EPS)
    return (y * g,)


def f_sgu(zu, zv, ln_g, ln_b, wm, sgb_t):
    u = jax.nn.gelu(zu)
    vn = _layer_norm(jax.nn.gelu(zv), ln_g, ln_b)
    row = lax.broadcasted_iota(jnp.int32, (CHUNK, CHUNK), 0)
    col = lax.broadcasted_iota(jnp.int32, (CHUNK, CHUNK), 1)
    tril = col <= row
    low = col < HEAD_DIM
    parts = []
    for p in range(4):
        vp = vn[:, CHUNK * p:CHUNK * (p + 1)]
        w0 = jnp.where(tril, wm[2 * p], 0.0)
        w1 = jnp.where(tril, wm[2 * p + 1], 0.0)
        parts.append(_bdot(w0, jnp.where(low, vp, 0.0)) + _bdot(w1, jnp.where(low, 0.0, vp)))
    s = jnp.concatenate(parts, axis=1)
    lane_g = lax.shift_right_logical(lax.broadcasted_iota(jnp.int32, s.shape, 1), 6)
    bias = jnp.zeros_like(s)
    for g in range(8):
        bias = jnp.where(lane_g == g, sgb_t[:, g:g + 1], bias)
    return (u * (s + bias),)


def f_glu(p, gl):
    return (p * jax.nn.sigmoid(gl),)


def f_lnsilu(c1, g, b):
    return (jax.nn.silu(_layer_norm(c1, g, b)),)


def _head_norm(x, g64):
    g = jnp.concatenate([g64] * N_HEADS, axis=1)
    lane_h = lax.shift_right_logical(lax.broadcasted_iota(jnp.int32, x.shape, 1), 6)
    x2 = x * x
    r = jnp.zeros_like(x)
    for h in range(N_HEADS):
        mh = lane_h == h
        ms = jnp.sum(jnp.where(mh, x2, 0.0), axis=-1, keepdims=True) * (1.0 / HEAD_DIM)
        r = jnp.where(mh, lax.rsqrt(ms + EPS), r)
    return (x * r) * g


def f_qkv(zq, zk, zv, qg, kg):
    return (_head_norm(zq, qg) * 0.125, _head_norm(zk, kg), zv)


def f_merge(g0, g1, g2, ya, yb, yc, b0, b1, b2):
    return (jax.nn.sigmoid(g0 + b0) * ya + jax.nn.sigmoid(g1 + b1) * yb + jax.nn.sigmoid(g2 + b2) * yc,)


HALO = 32


def conv31_fwd(x, w, b, T, tt=256):
    C = x.shape[1]
    r = tt // HALO

    def body(x_ref, h_ref, w_ref, b_ref, y_ref, buf):
        i = pl.program_id(0)
        halo = h_ref[...]
        buf[0:HALO, :] = jnp.where(i > 0, halo, jnp.zeros_like(halo))
        buf[HALO:HALO + tt, :] = x_ref[...]
        acc = jnp.zeros((tt, C), f32) + b_ref[...]
        for k in range(CV_KERNEL):
            acc = acc + w_ref[k:k + 1, :] * buf[pl.ds(HALO - (CV_KERNEL - 1) + k, tt), :]
        y_ref[...] = acc

    return pl.pallas_call(
        body, name="conv31_fwd", grid=(T // tt,),
        in_specs=[pl.BlockSpec((tt, C), lambda i: (i, 0)),
                  pl.BlockSpec((HALO, C), lambda i: (jnp.maximum(i * r - 1, 0), 0)),
                  pl.BlockSpec((CV_KERNEL, C), lambda i: (0, 0)), pl.BlockSpec((1, C), lambda i: (0, 0))],
        out_specs=pl.BlockSpec((tt, C), lambda i: (i, 0)),
        out_shape=jax.ShapeDtypeStruct((T, C), f32),
        scratch_shapes=[pltpu.VMEM((HALO + tt, C), f32)], compiler_params=_cparams(1),
    )(x, x, w, b)


def conv31_bwd(x, w, dy, T, tt=256):
    C = x.shape[1]
    r = tt // HALO
    n = T // tt

    def body(x_ref, h_ref, w_ref, dy_ref, dyn_ref, dx_ref, dw_ref, db_ref, xbuf, dbuf):
        i = pl.program_id(0)
        halo = h_ref[...]
        xbuf[0:HALO, :] = jnp.where(i > 0, halo, jnp.zeros_like(halo))
        xbuf[HALO:HALO + tt, :] = x_ref[...]
        nxt = dyn_ref[...]
        dy = dy_ref[...]
        dbuf[0:tt, :] = dy
        dbuf[tt:tt + HALO, :] = jnp.where(i < n - 1, nxt, jnp.zeros_like(nxt))

        @pl.when(i == 0)
        def _():
            dw_ref[...] = jnp.zeros_like(dw_ref)
            db_ref[...] = jnp.zeros_like(db_ref)

        acc = jnp.zeros((tt, C), f32)
        for k in range(CV_KERNEL):
            acc = acc + w_ref[k:k + 1, :] * dbuf[pl.ds(CV_KERNEL - 1 - k, tt), :]
            xs = xbuf[pl.ds(HALO - (CV_KERNEL - 1) + k, tt), :]
            dw_ref[k:k + 1, :] += jnp.sum(dy * xs, axis=0, keepdims=True)
        dx_ref[...] = acc
        db_ref[...] += jnp.sum(dy, axis=0, keepdims=True)

    return pl.pallas_call(
        body, name="conv31_bwd", grid=(n,),
        in_specs=[pl.BlockSpec((tt, C), lambda i: (i, 0)),
                  pl.BlockSpec((HALO, C), lambda i: (jnp.maximum(i * r - 1, 0), 0)),
                  pl.BlockSpec((CV_KERNEL, C), lambda i: (0, 0)),
                  pl.BlockSpec((tt, C), lambda i: (i, 0)),
                  pl.BlockSpec((HALO, C), lambda i: (jnp.minimum((i + 1) * r, n * r - 1), 0))],
        out_specs=[pl.BlockSpec((tt, C), lambda i: (i, 0)), pl.BlockSpec((CV_KERNEL, C), lambda i: (0, 0)),
                   pl.BlockSpec((1, C), lambda i: (0, 0))],
        out_shape=[jax.ShapeDtypeStruct((T, C), f32), jax.ShapeDtypeStruct((CV_KERNEL, C), f32),
                   jax.ShapeDtypeStruct((1, C), f32)],
        scratch_shapes=[pltpu.VMEM((HALO + tt, C), f32), pltpu.VMEM((HALO + tt, C), f32)], compiler_params=_cparams(1),
    )(x, x, w, dy, dy)


FFN_TC = 128


def _shift_down(x, s, row):
    return jnp.where(row >= s, pltpu.roll(x, s, 0), 0.0)


def _shift_up(x, s, row, T):
    return jnp.where(row < T - s, pltpu.roll(x, T - s, 0), 0.0)


def _conv3(x, w0, w1, w2, b, row):
    return w0 * _shift_down(x, 2, row) + w1 * _shift_down(x, 1, row) + w2 * x + b


def ffn_act_fwd(up, cw, cb, T):
    F = up.shape[1] // 2
    nj = F // FFN_TC

    def body(g_ref, v_ref, g0, g1, g2, gb, v0, v1, v2, vb, o_ref):
        row = lax.broadcasted_iota(jnp.int32, (T, FFN_TC), 0)
        gc = _conv3(g_ref[...], g0[...], g1[...], g2[...], gb[...], row)
        vc = _conv3(v_ref[...], v0[...], v1[...], v2[...], vb[...], row)
        o_ref[...] = (jax.nn.silu(gc) * vc).astype(o_ref.dtype)

    gspec = pl.BlockSpec((T, FFN_TC), lambda j: (0, j))
    vspec = pl.BlockSpec((T, FFN_TC), lambda j: (0, j + nj))
    pg = pl.BlockSpec((1, FFN_TC), lambda j: (0, j))
    pv = pl.BlockSpec((1, FFN_TC), lambda j: (0, j + nj))
    return pl.pallas_call(
        body, name="ffn_act_fwd", grid=(nj,),
        in_specs=[gspec, vspec, pg, pg, pg, pg, pv, pv, pv, pv], out_specs=gspec,
        out_shape=jax.ShapeDtypeStruct((T, F), bf16), compiler_params=_cparams(1),
    )(up, up, cw[0], cw[1], cw[2], cb, cw[0], cw[1], cw[2], cb)


def ffn_act_bwd(up, cw, cb, dact, T):
    F = up.shape[1] // 2
    nj = F // FFN_TC

    def side_grads(dc, x, w0, w1, w2, row, dup_ref, p_refs):
        dup = w2 * dc + w1 * _shift_up(dc, 1, row, T) + w0 * _shift_up(dc, 2, row, T)
        dup_ref[...] = dup.astype(dup_ref.dtype)
        p_refs[0][...] = jnp.sum(dc * _shift_down(x, 2, row), axis=0, keepdims=True)
        p_refs[1][...] = jnp.sum(dc * _shift_down(x, 1, row), axis=0, keepdims=True)
        p_refs[2][...] = jnp.sum(dc * x, axis=0, keepdims=True)
        p_refs[3][...] = jnp.sum(dc, axis=0, keepdims=True)

    def body(g_ref, v_ref, g0, g1, g2, gb, v0, v1, v2, vb, d_ref, dupg_ref, dupv_ref, *p_refs):
        row = lax.broadcasted_iota(jnp.int32, (T, FFN_TC), 0)
        xg, xv = g_ref[...], v_ref[...]
        gc = _conv3(xg, g0[...], g1[...], g2[...], gb[...], row)
        vc = _conv3(xv, v0[...], v1[...], v2[...], vb[...], row)
        d = d_ref[...]
        sg = jax.nn.sigmoid(gc)
        side_grads(d * vc * (sg * (1.0 + gc * (1.0 - sg))), xg, g0[...], g1[...], g2[...], row, dupg_ref, p_refs[:4])
        side_grads(d * (gc * sg), xv, v0[...], v1[...], v2[...], row, dupv_ref, p_refs[4:])

    gspec = pl.BlockSpec((T, FFN_TC), lambda j: (0, j))
    vspec = pl.BlockSpec((T, FFN_TC), lambda j: (0, j + nj))
    pg = pl.BlockSpec((1, FFN_TC), lambda j: (0, j))
    pv = pl.BlockSpec((1, FFN_TC), lambda j: (0, j + nj))
    res = pl.pallas_call(
        body, name="ffn_act_bwd", grid=(nj,),
        in_specs=[gspec, vspec, pg, pg, pg, pg, pv, pv, pv, pv, gspec],
        out_specs=[gspec, gspec] + [pg] * 8,
        out_shape=[jax.ShapeDtypeStruct((T, F), bf16)] * 2 + [jax.ShapeDtypeStruct((1, F), f32)] * 8,
        compiler_params=_cparams(1),
    )(up, up, cw[0], cw[1], cw[2], cb, cw[0], cw[1], cw[2], cb, dact)
    dup = concat_cols("dup_concat", [res[0], res[1]], T)
    return (dup,) + tuple(jnp.concatenate([res[2 + k], res[6 + k]], axis=1) for k in range(4))


BQ = 256
BK = 256
assert BQ % BK == 0
NT_DIMS = (((1,), (1,)), ((), ()))
TN_DIMS = (((0,), (0,)), ((), ()))


def _split_dot(x, u):
    x1 = x.astype(bf16)
    x2 = (x - x1.astype(f32)).astype(bf16)
    n = x.shape[0]
    y = jnp.dot(jnp.concatenate([x1, x2], axis=0), u, preferred_element_type=f32)
    return y[0:n] + y[n:2 * n]


def _log_sigmoids(z):
    sp = jnp.log(1.0 + jnp.exp(-jnp.abs(z)))
    lsp = jnp.minimum(z, 0.0) - sp
    return lsp, lsp - z


def _stack_heads(x):
    head1 = lax.broadcasted_iota(jnp.int32, x.shape, 1) >= HEAD_DIM
    zero = jnp.zeros_like(x)
    return jnp.concatenate([jnp.where(head1, zero, x), jnp.where(head1, x, zero)], axis=0)


def _unstack_heads(y):
    head1 = lax.broadcasted_iota(jnp.int32, (BQ, y.shape[1]), 1) >= HEAD_DIM
    return jnp.where(head1, y[BQ:2 * BQ], y[0:BQ])


def last_block(i):
    return ((i + 1) * BQ - 1) // BK


def _attn_masks():
    row = lax.broadcasted_iota(jnp.int32, (2 * BQ, BK), 0)
    col = lax.broadcasted_iota(jnp.int32, (2 * BQ, BK), 1)
    ur = lax.broadcasted_iota(jnp.int32, (BK, BK), 0)
    uc = lax.broadcasted_iota(jnp.int32, (BK, BK), 1)
    return (row & (BQ - 1)) - col, (ur > uc).astype(bf16), (ur < uc).astype(bf16)


def attn_fwd(q, k, v, T, gather=None):
    nq = T // BQ
    n_g = 0 if gather is None else len(gather)

    def body(*refs):
        q_ref, k_ref, v_ref = refs[:3]
        o_ref, tot_ref = refs[3 + n_g:5 + n_g]
        p, i = pl.program_id(0), pl.program_id(1)
        if n_g:
            start, forward, finish = _two_level_gather(refs[5 + n_g:5 + 2 * n_g], *refs[5 + 2 * n_g:])
            pl.when(jnp.logical_and(p == 0, i == 0))(start)
        qs = _stack_heads(q_ref[...])
        diff, u_after, _ = _attn_masks()

        def step(t, carry):
            acc, c = carry
            jb = last_block(i) - t
            ks = pl.multiple_of(jb * BK, BK)
            kb = k_ref[pl.ds(ks, BK), :]
            vb = v_ref[pl.ds(ks, BK), :]
            m = diff > jb * BK - i * BQ
            z = lax.dot_general(qs, kb, NT_DIMS, preferred_element_type=f32)
            lsp, lsn = _log_sigmoids(z)
            lm = jnp.where(m, lsn, 0.0)
            a = jnp.where(m, jnp.exp(lsp + _split_dot(lm, u_after)), 0.0)
            acc = acc + jnp.exp(c) * jnp.dot(a.astype(bf16), vb, preferred_element_type=f32)
            return acc, c + jnp.sum(lm, axis=-1, keepdims=True)

        acc, c = lax.fori_loop(0, last_block(i) + 1, step, (jnp.zeros((2 * BQ, 128), f32), jnp.zeros((2 * BQ, 1), f32)))
        o_ref[...] = _unstack_heads(acc).astype(o_ref.dtype)
        tot_ref[...] = _unstack_heads(jnp.broadcast_to(c, (2 * BQ, 128)))
        if n_g:
            @pl.when(jnp.logical_and(p == 3, i == nq - 1))
            def _():
                forward()
                finish()

    blk = pl.BlockSpec((BQ, 128), lambda p, i: (i, p))
    full = pl.BlockSpec((T, 128), lambda p, i: (0, p))
    bufs = [] if gather is None else gather_buffers(gather)
    res = pl.pallas_call(
        body, name="attn_fwd_gather" if n_g else "attn_fwd", grid=(4, nq),
        in_specs=[blk, full, full] + [_ANY] * n_g, out_specs=[blk, blk] + [_ANY] * n_g,
        out_shape=[jax.ShapeDtypeStruct((T, 512), bf16), jax.ShapeDtypeStruct((T, 512), f32)]
        + [jax.ShapeDtypeStruct(b.shape, b.dtype) for b in bufs],
        input_output_aliases={3 + k: 2 + k for k in range(n_g)},
        scratch_shapes=_sems(n_g, N_DEV - 1) if n_g else [], compiler_params=_cparams(2),
    )(q, k, v, *bufs)
    return res[0], res[1], list(res[2:])


def attn_bwd(q, k, v, do, tot, T, scatter=None):
    nq = T // BQ
    n_s = 0 if scatter is None else len(scatter[0])
    modes = [] if scatter is None else list(scatter[1])

    def body(*refs):
        q_ref, k_ref, v_ref, do_ref, tot_ref = refs[:5]
        dq_ref, dk_ref, dv_ref = refs[5 + 2 * n_s:8 + 2 * n_s]
        p, i = pl.program_id(0), pl.program_id(1)
        if n_s:
            def copies():
                return _chip_copies(refs[5:5 + n_s], refs[8 + 2 * n_s:8 + 3 * n_s], modes, *refs[8 + 3 * n_s:])

            @pl.when(jnp.logical_and(p == 0, i == 0))
            def _():
                for cp in copies():
                    cp.start()

        @pl.when(i == 0)
        def _():
            dk_ref[...] = jnp.zeros_like(dk_ref)
            dv_ref[...] = jnp.zeros_like(dv_ref)

        qs = _stack_heads(q_ref[...])
        dos = _stack_heads(do_ref[...].astype(bf16))
        totv = tot_ref[...]
        tots = jnp.concatenate([totv[:, 0:1], totv[:, HEAD_DIM:HEAD_DIM + 1]], axis=0)
        diff, u_after, u_before = _attn_masks()

        def step(jb, carry):
            dq, cl, cg = carry
            ks = pl.multiple_of(jb * BK, BK)
            kb = k_ref[pl.ds(ks, BK), :]
            vb = v_ref[pl.ds(ks, BK), :]
            m = diff > jb * BK - i * BQ
            z = lax.dot_general(qs, kb, NT_DIMS, preferred_element_type=f32)
            lsp, lsn = _log_sigmoids(z)
            lm = jnp.where(m, lsn, 0.0)
            a = jnp.where(m, jnp.exp(lsp + _split_dot(lm, u_after)), 0.0)
            g = a * lax.dot_general(dos, vb, NT_DIMS, preferred_element_type=f32)
            beta = jnp.exp(lsp)
            aa = jnp.where(m, g * jnp.exp(lsn) - _split_dot(g, u_before) * beta, 0.0)
            bb = jnp.where(m, beta, 0.0)
            cl = cl + jnp.sum(lm, axis=-1, keepdims=True)
            f = jnp.exp(tots - cl)
            dz = (f * aa - cg * bb).astype(bf16)
            cg = cg + f * jnp.sum(g, axis=-1, keepdims=True)
            dq = dq + jnp.dot(dz, kb, preferred_element_type=f32)
            dk_ref[pl.ds(ks, BK), :] += lax.dot_general(dz, qs, TN_DIMS, preferred_element_type=f32)
            dv_ref[pl.ds(ks, BK), :] += lax.dot_general((f * a).astype(bf16), dos, TN_DIMS, preferred_element_type=f32)
            return dq, cl, cg

        zc = jnp.zeros((2 * BQ, 1), f32)
        dq, _, _ = lax.fori_loop(0, last_block(i) + 1, step, (jnp.zeros((2 * BQ, 128), f32), zc, zc))
        dq_ref[...] = _unstack_heads(dq)
        if n_s:
            @pl.when(jnp.logical_and(p == 3, i == nq - 1))
            def _():
                for cp in copies():
                    cp.wait()

    blk = pl.BlockSpec((BQ, 128), lambda p, i: (i, p))
    full = pl.BlockSpec((T, 128), lambda p, i: (0, p))
    shp = jax.ShapeDtypeStruct((T, 512), f32)
    s_list = [] if scatter is None else list(scatter[0])
    bufs = chip_buffers(s_list, modes) if n_s else []
    res = pl.pallas_call(
        body, name="attn_bwd_scatter" if n_s else "attn_bwd", grid=(4, nq),
        in_specs=[blk, full, full, blk, blk] + [_ANY] * (2 * n_s), out_specs=[blk, full, full] + [_ANY] * n_s,
        out_shape=[shp, shp, shp] + [jax.ShapeDtypeStruct(b.shape, b.dtype) for b in bufs],
        input_output_aliases={5 + n_s + k: 3 + k for k in range(n_s)},
        scratch_shapes=_sems(n_s, 3) if n_s else [], compiler_params=_cparams(2),
    )(q, k, v, do, tot, *s_list, *bufs)
    return res[0], res[1], res[2], list(res[3:])


def loss_head(y, target, T, tm=256):
    D = y.shape[1]

    def body(y_ref, t_ref, dy_ref, l_ref):
        i = pl.program_id(0)
        err = y_ref[...] - t_ref[...]
        dy_ref[...] = err * (1.0 / D)
        part = 0.5 * jnp.sum(jnp.sum(err * err, axis=-1, keepdims=True) * (1.0 / D), axis=0, keepdims=True)

        @pl.when(i == 0)
        def _():
            l_ref[...] = jnp.zeros_like(l_ref)

        l_ref[...] += jnp.broadcast_to(part, l_ref.shape)

    spec = pl.BlockSpec((tm, D), lambda i: (i, 0))
    return pl.pallas_call(
        body, name="loss_head", grid=(T // tm,), in_specs=[spec, spec],
        out_specs=[spec, pl.BlockSpec((1, 128), lambda i: (0, 0))],
        out_shape=[jax.ShapeDtypeStruct((T, D), f32), jax.ShapeDtypeStruct((1, 128), f32)],
        compiler_params=_cparams(1),
    )(y, target)


def _row_tile(rows, cols, offset=0, max_elems=128 * 1024):
    best = None
    for t in range(16, rows + 1, 16):
        if rows % t == 0 and offset % t == 0 and t * cols <= max_elems:
            best = t
    return best if best is not None else rows


def adamw(name, w, m, v, parts, layer, prev=None, row_off=0):
    L, R, C = w.shape
    n_parts = parts.shape[0]
    tr = _row_tile(R, C, row_off, max_elems=256 * 1024)
    assert R % tr == 0 and row_off % tr == 0, (name, R, row_off, tr)

    def body(w_ref, m_ref, v_ref, p_ref, *rest):
        g_ref, d_ref, nm_ref, nv_ref = rest[-4:]
        g = p_ref[0].astype(f32)
        for s in range(1, n_parts):
            g = g + p_ref[s].astype(f32)
        wv = w_ref[...]
        mn = ADAM_B1 * m_ref[...] + (1.0 - ADAM_B1) * g
        vn = ADAM_B2 * v_ref[...] + (1.0 - ADAM_B2) * jnp.square(g)
        m_hat = mn / (1.0 - ADAM_B1 ** ADAM_STEP)
        v_hat = vn / (1.0 - ADAM_B2 ** ADAM_STEP)
        g_ref[...] = g
        d_ref[...] = -ADAM_LR * (m_hat / (jnp.sqrt(v_hat) + ADAM_EPS) + ADAM_WD * wv)
        nm_ref[...] = mn
        nv_ref[...] = vn

    spec = pl.BlockSpec((None, tr, C), lambda i: (layer, i, 0))
    shp = jax.ShapeDtypeStruct((L, R, C), f32)
    n_prev = 0 if prev is None else 4
    return pl.pallas_call(
        body, name=name, grid=(R // tr,),
        in_specs=[spec, spec, spec, pl.BlockSpec((n_parts, tr, C), lambda i: (0, row_off // tr + i, 0))] + [_ANY] * n_prev,
        out_specs=[spec] * 4, out_shape=[shp] * 4, input_output_aliases={4 + k: k for k in range(n_prev)},
        compiler_params=_cparams(1),
    )(w, m, v, parts, *(prev or ()))


REPL = ["ln1_g", "sg_ln_g", "sg_ln_b", "sg_w", "sg_b", "cv_b", "cv_ln_g", "cv_ln_b", "q_norm_g", "k_norm_g", "ln2_g",
        "ffn_conv_b"]
SMALL_SHARDED = ["b_gate", "cv_w", "ffn_conv_w"]
BIG = ["w_in", "w_a_out", "w_b_out", "w_c_out", "w_up", "w_out", "w_down"]
COL_SHARDED = ["w_a_out", "w_b_out", "w_c_out"]
TRANSPOSED = ["w_in", "w_up"]


def _pack(arrs, rows):
    flat = jnp.concatenate([a.reshape(-1) for a in arrs])
    return jnp.pad(flat, (0, rows * 128 - flat.shape[0])).reshape(rows, 128)


def _pack_layers(arrs, rows):
    flat = jnp.concatenate([a.reshape(a.shape[0], -1) for a in arrs], axis=1)
    return jnp.pad(flat, ((0, 0), (0, rows * 128 - flat.shape[1]))).reshape(flat.shape[0], rows, 128)


def _unpack_layers(packed, shapes):
    flat = packed.reshape(packed.shape[0], -1)
    out, pos = [], 0
    for s in shapes:
        n = 1
        for d in s[1:]:
            n *= d
        out.append(flat[:, pos:pos + n].reshape(s))
        pos += n
    return out


def _rows_for(shapes, mult):
    n = 0
    for s in shapes:
        k = 1
        for d in s:
            k *= d
        n += k
    rows = -(-n // 128)
    return -(-rows // mult) * mult


W_NAMES = ['ln1_g', 'w_in', 'b_gate', 'sg_ln_g', 'sg_ln_b', 'sg_w', 'sg_b', 'w_a_out', 'cv_w', 'cv_b', 'cv_ln_g', 'cv_ln_b',
           'w_b_out', 'q_norm_g', 'k_norm_g', 'w_c_out', 'w_out', 'ln2_g', 'w_up', 'ffn_conv_w', 'ffn_conv_b', 'w_down']


def _forward_layer(x, P, T, gather=None):
    D = x.shape[1]
    sv = {"x0": x}
    (h1,) = rowwise("rms_fwd", f_rms, [(x, D, 0)], [(P["ln1_g"], (1, D), None)], [(D, D, 0, bf16)], T, 256)
    z = mm_nt("in_proj", h1, P["w_in_t"], f32, tko=1664, tc=D)
    sv["h1"], sv["z"] = h1, z
    (ya_in,) = rowwise("sgu_fwd", f_sgu, [(z, 512, 0), (z, 512, 1)],
                       [(P["sg_ln_g"], (1, 512), None), (P["sg_ln_b"], (1, 512), None),
                        (P["sg_w"], (8, CHUNK, CHUNK), None), (P["sg_b_t"], (CHUNK, 8), None)],
                       [(512, 512, 0, bf16)], T, CHUNK)
    ya = mm_nn("a_out", ya_in, P["w_a_out"], f32)
    (c0,) = rowwise("glu_fwd", f_glu, [(z, 512, 2), (z, 512, 3)], [], [(512, 512, 0, f32)], T, 256)
    c1 = conv31_fwd(c0, P["cv_w"], P["cv_b"], T)
    (c3,) = rowwise("lnsilu_fwd", f_lnsilu, [(c1, 512, 0)], [(P["cv_ln_g"], (1, 512), None), (P["cv_ln_b"], (1, 512), None)],
                    [(512, 512, 0, bf16)], T, 256)
    yb = mm_nn("b_out", c3, P["w_b_out"], f32)
    q8, kn, vb = rowwise("qkv_fwd", f_qkv, [(z, 512, 4), (z, 512, 5), (z, 512, 6)],
                         [(P["q_norm_g"], (1, HEAD_DIM), None), (P["k_norm_g"], (1, HEAD_DIM), None)],
                         [(512, 512, 0, bf16)] * 3, T, 256)
    o, tot, gathered = attn_fwd(q8, kn, vb, T, gather=gather)
    yc = mm_nn("c_out", o, P["w_c_out"], f32)
    (merged,) = rowwise("merge_fwd", f_merge,
                        [(z, 512, lambda c: 7 + c), (z, 512, lambda c: 9 + c), (z, 512, lambda c: 11 + c),
                         (ya, 512, lambda c: c), (yb, 512, lambda c: c), (yc, 512, lambda c: c)],
                        [(P["b_gate"][k], (1, 512), lambda c: c) for k in range(3)],
                        [(D, 512, lambda c: c, bf16)], T, 256, ncol=2)
    x1 = mm_nn("out_proj", merged, P["w_out"], f32, add=x)
    sv.update(ya_in=ya_in, ya=ya, c0=c0, c1=c1, c3=c3, yb=yb, q8=q8, kn=kn, vb=vb, o=o, tot=tot, yc=yc, merged=merged, x1=x1)
    (h2,) = rowwise("rms_fwd", f_rms, [(x1, D, 0)], [(P["ln2_g"], (1, D), None)], [(D, D, 0, bf16)], T, 256)
    up = mm_nt("up_proj", h2, P["w_up_t"], f32, tko=1408, tc=D)
    act = ffn_act_fwd(up, P["ffn_conv_w"], P["ffn_conv_b"], T)
    x2 = mm_nn("down_proj", act, P["w_down"], f32, add=x1)
    sv.update(h2=h2, up=up, act=act)
    return x2, sv, gathered


def _backward_layer(dx2, P, sv, T, scatter=None):
    D = dx2.shape[1]
    G = {}
    G["w_down"] = mm_tn("dw_down", sv["act"], dx2, tk=1408)
    dact = mm_nt("d_act", dx2, P["w_down"], f32, tko=1408, tc=D)
    dup, dcw0, dcw1, dcw2, G["ffn_conv_b"] = ffn_act_bwd(sv["up"], P["ffn_conv_w"], P["ffn_conv_b"], dact, T)
    G["ffn_conv_w"] = jnp.concatenate([dcw0, dcw1, dcw2], axis=0)
    G["w_up"] = mm_tn("dw_up", dup, sv["h2"], tk=1408)
    dh2 = mm_nn("d_h2", dup, P["w_up_t"], f32, tk=1408)
    dx1, G["ln2_g"] = rowwise_bwd("rms_bwd", f_rms, [(sv["x1"], D, 0)], [(P["ln2_g"], (1, D), None)], [(dh2, D, 0)],
                                  [f32], T, 256, adds=[(dx2, D, 0)])
    G["w_out"] = mm_tn("dw_out", sv["merged"], dx1, tk=1024)
    dmerged = mm_nt("d_merged", dx1, P["w_out"], f32)
    z = sv["z"]
    dg0, dg1, dg2, dya, dyb, dyc, db0, db1, db2 = rowwise_bwd(
        "merge_bwd", f_merge,
        [(z, 512, lambda c: 7 + c), (z, 512, lambda c: 9 + c), (z, 512, lambda c: 11 + c),
         (sv["ya"], 512, lambda c: c), (sv["yb"], 512, lambda c: c), (sv["yc"], 512, lambda c: c)],
        [(P["b_gate"][k], (1, 512), lambda c: c) for k in range(3)],
        [(dmerged, 512, lambda c: c)], [bf16] * 6, T, 256, ncol=2)
    G["b_gate"] = jnp.concatenate([db0, db1, db2], axis=0)
    G["w_c_out"] = mm_tn("dw_c_out", sv["o"], dyc, n_slab=CHUNK)
    do = mm_nt("d_o", dyc, P["w_c_out"], f32)
    dq8, dkn, dvb, received = attn_bwd(sv["q8"], sv["kn"], sv["vb"], do, sv["tot"], T, scatter=scatter)
    dzq, dzk, dzv, G["q_norm_g"], G["k_norm_g"] = rowwise_bwd(
        "qkv_bwd", f_qkv, [(z, 512, 4), (z, 512, 5), (z, 512, 6)],
        [(P["q_norm_g"], (1, HEAD_DIM), None), (P["k_norm_g"], (1, HEAD_DIM), None)],
        [(dq8, 512, 0), (dkn, 512, 0), (dvb, 512, 0)], [bf16] * 3, T, 256)
    G["w_b_out"] = mm_tn("dw_b_out", sv["c3"], dyb, n_slab=CHUNK)
    dc3 = mm_nt("d_c3", dyb, P["w_b_out"], f32)
    dc1, G["cv_ln_g"], G["cv_ln_b"] = rowwise_bwd(
        "lnsilu_bwd", f_lnsilu, [(sv["c1"], 512, 0)], [(P["cv_ln_g"], (1, 512), None), (P["cv_ln_b"], (1, 512), None)],
        [(dc3, 512, 0)], [f32], T, 256)
    dc0, G["cv_w"], G["cv_b"] = conv31_bwd(sv["c0"], P["cv_w"], dc1, T)
    dzp, dzgl = rowwise_bwd("glu_bwd", f_glu, [(z, 512, 2), (z, 512, 3)], [], [(dc0, 512, 0)], [bf16] * 2, T, 256)
    G["w_a_out"] = mm_tn("dw_a_out", sv["ya_in"], dya, n_slab=CHUNK)
    dya_in = mm_nt("d_ya_in", dya, P["w_a_out"], f32)
    dzu, dzv_a, G["sg_ln_g"], G["sg_ln_b"], G["sg_w"], dsbt = rowwise_bwd(
        "sgu_bwd", f_sgu, [(z, 512, 0), (z, 512, 1)],
        [(P["sg_ln_g"], (1, 512), None), (P["sg_ln_b"], (1, 512), None), (P["sg_w"], (8, CHUNK, CHUNK), None),
         (P["sg_b_t"], (CHUNK, 8), None)],
        [(dya_in, 512, 0)], [bf16] * 2, T, CHUNK)
    G["sg_b"] = dsbt.T
    dz = concat_cols("dz_concat", [dzu, dzv_a, dzp, dzgl, dzq, dzk, dzv, dg0, dg1, dg2], T)
    G["w_in"] = mm_tn("dw_in", dz, sv["h1"], tk=1664)
    dh1 = mm_nn("d_h1", dz, P["w_in_t"], f32, tk=1664)
    dx0, G["ln1_g"] = rowwise_bwd("rms_bwd", f_rms, [(sv["x0"], D, 0)], [(P["ln1_g"], (1, D), None)], [(dh1, D, 0)],
                                  [f32], T, 256, adds=[(dx1, D, 0)])
    return dx0, G, received


def kernel(x, ln1_g, w_in, b_gate, sg_ln_g, sg_ln_b, sg_w, sg_b, w_a_out, cv_w, cv_b, cv_ln_g, cv_ln_b, w_b_out, q_norm_g, k_norm_g, w_c_out, w_out, ln2_g, w_up, ffn_conv_w, ffn_conv_b, w_down, loss_target, m_ln1_g, m_w_in, m_b_gate, m_sg_ln_g, m_sg_ln_b, m_sg_w, m_sg_b, m_w_a_out, m_cv_w, m_cv_b, m_cv_ln_g, m_cv_ln_b, m_w_b_out, m_q_norm_g, m_k_norm_g, m_w_c_out, m_w_out, m_ln2_g, m_w_up, m_ffn_conv_w, m_ffn_conv_b, m_w_down, v_ln1_g, v_w_in, v_b_gate, v_sg_ln_g, v_sg_ln_b, v_sg_w, v_sg_b, v_w_a_out, v_cv_w, v_cv_b, v_cv_ln_g, v_cv_ln_b, v_w_b_out, v_q_norm_g, v_k_norm_g, v_w_c_out, v_w_out, v_ln2_g, v_w_up, v_ffn_conv_w, v_ffn_conv_b, v_w_down):
    W = dict(ln1_g=ln1_g, w_in=w_in, b_gate=b_gate, sg_ln_g=sg_ln_g, sg_ln_b=sg_ln_b, sg_w=sg_w, sg_b=sg_b, w_a_out=w_a_out,
             cv_w=cv_w, cv_b=cv_b, cv_ln_g=cv_ln_g, cv_ln_b=cv_ln_b, w_b_out=w_b_out, q_norm_g=q_norm_g, k_norm_g=k_norm_g,
             w_c_out=w_c_out, w_out=w_out, ln2_g=ln2_g, w_up=w_up, ffn_conv_w=ffn_conv_w, ffn_conv_b=ffn_conv_b, w_down=w_down)
    M = dict(ln1_g=m_ln1_g, w_in=m_w_in, b_gate=m_b_gate, sg_ln_g=m_sg_ln_g, sg_ln_b=m_sg_ln_b, sg_w=m_sg_w, sg_b=m_sg_b,
             w_a_out=m_w_a_out, cv_w=m_cv_w, cv_b=m_cv_b, cv_ln_g=m_cv_ln_g, cv_ln_b=m_cv_ln_b, w_b_out=m_w_b_out,
             q_norm_g=m_q_norm_g, k_norm_g=m_k_norm_g, w_c_out=m_w_c_out, w_out=m_w_out, ln2_g=m_ln2_g, w_up=m_w_up,
             ffn_conv_w=m_ffn_conv_w, ffn_conv_b=m_ffn_conv_b, w_down=m_w_down)
    V = dict(ln1_g=v_ln1_g, w_in=v_w_in, b_gate=v_b_gate, sg_ln_g=v_sg_ln_g, sg_ln_b=v_sg_ln_b, sg_w=v_sg_w, sg_b=v_sg_b,
             w_a_out=v_w_a_out, cv_w=v_cv_w, cv_b=v_cv_b, cv_ln_g=v_cv_ln_g, cv_ln_b=v_cv_ln_b, w_b_out=v_w_b_out,
             q_norm_g=v_q_norm_g, k_norm_g=v_k_norm_g, w_c_out=v_w_c_out, w_out=v_w_out, ln2_g=v_ln2_g, w_up=v_w_up,
             ffn_conv_w=v_ffn_conv_w, ffn_conv_b=v_ffn_conv_b, w_down=v_w_down)
    T, D = x.shape[1], x.shape[2]
    L = DEPTH
    xs = x.reshape(T, D)
    target = loss_target.reshape(T, D)

    ss_shapes = [W[n].shape for n in SMALL_SHARDED]
    ss_rows = _rows_for(ss_shapes, 8)
    (ss_all,) = exchange("gather_small", [_pack([W[n] for n in SMALL_SHARDED], ss_rows)], ["gather"])
    full_small = {}
    pos = 0
    for n in SMALL_SHARDED:
        s = W[n].shape
        cnt = s[0] * s[1] * s[2]
        part = ss_all.reshape(N_DEV, -1)[:, pos:pos + cnt].reshape((N_DEV,) + s)
        full_small[n] = jnp.transpose(part, (1, 2, 0, 3)).reshape(s[0], s[1], N_DEV * s[2])
        pos += cnt

    params, saved = [], []
    cur = xs
    Wt = {n: (jnp.transpose(W[n], (0, 2, 1)) if n in TRANSPOSED else W[n]) for n in BIG}

    def local_slabs(l):
        return [Wt[n][l].astype(bf16) for n in BIG]

    slabs = gather2("gather_big", local_slabs(0))
    for l in range(L):
        P = {}
        for n, s in zip(BIG, slabs):
            if n in TRANSPOSED:
                P[n + "_t"] = s.reshape(-1, s.shape[-1])
            else:
                P[n] = assemble("assemble_" + n, s) if n in COL_SHARDED else s.reshape(-1, s.shape[-1])
        for n in REPL:
            P[n] = W[n][l]
        for n in ("ln1_g", "sg_ln_g", "sg_ln_b", "cv_b", "cv_ln_g", "cv_ln_b", "q_norm_g", "k_norm_g", "ln2_g", "ffn_conv_b"):
            P[n] = P[n].reshape(1, -1)
        P["sg_b_t"] = P["sg_b"].T
        P["cv_w"] = full_small["cv_w"][l]
        P["b_gate"] = [full_small["b_gate"][l][k:k + 1] for k in range(3)]
        P["ffn_conv_w"] = [full_small["ffn_conv_w"][l][k:k + 1] for k in range(3)]
        nxt = local_slabs(l + 1) if l + 1 < L else None
        cur, sv, slabs = _forward_layer(cur, P, T, gather=nxt)
        params.append(P)
        saved.append(sv)

    dy, loss_row = loss_head(cur, target, T)
    loss = lax.psum(loss_row[0, 0], ("x", "y", "c"))

    repl_shapes = [W[n].shape for n in REPL]
    repl_rows = _rows_for([s[1:] for s in repl_shapes], 16)
    ssl_rows = _rows_for([s[1:] for s in ss_shapes], 16)
    state = {"repl": [_pack_layers([X[n] for n in REPL], repl_rows) for X in (W, M, V)],
             "ss": [_pack_layers([X[n] for n in SMALL_SHARDED], ssl_rows) for X in (W, M, V)]}
    big_rows, big_off = {}, {}
    for n in BIG:
        big_off[n] = sum(big_rows.values())
        big_rows[n] = W[n].shape[1] * W[n].shape[2] // D
        state[n] = [(jnp.transpose(X[n], (0, 2, 1)) if n in TRANSPOSED else X[n]).reshape(L, big_rows[n], D)
                    for X in (W, M, V)]
    done = {n: None for n in state}

    def update(layer, received):
        for n in BIG:
            done[n] = adamw("adamw_" + n, *state[n], received[0], layer, done[n], row_off=big_off[n])
        done["repl"] = adamw("adamw_repl", *state["repl"], received[1], layer, done["repl"])
        done["ss"] = adamw("adamw_ss", *state["ss"], received[2], layer, done["ss"])

    def reduce_in_chip(G):
        big = [G[n].reshape(4, 2, big_rows[n], D) for n in BIG]
        ss_parts = []
        for n in SMALL_SHARDED:
            k, c = W[n].shape[1:]
            ss_parts.append(jnp.transpose(G[n].reshape(k, N_DEV, c), (1, 0, 2)).reshape(N_DEV, k * c))
        ss_send = jnp.concatenate(ss_parts, axis=1)
        ss_send = jnp.pad(ss_send, ((0, 0), (0, ssl_rows * 128 - ss_send.shape[1]))).reshape(4, 2, ssl_rows, 128)
        repl = _pack([G[n] for n in REPL], repl_rows)
        core = lax.axis_index("c")
        kept = jnp.concatenate([lax.dynamic_index_in_dim(b, core, 1, keepdims=False) for b in big], axis=1)
        ss_kept = lax.dynamic_index_in_dim(ss_send, core, 1, keepdims=False)
        got, repl_got, ss_got = pair_exchange("pair_exchange", big, [repl, ss_send], ["gather", "scatter"])
        return [add2("pair_add_big", kept, got), add2("pair_add_repl", repl, repl_got), add2("pair_add_ss", ss_kept, ss_got)]

    chip_modes = ["scatter", "gather", "scatter"]
    dcur = dy
    send = None
    for l in reversed(range(L)):
        dcur, G, recv = _backward_layer(dcur, params[l], saved[l], T, scatter=send)
        if send is not None:
            update(l + 1, recv)
        send = (reduce_in_chip(G), chip_modes)
    update(0, chip_exchange("chip_exchange_last", send[0], send[1]))

    results = [{}, {}, {}, {}]
    for k in range(4):
        for n, a in zip(REPL, _unpack_layers(done["repl"][k], repl_shapes)):
            results[k][n] = a
        for n, a in zip(SMALL_SHARDED, _unpack_layers(done["ss"][k], ss_shapes)):
            results[k][n] = a
        for n in BIG:
            results[k][n] = jnp.transpose(done[n][k], (0, 2, 1)) if n in TRANSPOSED else done[n][k].reshape(W[n].shape)
    out = [loss, dcur.reshape(1, T, D)]
    for k in range(4):
        out += [results[k][n] for n in W_NAMES]
    return tuple(out)
```

```python
import functools

import jax
import jax.numpy as jnp
from jax import lax
from jax.experimental import pallas as pl
from jax.experimental.pallas import tpu as pltpu

f32 = jnp.float32
bf16 = jnp.bfloat16

EPS = 1e-6
N_DEV = 8
DEPTH = 4
CHUNK = 128
HEAD_DIM = 64
N_HEADS = 8
CV_KERNEL = 31
VMEM_LIMIT_BYTES = 56 * 2 ** 20

ADAM_LR = 0.001
ADAM_B1 = 0.9
ADAM_B2 = 0.999
ADAM_EPS = 1e-08
ADAM_WD = 0.01
ADAM_STEP = 10

MESH = pl.DeviceIdType.MESH


def _cparams(n_grid):
    return pltpu.CompilerParams(dimension_semantics=("arbitrary",) * n_grid, vmem_limit_bytes=VMEM_LIMIT_BYTES)


def exchange(name, arrays, modes):
    n = len(arrays)

    def body(*refs):
        copies = _direct_copies(refs[:n], refs[n:2 * n], modes, *refs[2 * n:])
        for cp in copies:
            cp.start()
        for cp in copies:
            cp.wait()

    return pl.pallas_call(
        body, name=name, out_shape=_exchange_out_shapes(arrays, modes),
        in_specs=[_ANY] * n, out_specs=[_ANY] * n, scratch_shapes=_exchange_sems(n),
    )(*arrays)


_ANY = pl.BlockSpec(memory_space=pl.ANY)


def _exchange_out_shapes(arrays, modes):
    return [jax.ShapeDtypeStruct((N_DEV,) + tuple(a.shape) if m == "gather" else tuple(a.shape), a.dtype)
            for a, m in zip(arrays, modes)]


def _exchange_sems(n):
    return [pltpu.SemaphoreType.DMA((n, N_DEV - 1)), pltpu.SemaphoreType.DMA((n, N_DEV - 1)), pltpu.SemaphoreType.DMA((n,))]


def _direct_copies(ins, outs, modes, send_sems, recv_sems, local_sems):
    x, y, c = lax.axis_index("x"), lax.axis_index("y"), lax.axis_index("c")
    me = 4 * x + 2 * y + c
    copies = []
    for k in range(len(ins)):
        src_mine = ins[k] if modes[k] == "gather" else ins[k].at[me]
        copies.append(pltpu.make_async_copy(src_mine, outs[k].at[me], local_sems.at[k]))
        for r in range(1, N_DEV):
            px = 1 - x if r & 4 else x
            py = 1 - y if r & 2 else y
            pc = 1 - c if r & 1 else c
            src = ins[k] if modes[k] == "gather" else ins[k].at[4 * px + 2 * py + pc]
            copies.append(pltpu.make_async_remote_copy(
                src_ref=src, dst_ref=outs[k].at[me], send_sem=send_sems.at[k, r - 1], recv_sem=recv_sems.at[k, r - 1],
                device_id=(px, py, pc), device_id_type=MESH))
    return copies


def own_slot_buffer(block, n_slots, index):
    return lax.dynamic_update_slice(lax.empty((n_slots,) + tuple(block.shape), block.dtype), block[None],
                                    (index,) + (0,) * block.ndim)


def _sems(n, m):
    return [pltpu.SemaphoreType.DMA((n, m)), pltpu.SemaphoreType.DMA((n, m))]


def _two_level_gather(bufs, send_sems, recv_sems):
    x, y, c = lax.axis_index("x"), lax.axis_index("y"), lax.axis_index("c")
    me = 4 * x + 2 * y + c
    sibling = (x, y, 1 - c)
    chips = [(1 - x, y), (x, 1 - y), (1 - x, 1 - y)]

    def slot(px, py, pc):
        return 4 * px + 2 * py + pc

    def copy(k, sem, block, to):
        rows = bufs[k].at[block]
        return pltpu.make_async_remote_copy(src_ref=rows, dst_ref=rows, send_sem=send_sems.at[k, sem],
                                            recv_sem=recv_sems.at[k, sem], device_id=to, device_id_type=MESH)

    def first(k):
        return [copy(k, 0, me, sibling)] + [copy(k, 1 + j, me, (*chip, c)) for j, chip in enumerate(chips)]

    def passed_on(k, j):
        return copy(k, 4 + j, slot(*chips[j], c), sibling)

    def start():
        for k in range(len(bufs)):
            for cp in first(k):
                cp.start()

    def forward():
        for k in range(len(bufs)):
            for j in range(3):
                copy(k, 1 + j, slot(*chips[j], c), sibling).wait_recv()
                passed_on(k, j).start()

    def finish():
        for k in range(len(bufs)):
            copy(k, 0, slot(x, y, 1 - c), sibling).wait_recv()
            for j, chip in enumerate(chips):
                copy(k, 4 + j, slot(*chip, 1 - c), sibling).wait_recv()
            for cp in first(k) + [passed_on(k, j) for j in range(3)]:
                cp.wait_send()

    return start, forward, finish


def pair_exchange(name, big, smalls, modes):
    nb, ns = len(big), len(smalls)
    rows = [a.shape[2] for a in big]
    total, width = sum(rows), big[0].shape[3]
    n = nb + ns

    def body(*refs):
        ins, outs = refs[:n], refs[n:n + 1 + ns]
        send_sems, recv_sems = refs[-2:]
        x, y, c = lax.axis_index("x"), lax.axis_index("y"), lax.axis_index("c")
        copies = []

        def remote(k, src, dst):
            copies.append(pltpu.make_async_remote_copy(src_ref=src, dst_ref=dst, send_sem=send_sems.at[k, 0],
                                                       recv_sem=recv_sems.at[k, 0], device_id=(x, y, 1 - c),
                                                       device_id_type=MESH))

        off = 0
        for k in range(nb):
            remote(k, ins[k].at[:, 1 - c], outs[0].at[:, pl.ds(off, rows[k]), :])
            off += rows[k]
        for j in range(ns):
            remote(nb + j, ins[nb + j].at[:, 1 - c] if modes[j] == "scatter" else ins[nb + j], outs[1 + j])
        for cp in copies:
            cp.start()
        for cp in copies:
            cp.wait()

    out_shape = [jax.ShapeDtypeStruct((4, total, width), big[0].dtype)]
    for a, m in zip(smalls, modes):
        out_shape.append(jax.ShapeDtypeStruct((4,) + tuple(a.shape[2:]) if m == "scatter" else tuple(a.shape), a.dtype))
    return pl.pallas_call(
        body, name=name, out_shape=out_shape, in_specs=[_ANY] * n, out_specs=[_ANY] * len(out_shape),
        scratch_shapes=_sems(n, 1),
    )(*big, *smalls)


def add2(name, a, b):
    shape = a.shape
    a2, b2 = a.reshape(-1, shape[-1]), b.reshape(-1, shape[-1])
    R, C = a2.shape
    tr = _row_tile(R, C, max_elems=1200 * 1024)

    def body(a_ref, b_ref, o_ref):
        o_ref[...] = (a_ref[...].astype(f32) + b_ref[...].astype(f32)).astype(o_ref.dtype)

    spec = pl.BlockSpec((tr, C), lambda i: (i, 0))
    return pl.pallas_call(
        body, name=name, grid=(R // tr,), in_specs=[spec, spec], out_specs=spec,
        out_shape=jax.ShapeDtypeStruct((R, C), a.dtype), compiler_params=_cparams(1),
    )(a2, b2).reshape(shape)


def chip_buffers(arrays, modes):
    mine = 2 * lax.axis_index("x") + lax.axis_index("y")
    return [own_slot_buffer(lax.dynamic_index_in_dim(a, mine, 0, keepdims=False) if m == "scatter" else a, 4, mine)
            for a, m in zip(arrays, modes)]


def _chip_copies(ins, bufs, modes, send_sems, recv_sems):
    x, y, c = lax.axis_index("x"), lax.axis_index("y"), lax.axis_index("c")
    mine = 2 * x + y
    copies = []
    for k in range(len(ins)):
        for r in range(1, 4):
            px = 1 - x if r & 2 else x
            py = 1 - y if r & 1 else y
            src = ins[k].at[2 * px + py] if modes[k] == "scatter" else ins[k]
            copies.append(pltpu.make_async_remote_copy(
                src_ref=src, dst_ref=bufs[k].at[mine], send_sem=send_sems.at[k, r - 1], recv_sem=recv_sems.at[k, r - 1],
                device_id=(px, py, c), device_id_type=MESH))
    return copies


def chip_exchange(name, arrays, modes):
    n = len(arrays)

    def body(*refs):
        copies = _chip_copies(refs[:n], refs[2 * n:3 * n], modes, *refs[3 * n:])
        for cp in copies:
            cp.start()
        for cp in copies:
            cp.wait()

    bufs = chip_buffers(arrays, modes)
    return pl.pallas_call(
        body, name=name, out_shape=[jax.ShapeDtypeStruct(b.shape, b.dtype) for b in bufs],
        in_specs=[_ANY] * (2 * n), out_specs=[_ANY] * n, input_output_aliases={n + k: k for k in range(n)},
        scratch_shapes=_sems(n, 3),
    )(*arrays, *bufs)


def gather_buffers(blocks):
    me = 4 * lax.axis_index("x") + 2 * lax.axis_index("y") + lax.axis_index("c")
    return [own_slot_buffer(b, N_DEV, me) for b in blocks]


def gather2(name, arrays):
    n = len(arrays)

    def body(*refs):
        start, forward, finish = _two_level_gather(refs[n:2 * n], *refs[2 * n:])
        start()
        forward()
        finish()

    bufs = gather_buffers(arrays)
    return pl.pallas_call(
        body, name=name, out_shape=[jax.ShapeDtypeStruct(b.shape, b.dtype) for b in bufs],
        in_specs=[_ANY] * n, out_specs=[_ANY] * n, input_output_aliases={k: k for k in range(n)},
        scratch_shapes=_sems(n, N_DEV - 1),
    )(*bufs)


def assemble(name, slabs):
    _, K, Ns = slabs.shape
    g = N_DEV if Ns % 128 == 0 else 2

    def body(w_ref, o_ref):
        o_ref[...] = jnp.concatenate([w_ref[s] for s in range(g)], axis=1)

    return pl.pallas_call(
        body, name=name, grid=(N_DEV // g,),
        in_specs=[pl.BlockSpec((g, K, Ns), lambda m: (m, 0, 0))],
        out_specs=pl.BlockSpec((K, g * Ns), lambda m: (0, m)),
        out_shape=jax.ShapeDtypeStruct((K, N_DEV * Ns), slabs.dtype),
        compiler_params=_cparams(1),
    )(slabs)


def mm_nn(name, a, b, out_dtype, add=None, tm=1024, tn=512, tk=None):
    M, K = a.shape
    N = b.shape[1]
    tm, tn = min(tm, M), min(tn, N)
    tk = K if tk is None else tk
    nk = K // tk
    has_add = add is not None

    def body(a_ref, b_ref, *rest):
        o_ref = rest[1] if has_add else rest[0]
        part = jnp.dot(a_ref[...].astype(bf16), b_ref[...].astype(bf16), preferred_element_type=f32)
        if nk == 1:
            o_ref[...] = (part + rest[0][...] if has_add else part).astype(o_ref.dtype)
            return
        acc_ref = rest[-1]
        k = pl.program_id(2)

        @pl.when(k == 0)
        def _():
            acc_ref[...] = part + rest[0][...] if has_add else part

        @pl.when(k > 0)
        def _():
            acc_ref[...] += part

        @pl.when(k == nk - 1)
        def _():
            o_ref[...] = acc_ref[...].astype(o_ref.dtype)

    in_specs = [pl.BlockSpec((tm, tk), lambda i, j, k: (i, k)), pl.BlockSpec((tk, tn), lambda i, j, k: (k, j))]
    ops = [a, b]
    if has_add:
        in_specs.append(pl.BlockSpec((tm, tn), lambda i, j, k: (i, j)))
        ops.append(add)
    return pl.pallas_call(
        body, name=name, grid=(M // tm, N // tn, nk), in_specs=in_specs,
        out_specs=pl.BlockSpec((tm, tn), lambda i, j, k: (i, j)),
        out_shape=jax.ShapeDtypeStruct((M, N), out_dtype),
        scratch_shapes=[pltpu.VMEM((tm, tn), f32)] if nk > 1 else [], compiler_params=_cparams(3),
    )(*ops)


def concat_cols(name, pieces, T, tm=512):
    widths = [p.shape[1] for p in pieces]
    total = sum(widths)
    tm = min(tm, T)

    def body(*refs):
        o_ref = refs[-1]
        off = 0
        for r, w in zip(refs[:-1], widths):
            o_ref[:, off:off + w] = r[...]
            off += w

    return pl.pallas_call(
        body, name=name, grid=(T // tm,),
        in_specs=[pl.BlockSpec((tm, w), lambda i: (i, 0)) for w in widths],
        out_specs=pl.BlockSpec((tm, total), lambda i: (i, 0)),
        out_shape=jax.ShapeDtypeStruct((T, total), pieces[0].dtype), compiler_params=_cparams(1),
    )(*pieces)


def mm_nt(name, a, b, out_dtype, tm=1024, tko=None, tc=512):
    M, C = a.shape
    Ko = b.shape[0]
    tm, tc = min(tm, M), min(tc, C)
    tko = Ko if tko is None else tko
    nc = C // tc

    def body(a_ref, b_ref, o_ref, acc_ref):
        c = pl.program_id(2)
        part = lax.dot_general(a_ref[...].astype(bf16), b_ref[...].astype(bf16), (((1,), (1,)), ((), ())),
                               preferred_element_type=f32)

        @pl.when(c == 0)
        def _():
            acc_ref[...] = part

        @pl.when(c > 0)
        def _():
            acc_ref[...] += part

        @pl.when(c == nc - 1)
        def _():
            o_ref[...] = acc_ref[...].astype(o_ref.dtype)

    return pl.pallas_call(
        body, name=name, grid=(M // tm, Ko // tko, nc),
        in_specs=[pl.BlockSpec((tm, tc), lambda i, j, c: (i, c)), pl.BlockSpec((tko, tc), lambda i, j, c: (j, c))],
        out_specs=pl.BlockSpec((tm, tko), lambda i, j, c: (i, j)),
        out_shape=jax.ShapeDtypeStruct((M, Ko), out_dtype),
        scratch_shapes=[pltpu.VMEM((tm, tko), f32)], compiler_params=_cparams(3),
    )(a, b)


def mm_tn(name, a, b, n_slab=None, tk=512, tn=512):
    T, K = a.shape
    N = b.shape[1]
    tk = min(tk, K)
    if n_slab is None:
        tn = min(tn, N)

        def body(a_ref, b_ref, o_ref):
            o_ref[...] = lax.dot_general(a_ref[...].astype(bf16), b_ref[...].astype(bf16), (((0,), (0,)), ((), ())),
                                         preferred_element_type=f32).astype(o_ref.dtype)

        return pl.pallas_call(
            body, name=name, grid=(N // tn, K // tk),
            in_specs=[pl.BlockSpec((T, tk), lambda j, i: (0, i)), pl.BlockSpec((T, tn), lambda j, i: (0, j))],
            out_specs=pl.BlockSpec((tk, tn), lambda j, i: (i, j)),
            out_shape=jax.ShapeDtypeStruct((K, N), bf16), compiler_params=_cparams(2),
        )(a, b)

    Ns = n_slab
    g = N_DEV if Ns % 128 == 0 else 2
    tn = g * Ns

    def body(a_ref, b_ref, o_ref):
        val = lax.dot_general(a_ref[...].astype(bf16), b_ref[...].astype(bf16), (((0,), (0,)), ((), ())),
                              preferred_element_type=f32)
        for s in range(g):
            o_ref[s] = val[:, s * Ns:(s + 1) * Ns].astype(o_ref.dtype)

    return pl.pallas_call(
        body, name=name, grid=(K // tk, N_DEV // g),
        in_specs=[pl.BlockSpec((T, tk), lambda i, j: (0, i)), pl.BlockSpec((T, tn), lambda i, j: (0, j))],
        out_specs=pl.BlockSpec((g, tk, Ns), lambda i, j: (j, i, 0)),
        out_shape=jax.ShapeDtypeStruct((N_DEV, K, Ns), bf16), compiler_params=_cparams(2),
    )(a, b)


def _tile_spec(w, tm, cb):
    if callable(cb):
        return pl.BlockSpec((tm, w), lambda c, i: (i, cb(c)))
    return pl.BlockSpec((tm, w), lambda c, i: (i, cb))


def _param_spec(block, cb):
    nd = len(block)
    if cb is None:
        return pl.BlockSpec(block, lambda c, i: (0,) * nd)
    return pl.BlockSpec(block, lambda c, i: (0,) * (nd - 1) + (cb(c),))


def rowwise(name, fn, tiled, params, outs, T, tm, ncol=1):
    n_in = len(tiled) + len(params)
    n_t = len(tiled)

    def body(*refs):
        ins = [r[...].astype(f32) for r in refs[:n_t]] + [r[...] for r in refs[n_t:n_in]]
        res = fn(*ins)
        for r, o in zip(refs[n_in:], res):
            r[...] = o.astype(r.dtype)

    return pl.pallas_call(
        body, name=name, grid=(ncol, T // tm),
        in_specs=[_tile_spec(w, tm, cb) for _, w, cb in tiled] + [_param_spec(blk, cb) for _, blk, cb in params],
        out_specs=[_tile_spec(w, tm, cb) for _, w, cb, _ in outs],
        out_shape=[jax.ShapeDtypeStruct((T, cols), dt) for cols, _, _, dt in outs],
        compiler_params=_cparams(2),
    )(*[a for a, _, _ in tiled], *[a for a, _, _ in params])


def rowwise_bwd(name, fn, tiled, params, cts, grads, T, tm, ncol=1, adds=None):
    n_t, n_p, n_c = len(tiled), len(params), len(cts)
    adds = adds or [None] * n_t
    add_list = [(k, a) for k, a in enumerate(adds) if a is not None]
    want = [k for k, g in enumerate(grads) if g is not None]
    n_a = len(add_list)

    def body(*refs):
        pos = 0
        t_refs = refs[pos:pos + n_t]; pos += n_t
        p_refs = refs[pos:pos + n_p]; pos += n_p
        c_refs = refs[pos:pos + n_c]; pos += n_c
        a_refs = refs[pos:pos + n_a]; pos += n_a
        g_refs = refs[pos:pos + len(want)]; pos += len(want)
        pg_refs = refs[pos:pos + n_p]
        primals = [r[...].astype(f32) for r in t_refs] + [r[...] for r in p_refs]
        _, vjp = jax.vjp(fn, *primals)
        g = vjp(tuple(r[...].astype(f32) for r in c_refs))
        add_of = {k: a_refs[n][...] for n, (k, _) in enumerate(add_list)}
        for n, k in enumerate(want):
            val = g[k]
            if k in add_of:
                val = val + add_of[k]
            g_refs[n][...] = val.astype(g_refs[n].dtype)
        i = pl.program_id(1)
        for k in range(n_p):
            @pl.when(i == 0)
            def _(k=k):
                pg_refs[k][...] = g[n_t + k]

            @pl.when(i > 0)
            def _(k=k):
                pg_refs[k][...] += g[n_t + k]

    in_specs = ([_tile_spec(w, tm, cb) for _, w, cb in tiled] + [_param_spec(blk, cb) for _, blk, cb in params]
                + [_tile_spec(w, tm, cb) for _, w, cb in cts] + [_tile_spec(w, tm, cb) for _, (_, w, cb) in add_list])
    ops = ([a for a, _, _ in tiled] + [a for a, _, _ in params] + [a for a, _, _ in cts]
           + [a for _, (a, _, _) in add_list])
    out_specs, out_shape = [], []
    for k in want:
        w = tiled[k][1]
        out_specs.append(_tile_spec(w, tm, lambda c: c))
        out_shape.append(jax.ShapeDtypeStruct((T, ncol * w), grads[k]))
    for a, blk, cb in params:
        out_specs.append(_param_spec(blk, cb))
        out_shape.append(jax.ShapeDtypeStruct(a.shape, f32))
    return pl.pallas_call(
        body, name=name, grid=(ncol, T // tm), in_specs=in_specs, out_specs=out_specs, out_shape=out_shape,
        compiler_params=_cparams(2),
    )(*ops)


@jax.custom_vjp
def _bdot(a, b):
    return jnp.dot(a.astype(bf16), b.astype(bf16), preferred_element_type=f32)


def _bdot_fwd(a, b):
    return _bdot(a, b), (a, b)


def _bdot_bwd(res, ct):
    a, b = res
    ctb = ct.astype(bf16)
    da = lax.dot_general(ctb, b.astype(bf16), (((1,), (1,)), ((), ())), preferred_element_type=f32)
    db = lax.dot_general(a.astype(bf16), ctb, (((0,), (0,)), ((), ())), preferred_element_type=f32)
    return da, db


_bdot.defvjp(_bdot_fwd, _bdot_bwd)


def _layer_norm(x, g, b):
    mu = jnp.mean(x, axis=-1, keepdims=True)
    xc = x - mu
    y = xc * lax.rsqrt(jnp.mean(xc * xc, axis=-1, keepdims=True) + EPS)
    return y * g + b


def f_rms(x, g):
    y = x * lax.rsqrt(jnp.mean(x * x, axis=-1, keepdims=True) + EPS)
    return (y * g,)


def f_sgu(zu, zv, ln_g, ln_b, wm, sgb_t):
    u = jax.nn.gelu(zu)
    vn = _layer_norm(jax.nn.gelu(zv), ln_g, ln_b)
    row = lax.broadcasted_iota(jnp.int32, (CHUNK, CHUNK), 0)
    col = lax.broadcasted_iota(jnp.int32, (CHUNK, CHUNK), 1)
    tril = col <= row
    low = col < HEAD_DIM
    parts = []
    for p in range(4):
        vp = vn[:, CHUNK * p:CHUNK * (p + 1)]
        w0 = jnp.where(tril, wm[2 * p], 0.0)
        w1 = jnp.where(tril, wm[2 * p + 1], 0.0)
        parts.append(_bdot(w0, jnp.where(low, vp, 0.0)) + _bdot(w1, jnp.where(low, 0.0, vp)))
    s = jnp.concatenate(parts, axis=1)
    lane_g = lax.shift_right_logical(lax.broadcasted_iota(jnp.int32, s.shape, 1), 6)
    bias = jnp.zeros_like(s)
    for g in range(8):
        bias = jnp.where(lane_g == g, sgb_t[:, g:g + 1], bias)
    return (u * (s + bias),)


def f_glu(p, gl):
    return (p * jax.nn.sigmoid(gl),)


def f_lnsilu(c1, g, b):
    return (jax.nn.silu(_layer_norm(c1, g, b)),)


def _head_norm(x, g64):
    g = jnp.concatenate([g64] * N_HEADS, axis=1)
    lane_h = lax.shift_right_logical(lax.broadcasted_iota(jnp.int32, x.shape, 1), 6)
    x2 = x * x
    r = jnp.zeros_like(x)
    for h in range(N_HEADS):
        mh = lane_h == h
        ms = jnp.sum(jnp.where(mh, x2, 0.0), axis=-1, keepdims=True) * (1.0 / HEAD_DIM)
        r = jnp.where(mh, lax.rsqrt(ms + EPS), r)
    return (x * r) * g


def f_qkv(zq, zk, zv, qg, kg):
    return (_head_norm(zq, qg) * 0.125, _head_norm(zk, kg), zv)


def f_merge(g0, g1, g2, ya, yb, yc, b0, b1, b2):
    return (jax.nn.sigmoid(g0 + b0) * ya + jax.nn.sigmoid(g1 + b1) * yb + jax.nn.sigmoid(g2 + b2) * yc,)


HALO = 32


def conv31_fwd(x, w, b, T, tt=256):
    C = x.shape[1]
    r = tt // HALO

    def body(x_ref, h_ref, w_ref, b_ref, y_ref, buf):
        i = pl.program_id(0)
        halo = h_ref[...]
        buf[0:HALO, :] = jnp.where(i > 0, halo, jnp.zeros_like(halo))
        buf[HALO:HALO + tt, :] = x_ref[...]
        acc = jnp.zeros((tt, C), f32) + b_ref[...]
        for k in range(CV_KERNEL):
            acc = acc + w_ref[k:k + 1, :] * buf[pl.ds(HALO - (CV_KERNEL - 1) + k, tt), :]
        y_ref[...] = acc

    return pl.pallas_call(
        body, name="conv31_fwd", grid=(T // tt,),
        in_specs=[pl.BlockSpec((tt, C), lambda i: (i, 0)),
                  pl.BlockSpec((HALO, C), lambda i: (jnp.maximum(i * r - 1, 0), 0)),
                  pl.BlockSpec((CV_KERNEL, C), lambda i: (0, 0)), pl.BlockSpec((1, C), lambda i: (0, 0))],
        out_specs=pl.BlockSpec((tt, C), lambda i: (i, 0)),
        out_shape=jax.ShapeDtypeStruct((T, C), f32),
        scratch_shapes=[pltpu.VMEM((HALO + tt, C), f32)], compiler_params=_cparams(1),
    )(x, x, w, b)


def conv31_bwd(x, w, dy, T, tt=256):
    C = x.shape[1]
    r = tt // HALO
    n = T // tt

    def body(x_ref, h_ref, w_ref, dy_ref, dyn_ref, dx_ref, dw_ref, db_ref, xbuf, dbuf):
        i = pl.program_id(0)
        halo = h_ref[...]
        xbuf[0:HALO, :] = jnp.where(i > 0, halo, jnp.zeros_like(halo))
        xbuf[HALO:HALO + tt, :] = x_ref[...]
        nxt = dyn_ref[...]
        dy = dy_ref[...]
        dbuf[0:tt, :] = dy
        dbuf[tt:tt + HALO, :] = jnp.where(i < n - 1, nxt, jnp.zeros_like(nxt))

        @pl.when(i == 0)
        def _():
            dw_ref[...] = jnp.zeros_like(dw_ref)
            db_ref[...] = jnp.zeros_like(db_ref)

        acc = jnp.zeros((tt, C), f32)
        for k in range(CV_KERNEL):
            acc = acc + w_ref[k:k + 1, :] * dbuf[pl.ds(CV_KERNEL - 1 - k, tt), :]
            xs = xbuf[pl.ds(HALO - (CV_KERNEL - 1) + k, tt), :]
            dw_ref[k:k + 1, :] += jnp.sum(dy * xs, axis=0, keepdims=True)
        dx_ref[...] = acc
        db_ref[...] += jnp.sum(dy, axis=0, keepdims=True)

    return pl.pallas_call(
        body, name="conv31_bwd", grid=(n,),
        in_specs=[pl.BlockSpec((tt, C), lambda i: (i, 0)),
                  pl.BlockSpec((HALO, C), lambda i: (jnp.maximum(i * r - 1, 0), 0)),
                  pl.BlockSpec((CV_KERNEL, C), lambda i: (0, 0)),
                  pl.BlockSpec((tt, C), lambda i: (i, 0)),
                  pl.BlockSpec((HALO, C), lambda i: (jnp.minimum((i + 1) * r, n * r - 1), 0))],
        out_specs=[pl.BlockSpec((tt, C), lambda i: (i, 0)), pl.BlockSpec((CV_KERNEL, C), lambda i: (0, 0)),
                   pl.BlockSpec((1, C), lambda i: (0, 0))],
        out_shape=[jax.ShapeDtypeStruct((T, C), f32), jax.ShapeDtypeStruct((CV_KERNEL, C), f32),
                   jax.ShapeDtypeStruct((1, C), f32)],
        scratch_shapes=[pltpu.VMEM((HALO + tt, C), f32), pltpu.VMEM((HALO + tt, C), f32)], compiler_params=_cparams(1),
    )(x, x, w, dy, dy)


FFN_TC = 128
FFN_PAD = 8


def ffn_act_fwd(up, cw, cb, T):
    F = up.shape[1] // 2
    nj = F // FFN_TC
    rc = min(128, T)

    def body(g_ref, v_ref, g0, g1, g2, gb, v0, v1, v2, vb, o_ref, gp, vp):
        zeros = jnp.zeros((FFN_PAD, FFN_TC), f32)
        for p, x_ref in ((gp, g_ref), (vp, v_ref)):
            p[0:FFN_PAD, :] = zeros
            p[FFN_PAD:FFN_PAD + T, :] = x_ref[...].astype(f32)
        wg = (g0[...], g1[...], g2[...], gb[...])
        wv = (v0[...], v1[...], v2[...], vb[...])

        def conv(p, w, r):
            return (w[0] * p[pl.ds(FFN_PAD + r - 2, rc), :] + w[1] * p[pl.ds(FFN_PAD + r - 1, rc), :]
                    + w[2] * p[pl.ds(FFN_PAD + r, rc), :] + w[3])

        for r in range(0, T, rc):
            o_ref[pl.ds(r, rc), :] = (jax.nn.silu(conv(gp, wg, r)) * conv(vp, wv, r)).astype(o_ref.dtype)

    gspec = pl.BlockSpec((T, FFN_TC), lambda j: (0, j))
    vspec = pl.BlockSpec((T, FFN_TC), lambda j: (0, j + nj))
    pg = pl.BlockSpec((1, FFN_TC), lambda j: (0, j))
    pv = pl.BlockSpec((1, FFN_TC), lambda j: (0, j + nj))
    return pl.pallas_call(
        body, name="ffn_act_fwd", grid=(nj,),
        in_specs=[gspec, vspec, pg, pg, pg, pg, pv, pv, pv, pv], out_specs=gspec,
        out_shape=jax.ShapeDtypeStruct((T, F), bf16),
        scratch_shapes=[pltpu.VMEM((FFN_PAD + T, FFN_TC), f32)] * 2, compiler_params=_cparams(1),
    )(up, up, cw[0], cw[1], cw[2], cb, cw[0], cw[1], cw[2], cb)


def ffn_act_bwd(up, cw, cb, dact, T):
    F = up.shape[1] // 2
    nj = F // FFN_TC

    rc = min(128, T)
    ext = rc + FFN_PAD

    def body(g_ref, v_ref, g0, g1, g2, gb, v0, v1, v2, vb, d_ref, dupg_ref, dupv_ref, *rest):
        p_refs, (gp, vp, dp, dgs, dvs) = rest[:8], rest[8:]
        zeros = jnp.zeros((FFN_PAD, FFN_TC), f32)
        for p, x_ref in ((gp, g_ref), (vp, v_ref)):
            p[0:FFN_PAD, :] = zeros
            p[FFN_PAD:FFN_PAD + T, :] = x_ref[...].astype(f32)
            p[FFN_PAD + T:FFN_PAD + T + FFN_PAD, :] = zeros
        dp[0:T, :] = d_ref[...]
        dp[T:T + FFN_PAD, :] = zeros
        wg = (g0[...], g1[...], g2[...], gb[...])
        wv = (v0[...], v1[...], v2[...], vb[...])
        acc = [jnp.zeros((1, FFN_TC), f32) for _ in range(8)]

        def taps(p, r):
            return tuple(p[pl.ds(FFN_PAD + r - s, ext), :] for s in (2, 1, 0))

        for r in range(0, T, rc):
            xg, xv = taps(gp, r), taps(vp, r)
            gc = wg[0] * xg[0] + wg[1] * xg[1] + wg[2] * xg[2] + wg[3]
            vc = wv[0] * xv[0] + wv[1] * xv[1] + wv[2] * xv[2] + wv[3]
            d = dp[pl.ds(r, ext), :]
            sg = jax.nn.sigmoid(gc)
            sides = ((d * vc * (sg * (1.0 + gc * (1.0 - sg))), xg, wg, dgs, dupg_ref), (d * (gc * sg), xv, wv, dvs, dupv_ref))
            for side, (dc, x, w, buf, dup_ref) in enumerate(sides):
                buf[...] = dc
                dc0 = dc[0:rc]
                dup = w[2] * dc0 + w[1] * buf[pl.ds(1, rc), :] + w[0] * buf[pl.ds(2, rc), :]
                dup_ref[pl.ds(r, rc), :] = dup.astype(dup_ref.dtype)
                for k in range(3):
                    acc[4 * side + k] = acc[4 * side + k] + jnp.sum(dc0 * x[k][0:rc], axis=0, keepdims=True)
                acc[4 * side + 3] = acc[4 * side + 3] + jnp.sum(dc0, axis=0, keepdims=True)
        for k in range(8):
            p_refs[k][...] = acc[k]

    gspec = pl.BlockSpec((T, FFN_TC), lambda j: (0, j))
    vspec = pl.BlockSpec((T, FFN_TC), lambda j: (0, j + nj))
    pg = pl.BlockSpec((1, FFN_TC), lambda j: (0, j))
    pv = pl.BlockSpec((1, FFN_TC), lambda j: (0, j + nj))
    res = pl.pallas_call(
        body, name="ffn_act_bwd", grid=(nj,),
        in_specs=[gspec, vspec, pg, pg, pg, pg, pv, pv, pv, pv, gspec],
        out_specs=[gspec, gspec] + [pg] * 8,
        out_shape=[jax.ShapeDtypeStruct((T, F), bf16)] * 2 + [jax.ShapeDtypeStruct((1, F), f32)] * 8,
        scratch_shapes=[pltpu.VMEM((FFN_PAD + T + FFN_PAD, FFN_TC), f32)] * 2 + [pltpu.VMEM((T + FFN_PAD, FFN_TC), f32)]
        + [pltpu.VMEM((ext, FFN_TC), f32)] * 2,
        compiler_params=_cparams(1),
    )(up, up, cw[0], cw[1], cw[2], cb, cw[0], cw[1], cw[2], cb, dact)
    dup = concat_cols("dup_concat", [res[0], res[1]], T)
    return (dup,) + tuple(jnp.concatenate([res[2 + k], res[6 + k]], axis=1) for k in range(4))


BQ = 256
BK = 256
assert BQ % BK == 0
NT_DIMS = (((1,), (1,)), ((), ()))
TN_DIMS = (((0,), (0,)), ((), ()))


def _split_dot(x, u):
    x1 = x.astype(bf16)
    x2 = (x - x1.astype(f32)).astype(bf16)
    n = x.shape[0]
    y = jnp.dot(jnp.concatenate([x1, x2], axis=0), u, preferred_element_type=f32)
    return y[0:n] + y[n:2 * n]


def _log_sigmoids(z):
    sp = jnp.log(1.0 + jnp.exp(-jnp.abs(z)))
    lsp = jnp.minimum(z, 0.0) - sp
    return lsp, lsp - z


def _stack_heads(x):
    head1 = lax.broadcasted_iota(jnp.int32, x.shape, 1) >= HEAD_DIM
    zero = jnp.zeros_like(x)
    return jnp.concatenate([jnp.where(head1, zero, x), jnp.where(head1, x, zero)], axis=0)


def _unstack_heads(y):
    head1 = lax.broadcasted_iota(jnp.int32, (BQ, y.shape[1]), 1) >= HEAD_DIM
    return jnp.where(head1, y[BQ:2 * BQ], y[0:BQ])


def last_block(i):
    return ((i + 1) * BQ - 1) // BK


def _attn_masks():
    row = lax.broadcasted_iota(jnp.int32, (2 * BQ, BK), 0)
    col = lax.broadcasted_iota(jnp.int32, (2 * BQ, BK), 1)
    ur = lax.broadcasted_iota(jnp.int32, (BK, BK), 0)
    uc = lax.broadcasted_iota(jnp.int32, (BK, BK), 1)
    return (row & (BQ - 1)) - col, (ur > uc).astype(bf16), (ur < uc).astype(bf16)


def attn_fwd(q, k, v, T, gather=None):
    nq = T // BQ
    n_g = 0 if gather is None else len(gather)

    def body(*refs):
        q_ref, k_ref, v_ref = refs[:3]
        o_ref, tot_ref = refs[3 + n_g:5 + n_g]
        p, i = pl.program_id(0), pl.program_id(1)
        if n_g:
            start, forward, finish = _two_level_gather(refs[5 + n_g:5 + 2 * n_g], *refs[5 + 2 * n_g:])
            pl.when(jnp.logical_and(p == 0, i == 0))(start)
        qs = _stack_heads(q_ref[...])
        diff, u_after, _ = _attn_masks()

        def step(t, carry):
            acc, c = carry
            jb = last_block(i) - t
            ks = pl.multiple_of(jb * BK, BK)
            kb = k_ref[pl.ds(ks, BK), :]
            vb = v_ref[pl.ds(ks, BK), :]
            m = diff > jb * BK - i * BQ
            z = lax.dot_general(qs, kb, NT_DIMS, preferred_element_type=f32)
            lsp, lsn = _log_sigmoids(z)
            lm = jnp.where(m, lsn, 0.0)
            a = jnp.where(m, jnp.exp(lsp + _split_dot(lm, u_after)), 0.0)
            acc = acc + jnp.exp(c) * jnp.dot(a.astype(bf16), vb, preferred_element_type=f32)
            return acc, c + jnp.sum(lm, axis=-1, keepdims=True)

        acc, c = lax.fori_loop(0, last_block(i) + 1, step, (jnp.zeros((2 * BQ, 128), f32), jnp.zeros((2 * BQ, 1), f32)))
        o_ref[...] = _unstack_heads(acc).astype(o_ref.dtype)
        tot_ref[...] = _unstack_heads(jnp.broadcast_to(c, (2 * BQ, 128)))
        if n_g:
            @pl.when(jnp.logical_and(p == 3, i == nq - 1))
            def _():
                forward()
                finish()

    blk = pl.BlockSpec((BQ, 128), lambda p, i: (i, p))
    full = pl.BlockSpec((T, 128), lambda p, i: (0, p))
    bufs = [] if gather is None else gather_buffers(gather)
    res = pl.pallas_call(
        body, name="attn_fwd_gather" if n_g else "attn_fwd", grid=(4, nq),
        in_specs=[blk, full, full] + [_ANY] * n_g, out_specs=[blk, blk] + [_ANY] * n_g,
        out_shape=[jax.ShapeDtypeStruct((T, 512), bf16), jax.ShapeDtypeStruct((T, 512), f32)]
        + [jax.ShapeDtypeStruct(b.shape, b.dtype) for b in bufs],
        input_output_aliases={3 + k: 2 + k for k in range(n_g)},
        scratch_shapes=_sems(n_g, N_DEV - 1) if n_g else [], compiler_params=_cparams(2),
    )(q, k, v, *bufs)
    return res[0], res[1], list(res[2:])


def attn_bwd(q, k, v, do, tot, T, scatter=None):
    nq = T // BQ
    n_s = 0 if scatter is None else len(scatter[0])
    modes = [] if scatter is None else list(scatter[1])

    def body(*refs):
        q_ref, k_ref, v_ref, do_ref, tot_ref = refs[:5]
        dq_ref, dk_ref, dv_ref = refs[5 + 2 * n_s:8 + 2 * n_s]
        p, i = pl.program_id(0), pl.program_id(1)
        if n_s:
            def copies():
                return _chip_copies(refs[5:5 + n_s], refs[8 + 2 * n_s:8 + 3 * n_s], modes, *refs[8 + 3 * n_s:])

            @pl.when(jnp.logical_and(p == 0, i == 0))
            def _():
                for cp in copies():
                    cp.start()

        @pl.when(i == 0)
        def _():
            dk_ref[...] = jnp.zeros_like(dk_ref)
            dv_ref[...] = jnp.zeros_like(dv_ref)

        qs = _stack_heads(q_ref[...])
        dos = _stack_heads(do_ref[...].astype(bf16))
        totv = tot_ref[...]
        tots = jnp.concatenate([totv[:, 0:1], totv[:, HEAD_DIM:HEAD_DIM + 1]], axis=0)
        diff, u_after, u_before = _attn_masks()

        def step(jb, carry):
            dq, cl, cg = carry
            ks = pl.multiple_of(jb * BK, BK)
            kb = k_ref[pl.ds(ks, BK), :]
            vb = v_ref[pl.ds(ks, BK), :]
            m = diff > jb * BK - i * BQ
            z = lax.dot_general(qs, kb, NT_DIMS, preferred_element_type=f32)
            lsp, lsn = _log_sigmoids(z)
            lm = jnp.where(m, lsn, 0.0)
            a = jnp.where(m, jnp.exp(lsp + _split_dot(lm, u_after)), 0.0)
            g = a * lax.dot_general(dos, vb, NT_DIMS, preferred_element_type=f32)
            beta = jnp.exp(lsp)
            aa = jnp.where(m, g * jnp.exp(lsn) - _split_dot(g, u_before) * beta, 0.0)
            bb = jnp.where(m, beta, 0.0)
            cl = cl + jnp.sum(lm, axis=-1, keepdims=True)
            f = jnp.exp(tots - cl)
            dz = (f * aa - cg * bb).astype(bf16)
            cg = cg + f * jnp.sum(g, axis=-1, keepdims=True)
            dq = dq + jnp.dot(dz, kb, preferred_element_type=f32)
            dk_ref[pl.ds(ks, BK), :] += lax.dot_general(dz, qs, TN_DIMS, preferred_element_type=f32)
            dv_ref[pl.ds(ks, BK), :] += lax.dot_general((f * a).astype(bf16), dos, TN_DIMS, preferred_element_type=f32)
            return dq, cl, cg

        zc = jnp.zeros((2 * BQ, 1), f32)
        dq, _, _ = lax.fori_loop(0, last_block(i) + 1, step, (jnp.zeros((2 * BQ, 128), f32), zc, zc))
        dq_ref[...] = _unstack_heads(dq)
        if n_s:
            @pl.when(jnp.logical_and(p == 3, i == nq - 1))
            def _():
                for cp in copies():
                    cp.wait()

    blk = pl.BlockSpec((BQ, 128), lambda p, i: (i, p))
    full = pl.BlockSpec((T, 128), lambda p, i: (0, p))
    shp = jax.ShapeDtypeStruct((T, 512), f32)
    s_list = [] if scatter is None else list(scatter[0])
    bufs = chip_buffers(s_list, modes) if n_s else []
    res = pl.pallas_call(
        body, name="attn_bwd_scatter" if n_s else "attn_bwd", grid=(4, nq),
        in_specs=[blk, full, full, blk, blk] + [_ANY] * (2 * n_s), out_specs=[blk, full, full] + [_ANY] * n_s,
        out_shape=[shp, shp, shp] + [jax.ShapeDtypeStruct(b.shape, b.dtype) for b in bufs],
        input_output_aliases={5 + n_s + k: 3 + k for k in range(n_s)},
        scratch_shapes=_sems(n_s, 3) if n_s else [], compiler_params=_cparams(2),
    )(q, k, v, do, tot, *s_list, *bufs)
    return res[0], res[1], res[2], list(res[3:])


def loss_head(y, target, T, tm=256):
    D = y.shape[1]

    def body(y_ref, t_ref, dy_ref, l_ref):
        i = pl.program_id(0)
        err = y_ref[...] - t_ref[...]
        dy_ref[...] = err * (1.0 / D)
        part = 0.5 * jnp.sum(jnp.sum(err * err, axis=-1, keepdims=True) * (1.0 / D), axis=0, keepdims=True)

        @pl.when(i == 0)
        def _():
            l_ref[...] = jnp.zeros_like(l_ref)

        l_ref[...] += jnp.broadcast_to(part, l_ref.shape)

    spec = pl.BlockSpec((tm, D), lambda i: (i, 0))
    return pl.pallas_call(
        body, name="loss_head", grid=(T // tm,), in_specs=[spec, spec],
        out_specs=[spec, pl.BlockSpec((1, 128), lambda i: (0, 0))],
        out_shape=[jax.ShapeDtypeStruct((T, D), f32), jax.ShapeDtypeStruct((1, 128), f32)],
        compiler_params=_cparams(1),
    )(y, target)


def _row_tile(rows, cols, offset=0, max_elems=128 * 1024):
    best = None
    for t in range(16, rows + 1, 16):
        if rows % t == 0 and offset % t == 0 and t * cols <= max_elems:
            best = t
    return best if best is not None else rows


def adamw(name, w, m, v, parts, layer, prev=None, row_off=0):
    L, R, C = w.shape
    n_parts = parts.shape[0]
    tr = _row_tile(R, C, row_off, max_elems=256 * 1024)
    assert R % tr == 0 and row_off % tr == 0, (name, R, row_off, tr)

    def body(w_ref, m_ref, v_ref, p_ref, *rest):
        g_ref, d_ref, nm_ref, nv_ref = rest[-4:]
        g = p_ref[0].astype(f32)
        for s in range(1, n_parts):
            g = g + p_ref[s].astype(f32)
        wv = w_ref[...]
        mn = ADAM_B1 * m_ref[...] + (1.0 - ADAM_B1) * g
        vn = ADAM_B2 * v_ref[...] + (1.0 - ADAM_B2) * jnp.square(g)
        m_hat = mn / (1.0 - ADAM_B1 ** ADAM_STEP)
        v_hat = vn / (1.0 - ADAM_B2 ** ADAM_STEP)
        g_ref[...] = g
        d_ref[...] = -ADAM_LR * (m_hat / (jnp.sqrt(v_hat) + ADAM_EPS) + ADAM_WD * wv)
        nm_ref[...] = mn
        nv_ref[...] = vn

    spec = pl.BlockSpec((None, tr, C), lambda i: (layer, i, 0))
    shp = jax.ShapeDtypeStruct((L, R, C), f32)
    n_prev = 0 if prev is None else 4
    return pl.pallas_call(
        body, name=name, grid=(R // tr,),
        in_specs=[spec, spec, spec, pl.BlockSpec((n_parts, tr, C), lambda i: (0, row_off // tr + i, 0))] + [_ANY] * n_prev,
        out_specs=[spec] * 4, out_shape=[shp] * 4, input_output_aliases={4 + k: k for k in range(n_prev)},
        compiler_params=_cparams(1),
    )(w, m, v, parts, *(prev or ()))


REPL = ["ln1_g", "sg_ln_g", "sg_ln_b", "sg_w", "sg_b", "cv_b", "cv_ln_g", "cv_ln_b", "q_norm_g", "k_norm_g", "ln2_g",
        "ffn_conv_b"]
SMALL_SHARDED = ["b_gate", "cv_w", "ffn_conv_w"]
BIG = ["w_in", "w_a_out", "w_b_out", "w_c_out", "w_up", "w_out", "w_down"]
COL_SHARDED = ["w_a_out", "w_b_out", "w_c_out"]
TRANSPOSED = ["w_in", "w_up"]


def _pack(arrs, rows):
    flat = jnp.concatenate([a.reshape(-1) for a in arrs])
    return jnp.pad(flat, (0, rows * 128 - flat.shape[0])).reshape(rows, 128)


def _pack_layers(arrs, rows):
    flat = jnp.concatenate([a.reshape(a.shape[0], -1) for a in arrs], axis=1)
    return jnp.pad(flat, ((0, 0), (0, rows * 128 - flat.shape[1]))).reshape(flat.shape[0], rows, 128)


def _unpack_layers(packed, shapes):
    flat = packed.reshape(packed.shape[0], -1)
    out, pos = [], 0
    for s in shapes:
        n = 1
        for d in s[1:]:
            n *= d
        out.append(flat[:, pos:pos + n].reshape(s))
        pos += n
    return out


def _rows_for(shapes, mult):
    n = 0
    for s in shapes:
        k = 1
        for d in s:
            k *= d
        n += k
    rows = -(-n // 128)
    return -(-rows // mult) * mult


W_NAMES = ['ln1_g', 'w_in', 'b_gate', 'sg_ln_g', 'sg_ln_b', 'sg_w', 'sg_b', 'w_a_out', 'cv_w', 'cv_b', 'cv_ln_g', 'cv_ln_b',
           'w_b_out', 'q_norm_g', 'k_norm_g', 'w_c_out', 'w_out', 'ln2_g', 'w_up', 'ffn_conv_w', 'ffn_conv_b', 'w_down']


def _forward_layer(x, P, T, gather=None):
    D = x.shape[1]
    sv = {"x0": x}
    (h1,) = rowwise("rms_fwd", f_rms, [(x, D, 0)], [(P["ln1_g"], (1, D), None)], [(D, D, 0, bf16)], T, 256)
    z = mm_nt("in_proj", h1, P["w_in_t"], bf16, tko=1664, tc=D)
    sv["h1"], sv["z"] = h1, z
    (ya_in,) = rowwise("sgu_fwd", f_sgu, [(z, 512, 0), (z, 512, 1)],
                       [(P["sg_ln_g"], (1, 512), None), (P["sg_ln_b"], (1, 512), None),
                        (P["sg_w"], (8, CHUNK, CHUNK), None), (P["sg_b_t"], (CHUNK, 8), None)],
                       [(512, 512, 0, bf16)], T, CHUNK)
    ya = mm_nn("a_out", ya_in, P["w_a_out"], f32)
    (c0,) = rowwise("glu_fwd", f_glu, [(z, 512, 2), (z, 512, 3)], [], [(512, 512, 0, f32)], T, 256)
    c1 = conv31_fwd(c0, P["cv_w"], P["cv_b"], T)
    (c3,) = rowwise("lnsilu_fwd", f_lnsilu, [(c1, 512, 0)], [(P["cv_ln_g"], (1, 512), None), (P["cv_ln_b"], (1, 512), None)],
                    [(512, 512, 0, bf16)], T, 256)
    yb = mm_nn("b_out", c3, P["w_b_out"], f32)
    q8, kn, vb = rowwise("qkv_fwd", f_qkv, [(z, 512, 4), (z, 512, 5), (z, 512, 6)],
                         [(P["q_norm_g"], (1, HEAD_DIM), None), (P["k_norm_g"], (1, HEAD_DIM), None)],
                         [(512, 512, 0, bf16)] * 3, T, 256)
    o, tot, gathered = attn_fwd(q8, kn, vb, T, gather=gather)
    yc = mm_nn("c_out", o, P["w_c_out"], f32)
    (merged,) = rowwise("merge_fwd", f_merge,
                        [(z, 512, lambda c: 7 + c), (z, 512, lambda c: 9 + c), (z, 512, lambda c: 11 + c),
                         (ya, 512, lambda c: c), (yb, 512, lambda c: c), (yc, 512, lambda c: c)],
                        [(P["b_gate"][k], (1, 512), lambda c: c) for k in range(3)],
                        [(D, 512, lambda c: c, bf16)], T, 256, ncol=2)
    x1 = mm_nn("out_proj", merged, P["w_out"], f32, add=x)
    sv.update(ya_in=ya_in, ya=ya, c0=c0, c1=c1, c3=c3, yb=yb, q8=q8, kn=kn, vb=vb, o=o, tot=tot, yc=yc, merged=merged, x1=x1)
    (h2,) = rowwise("rms_fwd", f_rms, [(x1, D, 0)], [(P["ln2_g"], (1, D), None)], [(D, D, 0, bf16)], T, 256)
    up = mm_nt("up_proj", h2, P["w_up_t"], bf16, tko=1408, tc=D)
    act = ffn_act_fwd(up, P["ffn_conv_w"], P["ffn_conv_b"], T)
    x2 = mm_nn("down_proj", act, P["w_down"], f32, add=x1)
    sv.update(h2=h2, up=up, act=act)
    return x2, sv, gathered


def _backward_layer(dx2, P, sv, T, scatter=None):
    D = dx2.shape[1]
    G = {}
    G["w_down"] = mm_tn("dw_down", sv["act"], dx2, tk=1408)
    dact = mm_nt("d_act", dx2, P["w_down"], f32, tko=1408, tc=D)
    dup, dcw0, dcw1, dcw2, G["ffn_conv_b"] = ffn_act_bwd(sv["up"], P["ffn_conv_w"], P["ffn_conv_b"], dact, T)
    G["ffn_conv_w"] = jnp.concatenate([dcw0, dcw1, dcw2], axis=0)
    G["w_up"] = mm_tn("dw_up", dup, sv["h2"], tk=1408)
    dh2 = mm_nn("d_h2", dup, P["w_up_t"], f32, tk=1408)
    dx1, G["ln2_g"] = rowwise_bwd("rms_bwd", f_rms, [(sv["x1"], D, 0)], [(P["ln2_g"], (1, D), None)], [(dh2, D, 0)],
                                  [f32], T, 256, adds=[(dx2, D, 0)])
    G["w_out"] = mm_tn("dw_out", sv["merged"], dx1, tk=1024)
    dmerged = mm_nt("d_merged", dx1, P["w_out"], f32)
    z = sv["z"]
    dg0, dg1, dg2, dya, dyb, dyc, db0, db1, db2 = rowwise_bwd(
        "merge_bwd", f_merge,
        [(z, 512, lambda c: 7 + c), (z, 512, lambda c: 9 + c), (z, 512, lambda c: 11 + c),
         (sv["ya"], 512, lambda c: c), (sv["yb"], 512, lambda c: c), (sv["yc"], 512, lambda c: c)],
        [(P["b_gate"][k], (1, 512), lambda c: c) for k in range(3)],
        [(dmerged, 512, lambda c: c)], [bf16] * 6, T, 256, ncol=2)
    G["b_gate"] = jnp.concatenate([db0, db1, db2], axis=0)
    G["w_c_out"] = mm_tn("dw_c_out", sv["o"], dyc, n_slab=CHUNK)
    do = mm_nt("d_o", dyc, P["w_c_out"], f32)
    dq8, dkn, dvb, received = attn_bwd(sv["q8"], sv["kn"], sv["vb"], do, sv["tot"], T, scatter=scatter)
    dzq, dzk, dzv, G["q_norm_g"], G["k_norm_g"] = rowwise_bwd(
        "qkv_bwd", f_qkv, [(z, 512, 4), (z, 512, 5), (z, 512, 6)],
        [(P["q_norm_g"], (1, HEAD_DIM), None), (P["k_norm_g"], (1, HEAD_DIM), None)],
        [(dq8, 512, 0), (dkn, 512, 0), (dvb, 512, 0)], [bf16] * 3, T, 256)
    G["w_b_out"] = mm_tn("dw_b_out", sv["c3"], dyb, n_slab=CHUNK)
    dc3 = mm_nt("d_c3", dyb, P["w_b_out"], f32)
    dc1, G["cv_ln_g"], G["cv_ln_b"] = rowwise_bwd(
        "lnsilu_bwd", f_lnsilu, [(sv["c1"], 512, 0)], [(P["cv_ln_g"], (1, 512), None), (P["cv_ln_b"], (1, 512), None)],
        [(dc3, 512, 0)], [f32], T, 256)
    dc0, G["cv_w"], G["cv_b"] = conv31_bwd(sv["c0"], P["cv_w"], dc1, T)
    dzp, dzgl = rowwise_bwd("glu_bwd", f_glu, [(z, 512, 2), (z, 512, 3)], [], [(dc0, 512, 0)], [bf16] * 2, T, 256)
    G["w_a_out"] = mm_tn("dw_a_out", sv["ya_in"], dya, n_slab=CHUNK)
    dya_in = mm_nt("d_ya_in", dya, P["w_a_out"], f32)
    dzu, dzv_a, G["sg_ln_g"], G["sg_ln_b"], G["sg_w"], dsbt = rowwise_bwd(
        "sgu_bwd", f_sgu, [(z, 512, 0), (z, 512, 1)],
        [(P["sg_ln_g"], (1, 512), None), (P["sg_ln_b"], (1, 512), None), (P["sg_w"], (8, CHUNK, CHUNK), None),
         (P["sg_b_t"], (CHUNK, 8), None)],
        [(dya_in, 512, 0)], [bf16] * 2, T, CHUNK)
    G["sg_b"] = dsbt.T
    dz = concat_cols("dz_concat", [dzu, dzv_a, dzp, dzgl, dzq, dzk, dzv, dg0, dg1, dg2], T)
    G["w_in"] = mm_tn("dw_in", dz, sv["h1"], tk=1664)
    dh1 = mm_nn("d_h1", dz, P["w_in_t"], f32, tk=1664)
    dx0, G["ln1_g"] = rowwise_bwd("rms_bwd", f_rms, [(sv["x0"], D, 0)], [(P["ln1_g"], (1, D), None)], [(dh1, D, 0)],
                                  [f32], T, 256, adds=[(dx1, D, 0)])
    return dx0, G, received


def kernel(x, ln1_g, w_in, b_gate, sg_ln_g, sg_ln_b, sg_w, sg_b, w_a_out, cv_w, cv_b, cv_ln_g, cv_ln_b, w_b_out, q_norm_g, k_norm_g, w_c_out, w_out, ln2_g, w_up, ffn_conv_w, ffn_conv_b, w_down, loss_target, m_ln1_g, m_w_in, m_b_gate, m_sg_ln_g, m_sg_ln_b, m_sg_w, m_sg_b, m_w_a_out, m_cv_w, m_cv_b, m_cv_ln_g, m_cv_ln_b, m_w_b_out, m_q_norm_g, m_k_norm_g, m_w_c_out, m_w_out, m_ln2_g, m_w_up, m_ffn_conv_w, m_ffn_conv_b, m_w_down, v_ln1_g, v_w_in, v_b_gate, v_sg_ln_g, v_sg_ln_b, v_sg_w, v_sg_b, v_w_a_out, v_cv_w, v_cv_b, v_cv_ln_g, v_cv_ln_b, v_w_b_out, v_q_norm_g, v_k_norm_g, v_w_c_out, v_w_out, v_ln2_g, v_w_up, v_ffn_conv_w, v_ffn_conv_b, v_w_down):
    W = dict(ln1_g=ln1_g, w_in=w_in, b_gate=b_gate, sg_ln_g=sg_ln_g, sg_ln_b=sg_ln_b, sg_w=sg_w, sg_b=sg_b, w_a_out=w_a_out,
             cv_w=cv_w, cv_b=cv_b, cv_ln_g=cv_ln_g, cv_ln_b=cv_ln_b, w_b_out=w_b_out, q_norm_g=q_norm_g, k_norm_g=k_norm_g,
             w_c_out=w_c_out, w_out=w_out, ln2_g=ln2_g, w_up=w_up, ffn_conv_w=ffn_conv_w, ffn_conv_b=ffn_conv_b, w_down=w_down)
    M = dict(ln1_g=m_ln1_g, w_in=m_w_in, b_gate=m_b_gate, sg_ln_g=m_sg_ln_g, sg_ln_b=m_sg_ln_b, sg_w=m_sg_w, sg_b=m_sg_b,
             w_a_out=m_w_a_out, cv_w=m_cv_w, cv_b=m_cv_b, cv_ln_g=m_cv_ln_g, cv_ln_b=m_cv_ln_b, w_b_out=m_w_b_out,
             q_norm_g=m_q_norm_g, k_norm_g=m_k_norm_g, w_c_out=m_w_c_out, w_out=m_w_out, ln2_g=m_ln2_g, w_up=m_w_up,
             ffn_conv_w=m_ffn_conv_w, ffn_conv_b=m_ffn_conv_b, w_down=m_w_down)
    V = dict(ln1_g=v_ln1_g, w_in=v_w_in, b_gate=v_b_gate, sg_ln_g=v_sg_ln_g, sg_ln_b=v_sg_ln_b, sg_w=v_sg_w, sg_b=v_sg_b,
             w_a_out=v_w_a_out, cv_w=v_cv_w, cv_b=v_cv_b, cv_ln_g=v_cv_ln_g, cv_ln_b=v_cv_ln_b, w_b_out=v_w_b_out,
             q_norm_g=v_q_norm_g, k_norm_g=v_k_norm_g, w_c_out=v_w_c_out, w_out=v_w_out, ln2_g=v_ln2_g, w_up=v_w_up,
             ffn_conv_w=v_ffn_conv_w, ffn_conv_b=v_ffn_conv_b, w_down=v_w_down)
    T, D = x.shape[1], x.shape[2]
    L = DEPTH
    xs = x.reshape(T, D)
    target = loss_target.reshape(T, D)

    ss_shapes = [W[n].shape for n in SMALL_SHARDED]
    ss_rows = _rows_for(ss_shapes, 8)
    (ss_all,) = exchange("gather_small", [_pack([W[n] for n in SMALL_SHARDED], ss_rows)], ["gather"])
    full_small = {}
    pos = 0
    for n in SMALL_SHARDED:
        s = W[n].shape
        cnt = s[0] * s[1] * s[2]
        part = ss_all.reshape(N_DEV, -1)[:, pos:pos + cnt].reshape((N_DEV,) + s)
        full_small[n] = jnp.transpose(part, (1, 2, 0, 3)).reshape(s[0], s[1], N_DEV * s[2])
        pos += cnt

    params, saved = [], []
    cur = xs
    Wt = {n: (jnp.transpose(W[n], (0, 2, 1)) if n in TRANSPOSED else W[n]) for n in BIG}

    def local_slabs(l):
        return [Wt[n][l].astype(bf16) for n in BIG]

    slabs = gather2("gather_big", local_slabs(0))
    for l in range(L):
        P = {}
        for n, s in zip(BIG, slabs):
            if n in TRANSPOSED:
                P[n + "_t"] = s.reshape(-1, s.shape[-1])
            else:
                P[n] = assemble("assemble_" + n, s) if n in COL_SHARDED else s.reshape(-1, s.shape[-1])
        for n in REPL:
            P[n] = W[n][l]
        for n in ("ln1_g", "sg_ln_g", "sg_ln_b", "cv_b", "cv_ln_g", "cv_ln_b", "q_norm_g", "k_norm_g", "ln2_g", "ffn_conv_b"):
            P[n] = P[n].reshape(1, -1)
        P["sg_b_t"] = P["sg_b"].T
        P["cv_w"] = full_small["cv_w"][l]
        P["b_gate"] = [full_small["b_gate"][l][k:k + 1] for k in range(3)]
        P["ffn_conv_w"] = [full_small["ffn_conv_w"][l][k:k + 1] for k in range(3)]
        nxt = local_slabs(l + 1) if l + 1 < L else None
        cur, sv, slabs = _forward_layer(cur, P, T, gather=nxt)
        params.append(P)
        saved.append(sv)

    dy, loss_row = loss_head(cur, target, T)
    loss = lax.psum(loss_row[0, 0], ("x", "y", "c"))

    repl_shapes = [W[n].shape for n in REPL]
    repl_rows = _rows_for([s[1:] for s in repl_shapes], 16)
    ssl_rows = _rows_for([s[1:] for s in ss_shapes], 16)
    state = {"repl": [_pack_layers([X[n] for n in REPL], repl_rows) for X in (W, M, V)],
             "ss": [_pack_layers([X[n] for n in SMALL_SHARDED], ssl_rows) for X in (W, M, V)]}
    big_rows, big_off = {}, {}
    for n in BIG:
        big_off[n] = sum(big_rows.values())
        big_rows[n] = W[n].shape[1] * W[n].shape[2] // D
        state[n] = [(jnp.transpose(X[n], (0, 2, 1)) if n in TRANSPOSED else X[n]).reshape(L, big_rows[n], D)
                    for X in (W, M, V)]
    done = {n: None for n in state}

    def update(layer, received):
        for n in BIG:
            done[n] = adamw("adamw_" + n, *state[n], received[0], layer, done[n], row_off=big_off[n])
        done["repl"] = adamw("adamw_repl", *state["repl"], received[1], layer, done["repl"])
        done["ss"] = adamw("adamw_ss", *state["ss"], received[2], layer, done["ss"])

    def reduce_in_chip(G):
        big = [G[n].reshape(4, 2, big_rows[n], D) for n in BIG]
        ss_parts = []
        for n in SMALL_SHARDED:
            k, c = W[n].shape[1:]
            ss_parts.append(jnp.transpose(G[n].reshape(k, N_DEV, c), (1, 0, 2)).reshape(N_DEV, k * c))
        ss_send = jnp.concatenate(ss_parts, axis=1)
        ss_send = jnp.pad(ss_send, ((0, 0), (0, ssl_rows * 128 - ss_send.shape[1]))).reshape(4, 2, ssl_rows, 128)
        repl = _pack([G[n] for n in REPL], repl_rows)
        core = lax.axis_index("c")
        kept = jnp.concatenate([lax.dynamic_index_in_dim(b, core, 1, keepdims=False) for b in big], axis=1)
        ss_kept = lax.dynamic_index_in_dim(ss_send, core, 1, keepdims=False)
        got, repl_got, ss_got = pair_exchange("pair_exchange", big, [repl, ss_send], ["gather", "scatter"])
        return [add2("pair_add_big", kept, got), add2("pair_add_repl", repl, repl_got), add2("pair_add_ss", ss_kept, ss_got)]

    chip_modes = ["scatter", "gather", "scatter"]
    dcur = dy
    send = None
    for l in reversed(range(L)):
        dcur, G, recv = _backward_layer(dcur, params[l], saved[l], T, scatter=send)
        if send is not None:
            update(l + 1, recv)
        send = (reduce_in_chip(G), chip_modes)
    update(0, chip_exchange("chip_exchange_last", send[0], send[1]))

    results = [{}, {}, {}, {}]
    for k in range(4):
        for n, a in zip(REPL, _unpack_layers(done["repl"][k], repl_shapes)):
            results[k][n] = a
        for n, a in zip(SMALL_SHARDED, _unpack_layers(done["ss"][k], ss_shapes)):
            results[k][n] = a
        for n in BIG:
            results[k][n] = jnp.transpose(done[n][k], (0, 2, 1)) if n in TRANSPOSED else done[n][k].reshape(W[n].shape)
    out = [loss, dcur.reshape(1, T, D)]
    for k in range(4):
        out += [results[k][n] for n in W_NAMES]
    return tuple(out)
```

```python
import functools

import jax
import jax.numpy as jnp
from jax import lax
from jax.experimental import pallas as pl
from jax.experimental.pallas import tpu as pltpu

f32 = jnp.float32
bf16 = jnp.bfloat16

EPS = 1e-6
N_DEV = 8
DEPTH = 4
CHUNK = 128
HEAD_DIM = 64
N_HEADS = 8
CV_KERNEL = 31
VMEM_LIMIT_BYTES = 56 * 2 ** 20

ADAM_LR = 0.001
ADAM_B1 = 0.9
ADAM_B2 = 0.999
ADAM_EPS = 1e-08
ADAM_WD = 0.01
ADAM_STEP = 10

MESH = pl.DeviceIdType.MESH


def _cparams(n_grid):
    return pltpu.CompilerParams(dimension_semantics=("arbitrary",) * n_grid, vmem_limit_bytes=VMEM_LIMIT_BYTES)


def exchange(name, arrays, modes):
    n = len(arrays)

    def body(*refs):
        copies = _direct_copies(refs[:n], refs[n:2 * n], modes, *refs[2 * n:])
        for cp in copies:
            cp.start()
        for cp in copies:
            cp.wait()

    return pl.pallas_call(
        body, name=name, out_shape=_exchange_out_shapes(arrays, modes),
        in_specs=[_ANY] * n, out_specs=[_ANY] * n, scratch_shapes=_exchange_sems(n),
    )(*arrays)


_ANY = pl.BlockSpec(memory_space=pl.ANY)


def _exchange_out_shapes(arrays, modes):
    return [jax.ShapeDtypeStruct((N_DEV,) + tuple(a.shape) if m == "gather" else tuple(a.shape), a.dtype)
            for a, m in zip(arrays, modes)]


def _exchange_sems(n):
    return [pltpu.SemaphoreType.DMA((n, N_DEV - 1)), pltpu.SemaphoreType.DMA((n, N_DEV - 1)), pltpu.SemaphoreType.DMA((n,))]


def _direct_copies(ins, outs, modes, send_sems, recv_sems, local_sems):
    x, y, c = lax.axis_index("x"), lax.axis_index("y"), lax.axis_index("c")
    me = 4 * x + 2 * y + c
    copies = []
    for k in range(len(ins)):
        src_mine = ins[k] if modes[k] == "gather" else ins[k].at[me]
        copies.append(pltpu.make_async_copy(src_mine, outs[k].at[me], local_sems.at[k]))
        for r in range(1, N_DEV):
            px = 1 - x if r & 4 else x
            py = 1 - y if r & 2 else y
            pc = 1 - c if r & 1 else c
            src = ins[k] if modes[k] == "gather" else ins[k].at[4 * px + 2 * py + pc]
            copies.append(pltpu.make_async_remote_copy(
                src_ref=src, dst_ref=outs[k].at[me], send_sem=send_sems.at[k, r - 1], recv_sem=recv_sems.at[k, r - 1],
                device_id=(px, py, pc), device_id_type=MESH))
    return copies


def own_slot_buffer(block, n_slots, index):
    return lax.dynamic_update_slice(lax.empty((n_slots,) + tuple(block.shape), block.dtype), block[None],
                                    (index,) + (0,) * block.ndim)


def _sems(n, m):
    return [pltpu.SemaphoreType.DMA((n, m)), pltpu.SemaphoreType.DMA((n, m))]


def _two_level_gather(bufs, send_sems, recv_sems):
    x, y, c = lax.axis_index("x"), lax.axis_index("y"), lax.axis_index("c")
    me = 4 * x + 2 * y + c
    sibling = (x, y, 1 - c)
    chips = [(1 - x, y), (x, 1 - y), (1 - x, 1 - y)]

    def slot(px, py, pc):
        return 4 * px + 2 * py + pc

    def copy(k, sem, block, to):
        rows = bufs[k].at[block]
        return pltpu.make_async_remote_copy(src_ref=rows, dst_ref=rows, send_sem=send_sems.at[k, sem],
                                            recv_sem=recv_sems.at[k, sem], device_id=to, device_id_type=MESH)

    def first(k):
        return [copy(k, 0, me, sibling)] + [copy(k, 1 + j, me, (*chip, c)) for j, chip in enumerate(chips)]

    def passed_on(k, j):
        return copy(k, 4 + j, slot(*chips[j], c), sibling)

    def start():
        for k in range(len(bufs)):
            for cp in first(k):
                cp.start()

    def forward():
        for k in range(len(bufs)):
            for j in range(3):
                copy(k, 1 + j, slot(*chips[j], c), sibling).wait_recv()
                passed_on(k, j).start()

    def finish():
        for k in range(len(bufs)):
            copy(k, 0, slot(x, y, 1 - c), sibling).wait_recv()
            for j, chip in enumerate(chips):
                copy(k, 4 + j, slot(*chip, 1 - c), sibling).wait_recv()
            for cp in first(k) + [passed_on(k, j) for j in range(3)]:
                cp.wait_send()

    return start, forward, finish


def pair_exchange(name, big, smalls, modes):
    nb, ns = len(big), len(smalls)
    rows = [a.shape[2] for a in big]
    total, width = sum(rows), big[0].shape[3]
    n = nb + ns

    def body(*refs):
        ins, outs = refs[:n], refs[n:n + 1 + ns]
        send_sems, recv_sems = refs[-2:]
        x, y, c = lax.axis_index("x"), lax.axis_index("y"), lax.axis_index("c")
        copies = []

        def remote(k, src, dst):
            copies.append(pltpu.make_async_remote_copy(src_ref=src, dst_ref=dst, send_sem=send_sems.at[k, 0],
                                                       recv_sem=recv_sems.at[k, 0], device_id=(x, y, 1 - c),
                                                       device_id_type=MESH))

        off = 0
        for k in range(nb):
            remote(k, ins[k].at[:, 1 - c], outs[0].at[:, pl.ds(off, rows[k]), :])
            off += rows[k]
        for j in range(ns):
            remote(nb + j, ins[nb + j].at[:, 1 - c] if modes[j] == "scatter" else ins[nb + j], outs[1 + j])
        for cp in copies:
            cp.start()
        for cp in copies:
            cp.wait()

    out_shape = [jax.ShapeDtypeStruct((4, total, width), big[0].dtype)]
    for a, m in zip(smalls, modes):
        out_shape.append(jax.ShapeDtypeStruct((4,) + tuple(a.shape[2:]) if m == "scatter" else tuple(a.shape), a.dtype))
    return pl.pallas_call(
        body, name=name, out_shape=out_shape, in_specs=[_ANY] * n, out_specs=[_ANY] * len(out_shape),
        scratch_shapes=_sems(n, 1),
    )(*big, *smalls)


def add2(name, a, b):
    shape = a.shape
    a2, b2 = a.reshape(-1, shape[-1]), b.reshape(-1, shape[-1])
    R, C = a2.shape
    tr = _row_tile(R, C, max_elems=1200 * 1024)

    def body(a_ref, b_ref, o_ref):
        o_ref[...] = (a_ref[...].astype(f32) + b_ref[...].astype(f32)).astype(o_ref.dtype)

    spec = pl.BlockSpec((tr, C), lambda i: (i, 0))
    return pl.pallas_call(
        body, name=name, grid=(R // tr,), in_specs=[spec, spec], out_specs=spec,
        out_shape=jax.ShapeDtypeStruct((R, C), a.dtype), compiler_params=_cparams(1),
    )(a2, b2).reshape(shape)


def chip_buffers(arrays, modes):
    mine = 2 * lax.axis_index("x") + lax.axis_index("y")
    return [own_slot_buffer(lax.dynamic_index_in_dim(a, mine, 0, keepdims=False) if m == "scatter" else a, 4, mine)
            for a, m in zip(arrays, modes)]


def _chip_copies(ins, bufs, modes, send_sems, recv_sems):
    x, y, c = lax.axis_index("x"), lax.axis_index("y"), lax.axis_index("c")
    mine = 2 * x + y
    copies = []
    for k in range(len(ins)):
        for r in range(1, 4):
            px = 1 - x if r & 2 else x
            py = 1 - y if r & 1 else y
            src = ins[k].at[2 * px + py] if modes[k] == "scatter" else ins[k]
            copies.append(pltpu.make_async_remote_copy(
                src_ref=src, dst_ref=bufs[k].at[mine], send_sem=send_sems.at[k, r - 1], recv_sem=recv_sems.at[k, r - 1],
                device_id=(px, py, c), device_id_type=MESH))
    return copies


def chip_exchange(name, arrays, modes):
    n = len(arrays)

    def body(*refs):
        copies = _chip_copies(refs[:n], refs[2 * n:3 * n], modes, *refs[3 * n:])
        for cp in copies:
            cp.start()
        for cp in copies:
            cp.wait()

    bufs = chip_buffers(arrays, modes)
    return pl.pallas_call(
        body, name=name, out_shape=[jax.ShapeDtypeStruct(b.shape, b.dtype) for b in bufs],
        in_specs=[_ANY] * (2 * n), out_specs=[_ANY] * n, input_output_aliases={n + k: k for k in range(n)},
        scratch_shapes=_sems(n, 3),
    )(*arrays, *bufs)


def gather_buffers(blocks):
    me = 4 * lax.axis_index("x") + 2 * lax.axis_index("y") + lax.axis_index("c")
    return [own_slot_buffer(b, N_DEV, me) for b in blocks]


_HBM = pl.BlockSpec(memory_space=pltpu.HBM)
_SEM = pl.BlockSpec(memory_space=pltpu.SEMAPHORE)
_SPLIT_PARAMS = pltpu.CompilerParams(has_side_effects=pltpu.SideEffectType.DATAFLOW_SIDE_EFFECTING)


def _split_gather_copies(bufs, send_sems, recv_sems):
    x, y, c = lax.axis_index("x"), lax.axis_index("y"), lax.axis_index("c")
    me = 4 * x + 2 * y + c
    copies = []
    for k in range(len(bufs)):
        rows = bufs[k].at[me]
        for r in range(1, N_DEV):
            peer = (1 - x if r & 4 else x, 1 - y if r & 2 else y, 1 - c if r & 1 else c)
            copies.append(pltpu.make_async_remote_copy(
                src_ref=rows, dst_ref=rows, send_sem=send_sems.at[(N_DEV - 1) * k + r - 1],
                recv_sem=recv_sems.at[(N_DEV - 1) * k + r - 1], device_id=peer, device_id_type=MESH))
    return copies


def gather_start(name, blocks):
    n = len(blocks)
    bufs = gather_buffers(blocks)

    def body(*refs):
        for cp in _split_gather_copies(refs[n + 2:2 * n + 2], refs[n], refs[n + 1]):
            cp.start()
        refs[2 * n + 2][...] = jnp.zeros_like(refs[2 * n + 2])

    n_sem = n * (N_DEV - 1)
    res = pl.pallas_call(
        body, name=name,
        out_shape=(pltpu.SemaphoreType.DMA((n_sem,)), pltpu.SemaphoreType.DMA((n_sem,)),
                   *[pltpu.HBM(b.shape, b.dtype) for b in bufs], jax.ShapeDtypeStruct((8, 128), f32)),
        in_specs=[_HBM] * n, out_specs=(_SEM, _SEM, *[_HBM] * n, pl.BlockSpec(memory_space=pltpu.VMEM)),
        input_output_aliases={k: 2 + k for k in range(n)}, compiler_params=_SPLIT_PARAMS,
    )(*[pltpu.with_memory_space_constraint(b, pltpu.HBM) for b in bufs])
    return res[0], res[1], list(res[2:2 + n]), res[2 + n]


def gather_wait(name, send_sems, recv_sems, bufs, after):
    n = len(bufs)

    def body(*refs):
        for cp in _split_gather_copies(refs[:n], refs[n], refs[n + 1]):
            cp.wait_send()
            cp.wait_recv()

    return list(pl.pallas_call(
        body, name=name, out_shape=tuple(pltpu.HBM(b.shape, b.dtype) for b in bufs),
        in_specs=[_HBM] * n + [_SEM, _SEM, _ANY], out_specs=tuple([_HBM] * n),
        input_output_aliases={k: k for k in range(n)}, compiler_params=_SPLIT_PARAMS,
    )(*bufs, send_sems, recv_sems, after))


def gather2(name, arrays):
    n = len(arrays)

    def body(*refs):
        start, forward, finish = _two_level_gather(refs[n:2 * n], *refs[2 * n:])
        start()
        forward()
        finish()

    bufs = gather_buffers(arrays)
    return pl.pallas_call(
        body, name=name, out_shape=[jax.ShapeDtypeStruct(b.shape, b.dtype) for b in bufs],
        in_specs=[_ANY] * n, out_specs=[_ANY] * n, input_output_aliases={k: k for k in range(n)},
        scratch_shapes=_sems(n, N_DEV - 1),
    )(*bufs)


def assemble(name, slabs):
    _, K, Ns = slabs.shape
    g = N_DEV if Ns % 128 == 0 else 2

    def body(w_ref, o_ref):
        o_ref[...] = jnp.concatenate([w_ref[s] for s in range(g)], axis=1)

    return pl.pallas_call(
        body, name=name, grid=(N_DEV // g,),
        in_specs=[pl.BlockSpec((g, K, Ns), lambda m: (m, 0, 0))],
        out_specs=pl.BlockSpec((K, g * Ns), lambda m: (0, m)),
        out_shape=jax.ShapeDtypeStruct((K, N_DEV * Ns), slabs.dtype),
        compiler_params=_cparams(1),
    )(slabs)


def mm_nn(name, a, b, out_dtype, add=None, tm=1024, tn=512, tk=None):
    M, K = a.shape
    N = b.shape[1]
    tm, tn = min(tm, M), min(tn, N)
    tk = K if tk is None else tk
    nk = K // tk
    has_add = add is not None

    def body(a_ref, b_ref, *rest):
        o_ref = rest[1] if has_add else rest[0]
        part = jnp.dot(a_ref[...].astype(bf16), b_ref[...].astype(bf16), preferred_element_type=f32)
        if nk == 1:
            o_ref[...] = (part + rest[0][...] if has_add else part).astype(o_ref.dtype)
            return
        acc_ref = rest[-1]
        k = pl.program_id(2)

        @pl.when(k == 0)
        def _():
            acc_ref[...] = part + rest[0][...] if has_add else part

        @pl.when(k > 0)
        def _():
            acc_ref[...] += part

        @pl.when(k == nk - 1)
        def _():
            o_ref[...] = acc_ref[...].astype(o_ref.dtype)

    in_specs = [pl.BlockSpec((tm, tk), lambda i, j, k: (i, k)), pl.BlockSpec((tk, tn), lambda i, j, k: (k, j))]
    ops = [a, b]
    if has_add:
        in_specs.append(pl.BlockSpec((tm, tn), lambda i, j, k: (i, j)))
        ops.append(add)
    return pl.pallas_call(
        body, name=name, grid=(M // tm, N // tn, nk), in_specs=in_specs,
        out_specs=pl.BlockSpec((tm, tn), lambda i, j, k: (i, j)),
        out_shape=jax.ShapeDtypeStruct((M, N), out_dtype),
        scratch_shapes=[pltpu.VMEM((tm, tn), f32)] if nk > 1 else [], compiler_params=_cparams(3),
    )(*ops)


def concat_cols(name, pieces, T, tm=512):
    widths = [p.shape[1] for p in pieces]
    total = sum(widths)
    tm = min(tm, T)

    def body(*refs):
        o_ref = refs[-1]
        off = 0
        for r, w in zip(refs[:-1], widths):
            o_ref[:, off:off + w] = r[...]
            off += w

    return pl.pallas_call(
        body, name=name, grid=(T // tm,),
        in_specs=[pl.BlockSpec((tm, w), lambda i: (i, 0)) for w in widths],
        out_specs=pl.BlockSpec((tm, total), lambda i: (i, 0)),
        out_shape=jax.ShapeDtypeStruct((T, total), pieces[0].dtype), compiler_params=_cparams(1),
    )(*pieces)


def mm_nt(name, a, b, out_dtype, tm=1024, tko=None, tc=None):
    M, C = a.shape
    Ko = b.shape[0]
    tm = min(tm, M)
    tc = C if tc is None else min(tc, C)
    tko = Ko if tko is None else tko
    nc = C // tc

    def body(a_ref, b_ref, o_ref, *scratch):
        part = lax.dot_general(a_ref[...].astype(bf16), b_ref[...].astype(bf16), (((1,), (1,)), ((), ())),
                               preferred_element_type=f32)
        if nc == 1:
            o_ref[...] = part.astype(o_ref.dtype)
            return
        acc_ref = scratch[0]
        c = pl.program_id(2)

        @pl.when(c == 0)
        def _():
            acc_ref[...] = part

        @pl.when(c > 0)
        def _():
            acc_ref[...] += part

        @pl.when(c == nc - 1)
        def _():
            o_ref[...] = acc_ref[...].astype(o_ref.dtype)

    return pl.pallas_call(
        body, name=name, grid=(M // tm, Ko // tko, nc),
        in_specs=[pl.BlockSpec((tm, tc), lambda i, j, c: (i, c)), pl.BlockSpec((tko, tc), lambda i, j, c: (j, c))],
        out_specs=pl.BlockSpec((tm, tko), lambda i, j, c: (i, j)),
        out_shape=jax.ShapeDtypeStruct((M, Ko), out_dtype),
        scratch_shapes=[pltpu.VMEM((tm, tko), f32)] if nc > 1 else [], compiler_params=_cparams(3),
    )(a, b)


def mm_tn(name, a, b, n_slab=None, tk=512, tn=512):
    T, K = a.shape
    N = b.shape[1]
    tk = min(tk, K)
    if n_slab is None:
        tn = min(tn, N)

        def body(a_ref, b_ref, o_ref):
            o_ref[...] = lax.dot_general(a_ref[...].astype(bf16), b_ref[...].astype(bf16), (((0,), (0,)), ((), ())),
                                         preferred_element_type=f32).astype(o_ref.dtype)

        return pl.pallas_call(
            body, name=name, grid=(N // tn, K // tk),
            in_specs=[pl.BlockSpec((T, tk), lambda j, i: (0, i)), pl.BlockSpec((T, tn), lambda j, i: (0, j))],
            out_specs=pl.BlockSpec((tk, tn), lambda j, i: (i, j)),
            out_shape=jax.ShapeDtypeStruct((K, N), bf16), compiler_params=_cparams(2),
        )(a, b)

    Ns = n_slab
    g = N_DEV if Ns % 128 == 0 else 2
    tn = g * Ns

    def body(a_ref, b_ref, o_ref):
        val = lax.dot_general(a_ref[...].astype(bf16), b_ref[...].astype(bf16), (((0,), (0,)), ((), ())),
                              preferred_element_type=f32)
        for s in range(g):
            o_ref[s] = val[:, s * Ns:(s + 1) * Ns].astype(o_ref.dtype)

    return pl.pallas_call(
        body, name=name, grid=(K // tk, N_DEV // g),
        in_specs=[pl.BlockSpec((T, tk), lambda i, j: (0, i)), pl.BlockSpec((T, tn), lambda i, j: (0, j))],
        out_specs=pl.BlockSpec((g, tk, Ns), lambda i, j: (j, i, 0)),
        out_shape=jax.ShapeDtypeStruct((N_DEV, K, Ns), bf16), compiler_params=_cparams(2),
    )(a, b)


def _tile_spec(w, tm, cb):
    if callable(cb):
        return pl.BlockSpec((tm, w), lambda c, i: (i, cb(c)))
    return pl.BlockSpec((tm, w), lambda c, i: (i, cb))


def _param_spec(block, cb):
    nd = len(block)
    if cb is None:
        return pl.BlockSpec(block, lambda c, i: (0,) * nd)
    return pl.BlockSpec(block, lambda c, i: (0,) * (nd - 1) + (cb(c),))


def rowwise(name, fn, tiled, params, outs, T, tm, ncol=1):
    n_in = len(tiled) + len(params)
    n_t = len(tiled)

    def body(*refs):
        ins = [r[...].astype(f32) for r in refs[:n_t]] + [r[...] for r in refs[n_t:n_in]]
        res = fn(*ins)
        for r, o in zip(refs[n_in:], res):
            r[...] = o.astype(r.dtype)

    return pl.pallas_call(
        body, name=name, grid=(ncol, T // tm),
        in_specs=[_tile_spec(w, tm, cb) for _, w, cb in tiled] + [_param_spec(blk, cb) for _, blk, cb in params],
        out_specs=[_tile_spec(w, tm, cb) for _, w, cb, _ in outs],
        out_shape=[jax.ShapeDtypeStruct((T, cols), dt) for cols, _, _, dt in outs],
        compiler_params=_cparams(2),
    )(*[a for a, _, _ in tiled], *[a for a, _, _ in params])


def rowwise_bwd(name, fn, tiled, params, cts, grads, T, tm, ncol=1, adds=None):
    n_t, n_p, n_c = len(tiled), len(params), len(cts)
    adds = adds or [None] * n_t
    add_list = [(k, a) for k, a in enumerate(adds) if a is not None]
    want = [k for k, g in enumerate(grads) if g is not None]
    n_a = len(add_list)

    def body(*refs):
        pos = 0
        t_refs = refs[pos:pos + n_t]; pos += n_t
        p_refs = refs[pos:pos + n_p]; pos += n_p
        c_refs = refs[pos:pos + n_c]; pos += n_c
        a_refs = refs[pos:pos + n_a]; pos += n_a
        g_refs = refs[pos:pos + len(want)]; pos += len(want)
        pg_refs = refs[pos:pos + n_p]
        primals = [r[...].astype(f32) for r in t_refs] + [r[...] for r in p_refs]
        _, vjp = jax.vjp(fn, *primals)
        g = vjp(tuple(r[...].astype(f32) for r in c_refs))
        add_of = {k: a_refs[n][...] for n, (k, _) in enumerate(add_list)}
        for n, k in enumerate(want):
            val = g[k]
            if k in add_of:
                val = val + add_of[k]
            g_refs[n][...] = val.astype(g_refs[n].dtype)
        i = pl.program_id(1)
        for k in range(n_p):
            @pl.when(i == 0)
            def _(k=k):
                pg_refs[k][...] = g[n_t + k]

            @pl.when(i > 0)
            def _(k=k):
                pg_refs[k][...] += g[n_t + k]

    in_specs = ([_tile_spec(w, tm, cb) for _, w, cb in tiled] + [_param_spec(blk, cb) for _, blk, cb in params]
                + [_tile_spec(w, tm, cb) for _, w, cb in cts] + [_tile_spec(w, tm, cb) for _, (_, w, cb) in add_list])
    ops = ([a for a, _, _ in tiled] + [a for a, _, _ in params] + [a for a, _, _ in cts]
           + [a for _, (a, _, _) in add_list])
    out_specs, out_shape = [], []
    for k in want:
        w = tiled[k][1]
        out_specs.append(_tile_spec(w, tm, lambda c: c))
        out_shape.append(jax.ShapeDtypeStruct((T, ncol * w), grads[k]))
    for a, blk, cb in params:
        out_specs.append(_param_spec(blk, cb))
        out_shape.append(jax.ShapeDtypeStruct(a.shape, f32))
    return pl.pallas_call(
        body, name=name, grid=(ncol, T // tm), in_specs=in_specs, out_specs=out_specs, out_shape=out_shape,
        compiler_params=_cparams(2),
    )(*ops)


@jax.custom_vjp
def _bdot(a, b):
    return jnp.dot(a.astype(bf16), b.astype(bf16), preferred_element_type=f32)


def _bdot_fwd(a, b):
    return _bdot(a, b), (a, b)


def _bdot_bwd(res, ct):
    a, b = res
    ctb = ct.astype(bf16)
    da = lax.dot_general(ctb, b.astype(bf16), (((1,), (1,)), ((), ())), preferred_element_type=f32)
    db = lax.dot_general(a.astype(bf16), ctb, (((0,), (0,)), ((), ())), preferred_element_type=f32)
    return da, db


_bdot.defvjp(_bdot_fwd, _bdot_bwd)


def _layer_norm(x, g, b):
    mu = jnp.mean(x, axis=-1, keepdims=True)
    xc = x - mu
    y = xc * lax.rsqrt(jnp.mean(xc * xc, axis=-1, keepdims=True) + EPS)
    return y * g + b


def f_rms(x, g):
    y = x * lax.rsqrt(jnp.mean(x * x, axis=-1, keepdims=True) + EPS)
    return (y * g,)


def f_sgu(zu, zv, ln_g, ln_b, wm, sgb_t):
    u = jax.nn.gelu(zu)
    vn = _layer_norm(jax.nn.gelu(zv), ln_g, ln_b)
    row = lax.broadcasted_iota(jnp.int32, (CHUNK, CHUNK), 0)
    col = lax.broadcasted_iota(jnp.int32, (CHUNK, CHUNK), 1)
    tril = col <= row
    low = col < HEAD_DIM
    parts = []
    for p in range(4):
        vp = vn[:, CHUNK * p:CHUNK * (p + 1)]
        w0 = jnp.where(tril, wm[2 * p], 0.0)
        w1 = jnp.where(tril, wm[2 * p + 1], 0.0)
        parts.append(_bdot(w0, jnp.where(low, vp, 0.0)) + _bdot(w1, jnp.where(low, 0.0, vp)))
    s = jnp.concatenate(parts, axis=1)
    lane_g = lax.shift_right_logical(lax.broadcasted_iota(jnp.int32, s.shape, 1), 6)
    bias = jnp.zeros_like(s)
    for g in range(8):
        bias = jnp.where(lane_g == g, sgb_t[:, g:g + 1], bias)
    return (u * (s + bias),)


def f_glu(p, gl):
    return (p * jax.nn.sigmoid(gl),)


def f_lnsilu(c1, g, b):
    return (jax.nn.silu(_layer_norm(c1, g, b)),)


def _head_norm(x, g64):
    g = jnp.concatenate([g64] * N_HEADS, axis=1)
    lane_h = lax.shift_right_logical(lax.broadcasted_iota(jnp.int32, x.shape, 1), 6)
    x2 = x * x
    r = jnp.zeros_like(x)
    for h in range(N_HEADS):
        mh = lane_h == h
        ms = jnp.sum(jnp.where(mh, x2, 0.0), axis=-1, keepdims=True) * (1.0 / HEAD_DIM)
        r = jnp.where(mh, lax.rsqrt(ms + EPS), r)
    return (x * r) * g


def f_qkv(zq, zk, zv, qg, kg):
    return (_head_norm(zq, qg) * 0.125, _head_norm(zk, kg), zv)


def f_merge(g0, g1, g2, ya, yb, yc, b0, b1, b2):
    return (jax.nn.sigmoid(g0 + b0) * ya + jax.nn.sigmoid(g1 + b1) * yb + jax.nn.sigmoid(g2 + b2) * yc,)


HALO = 32


def conv31_fwd(x, w, b, T, tt=256):
    C = x.shape[1]
    r = tt // HALO

    def body(x_ref, h_ref, w_ref, b_ref, y_ref, buf):
        i = pl.program_id(0)
        halo = h_ref[...]
        buf[0:HALO, :] = jnp.where(i > 0, halo, jnp.zeros_like(halo))
        buf[HALO:HALO + tt, :] = x_ref[...]
        acc = jnp.zeros((tt, C), f32) + b_ref[...]
        for k in range(CV_KERNEL):
            acc = acc + w_ref[k:k + 1, :] * buf[pl.ds(HALO - (CV_KERNEL - 1) + k, tt), :]
        y_ref[...] = acc

    return pl.pallas_call(
        body, name="conv31_fwd", grid=(T // tt,),
        in_specs=[pl.BlockSpec((tt, C), lambda i: (i, 0)),
                  pl.BlockSpec((HALO, C), lambda i: (jnp.maximum(i * r - 1, 0), 0)),
                  pl.BlockSpec((CV_KERNEL, C), lambda i: (0, 0)), pl.BlockSpec((1, C), lambda i: (0, 0))],
        out_specs=pl.BlockSpec((tt, C), lambda i: (i, 0)),
        out_shape=jax.ShapeDtypeStruct((T, C), f32),
        scratch_shapes=[pltpu.VMEM((HALO + tt, C), f32)], compiler_params=_cparams(1),
    )(x, x, w, b)


def conv31_bwd(x, w, dy, T, tt=256):
    C = x.shape[1]
    r = tt // HALO
    n = T // tt

    def body(x_ref, h_ref, w_ref, dy_ref, dyn_ref, dx_ref, dw_ref, db_ref, xbuf, dbuf):
        i = pl.program_id(0)
        halo = h_ref[...]
        xbuf[0:HALO, :] = jnp.where(i > 0, halo, jnp.zeros_like(halo))
        xbuf[HALO:HALO + tt, :] = x_ref[...]
        nxt = dyn_ref[...]
        dy = dy_ref[...]
        dbuf[0:tt, :] = dy
        dbuf[tt:tt + HALO, :] = jnp.where(i < n - 1, nxt, jnp.zeros_like(nxt))

        @pl.when(i == 0)
        def _():
            dw_ref[...] = jnp.zeros_like(dw_ref)
            db_ref[...] = jnp.zeros_like(db_ref)

        acc = jnp.zeros((tt, C), f32)
        for k in range(CV_KERNEL):
            acc = acc + w_ref[k:k + 1, :] * dbuf[pl.ds(CV_KERNEL - 1 - k, tt), :]
            xs = xbuf[pl.ds(HALO - (CV_KERNEL - 1) + k, tt), :]
            dw_ref[k:k + 1, :] += jnp.sum(dy * xs, axis=0, keepdims=True)
        dx_ref[...] = acc
        db_ref[...] += jnp.sum(dy, axis=0, keepdims=True)

    return pl.pallas_call(
        body, name="conv31_bwd", grid=(n,),
        in_specs=[pl.BlockSpec((tt, C), lambda i: (i, 0)),
                  pl.BlockSpec((HALO, C), lambda i: (jnp.maximum(i * r - 1, 0), 0)),
                  pl.BlockSpec((CV_KERNEL, C), lambda i: (0, 0)),
                  pl.BlockSpec((tt, C), lambda i: (i, 0)),
                  pl.BlockSpec((HALO, C), lambda i: (jnp.minimum((i + 1) * r, n * r - 1), 0))],
        out_specs=[pl.BlockSpec((tt, C), lambda i: (i, 0)), pl.BlockSpec((CV_KERNEL, C), lambda i: (0, 0)),
                   pl.BlockSpec((1, C), lambda i: (0, 0))],
        out_shape=[jax.ShapeDtypeStruct((T, C), f32), jax.ShapeDtypeStruct((CV_KERNEL, C), f32),
                   jax.ShapeDtypeStruct((1, C), f32)],
        scratch_shapes=[pltpu.VMEM((HALO + tt, C), f32), pltpu.VMEM((HALO + tt, C), f32)], compiler_params=_cparams(1),
    )(x, x, w, dy, dy)


FFN_TC = 128
FFN_PAD = 8


def ffn_act_fwd(up, cw, cb, T):
    F = up.shape[1] // 2
    nj = F // FFN_TC
    rc = min(128, T)

    def body(g_ref, v_ref, g0, g1, g2, gb, v0, v1, v2, vb, o_ref, gp, vp):
        zeros = jnp.zeros((FFN_PAD, FFN_TC), f32)
        for p, x_ref in ((gp, g_ref), (vp, v_ref)):
            p[0:FFN_PAD, :] = zeros
            p[FFN_PAD:FFN_PAD + T, :] = x_ref[...].astype(f32)
        wg = (g0[...], g1[...], g2[...], gb[...])
        wv = (v0[...], v1[...], v2[...], vb[...])

        def conv(p, w, r):
            return (w[0] * p[pl.ds(FFN_PAD + r - 2, rc), :] + w[1] * p[pl.ds(FFN_PAD + r - 1, rc), :]
                    + w[2] * p[pl.ds(FFN_PAD + r, rc), :] + w[3])

        for r in range(0, T, rc):
            o_ref[pl.ds(r, rc), :] = (jax.nn.silu(conv(gp, wg, r)) * conv(vp, wv, r)).astype(o_ref.dtype)

    gspec = pl.BlockSpec((T, FFN_TC), lambda j: (0, j))
    vspec = pl.BlockSpec((T, FFN_TC), lambda j: (0, j + nj))
    pg = pl.BlockSpec((1, FFN_TC), lambda j: (0, j))
    pv = pl.BlockSpec((1, FFN_TC), lambda j: (0, j + nj))
    return pl.pallas_call(
        body, name="ffn_act_fwd", grid=(nj,),
        in_specs=[gspec, vspec, pg, pg, pg, pg, pv, pv, pv, pv], out_specs=gspec,
        out_shape=jax.ShapeDtypeStruct((T, F), bf16),
        scratch_shapes=[pltpu.VMEM((FFN_PAD + T, FFN_TC), f32)] * 2, compiler_params=_cparams(1),
    )(up, up, cw[0], cw[1], cw[2], cb, cw[0], cw[1], cw[2], cb)


def ffn_act_bwd(up, cw, cb, dact, T):
    F = up.shape[1] // 2
    nj = F // FFN_TC

    rc = min(128, T)
    ext = rc + FFN_PAD

    def body(g_ref, v_ref, g0, g1, g2, gb, v0, v1, v2, vb, d_ref, dupg_ref, dupv_ref, *rest):
        p_refs, (gp, vp, dp, dgs, dvs) = rest[:8], rest[8:]
        zeros = jnp.zeros((FFN_PAD, FFN_TC), f32)
        for p, x_ref in ((gp, g_ref), (vp, v_ref)):
            p[0:FFN_PAD, :] = zeros
            p[FFN_PAD:FFN_PAD + T, :] = x_ref[...].astype(f32)
            p[FFN_PAD + T:FFN_PAD + T + FFN_PAD, :] = zeros
        dp[0:T, :] = d_ref[...]
        dp[T:T + FFN_PAD, :] = zeros
        wg = (g0[...], g1[...], g2[...], gb[...])
        wv = (v0[...], v1[...], v2[...], vb[...])
        acc = [jnp.zeros((1, FFN_TC), f32) for _ in range(8)]

        def taps(p, r):
            return tuple(p[pl.ds(FFN_PAD + r - s, ext), :] for s in (2, 1, 0))

        for r in range(0, T, rc):
            xg, xv = taps(gp, r), taps(vp, r)
            gc = wg[0] * xg[0] + wg[1] * xg[1] + wg[2] * xg[2] + wg[3]
            vc = wv[0] * xv[0] + wv[1] * xv[1] + wv[2] * xv[2] + wv[3]
            d = dp[pl.ds(r, ext), :]
            sg = jax.nn.sigmoid(gc)
            sides = ((d * vc * (sg * (1.0 + gc * (1.0 - sg))), xg, wg, dgs, dupg_ref), (d * (gc * sg), xv, wv, dvs, dupv_ref))
            for side, (dc, x, w, buf, dup_ref) in enumerate(sides):
                buf[...] = dc
                dc0 = dc[0:rc]
                dup = w[2] * dc0 + w[1] * buf[pl.ds(1, rc), :] + w[0] * buf[pl.ds(2, rc), :]
                dup_ref[pl.ds(r, rc), :] = dup.astype(dup_ref.dtype)
                for k in range(3):
                    acc[4 * side + k] = acc[4 * side + k] + jnp.sum(dc0 * x[k][0:rc], axis=0, keepdims=True)
                acc[4 * side + 3] = acc[4 * side + 3] + jnp.sum(dc0, axis=0, keepdims=True)
        for k in range(8):
            p_refs[k][...] = acc[k]

    gspec = pl.BlockSpec((T, FFN_TC), lambda j: (0, j))
    vspec = pl.BlockSpec((T, FFN_TC), lambda j: (0, j + nj))
    pg = pl.BlockSpec((1, FFN_TC), lambda j: (0, j))
    pv = pl.BlockSpec((1, FFN_TC), lambda j: (0, j + nj))
    res = pl.pallas_call(
        body, name="ffn_act_bwd", grid=(nj,),
        in_specs=[gspec, vspec, pg, pg, pg, pg, pv, pv, pv, pv, gspec],
        out_specs=[gspec, gspec] + [pg] * 8,
        out_shape=[jax.ShapeDtypeStruct((T, F), bf16)] * 2 + [jax.ShapeDtypeStruct((1, F), f32)] * 8,
        scratch_shapes=[pltpu.VMEM((FFN_PAD + T + FFN_PAD, FFN_TC), f32)] * 2 + [pltpu.VMEM((T + FFN_PAD, FFN_TC), f32)]
        + [pltpu.VMEM((ext, FFN_TC), f32)] * 2,
        compiler_params=_cparams(1),
    )(up, up, cw[0], cw[1], cw[2], cb, cw[0], cw[1], cw[2], cb, dact)
    dup = concat_cols("dup_concat", [res[0], res[1]], T)
    return (dup,) + tuple(jnp.concatenate([res[2 + k], res[6 + k]], axis=1) for k in range(4))


BQ = 256
BK = 256
assert BQ % BK == 0
NT_DIMS = (((1,), (1,)), ((), ()))
TN_DIMS = (((0,), (0,)), ((), ()))


def _split_dot(x, u):
    x1 = x.astype(bf16)
    x2 = (x - x1.astype(f32)).astype(bf16)
    n = x.shape[0]
    y = jnp.dot(jnp.concatenate([x1, x2], axis=0), u, preferred_element_type=f32)
    return y[0:n] + y[n:2 * n]


def _log_sigmoids(z):
    sp = jnp.log(1.0 + jnp.exp(-jnp.abs(z)))
    lsp = jnp.minimum(z, 0.0) - sp
    return lsp, lsp - z


def _stack_heads(x):
    head1 = lax.broadcasted_iota(jnp.int32, x.shape, 1) >= HEAD_DIM
    zero = jnp.zeros_like(x)
    return jnp.concatenate([jnp.where(head1, zero, x), jnp.where(head1, x, zero)], axis=0)


def _unstack_heads(y):
    head1 = lax.broadcasted_iota(jnp.int32, (BQ, y.shape[1]), 1) >= HEAD_DIM
    return jnp.where(head1, y[BQ:2 * BQ], y[0:BQ])


def last_block(i):
    return ((i + 1) * BQ - 1) // BK


def _attn_masks():
    row = lax.broadcasted_iota(jnp.int32, (2 * BQ, BK), 0)
    col = lax.broadcasted_iota(jnp.int32, (2 * BQ, BK), 1)
    ur = lax.broadcasted_iota(jnp.int32, (BK, BK), 0)
    uc = lax.broadcasted_iota(jnp.int32, (BK, BK), 1)
    return (row & (BQ - 1)) - col, (ur > uc).astype(bf16), (ur < uc).astype(bf16)


def attn_fwd(q, k, v, T, gather=None):
    nq = T // BQ
    n_g = 0 if gather is None else len(gather)

    def body(*refs):
        q_ref, k_ref, v_ref = refs[:3]
        o_ref, tot_ref = refs[3 + n_g:5 + n_g]
        p, i = pl.program_id(0), pl.program_id(1)
        if n_g:
            start, forward, finish = _two_level_gather(refs[5 + n_g:5 + 2 * n_g], *refs[5 + 2 * n_g:])
            pl.when(jnp.logical_and(p == 0, i == 0))(start)
        qs = _stack_heads(q_ref[...])
        diff, u_after, _ = _attn_masks()

        def step(t, carry):
            acc, c = carry
            jb = last_block(i) - t
            ks = pl.multiple_of(jb * BK, BK)
            kb = k_ref[pl.ds(ks, BK), :]
            vb = v_ref[pl.ds(ks, BK), :]
            m = diff > jb * BK - i * BQ
            z = lax.dot_general(qs, kb, NT_DIMS, preferred_element_type=f32)
            lsp, lsn = _log_sigmoids(z)
            lm = jnp.where(m, lsn, 0.0)
            a = jnp.where(m, jnp.exp(lsp + _split_dot(lm, u_after)), 0.0)
            acc = acc + jnp.exp(c) * jnp.dot(a.astype(bf16), vb, preferred_element_type=f32)
            return acc, c + jnp.sum(lm, axis=-1, keepdims=True)

        acc, c = lax.fori_loop(0, last_block(i) + 1, step, (jnp.zeros((2 * BQ, 128), f32), jnp.zeros((2 * BQ, 1), f32)))
        o_ref[...] = _unstack_heads(acc).astype(o_ref.dtype)
        tot_ref[...] = _unstack_heads(jnp.broadcast_to(c, (2 * BQ, 128)))
        if n_g:
            @pl.when(jnp.logical_and(p == 3, i == nq - 1))
            def _():
                forward()
                finish()

    blk = pl.BlockSpec((BQ, 128), lambda p, i: (i, p))
    full = pl.BlockSpec((T, 128), lambda p, i: (0, p))
    bufs = [] if gather is None else gather_buffers(gather)
    res = pl.pallas_call(
        body, name="attn_fwd_gather" if n_g else "attn_fwd", grid=(4, nq),
        in_specs=[blk, full, full] + [_ANY] * n_g, out_specs=[blk, blk] + [_ANY] * n_g,
        out_shape=[jax.ShapeDtypeStruct((T, 512), bf16), jax.ShapeDtypeStruct((T, 512), f32)]
        + [jax.ShapeDtypeStruct(b.shape, b.dtype) for b in bufs],
        input_output_aliases={3 + k: 2 + k for k in range(n_g)},
        scratch_shapes=_sems(n_g, N_DEV - 1) if n_g else [], compiler_params=_cparams(2),
    )(q, k, v, *bufs)
    return res[0], res[1], list(res[2:])


def attn_bwd(q, k, v, do, tot, T, scatter=None):
    nq = T // BQ
    n_s = 0 if scatter is None else len(scatter[0])
    modes = [] if scatter is None else list(scatter[1])

    def body(*refs):
        q_ref, k_ref, v_ref, do_ref, tot_ref = refs[:5]
        dq_ref, dk_ref, dv_ref = refs[5 + 2 * n_s:8 + 2 * n_s]
        p, i = pl.program_id(0), pl.program_id(1)
        if n_s:
            def copies():
                return _chip_copies(refs[5:5 + n_s], refs[8 + 2 * n_s:8 + 3 * n_s], modes, *refs[8 + 3 * n_s:])

            @pl.when(jnp.logical_and(p == 0, i == 0))
            def _():
                for cp in copies():
                    cp.start()

        @pl.when(i == 0)
        def _():
            dk_ref[...] = jnp.zeros_like(dk_ref)
            dv_ref[...] = jnp.zeros_like(dv_ref)

        qs = _stack_heads(q_ref[...])
        dos = _stack_heads(do_ref[...].astype(bf16))
        totv = tot_ref[...]
        tots = jnp.concatenate([totv[:, 0:1], totv[:, HEAD_DIM:HEAD_DIM + 1]], axis=0)
        diff, u_after, u_before = _attn_masks()

        def step(jb, carry):
            dq, cl, cg = carry
            ks = pl.multiple_of(jb * BK, BK)
            kb = k_ref[pl.ds(ks, BK), :]
            vb = v_ref[pl.ds(ks, BK), :]
            m = diff > jb * BK - i * BQ
            z = lax.dot_general(qs, kb, NT_DIMS, preferred_element_type=f32)
            lsp, lsn = _log_sigmoids(z)
            lm = jnp.where(m, lsn, 0.0)
            a = jnp.where(m, jnp.exp(lsp + _split_dot(lm, u_after)), 0.0)
            g = a * lax.dot_general(dos, vb, NT_DIMS, preferred_element_type=f32)
            beta = jnp.exp(lsp)
            aa = jnp.where(m, g * jnp.exp(lsn) - _split_dot(g, u_before) * beta, 0.0)
            bb = jnp.where(m, beta, 0.0)
            cl = cl + jnp.sum(lm, axis=-1, keepdims=True)
            f = jnp.exp(tots - cl)
            dz = (f * aa - cg * bb).astype(bf16)
            cg = cg + f * jnp.sum(g, axis=-1, keepdims=True)
            dq = dq + jnp.dot(dz, kb, preferred_element_type=f32)
            dk_ref[pl.ds(ks, BK), :] += lax.dot_general(dz, qs, TN_DIMS, preferred_element_type=f32)
            dv_ref[pl.ds(ks, BK), :] += lax.dot_general((f * a).astype(bf16), dos, TN_DIMS, preferred_element_type=f32)
            return dq, cl, cg

        zc = jnp.zeros((2 * BQ, 1), f32)
        dq, _, _ = lax.fori_loop(0, last_block(i) + 1, step, (jnp.zeros((2 * BQ, 128), f32), zc, zc))
        dq_ref[...] = _unstack_heads(dq)
        if n_s:
            @pl.when(jnp.logical_and(p == 3, i == nq - 1))
            def _():
                for cp in copies():
                    cp.wait()

    blk = pl.BlockSpec((BQ, 128), lambda p, i: (i, p))
    full = pl.BlockSpec((T, 128), lambda p, i: (0, p))
    shp = jax.ShapeDtypeStruct((T, 512), f32)
    s_list = [] if scatter is None else list(scatter[0])
    bufs = chip_buffers(s_list, modes) if n_s else []
    res = pl.pallas_call(
        body, name="attn_bwd_scatter" if n_s else "attn_bwd", grid=(4, nq),
        in_specs=[blk, full, full, blk, blk] + [_ANY] * (2 * n_s), out_specs=[blk, full, full] + [_ANY] * n_s,
        out_shape=[shp, shp, shp] + [jax.ShapeDtypeStruct(b.shape, b.dtype) for b in bufs],
        input_output_aliases={5 + n_s + k: 3 + k for k in range(n_s)},
        scratch_shapes=_sems(n_s, 3) if n_s else [], compiler_params=_cparams(2),
    )(q, k, v, do, tot, *s_list, *bufs)
    return res[0], res[1], res[2], list(res[3:])


def loss_head(y, target, T, tm=256):
    D = y.shape[1]

    def body(y_ref, t_ref, dy_ref, l_ref):
        i = pl.program_id(0)
        err = y_ref[...] - t_ref[...]
        dy_ref[...] = err * (1.0 / D)
        part = 0.5 * jnp.sum(jnp.sum(err * err, axis=-1, keepdims=True) * (1.0 / D), axis=0, keepdims=True)

        @pl.when(i == 0)
        def _():
            l_ref[...] = jnp.zeros_like(l_ref)

        l_ref[...] += jnp.broadcast_to(part, l_ref.shape)

    spec = pl.BlockSpec((tm, D), lambda i: (i, 0))
    return pl.pallas_call(
        body, name="loss_head", grid=(T // tm,), in_specs=[spec, spec],
        out_specs=[spec, pl.BlockSpec((1, 128), lambda i: (0, 0))],
        out_shape=[jax.ShapeDtypeStruct((T, D), f32), jax.ShapeDtypeStruct((1, 128), f32)],
        compiler_params=_cparams(1),
    )(y, target)


def _row_tile(rows, cols, offset=0, max_elems=128 * 1024):
    best = None
    for t in range(16, rows + 1, 16):
        if rows % t == 0 and offset % t == 0 and t * cols <= max_elems:
            best = t
    return best if best is not None else rows


def adamw(name, w, m, v, parts, layer, prev=None, row_off=0):
    L, R, C = w.shape
    n_parts = parts.shape[0]
    tr = _row_tile(R, C, row_off, max_elems=256 * 1024)
    assert R % tr == 0 and row_off % tr == 0, (name, R, row_off, tr)

    def body(w_ref, m_ref, v_ref, p_ref, *rest):
        g_ref, d_ref, nm_ref, nv_ref = rest[-4:]
        g = p_ref[0].astype(f32)
        for s in range(1, n_parts):
            g = g + p_ref[s].astype(f32)
        wv = w_ref[...]
        mn = ADAM_B1 * m_ref[...] + (1.0 - ADAM_B1) * g
        vn = ADAM_B2 * v_ref[...] + (1.0 - ADAM_B2) * jnp.square(g)
        m_hat = mn / (1.0 - ADAM_B1 ** ADAM_STEP)
        v_hat = vn / (1.0 - ADAM_B2 ** ADAM_STEP)
        g_ref[...] = g
        d_ref[...] = -ADAM_LR * (m_hat / (jnp.sqrt(v_hat) + ADAM_EPS) + ADAM_WD * wv)
        nm_ref[...] = mn
        nv_ref[...] = vn

    spec = pl.BlockSpec((None, tr, C), lambda i: (layer, i, 0))
    shp = jax.ShapeDtypeStruct((L, R, C), f32)
    n_prev = 0 if prev is None else 4
    return pl.pallas_call(
        body, name=name, grid=(R // tr,),
        in_specs=[spec, spec, spec, pl.BlockSpec((n_parts, tr, C), lambda i: (0, row_off // tr + i, 0))] + [_ANY] * n_prev,
        out_specs=[spec] * 4, out_shape=[shp] * 4, input_output_aliases={4 + k: k for k in range(n_prev)},
        compiler_params=_cparams(1),
    )(w, m, v, parts, *(prev or ()))


REPL = ["ln1_g", "sg_ln_g", "sg_ln_b", "sg_w", "sg_b", "cv_b", "cv_ln_g", "cv_ln_b", "q_norm_g", "k_norm_g", "ln2_g",
        "ffn_conv_b"]
SMALL_SHARDED = ["b_gate", "cv_w", "ffn_conv_w"]
BIG = ["w_in", "w_a_out", "w_b_out", "w_c_out", "w_up", "w_out", "w_down"]
COL_SHARDED = ["w_a_out", "w_b_out", "w_c_out"]
TRANSPOSED = ["w_in", "w_up"]


def _pack(arrs, rows):
    flat = jnp.concatenate([a.reshape(-1) for a in arrs])
    return jnp.pad(flat, (0, rows * 128 - flat.shape[0])).reshape(rows, 128)


def _pack_layers(arrs, rows):
    flat = jnp.concatenate([a.reshape(a.shape[0], -1) for a in arrs], axis=1)
    return jnp.pad(flat, ((0, 0), (0, rows * 128 - flat.shape[1]))).reshape(flat.shape[0], rows, 128)


def _unpack_layers(packed, shapes):
    flat = packed.reshape(packed.shape[0], -1)
    out, pos = [], 0
    for s in shapes:
        n = 1
        for d in s[1:]:
            n *= d
        out.append(flat[:, pos:pos + n].reshape(s))
        pos += n
    return out


def _rows_for(shapes, mult):
    n = 0
    for s in shapes:
        k = 1
        for d in s:
            k *= d
        n += k
    rows = -(-n // 128)
    return -(-rows // mult) * mult


W_NAMES = ['ln1_g', 'w_in', 'b_gate', 'sg_ln_g', 'sg_ln_b', 'sg_w', 'sg_b', 'w_a_out', 'cv_w', 'cv_b', 'cv_ln_g', 'cv_ln_b',
           'w_b_out', 'q_norm_g', 'k_norm_g', 'w_c_out', 'w_out', 'ln2_g', 'w_up', 'ffn_conv_w', 'ffn_conv_b', 'w_down']


def _forward_layer(x, P, T, gather=None):
    D = x.shape[1]
    sv = {"x0": x}
    (h1,) = rowwise("rms_fwd", f_rms, [(x, D, 0)], [(P["ln1_g"], (1, D), None)], [(D, D, 0, bf16)], T, 256)
    z = mm_nt("in_proj", h1, P["w_in_t"], bf16, tko=1664, tc=D)
    sv["h1"], sv["z"] = h1, z
    (ya_in,) = rowwise("sgu_fwd", f_sgu, [(z, 512, 0), (z, 512, 1)],
                       [(P["sg_ln_g"], (1, 512), None), (P["sg_ln_b"], (1, 512), None),
                        (P["sg_w"], (8, CHUNK, CHUNK), None), (P["sg_b_t"], (CHUNK, 8), None)],
                       [(512, 512, 0, bf16)], T, CHUNK)
    ya = mm_nn("a_out", ya_in, P["w_a_out"], f32)
    (c0,) = rowwise("glu_fwd", f_glu, [(z, 512, 2), (z, 512, 3)], [], [(512, 512, 0, f32)], T, 256)
    c1 = conv31_fwd(c0, P["cv_w"], P["cv_b"], T)
    (c3,) = rowwise("lnsilu_fwd", f_lnsilu, [(c1, 512, 0)], [(P["cv_ln_g"], (1, 512), None), (P["cv_ln_b"], (1, 512), None)],
                    [(512, 512, 0, bf16)], T, 256)
    yb = mm_nn("b_out", c3, P["w_b_out"], f32)
    q8, kn, vb = rowwise("qkv_fwd", f_qkv, [(z, 512, 4), (z, 512, 5), (z, 512, 6)],
                         [(P["q_norm_g"], (1, HEAD_DIM), None), (P["k_norm_g"], (1, HEAD_DIM), None)],
                         [(512, 512, 0, bf16)] * 3, T, 256)
    o, tot, gathered = attn_fwd(q8, kn, vb, T, gather=gather)
    yc = mm_nn("c_out", o, P["w_c_out"], f32)
    (merged,) = rowwise("merge_fwd", f_merge,
                        [(z, 512, lambda c: 7 + c), (z, 512, lambda c: 9 + c), (z, 512, lambda c: 11 + c),
                         (ya, 512, lambda c: c), (yb, 512, lambda c: c), (yc, 512, lambda c: c)],
                        [(P["b_gate"][k], (1, 512), lambda c: c) for k in range(3)],
                        [(D, 512, lambda c: c, bf16)], T, 256, ncol=2)
    x1 = mm_nn("out_proj", merged, P["w_out"], f32, add=x)
    sv.update(ya_in=ya_in, ya=ya, c0=c0, c1=c1, c3=c3, yb=yb, q8=q8, kn=kn, vb=vb, o=o, tot=tot, yc=yc, merged=merged, x1=x1)
    (h2,) = rowwise("rms_fwd", f_rms, [(x1, D, 0)], [(P["ln2_g"], (1, D), None)], [(D, D, 0, bf16)], T, 256)
    up = mm_nt("up_proj", h2, P["w_up_t"], bf16, tko=1408, tc=D)
    act = ffn_act_fwd(up, P["ffn_conv_w"], P["ffn_conv_b"], T)
    x2 = mm_nn("down_proj", act, P["w_down"], f32, add=x1)
    sv.update(h2=h2, up=up, act=act)
    return x2, sv, gathered


def _backward_layer(dx2, P, sv, T, scatter=None):
    D = dx2.shape[1]
    G = {}
    G["w_down"] = mm_tn("dw_down", sv["act"], dx2, tk=1408)
    dact = mm_nt("d_act", dx2, P["w_down"], f32, tko=1408, tc=D)
    dup, dcw0, dcw1, dcw2, G["ffn_conv_b"] = ffn_act_bwd(sv["up"], P["ffn_conv_w"], P["ffn_conv_b"], dact, T)
    G["ffn_conv_w"] = jnp.concatenate([dcw0, dcw1, dcw2], axis=0)
    G["w_up"] = mm_tn("dw_up", dup, sv["h2"], tk=1408)
    dh2 = mm_nn("d_h2", dup, P["w_up_t"], f32, tm=512)
    dx1, G["ln2_g"] = rowwise_bwd("rms_bwd", f_rms, [(sv["x1"], D, 0)], [(P["ln2_g"], (1, D), None)], [(dh2, D, 0)],
                                  [f32], T, 256, adds=[(dx2, D, 0)])
    G["w_out"] = mm_tn("dw_out", sv["merged"], dx1, tk=1024)
    dmerged = mm_nt("d_merged", dx1, P["w_out"], f32)
    z = sv["z"]
    dg0, dg1, dg2, dya, dyb, dyc, db0, db1, db2 = rowwise_bwd(
        "merge_bwd", f_merge,
        [(z, 512, lambda c: 7 + c), (z, 512, lambda c: 9 + c), (z, 512, lambda c: 11 + c),
         (sv["ya"], 512, lambda c: c), (sv["yb"], 512, lambda c: c), (sv["yc"], 512, lambda c: c)],
        [(P["b_gate"][k], (1, 512), lambda c: c) for k in range(3)],
        [(dmerged, 512, lambda c: c)], [bf16] * 6, T, 256, ncol=2)
    G["b_gate"] = jnp.concatenate([db0, db1, db2], axis=0)
    G["w_c_out"] = mm_tn("dw_c_out", sv["o"], dyc, n_slab=CHUNK)
    do = mm_nt("d_o", dyc, P["w_c_out"], f32)
    dq8, dkn, dvb, received = attn_bwd(sv["q8"], sv["kn"], sv["vb"], do, sv["tot"], T, scatter=scatter)
    dzq, dzk, dzv, G["q_norm_g"], G["k_norm_g"] = rowwise_bwd(
        "qkv_bwd", f_qkv, [(z, 512, 4), (z, 512, 5), (z, 512, 6)],
        [(P["q_norm_g"], (1, HEAD_DIM), None), (P["k_norm_g"], (1, HEAD_DIM), None)],
        [(dq8, 512, 0), (dkn, 512, 0), (dvb, 512, 0)], [bf16] * 3, T, 256)
    G["w_b_out"] = mm_tn("dw_b_out", sv["c3"], dyb, n_slab=CHUNK)
    dc3 = mm_nt("d_c3", dyb, P["w_b_out"], f32)
    dc1, G["cv_ln_g"], G["cv_ln_b"] = rowwise_bwd(
        "lnsilu_bwd", f_lnsilu, [(sv["c1"], 512, 0)], [(P["cv_ln_g"], (1, 512), None), (P["cv_ln_b"], (1, 512), None)],
        [(dc3, 512, 0)], [f32], T, 256)
    dc0, G["cv_w"], G["cv_b"] = conv31_bwd(sv["c0"], P["cv_w"], dc1, T)
    dzp, dzgl = rowwise_bwd("glu_bwd", f_glu, [(z, 512, 2), (z, 512, 3)], [], [(dc0, 512, 0)], [bf16] * 2, T, 256)
    G["w_a_out"] = mm_tn("dw_a_out", sv["ya_in"], dya, n_slab=CHUNK)
    dya_in = mm_nt("d_ya_in", dya, P["w_a_out"], f32)
    dzu, dzv_a, G["sg_ln_g"], G["sg_ln_b"], G["sg_w"], dsbt = rowwise_bwd(
        "sgu_bwd", f_sgu, [(z, 512, 0), (z, 512, 1)],
        [(P["sg_ln_g"], (1, 512), None), (P["sg_ln_b"], (1, 512), None), (P["sg_w"], (8, CHUNK, CHUNK), None),
         (P["sg_b_t"], (CHUNK, 8), None)],
        [(dya_in, 512, 0)], [bf16] * 2, T, CHUNK)
    G["sg_b"] = dsbt.T
    dz = concat_cols("dz_concat", [dzu, dzv_a, dzp, dzgl, dzq, dzk, dzv, dg0, dg1, dg2], T)
    G["w_in"] = mm_tn("dw_in", dz, sv["h1"], tk=1664)
    dh1 = mm_nn("d_h1", dz, P["w_in_t"], f32, tm=512)
    dx0, G["ln1_g"] = rowwise_bwd("rms_bwd", f_rms, [(sv["x0"], D, 0)], [(P["ln1_g"], (1, D), None)], [(dh1, D, 0)],
                                  [f32], T, 256, adds=[(dx1, D, 0)])
    return dx0, G, received


def kernel(x, ln1_g, w_in, b_gate, sg_ln_g, sg_ln_b, sg_w, sg_b, w_a_out, cv_w, cv_b, cv_ln_g, cv_ln_b, w_b_out, q_norm_g, k_norm_g, w_c_out, w_out, ln2_g, w_up, ffn_conv_w, ffn_conv_b, w_down, loss_target, m_ln1_g, m_w_in, m_b_gate, m_sg_ln_g, m_sg_ln_b, m_sg_w, m_sg_b, m_w_a_out, m_cv_w, m_cv_b, m_cv_ln_g, m_cv_ln_b, m_w_b_out, m_q_norm_g, m_k_norm_g, m_w_c_out, m_w_out, m_ln2_g, m_w_up, m_ffn_conv_w, m_ffn_conv_b, m_w_down, v_ln1_g, v_w_in, v_b_gate, v_sg_ln_g, v_sg_ln_b, v_sg_w, v_sg_b, v_w_a_out, v_cv_w, v_cv_b, v_cv_ln_g, v_cv_ln_b, v_w_b_out, v_q_norm_g, v_k_norm_g, v_w_c_out, v_w_out, v_ln2_g, v_w_up, v_ffn_conv_w, v_ffn_conv_b, v_w_down):
    W = dict(ln1_g=ln1_g, w_in=w_in, b_gate=b_gate, sg_ln_g=sg_ln_g, sg_ln_b=sg_ln_b, sg_w=sg_w, sg_b=sg_b, w_a_out=w_a_out,
             cv_w=cv_w, cv_b=cv_b, cv_ln_g=cv_ln_g, cv_ln_b=cv_ln_b, w_b_out=w_b_out, q_norm_g=q_norm_g, k_norm_g=k_norm_g,
             w_c_out=w_c_out, w_out=w_out, ln2_g=ln2_g, w_up=w_up, ffn_conv_w=ffn_conv_w, ffn_conv_b=ffn_conv_b, w_down=w_down)
    M = dict(ln1_g=m_ln1_g, w_in=m_w_in, b_gate=m_b_gate, sg_ln_g=m_sg_ln_g, sg_ln_b=m_sg_ln_b, sg_w=m_sg_w, sg_b=m_sg_b,
             w_a_out=m_w_a_out, cv_w=m_cv_w, cv_b=m_cv_b, cv_ln_g=m_cv_ln_g, cv_ln_b=m_cv_ln_b, w_b_out=m_w_b_out,
             q_norm_g=m_q_norm_g, k_norm_g=m_k_norm_g, w_c_out=m_w_c_out, w_out=m_w_out, ln2_g=m_ln2_g, w_up=m_w_up,
             ffn_conv_w=m_ffn_conv_w, ffn_conv_b=m_ffn_conv_b, w_down=m_w_down)
    V = dict(ln1_g=v_ln1_g, w_in=v_w_in, b_gate=v_b_gate, sg_ln_g=v_sg_ln_g, sg_ln_b=v_sg_ln_b, sg_w=v_sg_w, sg_b=v_sg_b,
             w_a_out=v_w_a_out, cv_w=v_cv_w, cv_b=v_cv_b, cv_ln_g=v_cv_ln_g, cv_ln_b=v_cv_ln_b, w_b_out=v_w_b_out,
             q_norm_g=v_q_norm_g, k_norm_g=v_k_norm_g, w_c_out=v_w_c_out, w_out=v_w_out, ln2_g=v_ln2_g, w_up=v_w_up,
             ffn_conv_w=v_ffn_conv_w, ffn_conv_b=v_ffn_conv_b, w_down=v_w_down)
    T, D = x.shape[1], x.shape[2]
    L = DEPTH
    xs = x.reshape(T, D)
    target = loss_target.reshape(T, D)

    ss_shapes = [W[n].shape for n in SMALL_SHARDED]
    ss_rows = _rows_for(ss_shapes, 8)
    (ss_all,) = exchange("gather_small", [_pack([W[n] for n in SMALL_SHARDED], ss_rows)], ["gather"])
    full_small = {}
    pos = 0
    for n in SMALL_SHARDED:
        s = W[n].shape
        cnt = s[0] * s[1] * s[2]
        part = ss_all.reshape(N_DEV, -1)[:, pos:pos + cnt].reshape((N_DEV,) + s)
        full_small[n] = jnp.transpose(part, (1, 2, 0, 3)).reshape(s[0], s[1], N_DEV * s[2])
        pos += cnt

    params, saved = [], []
    cur = xs
    Wt = {n: (jnp.transpose(W[n], (0, 2, 1)) if n in TRANSPOSED else W[n]) for n in BIG}

    def local_slabs(l):
        return [Wt[n][l].astype(bf16) for n in BIG]

    slabs = gather2("gather_big", local_slabs(0))
    in_flight = {l: gather_start("gather_start_%d" % l, local_slabs(l)) for l in range(1, L)}
    start_token = sum(in_flight[l][3][0, 0] for l in range(1, L))
    for l in range(L):
        if l > 0:
            send_sems, recv_sems, bufs, _ = in_flight[l]
            slabs = gather_wait("gather_wait_%d" % l, send_sems, recv_sems, bufs, cur)
        P = {}
        for n, s in zip(BIG, slabs):
            if n in TRANSPOSED:
                P[n + "_t"] = s.reshape(-1, s.shape[-1])
            else:
                P[n] = assemble("assemble_" + n, s) if n in COL_SHARDED else s.reshape(-1, s.shape[-1])
        for n in REPL:
            P[n] = W[n][l]
        for n in ("ln1_g", "sg_ln_g", "sg_ln_b", "cv_b", "cv_ln_g", "cv_ln_b", "q_norm_g", "k_norm_g", "ln2_g", "ffn_conv_b"):
            P[n] = P[n].reshape(1, -1)
        P["sg_b_t"] = P["sg_b"].T
        P["cv_w"] = full_small["cv_w"][l]
        P["b_gate"] = [full_small["b_gate"][l][k:k + 1] for k in range(3)]
        P["ffn_conv_w"] = [full_small["ffn_conv_w"][l][k:k + 1] for k in range(3)]
        if l == 0:
            P["ln1_g"] = P["ln1_g"] + start_token
        cur, sv, _ = _forward_layer(cur, P, T)
        params.append(P)
        saved.append(sv)

    dy, loss_row = loss_head(cur, target, T)
    loss = lax.psum(loss_row[0, 0], ("x", "y", "c"))

    repl_shapes = [W[n].shape for n in REPL]
    repl_rows = _rows_for([s[1:] for s in repl_shapes], 16)
    ssl_rows = _rows_for([s[1:] for s in ss_shapes], 16)
    state = {"repl": [_pack_layers([X[n] for n in REPL], repl_rows) for X in (W, M, V)],
             "ss": [_pack_layers([X[n] for n in SMALL_SHARDED], ssl_rows) for X in (W, M, V)]}
    big_rows, big_off = {}, {}
    for n in BIG:
        big_off[n] = sum(big_rows.values())
        big_rows[n] = W[n].shape[1] * W[n].shape[2] // D
        state[n] = [(jnp.transpose(X[n], (0, 2, 1)) if n in TRANSPOSED else X[n]).reshape(L, big_rows[n], D)
                    for X in (W, M, V)]
    done = {n: None for n in state}

    def update(layer, received):
        for n in BIG:
            done[n] = adamw("adamw_" + n, *state[n], received[0], layer, done[n], row_off=big_off[n])
        done["repl"] = adamw("adamw_repl", *state["repl"], received[1], layer, done["repl"])
        done["ss"] = adamw("adamw_ss", *state["ss"], received[2], layer, done["ss"])

    def reduce_in_chip(G):
        big = [G[n].reshape(4, 2, big_rows[n], D) for n in BIG]
        ss_parts = []
        for n in SMALL_SHARDED:
            k, c = W[n].shape[1:]
            ss_parts.append(jnp.transpose(G[n].reshape(k, N_DEV, c), (1, 0, 2)).reshape(N_DEV, k * c))
        ss_send = jnp.concatenate(ss_parts, axis=1)
        ss_send = jnp.pad(ss_send, ((0, 0), (0, ssl_rows * 128 - ss_send.shape[1]))).reshape(4, 2, ssl_rows, 128)
        repl = _pack([G[n] for n in REPL], repl_rows)
        core = lax.axis_index("c")
        kept = jnp.concatenate([lax.dynamic_index_in_dim(b, core, 1, keepdims=False) for b in big], axis=1)
        ss_kept = lax.dynamic_index_in_dim(ss_send, core, 1, keepdims=False)
        got, repl_got, ss_got = pair_exchange("pair_exchange", big, [repl, ss_send], ["gather", "scatter"])
        return [add2("pair_add_big", kept, got), add2("pair_add_repl", repl, repl_got), add2("pair_add_ss", ss_kept, ss_got)]

    chip_modes = ["scatter", "gather", "scatter"]
    dcur = dy
    send = None
    for l in reversed(range(L)):
        dcur, G, recv = _backward_layer(dcur, params[l], saved[l], T, scatter=send)
        if send is not None:
            update(l + 1, recv)
        send = (reduce_in_chip(G), chip_modes)
    update(0, chip_exchange("chip_exchange_last", send[0], send[1]))

    results = [{}, {}, {}, {}]
    for k in range(4):
        for n, a in zip(REPL, _unpack_layers(done["repl"][k], repl_shapes)):
            results[k][n] = a
        for n, a in zip(SMALL_SHARDED, _unpack_layers(done["ss"][k], ss_shapes)):
            results[k][n] = a
        for n in BIG:
            results[k][n] = jnp.transpose(done[n][k], (0, 2, 1)) if n in TRANSPOSED else done[n][k].reshape(W[n].shape)
    out = [loss, dcur.reshape(1, T, D)]
    for k in range(4):
        out += [results[k][n] for n in W_NAMES]
    return tuple(out)
```

```python
import functools

import jax
import jax.numpy as jnp
from jax import lax
from jax.experimental import pallas as pl
from jax.experimental.pallas import tpu as pltpu

f32 = jnp.float32
bf16 = jnp.bfloat16

EPS = 1e-6
N_DEV = 8
DEPTH = 4
CHUNK = 128
HEAD_DIM = 64
N_HEADS = 8
CV_KERNEL = 31
VMEM_LIMIT_BYTES = 56 * 2 ** 20

ADAM_LR = 0.001
ADAM_B1 = 0.9
ADAM_B2 = 0.999
ADAM_EPS = 1e-08
ADAM_WD = 0.01
ADAM_STEP = 10

MESH = pl.DeviceIdType.MESH


def _cparams(n_grid):
    return pltpu.CompilerParams(dimension_semantics=("arbitrary",) * n_grid, vmem_limit_bytes=VMEM_LIMIT_BYTES)


def exchange(name, arrays, modes):
    n = len(arrays)

    def body(*refs):
        copies = _direct_copies(refs[:n], refs[n:2 * n], modes, *refs[2 * n:])
        for cp in copies:
            cp.start()
        for cp in copies:
            cp.wait()

    return pl.pallas_call(
        body, name=name, out_shape=_exchange_out_shapes(arrays, modes),
        in_specs=[_ANY] * n, out_specs=[_ANY] * n, scratch_shapes=_exchange_sems(n),
    )(*arrays)


_ANY = pl.BlockSpec(memory_space=pl.ANY)


def _exchange_out_shapes(arrays, modes):
    return [jax.ShapeDtypeStruct((N_DEV,) + tuple(a.shape) if m == "gather" else tuple(a.shape), a.dtype)
            for a, m in zip(arrays, modes)]


def _exchange_sems(n):
    return [pltpu.SemaphoreType.DMA((n, N_DEV - 1)), pltpu.SemaphoreType.DMA((n, N_DEV - 1)), pltpu.SemaphoreType.DMA((n,))]


def _direct_copies(ins, outs, modes, send_sems, recv_sems, local_sems):
    x, y, c = lax.axis_index("x"), lax.axis_index("y"), lax.axis_index("c")
    me = 4 * x + 2 * y + c
    copies = []
    for k in range(len(ins)):
        src_mine = ins[k] if modes[k] == "gather" else ins[k].at[me]
        copies.append(pltpu.make_async_copy(src_mine, outs[k].at[me], local_sems.at[k]))
        for r in range(1, N_DEV):
            px = 1 - x if r & 4 else x
            py = 1 - y if r & 2 else y
            pc = 1 - c if r & 1 else c
            src = ins[k] if modes[k] == "gather" else ins[k].at[4 * px + 2 * py + pc]
            copies.append(pltpu.make_async_remote_copy(
                src_ref=src, dst_ref=outs[k].at[me], send_sem=send_sems.at[k, r - 1], recv_sem=recv_sems.at[k, r - 1],
                device_id=(px, py, pc), device_id_type=MESH))
    return copies


def own_slot_buffer(block, n_slots, index):
    return lax.dynamic_update_slice(lax.empty((n_slots,) + tuple(block.shape), block.dtype), block[None],
                                    (index,) + (0,) * block.ndim)


def _sems(n, m):
    return [pltpu.SemaphoreType.DMA((n, m)), pltpu.SemaphoreType.DMA((n, m))]


def _two_level_gather(bufs, send_sems, recv_sems):
    x, y, c = lax.axis_index("x"), lax.axis_index("y"), lax.axis_index("c")
    me = 4 * x + 2 * y + c
    sibling = (x, y, 1 - c)
    chips = [(1 - x, y), (x, 1 - y), (1 - x, 1 - y)]

    def slot(px, py, pc):
        return 4 * px + 2 * py + pc

    def copy(k, sem, block, to):
        rows = bufs[k].at[block]
        return pltpu.make_async_remote_copy(src_ref=rows, dst_ref=rows, send_sem=send_sems.at[k, sem],
                                            recv_sem=recv_sems.at[k, sem], device_id=to, device_id_type=MESH)

    def first(k):
        return [copy(k, 0, me, sibling)] + [copy(k, 1 + j, me, (*chip, c)) for j, chip in enumerate(chips)]

    def passed_on(k, j):
        return copy(k, 4 + j, slot(*chips[j], c), sibling)

    def start():
        for k in range(len(bufs)):
            for cp in first(k):
                cp.start()

    def forward():
        for k in range(len(bufs)):
            for j in range(3):
                copy(k, 1 + j, slot(*chips[j], c), sibling).wait_recv()
                passed_on(k, j).start()

    def finish():
        for k in range(len(bufs)):
            copy(k, 0, slot(x, y, 1 - c), sibling).wait_recv()
            for j, chip in enumerate(chips):
                copy(k, 4 + j, slot(*chip, 1 - c), sibling).wait_recv()
            for cp in first(k) + [passed_on(k, j) for j in range(3)]:
                cp.wait_send()

    return start, forward, finish


def pair_exchange(name, big, smalls, modes):
    nb, ns = len(big), len(smalls)
    rows = [a.shape[2] for a in big]
    total, width = sum(rows), big[0].shape[3]
    n = nb + ns

    def body(*refs):
        ins, outs = refs[:n], refs[n:n + 1 + ns]
        send_sems, recv_sems = refs[-2:]
        x, y, c = lax.axis_index("x"), lax.axis_index("y"), lax.axis_index("c")
        copies = []

        def remote(k, src, dst):
            copies.append(pltpu.make_async_remote_copy(src_ref=src, dst_ref=dst, send_sem=send_sems.at[k, 0],
                                                       recv_sem=recv_sems.at[k, 0], device_id=(x, y, 1 - c),
                                                       device_id_type=MESH))

        off = 0
        for k in range(nb):
            remote(k, ins[k].at[:, 1 - c], outs[0].at[:, pl.ds(off, rows[k]), :])
            off += rows[k]
        for j in range(ns):
            remote(nb + j, ins[nb + j].at[:, 1 - c] if modes[j] == "scatter" else ins[nb + j], outs[1 + j])
        for cp in copies:
            cp.start()
        for cp in copies:
            cp.wait()

    out_shape = [jax.ShapeDtypeStruct((4, total, width), big[0].dtype)]
    for a, m in zip(smalls, modes):
        out_shape.append(jax.ShapeDtypeStruct((4,) + tuple(a.shape[2:]) if m == "scatter" else tuple(a.shape), a.dtype))
    return pl.pallas_call(
        body, name=name, out_shape=out_shape, in_specs=[_ANY] * n, out_specs=[_ANY] * len(out_shape),
        scratch_shapes=_sems(n, 1),
    )(*big, *smalls)


def add2(name, a, b):
    shape = a.shape
    a2, b2 = a.reshape(-1, shape[-1]), b.reshape(-1, shape[-1])
    R, C = a2.shape
    tr = _row_tile(R, C, max_elems=1200 * 1024)

    def body(a_ref, b_ref, o_ref):
        o_ref[...] = (a_ref[...].astype(f32) + b_ref[...].astype(f32)).astype(o_ref.dtype)

    spec = pl.BlockSpec((tr, C), lambda i: (i, 0))
    return pl.pallas_call(
        body, name=name, grid=(R // tr,), in_specs=[spec, spec], out_specs=spec,
        out_shape=jax.ShapeDtypeStruct((R, C), a.dtype), compiler_params=_cparams(1),
    )(a2, b2).reshape(shape)


def chip_buffers(arrays, modes):
    mine = 2 * lax.axis_index("x") + lax.axis_index("y")
    return [own_slot_buffer(lax.dynamic_index_in_dim(a, mine, 0, keepdims=False) if m == "scatter" else a, 4, mine)
            for a, m in zip(arrays, modes)]


def _chip_copies(ins, bufs, modes, send_sems, recv_sems):
    x, y, c = lax.axis_index("x"), lax.axis_index("y"), lax.axis_index("c")
    mine = 2 * x + y
    copies = []
    for k in range(len(ins)):
        for r in range(1, 4):
            px = 1 - x if r & 2 else x
            py = 1 - y if r & 1 else y
            src = ins[k].at[2 * px + py] if modes[k] == "scatter" else ins[k]
            copies.append(pltpu.make_async_remote_copy(
                src_ref=src, dst_ref=bufs[k].at[mine], send_sem=send_sems.at[k, r - 1], recv_sem=recv_sems.at[k, r - 1],
                device_id=(px, py, c), device_id_type=MESH))
    return copies


def chip_exchange(name, arrays, modes):
    n = len(arrays)

    def body(*refs):
        copies = _chip_copies(refs[:n], refs[2 * n:3 * n], modes, *refs[3 * n:])
        for cp in copies:
            cp.start()
        for cp in copies:
            cp.wait()

    bufs = chip_buffers(arrays, modes)
    return pl.pallas_call(
        body, name=name, out_shape=[jax.ShapeDtypeStruct(b.shape, b.dtype) for b in bufs],
        in_specs=[_ANY] * (2 * n), out_specs=[_ANY] * n, input_output_aliases={n + k: k for k in range(n)},
        scratch_shapes=_sems(n, 3),
    )(*arrays, *bufs)


def gather_buffers(blocks):
    me = 4 * lax.axis_index("x") + 2 * lax.axis_index("y") + lax.axis_index("c")
    return [own_slot_buffer(b, N_DEV, me) for b in blocks]


_HBM = pl.BlockSpec(memory_space=pltpu.HBM)
_SEM = pl.BlockSpec(memory_space=pltpu.SEMAPHORE)
_SPLIT_PARAMS = pltpu.CompilerParams(has_side_effects=pltpu.SideEffectType.DATAFLOW_SIDE_EFFECTING)


def _split_gather_copies(bufs, send_sems, recv_sems):
    x, y, c = lax.axis_index("x"), lax.axis_index("y"), lax.axis_index("c")
    me = 4 * x + 2 * y + c
    copies = []
    for k in range(len(bufs)):
        rows = bufs[k].at[me]
        for r in range(1, N_DEV):
            peer = (1 - x if r & 4 else x, 1 - y if r & 2 else y, 1 - c if r & 1 else c)
            copies.append(pltpu.make_async_remote_copy(
                src_ref=rows, dst_ref=rows, send_sem=send_sems.at[(N_DEV - 1) * k + r - 1],
                recv_sem=recv_sems.at[(N_DEV - 1) * k + r - 1], device_id=peer, device_id_type=MESH))
    return copies


def gather_start(name, blocks):
    n = len(blocks)
    bufs = gather_buffers(blocks)

    def body(*refs):
        for cp in _split_gather_copies(refs[n + 2:2 * n + 2], refs[n], refs[n + 1]):
            cp.start()
        refs[2 * n + 2][...] = jnp.zeros_like(refs[2 * n + 2])

    n_sem = n * (N_DEV - 1)
    res = pl.pallas_call(
        body, name=name,
        out_shape=(pltpu.SemaphoreType.DMA((n_sem,)), pltpu.SemaphoreType.DMA((n_sem,)),
                   *[pltpu.HBM(b.shape, b.dtype) for b in bufs], jax.ShapeDtypeStruct((8, 128), f32)),
        in_specs=[_HBM] * n, out_specs=(_SEM, _SEM, *[_HBM] * n, pl.BlockSpec(memory_space=pltpu.VMEM)),
        input_output_aliases={k: 2 + k for k in range(n)}, compiler_params=_SPLIT_PARAMS,
    )(*[pltpu.with_memory_space_constraint(b, pltpu.HBM) for b in bufs])
    return res[0], res[1], list(res[2:2 + n]), res[2 + n]


def gather_wait(name, send_sems, recv_sems, bufs, after):
    n = len(bufs)

    def body(*refs):
        for cp in _split_gather_copies(refs[:n], refs[n], refs[n + 1]):
            cp.wait_send()
            cp.wait_recv()

    return list(pl.pallas_call(
        body, name=name, out_shape=tuple(pltpu.HBM(b.shape, b.dtype) for b in bufs),
        in_specs=[_HBM] * n + [_SEM, _SEM, _ANY], out_specs=tuple([_HBM] * n),
        input_output_aliases={k: k for k in range(n)}, compiler_params=_SPLIT_PARAMS,
    )(*bufs, send_sems, recv_sems, after))


def gather2(name, arrays):
    n = len(arrays)

    def body(*refs):
        start, forward, finish = _two_level_gather(refs[n:2 * n], *refs[2 * n:])
        start()
        forward()
        finish()

    bufs = gather_buffers(arrays)
    return pl.pallas_call(
        body, name=name, out_shape=[jax.ShapeDtypeStruct(b.shape, b.dtype) for b in bufs],
        in_specs=[_ANY] * n, out_specs=[_ANY] * n, input_output_aliases={k: k for k in range(n)},
        scratch_shapes=_sems(n, N_DEV - 1),
    )(*bufs)


def assemble(name, slabs):
    _, K, Ns = slabs.shape
    g = N_DEV if Ns % 128 == 0 else 2

    def body(w_ref, o_ref):
        o_ref[...] = jnp.concatenate([w_ref[s] for s in range(g)], axis=1)

    return pl.pallas_call(
        body, name=name, grid=(N_DEV // g,),
        in_specs=[pl.BlockSpec((g, K, Ns), lambda m: (m, 0, 0))],
        out_specs=pl.BlockSpec((K, g * Ns), lambda m: (0, m)),
        out_shape=jax.ShapeDtypeStruct((K, N_DEV * Ns), slabs.dtype),
        compiler_params=_cparams(1),
    )(slabs)


def mm_nn(name, a, b, out_dtype, add=None, tm=1024, tn=512, tk=None):
    M, K = a.shape
    N = b.shape[1]
    tm, tn = min(tm, M), min(tn, N)
    tk = K if tk is None else tk
    nk = K // tk
    has_add = add is not None

    def body(a_ref, b_ref, *rest):
        o_ref = rest[1] if has_add else rest[0]
        part = jnp.dot(a_ref[...].astype(bf16), b_ref[...].astype(bf16), preferred_element_type=f32)
        if nk == 1:
            o_ref[...] = (part + rest[0][...] if has_add else part).astype(o_ref.dtype)
            return
        acc_ref = rest[-1]
        k = pl.program_id(2)

        @pl.when(k == 0)
        def _():
            acc_ref[...] = part + rest[0][...] if has_add else part

        @pl.when(k > 0)
        def _():
            acc_ref[...] += part

        @pl.when(k == nk - 1)
        def _():
            o_ref[...] = acc_ref[...].astype(o_ref.dtype)

    in_specs = [pl.BlockSpec((tm, tk), lambda i, j, k: (i, k)), pl.BlockSpec((tk, tn), lambda i, j, k: (k, j))]
    ops = [a, b]
    if has_add:
        in_specs.append(pl.BlockSpec((tm, tn), lambda i, j, k: (i, j)))
        ops.append(add)
    return pl.pallas_call(
        body, name=name, grid=(M // tm, N // tn, nk), in_specs=in_specs,
        out_specs=pl.BlockSpec((tm, tn), lambda i, j, k: (i, j)),
        out_shape=jax.ShapeDtypeStruct((M, N), out_dtype),
        scratch_shapes=[pltpu.VMEM((tm, tn), f32)] if nk > 1 else [], compiler_params=_cparams(3),
    )(*ops)


def concat_cols(name, pieces, T, tm=512):
    widths = [p.shape[1] for p in pieces]
    total = sum(widths)
    tm = min(tm, T)

    def body(*refs):
        o_ref = refs[-1]
        off = 0
        for r, w in zip(refs[:-1], widths):
            o_ref[:, off:off + w] = r[...]
            off += w

    return pl.pallas_call(
        body, name=name, grid=(T // tm,),
        in_specs=[pl.BlockSpec((tm, w), lambda i: (i, 0)) for w in widths],
        out_specs=pl.BlockSpec((tm, total), lambda i: (i, 0)),
        out_shape=jax.ShapeDtypeStruct((T, total), pieces[0].dtype), compiler_params=_cparams(1),
    )(*pieces)


def mm_nt(name, a, b, out_dtype, tm=1024, tko=None, tc=None):
    M, C = a.shape
    Ko = b.shape[0]
    tm = min(tm, M)
    tc = C if tc is None else min(tc, C)
    tko = Ko if tko is None else tko
    nc = C // tc

    def body(a_ref, b_ref, o_ref, *scratch):
        part = lax.dot_general(a_ref[...].astype(bf16), b_ref[...].astype(bf16), (((1,), (1,)), ((), ())),
                               preferred_element_type=f32)
        if nc == 1:
            o_ref[...] = part.astype(o_ref.dtype)
            return
        acc_ref = scratch[0]
        c = pl.program_id(2)

        @pl.when(c == 0)
        def _():
            acc_ref[...] = part

        @pl.when(c > 0)
        def _():
            acc_ref[...] += part

        @pl.when(c == nc - 1)
        def _():
            o_ref[...] = acc_ref[...].astype(o_ref.dtype)

    return pl.pallas_call(
        body, name=name, grid=(M // tm, Ko // tko, nc),
        in_specs=[pl.BlockSpec((tm, tc), lambda i, j, c: (i, c)), pl.BlockSpec((tko, tc), lambda i, j, c: (j, c))],
        out_specs=pl.BlockSpec((tm, tko), lambda i, j, c: (i, j)),
        out_shape=jax.ShapeDtypeStruct((M, Ko), out_dtype),
        scratch_shapes=[pltpu.VMEM((tm, tko), f32)] if nc > 1 else [], compiler_params=_cparams(3),
    )(a, b)


def mm_tn(name, a, b, n_slab=None, tk=512, tn=512):
    T, K = a.shape
    N = b.shape[1]
    tk = min(tk, K)
    if n_slab is None:
        tn = min(tn, N)

        def body(a_ref, b_ref, o_ref):
            o_ref[...] = lax.dot_general(a_ref[...].astype(bf16), b_ref[...].astype(bf16), (((0,), (0,)), ((), ())),
                                         preferred_element_type=f32).astype(o_ref.dtype)

        return pl.pallas_call(
            body, name=name, grid=(N // tn, K // tk),
            in_specs=[pl.BlockSpec((T, tk), lambda j, i: (0, i)), pl.BlockSpec((T, tn), lambda j, i: (0, j))],
            out_specs=pl.BlockSpec((tk, tn), lambda j, i: (i, j)),
            out_shape=jax.ShapeDtypeStruct((K, N), bf16), compiler_params=_cparams(2),
        )(a, b)

    Ns = n_slab
    g = N_DEV if Ns % 128 == 0 else 2
    tn = g * Ns

    def body(a_ref, b_ref, o_ref):
        val = lax.dot_general(a_ref[...].astype(bf16), b_ref[...].astype(bf16), (((0,), (0,)), ((), ())),
                              preferred_element_type=f32)
        for s in range(g):
            o_ref[s] = val[:, s * Ns:(s + 1) * Ns].astype(o_ref.dtype)

    return pl.pallas_call(
        body, name=name, grid=(K // tk, N_DEV // g),
        in_specs=[pl.BlockSpec((T, tk), lambda i, j: (0, i)), pl.BlockSpec((T, tn), lambda i, j: (0, j))],
        out_specs=pl.BlockSpec((g, tk, Ns), lambda i, j: (j, i, 0)),
        out_shape=jax.ShapeDtypeStruct((N_DEV, K, Ns), bf16), compiler_params=_cparams(2),
    )(a, b)


def _tile_spec(w, tm, cb):
    if callable(cb):
        return pl.BlockSpec((tm, w), lambda c, i: (i, cb(c)))
    return pl.BlockSpec((tm, w), lambda c, i: (i, cb))


def _param_spec(block, cb):
    nd = len(block)
    if cb is None:
        return pl.BlockSpec(block, lambda c, i: (0,) * nd)
    return pl.BlockSpec(block, lambda c, i: (0,) * (nd - 1) + (cb(c),))


def rowwise(name, fn, tiled, params, outs, T, tm, ncol=1):
    n_in = len(tiled) + len(params)
    n_t = len(tiled)

    def body(*refs):
        ins = [r[...].astype(f32) for r in refs[:n_t]] + [r[...] for r in refs[n_t:n_in]]
        res = fn(*ins)
        for r, o in zip(refs[n_in:], res):
            r[...] = o.astype(r.dtype)

    return pl.pallas_call(
        body, name=name, grid=(ncol, T // tm),
        in_specs=[_tile_spec(w, tm, cb) for _, w, cb in tiled] + [_param_spec(blk, cb) for _, blk, cb in params],
        out_specs=[_tile_spec(w, tm, cb) for _, w, cb, _ in outs],
        out_shape=[jax.ShapeDtypeStruct((T, cols), dt) for cols, _, _, dt in outs],
        compiler_params=_cparams(2),
    )(*[a for a, _, _ in tiled], *[a for a, _, _ in params])


def rowwise_bwd(name, fn, tiled, params, cts, grads, T, tm, ncol=1, adds=None):
    n_t, n_p, n_c = len(tiled), len(params), len(cts)
    adds = adds or [None] * n_t
    add_list = [(k, a) for k, a in enumerate(adds) if a is not None]
    want = [k for k, g in enumerate(grads) if g is not None]
    n_a = len(add_list)

    def body(*refs):
        pos = 0
        t_refs = refs[pos:pos + n_t]; pos += n_t
        p_refs = refs[pos:pos + n_p]; pos += n_p
        c_refs = refs[pos:pos + n_c]; pos += n_c
        a_refs = refs[pos:pos + n_a]; pos += n_a
        g_refs = refs[pos:pos + len(want)]; pos += len(want)
        pg_refs = refs[pos:pos + n_p]
        primals = [r[...].astype(f32) for r in t_refs] + [r[...] for r in p_refs]
        _, vjp = jax.vjp(fn, *primals)
        g = vjp(tuple(r[...].astype(f32) for r in c_refs))
        add_of = {k: a_refs[n][...] for n, (k, _) in enumerate(add_list)}
        for n, k in enumerate(want):
            val = g[k]
            if k in add_of:
                val = val + add_of[k]
            g_refs[n][...] = val.astype(g_refs[n].dtype)
        i = pl.program_id(1)
        for k in range(n_p):
            @pl.when(i == 0)
            def _(k=k):
                pg_refs[k][...] = g[n_t + k]

            @pl.when(i > 0)
            def _(k=k):
                pg_refs[k][...] += g[n_t + k]

    in_specs = ([_tile_spec(w, tm, cb) for _, w, cb in tiled] + [_param_spec(blk, cb) for _, blk, cb in params]
                + [_tile_spec(w, tm, cb) for _, w, cb in cts] + [_tile_spec(w, tm, cb) for _, (_, w, cb) in add_list])
    ops = ([a for a, _, _ in tiled] + [a for a, _, _ in params] + [a for a, _, _ in cts]
           + [a for _, (a, _, _) in add_list])
    out_specs, out_shape = [], []
    for k in want:
        w = tiled[k][1]
        out_specs.append(_tile_spec(w, tm, lambda c: c))
        out_shape.append(jax.ShapeDtypeStruct((T, ncol * w), grads[k]))
    for a, blk, cb in params:
        out_specs.append(_param_spec(blk, cb))
        out_shape.append(jax.ShapeDtypeStruct(a.shape, f32))
    return pl.pallas_call(
        body, name=name, grid=(ncol, T // tm), in_specs=in_specs, out_specs=out_specs, out_shape=out_shape,
        compiler_params=_cparams(2),
    )(*ops)


@jax.custom_vjp
def _bdot(a, b):
    return jnp.dot(a.astype(bf16), b.astype(bf16), preferred_element_type=f32)


def _bdot_fwd(a, b):
    return _bdot(a, b), (a, b)


def _bdot_bwd(res, ct):
    a, b = res
    ctb = ct.astype(bf16)
    da = lax.dot_general(ctb, b.astype(bf16), (((1,), (1,)), ((), ())), preferred_element_type=f32)
    db = lax.dot_general(a.astype(bf16), ctb, (((0,), (0,)), ((), ())), preferred_element_type=f32)
    return da, db


_bdot.defvjp(_bdot_fwd, _bdot_bwd)


def _layer_norm(x, g, b):
    mu = jnp.mean(x, axis=-1, keepdims=True)
    xc = x - mu
    y = xc * lax.rsqrt(jnp.mean(xc * xc, axis=-1, keepdims=True) + EPS)
    return y * g + b


def f_rms(x, g):
    y = x * lax.rsqrt(jnp.mean(x * x, axis=-1, keepdims=True) + EPS)
    return (y * g,)


def f_sgu(zu, zv, ln_g, ln_b, wm, sgb_t):
    u = jax.nn.gelu(zu)
    vn = _layer_norm(jax.nn.gelu(zv), ln_g, ln_b)
    row = lax.broadcasted_iota(jnp.int32, (CHUNK, CHUNK), 0)
    col = lax.broadcasted_iota(jnp.int32, (CHUNK, CHUNK), 1)
    tril = col <= row
    low = col < HEAD_DIM
    parts = []
    for p in range(4):
        vp = vn[:, CHUNK * p:CHUNK * (p + 1)]
        w0 = jnp.where(tril, wm[2 * p], 0.0)
        w1 = jnp.where(tril, wm[2 * p + 1], 0.0)
        parts.append(_bdot(w0, jnp.where(low, vp, 0.0)) + _bdot(w1, jnp.where(low, 0.0, vp)))
    s = jnp.concatenate(parts, axis=1)
    lane_g = lax.shift_right_logical(lax.broadcasted_iota(jnp.int32, s.shape, 1), 6)
    bias = jnp.zeros_like(s)
    for g in range(8):
        bias = jnp.where(lane_g == g, sgb_t[:, g:g + 1], bias)
    return (u * (s + bias),)


def f_glu(p, gl):
    return (p * jax.nn.sigmoid(gl),)


def f_lnsilu(c1, g, b):
    return (jax.nn.silu(_layer_norm(c1, g, b)),)


def _head_norm(x, g64):
    g = jnp.concatenate([g64] * N_HEADS, axis=1)
    lane_h = lax.shift_right_logical(lax.broadcasted_iota(jnp.int32, x.shape, 1), 6)
    x2 = x * x
    r = jnp.zeros_like(x)
    for h in range(N_HEADS):
        mh = lane_h == h
        ms = jnp.sum(jnp.where(mh, x2, 0.0), axis=-1, keepdims=True) * (1.0 / HEAD_DIM)
        r = jnp.where(mh, lax.rsqrt(ms + EPS), r)
    return (x * r) * g


def f_qkv(zq, zk, zv, qg, kg):
    return (_head_norm(zq, qg) * 0.125, _head_norm(zk, kg), zv)


def f_merge(g0, g1, g2, ya, yb, yc, b0, b1, b2):
    return (jax.nn.sigmoid(g0 + b0) * ya + jax.nn.sigmoid(g1 + b1) * yb + jax.nn.sigmoid(g2 + b2) * yc,)


HALO = 32


def conv31_fwd(x, w, b, T, tt=256):
    C = x.shape[1]
    r = tt // HALO

    def body(x_ref, h_ref, w_ref, b_ref, y_ref, buf):
        i = pl.program_id(0)
        halo = h_ref[...]
        buf[0:HALO, :] = jnp.where(i > 0, halo, jnp.zeros_like(halo))
        buf[HALO:HALO + tt, :] = x_ref[...]
        acc = jnp.zeros((tt, C), f32) + b_ref[...]
        for k in range(CV_KERNEL):
            acc = acc + w_ref[k:k + 1, :] * buf[pl.ds(HALO - (CV_KERNEL - 1) + k, tt), :]
        y_ref[...] = acc

    return pl.pallas_call(
        body, name="conv31_fwd", grid=(T // tt,),
        in_specs=[pl.BlockSpec((tt, C), lambda i: (i, 0)),
                  pl.BlockSpec((HALO, C), lambda i: (jnp.maximum(i * r - 1, 0), 0)),
                  pl.BlockSpec((CV_KERNEL, C), lambda i: (0, 0)), pl.BlockSpec((1, C), lambda i: (0, 0))],
        out_specs=pl.BlockSpec((tt, C), lambda i: (i, 0)),
        out_shape=jax.ShapeDtypeStruct((T, C), f32),
        scratch_shapes=[pltpu.VMEM((HALO + tt, C), f32)], compiler_params=_cparams(1),
    )(x, x, w, b)


def conv31_bwd(x, w, dy, T, tt=256):
    C = x.shape[1]
    r = tt // HALO
    n = T // tt

    def body(x_ref, h_ref, w_ref, dy_ref, dyn_ref, dx_ref, dw_ref, db_ref, xbuf, dbuf):
        i = pl.program_id(0)
        halo = h_ref[...]
        xbuf[0:HALO, :] = jnp.where(i > 0, halo, jnp.zeros_like(halo))
        xbuf[HALO:HALO + tt, :] = x_ref[...]
        nxt = dyn_ref[...]
        dy = dy_ref[...]
        dbuf[0:tt, :] = dy
        dbuf[tt:tt + HALO, :] = jnp.where(i < n - 1, nxt, jnp.zeros_like(nxt))

        @pl.when(i == 0)
        def _():
            dw_ref[...] = jnp.zeros_like(dw_ref)
            db_ref[...] = jnp.zeros_like(db_ref)

        acc = jnp.zeros((tt, C), f32)
        for k in range(CV_KERNEL):
            acc = acc + w_ref[k:k + 1, :] * dbuf[pl.ds(CV_KERNEL - 1 - k, tt), :]
            xs = xbuf[pl.ds(HALO - (CV_KERNEL - 1) + k, tt), :]
            dw_ref[k:k + 1, :] += jnp.sum(dy * xs, axis=0, keepdims=True)
        dx_ref[...] = acc
        db_ref[...] += jnp.sum(dy, axis=0, keepdims=True)

    return pl.pallas_call(
        body, name="conv31_bwd", grid=(n,),
        in_specs=[pl.BlockSpec((tt, C), lambda i: (i, 0)),
                  pl.BlockSpec((HALO, C), lambda i: (jnp.maximum(i * r - 1, 0), 0)),
                  pl.BlockSpec((CV_KERNEL, C), lambda i: (0, 0)),
                  pl.BlockSpec((tt, C), lambda i: (i, 0)),
                  pl.BlockSpec((HALO, C), lambda i: (jnp.minimum((i + 1) * r, n * r - 1), 0))],
        out_specs=[pl.BlockSpec((tt, C), lambda i: (i, 0)), pl.BlockSpec((CV_KERNEL, C), lambda i: (0, 0)),
                   pl.BlockSpec((1, C), lambda i: (0, 0))],
        out_shape=[jax.ShapeDtypeStruct((T, C), f32), jax.ShapeDtypeStruct((CV_KERNEL, C), f32),
                   jax.ShapeDtypeStruct((1, C), f32)],
        scratch_shapes=[pltpu.VMEM((HALO + tt, C), f32), pltpu.VMEM((HALO + tt, C), f32)], compiler_params=_cparams(1),
    )(x, x, w, dy, dy)


FFN_TC = 128
FFN_PAD = 8


def ffn_act_fwd(up, cw, cb, T):
    F = up.shape[1] // 2
    nj = F // FFN_TC
    rc = min(128, T)

    def body(g_ref, v_ref, g0, g1, g2, gb, v0, v1, v2, vb, o_ref, gp, vp):
        zeros = jnp.zeros((FFN_PAD, FFN_TC), f32)
        for p, x_ref in ((gp, g_ref), (vp, v_ref)):
            p[0:FFN_PAD, :] = zeros
            p[FFN_PAD:FFN_PAD + T, :] = x_ref[...].astype(f32)
        wg = (g0[...], g1[...], g2[...], gb[...])
        wv = (v0[...], v1[...], v2[...], vb[...])

        def conv(p, w, r):
            return (w[0] * p[pl.ds(FFN_PAD + r - 2, rc), :] + w[1] * p[pl.ds(FFN_PAD + r - 1, rc), :]
                    + w[2] * p[pl.ds(FFN_PAD + r, rc), :] + w[3])

        for r in range(0, T, rc):
            o_ref[pl.ds(r, rc), :] = (jax.nn.silu(conv(gp, wg, r)) * conv(vp, wv, r)).astype(o_ref.dtype)

    gspec = pl.BlockSpec((T, FFN_TC), lambda j: (0, j))
    vspec = pl.BlockSpec((T, FFN_TC), lambda j: (0, j + nj))
    pg = pl.BlockSpec((1, FFN_TC), lambda j: (0, j))
    pv = pl.BlockSpec((1, FFN_TC), lambda j: (0, j + nj))
    return pl.pallas_call(
        body, name="ffn_act_fwd", grid=(nj,),
        in_specs=[gspec, vspec, pg, pg, pg, pg, pv, pv, pv, pv], out_specs=gspec,
        out_shape=jax.ShapeDtypeStruct((T, F), bf16),
        scratch_shapes=[pltpu.VMEM((FFN_PAD + T, FFN_TC), f32)] * 2, compiler_params=_cparams(1),
    )(up, up, cw[0], cw[1], cw[2], cb, cw[0], cw[1], cw[2], cb)


def ffn_act_bwd(up, cw, cb, dact, T):
    F = up.shape[1] // 2
    nj = F // FFN_TC

    rc = min(128, T)
    ext = rc + FFN_PAD

    def body(g_ref, v_ref, g0, g1, g2, gb, v0, v1, v2, vb, d_ref, dupg_ref, dupv_ref, *rest):
        p_refs, (gp, vp, dp, dgs, dvs) = rest[:8], rest[8:]
        zeros = jnp.zeros((FFN_PAD, FFN_TC), f32)
        for p, x_ref in ((gp, g_ref), (vp, v_ref)):
            p[0:FFN_PAD, :] = zeros
            p[FFN_PAD:FFN_PAD + T, :] = x_ref[...].astype(f32)
            p[FFN_PAD + T:FFN_PAD + T + FFN_PAD, :] = zeros
        dp[0:T, :] = d_ref[...]
        dp[T:T + FFN_PAD, :] = zeros
        wg = (g0[...], g1[...], g2[...], gb[...])
        wv = (v0[...], v1[...], v2[...], vb[...])
        acc = [jnp.zeros((1, FFN_TC), f32) for _ in range(8)]

        def taps(p, r):
            return tuple(p[pl.ds(FFN_PAD + r - s, ext), :] for s in (2, 1, 0))

        for r in range(0, T, rc):
            xg, xv = taps(gp, r), taps(vp, r)
            gc = wg[0] * xg[0] + wg[1] * xg[1] + wg[2] * xg[2] + wg[3]
            vc = wv[0] * xv[0] + wv[1] * xv[1] + wv[2] * xv[2] + wv[3]
            d = dp[pl.ds(r, ext), :]
            sg = jax.nn.sigmoid(gc)
            sides = ((d * vc * (sg * (1.0 + gc * (1.0 - sg))), xg, wg, dgs, dupg_ref), (d * (gc * sg), xv, wv, dvs, dupv_ref))
            for side, (dc, x, w, buf, dup_ref) in enumerate(sides):
                buf[...] = dc
                dc0 = dc[0:rc]
                dup = w[2] * dc0 + w[1] * buf[pl.ds(1, rc), :] + w[0] * buf[pl.ds(2, rc), :]
                dup_ref[pl.ds(r, rc), :] = dup.astype(dup_ref.dtype)
                for k in range(3):
                    acc[4 * side + k] = acc[4 * side + k] + jnp.sum(dc0 * x[k][0:rc], axis=0, keepdims=True)
                acc[4 * side + 3] = acc[4 * side + 3] + jnp.sum(dc0, axis=0, keepdims=True)
        for k in range(8):
            p_refs[k][...] = acc[k]

    gspec = pl.BlockSpec((T, FFN_TC), lambda j: (0, j))
    vspec = pl.BlockSpec((T, FFN_TC), lambda j: (0, j + nj))
    pg = pl.BlockSpec((1, FFN_TC), lambda j: (0, j))
    pv = pl.BlockSpec((1, FFN_TC), lambda j: (0, j + nj))
    res = pl.pallas_call(
        body, name="ffn_act_bwd", grid=(nj,),
        in_specs=[gspec, vspec, pg, pg, pg, pg, pv, pv, pv, pv, gspec],
        out_specs=[gspec, gspec] + [pg] * 8,
        out_shape=[jax.ShapeDtypeStruct((T, F), bf16)] * 2 + [jax.ShapeDtypeStruct((1, F), f32)] * 8,
        scratch_shapes=[pltpu.VMEM((FFN_PAD + T + FFN_PAD, FFN_TC), f32)] * 2 + [pltpu.VMEM((T + FFN_PAD, FFN_TC), f32)]
        + [pltpu.VMEM((ext, FFN_TC), f32)] * 2,
        compiler_params=_cparams(1),
    )(up, up, cw[0], cw[1], cw[2], cb, cw[0], cw[1], cw[2], cb, dact)
    dup = concat_cols("dup_concat", [res[0], res[1]], T)
    return (dup,) + tuple(jnp.concatenate([res[2 + k], res[6 + k]], axis=1) for k in range(4))


BQ = 256
BK = 256
assert BQ % BK == 0
NT_DIMS = (((1,), (1,)), ((), ()))
TN_DIMS = (((0,), (0,)), ((), ()))


def _split_dot(x, u):
    x1 = x.astype(bf16)
    x2 = (x - x1.astype(f32)).astype(bf16)
    n = x.shape[0]
    y = jnp.dot(jnp.concatenate([x1, x2], axis=0), u, preferred_element_type=f32)
    return y[0:n] + y[n:2 * n]


def _log_sigmoids(z):
    sp = jnp.log(1.0 + jnp.exp(-jnp.abs(z)))
    lsp = jnp.minimum(z, 0.0) - sp
    return lsp, lsp - z


def _stack_heads(x):
    head1 = lax.broadcasted_iota(jnp.int32, x.shape, 1) >= HEAD_DIM
    zero = jnp.zeros_like(x)
    return jnp.concatenate([jnp.where(head1, zero, x), jnp.where(head1, x, zero)], axis=0)


def _unstack_heads(y):
    head1 = lax.broadcasted_iota(jnp.int32, (BQ, y.shape[1]), 1) >= HEAD_DIM
    return jnp.where(head1, y[BQ:2 * BQ], y[0:BQ])


def last_block(i):
    return ((i + 1) * BQ - 1) // BK


def _attn_masks():
    row = lax.broadcasted_iota(jnp.int32, (2 * BQ, BK), 0)
    col = lax.broadcasted_iota(jnp.int32, (2 * BQ, BK), 1)
    ur = lax.broadcasted_iota(jnp.int32, (BK, BK), 0)
    uc = lax.broadcasted_iota(jnp.int32, (BK, BK), 1)
    return (row & (BQ - 1)) - col, (ur > uc).astype(bf16), (ur < uc).astype(bf16)


def attn_fwd(q, k, v, T, gather=None):
    nq = T // BQ
    n_g = 0 if gather is None else len(gather)

    def body(*refs):
        q_ref, k_ref, v_ref = refs[:3]
        o_ref, tot_ref = refs[3 + n_g:5 + n_g]
        p, i = pl.program_id(0), pl.program_id(1)
        if n_g:
            start, forward, finish = _two_level_gather(refs[5 + n_g:5 + 2 * n_g], *refs[5 + 2 * n_g:])
            pl.when(jnp.logical_and(p == 0, i == 0))(start)
        qs = _stack_heads(q_ref[...])
        diff, u_after, _ = _attn_masks()

        def step(t, carry):
            acc, c = carry
            jb = last_block(i) - t
            ks = pl.multiple_of(jb * BK, BK)
            kb = k_ref[pl.ds(ks, BK), :]
            vb = v_ref[pl.ds(ks, BK), :]
            m = diff > jb * BK - i * BQ
            z = lax.dot_general(qs, kb, NT_DIMS, preferred_element_type=f32)
            lsp, lsn = _log_sigmoids(z)
            lm = jnp.where(m, lsn, 0.0)
            a = jnp.where(m, jnp.exp(lsp + _split_dot(lm, u_after)), 0.0)
            acc = acc + jnp.exp(c) * jnp.dot(a.astype(bf16), vb, preferred_element_type=f32)
            return acc, c + jnp.sum(lm, axis=-1, keepdims=True)

        acc, c = lax.fori_loop(0, last_block(i) + 1, step, (jnp.zeros((2 * BQ, 128), f32), jnp.zeros((2 * BQ, 1), f32)))
        o_ref[...] = _unstack_heads(acc).astype(o_ref.dtype)
        tot_ref[...] = _unstack_heads(jnp.broadcast_to(c, (2 * BQ, 128)))
        if n_g:
            @pl.when(jnp.logical_and(p == 3, i == nq - 1))
            def _():
                forward()
                finish()

    blk = pl.BlockSpec((BQ, 128), lambda p, i: (i, p))
    full = pl.BlockSpec((T, 128), lambda p, i: (0, p))
    bufs = [] if gather is None else gather_buffers(gather)
    res = pl.pallas_call(
        body, name="attn_fwd_gather" if n_g else "attn_fwd", grid=(4, nq),
        in_specs=[blk, full, full] + [_ANY] * n_g, out_specs=[blk, blk] + [_ANY] * n_g,
        out_shape=[jax.ShapeDtypeStruct((T, 512), bf16), jax.ShapeDtypeStruct((T, 512), f32)]
        + [jax.ShapeDtypeStruct(b.shape, b.dtype) for b in bufs],
        input_output_aliases={3 + k: 2 + k for k in range(n_g)},
        scratch_shapes=_sems(n_g, N_DEV - 1) if n_g else [], compiler_params=_cparams(2),
    )(q, k, v, *bufs)
    return res[0], res[1], list(res[2:])


def attn_bwd(q, k, v, do, tot, T, scatter=None):
    nq = T // BQ
    n_s = 0 if scatter is None else len(scatter[0])
    modes = [] if scatter is None else list(scatter[1])

    def body(*refs):
        q_ref, k_ref, v_ref, do_ref, tot_ref = refs[:5]
        dq_ref, dk_ref, dv_ref = refs[5 + 2 * n_s:8 + 2 * n_s]
        p, i = pl.program_id(0), pl.program_id(1)
        if n_s:
            def copies():
                return _chip_copies(refs[5:5 + n_s], refs[8 + 2 * n_s:8 + 3 * n_s], modes, *refs[8 + 3 * n_s:])

            @pl.when(jnp.logical_and(p == 0, i == 0))
            def _():
                for cp in copies():
                    cp.start()

        @pl.when(i == 0)
        def _():
            dk_ref[...] = jnp.zeros_like(dk_ref)
            dv_ref[...] = jnp.zeros_like(dv_ref)

        qs = _stack_heads(q_ref[...])
        dos = _stack_heads(do_ref[...].astype(bf16))
        totv = tot_ref[...]
        tots = jnp.concatenate([totv[:, 0:1], totv[:, HEAD_DIM:HEAD_DIM + 1]], axis=0)
        diff, u_after, u_before = _attn_masks()

        def step(jb, carry):
            dq, cl, cg = carry
            ks = pl.multiple_of(jb * BK, BK)
            kb = k_ref[pl.ds(ks, BK), :]
            vb = v_ref[pl.ds(ks, BK), :]
            m = diff > jb * BK - i * BQ
            z = lax.dot_general(qs, kb, NT_DIMS, preferred_element_type=f32)
            lsp, lsn = _log_sigmoids(z)
            lm = jnp.where(m, lsn, 0.0)
            a = jnp.where(m, jnp.exp(lsp + _split_dot(lm, u_after)), 0.0)
            g = a * lax.dot_general(dos, vb, NT_DIMS, preferred_element_type=f32)
            beta = jnp.exp(lsp)
            aa = jnp.where(m, g * jnp.exp(lsn) - _split_dot(g, u_before) * beta, 0.0)
            bb = jnp.where(m, beta, 0.0)
            cl = cl + jnp.sum(lm, axis=-1, keepdims=True)
            f = jnp.exp(tots - cl)
            dz = (f * aa - cg * bb).astype(bf16)
            cg = cg + f * jnp.sum(g, axis=-1, keepdims=True)
            dq = dq + jnp.dot(dz, kb, preferred_element_type=f32)
            dk_ref[pl.ds(ks, BK), :] += lax.dot_general(dz, qs, TN_DIMS, preferred_element_type=f32)
            dv_ref[pl.ds(ks, BK), :] += lax.dot_general((f * a).astype(bf16), dos, TN_DIMS, preferred_element_type=f32)
            return dq, cl, cg

        zc = jnp.zeros((2 * BQ, 1), f32)
        dq, _, _ = lax.fori_loop(0, last_block(i) + 1, step, (jnp.zeros((2 * BQ, 128), f32), zc, zc))
        dq_ref[...] = _unstack_heads(dq)
        if n_s:
            @pl.when(jnp.logical_and(p == 3, i == nq - 1))
            def _():
                for cp in copies():
                    cp.wait()

    blk = pl.BlockSpec((BQ, 128), lambda p, i: (i, p))
    full = pl.BlockSpec((T, 128), lambda p, i: (0, p))
    shp = jax.ShapeDtypeStruct((T, 512), f32)
    s_list = [] if scatter is None else list(scatter[0])
    bufs = chip_buffers(s_list, modes) if n_s else []
    res = pl.pallas_call(
        body, name="attn_bwd_scatter" if n_s else "attn_bwd", grid=(4, nq),
        in_specs=[blk, full, full, blk, blk] + [_ANY] * (2 * n_s), out_specs=[blk, full, full] + [_ANY] * n_s,
        out_shape=[shp, shp, shp] + [jax.ShapeDtypeStruct(b.shape, b.dtype) for b in bufs],
        input_output_aliases={5 + n_s + k: 3 + k for k in range(n_s)},
        scratch_shapes=_sems(n_s, 3) if n_s else [], compiler_params=_cparams(2),
    )(q, k, v, do, tot, *s_list, *bufs)
    return res[0], res[1], res[2], list(res[3:])


def loss_head(y, target, T, tm=256):
    D = y.shape[1]

    def body(y_ref, t_ref, dy_ref, l_ref):
        i = pl.program_id(0)
        err = y_ref[...] - t_ref[...]
        dy_ref[...] = err * (1.0 / D)
        part = 0.5 * jnp.sum(jnp.sum(err * err, axis=-1, keepdims=True) * (1.0 / D), axis=0, keepdims=True)

        @pl.when(i == 0)
        def _():
            l_ref[...] = jnp.zeros_like(l_ref)

        l_ref[...] += jnp.broadcast_to(part, l_ref.shape)

    spec = pl.BlockSpec((tm, D), lambda i: (i, 0))
    return pl.pallas_call(
        body, name="loss_head", grid=(T // tm,), in_specs=[spec, spec],
        out_specs=[spec, pl.BlockSpec((1, 128), lambda i: (0, 0))],
        out_shape=[jax.ShapeDtypeStruct((T, D), f32), jax.ShapeDtypeStruct((1, 128), f32)],
        compiler_params=_cparams(1),
    )(y, target)


def _row_tile(rows, cols, offset=0, max_elems=128 * 1024):
    best = None
    for t in range(16, rows + 1, 16):
        if rows % t == 0 and offset % t == 0 and t * cols <= max_elems:
            best = t
    return best if best is not None else rows


def adamw(name, w, m, v, parts, layer, prev=None, row_off=0):
    L, R, C = w.shape
    n_parts = parts.shape[0]
    tr = _row_tile(R, C, row_off, max_elems=256 * 1024)
    assert R % tr == 0 and row_off % tr == 0, (name, R, row_off, tr)

    def body(w_ref, m_ref, v_ref, p_ref, *rest):
        g_ref, d_ref, nm_ref, nv_ref = rest[-4:]
        g = p_ref[0].astype(f32)
        for s in range(1, n_parts):
            g = g + p_ref[s].astype(f32)
        wv = w_ref[...]
        mn = ADAM_B1 * m_ref[...] + (1.0 - ADAM_B1) * g
        vn = ADAM_B2 * v_ref[...] + (1.0 - ADAM_B2) * jnp.square(g)
        m_hat = mn / (1.0 - ADAM_B1 ** ADAM_STEP)
        v_hat = vn / (1.0 - ADAM_B2 ** ADAM_STEP)
        g_ref[...] = g
        d_ref[...] = -ADAM_LR * (m_hat / (jnp.sqrt(v_hat) + ADAM_EPS) + ADAM_WD * wv)
        nm_ref[...] = mn
        nv_ref[...] = vn

    spec = pl.BlockSpec((None, tr, C), lambda i: (layer, i, 0))
    shp = jax.ShapeDtypeStruct((L, R, C), f32)
    n_prev = 0 if prev is None else 4
    return pl.pallas_call(
        body, name=name, grid=(R // tr,),
        in_specs=[spec, spec, spec, pl.BlockSpec((n_parts, tr, C), lambda i: (0, row_off // tr + i, 0))] + [_ANY] * n_prev,
        out_specs=[spec] * 4, out_shape=[shp] * 4, input_output_aliases={4 + k: k for k in range(n_prev)},
        compiler_params=_cparams(1),
    )(w, m, v, parts, *(prev or ()))


REPL = ["ln1_g", "sg_ln_g", "sg_ln_b", "sg_w", "sg_b", "cv_b", "cv_ln_g", "cv_ln_b", "q_norm_g", "k_norm_g", "ln2_g",
        "ffn_conv_b"]
SMALL_SHARDED = ["b_gate", "cv_w", "ffn_conv_w"]
BIG = ["w_in", "w_a_out", "w_b_out", "w_c_out", "w_up", "w_out", "w_down"]
COL_SHARDED = ["w_a_out", "w_b_out", "w_c_out"]
TRANSPOSED = ["w_in", "w_up"]


def _pack(arrs, rows):
    flat = jnp.concatenate([a.reshape(-1) for a in arrs])
    return jnp.pad(flat, (0, rows * 128 - flat.shape[0])).reshape(rows, 128)


def _pack_layers(arrs, rows):
    flat = jnp.concatenate([a.reshape(a.shape[0], -1) for a in arrs], axis=1)
    return jnp.pad(flat, ((0, 0), (0, rows * 128 - flat.shape[1]))).reshape(flat.shape[0], rows, 128)


def _unpack_layers(packed, shapes):
    flat = packed.reshape(packed.shape[0], -1)
    out, pos = [], 0
    for s in shapes:
        n = 1
        for d in s[1:]:
            n *= d
        out.append(flat[:, pos:pos + n].reshape(s))
        pos += n
    return out


def _rows_for(shapes, mult):
    n = 0
    for s in shapes:
        k = 1
        for d in s:
            k *= d
        n += k
    rows = -(-n // 128)
    return -(-rows // mult) * mult


W_NAMES = ['ln1_g', 'w_in', 'b_gate', 'sg_ln_g', 'sg_ln_b', 'sg_w', 'sg_b', 'w_a_out', 'cv_w', 'cv_b', 'cv_ln_g', 'cv_ln_b',
           'w_b_out', 'q_norm_g', 'k_norm_g', 'w_c_out', 'w_out', 'ln2_g', 'w_up', 'ffn_conv_w', 'ffn_conv_b', 'w_down']


def _forward_layer(x, P, T, late=None):
    D = x.shape[1]
    sv = {"x0": x}
    (h1,) = rowwise("rms_fwd", f_rms, [(x, D, 0)], [(P["ln1_g"], (1, D), None)], [(D, D, 0, bf16)], T, 256)
    z = mm_nt("in_proj", h1, P["w_in_t"], bf16, tko=1664, tc=D)
    sv["h1"], sv["z"] = h1, z
    (ya_in,) = rowwise("sgu_fwd", f_sgu, [(z, 512, 0), (z, 512, 1)],
                       [(P["sg_ln_g"], (1, 512), None), (P["sg_ln_b"], (1, 512), None),
                        (P["sg_w"], (8, CHUNK, CHUNK), None), (P["sg_b_t"], (CHUNK, 8), None)],
                       [(512, 512, 0, bf16)], T, CHUNK)
    (c0,) = rowwise("glu_fwd", f_glu, [(z, 512, 2), (z, 512, 3)], [], [(512, 512, 0, f32)], T, 256)
    c1 = conv31_fwd(c0, P["cv_w"], P["cv_b"], T)
    (c3,) = rowwise("lnsilu_fwd", f_lnsilu, [(c1, 512, 0)], [(P["cv_ln_g"], (1, 512), None), (P["cv_ln_b"], (1, 512), None)],
                    [(512, 512, 0, bf16)], T, 256)
    q8, kn, vb = rowwise("qkv_fwd", f_qkv, [(z, 512, 4), (z, 512, 5), (z, 512, 6)],
                         [(P["q_norm_g"], (1, HEAD_DIM), None), (P["k_norm_g"], (1, HEAD_DIM), None)],
                         [(512, 512, 0, bf16)] * 3, T, 256)
    o, tot, _ = attn_fwd(q8, kn, vb, T)
    if late is not None:
        P.update(late(o))
    ya = mm_nn("a_out", ya_in, P["w_a_out"], f32)
    yb = mm_nn("b_out", c3, P["w_b_out"], f32)
    yc = mm_nn("c_out", o, P["w_c_out"], f32)
    (merged,) = rowwise("merge_fwd", f_merge,
                        [(z, 512, lambda c: 7 + c), (z, 512, lambda c: 9 + c), (z, 512, lambda c: 11 + c),
                         (ya, 512, lambda c: c), (yb, 512, lambda c: c), (yc, 512, lambda c: c)],
                        [(P["b_gate"][k], (1, 512), lambda c: c) for k in range(3)],
                        [(D, 512, lambda c: c, bf16)], T, 256, ncol=2)
    x1 = mm_nn("out_proj", merged, P["w_out"], f32, add=x)
    sv.update(ya_in=ya_in, ya=ya, c0=c0, c1=c1, c3=c3, yb=yb, q8=q8, kn=kn, vb=vb, o=o, tot=tot, yc=yc, merged=merged, x1=x1)
    (h2,) = rowwise("rms_fwd", f_rms, [(x1, D, 0)], [(P["ln2_g"], (1, D), None)], [(D, D, 0, bf16)], T, 256)
    up = mm_nt("up_proj", h2, P["w_up_t"], bf16, tko=1408, tc=D)
    act = ffn_act_fwd(up, P["ffn_conv_w"], P["ffn_conv_b"], T)
    x2 = mm_nn("down_proj", act, P["w_down"], f32, add=x1)
    sv.update(h2=h2, up=up, act=act)
    return x2, sv


def _backward_layer(dx2, P, sv, T, scatter=None):
    D = dx2.shape[1]
    G = {}
    G["w_down"] = mm_tn("dw_down", sv["act"], dx2, tk=1408)
    dact = mm_nt("d_act", dx2, P["w_down"], f32, tko=1408, tc=D)
    dup, dcw0, dcw1, dcw2, G["ffn_conv_b"] = ffn_act_bwd(sv["up"], P["ffn_conv_w"], P["ffn_conv_b"], dact, T)
    G["ffn_conv_w"] = jnp.concatenate([dcw0, dcw1, dcw2], axis=0)
    G["w_up"] = mm_tn("dw_up", dup, sv["h2"], tk=1408)
    dh2 = mm_nn("d_h2", dup, P["w_up_t"], f32, tm=512)
    dx1, G["ln2_g"] = rowwise_bwd("rms_bwd", f_rms, [(sv["x1"], D, 0)], [(P["ln2_g"], (1, D), None)], [(dh2, D, 0)],
                                  [f32], T, 256, adds=[(dx2, D, 0)])
    G["w_out"] = mm_tn("dw_out", sv["merged"], dx1, tk=1024)
    dmerged = mm_nt("d_merged", dx1, P["w_out"], f32)
    z = sv["z"]
    dg0, dg1, dg2, dya, dyb, dyc, db0, db1, db2 = rowwise_bwd(
        "merge_bwd", f_merge,
        [(z, 512, lambda c: 7 + c), (z, 512, lambda c: 9 + c), (z, 512, lambda c: 11 + c),
         (sv["ya"], 512, lambda c: c), (sv["yb"], 512, lambda c: c), (sv["yc"], 512, lambda c: c)],
        [(P["b_gate"][k], (1, 512), lambda c: c) for k in range(3)],
        [(dmerged, 512, lambda c: c)], [bf16] * 6, T, 256, ncol=2)
    G["b_gate"] = jnp.concatenate([db0, db1, db2], axis=0)
    G["w_c_out"] = mm_tn("dw_c_out", sv["o"], dyc, n_slab=CHUNK)
    do = mm_nt("d_o", dyc, P["w_c_out"], f32)
    dq8, dkn, dvb, received = attn_bwd(sv["q8"], sv["kn"], sv["vb"], do, sv["tot"], T, scatter=scatter)
    dzq, dzk, dzv, G["q_norm_g"], G["k_norm_g"] = rowwise_bwd(
        "qkv_bwd", f_qkv, [(z, 512, 4), (z, 512, 5), (z, 512, 6)],
        [(P["q_norm_g"], (1, HEAD_DIM), None), (P["k_norm_g"], (1, HEAD_DIM), None)],
        [(dq8, 512, 0), (dkn, 512, 0), (dvb, 512, 0)], [bf16] * 3, T, 256)
    G["w_b_out"] = mm_tn("dw_b_out", sv["c3"], dyb, n_slab=CHUNK)
    dc3 = mm_nt("d_c3", dyb, P["w_b_out"], f32)
    dc1, G["cv_ln_g"], G["cv_ln_b"] = rowwise_bwd(
        "lnsilu_bwd", f_lnsilu, [(sv["c1"], 512, 0)], [(P["cv_ln_g"], (1, 512), None), (P["cv_ln_b"], (1, 512), None)],
        [(dc3, 512, 0)], [f32], T, 256)
    dc0, G["cv_w"], G["cv_b"] = conv31_bwd(sv["c0"], P["cv_w"], dc1, T)
    dzp, dzgl = rowwise_bwd("glu_bwd", f_glu, [(z, 512, 2), (z, 512, 3)], [], [(dc0, 512, 0)], [bf16] * 2, T, 256)
    G["w_a_out"] = mm_tn("dw_a_out", sv["ya_in"], dya, n_slab=CHUNK)
    dya_in = mm_nt("d_ya_in", dya, P["w_a_out"], f32)
    dzu, dzv_a, G["sg_ln_g"], G["sg_ln_b"], G["sg_w"], dsbt = rowwise_bwd(
        "sgu_bwd", f_sgu, [(z, 512, 0), (z, 512, 1)],
        [(P["sg_ln_g"], (1, 512), None), (P["sg_ln_b"], (1, 512), None), (P["sg_w"], (8, CHUNK, CHUNK), None),
         (P["sg_b_t"], (CHUNK, 8), None)],
        [(dya_in, 512, 0)], [bf16] * 2, T, CHUNK)
    G["sg_b"] = dsbt.T
    dz = concat_cols("dz_concat", [dzu, dzv_a, dzp, dzgl, dzq, dzk, dzv, dg0, dg1, dg2], T)
    G["w_in"] = mm_tn("dw_in", dz, sv["h1"], tk=1664)
    dh1 = mm_nn("d_h1", dz, P["w_in_t"], f32, tm=512)
    dx0, G["ln1_g"] = rowwise_bwd("rms_bwd", f_rms, [(sv["x0"], D, 0)], [(P["ln1_g"], (1, D), None)], [(dh1, D, 0)],
                                  [f32], T, 256, adds=[(dx1, D, 0)])
    return dx0, G, received


def kernel(x, ln1_g, w_in, b_gate, sg_ln_g, sg_ln_b, sg_w, sg_b, w_a_out, cv_w, cv_b, cv_ln_g, cv_ln_b, w_b_out, q_norm_g, k_norm_g, w_c_out, w_out, ln2_g, w_up, ffn_conv_w, ffn_conv_b, w_down, loss_target, m_ln1_g, m_w_in, m_b_gate, m_sg_ln_g, m_sg_ln_b, m_sg_w, m_sg_b, m_w_a_out, m_cv_w, m_cv_b, m_cv_ln_g, m_cv_ln_b, m_w_b_out, m_q_norm_g, m_k_norm_g, m_w_c_out, m_w_out, m_ln2_g, m_w_up, m_ffn_conv_w, m_ffn_conv_b, m_w_down, v_ln1_g, v_w_in, v_b_gate, v_sg_ln_g, v_sg_ln_b, v_sg_w, v_sg_b, v_w_a_out, v_cv_w, v_cv_b, v_cv_ln_g, v_cv_ln_b, v_w_b_out, v_q_norm_g, v_k_norm_g, v_w_c_out, v_w_out, v_ln2_g, v_w_up, v_ffn_conv_w, v_ffn_conv_b, v_w_down):
    W = dict(ln1_g=ln1_g, w_in=w_in, b_gate=b_gate, sg_ln_g=sg_ln_g, sg_ln_b=sg_ln_b, sg_w=sg_w, sg_b=sg_b, w_a_out=w_a_out,
             cv_w=cv_w, cv_b=cv_b, cv_ln_g=cv_ln_g, cv_ln_b=cv_ln_b, w_b_out=w_b_out, q_norm_g=q_norm_g, k_norm_g=k_norm_g,
             w_c_out=w_c_out, w_out=w_out, ln2_g=ln2_g, w_up=w_up, ffn_conv_w=ffn_conv_w, ffn_conv_b=ffn_conv_b, w_down=w_down)
    M = dict(ln1_g=m_ln1_g, w_in=m_w_in, b_gate=m_b_gate, sg_ln_g=m_sg_ln_g, sg_ln_b=m_sg_ln_b, sg_w=m_sg_w, sg_b=m_sg_b,
             w_a_out=m_w_a_out, cv_w=m_cv_w, cv_b=m_cv_b, cv_ln_g=m_cv_ln_g, cv_ln_b=m_cv_ln_b, w_b_out=m_w_b_out,
             q_norm_g=m_q_norm_g, k_norm_g=m_k_norm_g, w_c_out=m_w_c_out, w_out=m_w_out, ln2_g=m_ln2_g, w_up=m_w_up,
             ffn_conv_w=m_ffn_conv_w, ffn_conv_b=m_ffn_conv_b, w_down=m_w_down)
    V = dict(ln1_g=v_ln1_g, w_in=v_w_in, b_gate=v_b_gate, sg_ln_g=v_sg_ln_g, sg_ln_b=v_sg_ln_b, sg_w=v_sg_w, sg_b=v_sg_b,
             w_a_out=v_w_a_out, cv_w=v_cv_w, cv_b=v_cv_b, cv_ln_g=v_cv_ln_g, cv_ln_b=v_cv_ln_b, w_b_out=v_w_b_out,
             q_norm_g=v_q_norm_g, k_norm_g=v_k_norm_g, w_c_out=v_w_c_out, w_out=v_w_out, ln2_g=v_ln2_g, w_up=v_w_up,
             ffn_conv_w=v_ffn_conv_w, ffn_conv_b=v_ffn_conv_b, w_down=v_w_down)
    T, D = x.shape[1], x.shape[2]
    L = DEPTH
    xs = x.reshape(T, D)
    target = loss_target.reshape(T, D)

    ss_shapes = [W[n].shape for n in SMALL_SHARDED]
    ss_rows = _rows_for(ss_shapes, 8)
    (ss_all,) = exchange("gather_small", [_pack([W[n] for n in SMALL_SHARDED], ss_rows)], ["gather"])
    full_small = {}
    pos = 0
    for n in SMALL_SHARDED:
        s = W[n].shape
        cnt = s[0] * s[1] * s[2]
        part = ss_all.reshape(N_DEV, -1)[:, pos:pos + cnt].reshape((N_DEV,) + s)
        full_small[n] = jnp.transpose(part, (1, 2, 0, 3)).reshape(s[0], s[1], N_DEV * s[2])
        pos += cnt

    params, saved = [], []
    cur = xs
    Wt = {n: (jnp.transpose(W[n], (0, 2, 1)) if n in TRANSPOSED else W[n]) for n in BIG}

    def local_slabs(l):
        return [Wt[n][l].astype(bf16) for n in BIG]

    def weights_of(names, slabs):
        out = {}
        for n, s in zip(names, slabs):
            if n in TRANSPOSED:
                out[n + "_t"] = s.reshape(-1, s.shape[-1])
            else:
                out[n] = assemble("assemble_" + n, s) if n in COL_SHARDED else s.reshape(-1, s.shape[-1])
        return out

    LATE = [n for n in BIG if n != "w_in"]
    first = gather2("gather_w_in", [local_slabs(0)[0]])
    late0 = gather_start("gather_start_0", lax.optimization_barrier((first, local_slabs(0)[1:]))[1])
    in_flight, prev_token = {}, late0[3]
    for l in range(1, L):
        in_flight[l] = gather_start("gather_start_%d" % l, lax.optimization_barrier((prev_token, local_slabs(l)))[1])
        prev_token = in_flight[l][3]
    start_token = prev_token[0, 0]
    for l in range(L):
        if l == 0:
            P = weights_of(["w_in"], first)
            late = lambda o: weights_of(LATE, gather_wait("gather_wait_0", late0[0], late0[1], late0[2], o))
        else:
            send_sems, recv_sems, bufs, _ = in_flight[l]
            P = weights_of(BIG, gather_wait("gather_wait_%d" % l, send_sems, recv_sems, bufs, cur))
            late = None
        for n in REPL:
            P[n] = W[n][l]
        for n in ("ln1_g", "sg_ln_g", "sg_ln_b", "cv_b", "cv_ln_g", "cv_ln_b", "q_norm_g", "k_norm_g", "ln2_g", "ffn_conv_b"):
            P[n] = P[n].reshape(1, -1)
        P["sg_b_t"] = P["sg_b"].T
        P["cv_w"] = full_small["cv_w"][l]
        P["b_gate"] = [full_small["b_gate"][l][k:k + 1] for k in range(3)]
        P["ffn_conv_w"] = [full_small["ffn_conv_w"][l][k:k + 1] for k in range(3)]
        if l == 0:
            P["ln1_g"] = P["ln1_g"] + start_token
        cur, sv = _forward_layer(cur, P, T, late=late)
        params.append(P)
        saved.append(sv)

    dy, loss_row = loss_head(cur, target, T)
    loss = lax.psum(loss_row[0, 0], ("x", "y", "c"))

    repl_shapes = [W[n].shape for n in REPL]
    repl_rows = _rows_for([s[1:] for s in repl_shapes], 16)
    ssl_rows = _rows_for([s[1:] for s in ss_shapes], 16)
    state = {"repl": [_pack_layers([X[n] for n in REPL], repl_rows) for X in (W, M, V)],
             "ss": [_pack_layers([X[n] for n in SMALL_SHARDED], ssl_rows) for X in (W, M, V)]}
    big_rows, big_off = {}, {}
    for n in BIG:
        big_off[n] = sum(big_rows.values())
        big_rows[n] = W[n].shape[1] * W[n].shape[2] // D
        state[n] = [(jnp.transpose(X[n], (0, 2, 1)) if n in TRANSPOSED else X[n]).reshape(L, big_rows[n], D)
                    for X in (W, M, V)]
    done = {n: None for n in state}

    def update(layer, received):
        for n in BIG:
            done[n] = adamw("adamw_" + n, *state[n], received[0], layer, done[n], row_off=big_off[n])
        done["repl"] = adamw("adamw_repl", *state["repl"], received[1], layer, done["repl"])
        done["ss"] = adamw("adamw_ss", *state["ss"], received[2], layer, done["ss"])

    def reduce_in_chip(G):
        big = [G[n].reshape(4, 2, big_rows[n], D) for n in BIG]
        ss_parts = []
        for n in SMALL_SHARDED:
            k, c = W[n].shape[1:]
            ss_parts.append(jnp.transpose(G[n].reshape(k, N_DEV, c), (1, 0, 2)).reshape(N_DEV, k * c))
        ss_send = jnp.concatenate(ss_parts, axis=1)
        ss_send = jnp.pad(ss_send, ((0, 0), (0, ssl_rows * 128 - ss_send.shape[1]))).reshape(4, 2, ssl_rows, 128)
        repl = _pack([G[n] for n in REPL], repl_rows)
        core = lax.axis_index("c")
        kept = jnp.concatenate([lax.dynamic_index_in_dim(b, core, 1, keepdims=False) for b in big], axis=1)
        ss_kept = lax.dynamic_index_in_dim(ss_send, core, 1, keepdims=False)
        got, repl_got, ss_got = pair_exchange("pair_exchange", big, [repl, ss_send], ["gather", "scatter"])
        return [add2("pair_add_big", kept, got), add2("pair_add_repl", repl, repl_got), add2("pair_add_ss", ss_kept, ss_got)]

    chip_modes = ["scatter", "gather", "scatter"]
    dcur = dy
    send = None
    for l in reversed(range(L)):
        dcur, G, recv = _backward_layer(dcur, params[l], saved[l], T, scatter=send)
        if send is not None:
            update(l + 1, recv)
        send = (reduce_in_chip(G), chip_modes)
    update(0, chip_exchange("chip_exchange_last", send[0], send[1]))

    results = [{}, {}, {}, {}]
    for k in range(4):
        for n, a in zip(REPL, _unpack_layers(done["repl"][k], repl_shapes)):
            results[k][n] = a
        for n, a in zip(SMALL_SHARDED, _unpack_layers(done["ss"][k], ss_shapes)):
            results[k][n] = a
        for n in BIG:
            results[k][n] = jnp.transpose(done[n][k], (0, 2, 1)) if n in TRANSPOSED else done[n][k].reshape(W[n].shape)
    out = [loss, dcur.reshape(1, T, D)]
    for k in range(4):
        out += [results[k][n] for n in W_NAMES]
    return tuple(out)
```

```python
import functools

import jax
import jax.numpy as jnp
from jax import lax
from jax.experimental import pallas as pl
from jax.experimental.pallas import tpu as pltpu

f32 = jnp.float32
bf16 = jnp.bfloat16

EPS = 1e-6
N_DEV = 8
DEPTH = 4
CHUNK = 128
HEAD_DIM = 64
N_HEADS = 8
CV_KERNEL = 31
VMEM_LIMIT_BYTES = 56 * 2 ** 20

ADAM_LR = 0.001
ADAM_B1 = 0.9
ADAM_B2 = 0.999
ADAM_EPS = 1e-08
ADAM_WD = 0.01
ADAM_STEP = 10

MESH = pl.DeviceIdType.MESH


def _cparams(n_grid):
    return pltpu.CompilerParams(dimension_semantics=("arbitrary",) * n_grid, vmem_limit_bytes=VMEM_LIMIT_BYTES)


def exchange(name, arrays, modes):
    n = len(arrays)

    def body(*refs):
        copies = _direct_copies(refs[:n], refs[n:2 * n], modes, *refs[2 * n:])
        for cp in copies:
            cp.start()
        for cp in copies:
            cp.wait()

    return pl.pallas_call(
        body, name=name, out_shape=_exchange_out_shapes(arrays, modes),
        in_specs=[_ANY] * n, out_specs=[_ANY] * n, scratch_shapes=_exchange_sems(n),
    )(*arrays)


_ANY = pl.BlockSpec(memory_space=pl.ANY)


def _exchange_out_shapes(arrays, modes):
    return [jax.ShapeDtypeStruct((N_DEV,) + tuple(a.shape) if m == "gather" else tuple(a.shape), a.dtype)
            for a, m in zip(arrays, modes)]


def _exchange_sems(n):
    return [pltpu.SemaphoreType.DMA((n, N_DEV - 1)), pltpu.SemaphoreType.DMA((n, N_DEV - 1)), pltpu.SemaphoreType.DMA((n,))]


def _direct_copies(ins, outs, modes, send_sems, recv_sems, local_sems):
    x, y, c = lax.axis_index("x"), lax.axis_index("y"), lax.axis_index("c")
    me = 4 * x + 2 * y + c
    copies = []
    for k in range(len(ins)):
        src_mine = ins[k] if modes[k] == "gather" else ins[k].at[me]
        copies.append(pltpu.make_async_copy(src_mine, outs[k].at[me], local_sems.at[k]))
        for r in range(1, N_DEV):
            px = 1 - x if r & 4 else x
            py = 1 - y if r & 2 else y
            pc = 1 - c if r & 1 else c
            src = ins[k] if modes[k] == "gather" else ins[k].at[4 * px + 2 * py + pc]
            copies.append(pltpu.make_async_remote_copy(
                src_ref=src, dst_ref=outs[k].at[me], send_sem=send_sems.at[k, r - 1], recv_sem=recv_sems.at[k, r - 1],
                device_id=(px, py, pc), device_id_type=MESH))
    return copies


def own_slot_buffer(block, n_slots, index):
    return lax.dynamic_update_slice(lax.empty((n_slots,) + tuple(block.shape), block.dtype), block[None],
                                    (index,) + (0,) * block.ndim)


def _sems(n, m):
    return [pltpu.SemaphoreType.DMA((n, m)), pltpu.SemaphoreType.DMA((n, m))]


def _two_level_gather(bufs, send_sems, recv_sems):
    x, y, c = lax.axis_index("x"), lax.axis_index("y"), lax.axis_index("c")
    me = 4 * x + 2 * y + c
    sibling = (x, y, 1 - c)
    chips = [(1 - x, y), (x, 1 - y), (1 - x, 1 - y)]

    def slot(px, py, pc):
        return 4 * px + 2 * py + pc

    def copy(k, sem, block, to):
        rows = bufs[k].at[block]
        return pltpu.make_async_remote_copy(src_ref=rows, dst_ref=rows, send_sem=send_sems.at[k, sem],
                                            recv_sem=recv_sems.at[k, sem], device_id=to, device_id_type=MESH)

    def first(k):
        return [copy(k, 0, me, sibling)] + [copy(k, 1 + j, me, (*chip, c)) for j, chip in enumerate(chips)]

    def passed_on(k, j):
        return copy(k, 4 + j, slot(*chips[j], c), sibling)

    def start():
        for k in range(len(bufs)):
            for cp in first(k):
                cp.start()

    def forward():
        for k in range(len(bufs)):
            for j in range(3):
                copy(k, 1 + j, slot(*chips[j], c), sibling).wait_recv()
                passed_on(k, j).start()

    def finish():
        for k in range(len(bufs)):
            copy(k, 0, slot(x, y, 1 - c), sibling).wait_recv()
            for j, chip in enumerate(chips):
                copy(k, 4 + j, slot(*chip, 1 - c), sibling).wait_recv()
            for cp in first(k) + [passed_on(k, j) for j in range(3)]:
                cp.wait_send()

    return start, forward, finish


def pair_exchange(name, big, smalls, modes):
    nb, ns = len(big), len(smalls)
    rows = [a.shape[2] for a in big]
    total, width = sum(rows), big[0].shape[3]
    n = nb + ns

    def body(*refs):
        ins, outs = refs[:n], refs[n:n + 1 + ns]
        send_sems, recv_sems = refs[-2:]
        x, y, c = lax.axis_index("x"), lax.axis_index("y"), lax.axis_index("c")
        copies = []

        def remote(k, src, dst):
            copies.append(pltpu.make_async_remote_copy(src_ref=src, dst_ref=dst, send_sem=send_sems.at[k, 0],
                                                       recv_sem=recv_sems.at[k, 0], device_id=(x, y, 1 - c),
                                                       device_id_type=MESH))

        off = 0
        for k in range(nb):
            remote(k, ins[k].at[:, 1 - c], outs[0].at[:, pl.ds(off, rows[k]), :])
            off += rows[k]
        for j in range(ns):
            remote(nb + j, ins[nb + j].at[:, 1 - c] if modes[j] == "scatter" else ins[nb + j], outs[1 + j])
        for cp in copies:
            cp.start()
        for cp in copies:
            cp.wait()

    out_shape = [jax.ShapeDtypeStruct((4, total, width), big[0].dtype)]
    for a, m in zip(smalls, modes):
        out_shape.append(jax.ShapeDtypeStruct((4,) + tuple(a.shape[2:]) if m == "scatter" else tuple(a.shape), a.dtype))
    return pl.pallas_call(
        body, name=name, out_shape=out_shape, in_specs=[_ANY] * n, out_specs=[_ANY] * len(out_shape),
        scratch_shapes=_sems(n, 1),
    )(*big, *smalls)


def add2(name, a, b):
    shape = a.shape
    a2, b2 = a.reshape(-1, shape[-1]), b.reshape(-1, shape[-1])
    R, C = a2.shape
    tr = _row_tile(R, C, max_elems=1200 * 1024)

    def body(a_ref, b_ref, o_ref):
        o_ref[...] = (a_ref[...].astype(f32) + b_ref[...].astype(f32)).astype(o_ref.dtype)

    spec = pl.BlockSpec((tr, C), lambda i: (i, 0))
    return pl.pallas_call(
        body, name=name, grid=(R // tr,), in_specs=[spec, spec], out_specs=spec,
        out_shape=jax.ShapeDtypeStruct((R, C), a.dtype), compiler_params=_cparams(1),
    )(a2, b2).reshape(shape)


def chip_buffers(arrays, modes):
    mine = 2 * lax.axis_index("x") + lax.axis_index("y")
    return [own_slot_buffer(lax.dynamic_index_in_dim(a, mine, 0, keepdims=False) if m == "scatter" else a, 4, mine)
            for a, m in zip(arrays, modes)]


def _chip_copies(ins, bufs, modes, send_sems, recv_sems):
    x, y, c = lax.axis_index("x"), lax.axis_index("y"), lax.axis_index("c")
    mine = 2 * x + y
    copies = []
    for k in range(len(ins)):
        for r in range(1, 4):
            px = 1 - x if r & 2 else x
            py = 1 - y if r & 1 else y
            src = ins[k].at[2 * px + py] if modes[k] == "scatter" else ins[k]
            copies.append(pltpu.make_async_remote_copy(
                src_ref=src, dst_ref=bufs[k].at[mine], send_sem=send_sems.at[k, r - 1], recv_sem=recv_sems.at[k, r - 1],
                device_id=(px, py, c), device_id_type=MESH))
    return copies


def chip_exchange(name, arrays, modes):
    n = len(arrays)

    def body(*refs):
        copies = _chip_copies(refs[:n], refs[2 * n:3 * n], modes, *refs[3 * n:])
        for cp in copies:
            cp.start()
        for cp in copies:
            cp.wait()

    bufs = chip_buffers(arrays, modes)
    return pl.pallas_call(
        body, name=name, out_shape=[jax.ShapeDtypeStruct(b.shape, b.dtype) for b in bufs],
        in_specs=[_ANY] * (2 * n), out_specs=[_ANY] * n, input_output_aliases={n + k: k for k in range(n)},
        scratch_shapes=_sems(n, 3),
    )(*arrays, *bufs)


def gather_buffers(blocks):
    me = 4 * lax.axis_index("x") + 2 * lax.axis_index("y") + lax.axis_index("c")
    return [own_slot_buffer(b, N_DEV, me) for b in blocks]


_HBM = pl.BlockSpec(memory_space=pltpu.HBM)
_SEM = pl.BlockSpec(memory_space=pltpu.SEMAPHORE)
_SPLIT_PARAMS = pltpu.CompilerParams(has_side_effects=pltpu.SideEffectType.DATAFLOW_SIDE_EFFECTING)


def _split_gather_copies(bufs, send_sems, recv_sems):
    x, y, c = lax.axis_index("x"), lax.axis_index("y"), lax.axis_index("c")
    me = 4 * x + 2 * y + c
    copies = []
    for k in range(len(bufs)):
        rows = bufs[k].at[me]
        for r in range(1, N_DEV):
            peer = (1 - x if r & 4 else x, 1 - y if r & 2 else y, 1 - c if r & 1 else c)
            copies.append(pltpu.make_async_remote_copy(
                src_ref=rows, dst_ref=rows, send_sem=send_sems.at[(N_DEV - 1) * k + r - 1],
                recv_sem=recv_sems.at[(N_DEV - 1) * k + r - 1], device_id=peer, device_id_type=MESH))
    return copies


def gather_start(name, blocks):
    n = len(blocks)
    bufs = gather_buffers(blocks)

    def body(*refs):
        for cp in _split_gather_copies(refs[n + 2:2 * n + 2], refs[n], refs[n + 1]):
            cp.start()
        refs[2 * n + 2][...] = jnp.zeros_like(refs[2 * n + 2])

    n_sem = n * (N_DEV - 1)
    res = pl.pallas_call(
        body, name=name,
        out_shape=(pltpu.SemaphoreType.DMA((n_sem,)), pltpu.SemaphoreType.DMA((n_sem,)),
                   *[pltpu.HBM(b.shape, b.dtype) for b in bufs], jax.ShapeDtypeStruct((8, 128), f32)),
        in_specs=[_HBM] * n, out_specs=(_SEM, _SEM, *[_HBM] * n, pl.BlockSpec(memory_space=pltpu.VMEM)),
        input_output_aliases={k: 2 + k for k in range(n)}, compiler_params=_SPLIT_PARAMS,
    )(*[pltpu.with_memory_space_constraint(b, pltpu.HBM) for b in bufs])
    return res[0], res[1], list(res[2:2 + n]), res[2 + n]


def gather_wait(name, send_sems, recv_sems, bufs, after):
    n = len(bufs)

    def body(*refs):
        for cp in _split_gather_copies(refs[:n], refs[n], refs[n + 1]):
            cp.wait_send()
            cp.wait_recv()

    return list(pl.pallas_call(
        body, name=name, out_shape=tuple(pltpu.HBM(b.shape, b.dtype) for b in bufs),
        in_specs=[_HBM] * n + [_SEM, _SEM, _ANY], out_specs=tuple([_HBM] * n),
        input_output_aliases={k: k for k in range(n)}, compiler_params=_SPLIT_PARAMS,
    )(*bufs, send_sems, recv_sems, after))


def gather2(name, arrays):
    n = len(arrays)

    def body(*refs):
        start, forward, finish = _two_level_gather(refs[n:2 * n], *refs[2 * n + 1:])
        start()
        forward()
        finish()
        refs[2 * n][...] = jnp.zeros_like(refs[2 * n])

    bufs = gather_buffers(arrays)
    res = pl.pallas_call(
        body, name=name, out_shape=[jax.ShapeDtypeStruct(b.shape, b.dtype) for b in bufs] + [jax.ShapeDtypeStruct((8, 128), f32)],
        in_specs=[_ANY] * n, out_specs=[_ANY] * n + [pl.BlockSpec(memory_space=pltpu.VMEM)],
        input_output_aliases={k: k for k in range(n)}, scratch_shapes=_sems(n, N_DEV - 1),
    )(*bufs)
    return list(res[:n]), res[n]


def assemble(name, slabs):
    _, K, Ns = slabs.shape
    g = N_DEV if Ns % 128 == 0 else 2

    def body(w_ref, o_ref):
        o_ref[...] = jnp.concatenate([w_ref[s] for s in range(g)], axis=1)

    return pl.pallas_call(
        body, name=name, grid=(N_DEV // g,),
        in_specs=[pl.BlockSpec((g, K, Ns), lambda m: (m, 0, 0))],
        out_specs=pl.BlockSpec((K, g * Ns), lambda m: (0, m)),
        out_shape=jax.ShapeDtypeStruct((K, N_DEV * Ns), slabs.dtype),
        compiler_params=_cparams(1),
    )(slabs)


def mm_nn(name, a, b, out_dtype, add=None, tm=1024, tn=512, tk=None):
    M, K = a.shape
    N = b.shape[1]
    tm, tn = min(tm, M), min(tn, N)
    tk = K if tk is None else tk
    nk = K // tk
    has_add = add is not None

    def body(a_ref, b_ref, *rest):
        o_ref = rest[1] if has_add else rest[0]
        part = jnp.dot(a_ref[...].astype(bf16), b_ref[...].astype(bf16), preferred_element_type=f32)
        if nk == 1:
            o_ref[...] = (part + rest[0][...] if has_add else part).astype(o_ref.dtype)
            return
        acc_ref = rest[-1]
        k = pl.program_id(2)

        @pl.when(k == 0)
        def _():
            acc_ref[...] = part + rest[0][...] if has_add else part

        @pl.when(k > 0)
        def _():
            acc_ref[...] += part

        @pl.when(k == nk - 1)
        def _():
            o_ref[...] = acc_ref[...].astype(o_ref.dtype)

    in_specs = [pl.BlockSpec((tm, tk), lambda i, j, k: (i, k)), pl.BlockSpec((tk, tn), lambda i, j, k: (k, j))]
    ops = [a, b]
    if has_add:
        in_specs.append(pl.BlockSpec((tm, tn), lambda i, j, k: (i, j)))
        ops.append(add)
    return pl.pallas_call(
        body, name=name, grid=(M // tm, N // tn, nk), in_specs=in_specs,
        out_specs=pl.BlockSpec((tm, tn), lambda i, j, k: (i, j)),
        out_shape=jax.ShapeDtypeStruct((M, N), out_dtype),
        scratch_shapes=[pltpu.VMEM((tm, tn), f32)] if nk > 1 else [], compiler_params=_cparams(3),
    )(*ops)


def concat_cols(name, pieces, T, tm=512):
    widths = [p.shape[1] for p in pieces]
    total = sum(widths)
    tm = min(tm, T)

    def body(*refs):
        o_ref = refs[-1]
        off = 0
        for r, w in zip(refs[:-1], widths):
            o_ref[:, off:off + w] = r[...]
            off += w

    return pl.pallas_call(
        body, name=name, grid=(T // tm,),
        in_specs=[pl.BlockSpec((tm, w), lambda i: (i, 0)) for w in widths],
        out_specs=pl.BlockSpec((tm, total), lambda i: (i, 0)),
        out_shape=jax.ShapeDtypeStruct((T, total), pieces[0].dtype), compiler_params=_cparams(1),
    )(*pieces)


def mm_nt(name, a, b, out_dtype, tm=1024, tko=None, tc=None):
    M, C = a.shape
    Ko = b.shape[0]
    tm = min(tm, M)
    tc = C if tc is None else min(tc, C)
    tko = Ko if tko is None else tko
    nc = C // tc

    def body(a_ref, b_ref, o_ref, *scratch):
        part = lax.dot_general(a_ref[...].astype(bf16), b_ref[...].astype(bf16), (((1,), (1,)), ((), ())),
                               preferred_element_type=f32)
        if nc == 1:
            o_ref[...] = part.astype(o_ref.dtype)
            return
        acc_ref = scratch[0]
        c = pl.program_id(2)

        @pl.when(c == 0)
        def _():
            acc_ref[...] = part

        @pl.when(c > 0)
        def _():
            acc_ref[...] += part

        @pl.when(c == nc - 1)
        def _():
            o_ref[...] = acc_ref[...].astype(o_ref.dtype)

    return pl.pallas_call(
        body, name=name, grid=(M // tm, Ko // tko, nc),
        in_specs=[pl.BlockSpec((tm, tc), lambda i, j, c: (i, c)), pl.BlockSpec((tko, tc), lambda i, j, c: (j, c))],
        out_specs=pl.BlockSpec((tm, tko), lambda i, j, c: (i, j)),
        out_shape=jax.ShapeDtypeStruct((M, Ko), out_dtype),
        scratch_shapes=[pltpu.VMEM((tm, tko), f32)] if nc > 1 else [], compiler_params=_cparams(3),
    )(a, b)


def mm_tn(name, a, b, n_slab=None, tk=512, tn=512):
    T, K = a.shape
    N = b.shape[1]
    tk = min(tk, K)
    if n_slab is None:
        tn = min(tn, N)

        def body(a_ref, b_ref, o_ref):
            o_ref[...] = lax.dot_general(a_ref[...].astype(bf16), b_ref[...].astype(bf16), (((0,), (0,)), ((), ())),
                                         preferred_element_type=f32).astype(o_ref.dtype)

        return pl.pallas_call(
            body, name=name, grid=(N // tn, K // tk),
            in_specs=[pl.BlockSpec((T, tk), lambda j, i: (0, i)), pl.BlockSpec((T, tn), lambda j, i: (0, j))],
            out_specs=pl.BlockSpec((tk, tn), lambda j, i: (i, j)),
            out_shape=jax.ShapeDtypeStruct((K, N), bf16), compiler_params=_cparams(2),
        )(a, b)

    Ns = n_slab
    g = N_DEV if Ns % 128 == 0 else 2
    tn = g * Ns

    def body(a_ref, b_ref, o_ref):
        val = lax.dot_general(a_ref[...].astype(bf16), b_ref[...].astype(bf16), (((0,), (0,)), ((), ())),
                              preferred_element_type=f32)
        for s in range(g):
            o_ref[s] = val[:, s * Ns:(s + 1) * Ns].astype(o_ref.dtype)

    return pl.pallas_call(
        body, name=name, grid=(K // tk, N_DEV // g),
        in_specs=[pl.BlockSpec((T, tk), lambda i, j: (0, i)), pl.BlockSpec((T, tn), lambda i, j: (0, j))],
        out_specs=pl.BlockSpec((g, tk, Ns), lambda i, j: (j, i, 0)),
        out_shape=jax.ShapeDtypeStruct((N_DEV, K, Ns), bf16), compiler_params=_cparams(2),
    )(a, b)


def _tile_spec(w, tm, cb):
    if callable(cb):
        return pl.BlockSpec((tm, w), lambda c, i: (i, cb(c)))
    return pl.BlockSpec((tm, w), lambda c, i: (i, cb))


def _param_spec(block, cb):
    nd = len(block)
    if cb is None:
        return pl.BlockSpec(block, lambda c, i: (0,) * nd)
    return pl.BlockSpec(block, lambda c, i: (0,) * (nd - 1) + (cb(c),))


def rowwise(name, fn, tiled, params, outs, T, tm, ncol=1):
    n_in = len(tiled) + len(params)
    n_t = len(tiled)

    def body(*refs):
        ins = [r[...].astype(f32) for r in refs[:n_t]] + [r[...] for r in refs[n_t:n_in]]
        res = fn(*ins)
        for r, o in zip(refs[n_in:], res):
            r[...] = o.astype(r.dtype)

    return pl.pallas_call(
        body, name=name, grid=(ncol, T // tm),
        in_specs=[_tile_spec(w, tm, cb) for _, w, cb in tiled] + [_param_spec(blk, cb) for _, blk, cb in params],
        out_specs=[_tile_spec(w, tm, cb) for _, w, cb, _ in outs],
        out_shape=[jax.ShapeDtypeStruct((T, cols), dt) for cols, _, _, dt in outs],
        compiler_params=_cparams(2),
    )(*[a for a, _, _ in tiled], *[a for a, _, _ in params])


def rowwise_bwd(name, fn, tiled, params, cts, grads, T, tm, ncol=1, adds=None):
    n_t, n_p, n_c = len(tiled), len(params), len(cts)
    adds = adds or [None] * n_t
    add_list = [(k, a) for k, a in enumerate(adds) if a is not None]
    want = [k for k, g in enumerate(grads) if g is not None]
    n_a = len(add_list)

    def body(*refs):
        pos = 0
        t_refs = refs[pos:pos + n_t]; pos += n_t
        p_refs = refs[pos:pos + n_p]; pos += n_p
        c_refs = refs[pos:pos + n_c]; pos += n_c
        a_refs = refs[pos:pos + n_a]; pos += n_a
        g_refs = refs[pos:pos + len(want)]; pos += len(want)
        pg_refs = refs[pos:pos + n_p]
        primals = [r[...].astype(f32) for r in t_refs] + [r[...] for r in p_refs]
        _, vjp = jax.vjp(fn, *primals)
        g = vjp(tuple(r[...].astype(f32) for r in c_refs))
        add_of = {k: a_refs[n][...] for n, (k, _) in enumerate(add_list)}
        for n, k in enumerate(want):
            val = g[k]
            if k in add_of:
                val = val + add_of[k]
            g_refs[n][...] = val.astype(g_refs[n].dtype)
        i = pl.program_id(1)
        for k in range(n_p):
            @pl.when(i == 0)
            def _(k=k):
                pg_refs[k][...] = g[n_t + k]

            @pl.when(i > 0)
            def _(k=k):
                pg_refs[k][...] += g[n_t + k]

    in_specs = ([_tile_spec(w, tm, cb) for _, w, cb in tiled] + [_param_spec(blk, cb) for _, blk, cb in params]
                + [_tile_spec(w, tm, cb) for _, w, cb in cts] + [_tile_spec(w, tm, cb) for _, (_, w, cb) in add_list])
    ops = ([a for a, _, _ in tiled] + [a for a, _, _ in params] + [a for a, _, _ in cts]
           + [a for _, (a, _, _) in add_list])
    out_specs, out_shape = [], []
    for k in want:
        w = tiled[k][1]
        out_specs.append(_tile_spec(w, tm, lambda c: c))
        out_shape.append(jax.ShapeDtypeStruct((T, ncol * w), grads[k]))
    for a, blk, cb in params:
        out_specs.append(_param_spec(blk, cb))
        out_shape.append(jax.ShapeDtypeStruct(a.shape, f32))
    return pl.pallas_call(
        body, name=name, grid=(ncol, T // tm), in_specs=in_specs, out_specs=out_specs, out_shape=out_shape,
        compiler_params=_cparams(2),
    )(*ops)


@jax.custom_vjp
def _bdot(a, b):
    return jnp.dot(a.astype(bf16), b.astype(bf16), preferred_element_type=f32)


def _bdot_fwd(a, b):
    return _bdot(a, b), (a, b)


def _bdot_bwd(res, ct):
    a, b = res
    ctb = ct.astype(bf16)
    da = lax.dot_general(ctb, b.astype(bf16), (((1,), (1,)), ((), ())), preferred_element_type=f32)
    db = lax.dot_general(a.astype(bf16), ctb, (((0,), (0,)), ((), ())), preferred_element_type=f32)
    return da, db


_bdot.defvjp(_bdot_fwd, _bdot_bwd)


def _layer_norm(x, g, b):
    mu = jnp.mean(x, axis=-1, keepdims=True)
    xc = x - mu
    y = xc * lax.rsqrt(jnp.mean(xc * xc, axis=-1, keepdims=True) + EPS)
    return y * g + b


def f_rms(x, g):
    y = x * lax.rsqrt(jnp.mean(x * x, axis=-1, keepdims=True) + EPS)
    return (y * g,)


def f_sgu(zu, zv, ln_g, ln_b, wm, sgb_t):
    u = jax.nn.gelu(zu)
    vn = _layer_norm(jax.nn.gelu(zv), ln_g, ln_b)
    row = lax.broadcasted_iota(jnp.int32, (CHUNK, CHUNK), 0)
    col = lax.broadcasted_iota(jnp.int32, (CHUNK, CHUNK), 1)
    tril = col <= row
    low = col < HEAD_DIM
    parts = []
    for p in range(4):
        vp = vn[:, CHUNK * p:CHUNK * (p + 1)]
        w0 = jnp.where(tril, wm[2 * p], 0.0)
        w1 = jnp.where(tril, wm[2 * p + 1], 0.0)
        parts.append(_bdot(w0, jnp.where(low, vp, 0.0)) + _bdot(w1, jnp.where(low, 0.0, vp)))
    s = jnp.concatenate(parts, axis=1)
    lane_g = lax.shift_right_logical(lax.broadcasted_iota(jnp.int32, s.shape, 1), 6)
    bias = jnp.zeros_like(s)
    for g in range(8):
        bias = jnp.where(lane_g == g, sgb_t[:, g:g + 1], bias)
    return (u * (s + bias),)


def f_glu(p, gl):
    return (p * jax.nn.sigmoid(gl),)


def f_lnsilu(c1, g, b):
    return (jax.nn.silu(_layer_norm(c1, g, b)),)


def _head_norm(x, g64):
    g = jnp.concatenate([g64] * N_HEADS, axis=1)
    lane_h = lax.shift_right_logical(lax.broadcasted_iota(jnp.int32, x.shape, 1), 6)
    x2 = x * x
    r = jnp.zeros_like(x)
    for h in range(N_HEADS):
        mh = lane_h == h
        ms = jnp.sum(jnp.where(mh, x2, 0.0), axis=-1, keepdims=True) * (1.0 / HEAD_DIM)
        r = jnp.where(mh, lax.rsqrt(ms + EPS), r)
    return (x * r) * g


def f_qkv(zq, zk, zv, qg, kg):
    return (_head_norm(zq, qg) * 0.125, _head_norm(zk, kg), zv)


def f_merge(g0, g1, g2, ya, yb, yc, b0, b1, b2):
    return (jax.nn.sigmoid(g0 + b0) * ya + jax.nn.sigmoid(g1 + b1) * yb + jax.nn.sigmoid(g2 + b2) * yc,)


HALO = 32


def conv31_fwd(x, w, b, T, tt=256):
    C = x.shape[1]
    r = tt // HALO

    def body(x_ref, h_ref, w_ref, b_ref, y_ref, buf):
        i = pl.program_id(0)
        halo = h_ref[...]
        buf[0:HALO, :] = jnp.where(i > 0, halo, jnp.zeros_like(halo))
        buf[HALO:HALO + tt, :] = x_ref[...]
        acc = jnp.zeros((tt, C), f32) + b_ref[...]
        for k in range(CV_KERNEL):
            acc = acc + w_ref[k:k + 1, :] * buf[pl.ds(HALO - (CV_KERNEL - 1) + k, tt), :]
        y_ref[...] = acc

    return pl.pallas_call(
        body, name="conv31_fwd", grid=(T // tt,),
        in_specs=[pl.BlockSpec((tt, C), lambda i: (i, 0)),
                  pl.BlockSpec((HALO, C), lambda i: (jnp.maximum(i * r - 1, 0), 0)),
                  pl.BlockSpec((CV_KERNEL, C), lambda i: (0, 0)), pl.BlockSpec((1, C), lambda i: (0, 0))],
        out_specs=pl.BlockSpec((tt, C), lambda i: (i, 0)),
        out_shape=jax.ShapeDtypeStruct((T, C), f32),
        scratch_shapes=[pltpu.VMEM((HALO + tt, C), f32)], compiler_params=_cparams(1),
    )(x, x, w, b)


def conv31_bwd(x, w, dy, T, tt=256):
    C = x.shape[1]
    r = tt // HALO
    n = T // tt

    def body(x_ref, h_ref, w_ref, dy_ref, dyn_ref, dx_ref, dw_ref, db_ref, xbuf, dbuf):
        i = pl.program_id(0)
        halo = h_ref[...]
        xbuf[0:HALO, :] = jnp.where(i > 0, halo, jnp.zeros_like(halo))
        xbuf[HALO:HALO + tt, :] = x_ref[...]
        nxt = dyn_ref[...]
        dy = dy_ref[...]
        dbuf[0:tt, :] = dy
        dbuf[tt:tt + HALO, :] = jnp.where(i < n - 1, nxt, jnp.zeros_like(nxt))

        @pl.when(i == 0)
        def _():
            dw_ref[...] = jnp.zeros_like(dw_ref)
            db_ref[...] = jnp.zeros_like(db_ref)

        acc = jnp.zeros((tt, C), f32)
        for k in range(CV_KERNEL):
            acc = acc + w_ref[k:k + 1, :] * dbuf[pl.ds(CV_KERNEL - 1 - k, tt), :]
            xs = xbuf[pl.ds(HALO - (CV_KERNEL - 1) + k, tt), :]
            dw_ref[k:k + 1, :] += jnp.sum(dy * xs, axis=0, keepdims=True)
        dx_ref[...] = acc
        db_ref[...] += jnp.sum(dy, axis=0, keepdims=True)

    return pl.pallas_call(
        body, name="conv31_bwd", grid=(n,),
        in_specs=[pl.BlockSpec((tt, C), lambda i: (i, 0)),
                  pl.BlockSpec((HALO, C), lambda i: (jnp.maximum(i * r - 1, 0), 0)),
                  pl.BlockSpec((CV_KERNEL, C), lambda i: (0, 0)),
                  pl.BlockSpec((tt, C), lambda i: (i, 0)),
                  pl.BlockSpec((HALO, C), lambda i: (jnp.minimum((i + 1) * r, n * r - 1), 0))],
        out_specs=[pl.BlockSpec((tt, C), lambda i: (i, 0)), pl.BlockSpec((CV_KERNEL, C), lambda i: (0, 0)),
                   pl.BlockSpec((1, C), lambda i: (0, 0))],
        out_shape=[jax.ShapeDtypeStruct((T, C), f32), jax.ShapeDtypeStruct((CV_KERNEL, C), f32),
                   jax.ShapeDtypeStruct((1, C), f32)],
        scratch_shapes=[pltpu.VMEM((HALO + tt, C), f32), pltpu.VMEM((HALO + tt, C), f32)], compiler_params=_cparams(1),
    )(x, x, w, dy, dy)


FFN_TC = 128
FFN_PAD = 8


def ffn_act_fwd(up, cw, cb, T):
    F = up.shape[1] // 2
    nj = F // FFN_TC
    rc = min(128, T)

    def body(g_ref, v_ref, g0, g1, g2, gb, v0, v1, v2, vb, o_ref, gp, vp):
        zeros = jnp.zeros((FFN_PAD, FFN_TC), f32)
        for p, x_ref in ((gp, g_ref), (vp, v_ref)):
            p[0:FFN_PAD, :] = zeros
            p[FFN_PAD:FFN_PAD + T, :] = x_ref[...].astype(f32)
        wg = (g0[...], g1[...], g2[...], gb[...])
        wv = (v0[...], v1[...], v2[...], vb[...])

        def conv(p, w, r):
            return (w[0] * p[pl.ds(FFN_PAD + r - 2, rc), :] + w[1] * p[pl.ds(FFN_PAD + r - 1, rc), :]
                    + w[2] * p[pl.ds(FFN_PAD + r, rc), :] + w[3])

        for r in range(0, T, rc):
            o_ref[pl.ds(r, rc), :] = (jax.nn.silu(conv(gp, wg, r)) * conv(vp, wv, r)).astype(o_ref.dtype)

    gspec = pl.BlockSpec((T, FFN_TC), lambda j: (0, j))
    vspec = pl.BlockSpec((T, FFN_TC), lambda j: (0, j + nj))
    pg = pl.BlockSpec((1, FFN_TC), lambda j: (0, j))
    pv = pl.BlockSpec((1, FFN_TC), lambda j: (0, j + nj))
    return pl.pallas_call(
        body, name="ffn_act_fwd", grid=(nj,),
        in_specs=[gspec, vspec, pg, pg, pg, pg, pv, pv, pv, pv], out_specs=gspec,
        out_shape=jax.ShapeDtypeStruct((T, F), bf16),
        scratch_shapes=[pltpu.VMEM((FFN_PAD + T, FFN_TC), f32)] * 2, compiler_params=_cparams(1),
    )(up, up, cw[0], cw[1], cw[2], cb, cw[0], cw[1], cw[2], cb)


def ffn_act_bwd(up, cw, cb, dact, T):
    F = up.shape[1] // 2
    nj = F // FFN_TC

    rc = min(128, T)
    ext = rc + FFN_PAD

    def body(g_ref, v_ref, g0, g1, g2, gb, v0, v1, v2, vb, d_ref, dupg_ref, dupv_ref, *rest):
        p_refs, (gp, vp, dp, dgs, dvs) = rest[:8], rest[8:]
        zeros = jnp.zeros((FFN_PAD, FFN_TC), f32)
        for p, x_ref in ((gp, g_ref), (vp, v_ref)):
            p[0:FFN_PAD, :] = zeros
            p[FFN_PAD:FFN_PAD + T, :] = x_ref[...].astype(f32)
            p[FFN_PAD + T:FFN_PAD + T + FFN_PAD, :] = zeros
        dp[0:T, :] = d_ref[...]
        dp[T:T + FFN_PAD, :] = zeros
        wg = (g0[...], g1[...], g2[...], gb[...])
        wv = (v0[...], v1[...], v2[...], vb[...])
        acc = [jnp.zeros((1, FFN_TC), f32) for _ in range(8)]

        def taps(p, r):
            return tuple(p[pl.ds(FFN_PAD + r - s, ext), :] for s in (2, 1, 0))

        for r in range(0, T, rc):
            xg, xv = taps(gp, r), taps(vp, r)
            gc = wg[0] * xg[0] + wg[1] * xg[1] + wg[2] * xg[2] + wg[3]
            vc = wv[0] * xv[0] + wv[1] * xv[1] + wv[2] * xv[2] + wv[3]
            d = dp[pl.ds(r, ext), :]
            sg = jax.nn.sigmoid(gc)
            sides = ((d * vc * (sg * (1.0 + gc * (1.0 - sg))), xg, wg, dgs, dupg_ref), (d * (gc * sg), xv, wv, dvs, dupv_ref))
            for side, (dc, x, w, buf, dup_ref) in enumerate(sides):
                buf[...] = dc
                dc0 = dc[0:rc]
                dup = w[2] * dc0 + w[1] * buf[pl.ds(1, rc), :] + w[0] * buf[pl.ds(2, rc), :]
                dup_ref[pl.ds(r, rc), :] = dup.astype(dup_ref.dtype)
                for k in range(3):
                    acc[4 * side + k] = acc[4 * side + k] + jnp.sum(dc0 * x[k][0:rc], axis=0, keepdims=True)
                acc[4 * side + 3] = acc[4 * side + 3] + jnp.sum(dc0, axis=0, keepdims=True)
        for k in range(8):
            p_refs[k][...] = acc[k]

    gspec = pl.BlockSpec((T, FFN_TC), lambda j: (0, j))
    vspec = pl.BlockSpec((T, FFN_TC), lambda j: (0, j + nj))
    pg = pl.BlockSpec((1, FFN_TC), lambda j: (0, j))
    pv = pl.BlockSpec((1, FFN_TC), lambda j: (0, j + nj))
    res = pl.pallas_call(
        body, name="ffn_act_bwd", grid=(nj,),
        in_specs=[gspec, vspec, pg, pg, pg, pg, pv, pv, pv, pv, gspec],
        out_specs=[gspec, gspec] + [pg] * 8,
        out_shape=[jax.ShapeDtypeStruct((T, F), bf16)] * 2 + [jax.ShapeDtypeStruct((1, F), f32)] * 8,
        scratch_shapes=[pltpu.VMEM((FFN_PAD + T + FFN_PAD, FFN_TC), f32)] * 2 + [pltpu.VMEM((T + FFN_PAD, FFN_TC), f32)]
        + [pltpu.VMEM((ext, FFN_TC), f32)] * 2,
        compiler_params=_cparams(1),
    )(up, up, cw[0], cw[1], cw[2], cb, cw[0], cw[1], cw[2], cb, dact)
    dup = concat_cols("dup_concat", [res[0], res[1]], T)
    return (dup,) + tuple(jnp.concatenate([res[2 + k], res[6 + k]], axis=1) for k in range(4))


BQ = 256
BK = 256
assert BQ % BK == 0
NT_DIMS = (((1,), (1,)), ((), ()))
TN_DIMS = (((0,), (0,)), ((), ()))


def _split_dot(x, u):
    x1 = x.astype(bf16)
    x2 = (x - x1.astype(f32)).astype(bf16)
    n = x.shape[0]
    y = jnp.dot(jnp.concatenate([x1, x2], axis=0), u, preferred_element_type=f32)
    return y[0:n] + y[n:2 * n]


def _log_sigmoids(z):
    sp = jnp.log(1.0 + jnp.exp(-jnp.abs(z)))
    lsp = jnp.minimum(z, 0.0) - sp
    return lsp, lsp - z


def _stack_heads(x):
    head1 = lax.broadcasted_iota(jnp.int32, x.shape, 1) >= HEAD_DIM
    zero = jnp.zeros_like(x)
    return jnp.concatenate([jnp.where(head1, zero, x), jnp.where(head1, x, zero)], axis=0)


def _unstack_heads(y):
    head1 = lax.broadcasted_iota(jnp.int32, (BQ, y.shape[1]), 1) >= HEAD_DIM
    return jnp.where(head1, y[BQ:2 * BQ], y[0:BQ])


def last_block(i):
    return ((i + 1) * BQ - 1) // BK


def _attn_masks():
    row = lax.broadcasted_iota(jnp.int32, (2 * BQ, BK), 0)
    col = lax.broadcasted_iota(jnp.int32, (2 * BQ, BK), 1)
    ur = lax.broadcasted_iota(jnp.int32, (BK, BK), 0)
    uc = lax.broadcasted_iota(jnp.int32, (BK, BK), 1)
    return (row & (BQ - 1)) - col, (ur > uc).astype(bf16), (ur < uc).astype(bf16)


def attn_fwd(q, k, v, T, gather=None):
    nq = T // BQ
    n_g = 0 if gather is None else len(gather)

    def body(*refs):
        q_ref, k_ref, v_ref = refs[:3]
        o_ref, tot_ref = refs[3 + n_g:5 + n_g]
        p, i = pl.program_id(0), pl.program_id(1)
        if n_g:
            start, forward, finish = _two_level_gather(refs[5 + n_g:5 + 2 * n_g], *refs[5 + 2 * n_g:])
            pl.when(jnp.logical_and(p == 0, i == 0))(start)
        qs = _stack_heads(q_ref[...])
        diff, u_after, _ = _attn_masks()

        def step(t, carry):
            acc, c = carry
            jb = last_block(i) - t
            ks = pl.multiple_of(jb * BK, BK)
            kb = k_ref[pl.ds(ks, BK), :]
            vb = v_ref[pl.ds(ks, BK), :]
            m = diff > jb * BK - i * BQ
            z = lax.dot_general(qs, kb, NT_DIMS, preferred_element_type=f32)
            lsp, lsn = _log_sigmoids(z)
            lm = jnp.where(m, lsn, 0.0)
            a = jnp.where(m, jnp.exp(lsp + _split_dot(lm, u_after)), 0.0)
            acc = acc + jnp.exp(c) * jnp.dot(a.astype(bf16), vb, preferred_element_type=f32)
            return acc, c + jnp.sum(lm, axis=-1, keepdims=True)

        acc, c = lax.fori_loop(0, last_block(i) + 1, step, (jnp.zeros((2 * BQ, 128), f32), jnp.zeros((2 * BQ, 1), f32)))
        o_ref[...] = _unstack_heads(acc).astype(o_ref.dtype)
        tot_ref[...] = _unstack_heads(jnp.broadcast_to(c, (2 * BQ, 128)))
        if n_g:
            @pl.when(jnp.logical_and(p == 3, i == nq - 1))
            def _():
                forward()
                finish()

    blk = pl.BlockSpec((BQ, 128), lambda p, i: (i, p))
    full = pl.BlockSpec((T, 128), lambda p, i: (0, p))
    bufs = [] if gather is None else gather_buffers(gather)
    res = pl.pallas_call(
        body, name="attn_fwd_gather" if n_g else "attn_fwd", grid=(4, nq),
        in_specs=[blk, full, full] + [_ANY] * n_g, out_specs=[blk, blk] + [_ANY] * n_g,
        out_shape=[jax.ShapeDtypeStruct((T, 512), bf16), jax.ShapeDtypeStruct((T, 512), f32)]
        + [jax.ShapeDtypeStruct(b.shape, b.dtype) for b in bufs],
        input_output_aliases={3 + k: 2 + k for k in range(n_g)},
        scratch_shapes=_sems(n_g, N_DEV - 1) if n_g else [], compiler_params=_cparams(2),
    )(q, k, v, *bufs)
    return res[0], res[1], list(res[2:])


def attn_bwd(q, k, v, do, tot, T, scatter=None):
    nq = T // BQ
    n_s = 0 if scatter is None else len(scatter[0])
    modes = [] if scatter is None else list(scatter[1])

    def body(*refs):
        q_ref, k_ref, v_ref, do_ref, tot_ref = refs[:5]
        dq_ref, dk_ref, dv_ref = refs[5 + 2 * n_s:8 + 2 * n_s]
        p, i = pl.program_id(0), pl.program_id(1)
        if n_s:
            def copies():
                return _chip_copies(refs[5:5 + n_s], refs[8 + 2 * n_s:8 + 3 * n_s], modes, *refs[8 + 3 * n_s:])

            @pl.when(jnp.logical_and(p == 0, i == 0))
            def _():
                for cp in copies():
                    cp.start()

        @pl.when(i == 0)
        def _():
            dk_ref[...] = jnp.zeros_like(dk_ref)
            dv_ref[...] = jnp.zeros_like(dv_ref)

        qs = _stack_heads(q_ref[...])
        dos = _stack_heads(do_ref[...].astype(bf16))
        totv = tot_ref[...]
        tots = jnp.concatenate([totv[:, 0:1], totv[:, HEAD_DIM:HEAD_DIM + 1]], axis=0)
        diff, u_after, u_before = _attn_masks()

        def step(jb, carry):
            dq, cl, cg = carry
            ks = pl.multiple_of(jb * BK, BK)
            kb = k_ref[pl.ds(ks, BK), :]
            vb = v_ref[pl.ds(ks, BK), :]
            m = diff > jb * BK - i * BQ
            z = lax.dot_general(qs, kb, NT_DIMS, preferred_element_type=f32)
            lsp, lsn = _log_sigmoids(z)
            lm = jnp.where(m, lsn, 0.0)
            a = jnp.where(m, jnp.exp(lsp + _split_dot(lm, u_after)), 0.0)
            g = a * lax.dot_general(dos, vb, NT_DIMS, preferred_element_type=f32)
            beta = jnp.exp(lsp)
            aa = jnp.where(m, g * jnp.exp(lsn) - _split_dot(g, u_before) * beta, 0.0)
            bb = jnp.where(m, beta, 0.0)
            cl = cl + jnp.sum(lm, axis=-1, keepdims=True)
            f = jnp.exp(tots - cl)
            dz = (f * aa - cg * bb).astype(bf16)
            cg = cg + f * jnp.sum(g, axis=-1, keepdims=True)
            dq = dq + jnp.dot(dz, kb, preferred_element_type=f32)
            dk_ref[pl.ds(ks, BK), :] += lax.dot_general(dz, qs, TN_DIMS, preferred_element_type=f32)
            dv_ref[pl.ds(ks, BK), :] += lax.dot_general((f * a).astype(bf16), dos, TN_DIMS, preferred_element_type=f32)
            return dq, cl, cg

        zc = jnp.zeros((2 * BQ, 1), f32)
        dq, _, _ = lax.fori_loop(0, last_block(i) + 1, step, (jnp.zeros((2 * BQ, 128), f32), zc, zc))
        dq_ref[...] = _unstack_heads(dq)
        if n_s:
            @pl.when(jnp.logical_and(p == 3, i == nq - 1))
            def _():
                for cp in copies():
                    cp.wait()

    blk = pl.BlockSpec((BQ, 128), lambda p, i: (i, p))
    full = pl.BlockSpec((T, 128), lambda p, i: (0, p))
    shp = jax.ShapeDtypeStruct((T, 512), f32)
    s_list = [] if scatter is None else list(scatter[0])
    bufs = chip_buffers(s_list, modes) if n_s else []
    res = pl.pallas_call(
        body, name="attn_bwd_scatter" if n_s else "attn_bwd", grid=(4, nq),
        in_specs=[blk, full, full, blk, blk] + [_ANY] * (2 * n_s), out_specs=[blk, full, full] + [_ANY] * n_s,
        out_shape=[shp, shp, shp] + [jax.ShapeDtypeStruct(b.shape, b.dtype) for b in bufs],
        input_output_aliases={5 + n_s + k: 3 + k for k in range(n_s)},
        scratch_shapes=_sems(n_s, 3) if n_s else [], compiler_params=_cparams(2),
    )(q, k, v, do, tot, *s_list, *bufs)
    return res[0], res[1], res[2], list(res[3:])


def loss_head(y, target, T, tm=256):
    D = y.shape[1]

    def body(y_ref, t_ref, dy_ref, l_ref):
        i = pl.program_id(0)
        err = y_ref[...] - t_ref[...]
        dy_ref[...] = err * (1.0 / D)
        part = 0.5 * jnp.sum(jnp.sum(err * err, axis=-1, keepdims=True) * (1.0 / D), axis=0, keepdims=True)

        @pl.when(i == 0)
        def _():
            l_ref[...] = jnp.zeros_like(l_ref)

        l_ref[...] += jnp.broadcast_to(part, l_ref.shape)

    spec = pl.BlockSpec((tm, D), lambda i: (i, 0))
    return pl.pallas_call(
        body, name="loss_head", grid=(T // tm,), in_specs=[spec, spec],
        out_specs=[spec, pl.BlockSpec((1, 128), lambda i: (0, 0))],
        out_shape=[jax.ShapeDtypeStruct((T, D), f32), jax.ShapeDtypeStruct((1, 128), f32)],
        compiler_params=_cparams(1),
    )(y, target)


def _row_tile(rows, cols, offset=0, max_elems=128 * 1024):
    best = None
    for t in range(16, rows + 1, 16):
        if rows % t == 0 and offset % t == 0 and t * cols <= max_elems:
            best = t
    return best if best is not None else rows


def adamw(name, w, m, v, parts, layer, prev=None, row_off=0):
    L, R, C = w.shape
    n_parts = parts.shape[0]
    tr = _row_tile(R, C, row_off, max_elems=256 * 1024)
    assert R % tr == 0 and row_off % tr == 0, (name, R, row_off, tr)

    def body(w_ref, m_ref, v_ref, p_ref, *rest):
        g_ref, d_ref, nm_ref, nv_ref = rest[-4:]
        g = p_ref[0].astype(f32)
        for s in range(1, n_parts):
            g = g + p_ref[s].astype(f32)
        wv = w_ref[...]
        mn = ADAM_B1 * m_ref[...] + (1.0 - ADAM_B1) * g
        vn = ADAM_B2 * v_ref[...] + (1.0 - ADAM_B2) * jnp.square(g)
        m_hat = mn / (1.0 - ADAM_B1 ** ADAM_STEP)
        v_hat = vn / (1.0 - ADAM_B2 ** ADAM_STEP)
        g_ref[...] = g
        d_ref[...] = -ADAM_LR * (m_hat / (jnp.sqrt(v_hat) + ADAM_EPS) + ADAM_WD * wv)
        nm_ref[...] = mn
        nv_ref[...] = vn

    spec = pl.BlockSpec((None, tr, C), lambda i: (layer, i, 0))
    shp = jax.ShapeDtypeStruct((L, R, C), f32)
    n_prev = 0 if prev is None else 4
    return pl.pallas_call(
        body, name=name, grid=(R // tr,),
        in_specs=[spec, spec, spec, pl.BlockSpec((n_parts, tr, C), lambda i: (0, row_off // tr + i, 0))] + [_ANY] * n_prev,
        out_specs=[spec] * 4, out_shape=[shp] * 4, input_output_aliases={4 + k: k for k in range(n_prev)},
        compiler_params=_cparams(1),
    )(w, m, v, parts, *(prev or ()))


REPL = ["ln1_g", "sg_ln_g", "sg_ln_b", "sg_w", "sg_b", "cv_b", "cv_ln_g", "cv_ln_b", "q_norm_g", "k_norm_g", "ln2_g",
        "ffn_conv_b"]
SMALL_SHARDED = ["b_gate", "cv_w", "ffn_conv_w"]
BIG = ["w_in", "w_a_out", "w_b_out", "w_c_out", "w_up", "w_out", "w_down"]
COL_SHARDED = ["w_a_out", "w_b_out", "w_c_out"]
TRANSPOSED = ["w_in", "w_up"]


def _pack(arrs, rows):
    flat = jnp.concatenate([a.reshape(-1) for a in arrs])
    return jnp.pad(flat, (0, rows * 128 - flat.shape[0])).reshape(rows, 128)


def _pack_layers(arrs, rows):
    flat = jnp.concatenate([a.reshape(a.shape[0], -1) for a in arrs], axis=1)
    return jnp.pad(flat, ((0, 0), (0, rows * 128 - flat.shape[1]))).reshape(flat.shape[0], rows, 128)


def _unpack_layers(packed, shapes):
    flat = packed.reshape(packed.shape[0], -1)
    out, pos = [], 0
    for s in shapes:
        n = 1
        for d in s[1:]:
            n *= d
        out.append(flat[:, pos:pos + n].reshape(s))
        pos += n
    return out


def _rows_for(shapes, mult):
    n = 0
    for s in shapes:
        k = 1
        for d in s:
            k *= d
        n += k
    rows = -(-n // 128)
    return -(-rows // mult) * mult


W_NAMES = ['ln1_g', 'w_in', 'b_gate', 'sg_ln_g', 'sg_ln_b', 'sg_w', 'sg_b', 'w_a_out', 'cv_w', 'cv_b', 'cv_ln_g', 'cv_ln_b',
           'w_b_out', 'q_norm_g', 'k_norm_g', 'w_c_out', 'w_out', 'ln2_g', 'w_up', 'ffn_conv_w', 'ffn_conv_b', 'w_down']


def _forward_layer(x, P, T, late=None):
    D = x.shape[1]
    sv = {"x0": x}
    (h1,) = rowwise("rms_fwd", f_rms, [(x, D, 0)], [(P["ln1_g"], (1, D), None)], [(D, D, 0, bf16)], T, 256)
    z = mm_nt("in_proj", h1, P["w_in_t"], bf16, tko=1664, tc=D)
    sv["h1"], sv["z"] = h1, z
    (ya_in,) = rowwise("sgu_fwd", f_sgu, [(z, 512, 0), (z, 512, 1)],
                       [(P["sg_ln_g"], (1, 512), None), (P["sg_ln_b"], (1, 512), None),
                        (P["sg_w"], (8, CHUNK, CHUNK), None), (P["sg_b_t"], (CHUNK, 8), None)],
                       [(512, 512, 0, bf16)], T, CHUNK)
    (c0,) = rowwise("glu_fwd", f_glu, [(z, 512, 2), (z, 512, 3)], [], [(512, 512, 0, f32)], T, 256)
    c1 = conv31_fwd(c0, P["cv_w"], P["cv_b"], T)
    (c3,) = rowwise("lnsilu_fwd", f_lnsilu, [(c1, 512, 0)], [(P["cv_ln_g"], (1, 512), None), (P["cv_ln_b"], (1, 512), None)],
                    [(512, 512, 0, bf16)], T, 256)
    q8, kn, vb = rowwise("qkv_fwd", f_qkv, [(z, 512, 4), (z, 512, 5), (z, 512, 6)],
                         [(P["q_norm_g"], (1, HEAD_DIM), None), (P["k_norm_g"], (1, HEAD_DIM), None)],
                         [(512, 512, 0, bf16)] * 3, T, 256)
    o, tot, _ = attn_fwd(q8, kn, vb, T)
    if late is not None:
        P.update(late(o))
    ya = mm_nn("a_out", ya_in, P["w_a_out"], f32)
    yb = mm_nn("b_out", c3, P["w_b_out"], f32)
    yc = mm_nn("c_out", o, P["w_c_out"], f32)
    (merged,) = rowwise("merge_fwd", f_merge,
                        [(z, 512, lambda c: 7 + c), (z, 512, lambda c: 9 + c), (z, 512, lambda c: 11 + c),
                         (ya, 512, lambda c: c), (yb, 512, lambda c: c), (yc, 512, lambda c: c)],
                        [(P["b_gate"][k], (1, 512), lambda c: c) for k in range(3)],
                        [(D, 512, lambda c: c, bf16)], T, 256, ncol=2)
    x1 = mm_nn("out_proj", merged, P["w_out"], f32, add=x)
    sv.update(ya_in=ya_in, ya=ya, c0=c0, c1=c1, c3=c3, yb=yb, q8=q8, kn=kn, vb=vb, o=o, tot=tot, yc=yc, merged=merged, x1=x1)
    (h2,) = rowwise("rms_fwd", f_rms, [(x1, D, 0)], [(P["ln2_g"], (1, D), None)], [(D, D, 0, bf16)], T, 256)
    up = mm_nt("up_proj", h2, P["w_up_t"], bf16, tko=1408, tc=D)
    act = ffn_act_fwd(up, P["ffn_conv_w"], P["ffn_conv_b"], T)
    x2 = mm_nn("down_proj", act, P["w_down"], f32, add=x1)
    sv.update(h2=h2, up=up, act=act)
    return x2, sv


def _backward_layer(dx2, P, sv, T, scatter=None):
    D = dx2.shape[1]
    G = {}
    G["w_down"] = mm_tn("dw_down", sv["act"], dx2, tk=1408)
    dact = mm_nt("d_act", dx2, P["w_down"], f32, tko=1408, tc=D)
    dup, dcw0, dcw1, dcw2, G["ffn_conv_b"] = ffn_act_bwd(sv["up"], P["ffn_conv_w"], P["ffn_conv_b"], dact, T)
    G["ffn_conv_w"] = jnp.concatenate([dcw0, dcw1, dcw2], axis=0)
    G["w_up"] = mm_tn("dw_up", dup, sv["h2"], tk=1408)
    dh2 = mm_nn("d_h2", dup, P["w_up_t"], f32, tm=512)
    dx1, G["ln2_g"] = rowwise_bwd("rms_bwd", f_rms, [(sv["x1"], D, 0)], [(P["ln2_g"], (1, D), None)], [(dh2, D, 0)],
                                  [f32], T, 256, adds=[(dx2, D, 0)])
    G["w_out"] = mm_tn("dw_out", sv["merged"], dx1, tk=1024)
    dmerged = mm_nt("d_merged", dx1, P["w_out"], f32)
    z = sv["z"]
    dg0, dg1, dg2, dya, dyb, dyc, db0, db1, db2 = rowwise_bwd(
        "merge_bwd", f_merge,
        [(z, 512, lambda c: 7 + c), (z, 512, lambda c: 9 + c), (z, 512, lambda c: 11 + c),
         (sv["ya"], 512, lambda c: c), (sv["yb"], 512, lambda c: c), (sv["yc"], 512, lambda c: c)],
        [(P["b_gate"][k], (1, 512), lambda c: c) for k in range(3)],
        [(dmerged, 512, lambda c: c)], [bf16] * 6, T, 256, ncol=2)
    G["b_gate"] = jnp.concatenate([db0, db1, db2], axis=0)
    G["w_c_out"] = mm_tn("dw_c_out", sv["o"], dyc, n_slab=CHUNK)
    do = mm_nt("d_o", dyc, P["w_c_out"], f32)
    dq8, dkn, dvb, received = attn_bwd(sv["q8"], sv["kn"], sv["vb"], do, sv["tot"], T, scatter=scatter)
    dzq, dzk, dzv, G["q_norm_g"], G["k_norm_g"] = rowwise_bwd(
        "qkv_bwd", f_qkv, [(z, 512, 4), (z, 512, 5), (z, 512, 6)],
        [(P["q_norm_g"], (1, HEAD_DIM), None), (P["k_norm_g"], (1, HEAD_DIM), None)],
        [(dq8, 512, 0), (dkn, 512, 0), (dvb, 512, 0)], [bf16] * 3, T, 256)
    G["w_b_out"] = mm_tn("dw_b_out", sv["c3"], dyb, n_slab=CHUNK)
    dc3 = mm_nt("d_c3", dyb, P["w_b_out"], f32)
    dc1, G["cv_ln_g"], G["cv_ln_b"] = rowwise_bwd(
        "lnsilu_bwd", f_lnsilu, [(sv["c1"], 512, 0)], [(P["cv_ln_g"], (1, 512), None), (P["cv_ln_b"], (1, 512), None)],
        [(dc3, 512, 0)], [f32], T, 256)
    dc0, G["cv_w"], G["cv_b"] = conv31_bwd(sv["c0"], P["cv_w"], dc1, T)
    dzp, dzgl = rowwise_bwd("glu_bwd", f_glu, [(z, 512, 2), (z, 512, 3)], [], [(dc0, 512, 0)], [bf16] * 2, T, 256)
    G["w_a_out"] = mm_tn("dw_a_out", sv["ya_in"], dya, n_slab=CHUNK)
    dya_in = mm_nt("d_ya_in", dya, P["w_a_out"], f32)
    dzu, dzv_a, G["sg_ln_g"], G["sg_ln_b"], G["sg_w"], dsbt = rowwise_bwd(
        "sgu_bwd", f_sgu, [(z, 512, 0), (z, 512, 1)],
        [(P["sg_ln_g"], (1, 512), None), (P["sg_ln_b"], (1, 512), None), (P["sg_w"], (8, CHUNK, CHUNK), None),
         (P["sg_b_t"], (CHUNK, 8), None)],
        [(dya_in, 512, 0)], [bf16] * 2, T, CHUNK)
    G["sg_b"] = dsbt.T
    dz = concat_cols("dz_concat", [dzu, dzv_a, dzp, dzgl, dzq, dzk, dzv, dg0, dg1, dg2], T)
    G["w_in"] = mm_tn("dw_in", dz, sv["h1"], tk=1664)
    dh1 = mm_nn("d_h1", dz, P["w_in_t"], f32, tm=512)
    dx0, G["ln1_g"] = rowwise_bwd("rms_bwd", f_rms, [(sv["x0"], D, 0)], [(P["ln1_g"], (1, D), None)], [(dh1, D, 0)],
                                  [f32], T, 256, adds=[(dx1, D, 0)])
    return dx0, G, received


def kernel(x, ln1_g, w_in, b_gate, sg_ln_g, sg_ln_b, sg_w, sg_b, w_a_out, cv_w, cv_b, cv_ln_g, cv_ln_b, w_b_out, q_norm_g, k_norm_g, w_c_out, w_out, ln2_g, w_up, ffn_conv_w, ffn_conv_b, w_down, loss_target, m_ln1_g, m_w_in, m_b_gate, m_sg_ln_g, m_sg_ln_b, m_sg_w, m_sg_b, m_w_a_out, m_cv_w, m_cv_b, m_cv_ln_g, m_cv_ln_b, m_w_b_out, m_q_norm_g, m_k_norm_g, m_w_c_out, m_w_out, m_ln2_g, m_w_up, m_ffn_conv_w, m_ffn_conv_b, m_w_down, v_ln1_g, v_w_in, v_b_gate, v_sg_ln_g, v_sg_ln_b, v_sg_w, v_sg_b, v_w_a_out, v_cv_w, v_cv_b, v_cv_ln_g, v_cv_ln_b, v_w_b_out, v_q_norm_g, v_k_norm_g, v_w_c_out, v_w_out, v_ln2_g, v_w_up, v_ffn_conv_w, v_ffn_conv_b, v_w_down):
    W = dict(ln1_g=ln1_g, w_in=w_in, b_gate=b_gate, sg_ln_g=sg_ln_g, sg_ln_b=sg_ln_b, sg_w=sg_w, sg_b=sg_b, w_a_out=w_a_out,
             cv_w=cv_w, cv_b=cv_b, cv_ln_g=cv_ln_g, cv_ln_b=cv_ln_b, w_b_out=w_b_out, q_norm_g=q_norm_g, k_norm_g=k_norm_g,
             w_c_out=w_c_out, w_out=w_out, ln2_g=ln2_g, w_up=w_up, ffn_conv_w=ffn_conv_w, ffn_conv_b=ffn_conv_b, w_down=w_down)
    M = dict(ln1_g=m_ln1_g, w_in=m_w_in, b_gate=m_b_gate, sg_ln_g=m_sg_ln_g, sg_ln_b=m_sg_ln_b, sg_w=m_sg_w, sg_b=m_sg_b,
             w_a_out=m_w_a_out, cv_w=m_cv_w, cv_b=m_cv_b, cv_ln_g=m_cv_ln_g, cv_ln_b=m_cv_ln_b, w_b_out=m_w_b_out,
             q_norm_g=m_q_norm_g, k_norm_g=m_k_norm_g, w_c_out=m_w_c_out, w_out=m_w_out, ln2_g=m_ln2_g, w_up=m_w_up,
             ffn_conv_w=m_ffn_conv_w, ffn_conv_b=m_ffn_conv_b, w_down=m_w_down)
    V = dict(ln1_g=v_ln1_g, w_in=v_w_in, b_gate=v_b_gate, sg_ln_g=v_sg_ln_g, sg_ln_b=v_sg_ln_b, sg_w=v_sg_w, sg_b=v_sg_b,
             w_a_out=v_w_a_out, cv_w=v_cv_w, cv_b=v_cv_b, cv_ln_g=v_cv_ln_g, cv_ln_b=v_cv_ln_b, w_b_out=v_w_b_out,
             q_norm_g=v_q_norm_g, k_norm_g=v_k_norm_g, w_c_out=v_w_c_out, w_out=v_w_out, ln2_g=v_ln2_g, w_up=v_w_up,
             ffn_conv_w=v_ffn_conv_w, ffn_conv_b=v_ffn_conv_b, w_down=v_w_down)
    T, D = x.shape[1], x.shape[2]
    L = DEPTH
    xs = x.reshape(T, D)
    target = loss_target.reshape(T, D)

    ss_shapes = [W[n].shape for n in SMALL_SHARDED]
    ss_rows = _rows_for(ss_shapes, 8)
    (ss_all,) = exchange("gather_small", [_pack([W[n] for n in SMALL_SHARDED], ss_rows)], ["gather"])
    full_small = {}
    pos = 0
    for n in SMALL_SHARDED:
        s = W[n].shape
        cnt = s[0] * s[1] * s[2]
        part = ss_all.reshape(N_DEV, -1)[:, pos:pos + cnt].reshape((N_DEV,) + s)
        full_small[n] = jnp.transpose(part, (1, 2, 0, 3)).reshape(s[0], s[1], N_DEV * s[2])
        pos += cnt

    params, saved = [], []
    cur = xs
    Wt = {n: (jnp.transpose(W[n], (0, 2, 1)) if n in TRANSPOSED else W[n]) for n in BIG}

    def local_slabs(l):
        return [Wt[n][l].astype(bf16) for n in BIG]

    def weights_of(names, slabs):
        out = {}
        for n, s in zip(names, slabs):
            if n in TRANSPOSED:
                out[n + "_t"] = s.reshape(-1, s.shape[-1])
            else:
                out[n] = assemble("assemble_" + n, s) if n in COL_SHARDED else s.reshape(-1, s.shape[-1])
        return out

    LATE = [n for n in BIG if n != "w_in"]
    def behind(token, blocks):
        return [blocks[0] + token[0, 0].astype(blocks[0].dtype)] + list(blocks[1:])

    first, token = gather2("gather_w_in", [local_slabs(0)[0]])
    late0 = gather_start("gather_start_0", behind(token, local_slabs(0)[1:]))
    in_flight, token = {}, late0[3]
    for l in range(1, L):
        in_flight[l] = gather_start("gather_start_%d" % l, behind(token, local_slabs(l)))
        token = in_flight[l][3]
    start_token = token[0, 0]
    for l in range(L):
        if l == 0:
            P = weights_of(["w_in"], first)
            late = lambda o: weights_of(LATE, gather_wait("gather_wait_0", late0[0], late0[1], late0[2], o))
        else:
            send_sems, recv_sems, bufs, _ = in_flight[l]
            P = weights_of(BIG, gather_wait("gather_wait_%d" % l, send_sems, recv_sems, bufs, cur))
            late = None
        for n in REPL:
            P[n] = W[n][l]
        for n in ("ln1_g", "sg_ln_g", "sg_ln_b", "cv_b", "cv_ln_g", "cv_ln_b", "q_norm_g", "k_norm_g", "ln2_g", "ffn_conv_b"):
            P[n] = P[n].reshape(1, -1)
        P["sg_b_t"] = P["sg_b"].T
        P["cv_w"] = full_small["cv_w"][l]
        P["b_gate"] = [full_small["b_gate"][l][k:k + 1] for k in range(3)]
        P["ffn_conv_w"] = [full_small["ffn_conv_w"][l][k:k + 1] for k in range(3)]
        if l == 0:
            P["ln1_g"] = P["ln1_g"] + start_token
        cur, sv = _forward_layer(cur, P, T, late=late)
        params.append(P)
        saved.append(sv)

    dy, loss_row = loss_head(cur, target, T)
    loss = lax.psum(loss_row[0, 0], ("x", "y", "c"))

    repl_shapes = [W[n].shape for n in REPL]
    repl_rows = _rows_for([s[1:] for s in repl_shapes], 16)
    ssl_rows = _rows_for([s[1:] for s in ss_shapes], 16)
    state = {"repl": [_pack_layers([X[n] for n in REPL], repl_rows) for X in (W, M, V)],
             "ss": [_pack_layers([X[n] for n in SMALL_SHARDED], ssl_rows) for X in (W, M, V)]}
    big_rows, big_off = {}, {}
    for n in BIG:
        big_off[n] = sum(big_rows.values())
        big_rows[n] = W[n].shape[1] * W[n].shape[2] // D
        state[n] = [(jnp.transpose(X[n], (0, 2, 1)) if n in TRANSPOSED else X[n]).reshape(L, big_rows[n], D)
                    for X in (W, M, V)]
    done = {n: None for n in state}

    def update(layer, received):
        for n in BIG:
            done[n] = adamw("adamw_" + n, *state[n], received[0], layer, done[n], row_off=big_off[n])
        done["repl"] = adamw("adamw_repl", *state["repl"], received[1], layer, done["repl"])
        done["ss"] = adamw("adamw_ss", *state["ss"], received[2], layer, done["ss"])

    def reduce_in_chip(G):
        big = [G[n].reshape(4, 2, big_rows[n], D) for n in BIG]
        ss_parts = []
        for n in SMALL_SHARDED:
            k, c = W[n].shape[1:]
            ss_parts.append(jnp.transpose(G[n].reshape(k, N_DEV, c), (1, 0, 2)).reshape(N_DEV, k * c))
        ss_send = jnp.concatenate(ss_parts, axis=1)
        ss_send = jnp.pad(ss_send, ((0, 0), (0, ssl_rows * 128 - ss_send.shape[1]))).reshape(4, 2, ssl_rows, 128)
        repl = _pack([G[n] for n in REPL], repl_rows)
        core = lax.axis_index("c")
        kept = jnp.concatenate([lax.dynamic_index_in_dim(b, core, 1, keepdims=False) for b in big], axis=1)
        ss_kept = lax.dynamic_index_in_dim(ss_send, core, 1, keepdims=False)
        got, repl_got, ss_got = pair_exchange("pair_exchange", big, [repl, ss_send], ["gather", "scatter"])
        return [add2("pair_add_big", kept, got), add2("pair_add_repl", repl, repl_got), add2("pair_add_ss", ss_kept, ss_got)]

    chip_modes = ["scatter", "gather", "scatter"]
    dcur = dy
    send = None
    for l in reversed(range(L)):
        dcur, G, recv = _backward_layer(dcur, params[l], saved[l], T, scatter=send)
        if send is not None:
            update(l + 1, recv)
        send = (reduce_in_chip(G), chip_modes)
    update(0, chip_exchange("chip_exchange_last", send[0], send[1]))

    results = [{}, {}, {}, {}]
    for k in range(4):
        for n, a in zip(REPL, _unpack_layers(done["repl"][k], repl_shapes)):
            results[k][n] = a
        for n, a in zip(SMALL_SHARDED, _unpack_layers(done["ss"][k], ss_shapes)):
            results[k][n] = a
        for n in BIG:
            results[k][n] = jnp.transpose(done[n][k], (0, 2, 1)) if n in TRANSPOSED else done[n][k].reshape(W[n].shape)
    out = [loss, dcur.reshape(1, T, D)]
    for k in range(4):
        out += [results[k][n] for n in W_NAMES]
    return tuple(out)
```

```python
import functools

import jax
import jax.numpy as jnp
from jax import lax
from jax.experimental import pallas as pl
from jax.experimental.pallas import tpu as pltpu

f32 = jnp.float32
bf16 = jnp.bfloat16

EPS = 1e-6
N_DEV = 8
DEPTH = 4
CHUNK = 128
HEAD_DIM = 64
N_HEADS = 8
CV_KERNEL = 31
VMEM_LIMIT_BYTES = 56 * 2 ** 20

ADAM_LR = 0.001
ADAM_B1 = 0.9
ADAM_B2 = 0.999
ADAM_EPS = 1e-08
ADAM_WD = 0.01
ADAM_STEP = 10

MESH = pl.DeviceIdType.MESH


def _cparams(n_grid):
    return pltpu.CompilerParams(dimension_semantics=("arbitrary",) * n_grid, vmem_limit_bytes=VMEM_LIMIT_BYTES)


def exchange(name, arrays, modes):
    n = len(arrays)

    def body(*refs):
        copies = _direct_copies(refs[:n], refs[n:2 * n], modes, *refs[2 * n + 1:])
        for cp in copies:
            cp.start()
        for cp in copies:
            cp.wait()
        refs[2 * n][...] = jnp.zeros_like(refs[2 * n])

    res = pl.pallas_call(
        body, name=name, out_shape=_exchange_out_shapes(arrays, modes) + [jax.ShapeDtypeStruct((8, 128), f32)],
        in_specs=[_ANY] * n, out_specs=[_ANY] * n + [pl.BlockSpec(memory_space=pltpu.VMEM)],
        scratch_shapes=_exchange_sems(n),
    )(*arrays)
    return list(res[:n]), res[n]


_ANY = pl.BlockSpec(memory_space=pl.ANY)


def _exchange_out_shapes(arrays, modes):
    return [jax.ShapeDtypeStruct((N_DEV,) + tuple(a.shape) if m == "gather" else tuple(a.shape), a.dtype)
            for a, m in zip(arrays, modes)]


def _exchange_sems(n):
    return [pltpu.SemaphoreType.DMA((n, N_DEV - 1)), pltpu.SemaphoreType.DMA((n, N_DEV - 1)), pltpu.SemaphoreType.DMA((n,))]


def _direct_copies(ins, outs, modes, send_sems, recv_sems, local_sems):
    x, y, c = lax.axis_index("x"), lax.axis_index("y"), lax.axis_index("c")
    me = 4 * x + 2 * y + c
    copies = []
    for k in range(len(ins)):
        src_mine = ins[k] if modes[k] == "gather" else ins[k].at[me]
        copies.append(pltpu.make_async_copy(src_mine, outs[k].at[me], local_sems.at[k]))
        for r in range(1, N_DEV):
            px = 1 - x if r & 4 else x
            py = 1 - y if r & 2 else y
            pc = 1 - c if r & 1 else c
            src = ins[k] if modes[k] == "gather" else ins[k].at[4 * px + 2 * py + pc]
            copies.append(pltpu.make_async_remote_copy(
                src_ref=src, dst_ref=outs[k].at[me], send_sem=send_sems.at[k, r - 1], recv_sem=recv_sems.at[k, r - 1],
                device_id=(px, py, pc), device_id_type=MESH))
    return copies


def own_slot_buffer(block, n_slots, index):
    return lax.dynamic_update_slice(lax.empty((n_slots,) + tuple(block.shape), block.dtype), block[None],
                                    (index,) + (0,) * block.ndim)


def _sems(n, m):
    return [pltpu.SemaphoreType.DMA((n, m)), pltpu.SemaphoreType.DMA((n, m))]


def _two_level_gather(bufs, send_sems, recv_sems):
    x, y, c = lax.axis_index("x"), lax.axis_index("y"), lax.axis_index("c")
    me = 4 * x + 2 * y + c
    sibling = (x, y, 1 - c)
    chips = [(1 - x, y), (x, 1 - y), (1 - x, 1 - y)]

    def slot(px, py, pc):
        return 4 * px + 2 * py + pc

    def copy(k, sem, block, to):
        rows = bufs[k].at[block]
        return pltpu.make_async_remote_copy(src_ref=rows, dst_ref=rows, send_sem=send_sems.at[k, sem],
                                            recv_sem=recv_sems.at[k, sem], device_id=to, device_id_type=MESH)

    def first(k):
        return [copy(k, 0, me, sibling)] + [copy(k, 1 + j, me, (*chip, c)) for j, chip in enumerate(chips)]

    def passed_on(k, j):
        return copy(k, 4 + j, slot(*chips[j], c), sibling)

    def start():
        for k in range(len(bufs)):
            for cp in first(k):
                cp.start()

    def forward():
        for k in range(len(bufs)):
            for j in range(3):
                copy(k, 1 + j, slot(*chips[j], c), sibling).wait_recv()
                passed_on(k, j).start()

    def finish():
        for k in range(len(bufs)):
            copy(k, 0, slot(x, y, 1 - c), sibling).wait_recv()
            for j, chip in enumerate(chips):
                copy(k, 4 + j, slot(*chip, 1 - c), sibling).wait_recv()
            for cp in first(k) + [passed_on(k, j) for j in range(3)]:
                cp.wait_send()

    return start, forward, finish


def pair_exchange(name, big, smalls, modes):
    nb, ns = len(big), len(smalls)
    rows = [a.shape[2] for a in big]
    total, width = sum(rows), big[0].shape[3]
    n = nb + ns

    def body(*refs):
        ins, outs = refs[:n], refs[n:n + 1 + ns]
        send_sems, recv_sems = refs[-2:]
        x, y, c = lax.axis_index("x"), lax.axis_index("y"), lax.axis_index("c")
        copies = []

        def remote(k, src, dst):
            copies.append(pltpu.make_async_remote_copy(src_ref=src, dst_ref=dst, send_sem=send_sems.at[k, 0],
                                                       recv_sem=recv_sems.at[k, 0], device_id=(x, y, 1 - c),
                                                       device_id_type=MESH))

        off = 0
        for k in range(nb):
            remote(k, ins[k].at[:, 1 - c], outs[0].at[:, pl.ds(off, rows[k]), :])
            off += rows[k]
        for j in range(ns):
            remote(nb + j, ins[nb + j].at[:, 1 - c] if modes[j] == "scatter" else ins[nb + j], outs[1 + j])
        for cp in copies:
            cp.start()
        for cp in copies:
            cp.wait()

    out_shape = [jax.ShapeDtypeStruct((4, total, width), big[0].dtype)]
    for a, m in zip(smalls, modes):
        out_shape.append(jax.ShapeDtypeStruct((4,) + tuple(a.shape[2:]) if m == "scatter" else tuple(a.shape), a.dtype))
    return pl.pallas_call(
        body, name=name, out_shape=out_shape, in_specs=[_ANY] * n, out_specs=[_ANY] * len(out_shape),
        scratch_shapes=_sems(n, 1),
    )(*big, *smalls)


def add2(name, a, b):
    shape = a.shape
    a2, b2 = a.reshape(-1, shape[-1]), b.reshape(-1, shape[-1])
    R, C = a2.shape
    tr = _row_tile(R, C, max_elems=1200 * 1024)

    def body(a_ref, b_ref, o_ref):
        o_ref[...] = (a_ref[...].astype(f32) + b_ref[...].astype(f32)).astype(o_ref.dtype)

    spec = pl.BlockSpec((tr, C), lambda i: (i, 0))
    return pl.pallas_call(
        body, name=name, grid=(R // tr,), in_specs=[spec, spec], out_specs=spec,
        out_shape=jax.ShapeDtypeStruct((R, C), a.dtype), compiler_params=_cparams(1),
    )(a2, b2).reshape(shape)


def chip_buffers(arrays, modes):
    mine = 2 * lax.axis_index("x") + lax.axis_index("y")
    return [own_slot_buffer(lax.dynamic_index_in_dim(a, mine, 0, keepdims=False) if m == "scatter" else a, 4, mine)
            for a, m in zip(arrays, modes)]


def _chip_copies(ins, bufs, modes, send_sems, recv_sems):
    x, y, c = lax.axis_index("x"), lax.axis_index("y"), lax.axis_index("c")
    mine = 2 * x + y
    copies = []
    for k in range(len(ins)):
        for r in range(1, 4):
            px = 1 - x if r & 2 else x
            py = 1 - y if r & 1 else y
            src = ins[k].at[2 * px + py] if modes[k] == "scatter" else ins[k]
            copies.append(pltpu.make_async_remote_copy(
                src_ref=src, dst_ref=bufs[k].at[mine], send_sem=send_sems.at[k, r - 1], recv_sem=recv_sems.at[k, r - 1],
                device_id=(px, py, c), device_id_type=MESH))
    return copies


def chip_exchange(name, arrays, modes):
    n = len(arrays)

    def body(*refs):
        copies = _chip_copies(refs[:n], refs[2 * n:3 * n], modes, *refs[3 * n:])
        for cp in copies:
            cp.start()
        for cp in copies:
            cp.wait()

    bufs = chip_buffers(arrays, modes)
    return pl.pallas_call(
        body, name=name, out_shape=[jax.ShapeDtypeStruct(b.shape, b.dtype) for b in bufs],
        in_specs=[_ANY] * (2 * n), out_specs=[_ANY] * n, input_output_aliases={n + k: k for k in range(n)},
        scratch_shapes=_sems(n, 3),
    )(*arrays, *bufs)


def gather_buffers(blocks):
    me = 4 * lax.axis_index("x") + 2 * lax.axis_index("y") + lax.axis_index("c")
    return [own_slot_buffer(b, N_DEV, me) for b in blocks]


_HBM = pl.BlockSpec(memory_space=pltpu.HBM)
_SEM = pl.BlockSpec(memory_space=pltpu.SEMAPHORE)
_SPLIT_PARAMS = pltpu.CompilerParams(has_side_effects=pltpu.SideEffectType.DATAFLOW_SIDE_EFFECTING)


def _split_gather_copies(bufs, send_sems, recv_sems):
    x, y, c = lax.axis_index("x"), lax.axis_index("y"), lax.axis_index("c")
    me = 4 * x + 2 * y + c
    copies = []
    for k in range(len(bufs)):
        rows = bufs[k].at[me]
        for r in range(1, N_DEV):
            peer = (1 - x if r & 4 else x, 1 - y if r & 2 else y, 1 - c if r & 1 else c)
            copies.append(pltpu.make_async_remote_copy(
                src_ref=rows, dst_ref=rows, send_sem=send_sems.at[(N_DEV - 1) * k + r - 1],
                recv_sem=recv_sems.at[(N_DEV - 1) * k + r - 1], device_id=peer, device_id_type=MESH))
    return copies


def gather_start(name, blocks):
    n = len(blocks)
    bufs = gather_buffers(blocks)

    def body(*refs):
        for cp in _split_gather_copies(refs[n + 2:2 * n + 2], refs[n], refs[n + 1]):
            cp.start()
        refs[2 * n + 2][...] = jnp.zeros_like(refs[2 * n + 2])

    n_sem = n * (N_DEV - 1)
    res = pl.pallas_call(
        body, name=name,
        out_shape=(pltpu.SemaphoreType.DMA((n_sem,)), pltpu.SemaphoreType.DMA((n_sem,)),
                   *[pltpu.HBM(b.shape, b.dtype) for b in bufs], jax.ShapeDtypeStruct((8, 128), f32)),
        in_specs=[_HBM] * n, out_specs=(_SEM, _SEM, *[_HBM] * n, pl.BlockSpec(memory_space=pltpu.VMEM)),
        input_output_aliases={k: 2 + k for k in range(n)}, compiler_params=_SPLIT_PARAMS,
    )(*[pltpu.with_memory_space_constraint(b, pltpu.HBM) for b in bufs])
    return res[0], res[1], list(res[2:2 + n]), res[2 + n]


def gather_wait(name, send_sems, recv_sems, bufs, after):
    n = len(bufs)

    def body(*refs):
        for cp in _split_gather_copies(refs[:n], refs[n], refs[n + 1]):
            cp.wait_send()
            cp.wait_recv()

    return list(pl.pallas_call(
        body, name=name, out_shape=tuple(pltpu.HBM(b.shape, b.dtype) for b in bufs),
        in_specs=[_HBM] * n + [_SEM, _SEM, _ANY], out_specs=tuple([_HBM] * n),
        input_output_aliases={k: k for k in range(n)}, compiler_params=_SPLIT_PARAMS,
    )(*bufs, send_sems, recv_sems, after))


def gather2(name, arrays):
    n = len(arrays)

    def body(*refs):
        start, forward, finish = _two_level_gather(refs[n:2 * n], *refs[2 * n + 1:])
        start()
        forward()
        finish()
        refs[2 * n][...] = jnp.zeros_like(refs[2 * n])

    bufs = gather_buffers(arrays)
    res = pl.pallas_call(
        body, name=name, out_shape=[jax.ShapeDtypeStruct(b.shape, b.dtype) for b in bufs] + [jax.ShapeDtypeStruct((8, 128), f32)],
        in_specs=[_ANY] * n, out_specs=[_ANY] * n + [pl.BlockSpec(memory_space=pltpu.VMEM)],
        input_output_aliases={k: k for k in range(n)}, scratch_shapes=_sems(n, N_DEV - 1),
    )(*bufs)
    return list(res[:n]), res[n]


def assemble(name, slabs):
    _, K, Ns = slabs.shape
    g = N_DEV if Ns % 128 == 0 else 2

    def body(w_ref, o_ref):
        o_ref[...] = jnp.concatenate([w_ref[s] for s in range(g)], axis=1)

    return pl.pallas_call(
        body, name=name, grid=(N_DEV // g,),
        in_specs=[pl.BlockSpec((g, K, Ns), lambda m: (m, 0, 0))],
        out_specs=pl.BlockSpec((K, g * Ns), lambda m: (0, m)),
        out_shape=jax.ShapeDtypeStruct((K, N_DEV * Ns), slabs.dtype),
        compiler_params=_cparams(1),
    )(slabs)


def mm_nn(name, a, b, out_dtype, add=None, tm=1024, tn=512, tk=None):
    M, K = a.shape
    N = b.shape[1]
    tm, tn = min(tm, M), min(tn, N)
    tk = K if tk is None else tk
    nk = K // tk
    has_add = add is not None

    def body(a_ref, b_ref, *rest):
        o_ref = rest[1] if has_add else rest[0]
        part = jnp.dot(a_ref[...].astype(bf16), b_ref[...].astype(bf16), preferred_element_type=f32)
        if nk == 1:
            o_ref[...] = (part + rest[0][...] if has_add else part).astype(o_ref.dtype)
            return
        acc_ref = rest[-1]
        k = pl.program_id(2)

        @pl.when(k == 0)
        def _():
            acc_ref[...] = part + rest[0][...] if has_add else part

        @pl.when(k > 0)
        def _():
            acc_ref[...] += part

        @pl.when(k == nk - 1)
        def _():
            o_ref[...] = acc_ref[...].astype(o_ref.dtype)

    in_specs = [pl.BlockSpec((tm, tk), lambda i, j, k: (i, k)), pl.BlockSpec((tk, tn), lambda i, j, k: (k, j))]
    ops = [a, b]
    if has_add:
        in_specs.append(pl.BlockSpec((tm, tn), lambda i, j, k: (i, j)))
        ops.append(add)
    return pl.pallas_call(
        body, name=name, grid=(M // tm, N // tn, nk), in_specs=in_specs,
        out_specs=pl.BlockSpec((tm, tn), lambda i, j, k: (i, j)),
        out_shape=jax.ShapeDtypeStruct((M, N), out_dtype),
        scratch_shapes=[pltpu.VMEM((tm, tn), f32)] if nk > 1 else [], compiler_params=_cparams(3),
    )(*ops)


def concat_cols(name, pieces, T, tm=512):
    widths = [p.shape[1] for p in pieces]
    total = sum(widths)
    tm = min(tm, T)

    def body(*refs):
        o_ref = refs[-1]
        off = 0
        for r, w in zip(refs[:-1], widths):
            o_ref[:, off:off + w] = r[...]
            off += w

    return pl.pallas_call(
        body, name=name, grid=(T // tm,),
        in_specs=[pl.BlockSpec((tm, w), lambda i: (i, 0)) for w in widths],
        out_specs=pl.BlockSpec((tm, total), lambda i: (i, 0)),
        out_shape=jax.ShapeDtypeStruct((T, total), pieces[0].dtype), compiler_params=_cparams(1),
    )(*pieces)


def mm_nt(name, a, b, out_dtype, tm=1024, tko=None, tc=None):
    M, C = a.shape
    Ko = b.shape[0]
    tm = min(tm, M)
    tc = C if tc is None else min(tc, C)
    tko = Ko if tko is None else tko
    nc = C // tc

    def body(a_ref, b_ref, o_ref, *scratch):
        part = lax.dot_general(a_ref[...].astype(bf16), b_ref[...].astype(bf16), (((1,), (1,)), ((), ())),
                               preferred_element_type=f32)
        if nc == 1:
            o_ref[...] = part.astype(o_ref.dtype)
            return
        acc_ref = scratch[0]
        c = pl.program_id(2)

        @pl.when(c == 0)
        def _():
            acc_ref[...] = part

        @pl.when(c > 0)
        def _():
            acc_ref[...] += part

        @pl.when(c == nc - 1)
        def _():
            o_ref[...] = acc_ref[...].astype(o_ref.dtype)

    return pl.pallas_call(
        body, name=name, grid=(M // tm, Ko // tko, nc),
        in_specs=[pl.BlockSpec((tm, tc), lambda i, j, c: (i, c)), pl.BlockSpec((tko, tc), lambda i, j, c: (j, c))],
        out_specs=pl.BlockSpec((tm, tko), lambda i, j, c: (i, j)),
        out_shape=jax.ShapeDtypeStruct((M, Ko), out_dtype),
        scratch_shapes=[pltpu.VMEM((tm, tko), f32)] if nc > 1 else [], compiler_params=_cparams(3),
    )(a, b)


def mm_tn(name, a, b, n_slab=None, tk=512, tn=512):
    T, K = a.shape
    N = b.shape[1]
    tk = min(tk, K)
    if n_slab is None:
        tn = min(tn, N)

        def body(a_ref, b_ref, o_ref):
            o_ref[...] = lax.dot_general(a_ref[...].astype(bf16), b_ref[...].astype(bf16), (((0,), (0,)), ((), ())),
                                         preferred_element_type=f32).astype(o_ref.dtype)

        return pl.pallas_call(
            body, name=name, grid=(N // tn, K // tk),
            in_specs=[pl.BlockSpec((T, tk), lambda j, i: (0, i)), pl.BlockSpec((T, tn), lambda j, i: (0, j))],
            out_specs=pl.BlockSpec((tk, tn), lambda j, i: (i, j)),
            out_shape=jax.ShapeDtypeStruct((K, N), bf16), compiler_params=_cparams(2),
        )(a, b)

    Ns = n_slab
    g = N_DEV if Ns % 128 == 0 else 2
    tn = g * Ns

    def body(a_ref, b_ref, o_ref):
        val = lax.dot_general(a_ref[...].astype(bf16), b_ref[...].astype(bf16), (((0,), (0,)), ((), ())),
                              preferred_element_type=f32)
        for s in range(g):
            o_ref[s] = val[:, s * Ns:(s + 1) * Ns].astype(o_ref.dtype)

    return pl.pallas_call(
        body, name=name, grid=(K // tk, N_DEV // g),
        in_specs=[pl.BlockSpec((T, tk), lambda i, j: (0, i)), pl.BlockSpec((T, tn), lambda i, j: (0, j))],
        out_specs=pl.BlockSpec((g, tk, Ns), lambda i, j: (j, i, 0)),
        out_shape=jax.ShapeDtypeStruct((N_DEV, K, Ns), bf16), compiler_params=_cparams(2),
    )(a, b)


def _tile_spec(w, tm, cb):
    if callable(cb):
        return pl.BlockSpec((tm, w), lambda c, i: (i, cb(c)))
    return pl.BlockSpec((tm, w), lambda c, i: (i, cb))


def _param_spec(block, cb):
    nd = len(block)
    if cb is None:
        return pl.BlockSpec(block, lambda c, i: (0,) * nd)
    return pl.BlockSpec(block, lambda c, i: (0,) * (nd - 1) + (cb(c),))


def rowwise(name, fn, tiled, params, outs, T, tm, ncol=1):
    n_in = len(tiled) + len(params)
    n_t = len(tiled)

    def body(*refs):
        ins = [r[...].astype(f32) for r in refs[:n_t]] + [r[...] for r in refs[n_t:n_in]]
        res = fn(*ins)
        for r, o in zip(refs[n_in:], res):
            r[...] = o.astype(r.dtype)

    return pl.pallas_call(
        body, name=name, grid=(ncol, T // tm),
        in_specs=[_tile_spec(w, tm, cb) for _, w, cb in tiled] + [_param_spec(blk, cb) for _, blk, cb in params],
        out_specs=[_tile_spec(w, tm, cb) for _, w, cb, _ in outs],
        out_shape=[jax.ShapeDtypeStruct((T, cols), dt) for cols, _, _, dt in outs],
        compiler_params=_cparams(2),
    )(*[a for a, _, _ in tiled], *[a for a, _, _ in params])


def rowwise_bwd(name, fn, tiled, params, cts, grads, T, tm, ncol=1, adds=None):
    n_t, n_p, n_c = len(tiled), len(params), len(cts)
    adds = adds or [None] * n_t
    add_list = [(k, a) for k, a in enumerate(adds) if a is not None]
    want = [k for k, g in enumerate(grads) if g is not None]
    n_a = len(add_list)

    def body(*refs):
        pos = 0
        t_refs = refs[pos:pos + n_t]; pos += n_t
        p_refs = refs[pos:pos + n_p]; pos += n_p
        c_refs = refs[pos:pos + n_c]; pos += n_c
        a_refs = refs[pos:pos + n_a]; pos += n_a
        g_refs = refs[pos:pos + len(want)]; pos += len(want)
        pg_refs = refs[pos:pos + n_p]
        primals = [r[...].astype(f32) for r in t_refs] + [r[...] for r in p_refs]
        _, vjp = jax.vjp(fn, *primals)
        g = vjp(tuple(r[...].astype(f32) for r in c_refs))
        add_of = {k: a_refs[n][...] for n, (k, _) in enumerate(add_list)}
        for n, k in enumerate(want):
            val = g[k]
            if k in add_of:
                val = val + add_of[k]
            g_refs[n][...] = val.astype(g_refs[n].dtype)
        i = pl.program_id(1)
        for k in range(n_p):
            @pl.when(i == 0)
            def _(k=k):
                pg_refs[k][...] = g[n_t + k]

            @pl.when(i > 0)
            def _(k=k):
                pg_refs[k][...] += g[n_t + k]

    in_specs = ([_tile_spec(w, tm, cb) for _, w, cb in tiled] + [_param_spec(blk, cb) for _, blk, cb in params]
                + [_tile_spec(w, tm, cb) for _, w, cb in cts] + [_tile_spec(w, tm, cb) for _, (_, w, cb) in add_list])
    ops = ([a for a, _, _ in tiled] + [a for a, _, _ in params] + [a for a, _, _ in cts]
           + [a for _, (a, _, _) in add_list])
    out_specs, out_shape = [], []
    for k in want:
        w = tiled[k][1]
        out_specs.append(_tile_spec(w, tm, lambda c: c))
        out_shape.append(jax.ShapeDtypeStruct((T, ncol * w), grads[k]))
    for a, blk, cb in params:
        out_specs.append(_param_spec(blk, cb))
        out_shape.append(jax.ShapeDtypeStruct(a.shape, f32))
    return pl.pallas_call(
        body, name=name, grid=(ncol, T // tm), in_specs=in_specs, out_specs=out_specs, out_shape=out_shape,
        compiler_params=_cparams(2),
    )(*ops)


@jax.custom_vjp
def _bdot(a, b):
    return jnp.dot(a.astype(bf16), b.astype(bf16), preferred_element_type=f32)


def _bdot_fwd(a, b):
    return _bdot(a, b), (a, b)


def _bdot_bwd(res, ct):
    a, b = res
    ctb = ct.astype(bf16)
    da = lax.dot_general(ctb, b.astype(bf16), (((1,), (1,)), ((), ())), preferred_element_type=f32)
    db = lax.dot_general(a.astype(bf16), ctb, (((0,), (0,)), ((), ())), preferred_element_type=f32)
    return da, db


_bdot.defvjp(_bdot_fwd, _bdot_bwd)


def _layer_norm(x, g, b):
    mu = jnp.mean(x, axis=-1, keepdims=True)
    xc = x - mu
    y = xc * lax.rsqrt(jnp.mean(xc * xc, axis=-1, keepdims=True) + EPS)
    return y * g + b


def f_rms(x, g):
    y = x * lax.rsqrt(jnp.mean(x * x, axis=-1, keepdims=True) + EPS)
    return (y * g,)


def f_sgu(zu, zv, ln_g, ln_b, wm, sgb_t):
    u = jax.nn.gelu(zu)
    vn = _layer_norm(jax.nn.gelu(zv), ln_g, ln_b)
    row = lax.broadcasted_iota(jnp.int32, (CHUNK, CHUNK), 0)
    col = lax.broadcasted_iota(jnp.int32, (CHUNK, CHUNK), 1)
    tril = col <= row
    low = col < HEAD_DIM
    parts = []
    for p in range(4):
        vp = vn[:, CHUNK * p:CHUNK * (p + 1)]
        w0 = jnp.where(tril, wm[2 * p], 0.0)
        w1 = jnp.where(tril, wm[2 * p + 1], 0.0)
        parts.append(_bdot(w0, jnp.where(low, vp, 0.0)) + _bdot(w1, jnp.where(low, 0.0, vp)))
    s = jnp.concatenate(parts, axis=1)
    lane_g = lax.shift_right_logical(lax.broadcasted_iota(jnp.int32, s.shape, 1), 6)
    bias = jnp.zeros_like(s)
    for g in range(8):
        bias = jnp.where(lane_g == g, sgb_t[:, g:g + 1], bias)
    return (u * (s + bias),)


def f_glu(p, gl):
    return (p * jax.nn.sigmoid(gl),)


def f_lnsilu(c1, g, b):
    return (jax.nn.silu(_layer_norm(c1, g, b)),)


def _head_norm(x, g64):
    g = jnp.concatenate([g64] * N_HEADS, axis=1)
    lane_h = lax.shift_right_logical(lax.broadcasted_iota(jnp.int32, x.shape, 1), 6)
    x2 = x * x
    r = jnp.zeros_like(x)
    for h in range(N_HEADS):
        mh = lane_h == h
        ms = jnp.sum(jnp.where(mh, x2, 0.0), axis=-1, keepdims=True) * (1.0 / HEAD_DIM)
        r = jnp.where(mh, lax.rsqrt(ms + EPS), r)
    return (x * r) * g


def f_qkv(zq, zk, zv, qg, kg):
    return (_head_norm(zq, qg) * 0.125, _head_norm(zk, kg), zv)


def f_merge(g0, g1, g2, ya, yb, yc, b0, b1, b2):
    return (jax.nn.sigmoid(g0 + b0) * ya + jax.nn.sigmoid(g1 + b1) * yb + jax.nn.sigmoid(g2 + b2) * yc,)


HALO = 32


def conv31_fwd(x, w, b, T, tt=256):
    C = x.shape[1]
    r = tt // HALO

    def body(x_ref, h_ref, w_ref, b_ref, y_ref, buf):
        i = pl.program_id(0)
        halo = h_ref[...]
        buf[0:HALO, :] = jnp.where(i > 0, halo, jnp.zeros_like(halo))
        buf[HALO:HALO + tt, :] = x_ref[...]
        acc = jnp.zeros((tt, C), f32) + b_ref[...]
        for k in range(CV_KERNEL):
            acc = acc + w_ref[k:k + 1, :] * buf[pl.ds(HALO - (CV_KERNEL - 1) + k, tt), :]
        y_ref[...] = acc

    return pl.pallas_call(
        body, name="conv31_fwd", grid=(T // tt,),
        in_specs=[pl.BlockSpec((tt, C), lambda i: (i, 0)),
                  pl.BlockSpec((HALO, C), lambda i: (jnp.maximum(i * r - 1, 0), 0)),
                  pl.BlockSpec((CV_KERNEL, C), lambda i: (0, 0)), pl.BlockSpec((1, C), lambda i: (0, 0))],
        out_specs=pl.BlockSpec((tt, C), lambda i: (i, 0)),
        out_shape=jax.ShapeDtypeStruct((T, C), f32),
        scratch_shapes=[pltpu.VMEM((HALO + tt, C), f32)], compiler_params=_cparams(1),
    )(x, x, w, b)


def conv31_bwd(x, w, dy, T, tt=256):
    C = x.shape[1]
    r = tt // HALO
    n = T // tt

    def body(x_ref, h_ref, w_ref, dy_ref, dyn_ref, dx_ref, dw_ref, db_ref, xbuf, dbuf):
        i = pl.program_id(0)
        halo = h_ref[...]
        xbuf[0:HALO, :] = jnp.where(i > 0, halo, jnp.zeros_like(halo))
        xbuf[HALO:HALO + tt, :] = x_ref[...]
        nxt = dyn_ref[...]
        dy = dy_ref[...]
        dbuf[0:tt, :] = dy
        dbuf[tt:tt + HALO, :] = jnp.where(i < n - 1, nxt, jnp.zeros_like(nxt))

        @pl.when(i == 0)
        def _():
            dw_ref[...] = jnp.zeros_like(dw_ref)
            db_ref[...] = jnp.zeros_like(db_ref)

        acc = jnp.zeros((tt, C), f32)
        for k in range(CV_KERNEL):
            acc = acc + w_ref[k:k + 1, :] * dbuf[pl.ds(CV_KERNEL - 1 - k, tt), :]
            xs = xbuf[pl.ds(HALO - (CV_KERNEL - 1) + k, tt), :]
            dw_ref[k:k + 1, :] += jnp.sum(dy * xs, axis=0, keepdims=True)
        dx_ref[...] = acc
        db_ref[...] += jnp.sum(dy, axis=0, keepdims=True)

    return pl.pallas_call(
        body, name="conv31_bwd", grid=(n,),
        in_specs=[pl.BlockSpec((tt, C), lambda i: (i, 0)),
                  pl.BlockSpec((HALO, C), lambda i: (jnp.maximum(i * r - 1, 0), 0)),
                  pl.BlockSpec((CV_KERNEL, C), lambda i: (0, 0)),
                  pl.BlockSpec((tt, C), lambda i: (i, 0)),
                  pl.BlockSpec((HALO, C), lambda i: (jnp.minimum((i + 1) * r, n * r - 1), 0))],
        out_specs=[pl.BlockSpec((tt, C), lambda i: (i, 0)), pl.BlockSpec((CV_KERNEL, C), lambda i: (0, 0)),
                   pl.BlockSpec((1, C), lambda i: (0, 0))],
        out_shape=[jax.ShapeDtypeStruct((T, C), f32), jax.ShapeDtypeStruct((CV_KERNEL, C), f32),
                   jax.ShapeDtypeStruct((1, C), f32)],
        scratch_shapes=[pltpu.VMEM((HALO + tt, C), f32), pltpu.VMEM((HALO + tt, C), f32)], compiler_params=_cparams(1),
    )(x, x, w, dy, dy)


FFN_TC = 128
FFN_PAD = 8


def ffn_act_fwd(up, cw, cb, T):
    F = up.shape[1] // 2
    nj = F // FFN_TC
    rc = min(128, T)

    def body(g_ref, v_ref, g0, g1, g2, gb, v0, v1, v2, vb, o_ref, gp, vp):
        zeros = jnp.zeros((FFN_PAD, FFN_TC), f32)
        for p, x_ref in ((gp, g_ref), (vp, v_ref)):
            p[0:FFN_PAD, :] = zeros
            p[FFN_PAD:FFN_PAD + T, :] = x_ref[...].astype(f32)
        wg = (g0[...], g1[...], g2[...], gb[...])
        wv = (v0[...], v1[...], v2[...], vb[...])

        def conv(p, w, r):
            return (w[0] * p[pl.ds(FFN_PAD + r - 2, rc), :] + w[1] * p[pl.ds(FFN_PAD + r - 1, rc), :]
                    + w[2] * p[pl.ds(FFN_PAD + r, rc), :] + w[3])

        for r in range(0, T, rc):
            o_ref[pl.ds(r, rc), :] = (jax.nn.silu(conv(gp, wg, r)) * conv(vp, wv, r)).astype(o_ref.dtype)

    gspec = pl.BlockSpec((T, FFN_TC), lambda j: (0, j))
    vspec = pl.BlockSpec((T, FFN_TC), lambda j: (0, j + nj))
    pg = pl.BlockSpec((1, FFN_TC), lambda j: (0, j))
    pv = pl.BlockSpec((1, FFN_TC), lambda j: (0, j + nj))
    return pl.pallas_call(
        body, name="ffn_act_fwd", grid=(nj,),
        in_specs=[gspec, vspec, pg, pg, pg, pg, pv, pv, pv, pv], out_specs=gspec,
        out_shape=jax.ShapeDtypeStruct((T, F), bf16),
        scratch_shapes=[pltpu.VMEM((FFN_PAD + T, FFN_TC), f32)] * 2, compiler_params=_cparams(1),
    )(up, up, cw[0], cw[1], cw[2], cb, cw[0], cw[1], cw[2], cb)


def ffn_act_bwd(up, cw, cb, dact, T):
    F = up.shape[1] // 2
    nj = F // FFN_TC

    rc = min(128, T)
    ext = rc + FFN_PAD

    def body(g_ref, v_ref, g0, g1, g2, gb, v0, v1, v2, vb, d_ref, dupg_ref, dupv_ref, *rest):
        p_refs, (gp, vp, dp, dgs, dvs) = rest[:8], rest[8:]
        zeros = jnp.zeros((FFN_PAD, FFN_TC), f32)
        for p, x_ref in ((gp, g_ref), (vp, v_ref)):
            p[0:FFN_PAD, :] = zeros
            p[FFN_PAD:FFN_PAD + T, :] = x_ref[...].astype(f32)
            p[FFN_PAD + T:FFN_PAD + T + FFN_PAD, :] = zeros
        dp[0:T, :] = d_ref[...]
        dp[T:T + FFN_PAD, :] = zeros
        wg = (g0[...], g1[...], g2[...], gb[...])
        wv = (v0[...], v1[...], v2[...], vb[...])
        acc = [jnp.zeros((1, FFN_TC), f32) for _ in range(8)]

        def taps(p, r):
            return tuple(p[pl.ds(FFN_PAD + r - s, ext), :] for s in (2, 1, 0))

        for r in range(0, T, rc):
            xg, xv = taps(gp, r), taps(vp, r)
            gc = wg[0] * xg[0] + wg[1] * xg[1] + wg[2] * xg[2] + wg[3]
            vc = wv[0] * xv[0] + wv[1] * xv[1] + wv[2] * xv[2] + wv[3]
            d = dp[pl.ds(r, ext), :]
            sg = jax.nn.sigmoid(gc)
            sides = ((d * vc * (sg * (1.0 + gc * (1.0 - sg))), xg, wg, dgs, dupg_ref), (d * (gc * sg), xv, wv, dvs, dupv_ref))
            for side, (dc, x, w, buf, dup_ref) in enumerate(sides):
                buf[...] = dc
                dc0 = dc[0:rc]
                dup = w[2] * dc0 + w[1] * buf[pl.ds(1, rc), :] + w[0] * buf[pl.ds(2, rc), :]
                dup_ref[pl.ds(r, rc), :] = dup.astype(dup_ref.dtype)
                for k in range(3):
                    acc[4 * side + k] = acc[4 * side + k] + jnp.sum(dc0 * x[k][0:rc], axis=0, keepdims=True)
                acc[4 * side + 3] = acc[4 * side + 3] + jnp.sum(dc0, axis=0, keepdims=True)
        for k in range(8):
            p_refs[k][...] = acc[k]

    gspec = pl.BlockSpec((T, FFN_TC), lambda j: (0, j))
    vspec = pl.BlockSpec((T, FFN_TC), lambda j: (0, j + nj))
    pg = pl.BlockSpec((1, FFN_TC), lambda j: (0, j))
    pv = pl.BlockSpec((1, FFN_TC), lambda j: (0, j + nj))
    res = pl.pallas_call(
        body, name="ffn_act_bwd", grid=(nj,),
        in_specs=[gspec, vspec, pg, pg, pg, pg, pv, pv, pv, pv, gspec],
        out_specs=[gspec, gspec] + [pg] * 8,
        out_shape=[jax.ShapeDtypeStruct((T, F), bf16)] * 2 + [jax.ShapeDtypeStruct((1, F), f32)] * 8,
        scratch_shapes=[pltpu.VMEM((FFN_PAD + T + FFN_PAD, FFN_TC), f32)] * 2 + [pltpu.VMEM((T + FFN_PAD, FFN_TC), f32)]
        + [pltpu.VMEM((ext, FFN_TC), f32)] * 2,
        compiler_params=_cparams(1),
    )(up, up, cw[0], cw[1], cw[2], cb, cw[0], cw[1], cw[2], cb, dact)
    dup = concat_cols("dup_concat", [res[0], res[1]], T)
    return (dup,) + tuple(jnp.concatenate([res[2 + k], res[6 + k]], axis=1) for k in range(4))


BQ = 256
BK = 256
assert BQ % BK == 0
NT_DIMS = (((1,), (1,)), ((), ()))
TN_DIMS = (((0,), (0,)), ((), ()))


def _split_dot(x, u):
    x1 = x.astype(bf16)
    x2 = (x - x1.astype(f32)).astype(bf16)
    n = x.shape[0]
    y = jnp.dot(jnp.concatenate([x1, x2], axis=0), u, preferred_element_type=f32)
    return y[0:n] + y[n:2 * n]


def _log_sigmoids(z):
    sp = jnp.log(1.0 + jnp.exp(-jnp.abs(z)))
    lsp = jnp.minimum(z, 0.0) - sp
    return lsp, lsp - z


def _stack_heads(x):
    head1 = lax.broadcasted_iota(jnp.int32, x.shape, 1) >= HEAD_DIM
    zero = jnp.zeros_like(x)
    return jnp.concatenate([jnp.where(head1, zero, x), jnp.where(head1, x, zero)], axis=0)


def _unstack_heads(y):
    head1 = lax.broadcasted_iota(jnp.int32, (BQ, y.shape[1]), 1) >= HEAD_DIM
    return jnp.where(head1, y[BQ:2 * BQ], y[0:BQ])


def last_block(i):
    return ((i + 1) * BQ - 1) // BK


def _attn_masks():
    row = lax.broadcasted_iota(jnp.int32, (2 * BQ, BK), 0)
    col = lax.broadcasted_iota(jnp.int32, (2 * BQ, BK), 1)
    ur = lax.broadcasted_iota(jnp.int32, (BK, BK), 0)
    uc = lax.broadcasted_iota(jnp.int32, (BK, BK), 1)
    return (row & (BQ - 1)) - col, (ur > uc).astype(bf16), (ur < uc).astype(bf16)


def attn_fwd(q, k, v, T, gather=None):
    nq = T // BQ
    n_g = 0 if gather is None else len(gather)

    def body(*refs):
        q_ref, k_ref, v_ref = refs[:3]
        o_ref, tot_ref = refs[3 + n_g:5 + n_g]
        p, i = pl.program_id(0), pl.program_id(1)
        if n_g:
            start, forward, finish = _two_level_gather(refs[5 + n_g:5 + 2 * n_g], *refs[5 + 2 * n_g:])
            pl.when(jnp.logical_and(p == 0, i == 0))(start)
        qs = _stack_heads(q_ref[...])
        diff, u_after, _ = _attn_masks()

        def step(t, carry):
            acc, c = carry
            jb = last_block(i) - t
            ks = pl.multiple_of(jb * BK, BK)
            kb = k_ref[pl.ds(ks, BK), :]
            vb = v_ref[pl.ds(ks, BK), :]
            m = diff > jb * BK - i * BQ
            z = lax.dot_general(qs, kb, NT_DIMS, preferred_element_type=f32)
            lsp, lsn = _log_sigmoids(z)
            lm = jnp.where(m, lsn, 0.0)
            a = jnp.where(m, jnp.exp(lsp + _split_dot(lm, u_after)), 0.0)
            acc = acc + jnp.exp(c) * jnp.dot(a.astype(bf16), vb, preferred_element_type=f32)
            return acc, c + jnp.sum(lm, axis=-1, keepdims=True)

        acc, c = lax.fori_loop(0, last_block(i) + 1, step, (jnp.zeros((2 * BQ, 128), f32), jnp.zeros((2 * BQ, 1), f32)))
        o_ref[...] = _unstack_heads(acc).astype(o_ref.dtype)
        tot_ref[...] = _unstack_heads(jnp.broadcast_to(c, (2 * BQ, 128)))
        if n_g:
            @pl.when(jnp.logical_and(p == 3, i == nq - 1))
            def _():
                forward()
                finish()

    blk = pl.BlockSpec((BQ, 128), lambda p, i: (i, p))
    full = pl.BlockSpec((T, 128), lambda p, i: (0, p))
    bufs = [] if gather is None else gather_buffers(gather)
    res = pl.pallas_call(
        body, name="attn_fwd_gather" if n_g else "attn_fwd", grid=(4, nq),
        in_specs=[blk, full, full] + [_ANY] * n_g, out_specs=[blk, blk] + [_ANY] * n_g,
        out_shape=[jax.ShapeDtypeStruct((T, 512), bf16), jax.ShapeDtypeStruct((T, 512), f32)]
        + [jax.ShapeDtypeStruct(b.shape, b.dtype) for b in bufs],
        input_output_aliases={3 + k: 2 + k for k in range(n_g)},
        scratch_shapes=_sems(n_g, N_DEV - 1) if n_g else [], compiler_params=_cparams(2),
    )(q, k, v, *bufs)
    return res[0], res[1], list(res[2:])


def attn_bwd(q, k, v, do, tot, T, scatter=None):
    nq = T // BQ
    n_s = 0 if scatter is None else len(scatter[0])
    modes = [] if scatter is None else list(scatter[1])

    def body(*refs):
        q_ref, k_ref, v_ref, do_ref, tot_ref = refs[:5]
        dq_ref, dk_ref, dv_ref = refs[5 + 2 * n_s:8 + 2 * n_s]
        p, i = pl.program_id(0), pl.program_id(1)
        if n_s:
            def copies():
                return _chip_copies(refs[5:5 + n_s], refs[8 + 2 * n_s:8 + 3 * n_s], modes, *refs[8 + 3 * n_s:])

            @pl.when(jnp.logical_and(p == 0, i == 0))
            def _():
                for cp in copies():
                    cp.start()

        @pl.when(i == 0)
        def _():
            dk_ref[...] = jnp.zeros_like(dk_ref)
            dv_ref[...] = jnp.zeros_like(dv_ref)

        qs = _stack_heads(q_ref[...])
        dos = _stack_heads(do_ref[...].astype(bf16))
        totv = tot_ref[...]
        tots = jnp.concatenate([totv[:, 0:1], totv[:, HEAD_DIM:HEAD_DIM + 1]], axis=0)
        diff, u_after, u_before = _attn_masks()

        def step(jb, carry):
            dq, cl, cg = carry
            ks = pl.multiple_of(jb * BK, BK)
            kb = k_ref[pl.ds(ks, BK), :]
            vb = v_ref[pl.ds(ks, BK), :]
            m = diff > jb * BK - i * BQ
            z = lax.dot_general(qs, kb, NT_DIMS, preferred_element_type=f32)
            lsp, lsn = _log_sigmoids(z)
            lm = jnp.where(m, lsn, 0.0)
            a = jnp.where(m, jnp.exp(lsp + _split_dot(lm, u_after)), 0.0)
            g = a * lax.dot_general(dos, vb, NT_DIMS, preferred_element_type=f32)
            beta = jnp.exp(lsp)
            aa = jnp.where(m, g * jnp.exp(lsn) - _split_dot(g, u_before) * beta, 0.0)
            bb = jnp.where(m, beta, 0.0)
            cl = cl + jnp.sum(lm, axis=-1, keepdims=True)
            f = jnp.exp(tots - cl)
            dz = (f * aa - cg * bb).astype(bf16)
            cg = cg + f * jnp.sum(g, axis=-1, keepdims=True)
            dq = dq + jnp.dot(dz, kb, preferred_element_type=f32)
            dk_ref[pl.ds(ks, BK), :] += lax.dot_general(dz, qs, TN_DIMS, preferred_element_type=f32)
            dv_ref[pl.ds(ks, BK), :] += lax.dot_general((f * a).astype(bf16), dos, TN_DIMS, preferred_element_type=f32)
            return dq, cl, cg

        zc = jnp.zeros((2 * BQ, 1), f32)
        dq, _, _ = lax.fori_loop(0, last_block(i) + 1, step, (jnp.zeros((2 * BQ, 128), f32), zc, zc))
        dq_ref[...] = _unstack_heads(dq)
        if n_s:
            @pl.when(jnp.logical_and(p == 3, i == nq - 1))
            def _():
                for cp in copies():
                    cp.wait()

    blk = pl.BlockSpec((BQ, 128), lambda p, i: (i, p))
    full = pl.BlockSpec((T, 128), lambda p, i: (0, p))
    shp = jax.ShapeDtypeStruct((T, 512), f32)
    s_list = [] if scatter is None else list(scatter[0])
    bufs = chip_buffers(s_list, modes) if n_s else []
    res = pl.pallas_call(
        body, name="attn_bwd_scatter" if n_s else "attn_bwd", grid=(4, nq),
        in_specs=[blk, full, full, blk, blk] + [_ANY] * (2 * n_s), out_specs=[blk, full, full] + [_ANY] * n_s,
        out_shape=[shp, shp, shp] + [jax.ShapeDtypeStruct(b.shape, b.dtype) for b in bufs],
        input_output_aliases={5 + n_s + k: 3 + k for k in range(n_s)},
        scratch_shapes=_sems(n_s, 3) if n_s else [], compiler_params=_cparams(2),
    )(q, k, v, do, tot, *s_list, *bufs)
    return res[0], res[1], res[2], list(res[3:])


def loss_head(y, target, T, tm=256):
    D = y.shape[1]

    def body(y_ref, t_ref, dy_ref, l_ref):
        i = pl.program_id(0)
        err = y_ref[...] - t_ref[...]
        dy_ref[...] = err * (1.0 / D)
        part = 0.5 * jnp.sum(jnp.sum(err * err, axis=-1, keepdims=True) * (1.0 / D), axis=0, keepdims=True)

        @pl.when(i == 0)
        def _():
            l_ref[...] = jnp.zeros_like(l_ref)

        l_ref[...] += jnp.broadcast_to(part, l_ref.shape)

    spec = pl.BlockSpec((tm, D), lambda i: (i, 0))
    return pl.pallas_call(
        body, name="loss_head", grid=(T // tm,), in_specs=[spec, spec],
        out_specs=[spec, pl.BlockSpec((1, 128), lambda i: (0, 0))],
        out_shape=[jax.ShapeDtypeStruct((T, D), f32), jax.ShapeDtypeStruct((1, 128), f32)],
        compiler_params=_cparams(1),
    )(y, target)


def _row_tile(rows, cols, offset=0, max_elems=128 * 1024):
    best = None
    for t in range(16, rows + 1, 16):
        if rows % t == 0 and offset % t == 0 and t * cols <= max_elems:
            best = t
    return best if best is not None else rows


def adamw(name, w, m, v, parts, layer, prev=None, row_off=0):
    L, R, C = w.shape
    n_parts = parts.shape[0]
    tr = _row_tile(R, C, row_off, max_elems=256 * 1024)
    assert R % tr == 0 and row_off % tr == 0, (name, R, row_off, tr)

    def body(w_ref, m_ref, v_ref, p_ref, *rest):
        g_ref, d_ref, nm_ref, nv_ref = rest[-4:]
        g = p_ref[0].astype(f32)
        for s in range(1, n_parts):
            g = g + p_ref[s].astype(f32)
        wv = w_ref[...]
        mn = ADAM_B1 * m_ref[...] + (1.0 - ADAM_B1) * g
        vn = ADAM_B2 * v_ref[...] + (1.0 - ADAM_B2) * jnp.square(g)
        m_hat = mn / (1.0 - ADAM_B1 ** ADAM_STEP)
        v_hat = vn / (1.0 - ADAM_B2 ** ADAM_STEP)
        g_ref[...] = g
        d_ref[...] = -ADAM_LR * (m_hat / (jnp.sqrt(v_hat) + ADAM_EPS) + ADAM_WD * wv)
        nm_ref[...] = mn
        nv_ref[...] = vn

    spec = pl.BlockSpec((None, tr, C), lambda i: (layer, i, 0))
    shp = jax.ShapeDtypeStruct((L, R, C), f32)
    n_prev = 0 if prev is None else 4
    return pl.pallas_call(
        body, name=name, grid=(R // tr,),
        in_specs=[spec, spec, spec, pl.BlockSpec((n_parts, tr, C), lambda i: (0, row_off // tr + i, 0))] + [_ANY] * n_prev,
        out_specs=[spec] * 4, out_shape=[shp] * 4, input_output_aliases={4 + k: k for k in range(n_prev)},
        compiler_params=_cparams(1),
    )(w, m, v, parts, *(prev or ()))


REPL = ["ln1_g", "sg_ln_g", "sg_ln_b", "sg_w", "sg_b", "cv_b", "cv_ln_g", "cv_ln_b", "q_norm_g", "k_norm_g", "ln2_g",
        "ffn_conv_b"]
SMALL_SHARDED = ["b_gate", "cv_w", "ffn_conv_w"]
BIG = ["w_in", "w_a_out", "w_b_out", "w_c_out", "w_up", "w_out", "w_down"]
COL_SHARDED = ["w_a_out", "w_b_out", "w_c_out"]
TRANSPOSED = ["w_in", "w_up"]


def _pack(arrs, rows):
    flat = jnp.concatenate([a.reshape(-1) for a in arrs])
    return jnp.pad(flat, (0, rows * 128 - flat.shape[0])).reshape(rows, 128)


def _pack_layers(arrs, rows):
    flat = jnp.concatenate([a.reshape(a.shape[0], -1) for a in arrs], axis=1)
    return jnp.pad(flat, ((0, 0), (0, rows * 128 - flat.shape[1]))).reshape(flat.shape[0], rows, 128)


def _unpack_layers(packed, shapes):
    flat = packed.reshape(packed.shape[0], -1)
    out, pos = [], 0
    for s in shapes:
        n = 1
        for d in s[1:]:
            n *= d
        out.append(flat[:, pos:pos + n].reshape(s))
        pos += n
    return out


def _rows_for(shapes, mult):
    n = 0
    for s in shapes:
        k = 1
        for d in s:
            k *= d
        n += k
    rows = -(-n // 128)
    return -(-rows // mult) * mult


W_NAMES = ['ln1_g', 'w_in', 'b_gate', 'sg_ln_g', 'sg_ln_b', 'sg_w', 'sg_b', 'w_a_out', 'cv_w', 'cv_b', 'cv_ln_g', 'cv_ln_b',
           'w_b_out', 'q_norm_g', 'k_norm_g', 'w_c_out', 'w_out', 'ln2_g', 'w_up', 'ffn_conv_w', 'ffn_conv_b', 'w_down']


def _forward_layer(x, P, T, late=None):
    D = x.shape[1]
    sv = {"x0": x}
    (h1,) = rowwise("rms_fwd", f_rms, [(x, D, 0)], [(P["ln1_g"], (1, D), None)], [(D, D, 0, bf16)], T, 256)
    z = mm_nt("in_proj", h1, P["w_in_t"], bf16, tko=1664, tc=D)
    sv["h1"], sv["z"] = h1, z
    (ya_in,) = rowwise("sgu_fwd", f_sgu, [(z, 512, 0), (z, 512, 1)],
                       [(P["sg_ln_g"], (1, 512), None), (P["sg_ln_b"], (1, 512), None),
                        (P["sg_w"], (8, CHUNK, CHUNK), None), (P["sg_b_t"], (CHUNK, 8), None)],
                       [(512, 512, 0, bf16)], T, CHUNK)
    (c0,) = rowwise("glu_fwd", f_glu, [(z, 512, 2), (z, 512, 3)], [], [(512, 512, 0, f32)], T, 256)
    c1 = conv31_fwd(c0, P["cv_w"], P["cv_b"], T)
    (c3,) = rowwise("lnsilu_fwd", f_lnsilu, [(c1, 512, 0)], [(P["cv_ln_g"], (1, 512), None), (P["cv_ln_b"], (1, 512), None)],
                    [(512, 512, 0, bf16)], T, 256)
    q8, kn, vb = rowwise("qkv_fwd", f_qkv, [(z, 512, 4), (z, 512, 5), (z, 512, 6)],
                         [(P["q_norm_g"], (1, HEAD_DIM), None), (P["k_norm_g"], (1, HEAD_DIM), None)],
                         [(512, 512, 0, bf16)] * 3, T, 256)
    o, tot, _ = attn_fwd(q8, kn, vb, T)
    if late is not None:
        P.update(late(o))
    ya = mm_nn("a_out", ya_in, P["w_a_out"], f32)
    yb = mm_nn("b_out", c3, P["w_b_out"], f32)
    yc = mm_nn("c_out", o, P["w_c_out"], f32)
    (merged,) = rowwise("merge_fwd", f_merge,
                        [(z, 512, lambda c: 7 + c), (z, 512, lambda c: 9 + c), (z, 512, lambda c: 11 + c),
                         (ya, 512, lambda c: c), (yb, 512, lambda c: c), (yc, 512, lambda c: c)],
                        [(P["b_gate"][k], (1, 512), lambda c: c) for k in range(3)],
                        [(D, 512, lambda c: c, bf16)], T, 256, ncol=2)
    x1 = mm_nn("out_proj", merged, P["w_out"], f32, add=x)
    sv.update(ya_in=ya_in, ya=ya, c0=c0, c1=c1, c3=c3, yb=yb, q8=q8, kn=kn, vb=vb, o=o, tot=tot, yc=yc, merged=merged, x1=x1)
    (h2,) = rowwise("rms_fwd", f_rms, [(x1, D, 0)], [(P["ln2_g"], (1, D), None)], [(D, D, 0, bf16)], T, 256)
    up = mm_nt("up_proj", h2, P["w_up_t"], bf16, tko=1408, tc=D)
    act = ffn_act_fwd(up, P["ffn_conv_w"], P["ffn_conv_b"], T)
    x2 = mm_nn("down_proj", act, P["w_down"], f32, add=x1)
    sv.update(h2=h2, up=up, act=act)
    return x2, sv


def _backward_layer(dx2, P, sv, T, scatter=None):
    D = dx2.shape[1]
    G = {}
    G["w_down"] = mm_tn("dw_down", sv["act"], dx2, tk=1408)
    dact = mm_nt("d_act", dx2, P["w_down"], f32, tko=1408, tc=D)
    dup, dcw0, dcw1, dcw2, G["ffn_conv_b"] = ffn_act_bwd(sv["up"], P["ffn_conv_w"], P["ffn_conv_b"], dact, T)
    G["ffn_conv_w"] = jnp.concatenate([dcw0, dcw1, dcw2], axis=0)
    G["w_up"] = mm_tn("dw_up", dup, sv["h2"], tk=1408)
    dh2 = mm_nn("d_h2", dup, P["w_up_t"], f32, tm=512)
    dx1, G["ln2_g"] = rowwise_bwd("rms_bwd", f_rms, [(sv["x1"], D, 0)], [(P["ln2_g"], (1, D), None)], [(dh2, D, 0)],
                                  [f32], T, 256, adds=[(dx2, D, 0)])
    G["w_out"] = mm_tn("dw_out", sv["merged"], dx1, tk=1024)
    dmerged = mm_nt("d_merged", dx1, P["w_out"], f32)
    z = sv["z"]
    dg0, dg1, dg2, dya, dyb, dyc, db0, db1, db2 = rowwise_bwd(
        "merge_bwd", f_merge,
        [(z, 512, lambda c: 7 + c), (z, 512, lambda c: 9 + c), (z, 512, lambda c: 11 + c),
         (sv["ya"], 512, lambda c: c), (sv["yb"], 512, lambda c: c), (sv["yc"], 512, lambda c: c)],
        [(P["b_gate"][k], (1, 512), lambda c: c) for k in range(3)],
        [(dmerged, 512, lambda c: c)], [bf16] * 6, T, 256, ncol=2)
    G["b_gate"] = jnp.concatenate([db0, db1, db2], axis=0)
    G["w_c_out"] = mm_tn("dw_c_out", sv["o"], dyc, n_slab=CHUNK)
    do = mm_nt("d_o", dyc, P["w_c_out"], f32)
    dq8, dkn, dvb, received = attn_bwd(sv["q8"], sv["kn"], sv["vb"], do, sv["tot"], T, scatter=scatter)
    dzq, dzk, dzv, G["q_norm_g"], G["k_norm_g"] = rowwise_bwd(
        "qkv_bwd", f_qkv, [(z, 512, 4), (z, 512, 5), (z, 512, 6)],
        [(P["q_norm_g"], (1, HEAD_DIM), None), (P["k_norm_g"], (1, HEAD_DIM), None)],
        [(dq8, 512, 0), (dkn, 512, 0), (dvb, 512, 0)], [bf16] * 3, T, 256)
    G["w_b_out"] = mm_tn("dw_b_out", sv["c3"], dyb, n_slab=CHUNK)
    dc3 = mm_nt("d_c3", dyb, P["w_b_out"], f32)
    dc1, G["cv_ln_g"], G["cv_ln_b"] = rowwise_bwd(
        "lnsilu_bwd", f_lnsilu, [(sv["c1"], 512, 0)], [(P["cv_ln_g"], (1, 512), None), (P["cv_ln_b"], (1, 512), None)],
        [(dc3, 512, 0)], [f32], T, 256)
    dc0, G["cv_w"], G["cv_b"] = conv31_bwd(sv["c0"], P["cv_w"], dc1, T)
    dzp, dzgl = rowwise_bwd("glu_bwd", f_glu, [(z, 512, 2), (z, 512, 3)], [], [(dc0, 512, 0)], [bf16] * 2, T, 256)
    G["w_a_out"] = mm_tn("dw_a_out", sv["ya_in"], dya, n_slab=CHUNK)
    dya_in = mm_nt("d_ya_in", dya, P["w_a_out"], f32)
    dzu, dzv_a, G["sg_ln_g"], G["sg_ln_b"], G["sg_w"], dsbt = rowwise_bwd(
        "sgu_bwd", f_sgu, [(z, 512, 0), (z, 512, 1)],
        [(P["sg_ln_g"], (1, 512), None), (P["sg_ln_b"], (1, 512), None), (P["sg_w"], (8, CHUNK, CHUNK), None),
         (P["sg_b_t"], (CHUNK, 8), None)],
        [(dya_in, 512, 0)], [bf16] * 2, T, CHUNK)
    G["sg_b"] = dsbt.T
    dz = concat_cols("dz_concat", [dzu, dzv_a, dzp, dzgl, dzq, dzk, dzv, dg0, dg1, dg2], T)
    G["w_in"] = mm_tn("dw_in", dz, sv["h1"], tk=1664)
    dh1 = mm_nn("d_h1", dz, P["w_in_t"], f32, tm=512)
    dx0, G["ln1_g"] = rowwise_bwd("rms_bwd", f_rms, [(sv["x0"], D, 0)], [(P["ln1_g"], (1, D), None)], [(dh1, D, 0)],
                                  [f32], T, 256, adds=[(dx1, D, 0)])
    return dx0, G, received


def kernel(x, ln1_g, w_in, b_gate, sg_ln_g, sg_ln_b, sg_w, sg_b, w_a_out, cv_w, cv_b, cv_ln_g, cv_ln_b, w_b_out, q_norm_g, k_norm_g, w_c_out, w_out, ln2_g, w_up, ffn_conv_w, ffn_conv_b, w_down, loss_target, m_ln1_g, m_w_in, m_b_gate, m_sg_ln_g, m_sg_ln_b, m_sg_w, m_sg_b, m_w_a_out, m_cv_w, m_cv_b, m_cv_ln_g, m_cv_ln_b, m_w_b_out, m_q_norm_g, m_k_norm_g, m_w_c_out, m_w_out, m_ln2_g, m_w_up, m_ffn_conv_w, m_ffn_conv_b, m_w_down, v_ln1_g, v_w_in, v_b_gate, v_sg_ln_g, v_sg_ln_b, v_sg_w, v_sg_b, v_w_a_out, v_cv_w, v_cv_b, v_cv_ln_g, v_cv_ln_b, v_w_b_out, v_q_norm_g, v_k_norm_g, v_w_c_out, v_w_out, v_ln2_g, v_w_up, v_ffn_conv_w, v_ffn_conv_b, v_w_down):
    W = dict(ln1_g=ln1_g, w_in=w_in, b_gate=b_gate, sg_ln_g=sg_ln_g, sg_ln_b=sg_ln_b, sg_w=sg_w, sg_b=sg_b, w_a_out=w_a_out,
             cv_w=cv_w, cv_b=cv_b, cv_ln_g=cv_ln_g, cv_ln_b=cv_ln_b, w_b_out=w_b_out, q_norm_g=q_norm_g, k_norm_g=k_norm_g,
             w_c_out=w_c_out, w_out=w_out, ln2_g=ln2_g, w_up=w_up, ffn_conv_w=ffn_conv_w, ffn_conv_b=ffn_conv_b, w_down=w_down)
    M = dict(ln1_g=m_ln1_g, w_in=m_w_in, b_gate=m_b_gate, sg_ln_g=m_sg_ln_g, sg_ln_b=m_sg_ln_b, sg_w=m_sg_w, sg_b=m_sg_b,
             w_a_out=m_w_a_out, cv_w=m_cv_w, cv_b=m_cv_b, cv_ln_g=m_cv_ln_g, cv_ln_b=m_cv_ln_b, w_b_out=m_w_b_out,
             q_norm_g=m_q_norm_g, k_norm_g=m_k_norm_g, w_c_out=m_w_c_out, w_out=m_w_out, ln2_g=m_ln2_g, w_up=m_w_up,
             ffn_conv_w=m_ffn_conv_w, ffn_conv_b=m_ffn_conv_b, w_down=m_w_down)
    V = dict(ln1_g=v_ln1_g, w_in=v_w_in, b_gate=v_b_gate, sg_ln_g=v_sg_ln_g, sg_ln_b=v_sg_ln_b, sg_w=v_sg_w, sg_b=v_sg_b,
             w_a_out=v_w_a_out, cv_w=v_cv_w, cv_b=v_cv_b, cv_ln_g=v_cv_ln_g, cv_ln_b=v_cv_ln_b, w_b_out=v_w_b_out,
             q_norm_g=v_q_norm_g, k_norm_g=v_k_norm_g, w_c_out=v_w_c_out, w_out=v_w_out, ln2_g=v_ln2_g, w_up=v_w_up,
             ffn_conv_w=v_ffn_conv_w, ffn_conv_b=v_ffn_conv_b, w_down=v_w_down)
    T, D = x.shape[1], x.shape[2]
    L = DEPTH
    xs = x.reshape(T, D)
    target = loss_target.reshape(T, D)

    ss_shapes = [W[n].shape for n in SMALL_SHARDED]
    ss_rows = _rows_for(ss_shapes, 8)
    (ss_all,), small_token = exchange("gather_small", [_pack([W[n] for n in SMALL_SHARDED], ss_rows)], ["gather"])
    full_small = {}
    pos = 0
    for n in SMALL_SHARDED:
        s = W[n].shape
        cnt = s[0] * s[1] * s[2]
        part = ss_all.reshape(N_DEV, -1)[:, pos:pos + cnt].reshape((N_DEV,) + s)
        full_small[n] = jnp.transpose(part, (1, 2, 0, 3)).reshape(s[0], s[1], N_DEV * s[2])
        pos += cnt

    params, saved = [], []
    cur = xs
    Wt = {n: (jnp.transpose(W[n], (0, 2, 1)) if n in TRANSPOSED else W[n]) for n in BIG}

    def local_slabs(l):
        return [Wt[n][l].astype(bf16) for n in BIG]

    def weights_of(names, slabs):
        out = {}
        for n, s in zip(names, slabs):
            if n in TRANSPOSED:
                out[n + "_t"] = s.reshape(-1, s.shape[-1])
            else:
                out[n] = assemble("assemble_" + n, s) if n in COL_SHARDED else s.reshape(-1, s.shape[-1])
        return out

    LATE = [n for n in BIG if n != "w_in"]
    def behind(token, blocks):
        return [blocks[0] + token[0, 0].astype(blocks[0].dtype)] + list(blocks[1:])

    first, token = gather2("gather_w_in", behind(small_token, [local_slabs(0)[0]]))
    late0 = gather_start("gather_start_0", behind(token, local_slabs(0)[1:]))
    in_flight, token = {}, late0[3]
    for l in range(1, L):
        in_flight[l] = gather_start("gather_start_%d" % l, behind(token, local_slabs(l)))
        token = in_flight[l][3]
    start_token = token[0, 0]
    for l in range(L):
        if l == 0:
            P = weights_of(["w_in"], first)
            late = lambda o: weights_of(LATE, gather_wait("gather_wait_0", late0[0], late0[1], late0[2], o))
        else:
            send_sems, recv_sems, bufs, _ = in_flight[l]
            P = weights_of(BIG, gather_wait("gather_wait_%d" % l, send_sems, recv_sems, bufs, cur))
            late = None
        for n in REPL:
            P[n] = W[n][l]
        for n in ("ln1_g", "sg_ln_g", "sg_ln_b", "cv_b", "cv_ln_g", "cv_ln_b", "q_norm_g", "k_norm_g", "ln2_g", "ffn_conv_b"):
            P[n] = P[n].reshape(1, -1)
        P["sg_b_t"] = P["sg_b"].T
        P["cv_w"] = full_small["cv_w"][l]
        P["b_gate"] = [full_small["b_gate"][l][k:k + 1] for k in range(3)]
        P["ffn_conv_w"] = [full_small["ffn_conv_w"][l][k:k + 1] for k in range(3)]
        if l == 0:
            P["ln1_g"] = P["ln1_g"] + start_token
        cur, sv = _forward_layer(cur, P, T, late=late)
        params.append(P)
        saved.append(sv)

    dy, loss_row = loss_head(cur, target, T)
    loss = lax.psum(loss_row[0, 0], ("x", "y", "c"))

    repl_shapes = [W[n].shape for n in REPL]
    repl_rows = _rows_for([s[1:] for s in repl_shapes], 16)
    ssl_rows = _rows_for([s[1:] for s in ss_shapes], 16)
    state = {"repl": [_pack_layers([X[n] for n in REPL], repl_rows) for X in (W, M, V)],
             "ss": [_pack_layers([X[n] for n in SMALL_SHARDED], ssl_rows) for X in (W, M, V)]}
    big_rows, big_off = {}, {}
    for n in BIG:
        big_off[n] = sum(big_rows.values())
        big_rows[n] = W[n].shape[1] * W[n].shape[2] // D
        state[n] = [(jnp.transpose(X[n], (0, 2, 1)) if n in TRANSPOSED else X[n]).reshape(L, big_rows[n], D)
                    for X in (W, M, V)]
    done = {n: None for n in state}

    def update(layer, received):
        for n in BIG:
            done[n] = adamw("adamw_" + n, *state[n], received[0], layer, done[n], row_off=big_off[n])
        done["repl"] = adamw("adamw_repl", *state["repl"], received[1], layer, done["repl"])
        done["ss"] = adamw("adamw_ss", *state["ss"], received[2], layer, done["ss"])

    def reduce_in_chip(G):
        big = [G[n].reshape(4, 2, big_rows[n], D) for n in BIG]
        ss_parts = []
        for n in SMALL_SHARDED:
            k, c = W[n].shape[1:]
            ss_parts.append(jnp.transpose(G[n].reshape(k, N_DEV, c), (1, 0, 2)).reshape(N_DEV, k * c))
        ss_send = jnp.concatenate(ss_parts, axis=1)
        ss_send = jnp.pad(ss_send, ((0, 0), (0, ssl_rows * 128 - ss_send.shape[1]))).reshape(4, 2, ssl_rows, 128)
        repl = _pack([G[n] for n in REPL], repl_rows)
        core = lax.axis_index("c")
        kept = jnp.concatenate([lax.dynamic_index_in_dim(b, core, 1, keepdims=False) for b in big], axis=1)
        ss_kept = lax.dynamic_index_in_dim(ss_send, core, 1, keepdims=False)
        got, repl_got, ss_got = pair_exchange("pair_exchange", big, [repl, ss_send], ["gather", "scatter"])
        return [add2("pair_add_big", kept, got), add2("pair_add_repl", repl, repl_got), add2("pair_add_ss", ss_kept, ss_got)]

    chip_modes = ["scatter", "gather", "scatter"]
    dcur = dy
    send = None
    for l in reversed(range(L)):
        dcur, G, recv = _backward_layer(dcur, params[l], saved[l], T, scatter=send)
        if send is not None:
            update(l + 1, recv)
        send = (reduce_in_chip(G), chip_modes)
    update(0, chip_exchange("chip_exchange_last", send[0], send[1]))

    results = [{}, {}, {}, {}]
    for k in range(4):
        for n, a in zip(REPL, _unpack_layers(done["repl"][k], repl_shapes)):
            results[k][n] = a
        for n, a in zip(SMALL_SHARDED, _unpack_layers(done["ss"][k], ss_shapes)):
            results[k][n] = a
        for n in BIG:
            results[k][n] = jnp.transpose(done[n][k], (0, 2, 1)) if n in TRANSPOSED else done[n][k].reshape(W[n].shape)
    out = [loss, dcur.reshape(1, T, D)]
    for k in range(4):
        out += [results[k][n] for n in W_NAMES]
    return tuple(out)
```

```python
import functools

import jax
import jax.numpy as jnp
from jax import lax
from jax.experimental import pallas as pl
from jax.experimental.pallas import tpu as pltpu

f32 = jnp.float32
bf16 = jnp.bfloat16

EPS = 1e-6
N_DEV = 8
DEPTH = 4
CHUNK = 128
HEAD_DIM = 64
N_HEADS = 8
CV_KERNEL = 31
VMEM_LIMIT_BYTES = 56 * 2 ** 20

ADAM_LR = 0.001
ADAM_B1 = 0.9
ADAM_B2 = 0.999
ADAM_EPS = 1e-08
ADAM_WD = 0.01
ADAM_STEP = 10

MESH = pl.DeviceIdType.MESH


def _cparams(n_grid):
    return pltpu.CompilerParams(dimension_semantics=("arbitrary",) * n_grid, vmem_limit_bytes=VMEM_LIMIT_BYTES)


def exchange(name, arrays, modes):
    n = len(arrays)

    def body(*refs):
        copies = _direct_copies(refs[:n], refs[n:2 * n], modes, *refs[2 * n + 1:])
        for cp in copies:
            cp.start()
        for cp in copies:
            cp.wait()
        refs[2 * n][...] = jnp.zeros_like(refs[2 * n])

    res = pl.pallas_call(
        body, name=name, out_shape=_exchange_out_shapes(arrays, modes) + [jax.ShapeDtypeStruct((8, 128), f32)],
        in_specs=[_ANY] * n, out_specs=[_ANY] * n + [pl.BlockSpec(memory_space=pltpu.VMEM)],
        scratch_shapes=_exchange_sems(n),
    )(*arrays)
    return list(res[:n]), res[n]


_ANY = pl.BlockSpec(memory_space=pl.ANY)


def _exchange_out_shapes(arrays, modes):
    return [jax.ShapeDtypeStruct((N_DEV,) + tuple(a.shape) if m == "gather" else tuple(a.shape), a.dtype)
            for a, m in zip(arrays, modes)]


def _exchange_sems(n):
    return [pltpu.SemaphoreType.DMA((n, N_DEV - 1)), pltpu.SemaphoreType.DMA((n, N_DEV - 1)), pltpu.SemaphoreType.DMA((n,))]


def _direct_copies(ins, outs, modes, send_sems, recv_sems, local_sems):
    x, y, c = lax.axis_index("x"), lax.axis_index("y"), lax.axis_index("c")
    me = 4 * x + 2 * y + c
    copies = []
    for k in range(len(ins)):
        src_mine = ins[k] if modes[k] == "gather" else ins[k].at[me]
        copies.append(pltpu.make_async_copy(src_mine, outs[k].at[me], local_sems.at[k]))
        for r in range(1, N_DEV):
            px = 1 - x if r & 4 else x
            py = 1 - y if r & 2 else y
            pc = 1 - c if r & 1 else c
            src = ins[k] if modes[k] == "gather" else ins[k].at[4 * px + 2 * py + pc]
            copies.append(pltpu.make_async_remote_copy(
                src_ref=src, dst_ref=outs[k].at[me], send_sem=send_sems.at[k, r - 1], recv_sem=recv_sems.at[k, r - 1],
                device_id=(px, py, pc), device_id_type=MESH))
    return copies


def own_slot_buffer(block, n_slots, index):
    return lax.dynamic_update_slice(lax.empty((n_slots,) + tuple(block.shape), block.dtype), block[None],
                                    (index,) + (0,) * block.ndim)


def _sems(n, m):
    return [pltpu.SemaphoreType.DMA((n, m)), pltpu.SemaphoreType.DMA((n, m))]


def _two_level_gather(bufs, send_sems, recv_sems):
    x, y, c = lax.axis_index("x"), lax.axis_index("y"), lax.axis_index("c")
    me = 4 * x + 2 * y + c
    sibling = (x, y, 1 - c)
    chips = [(1 - x, y), (x, 1 - y), (1 - x, 1 - y)]

    def slot(px, py, pc):
        return 4 * px + 2 * py + pc

    def copy(k, sem, block, to):
        rows = bufs[k].at[block]
        return pltpu.make_async_remote_copy(src_ref=rows, dst_ref=rows, send_sem=send_sems.at[k, sem],
                                            recv_sem=recv_sems.at[k, sem], device_id=to, device_id_type=MESH)

    def first(k):
        return [copy(k, 0, me, sibling)] + [copy(k, 1 + j, me, (*chip, c)) for j, chip in enumerate(chips)]

    def passed_on(k, j):
        return copy(k, 4 + j, slot(*chips[j], c), sibling)

    def start():
        for k in range(len(bufs)):
            for cp in first(k):
                cp.start()

    def forward():
        for k in range(len(bufs)):
            for j in range(3):
                copy(k, 1 + j, slot(*chips[j], c), sibling).wait_recv()
                passed_on(k, j).start()

    def finish():
        for k in range(len(bufs)):
            copy(k, 0, slot(x, y, 1 - c), sibling).wait_recv()
            for j, chip in enumerate(chips):
                copy(k, 4 + j, slot(*chip, 1 - c), sibling).wait_recv()
            for cp in first(k) + [passed_on(k, j) for j in range(3)]:
                cp.wait_send()

    return start, forward, finish


def pair_exchange(name, big, smalls, modes):
    nb, ns = len(big), len(smalls)
    rows = [a.shape[2] for a in big]
    total, width = sum(rows), big[0].shape[3]
    n = nb + ns

    def body(*refs):
        ins, outs = refs[:n], refs[n:n + 1 + ns]
        send_sems, recv_sems = refs[-2:]
        x, y, c = lax.axis_index("x"), lax.axis_index("y"), lax.axis_index("c")
        copies = []

        def remote(k, src, dst):
            copies.append(pltpu.make_async_remote_copy(src_ref=src, dst_ref=dst, send_sem=send_sems.at[k, 0],
                                                       recv_sem=recv_sems.at[k, 0], device_id=(x, y, 1 - c),
                                                       device_id_type=MESH))

        off = 0
        for k in range(nb):
            remote(k, ins[k].at[:, 1 - c], outs[0].at[:, pl.ds(off, rows[k]), :])
            off += rows[k]
        for j in range(ns):
            remote(nb + j, ins[nb + j].at[:, 1 - c] if modes[j] == "scatter" else ins[nb + j], outs[1 + j])
        for cp in copies:
            cp.start()
        for cp in copies:
            cp.wait()

    out_shape = [jax.ShapeDtypeStruct((4, total, width), big[0].dtype)]
    for a, m in zip(smalls, modes):
        out_shape.append(jax.ShapeDtypeStruct((4,) + tuple(a.shape[2:]) if m == "scatter" else tuple(a.shape), a.dtype))
    return pl.pallas_call(
        body, name=name, out_shape=out_shape, in_specs=[_ANY] * n, out_specs=[_ANY] * len(out_shape),
        scratch_shapes=_sems(n, 1),
    )(*big, *smalls)


def add2(name, a, b):
    shape = a.shape
    a2, b2 = a.reshape(-1, shape[-1]), b.reshape(-1, shape[-1])
    R, C = a2.shape
    tr = _row_tile(R, C, max_elems=1200 * 1024)

    def body(a_ref, b_ref, o_ref):
        o_ref[...] = (a_ref[...].astype(f32) + b_ref[...].astype(f32)).astype(o_ref.dtype)

    spec = pl.BlockSpec((tr, C), lambda i: (i, 0))
    return pl.pallas_call(
        body, name=name, grid=(R // tr,), in_specs=[spec, spec], out_specs=spec,
        out_shape=jax.ShapeDtypeStruct((R, C), a.dtype), compiler_params=_cparams(1),
    )(a2, b2).reshape(shape)


def chip_buffers(arrays, modes):
    mine = 2 * lax.axis_index("x") + lax.axis_index("y")
    return [own_slot_buffer(lax.dynamic_index_in_dim(a, mine, 0, keepdims=False) if m == "scatter" else a, 4, mine)
            for a, m in zip(arrays, modes)]


def _chip_copies(ins, bufs, modes, send_sems, recv_sems):
    x, y, c = lax.axis_index("x"), lax.axis_index("y"), lax.axis_index("c")
    mine = 2 * x + y
    copies = []
    for k in range(len(ins)):
        for r in range(1, 4):
            px = 1 - x if r & 2 else x
            py = 1 - y if r & 1 else y
            src = ins[k].at[2 * px + py] if modes[k] == "scatter" else ins[k]
            copies.append(pltpu.make_async_remote_copy(
                src_ref=src, dst_ref=bufs[k].at[mine], send_sem=send_sems.at[k, r - 1], recv_sem=recv_sems.at[k, r - 1],
                device_id=(px, py, c), device_id_type=MESH))
    return copies


def chip_exchange(name, arrays, modes):
    n = len(arrays)

    def body(*refs):
        copies = _chip_copies(refs[:n], refs[2 * n:3 * n], modes, *refs[3 * n:])
        for cp in copies:
            cp.start()
        for cp in copies:
            cp.wait()

    bufs = chip_buffers(arrays, modes)
    return pl.pallas_call(
        body, name=name, out_shape=[jax.ShapeDtypeStruct(b.shape, b.dtype) for b in bufs],
        in_specs=[_ANY] * (2 * n), out_specs=[_ANY] * n, input_output_aliases={n + k: k for k in range(n)},
        scratch_shapes=_sems(n, 3),
    )(*arrays, *bufs)


def gather_buffers(blocks):
    me = 4 * lax.axis_index("x") + 2 * lax.axis_index("y") + lax.axis_index("c")
    return [own_slot_buffer(b, N_DEV, me) for b in blocks]


_HBM = pl.BlockSpec(memory_space=pltpu.HBM)
_SEM = pl.BlockSpec(memory_space=pltpu.SEMAPHORE)
_SPLIT_PARAMS = pltpu.CompilerParams(has_side_effects=pltpu.SideEffectType.DATAFLOW_SIDE_EFFECTING)


def _split_gather_copies(bufs, send_sems, recv_sems):
    x, y, c = lax.axis_index("x"), lax.axis_index("y"), lax.axis_index("c")
    me = 4 * x + 2 * y + c
    copies = []
    for k in range(len(bufs)):
        rows = bufs[k].at[me]
        for r in range(1, N_DEV):
            peer = (1 - x if r & 4 else x, 1 - y if r & 2 else y, 1 - c if r & 1 else c)
            copies.append(pltpu.make_async_remote_copy(
                src_ref=rows, dst_ref=rows, send_sem=send_sems.at[(N_DEV - 1) * k + r - 1],
                recv_sem=recv_sems.at[(N_DEV - 1) * k + r - 1], device_id=peer, device_id_type=MESH))
    return copies


def gather_start(name, blocks):
    n = len(blocks)
    bufs = gather_buffers(blocks)

    def body(*refs):
        for cp in _split_gather_copies(refs[n + 2:2 * n + 2], refs[n], refs[n + 1]):
            cp.start()
        refs[2 * n + 2][...] = jnp.zeros_like(refs[2 * n + 2])

    n_sem = n * (N_DEV - 1)
    res = pl.pallas_call(
        body, name=name,
        out_shape=(pltpu.SemaphoreType.DMA((n_sem,)), pltpu.SemaphoreType.DMA((n_sem,)),
                   *[pltpu.HBM(b.shape, b.dtype) for b in bufs], jax.ShapeDtypeStruct((8, 128), f32)),
        in_specs=[_HBM] * n, out_specs=(_SEM, _SEM, *[_HBM] * n, pl.BlockSpec(memory_space=pltpu.VMEM)),
        input_output_aliases={k: 2 + k for k in range(n)}, compiler_params=_SPLIT_PARAMS,
    )(*[pltpu.with_memory_space_constraint(b, pltpu.HBM) for b in bufs])
    return res[0], res[1], list(res[2:2 + n]), res[2 + n]


def gather_wait(name, send_sems, recv_sems, bufs, after):
    n = len(bufs)

    def body(*refs):
        for cp in _split_gather_copies(refs[:n], refs[n], refs[n + 1]):
            cp.wait_send()
            cp.wait_recv()

    return list(pl.pallas_call(
        body, name=name, out_shape=tuple(pltpu.HBM(b.shape, b.dtype) for b in bufs),
        in_specs=[_HBM] * n + [_SEM, _SEM, _ANY], out_specs=tuple([_HBM] * n),
        input_output_aliases={k: k for k in range(n)}, compiler_params=_SPLIT_PARAMS,
    )(*bufs, send_sems, recv_sems, after))


def gather2(name, arrays):
    n = len(arrays)

    def body(*refs):
        start, forward, finish = _two_level_gather(refs[n:2 * n], *refs[2 * n + 1:])
        start()
        forward()
        finish()
        refs[2 * n][...] = jnp.zeros_like(refs[2 * n])

    bufs = gather_buffers(arrays)
    res = pl.pallas_call(
        body, name=name, out_shape=[jax.ShapeDtypeStruct(b.shape, b.dtype) for b in bufs] + [jax.ShapeDtypeStruct((8, 128), f32)],
        in_specs=[_ANY] * n, out_specs=[_ANY] * n + [pl.BlockSpec(memory_space=pltpu.VMEM)],
        input_output_aliases={k: k for k in range(n)}, scratch_shapes=_sems(n, N_DEV - 1),
    )(*bufs)
    return list(res[:n]), res[n]


def assemble(name, slabs):
    _, K, Ns = slabs.shape
    g = N_DEV if Ns % 128 == 0 else 2

    def body(w_ref, o_ref):
        o_ref[...] = jnp.concatenate([w_ref[s] for s in range(g)], axis=1)

    return pl.pallas_call(
        body, name=name, grid=(N_DEV // g,),
        in_specs=[pl.BlockSpec((g, K, Ns), lambda m: (m, 0, 0))],
        out_specs=pl.BlockSpec((K, g * Ns), lambda m: (0, m)),
        out_shape=jax.ShapeDtypeStruct((K, N_DEV * Ns), slabs.dtype),
        compiler_params=_cparams(1),
    )(slabs)


def mm_nn(name, a, b, out_dtype, add=None, tm=1024, tn=512, tk=None):
    M, K = a.shape
    N = b.shape[1]
    tm, tn = min(tm, M), min(tn, N)
    tk = K if tk is None else tk
    nk = K // tk
    has_add = add is not None

    def body(a_ref, b_ref, *rest):
        o_ref = rest[1] if has_add else rest[0]
        part = jnp.dot(a_ref[...].astype(bf16), b_ref[...].astype(bf16), preferred_element_type=f32)
        if nk == 1:
            o_ref[...] = (part + rest[0][...] if has_add else part).astype(o_ref.dtype)
            return
        acc_ref = rest[-1]
        k = pl.program_id(2)

        @pl.when(k == 0)
        def _():
            acc_ref[...] = part + rest[0][...] if has_add else part

        @pl.when(k > 0)
        def _():
            acc_ref[...] += part

        @pl.when(k == nk - 1)
        def _():
            o_ref[...] = acc_ref[...].astype(o_ref.dtype)

    in_specs = [pl.BlockSpec((tm, tk), lambda i, j, k: (i, k)), pl.BlockSpec((tk, tn), lambda i, j, k: (k, j))]
    ops = [a, b]
    if has_add:
        in_specs.append(pl.BlockSpec((tm, tn), lambda i, j, k: (i, j)))
        ops.append(add)
    return pl.pallas_call(
        body, name=name, grid=(M // tm, N // tn, nk), in_specs=in_specs,
        out_specs=pl.BlockSpec((tm, tn), lambda i, j, k: (i, j)),
        out_shape=jax.ShapeDtypeStruct((M, N), out_dtype),
        scratch_shapes=[pltpu.VMEM((tm, tn), f32)] if nk > 1 else [], compiler_params=_cparams(3),
    )(*ops)


def mm_many(name, xs, ws, out_dtype, transpose_w=False, tm=512):
    n = len(xs)
    M = xs[0].shape[0]
    tm = min(tm, M)
    dims = NT_DIMS if transpose_w else (((1,), (0,)), ((), ()))

    def body(*refs):
        for k in range(n):
            refs[2 * n + k][...] = lax.dot_general(refs[k][...].astype(bf16), refs[n + k][...].astype(bf16), dims,
                                                   preferred_element_type=f32).astype(out_dtype)

    n_out = [w.shape[0] if transpose_w else w.shape[1] for w in ws]
    return pl.pallas_call(
        body, name=name, grid=(M // tm,),
        in_specs=[pl.BlockSpec((tm, x.shape[1]), lambda i: (i, 0)) for x in xs]
        + [pl.BlockSpec(w.shape, lambda i: (0, 0)) for w in ws],
        out_specs=[pl.BlockSpec((tm, c), lambda i: (i, 0)) for c in n_out],
        out_shape=[jax.ShapeDtypeStruct((M, c), out_dtype) for c in n_out], compiler_params=_cparams(1),
    )(*xs, *ws)


def concat_cols(name, pieces, T, tm=512):
    widths = [p.shape[1] for p in pieces]
    total = sum(widths)
    tm = min(tm, T)

    def body(*refs):
        o_ref = refs[-1]
        off = 0
        for r, w in zip(refs[:-1], widths):
            o_ref[:, off:off + w] = r[...]
            off += w

    return pl.pallas_call(
        body, name=name, grid=(T // tm,),
        in_specs=[pl.BlockSpec((tm, w), lambda i: (i, 0)) for w in widths],
        out_specs=pl.BlockSpec((tm, total), lambda i: (i, 0)),
        out_shape=jax.ShapeDtypeStruct((T, total), pieces[0].dtype), compiler_params=_cparams(1),
    )(*pieces)


def mm_nt(name, a, b, out_dtype, tm=1024, tko=None, tc=None):
    M, C = a.shape
    Ko = b.shape[0]
    tm = min(tm, M)
    tc = C if tc is None else min(tc, C)
    tko = Ko if tko is None else tko
    nc = C // tc

    def body(a_ref, b_ref, o_ref, *scratch):
        part = lax.dot_general(a_ref[...].astype(bf16), b_ref[...].astype(bf16), (((1,), (1,)), ((), ())),
                               preferred_element_type=f32)
        if nc == 1:
            o_ref[...] = part.astype(o_ref.dtype)
            return
        acc_ref = scratch[0]
        c = pl.program_id(2)

        @pl.when(c == 0)
        def _():
            acc_ref[...] = part

        @pl.when(c > 0)
        def _():
            acc_ref[...] += part

        @pl.when(c == nc - 1)
        def _():
            o_ref[...] = acc_ref[...].astype(o_ref.dtype)

    return pl.pallas_call(
        body, name=name, grid=(M // tm, Ko // tko, nc),
        in_specs=[pl.BlockSpec((tm, tc), lambda i, j, c: (i, c)), pl.BlockSpec((tko, tc), lambda i, j, c: (j, c))],
        out_specs=pl.BlockSpec((tm, tko), lambda i, j, c: (i, j)),
        out_shape=jax.ShapeDtypeStruct((M, Ko), out_dtype),
        scratch_shapes=[pltpu.VMEM((tm, tko), f32)] if nc > 1 else [], compiler_params=_cparams(3),
    )(a, b)


def mm_tn(name, a, b, n_slab=None, tk=512, tn=512):
    T, K = a.shape
    N = b.shape[1]
    tk = min(tk, K)
    if n_slab is None:
        tn = min(tn, N)

        def body(a_ref, b_ref, o_ref):
            o_ref[...] = lax.dot_general(a_ref[...].astype(bf16), b_ref[...].astype(bf16), (((0,), (0,)), ((), ())),
                                         preferred_element_type=f32).astype(o_ref.dtype)

        return pl.pallas_call(
            body, name=name, grid=(N // tn, K // tk),
            in_specs=[pl.BlockSpec((T, tk), lambda j, i: (0, i)), pl.BlockSpec((T, tn), lambda j, i: (0, j))],
            out_specs=pl.BlockSpec((tk, tn), lambda j, i: (i, j)),
            out_shape=jax.ShapeDtypeStruct((K, N), bf16), compiler_params=_cparams(2),
        )(a, b)

    Ns = n_slab
    g = N_DEV if Ns % 128 == 0 else 2
    tn = g * Ns

    def body(a_ref, b_ref, o_ref):
        val = lax.dot_general(a_ref[...].astype(bf16), b_ref[...].astype(bf16), (((0,), (0,)), ((), ())),
                              preferred_element_type=f32)
        for s in range(g):
            o_ref[s] = val[:, s * Ns:(s + 1) * Ns].astype(o_ref.dtype)

    return pl.pallas_call(
        body, name=name, grid=(K // tk, N_DEV // g),
        in_specs=[pl.BlockSpec((T, tk), lambda i, j: (0, i)), pl.BlockSpec((T, tn), lambda i, j: (0, j))],
        out_specs=pl.BlockSpec((g, tk, Ns), lambda i, j: (j, i, 0)),
        out_shape=jax.ShapeDtypeStruct((N_DEV, K, Ns), bf16), compiler_params=_cparams(2),
    )(a, b)


def _tile_spec(w, tm, cb):
    if callable(cb):
        return pl.BlockSpec((tm, w), lambda c, i: (i, cb(c)))
    return pl.BlockSpec((tm, w), lambda c, i: (i, cb))


def _param_spec(block, cb):
    nd = len(block)
    if cb is None:
        return pl.BlockSpec(block, lambda c, i: (0,) * nd)
    return pl.BlockSpec(block, lambda c, i: (0,) * (nd - 1) + (cb(c),))


def rowwise(name, fn, tiled, params, outs, T, tm, ncol=1):
    n_in = len(tiled) + len(params)
    n_t = len(tiled)

    def body(*refs):
        ins = [r[...].astype(f32) for r in refs[:n_t]] + [r[...] for r in refs[n_t:n_in]]
        res = fn(*ins)
        for r, o in zip(refs[n_in:], res):
            r[...] = o.astype(r.dtype)

    return pl.pallas_call(
        body, name=name, grid=(ncol, T // tm),
        in_specs=[_tile_spec(w, tm, cb) for _, w, cb in tiled] + [_param_spec(blk, cb) for _, blk, cb in params],
        out_specs=[_tile_spec(w, tm, cb) for _, w, cb, _ in outs],
        out_shape=[jax.ShapeDtypeStruct((T, cols), dt) for cols, _, _, dt in outs],
        compiler_params=_cparams(2),
    )(*[a for a, _, _ in tiled], *[a for a, _, _ in params])


def rowwise_bwd(name, fn, tiled, params, cts, grads, T, tm, ncol=1, adds=None):
    n_t, n_p, n_c = len(tiled), len(params), len(cts)
    adds = adds or [None] * n_t
    add_list = [(k, a) for k, a in enumerate(adds) if a is not None]
    want = [k for k, g in enumerate(grads) if g is not None]
    n_a = len(add_list)

    def body(*refs):
        pos = 0
        t_refs = refs[pos:pos + n_t]; pos += n_t
        p_refs = refs[pos:pos + n_p]; pos += n_p
        c_refs = refs[pos:pos + n_c]; pos += n_c
        a_refs = refs[pos:pos + n_a]; pos += n_a
        g_refs = refs[pos:pos + len(want)]; pos += len(want)
        pg_refs = refs[pos:pos + n_p]
        primals = [r[...].astype(f32) for r in t_refs] + [r[...] for r in p_refs]
        _, vjp = jax.vjp(fn, *primals)
        g = vjp(tuple(r[...].astype(f32) for r in c_refs))
        add_of = {k: a_refs[n][...] for n, (k, _) in enumerate(add_list)}
        for n, k in enumerate(want):
            val = g[k]
            if k in add_of:
                val = val + add_of[k]
            g_refs[n][...] = val.astype(g_refs[n].dtype)
        i = pl.program_id(1)
        for k in range(n_p):
            @pl.when(i == 0)
            def _(k=k):
                pg_refs[k][...] = g[n_t + k]

            @pl.when(i > 0)
            def _(k=k):
                pg_refs[k][...] += g[n_t + k]

    in_specs = ([_tile_spec(w, tm, cb) for _, w, cb in tiled] + [_param_spec(blk, cb) for _, blk, cb in params]
                + [_tile_spec(w, tm, cb) for _, w, cb in cts] + [_tile_spec(w, tm, cb) for _, (_, w, cb) in add_list])
    ops = ([a for a, _, _ in tiled] + [a for a, _, _ in params] + [a for a, _, _ in cts]
           + [a for _, (a, _, _) in add_list])
    out_specs, out_shape = [], []
    for k in want:
        w = tiled[k][1]
        out_specs.append(_tile_spec(w, tm, lambda c: c))
        out_shape.append(jax.ShapeDtypeStruct((T, ncol * w), grads[k]))
    for a, blk, cb in params:
        out_specs.append(_param_spec(blk, cb))
        out_shape.append(jax.ShapeDtypeStruct(a.shape, f32))
    return pl.pallas_call(
        body, name=name, grid=(ncol, T // tm), in_specs=in_specs, out_specs=out_specs, out_shape=out_shape,
        compiler_params=_cparams(2),
    )(*ops)


@jax.custom_vjp
def _bdot(a, b):
    return jnp.dot(a.astype(bf16), b.astype(bf16), preferred_element_type=f32)


def _bdot_fwd(a, b):
    return _bdot(a, b), (a, b)


def _bdot_bwd(res, ct):
    a, b = res
    ctb = ct.astype(bf16)
    da = lax.dot_general(ctb, b.astype(bf16), (((1,), (1,)), ((), ())), preferred_element_type=f32)
    db = lax.dot_general(a.astype(bf16), ctb, (((0,), (0,)), ((), ())), preferred_element_type=f32)
    return da, db


_bdot.defvjp(_bdot_fwd, _bdot_bwd)


def _layer_norm(x, g, b):
    mu = jnp.mean(x, axis=-1, keepdims=True)
    xc = x - mu
    y = xc * lax.rsqrt(jnp.mean(xc * xc, axis=-1, keepdims=True) + EPS)
    return y * g + b


def f_rms(x, g):
    y = x * lax.rsqrt(jnp.mean(x * x, axis=-1, keepdims=True) + EPS)
    return (y * g,)


def f_sgu(zu, zv, ln_g, ln_b, wm, sgb_t):
    u = jax.nn.gelu(zu)
    vn = _layer_norm(jax.nn.gelu(zv), ln_g, ln_b)
    row = lax.broadcasted_iota(jnp.int32, (CHUNK, CHUNK), 0)
    col = lax.broadcasted_iota(jnp.int32, (CHUNK, CHUNK), 1)
    tril = col <= row
    low = col < HEAD_DIM
    parts = []
    for p in range(4):
        vp = vn[:, CHUNK * p:CHUNK * (p + 1)]
        w0 = jnp.where(tril, wm[2 * p], 0.0)
        w1 = jnp.where(tril, wm[2 * p + 1], 0.0)
        parts.append(_bdot(w0, jnp.where(low, vp, 0.0)) + _bdot(w1, jnp.where(low, 0.0, vp)))
    s = jnp.concatenate(parts, axis=1)
    lane_g = lax.shift_right_logical(lax.broadcasted_iota(jnp.int32, s.shape, 1), 6)
    bias = jnp.zeros_like(s)
    for g in range(8):
        bias = jnp.where(lane_g == g, sgb_t[:, g:g + 1], bias)
    return (u * (s + bias),)


def f_glu(p, gl):
    return (p * jax.nn.sigmoid(gl),)


def f_lnsilu(c1, g, b):
    return (jax.nn.silu(_layer_norm(c1, g, b)),)


def _head_norm(x, g64):
    g = jnp.concatenate([g64] * N_HEADS, axis=1)
    lane_h = lax.shift_right_logical(lax.broadcasted_iota(jnp.int32, x.shape, 1), 6)
    x2 = x * x
    r = jnp.zeros_like(x)
    for h in range(N_HEADS):
        mh = lane_h == h
        ms = jnp.sum(jnp.where(mh, x2, 0.0), axis=-1, keepdims=True) * (1.0 / HEAD_DIM)
        r = jnp.where(mh, lax.rsqrt(ms + EPS), r)
    return (x * r) * g


def f_qkv(zq, zk, zv, qg, kg):
    return (_head_norm(zq, qg) * 0.125, _head_norm(zk, kg), zv)


def f_merge(g0, g1, g2, ya, yb, yc, b0, b1, b2):
    return (jax.nn.sigmoid(g0 + b0) * ya + jax.nn.sigmoid(g1 + b1) * yb + jax.nn.sigmoid(g2 + b2) * yc,)


HALO = 32


def conv31_fwd(x, w, b, T, tt=256):
    C = x.shape[1]
    r = tt // HALO

    def body(x_ref, h_ref, w_ref, b_ref, y_ref, buf):
        i = pl.program_id(0)
        halo = h_ref[...]
        buf[0:HALO, :] = jnp.where(i > 0, halo, jnp.zeros_like(halo))
        buf[HALO:HALO + tt, :] = x_ref[...]
        acc = jnp.zeros((tt, C), f32) + b_ref[...]
        for k in range(CV_KERNEL):
            acc = acc + w_ref[k:k + 1, :] * buf[pl.ds(HALO - (CV_KERNEL - 1) + k, tt), :]
        y_ref[...] = acc

    return pl.pallas_call(
        body, name="conv31_fwd", grid=(T // tt,),
        in_specs=[pl.BlockSpec((tt, C), lambda i: (i, 0)),
                  pl.BlockSpec((HALO, C), lambda i: (jnp.maximum(i * r - 1, 0), 0)),
                  pl.BlockSpec((CV_KERNEL, C), lambda i: (0, 0)), pl.BlockSpec((1, C), lambda i: (0, 0))],
        out_specs=pl.BlockSpec((tt, C), lambda i: (i, 0)),
        out_shape=jax.ShapeDtypeStruct((T, C), f32),
        scratch_shapes=[pltpu.VMEM((HALO + tt, C), f32)], compiler_params=_cparams(1),
    )(x, x, w, b)


def conv31_bwd(x, w, dy, T, tt=256):
    C = x.shape[1]
    r = tt // HALO
    n = T // tt

    def body(x_ref, h_ref, w_ref, dy_ref, dyn_ref, dx_ref, dw_ref, db_ref, xbuf, dbuf):
        i = pl.program_id(0)
        halo = h_ref[...]
        xbuf[0:HALO, :] = jnp.where(i > 0, halo, jnp.zeros_like(halo))
        xbuf[HALO:HALO + tt, :] = x_ref[...]
        nxt = dyn_ref[...]
        dy = dy_ref[...]
        dbuf[0:tt, :] = dy
        dbuf[tt:tt + HALO, :] = jnp.where(i < n - 1, nxt, jnp.zeros_like(nxt))

        @pl.when(i == 0)
        def _():
            dw_ref[...] = jnp.zeros_like(dw_ref)
            db_ref[...] = jnp.zeros_like(db_ref)

        acc = jnp.zeros((tt, C), f32)
        for k in range(CV_KERNEL):
            acc = acc + w_ref[k:k + 1, :] * dbuf[pl.ds(CV_KERNEL - 1 - k, tt), :]
            xs = xbuf[pl.ds(HALO - (CV_KERNEL - 1) + k, tt), :]
            dw_ref[k:k + 1, :] += jnp.sum(dy * xs, axis=0, keepdims=True)
        dx_ref[...] = acc
        db_ref[...] += jnp.sum(dy, axis=0, keepdims=True)

    return pl.pallas_call(
        body, name="conv31_bwd", grid=(n,),
        in_specs=[pl.BlockSpec((tt, C), lambda i: (i, 0)),
                  pl.BlockSpec((HALO, C), lambda i: (jnp.maximum(i * r - 1, 0), 0)),
                  pl.BlockSpec((CV_KERNEL, C), lambda i: (0, 0)),
                  pl.BlockSpec((tt, C), lambda i: (i, 0)),
                  pl.BlockSpec((HALO, C), lambda i: (jnp.minimum((i + 1) * r, n * r - 1), 0))],
        out_specs=[pl.BlockSpec((tt, C), lambda i: (i, 0)), pl.BlockSpec((CV_KERNEL, C), lambda i: (0, 0)),
                   pl.BlockSpec((1, C), lambda i: (0, 0))],
        out_shape=[jax.ShapeDtypeStruct((T, C), f32), jax.ShapeDtypeStruct((CV_KERNEL, C), f32),
                   jax.ShapeDtypeStruct((1, C), f32)],
        scratch_shapes=[pltpu.VMEM((HALO + tt, C), f32), pltpu.VMEM((HALO + tt, C), f32)], compiler_params=_cparams(1),
    )(x, x, w, dy, dy)


FFN_TC = 128
FFN_PAD = 8


def ffn_act_fwd(up, cw, cb, T):
    F = up.shape[1] // 2
    nj = F // FFN_TC
    rc = min(128, T)

    def body(g_ref, v_ref, g0, g1, g2, gb, v0, v1, v2, vb, o_ref, gp, vp):
        zeros = jnp.zeros((FFN_PAD, FFN_TC), f32)
        for p, x_ref in ((gp, g_ref), (vp, v_ref)):
            p[0:FFN_PAD, :] = zeros
            p[FFN_PAD:FFN_PAD + T, :] = x_ref[...].astype(f32)
        wg = (g0[...], g1[...], g2[...], gb[...])
        wv = (v0[...], v1[...], v2[...], vb[...])

        def conv(p, w, r):
            return (w[0] * p[pl.ds(FFN_PAD + r - 2, rc), :] + w[1] * p[pl.ds(FFN_PAD + r - 1, rc), :]
                    + w[2] * p[pl.ds(FFN_PAD + r, rc), :] + w[3])

        for r in range(0, T, rc):
            o_ref[pl.ds(r, rc), :] = (jax.nn.silu(conv(gp, wg, r)) * conv(vp, wv, r)).astype(o_ref.dtype)

    gspec = pl.BlockSpec((T, FFN_TC), lambda j: (0, j))
    vspec = pl.BlockSpec((T, FFN_TC), lambda j: (0, j + nj))
    pg = pl.BlockSpec((1, FFN_TC), lambda j: (0, j))
    pv = pl.BlockSpec((1, FFN_TC), lambda j: (0, j + nj))
    return pl.pallas_call(
        body, name="ffn_act_fwd", grid=(nj,),
        in_specs=[gspec, vspec, pg, pg, pg, pg, pv, pv, pv, pv], out_specs=gspec,
        out_shape=jax.ShapeDtypeStruct((T, F), bf16),
        scratch_shapes=[pltpu.VMEM((FFN_PAD + T, FFN_TC), f32)] * 2, compiler_params=_cparams(1),
    )(up, up, cw[0], cw[1], cw[2], cb, cw[0], cw[1], cw[2], cb)


def ffn_act_bwd(up, cw, cb, dact, T):
    F = up.shape[1] // 2
    nj = F // FFN_TC

    rc = min(128, T)
    ext = rc + FFN_PAD

    def body(g_ref, v_ref, g0, g1, g2, gb, v0, v1, v2, vb, d_ref, dupg_ref, dupv_ref, *rest):
        p_refs, (gp, vp, dp, dgs, dvs) = rest[:8], rest[8:]
        zeros = jnp.zeros((FFN_PAD, FFN_TC), f32)
        for p, x_ref in ((gp, g_ref), (vp, v_ref)):
            p[0:FFN_PAD, :] = zeros
            p[FFN_PAD:FFN_PAD + T, :] = x_ref[...].astype(f32)
            p[FFN_PAD + T:FFN_PAD + T + FFN_PAD, :] = zeros
        dp[0:T, :] = d_ref[...]
        dp[T:T + FFN_PAD, :] = zeros
        wg = (g0[...], g1[...], g2[...], gb[...])
        wv = (v0[...], v1[...], v2[...], vb[...])
        acc = [jnp.zeros((1, FFN_TC), f32) for _ in range(8)]

        def taps(p, r):
            return tuple(p[pl.ds(FFN_PAD + r - s, ext), :] for s in (2, 1, 0))

        for r in range(0, T, rc):
            xg, xv = taps(gp, r), taps(vp, r)
            gc = wg[0] * xg[0] + wg[1] * xg[1] + wg[2] * xg[2] + wg[3]
            vc = wv[0] * xv[0] + wv[1] * xv[1] + wv[2] * xv[2] + wv[3]
            d = dp[pl.ds(r, ext), :]
            sg = jax.nn.sigmoid(gc)
            sides = ((d * vc * (sg * (1.0 + gc * (1.0 - sg))), xg, wg, dgs, dupg_ref), (d * (gc * sg), xv, wv, dvs, dupv_ref))
            for side, (dc, x, w, buf, dup_ref) in enumerate(sides):
                buf[...] = dc
                dc0 = dc[0:rc]
                dup = w[2] * dc0 + w[1] * buf[pl.ds(1, rc), :] + w[0] * buf[pl.ds(2, rc), :]
                dup_ref[pl.ds(r, rc), :] = dup.astype(dup_ref.dtype)
                for k in range(3):
                    acc[4 * side + k] = acc[4 * side + k] + jnp.sum(dc0 * x[k][0:rc], axis=0, keepdims=True)
                acc[4 * side + 3] = acc[4 * side + 3] + jnp.sum(dc0, axis=0, keepdims=True)
        for k in range(8):
            p_refs[k][...] = acc[k]

    gspec = pl.BlockSpec((T, FFN_TC), lambda j: (0, j))
    vspec = pl.BlockSpec((T, FFN_TC), lambda j: (0, j + nj))
    pg = pl.BlockSpec((1, FFN_TC), lambda j: (0, j))
    pv = pl.BlockSpec((1, FFN_TC), lambda j: (0, j + nj))
    res = pl.pallas_call(
        body, name="ffn_act_bwd", grid=(nj,),
        in_specs=[gspec, vspec, pg, pg, pg, pg, pv, pv, pv, pv, gspec],
        out_specs=[gspec, gspec] + [pg] * 8,
        out_shape=[jax.ShapeDtypeStruct((T, F), bf16)] * 2 + [jax.ShapeDtypeStruct((1, F), f32)] * 8,
        scratch_shapes=[pltpu.VMEM((FFN_PAD + T + FFN_PAD, FFN_TC), f32)] * 2 + [pltpu.VMEM((T + FFN_PAD, FFN_TC), f32)]
        + [pltpu.VMEM((ext, FFN_TC), f32)] * 2,
        compiler_params=_cparams(1),
    )(up, up, cw[0], cw[1], cw[2], cb, cw[0], cw[1], cw[2], cb, dact)
    dup = concat_cols("dup_concat", [res[0], res[1]], T)
    return (dup,) + tuple(jnp.concatenate([res[2 + k], res[6 + k]], axis=1) for k in range(4))


BQ = 256
BK = 256
assert BQ == BK
NT_DIMS = (((1,), (1,)), ((), ()))
TN_DIMS = (((0,), (0,)), ((), ()))


def _split_dot(x, u):
    x1 = x.astype(bf16)
    x2 = (x - x1.astype(f32)).astype(bf16)
    n = x.shape[0]
    y = jnp.dot(jnp.concatenate([x1, x2], axis=0), u, preferred_element_type=f32)
    return y[0:n] + y[n:2 * n]


def _log_sigmoids(z):
    sp = jnp.log(1.0 + jnp.exp(-jnp.abs(z)))
    lsp = jnp.minimum(z, 0.0) - sp
    return lsp, lsp - z


def _stack_heads(x):
    head1 = lax.broadcasted_iota(jnp.int32, x.shape, 1) >= HEAD_DIM
    zero = jnp.zeros_like(x)
    return jnp.concatenate([jnp.where(head1, zero, x), jnp.where(head1, x, zero)], axis=0)


def _unstack_heads(y):
    head1 = lax.broadcasted_iota(jnp.int32, (BQ, y.shape[1]), 1) >= HEAD_DIM
    return jnp.where(head1, y[BQ:2 * BQ], y[0:BQ])


def _attn_masks():
    row = lax.broadcasted_iota(jnp.int32, (2 * BQ, BK), 0)
    col = lax.broadcasted_iota(jnp.int32, (2 * BQ, BK), 1)
    ur = lax.broadcasted_iota(jnp.int32, (BK, BK), 0)
    uc = lax.broadcasted_iota(jnp.int32, (BK, BK), 1)
    return (row & (BQ - 1)) - col, (ur > uc).astype(bf16), (ur < uc).astype(bf16)


def attn_fwd(q, k, v, T, gather=None):
    nq = T // BQ
    n_g = 0 if gather is None else len(gather)

    def body(*refs):
        q_ref, k_ref, v_ref = refs[:3]
        o_ref, tot_ref = refs[3 + n_g:5 + n_g]
        p, i = pl.program_id(0), pl.program_id(1)
        if n_g:
            start, forward, finish = _two_level_gather(refs[5 + n_g:5 + 2 * n_g], *refs[5 + 2 * n_g:])
            pl.when(jnp.logical_and(p == 0, i == 0))(start)
        qs = _stack_heads(q_ref[...])
        diff, u_after, _ = _attn_masks()

        def step(jb, carry, diagonal):
            acc, c = carry
            ks = pl.multiple_of(jb * BK, BK)
            kb = k_ref[pl.ds(ks, BK), :]
            vb = v_ref[pl.ds(ks, BK), :]
            z = lax.dot_general(qs, kb, NT_DIMS, preferred_element_type=f32)
            lsp, lm = _log_sigmoids(z)
            if diagonal:
                m = diff > 0
                lm = jnp.where(m, lm, 0.0)
            a = jnp.exp(lsp + _split_dot(lm, u_after))
            if diagonal:
                a = jnp.where(m, a, 0.0)
            acc = acc + jnp.exp(c) * jnp.dot(a.astype(bf16), vb, preferred_element_type=f32)
            return acc, c + jnp.sum(lm, axis=-1, keepdims=True)

        carry = step(i, (jnp.zeros((2 * BQ, 128), f32), jnp.zeros((2 * BQ, 1), f32)), True)
        acc, c = lax.fori_loop(0, i, lambda t, cr: step(i - 1 - t, cr, False), carry)
        o_ref[...] = _unstack_heads(acc).astype(o_ref.dtype)
        tot_ref[...] = _unstack_heads(jnp.broadcast_to(c, (2 * BQ, 128)))
        if n_g:
            @pl.when(jnp.logical_and(p == 3, i == nq - 1))
            def _():
                forward()
                finish()

    blk = pl.BlockSpec((BQ, 128), lambda p, i: (i, p))
    full = pl.BlockSpec((T, 128), lambda p, i: (0, p))
    bufs = [] if gather is None else gather_buffers(gather)
    res = pl.pallas_call(
        body, name="attn_fwd_gather" if n_g else "attn_fwd", grid=(4, nq),
        in_specs=[blk, full, full] + [_ANY] * n_g, out_specs=[blk, blk] + [_ANY] * n_g,
        out_shape=[jax.ShapeDtypeStruct((T, 512), bf16), jax.ShapeDtypeStruct((T, 512), f32)]
        + [jax.ShapeDtypeStruct(b.shape, b.dtype) for b in bufs],
        input_output_aliases={3 + k: 2 + k for k in range(n_g)},
        scratch_shapes=_sems(n_g, N_DEV - 1) if n_g else [], compiler_params=_cparams(2),
    )(q, k, v, *bufs)
    return res[0], res[1], list(res[2:])


def attn_bwd(q, k, v, do, tot, T, scatter=None):
    nq = T // BQ
    n_s = 0 if scatter is None else len(scatter[0])
    modes = [] if scatter is None else list(scatter[1])

    def body(*refs):
        q_ref, k_ref, v_ref, do_ref, tot_ref = refs[:5]
        dq_ref, dk_ref, dv_ref = refs[5 + 2 * n_s:8 + 2 * n_s]
        p, i = pl.program_id(0), pl.program_id(1)
        if n_s:
            def copies():
                return _chip_copies(refs[5:5 + n_s], refs[8 + 2 * n_s:8 + 3 * n_s], modes, *refs[8 + 3 * n_s:])

            @pl.when(jnp.logical_and(p == 0, i == 0))
            def _():
                for cp in copies():
                    cp.start()

        @pl.when(i == 0)
        def _():
            dk_ref[...] = jnp.zeros_like(dk_ref)
            dv_ref[...] = jnp.zeros_like(dv_ref)

        qs = _stack_heads(q_ref[...])
        dos = _stack_heads(do_ref[...].astype(bf16))
        totv = tot_ref[...]
        tots = jnp.concatenate([totv[:, 0:1], totv[:, HEAD_DIM:HEAD_DIM + 1]], axis=0)
        diff, u_after, u_before = _attn_masks()

        def step(jb, carry, diagonal):
            dq, cl, cg = carry
            ks = pl.multiple_of(jb * BK, BK)
            kb = k_ref[pl.ds(ks, BK), :]
            vb = v_ref[pl.ds(ks, BK), :]
            z = lax.dot_general(qs, kb, NT_DIMS, preferred_element_type=f32)
            lsp, lsn = _log_sigmoids(z)
            lm = lsn
            if diagonal:
                m = diff > 0
                lm = jnp.where(m, lsn, 0.0)
            a = jnp.exp(lsp + _split_dot(lm, u_after))
            if diagonal:
                a = jnp.where(m, a, 0.0)
            g = a * lax.dot_general(dos, vb, NT_DIMS, preferred_element_type=f32)
            bb = jnp.exp(lsp)
            aa = g * jnp.exp(lsn) - _split_dot(g, u_before) * bb
            if diagonal:
                aa = jnp.where(m, aa, 0.0)
                bb = jnp.where(m, bb, 0.0)
            cl = cl + jnp.sum(lm, axis=-1, keepdims=True)
            f = jnp.exp(tots - cl)
            dz = (f * aa - cg * bb).astype(bf16)
            cg = cg + f * jnp.sum(g, axis=-1, keepdims=True)
            dq = dq + jnp.dot(dz, kb, preferred_element_type=f32)
            dk_ref[pl.ds(ks, BK), :] += lax.dot_general(dz, qs, TN_DIMS, preferred_element_type=f32)
            dv_ref[pl.ds(ks, BK), :] += lax.dot_general((f * a).astype(bf16), dos, TN_DIMS, preferred_element_type=f32)
            return dq, cl, cg

        zc = jnp.zeros((2 * BQ, 1), f32)
        carry = lax.fori_loop(0, i, lambda jb, cr: step(jb, cr, False), (jnp.zeros((2 * BQ, 128), f32), zc, zc))
        dq, _, _ = step(i, carry, True)
        dq_ref[...] = _unstack_heads(dq)
        if n_s:
            @pl.when(jnp.logical_and(p == 3, i == nq - 1))
            def _():
                for cp in copies():
                    cp.wait()

    blk = pl.BlockSpec((BQ, 128), lambda p, i: (i, p))
    full = pl.BlockSpec((T, 128), lambda p, i: (0, p))
    shp = jax.ShapeDtypeStruct((T, 512), f32)
    s_list = [] if scatter is None else list(scatter[0])
    bufs = chip_buffers(s_list, modes) if n_s else []
    res = pl.pallas_call(
        body, name="attn_bwd_scatter" if n_s else "attn_bwd", grid=(4, nq),
        in_specs=[blk, full, full, blk, blk] + [_ANY] * (2 * n_s), out_specs=[blk, full, full] + [_ANY] * n_s,
        out_shape=[shp, shp, shp] + [jax.ShapeDtypeStruct(b.shape, b.dtype) for b in bufs],
        input_output_aliases={5 + n_s + k: 3 + k for k in range(n_s)},
        scratch_shapes=_sems(n_s, 3) if n_s else [], compiler_params=_cparams(2),
    )(q, k, v, do, tot, *s_list, *bufs)
    return res[0], res[1], res[2], list(res[3:])


def loss_head(y, target, T, tm=256):
    D = y.shape[1]

    def body(y_ref, t_ref, dy_ref, l_ref):
        i = pl.program_id(0)
        err = y_ref[...] - t_ref[...]
        dy_ref[...] = err * (1.0 / D)
        part = 0.5 * jnp.sum(jnp.sum(err * err, axis=-1, keepdims=True) * (1.0 / D), axis=0, keepdims=True)

        @pl.when(i == 0)
        def _():
            l_ref[...] = jnp.zeros_like(l_ref)

        l_ref[...] += jnp.broadcast_to(part, l_ref.shape)

    spec = pl.BlockSpec((tm, D), lambda i: (i, 0))
    return pl.pallas_call(
        body, name="loss_head", grid=(T // tm,), in_specs=[spec, spec],
        out_specs=[spec, pl.BlockSpec((1, 128), lambda i: (0, 0))],
        out_shape=[jax.ShapeDtypeStruct((T, D), f32), jax.ShapeDtypeStruct((1, 128), f32)],
        compiler_params=_cparams(1),
    )(y, target)


def _row_tile(rows, cols, offset=0, max_elems=128 * 1024):
    best = None
    for t in range(16, rows + 1, 16):
        if rows % t == 0 and offset % t == 0 and t * cols <= max_elems:
            best = t
    return best if best is not None else rows


def adamw(name, w, m, v, parts, layer, prev=None, row_off=0):
    L, R, C = w.shape
    n_parts = parts.shape[0]
    tr = _row_tile(R, C, row_off, max_elems=256 * 1024)
    assert R % tr == 0 and row_off % tr == 0, (name, R, row_off, tr)

    def body(w_ref, m_ref, v_ref, p_ref, *rest):
        g_ref, d_ref, nm_ref, nv_ref = rest[-4:]
        g = p_ref[0].astype(f32)
        for s in range(1, n_parts):
            g = g + p_ref[s].astype(f32)
        wv = w_ref[...]
        mn = ADAM_B1 * m_ref[...] + (1.0 - ADAM_B1) * g
        vn = ADAM_B2 * v_ref[...] + (1.0 - ADAM_B2) * jnp.square(g)
        m_hat = mn / (1.0 - ADAM_B1 ** ADAM_STEP)
        v_hat = vn / (1.0 - ADAM_B2 ** ADAM_STEP)
        g_ref[...] = g
        d_ref[...] = -ADAM_LR * (m_hat / (jnp.sqrt(v_hat) + ADAM_EPS) + ADAM_WD * wv)
        nm_ref[...] = mn
        nv_ref[...] = vn

    spec = pl.BlockSpec((None, tr, C), lambda i: (layer, i, 0))
    shp = jax.ShapeDtypeStruct((L, R, C), f32)
    n_prev = 0 if prev is None else 4
    return pl.pallas_call(
        body, name=name, grid=(R // tr,),
        in_specs=[spec, spec, spec, pl.BlockSpec((n_parts, tr, C), lambda i: (0, row_off // tr + i, 0))] + [_ANY] * n_prev,
        out_specs=[spec] * 4, out_shape=[shp] * 4, input_output_aliases={4 + k: k for k in range(n_prev)},
        compiler_params=_cparams(1),
    )(w, m, v, parts, *(prev or ()))


REPL = ["ln1_g", "sg_ln_g", "sg_ln_b", "sg_w", "sg_b", "cv_b", "cv_ln_g", "cv_ln_b", "q_norm_g", "k_norm_g", "ln2_g",
        "ffn_conv_b"]
SMALL_SHARDED = ["b_gate", "cv_w", "ffn_conv_w"]
BIG = ["w_in", "w_a_out", "w_b_out", "w_c_out", "w_up", "w_out", "w_down"]
COL_SHARDED = ["w_a_out", "w_b_out", "w_c_out"]
TRANSPOSED = ["w_in", "w_up"]


def _pack(arrs, rows):
    flat = jnp.concatenate([a.reshape(-1) for a in arrs])
    return jnp.pad(flat, (0, rows * 128 - flat.shape[0])).reshape(rows, 128)


def _pack_layers(arrs, rows):
    flat = jnp.concatenate([a.reshape(a.shape[0], -1) for a in arrs], axis=1)
    return jnp.pad(flat, ((0, 0), (0, rows * 128 - flat.shape[1]))).reshape(flat.shape[0], rows, 128)


def _unpack_layers(packed, shapes):
    flat = packed.reshape(packed.shape[0], -1)
    out, pos = [], 0
    for s in shapes:
        n = 1
        for d in s[1:]:
            n *= d
        out.append(flat[:, pos:pos + n].reshape(s))
        pos += n
    return out


def _rows_for(shapes, mult):
    n = 0
    for s in shapes:
        k = 1
        for d in s:
            k *= d
        n += k
    rows = -(-n // 128)
    return -(-rows // mult) * mult


W_NAMES = ['ln1_g', 'w_in', 'b_gate', 'sg_ln_g', 'sg_ln_b', 'sg_w', 'sg_b', 'w_a_out', 'cv_w', 'cv_b', 'cv_ln_g', 'cv_ln_b',
           'w_b_out', 'q_norm_g', 'k_norm_g', 'w_c_out', 'w_out', 'ln2_g', 'w_up', 'ffn_conv_w', 'ffn_conv_b', 'w_down']


def _forward_layer(x, P, T, late=None):
    D = x.shape[1]
    sv = {"x0": x}
    (h1,) = rowwise("rms_fwd", f_rms, [(x, D, 0)], [(P["ln1_g"], (1, D), None)], [(D, D, 0, bf16)], T, 256)
    z = mm_nt("in_proj", h1, P["w_in_t"], bf16, tko=1664, tc=D)
    sv["h1"], sv["z"] = h1, z
    (ya_in,) = rowwise("sgu_fwd", f_sgu, [(z, 512, 0), (z, 512, 1)],
                       [(P["sg_ln_g"], (1, 512), None), (P["sg_ln_b"], (1, 512), None),
                        (P["sg_w"], (8, CHUNK, CHUNK), None), (P["sg_b_t"], (CHUNK, 8), None)],
                       [(512, 512, 0, bf16)], T, CHUNK)
    (c0,) = rowwise("glu_fwd", f_glu, [(z, 512, 2), (z, 512, 3)], [], [(512, 512, 0, f32)], T, 256)
    c1 = conv31_fwd(c0, P["cv_w"], P["cv_b"], T)
    (c3,) = rowwise("lnsilu_fwd", f_lnsilu, [(c1, 512, 0)], [(P["cv_ln_g"], (1, 512), None), (P["cv_ln_b"], (1, 512), None)],
                    [(512, 512, 0, bf16)], T, 256)
    q8, kn, vb = rowwise("qkv_fwd", f_qkv, [(z, 512, 4), (z, 512, 5), (z, 512, 6)],
                         [(P["q_norm_g"], (1, HEAD_DIM), None), (P["k_norm_g"], (1, HEAD_DIM), None)],
                         [(512, 512, 0, bf16)] * 3, T, 256)
    o, tot, _ = attn_fwd(q8, kn, vb, T)
    if late is not None:
        P.update(late(o))
    ya, yb, yc = mm_many("branch_out", [ya_in, c3, o], [P["w_a_out"], P["w_b_out"], P["w_c_out"]], f32)
    (merged,) = rowwise("merge_fwd", f_merge,
                        [(z, 512, lambda c: 7 + c), (z, 512, lambda c: 9 + c), (z, 512, lambda c: 11 + c),
                         (ya, 512, lambda c: c), (yb, 512, lambda c: c), (yc, 512, lambda c: c)],
                        [(P["b_gate"][k], (1, 512), lambda c: c) for k in range(3)],
                        [(D, 512, lambda c: c, bf16)], T, 256, ncol=2)
    x1 = mm_nn("out_proj", merged, P["w_out"], f32, add=x)
    sv.update(ya_in=ya_in, ya=ya, c0=c0, c1=c1, c3=c3, yb=yb, q8=q8, kn=kn, vb=vb, o=o, tot=tot, yc=yc, merged=merged, x1=x1)
    (h2,) = rowwise("rms_fwd", f_rms, [(x1, D, 0)], [(P["ln2_g"], (1, D), None)], [(D, D, 0, bf16)], T, 256)
    up = mm_nt("up_proj", h2, P["w_up_t"], bf16, tko=1408, tc=D)
    act = ffn_act_fwd(up, P["ffn_conv_w"], P["ffn_conv_b"], T)
    x2 = mm_nn("down_proj", act, P["w_down"], f32, add=x1)
    sv.update(h2=h2, up=up, act=act)
    return x2, sv


def _backward_layer(dx2, P, sv, T, scatter=None):
    D = dx2.shape[1]
    G = {}
    G["w_down"] = mm_tn("dw_down", sv["act"], dx2, tk=1408)
    dact = mm_nt("d_act", dx2, P["w_down"], f32, tko=1408, tc=D)
    dup, dcw0, dcw1, dcw2, G["ffn_conv_b"] = ffn_act_bwd(sv["up"], P["ffn_conv_w"], P["ffn_conv_b"], dact, T)
    G["ffn_conv_w"] = jnp.concatenate([dcw0, dcw1, dcw2], axis=0)
    G["w_up"] = mm_tn("dw_up", dup, sv["h2"], tk=1408)
    dh2 = mm_nn("d_h2", dup, P["w_up_t"], f32, tm=512)
    dx1, G["ln2_g"] = rowwise_bwd("rms_bwd", f_rms, [(sv["x1"], D, 0)], [(P["ln2_g"], (1, D), None)], [(dh2, D, 0)],
                                  [f32], T, 256, adds=[(dx2, D, 0)])
    G["w_out"] = mm_tn("dw_out", sv["merged"], dx1, tk=1024)
    dmerged = mm_nt("d_merged", dx1, P["w_out"], f32)
    z = sv["z"]
    dg0, dg1, dg2, dya, dyb, dyc, db0, db1, db2 = rowwise_bwd(
        "merge_bwd", f_merge,
        [(z, 512, lambda c: 7 + c), (z, 512, lambda c: 9 + c), (z, 512, lambda c: 11 + c),
         (sv["ya"], 512, lambda c: c), (sv["yb"], 512, lambda c: c), (sv["yc"], 512, lambda c: c)],
        [(P["b_gate"][k], (1, 512), lambda c: c) for k in range(3)],
        [(dmerged, 512, lambda c: c)], [bf16] * 6, T, 256, ncol=2)
    G["b_gate"] = jnp.concatenate([db0, db1, db2], axis=0)
    dya_in, dc3, do = mm_many("d_branch_in", [dya, dyb, dyc], [P["w_a_out"], P["w_b_out"], P["w_c_out"]], f32,
                              transpose_w=True)
    G["w_c_out"] = mm_tn("dw_c_out", sv["o"], dyc, n_slab=CHUNK)
    dq8, dkn, dvb, received = attn_bwd(sv["q8"], sv["kn"], sv["vb"], do, sv["tot"], T, scatter=scatter)
    dzq, dzk, dzv, G["q_norm_g"], G["k_norm_g"] = rowwise_bwd(
        "qkv_bwd", f_qkv, [(z, 512, 4), (z, 512, 5), (z, 512, 6)],
        [(P["q_norm_g"], (1, HEAD_DIM), None), (P["k_norm_g"], (1, HEAD_DIM), None)],
        [(dq8, 512, 0), (dkn, 512, 0), (dvb, 512, 0)], [bf16] * 3, T, 256)
    G["w_b_out"] = mm_tn("dw_b_out", sv["c3"], dyb, n_slab=CHUNK)
    dc1, G["cv_ln_g"], G["cv_ln_b"] = rowwise_bwd(
        "lnsilu_bwd", f_lnsilu, [(sv["c1"], 512, 0)], [(P["cv_ln_g"], (1, 512), None), (P["cv_ln_b"], (1, 512), None)],
        [(dc3, 512, 0)], [f32], T, 256)
    dc0, G["cv_w"], G["cv_b"] = conv31_bwd(sv["c0"], P["cv_w"], dc1, T)
    dzp, dzgl = rowwise_bwd("glu_bwd", f_glu, [(z, 512, 2), (z, 512, 3)], [], [(dc0, 512, 0)], [bf16] * 2, T, 256)
    G["w_a_out"] = mm_tn("dw_a_out", sv["ya_in"], dya, n_slab=CHUNK)
    dzu, dzv_a, G["sg_ln_g"], G["sg_ln_b"], G["sg_w"], dsbt = rowwise_bwd(
        "sgu_bwd", f_sgu, [(z, 512, 0), (z, 512, 1)],
        [(P["sg_ln_g"], (1, 512), None), (P["sg_ln_b"], (1, 512), None), (P["sg_w"], (8, CHUNK, CHUNK), None),
         (P["sg_b_t"], (CHUNK, 8), None)],
        [(dya_in, 512, 0)], [bf16] * 2, T, CHUNK)
    G["sg_b"] = dsbt.T
    dz = concat_cols("dz_concat", [dzu, dzv_a, dzp, dzgl, dzq, dzk, dzv, dg0, dg1, dg2], T)
    G["w_in"] = mm_tn("dw_in", dz, sv["h1"], tk=1664)
    dh1 = mm_nn("d_h1", dz, P["w_in_t"], f32, tm=512)
    dx0, G["ln1_g"] = rowwise_bwd("rms_bwd", f_rms, [(sv["x0"], D, 0)], [(P["ln1_g"], (1, D), None)], [(dh1, D, 0)],
                                  [f32], T, 256, adds=[(dx1, D, 0)])
    return dx0, G, received


def kernel(x, ln1_g, w_in, b_gate, sg_ln_g, sg_ln_b, sg_w, sg_b, w_a_out, cv_w, cv_b, cv_ln_g, cv_ln_b, w_b_out, q_norm_g, k_norm_g, w_c_out, w_out, ln2_g, w_up, ffn_conv_w, ffn_conv_b, w_down, loss_target, m_ln1_g, m_w_in, m_b_gate, m_sg_ln_g, m_sg_ln_b, m_sg_w, m_sg_b, m_w_a_out, m_cv_w, m_cv_b, m_cv_ln_g, m_cv_ln_b, m_w_b_out, m_q_norm_g, m_k_norm_g, m_w_c_out, m_w_out, m_ln2_g, m_w_up, m_ffn_conv_w, m_ffn_conv_b, m_w_down, v_ln1_g, v_w_in, v_b_gate, v_sg_ln_g, v_sg_ln_b, v_sg_w, v_sg_b, v_w_a_out, v_cv_w, v_cv_b, v_cv_ln_g, v_cv_ln_b, v_w_b_out, v_q_norm_g, v_k_norm_g, v_w_c_out, v_w_out, v_ln2_g, v_w_up, v_ffn_conv_w, v_ffn_conv_b, v_w_down):
    W = dict(ln1_g=ln1_g, w_in=w_in, b_gate=b_gate, sg_ln_g=sg_ln_g, sg_ln_b=sg_ln_b, sg_w=sg_w, sg_b=sg_b, w_a_out=w_a_out,
             cv_w=cv_w, cv_b=cv_b, cv_ln_g=cv_ln_g, cv_ln_b=cv_ln_b, w_b_out=w_b_out, q_norm_g=q_norm_g, k_norm_g=k_norm_g,
             w_c_out=w_c_out, w_out=w_out, ln2_g=ln2_g, w_up=w_up, ffn_conv_w=ffn_conv_w, ffn_conv_b=ffn_conv_b, w_down=w_down)
    M = dict(ln1_g=m_ln1_g, w_in=m_w_in, b_gate=m_b_gate, sg_ln_g=m_sg_ln_g, sg_ln_b=m_sg_ln_b, sg_w=m_sg_w, sg_b=m_sg_b,
             w_a_out=m_w_a_out, cv_w=m_cv_w, cv_b=m_cv_b, cv_ln_g=m_cv_ln_g, cv_ln_b=m_cv_ln_b, w_b_out=m_w_b_out,
             q_norm_g=m_q_norm_g, k_norm_g=m_k_norm_g, w_c_out=m_w_c_out, w_out=m_w_out, ln2_g=m_ln2_g, w_up=m_w_up,
             ffn_conv_w=m_ffn_conv_w, ffn_conv_b=m_ffn_conv_b, w_down=m_w_down)
    V = dict(ln1_g=v_ln1_g, w_in=v_w_in, b_gate=v_b_gate, sg_ln_g=v_sg_ln_g, sg_ln_b=v_sg_ln_b, sg_w=v_sg_w, sg_b=v_sg_b,
             w_a_out=v_w_a_out, cv_w=v_cv_w, cv_b=v_cv_b, cv_ln_g=v_cv_ln_g, cv_ln_b=v_cv_ln_b, w_b_out=v_w_b_out,
             q_norm_g=v_q_norm_g, k_norm_g=v_k_norm_g, w_c_out=v_w_c_out, w_out=v_w_out, ln2_g=v_ln2_g, w_up=v_w_up,
             ffn_conv_w=v_ffn_conv_w, ffn_conv_b=v_ffn_conv_b, w_down=v_w_down)
    T, D = x.shape[1], x.shape[2]
    L = DEPTH
    xs = x.reshape(T, D)
    target = loss_target.reshape(T, D)

    ss_shapes = [W[n].shape for n in SMALL_SHARDED]
    ss_rows = _rows_for(ss_shapes, 8)
    (ss_all,), small_token = exchange("gather_small", [_pack([W[n] for n in SMALL_SHARDED], ss_rows)], ["gather"])
    full_small = {}
    pos = 0
    for n in SMALL_SHARDED:
        s = W[n].shape
        cnt = s[0] * s[1] * s[2]
        part = ss_all.reshape(N_DEV, -1)[:, pos:pos + cnt].reshape((N_DEV,) + s)
        full_small[n] = jnp.transpose(part, (1, 2, 0, 3)).reshape(s[0], s[1], N_DEV * s[2])
        pos += cnt

    params, saved = [], []
    cur = xs
    Wt = {n: (jnp.transpose(W[n], (0, 2, 1)) if n in TRANSPOSED else W[n]) for n in BIG}

    def local_slabs(l):
        return [Wt[n][l].astype(bf16) for n in BIG]

    def weights_of(names, slabs):
        out = {}
        for n, s in zip(names, slabs):
            if n in TRANSPOSED:
                out[n + "_t"] = s.reshape(-1, s.shape[-1])
            else:
                out[n] = assemble("assemble_" + n, s) if n in COL_SHARDED else s.reshape(-1, s.shape[-1])
        return out

    LATE = [n for n in BIG if n != "w_in"]
    def behind(token, blocks):
        return [blocks[0] + token[0, 0].astype(blocks[0].dtype)] + list(blocks[1:])

    first, token = gather2("gather_w_in", behind(small_token, [local_slabs(0)[0]]))
    late0 = gather_start("gather_start_0", behind(token, local_slabs(0)[1:]))
    in_flight, token = {}, late0[3]
    for l in range(1, L):
        in_flight[l] = gather_start("gather_start_%d" % l, behind(token, local_slabs(l)))
        token = in_flight[l][3]
    start_token = token[0, 0]
    for l in range(L):
        if l == 0:
            P = weights_of(["w_in"], first)
            late = lambda o: weights_of(LATE, gather_wait("gather_wait_0", late0[0], late0[1], late0[2], o))
        else:
            send_sems, recv_sems, bufs, _ = in_flight[l]
            P = weights_of(BIG, gather_wait("gather_wait_%d" % l, send_sems, recv_sems, bufs, cur))
            late = None
        for n in REPL:
            P[n] = W[n][l]
        for n in ("ln1_g", "sg_ln_g", "sg_ln_b", "cv_b", "cv_ln_g", "cv_ln_b", "q_norm_g", "k_norm_g", "ln2_g", "ffn_conv_b"):
            P[n] = P[n].reshape(1, -1)
        P["sg_b_t"] = P["sg_b"].T
        P["cv_w"] = full_small["cv_w"][l]
        P["b_gate"] = [full_small["b_gate"][l][k:k + 1] for k in range(3)]
        P["ffn_conv_w"] = [full_small["ffn_conv_w"][l][k:k + 1] for k in range(3)]
        if l == 0:
            P["ln1_g"] = P["ln1_g"] + start_token
        cur, sv = _forward_layer(cur, P, T, late=late)
        params.append(P)
        saved.append(sv)

    dy, loss_row = loss_head(cur, target, T)
    loss = lax.psum(loss_row[0, 0], ("x", "y", "c"))

    repl_shapes = [W[n].shape for n in REPL]
    repl_rows = _rows_for([s[1:] for s in repl_shapes], 16)
    ssl_rows = _rows_for([s[1:] for s in ss_shapes], 16)
    state = {"repl": [_pack_layers([X[n] for n in REPL], repl_rows) for X in (W, M, V)],
             "ss": [_pack_layers([X[n] for n in SMALL_SHARDED], ssl_rows) for X in (W, M, V)]}
    big_rows, big_off = {}, {}
    for n in BIG:
        big_off[n] = sum(big_rows.values())
        big_rows[n] = W[n].shape[1] * W[n].shape[2] // D
        state[n] = [(jnp.transpose(X[n], (0, 2, 1)) if n in TRANSPOSED else X[n]).reshape(L, big_rows[n], D)
                    for X in (W, M, V)]
    done = {n: None for n in state}

    def update(layer, received):
        for n in BIG:
            done[n] = adamw("adamw_" + n, *state[n], received[0], layer, done[n], row_off=big_off[n])
        done["repl"] = adamw("adamw_repl", *state["repl"], received[1], layer, done["repl"])
        done["ss"] = adamw("adamw_ss", *state["ss"], received[2], layer, done["ss"])

    def reduce_in_chip(G):
        big = [G[n].reshape(4, 2, big_rows[n], D) for n in BIG]
        ss_parts = []
        for n in SMALL_SHARDED:
            k, c = W[n].shape[1:]
            ss_parts.append(jnp.transpose(G[n].reshape(k, N_DEV, c), (1, 0, 2)).reshape(N_DEV, k * c))
        ss_send = jnp.concatenate(ss_parts, axis=1)
        ss_send = jnp.pad(ss_send, ((0, 0), (0, ssl_rows * 128 - ss_send.shape[1]))).reshape(4, 2, ssl_rows, 128)
        repl = _pack([G[n] for n in REPL], repl_rows)
        core = lax.axis_index("c")
        kept = jnp.concatenate([lax.dynamic_index_in_dim(b, core, 1, keepdims=False) for b in big], axis=1)
        ss_kept = lax.dynamic_index_in_dim(ss_send, core, 1, keepdims=False)
        got, repl_got, ss_got = pair_exchange("pair_exchange", big, [repl, ss_send], ["gather", "scatter"])
        return [add2("pair_add_big", kept, got), add2("pair_add_repl", repl, repl_got), add2("pair_add_ss", ss_kept, ss_got)]

    chip_modes = ["scatter", "gather", "scatter"]
    dcur = dy
    send = None
    for l in reversed(range(L)):
        dcur, G, recv = _backward_layer(dcur, params[l], saved[l], T, scatter=send)
        if send is not None:
            update(l + 1, recv)
        send = (reduce_in_chip(G), chip_modes)
    update(0, chip_exchange("chip_exchange_last", send[0], send[1]))

    results = [{}, {}, {}, {}]
    for k in range(4):
        for n, a in zip(REPL, _unpack_layers(done["repl"][k], repl_shapes)):
            results[k][n] = a
        for n, a in zip(SMALL_SHARDED, _unpack_layers(done["ss"][k], ss_shapes)):
            results[k][n] = a
        for n in BIG:
            results[k][n] = jnp.transpose(done[n][k], (0, 2, 1)) if n in TRANSPOSED else done[n][k].reshape(W[n].shape)
    out = [loss, dcur.reshape(1, T, D)]
    for k in range(4):
        out += [results[k][n] for n in W_NAMES]
    return tuple(out)
```

```python
import functools

import jax
import jax.numpy as jnp
from jax import lax
from jax.experimental import pallas as pl
from jax.experimental.pallas import tpu as pltpu

f32 = jnp.float32
bf16 = jnp.bfloat16

EPS = 1e-6
N_DEV = 8
DEPTH = 4
CHUNK = 128
HEAD_DIM = 64
N_HEADS = 8
CV_KERNEL = 31
VMEM_LIMIT_BYTES = 56 * 2 ** 20

ADAM_LR = 0.001
ADAM_B1 = 0.9
ADAM_B2 = 0.999
ADAM_EPS = 1e-08
ADAM_WD = 0.01
ADAM_STEP = 10

MESH = pl.DeviceIdType.MESH


def _cparams(n_grid):
    return pltpu.CompilerParams(dimension_semantics=("arbitrary",) * n_grid, vmem_limit_bytes=VMEM_LIMIT_BYTES)


def exchange(name, arrays, modes):
    n = len(arrays)

    def body(*refs):
        copies = _direct_copies(refs[:n], refs[n:2 * n], modes, *refs[2 * n + 1:])
        for cp in copies:
            cp.start()
        for cp in copies:
            cp.wait()
        refs[2 * n][...] = jnp.zeros_like(refs[2 * n])

    res = pl.pallas_call(
        body, name=name, out_shape=_exchange_out_shapes(arrays, modes) + [jax.ShapeDtypeStruct((8, 128), f32)],
        in_specs=[_ANY] * n, out_specs=[_ANY] * n + [pl.BlockSpec(memory_space=pltpu.VMEM)],
        scratch_shapes=_exchange_sems(n),
    )(*arrays)
    return list(res[:n]), res[n]


_ANY = pl.BlockSpec(memory_space=pl.ANY)


def _exchange_out_shapes(arrays, modes):
    return [jax.ShapeDtypeStruct((N_DEV,) + tuple(a.shape) if m == "gather" else tuple(a.shape), a.dtype)
            for a, m in zip(arrays, modes)]


def _exchange_sems(n):
    return [pltpu.SemaphoreType.DMA((n, N_DEV - 1)), pltpu.SemaphoreType.DMA((n, N_DEV - 1)), pltpu.SemaphoreType.DMA((n,))]


def _direct_copies(ins, outs, modes, send_sems, recv_sems, local_sems):
    x, y, c = lax.axis_index("x"), lax.axis_index("y"), lax.axis_index("c")
    me = 4 * x + 2 * y + c
    copies = []
    for k in range(len(ins)):
        src_mine = ins[k] if modes[k] == "gather" else ins[k].at[me]
        copies.append(pltpu.make_async_copy(src_mine, outs[k].at[me], local_sems.at[k]))
        for r in range(1, N_DEV):
            px = 1 - x if r & 4 else x
            py = 1 - y if r & 2 else y
            pc = 1 - c if r & 1 else c
            src = ins[k] if modes[k] == "gather" else ins[k].at[4 * px + 2 * py + pc]
            copies.append(pltpu.make_async_remote_copy(
                src_ref=src, dst_ref=outs[k].at[me], send_sem=send_sems.at[k, r - 1], recv_sem=recv_sems.at[k, r - 1],
                device_id=(px, py, pc), device_id_type=MESH))
    return copies


def own_slot_buffer(block, n_slots, index):
    return lax.dynamic_update_slice(lax.empty((n_slots,) + tuple(block.shape), block.dtype), block[None],
                                    (index,) + (0,) * block.ndim)


def _sems(n, m):
    return [pltpu.SemaphoreType.DMA((n, m)), pltpu.SemaphoreType.DMA((n, m))]


def _two_level_gather(bufs, send_sems, recv_sems):
    x, y, c = lax.axis_index("x"), lax.axis_index("y"), lax.axis_index("c")
    me = 4 * x + 2 * y + c
    sibling = (x, y, 1 - c)
    chips = [(1 - x, y), (x, 1 - y), (1 - x, 1 - y)]

    def slot(px, py, pc):
        return 4 * px + 2 * py + pc

    def copy(k, sem, block, to):
        rows = bufs[k].at[block]
        return pltpu.make_async_remote_copy(src_ref=rows, dst_ref=rows, send_sem=send_sems.at[k, sem],
                                            recv_sem=recv_sems.at[k, sem], device_id=to, device_id_type=MESH)

    def first(k):
        return [copy(k, 0, me, sibling)] + [copy(k, 1 + j, me, (*chip, c)) for j, chip in enumerate(chips)]

    def passed_on(k, j):
        return copy(k, 4 + j, slot(*chips[j], c), sibling)

    def start():
        for k in range(len(bufs)):
            for cp in first(k):
                cp.start()

    def forward():
        for k in range(len(bufs)):
            for j in range(3):
                copy(k, 1 + j, slot(*chips[j], c), sibling).wait_recv()
                passed_on(k, j).start()

    def finish():
        for k in range(len(bufs)):
            copy(k, 0, slot(x, y, 1 - c), sibling).wait_recv()
            for j, chip in enumerate(chips):
                copy(k, 4 + j, slot(*chip, 1 - c), sibling).wait_recv()
            for cp in first(k) + [passed_on(k, j) for j in range(3)]:
                cp.wait_send()

    return start, forward, finish


def pair_exchange(name, big, smalls, modes):
    nb, ns = len(big), len(smalls)
    rows = [a.shape[2] for a in big]
    total, width = sum(rows), big[0].shape[3]
    n = nb + ns

    def body(*refs):
        ins, outs = refs[:n], refs[n:n + 1 + ns]
        send_sems, recv_sems = refs[-2:]
        x, y, c = lax.axis_index("x"), lax.axis_index("y"), lax.axis_index("c")
        copies = []

        def remote(k, src, dst):
            copies.append(pltpu.make_async_remote_copy(src_ref=src, dst_ref=dst, send_sem=send_sems.at[k, 0],
                                                       recv_sem=recv_sems.at[k, 0], device_id=(x, y, 1 - c),
                                                       device_id_type=MESH))

        off = 0
        for k in range(nb):
            remote(k, ins[k].at[:, 1 - c], outs[0].at[:, pl.ds(off, rows[k]), :])
            off += rows[k]
        for j in range(ns):
            remote(nb + j, ins[nb + j].at[:, 1 - c] if modes[j] == "scatter" else ins[nb + j], outs[1 + j])
        for cp in copies:
            cp.start()
        for cp in copies:
            cp.wait()

    out_shape = [jax.ShapeDtypeStruct((4, total, width), big[0].dtype)]
    for a, m in zip(smalls, modes):
        out_shape.append(jax.ShapeDtypeStruct((4,) + tuple(a.shape[2:]) if m == "scatter" else tuple(a.shape), a.dtype))
    return pl.pallas_call(
        body, name=name, out_shape=out_shape, in_specs=[_ANY] * n, out_specs=[_ANY] * len(out_shape),
        scratch_shapes=_sems(n, 1),
    )(*big, *smalls)


def add2(name, a, b):
    shape = a.shape
    a2, b2 = a.reshape(-1, shape[-1]), b.reshape(-1, shape[-1])
    R, C = a2.shape
    tr = _row_tile(R, C, max_elems=1200 * 1024)

    def body(a_ref, b_ref, o_ref):
        o_ref[...] = (a_ref[...].astype(f32) + b_ref[...].astype(f32)).astype(o_ref.dtype)

    spec = pl.BlockSpec((tr, C), lambda i: (i, 0))
    return pl.pallas_call(
        body, name=name, grid=(R // tr,), in_specs=[spec, spec], out_specs=spec,
        out_shape=jax.ShapeDtypeStruct((R, C), a.dtype), compiler_params=_cparams(1),
    )(a2, b2).reshape(shape)


def chip_buffers(arrays, modes):
    mine = 2 * lax.axis_index("x") + lax.axis_index("y")
    return [own_slot_buffer(lax.dynamic_index_in_dim(a, mine, 0, keepdims=False) if m == "scatter" else a, 4, mine)
            for a, m in zip(arrays, modes)]


def _chip_copies(ins, bufs, modes, send_sems, recv_sems):
    x, y, c = lax.axis_index("x"), lax.axis_index("y"), lax.axis_index("c")
    mine = 2 * x + y
    copies = []
    for k in range(len(ins)):
        for r in range(1, 4):
            px = 1 - x if r & 2 else x
            py = 1 - y if r & 1 else y
            src = ins[k].at[2 * px + py] if modes[k] == "scatter" else ins[k]
            copies.append(pltpu.make_async_remote_copy(
                src_ref=src, dst_ref=bufs[k].at[mine], send_sem=send_sems.at[k, r - 1], recv_sem=recv_sems.at[k, r - 1],
                device_id=(px, py, c), device_id_type=MESH))
    return copies


def chip_exchange(name, arrays, modes):
    n = len(arrays)

    def body(*refs):
        copies = _chip_copies(refs[:n], refs[2 * n:3 * n], modes, *refs[3 * n:])
        for cp in copies:
            cp.start()
        for cp in copies:
            cp.wait()

    bufs = chip_buffers(arrays, modes)
    return pl.pallas_call(
        body, name=name, out_shape=[jax.ShapeDtypeStruct(b.shape, b.dtype) for b in bufs],
        in_specs=[_ANY] * (2 * n), out_specs=[_ANY] * n, input_output_aliases={n + k: k for k in range(n)},
        scratch_shapes=_sems(n, 3),
    )(*arrays, *bufs)


def gather_buffers(blocks):
    me = 4 * lax.axis_index("x") + 2 * lax.axis_index("y") + lax.axis_index("c")
    return [own_slot_buffer(b, N_DEV, me) for b in blocks]


_HBM = pl.BlockSpec(memory_space=pltpu.HBM)
_SEM = pl.BlockSpec(memory_space=pltpu.SEMAPHORE)
_SPLIT_PARAMS = pltpu.CompilerParams(has_side_effects=pltpu.SideEffectType.DATAFLOW_SIDE_EFFECTING)


def _split_gather_copies(bufs, send_sems, recv_sems):
    x, y, c = lax.axis_index("x"), lax.axis_index("y"), lax.axis_index("c")
    me = 4 * x + 2 * y + c
    copies = []
    for k in range(len(bufs)):
        rows = bufs[k].at[me]
        for r in range(1, N_DEV):
            peer = (1 - x if r & 4 else x, 1 - y if r & 2 else y, 1 - c if r & 1 else c)
            copies.append(pltpu.make_async_remote_copy(
                src_ref=rows, dst_ref=rows, send_sem=send_sems.at[(N_DEV - 1) * k + r - 1],
                recv_sem=recv_sems.at[(N_DEV - 1) * k + r - 1], device_id=peer, device_id_type=MESH))
    return copies


def gather_start(name, blocks):
    n = len(blocks)
    bufs = gather_buffers(blocks)

    def body(*refs):
        for cp in _split_gather_copies(refs[n + 2:2 * n + 2], refs[n], refs[n + 1]):
            cp.start()
        refs[2 * n + 2][...] = jnp.zeros_like(refs[2 * n + 2])

    n_sem = n * (N_DEV - 1)
    res = pl.pallas_call(
        body, name=name,
        out_shape=(pltpu.SemaphoreType.DMA((n_sem,)), pltpu.SemaphoreType.DMA((n_sem,)),
                   *[pltpu.HBM(b.shape, b.dtype) for b in bufs], jax.ShapeDtypeStruct((8, 128), f32)),
        in_specs=[_HBM] * n, out_specs=(_SEM, _SEM, *[_HBM] * n, pl.BlockSpec(memory_space=pltpu.VMEM)),
        input_output_aliases={k: 2 + k for k in range(n)}, compiler_params=_SPLIT_PARAMS,
    )(*[pltpu.with_memory_space_constraint(b, pltpu.HBM) for b in bufs])
    return res[0], res[1], list(res[2:2 + n]), res[2 + n]


def gather_wait(name, send_sems, recv_sems, bufs, after):
    n = len(bufs)

    def body(*refs):
        for cp in _split_gather_copies(refs[:n], refs[n], refs[n + 1]):
            cp.wait_send()
            cp.wait_recv()

    return list(pl.pallas_call(
        body, name=name, out_shape=tuple(pltpu.HBM(b.shape, b.dtype) for b in bufs),
        in_specs=[_HBM] * n + [_SEM, _SEM, _ANY], out_specs=tuple([_HBM] * n),
        input_output_aliases={k: k for k in range(n)}, compiler_params=_SPLIT_PARAMS,
    )(*bufs, send_sems, recv_sems, after))


def gather2(name, arrays):
    n = len(arrays)

    def body(*refs):
        start, forward, finish = _two_level_gather(refs[n:2 * n], *refs[2 * n + 1:])
        start()
        forward()
        finish()
        refs[2 * n][...] = jnp.zeros_like(refs[2 * n])

    bufs = gather_buffers(arrays)
    res = pl.pallas_call(
        body, name=name, out_shape=[jax.ShapeDtypeStruct(b.shape, b.dtype) for b in bufs] + [jax.ShapeDtypeStruct((8, 128), f32)],
        in_specs=[_ANY] * n, out_specs=[_ANY] * n + [pl.BlockSpec(memory_space=pltpu.VMEM)],
        input_output_aliases={k: k for k in range(n)}, scratch_shapes=_sems(n, N_DEV - 1),
    )(*bufs)
    return list(res[:n]), res[n]


def assemble(name, slabs):
    _, K, Ns = slabs.shape
    g = N_DEV if Ns % 128 == 0 else 2

    def body(w_ref, o_ref):
        o_ref[...] = jnp.concatenate([w_ref[s] for s in range(g)], axis=1)

    return pl.pallas_call(
        body, name=name, grid=(N_DEV // g,),
        in_specs=[pl.BlockSpec((g, K, Ns), lambda m: (m, 0, 0))],
        out_specs=pl.BlockSpec((K, g * Ns), lambda m: (0, m)),
        out_shape=jax.ShapeDtypeStruct((K, N_DEV * Ns), slabs.dtype),
        compiler_params=_cparams(1),
    )(slabs)


def mm_nn(name, a, b, out_dtype, add=None, tm=1024, tn=512, tk=None):
    M, K = a.shape
    N = b.shape[1]
    tm, tn = min(tm, M), min(tn, N)
    tk = K if tk is None else tk
    nk = K // tk
    has_add = add is not None

    def body(a_ref, b_ref, *rest):
        o_ref = rest[1] if has_add else rest[0]
        part = jnp.dot(a_ref[...].astype(bf16), b_ref[...].astype(bf16), preferred_element_type=f32)
        if nk == 1:
            o_ref[...] = (part + rest[0][...] if has_add else part).astype(o_ref.dtype)
            return
        acc_ref = rest[-1]
        k = pl.program_id(2)

        @pl.when(k == 0)
        def _():
            acc_ref[...] = part + rest[0][...] if has_add else part

        @pl.when(k > 0)
        def _():
            acc_ref[...] += part

        @pl.when(k == nk - 1)
        def _():
            o_ref[...] = acc_ref[...].astype(o_ref.dtype)

    in_specs = [pl.BlockSpec((tm, tk), lambda i, j, k: (i, k)), pl.BlockSpec((tk, tn), lambda i, j, k: (k, j))]
    ops = [a, b]
    if has_add:
        in_specs.append(pl.BlockSpec((tm, tn), lambda i, j, k: (i, j)))
        ops.append(add)
    return pl.pallas_call(
        body, name=name, grid=(M // tm, N // tn, nk), in_specs=in_specs,
        out_specs=pl.BlockSpec((tm, tn), lambda i, j, k: (i, j)),
        out_shape=jax.ShapeDtypeStruct((M, N), out_dtype),
        scratch_shapes=[pltpu.VMEM((tm, tn), f32)] if nk > 1 else [], compiler_params=_cparams(3),
    )(*ops)


def mm_many(name, xs, ws, out_dtype, transpose_w=False, tm=512):
    n = len(xs)
    M = xs[0].shape[0]
    tm = min(tm, M)
    dims = NT_DIMS if transpose_w else (((1,), (0,)), ((), ()))

    def body(*refs):
        for k in range(n):
            refs[2 * n + k][...] = lax.dot_general(refs[k][...].astype(bf16), refs[n + k][...].astype(bf16), dims,
                                                   preferred_element_type=f32).astype(out_dtype)

    n_out = [w.shape[0] if transpose_w else w.shape[1] for w in ws]
    return pl.pallas_call(
        body, name=name, grid=(M // tm,),
        in_specs=[pl.BlockSpec((tm, x.shape[1]), lambda i: (i, 0)) for x in xs]
        + [pl.BlockSpec(w.shape, lambda i: (0, 0)) for w in ws],
        out_specs=[pl.BlockSpec((tm, c), lambda i: (i, 0)) for c in n_out],
        out_shape=[jax.ShapeDtypeStruct((M, c), out_dtype) for c in n_out], compiler_params=_cparams(1),
    )(*xs, *ws)


def concat_cols(name, pieces, T, tm=512):
    widths = [p.shape[1] for p in pieces]
    total = sum(widths)
    tm = min(tm, T)

    def body(*refs):
        o_ref = refs[-1]
        off = 0
        for r, w in zip(refs[:-1], widths):
            o_ref[:, off:off + w] = r[...]
            off += w

    return pl.pallas_call(
        body, name=name, grid=(T // tm,),
        in_specs=[pl.BlockSpec((tm, w), lambda i: (i, 0)) for w in widths],
        out_specs=pl.BlockSpec((tm, total), lambda i: (i, 0)),
        out_shape=jax.ShapeDtypeStruct((T, total), pieces[0].dtype), compiler_params=_cparams(1),
    )(*pieces)


def mm_nt(name, a, b, out_dtype, tm=1024, tko=None, tc=None):
    M, C = a.shape
    Ko = b.shape[0]
    tm = min(tm, M)
    tc = C if tc is None else min(tc, C)
    tko = Ko if tko is None else tko
    nc = C // tc

    def body(a_ref, b_ref, o_ref, *scratch):
        part = lax.dot_general(a_ref[...].astype(bf16), b_ref[...].astype(bf16), (((1,), (1,)), ((), ())),
                               preferred_element_type=f32)
        if nc == 1:
            o_ref[...] = part.astype(o_ref.dtype)
            return
        acc_ref = scratch[0]
        c = pl.program_id(2)

        @pl.when(c == 0)
        def _():
            acc_ref[...] = part

        @pl.when(c > 0)
        def _():
            acc_ref[...] += part

        @pl.when(c == nc - 1)
        def _():
            o_ref[...] = acc_ref[...].astype(o_ref.dtype)

    return pl.pallas_call(
        body, name=name, grid=(M // tm, Ko // tko, nc),
        in_specs=[pl.BlockSpec((tm, tc), lambda i, j, c: (i, c)), pl.BlockSpec((tko, tc), lambda i, j, c: (j, c))],
        out_specs=pl.BlockSpec((tm, tko), lambda i, j, c: (i, j)),
        out_shape=jax.ShapeDtypeStruct((M, Ko), out_dtype),
        scratch_shapes=[pltpu.VMEM((tm, tko), f32)] if nc > 1 else [], compiler_params=_cparams(3),
    )(a, b)


def mm_tn(name, a, b, n_slab=None, tk=512, tn=512):
    T, K = a.shape
    N = b.shape[1]
    tk = min(tk, K)
    if n_slab is None:
        tn = min(tn, N)

        def body(a_ref, b_ref, o_ref):
            o_ref[...] = lax.dot_general(a_ref[...].astype(bf16), b_ref[...].astype(bf16), (((0,), (0,)), ((), ())),
                                         preferred_element_type=f32).astype(o_ref.dtype)

        return pl.pallas_call(
            body, name=name, grid=(N // tn, K // tk),
            in_specs=[pl.BlockSpec((T, tk), lambda j, i: (0, i)), pl.BlockSpec((T, tn), lambda j, i: (0, j))],
            out_specs=pl.BlockSpec((tk, tn), lambda j, i: (i, j)),
            out_shape=jax.ShapeDtypeStruct((K, N), bf16), compiler_params=_cparams(2),
        )(a, b)

    Ns = n_slab
    g = N_DEV if Ns % 128 == 0 else 2
    tn = g * Ns

    def body(a_ref, b_ref, o_ref):
        val = lax.dot_general(a_ref[...].astype(bf16), b_ref[...].astype(bf16), (((0,), (0,)), ((), ())),
                              preferred_element_type=f32)
        for s in range(g):
            o_ref[s] = val[:, s * Ns:(s + 1) * Ns].astype(o_ref.dtype)

    return pl.pallas_call(
        body, name=name, grid=(K // tk, N_DEV // g),
        in_specs=[pl.BlockSpec((T, tk), lambda i, j: (0, i)), pl.BlockSpec((T, tn), lambda i, j: (0, j))],
        out_specs=pl.BlockSpec((g, tk, Ns), lambda i, j: (j, i, 0)),
        out_shape=jax.ShapeDtypeStruct((N_DEV, K, Ns), bf16), compiler_params=_cparams(2),
    )(a, b)


def _tile_spec(w, tm, cb):
    if callable(cb):
        return pl.BlockSpec((tm, w), lambda c, i: (i, cb(c)))
    return pl.BlockSpec((tm, w), lambda c, i: (i, cb))


def _param_spec(block, cb):
    nd = len(block)
    if cb is None:
        return pl.BlockSpec(block, lambda c, i: (0,) * nd)
    return pl.BlockSpec(block, lambda c, i: (0,) * (nd - 1) + (cb(c),))


def rowwise(name, fn, tiled, params, outs, T, tm, ncol=1):
    n_in = len(tiled) + len(params)
    n_t = len(tiled)

    def body(*refs):
        ins = [r[...].astype(f32) for r in refs[:n_t]] + [r[...] for r in refs[n_t:n_in]]
        res = fn(*ins)
        for r, o in zip(refs[n_in:], res):
            r[...] = o.astype(r.dtype)

    return pl.pallas_call(
        body, name=name, grid=(ncol, T // tm),
        in_specs=[_tile_spec(w, tm, cb) for _, w, cb in tiled] + [_param_spec(blk, cb) for _, blk, cb in params],
        out_specs=[_tile_spec(w, tm, cb) for _, w, cb, _ in outs],
        out_shape=[jax.ShapeDtypeStruct((T, cols), dt) for cols, _, _, dt in outs],
        compiler_params=_cparams(2),
    )(*[a for a, _, _ in tiled], *[a for a, _, _ in params])


def rowwise_bwd(name, fn, tiled, params, cts, grads, T, tm, ncol=1, adds=None):
    n_t, n_p, n_c = len(tiled), len(params), len(cts)
    adds = adds or [None] * n_t
    add_list = [(k, a) for k, a in enumerate(adds) if a is not None]
    want = [k for k, g in enumerate(grads) if g is not None]
    n_a = len(add_list)

    def body(*refs):
        pos = 0
        t_refs = refs[pos:pos + n_t]; pos += n_t
        p_refs = refs[pos:pos + n_p]; pos += n_p
        c_refs = refs[pos:pos + n_c]; pos += n_c
        a_refs = refs[pos:pos + n_a]; pos += n_a
        g_refs = refs[pos:pos + len(want)]; pos += len(want)
        pg_refs = refs[pos:pos + n_p]
        primals = [r[...].astype(f32) for r in t_refs] + [r[...] for r in p_refs]
        _, vjp = jax.vjp(fn, *primals)
        g = vjp(tuple(r[...].astype(f32) for r in c_refs))
        add_of = {k: a_refs[n][...] for n, (k, _) in enumerate(add_list)}
        for n, k in enumerate(want):
            val = g[k]
            if k in add_of:
                val = val + add_of[k]
            g_refs[n][...] = val.astype(g_refs[n].dtype)
        i = pl.program_id(1)
        for k in range(n_p):
            @pl.when(i == 0)
            def _(k=k):
                pg_refs[k][...] = g[n_t + k]

            @pl.when(i > 0)
            def _(k=k):
                pg_refs[k][...] += g[n_t + k]

    in_specs = ([_tile_spec(w, tm, cb) for _, w, cb in tiled] + [_param_spec(blk, cb) for _, blk, cb in params]
                + [_tile_spec(w, tm, cb) for _, w, cb in cts] + [_tile_spec(w, tm, cb) for _, (_, w, cb) in add_list])
    ops = ([a for a, _, _ in tiled] + [a for a, _, _ in params] + [a for a, _, _ in cts]
           + [a for _, (a, _, _) in add_list])
    out_specs, out_shape = [], []
    for k in want:
        w = tiled[k][1]
        out_specs.append(_tile_spec(w, tm, lambda c: c))
        out_shape.append(jax.ShapeDtypeStruct((T, ncol * w), grads[k]))
    for a, blk, cb in params:
        out_specs.append(_param_spec(blk, cb))
        out_shape.append(jax.ShapeDtypeStruct(a.shape, f32))
    return pl.pallas_call(
        body, name=name, grid=(ncol, T // tm), in_specs=in_specs, out_specs=out_specs, out_shape=out_shape,
        compiler_params=_cparams(2),
    )(*ops)


@jax.custom_vjp
def _bdot(a, b):
    return jnp.dot(a.astype(bf16), b.astype(bf16), preferred_element_type=f32)


def _bdot_fwd(a, b):
    return _bdot(a, b), (a, b)


def _bdot_bwd(res, ct):
    a, b = res
    ctb = ct.astype(bf16)
    da = lax.dot_general(ctb, b.astype(bf16), (((1,), (1,)), ((), ())), preferred_element_type=f32)
    db = lax.dot_general(a.astype(bf16), ctb, (((0,), (0,)), ((), ())), preferred_element_type=f32)
    return da, db


_bdot.defvjp(_bdot_fwd, _bdot_bwd)


def _layer_norm(x, g, b):
    mu = jnp.mean(x, axis=-1, keepdims=True)
    xc = x - mu
    y = xc * lax.rsqrt(jnp.mean(xc * xc, axis=-1, keepdims=True) + EPS)
    return y * g + b


def f_rms(x, g):
    y = x * lax.rsqrt(jnp.mean(x * x, axis=-1, keepdims=True) + EPS)
    return (y * g,)


def f_sgu(zu, zv, ln_g, ln_b, wm, sgb_t):
    u = jax.nn.gelu(zu)
    vn = _layer_norm(jax.nn.gelu(zv), ln_g, ln_b)
    row = lax.broadcasted_iota(jnp.int32, (CHUNK, CHUNK), 0)
    col = lax.broadcasted_iota(jnp.int32, (CHUNK, CHUNK), 1)
    tril = col <= row
    low = col < HEAD_DIM
    parts = []
    for p in range(4):
        vp = vn[:, CHUNK * p:CHUNK * (p + 1)]
        w0 = jnp.where(tril, wm[2 * p], 0.0)
        w1 = jnp.where(tril, wm[2 * p + 1], 0.0)
        parts.append(_bdot(w0, jnp.where(low, vp, 0.0)) + _bdot(w1, jnp.where(low, 0.0, vp)))
    s = jnp.concatenate(parts, axis=1)
    lane_g = lax.shift_right_logical(lax.broadcasted_iota(jnp.int32, s.shape, 1), 6)
    bias = jnp.zeros_like(s)
    for g in range(8):
        bias = jnp.where(lane_g == g, sgb_t[:, g:g + 1], bias)
    return (u * (s + bias),)


def f_glu(p, gl):
    return (p * jax.nn.sigmoid(gl),)


def f_lnsilu(c1, g, b):
    return (jax.nn.silu(_layer_norm(c1, g, b)),)


def _head_norm(x, g64):
    g = jnp.concatenate([g64] * N_HEADS, axis=1)
    lane_h = lax.shift_right_logical(lax.broadcasted_iota(jnp.int32, x.shape, 1), 6)
    x2 = x * x
    r = jnp.zeros_like(x)
    for h in range(N_HEADS):
        mh = lane_h == h
        ms = jnp.sum(jnp.where(mh, x2, 0.0), axis=-1, keepdims=True) * (1.0 / HEAD_DIM)
        r = jnp.where(mh, lax.rsqrt(ms + EPS), r)
    return (x * r) * g


def f_qkv(zq, zk, zv, qg, kg):
    return (_head_norm(zq, qg) * 0.125, _head_norm(zk, kg), zv)


def f_merge(g0, g1, g2, ya, yb, yc, b0, b1, b2):
    return (jax.nn.sigmoid(g0 + b0) * ya + jax.nn.sigmoid(g1 + b1) * yb + jax.nn.sigmoid(g2 + b2) * yc,)


HALO = 32


def conv31_fwd(x, w, b, T, tt=256):
    C = x.shape[1]
    r = tt // HALO

    def body(x_ref, h_ref, w_ref, b_ref, y_ref, buf):
        i = pl.program_id(0)
        halo = h_ref[...]
        buf[0:HALO, :] = jnp.where(i > 0, halo, jnp.zeros_like(halo))
        buf[HALO:HALO + tt, :] = x_ref[...]
        acc = jnp.zeros((tt, C), f32) + b_ref[...]
        for k in range(CV_KERNEL):
            acc = acc + w_ref[k:k + 1, :] * buf[pl.ds(HALO - (CV_KERNEL - 1) + k, tt), :]
        y_ref[...] = acc

    return pl.pallas_call(
        body, name="conv31_fwd", grid=(T // tt,),
        in_specs=[pl.BlockSpec((tt, C), lambda i: (i, 0)),
                  pl.BlockSpec((HALO, C), lambda i: (jnp.maximum(i * r - 1, 0), 0)),
                  pl.BlockSpec((CV_KERNEL, C), lambda i: (0, 0)), pl.BlockSpec((1, C), lambda i: (0, 0))],
        out_specs=pl.BlockSpec((tt, C), lambda i: (i, 0)),
        out_shape=jax.ShapeDtypeStruct((T, C), f32),
        scratch_shapes=[pltpu.VMEM((HALO + tt, C), f32)], compiler_params=_cparams(1),
    )(x, x, w, b)


def conv31_bwd(x, w, dy, T, tt=256):
    C = x.shape[1]
    r = tt // HALO
    n = T // tt

    def body(x_ref, h_ref, w_ref, dy_ref, dyn_ref, dx_ref, dw_ref, db_ref, xbuf, dbuf):
        i = pl.program_id(0)
        halo = h_ref[...]
        xbuf[0:HALO, :] = jnp.where(i > 0, halo, jnp.zeros_like(halo))
        xbuf[HALO:HALO + tt, :] = x_ref[...]
        nxt = dyn_ref[...]
        dy = dy_ref[...]
        dbuf[0:tt, :] = dy
        dbuf[tt:tt + HALO, :] = jnp.where(i < n - 1, nxt, jnp.zeros_like(nxt))

        @pl.when(i == 0)
        def _():
            dw_ref[...] = jnp.zeros_like(dw_ref)
            db_ref[...] = jnp.zeros_like(db_ref)

        acc = jnp.zeros((tt, C), f32)
        for k in range(CV_KERNEL):
            acc = acc + w_ref[k:k + 1, :] * dbuf[pl.ds(CV_KERNEL - 1 - k, tt), :]
            xs = xbuf[pl.ds(HALO - (CV_KERNEL - 1) + k, tt), :]
            dw_ref[k:k + 1, :] += jnp.sum(dy * xs, axis=0, keepdims=True)
        dx_ref[...] = acc
        db_ref[...] += jnp.sum(dy, axis=0, keepdims=True)

    return pl.pallas_call(
        body, name="conv31_bwd", grid=(n,),
        in_specs=[pl.BlockSpec((tt, C), lambda i: (i, 0)),
                  pl.BlockSpec((HALO, C), lambda i: (jnp.maximum(i * r - 1, 0), 0)),
                  pl.BlockSpec((CV_KERNEL, C), lambda i: (0, 0)),
                  pl.BlockSpec((tt, C), lambda i: (i, 0)),
                  pl.BlockSpec((HALO, C), lambda i: (jnp.minimum((i + 1) * r, n * r - 1), 0))],
        out_specs=[pl.BlockSpec((tt, C), lambda i: (i, 0)), pl.BlockSpec((CV_KERNEL, C), lambda i: (0, 0)),
                   pl.BlockSpec((1, C), lambda i: (0, 0))],
        out_shape=[jax.ShapeDtypeStruct((T, C), f32), jax.ShapeDtypeStruct((CV_KERNEL, C), f32),
                   jax.ShapeDtypeStruct((1, C), f32)],
        scratch_shapes=[pltpu.VMEM((HALO + tt, C), f32), pltpu.VMEM((HALO + tt, C), f32)], compiler_params=_cparams(1),
    )(x, x, w, dy, dy)


FFN_TC = 128
FFN_PAD = 8


def ffn_act_fwd(up, cw, cb, T):
    F = up.shape[1] // 2
    nj = F // FFN_TC
    rc = min(128, T)

    def body(g_ref, v_ref, g0, g1, g2, gb, v0, v1, v2, vb, o_ref, gp, vp):
        zeros = jnp.zeros((FFN_PAD, FFN_TC), f32)
        for p, x_ref in ((gp, g_ref), (vp, v_ref)):
            p[0:FFN_PAD, :] = zeros
            p[FFN_PAD:FFN_PAD + T, :] = x_ref[...].astype(f32)
        wg = (g0[...], g1[...], g2[...], gb[...])
        wv = (v0[...], v1[...], v2[...], vb[...])

        def conv(p, w, r):
            return (w[0] * p[pl.ds(FFN_PAD + r - 2, rc), :] + w[1] * p[pl.ds(FFN_PAD + r - 1, rc), :]
                    + w[2] * p[pl.ds(FFN_PAD + r, rc), :] + w[3])

        for r in range(0, T, rc):
            o_ref[pl.ds(r, rc), :] = (jax.nn.silu(conv(gp, wg, r)) * conv(vp, wv, r)).astype(o_ref.dtype)

    gspec = pl.BlockSpec((T, FFN_TC), lambda j: (0, j))
    vspec = pl.BlockSpec((T, FFN_TC), lambda j: (0, j + nj))
    pg = pl.BlockSpec((1, FFN_TC), lambda j: (0, j))
    pv = pl.BlockSpec((1, FFN_TC), lambda j: (0, j + nj))
    return pl.pallas_call(
        body, name="ffn_act_fwd", grid=(nj,),
        in_specs=[gspec, vspec, pg, pg, pg, pg, pv, pv, pv, pv], out_specs=gspec,
        out_shape=jax.ShapeDtypeStruct((T, F), bf16),
        scratch_shapes=[pltpu.VMEM((FFN_PAD + T, FFN_TC), f32)] * 2, compiler_params=_cparams(1),
    )(up, up, cw[0], cw[1], cw[2], cb, cw[0], cw[1], cw[2], cb)


def ffn_act_bwd(up, cw, cb, dact, T):
    F = up.shape[1] // 2
    nj = F // FFN_TC

    rc = min(128, T)
    ext = rc + FFN_PAD

    def body(g_ref, v_ref, g0, g1, g2, gb, v0, v1, v2, vb, d_ref, dupg_ref, dupv_ref, *rest):
        p_refs, (gp, vp, dp, dgs, dvs) = rest[:8], rest[8:]
        zeros = jnp.zeros((FFN_PAD, FFN_TC), f32)
        for p, x_ref in ((gp, g_ref), (vp, v_ref)):
            p[0:FFN_PAD, :] = zeros
            p[FFN_PAD:FFN_PAD + T, :] = x_ref[...].astype(f32)
            p[FFN_PAD + T:FFN_PAD + T + FFN_PAD, :] = zeros
        dp[0:T, :] = d_ref[...]
        dp[T:T + FFN_PAD, :] = zeros
        wg = (g0[...], g1[...], g2[...], gb[...])
        wv = (v0[...], v1[...], v2[...], vb[...])
        acc = [jnp.zeros((1, FFN_TC), f32) for _ in range(8)]

        def taps(p, r):
            return tuple(p[pl.ds(FFN_PAD + r - s, ext), :] for s in (2, 1, 0))

        for r in range(0, T, rc):
            xg, xv = taps(gp, r), taps(vp, r)
            gc = wg[0] * xg[0] + wg[1] * xg[1] + wg[2] * xg[2] + wg[3]
            vc = wv[0] * xv[0] + wv[1] * xv[1] + wv[2] * xv[2] + wv[3]
            d = dp[pl.ds(r, ext), :]
            sg = jax.nn.sigmoid(gc)
            sides = ((d * vc * (sg * (1.0 + gc * (1.0 - sg))), xg, wg, dgs, dupg_ref), (d * (gc * sg), xv, wv, dvs, dupv_ref))
            for side, (dc, x, w, buf, dup_ref) in enumerate(sides):
                buf[...] = dc
                dc0 = dc[0:rc]
                dup = w[2] * dc0 + w[1] * buf[pl.ds(1, rc), :] + w[0] * buf[pl.ds(2, rc), :]
                dup_ref[pl.ds(r, rc), :] = dup.astype(dup_ref.dtype)
                for k in range(3):
                    acc[4 * side + k] = acc[4 * side + k] + jnp.sum(dc0 * x[k][0:rc], axis=0, keepdims=True)
                acc[4 * side + 3] = acc[4 * side + 3] + jnp.sum(dc0, axis=0, keepdims=True)
        for k in range(8):
            p_refs[k][...] = acc[k]

    gspec = pl.BlockSpec((T, FFN_TC), lambda j: (0, j))
    vspec = pl.BlockSpec((T, FFN_TC), lambda j: (0, j + nj))
    pg = pl.BlockSpec((1, FFN_TC), lambda j: (0, j))
    pv = pl.BlockSpec((1, FFN_TC), lambda j: (0, j + nj))
    res = pl.pallas_call(
        body, name="ffn_act_bwd", grid=(nj,),
        in_specs=[gspec, vspec, pg, pg, pg, pg, pv, pv, pv, pv, gspec],
        out_specs=[gspec, gspec] + [pg] * 8,
        out_shape=[jax.ShapeDtypeStruct((T, F), bf16)] * 2 + [jax.ShapeDtypeStruct((1, F), f32)] * 8,
        scratch_shapes=[pltpu.VMEM((FFN_PAD + T + FFN_PAD, FFN_TC), f32)] * 2 + [pltpu.VMEM((T + FFN_PAD, FFN_TC), f32)]
        + [pltpu.VMEM((ext, FFN_TC), f32)] * 2,
        compiler_params=_cparams(1),
    )(up, up, cw[0], cw[1], cw[2], cb, cw[0], cw[1], cw[2], cb, dact)
    dup = concat_cols("dup_concat", [res[0], res[1]], T)
    return (dup,) + tuple(jnp.concatenate([res[2 + k], res[6 + k]], axis=1) for k in range(4))


BQ = 256
BK = 256
assert BQ == BK
NT_DIMS = (((1,), (1,)), ((), ()))
TN_DIMS = (((0,), (0,)), ((), ()))


def _split_dot(x, u):
    x1 = x.astype(bf16)
    x2 = (x - x1.astype(f32)).astype(bf16)
    n = x.shape[0]
    y = jnp.dot(jnp.concatenate([x1, x2], axis=0), u, preferred_element_type=f32)
    return y[0:n] + y[n:2 * n]


def _log_sigmoids(z):
    sp = jnp.log(1.0 + jnp.exp(-jnp.abs(z)))
    lsp = jnp.minimum(z, 0.0) - sp
    return lsp, lsp - z


def _stack_heads(x):
    head1 = lax.broadcasted_iota(jnp.int32, x.shape, 1) >= HEAD_DIM
    zero = jnp.zeros_like(x)
    return jnp.concatenate([jnp.where(head1, zero, x), jnp.where(head1, x, zero)], axis=0)


def _unstack_heads(y):
    head1 = lax.broadcasted_iota(jnp.int32, (BQ, y.shape[1]), 1) >= HEAD_DIM
    return jnp.where(head1, y[BQ:2 * BQ], y[0:BQ])


def _attn_masks():
    row = lax.broadcasted_iota(jnp.int32, (2 * BQ, BK), 0)
    col = lax.broadcasted_iota(jnp.int32, (2 * BQ, BK), 1)
    ur = lax.broadcasted_iota(jnp.int32, (BK, BK), 0)
    uc = lax.broadcasted_iota(jnp.int32, (BK, BK), 1)
    return (row & (BQ - 1)) - col, (ur > uc).astype(bf16), (ur < uc).astype(bf16)


def attn_fwd(q, k, v, T, gather=None):
    nq = T // BQ
    n_g = 0 if gather is None else len(gather)

    def body(*refs):
        q_ref, k_ref, v_ref = refs[:3]
        o_ref, tot_ref = refs[3 + n_g:5 + n_g]
        p, i = pl.program_id(0), pl.program_id(1)
        if n_g:
            start, forward, finish = _two_level_gather(refs[5 + n_g:5 + 2 * n_g], *refs[5 + 2 * n_g:])
            pl.when(jnp.logical_and(p == 0, i == 0))(start)
        qs = _stack_heads(q_ref[...])
        diff, u_after, _ = _attn_masks()

        def step(jb, carry, diagonal):
            acc, c = carry
            ks = pl.multiple_of(jb * BK, BK)
            kb = k_ref[pl.ds(ks, BK), :]
            vb = v_ref[pl.ds(ks, BK), :]
            z = lax.dot_general(qs, kb, NT_DIMS, preferred_element_type=f32)
            lsp, lm = _log_sigmoids(z)
            if diagonal:
                m = diff > 0
                lm = jnp.where(m, lm, 0.0)
            a = jnp.exp(lsp + _split_dot(lm, u_after))
            if diagonal:
                a = jnp.where(m, a, 0.0)
            acc = acc + jnp.exp(c) * jnp.dot(a.astype(bf16), vb, preferred_element_type=f32)
            return acc, c + jnp.sum(lm, axis=-1, keepdims=True)

        carry = step(i, (jnp.zeros((2 * BQ, 128), f32), jnp.zeros((2 * BQ, 1), f32)), True)
        acc, c = lax.fori_loop(0, i, lambda t, cr: step(i - 1 - t, cr, False), carry)
        o_ref[...] = _unstack_heads(acc).astype(o_ref.dtype)
        tot_ref[...] = _unstack_heads(jnp.broadcast_to(c, (2 * BQ, 128)))
        if n_g:
            @pl.when(jnp.logical_and(p == 3, i == nq - 1))
            def _():
                forward()
                finish()

    blk = pl.BlockSpec((BQ, 128), lambda p, i: (i, p))
    full = pl.BlockSpec((T, 128), lambda p, i: (0, p))
    bufs = [] if gather is None else gather_buffers(gather)
    res = pl.pallas_call(
        body, name="attn_fwd_gather" if n_g else "attn_fwd", grid=(4, nq),
        in_specs=[blk, full, full] + [_ANY] * n_g, out_specs=[blk, blk] + [_ANY] * n_g,
        out_shape=[jax.ShapeDtypeStruct((T, 512), bf16), jax.ShapeDtypeStruct((T, 512), f32)]
        + [jax.ShapeDtypeStruct(b.shape, b.dtype) for b in bufs],
        input_output_aliases={3 + k: 2 + k for k in range(n_g)},
        scratch_shapes=_sems(n_g, N_DEV - 1) if n_g else [], compiler_params=_cparams(2),
    )(q, k, v, *bufs)
    return res[0], res[1], list(res[2:])


def attn_bwd(q, k, v, do, tot, T, scatter=None):
    nq = T // BQ
    n_s = 0 if scatter is None else len(scatter[0])
    modes = [] if scatter is None else list(scatter[1])

    def body(*refs):
        q_ref, k_ref, v_ref, do_ref, tot_ref = refs[:5]
        dq_ref, dk_ref, dv_ref = refs[5 + 2 * n_s:8 + 2 * n_s]
        p, i = pl.program_id(0), pl.program_id(1)
        if n_s:
            def copies():
                return _chip_copies(refs[5:5 + n_s], refs[8 + 2 * n_s:8 + 3 * n_s], modes, *refs[8 + 3 * n_s:])

            @pl.when(jnp.logical_and(p == 0, i == 0))
            def _():
                for cp in copies():
                    cp.start()

        @pl.when(i == 0)
        def _():
            dk_ref[...] = jnp.zeros_like(dk_ref)
            dv_ref[...] = jnp.zeros_like(dv_ref)

        qs = _stack_heads(q_ref[...])
        dos = _stack_heads(do_ref[...].astype(bf16))
        totv = tot_ref[...]
        tots = jnp.concatenate([totv[:, 0:1], totv[:, HEAD_DIM:HEAD_DIM + 1]], axis=0)
        diff, u_after, u_before = _attn_masks()

        def step(jb, carry, diagonal):
            dq, cl, cg = carry
            ks = pl.multiple_of(jb * BK, BK)
            kb = k_ref[pl.ds(ks, BK), :]
            vb = v_ref[pl.ds(ks, BK), :]
            z = lax.dot_general(qs, kb, NT_DIMS, preferred_element_type=f32)
            lsp, lsn = _log_sigmoids(z)
            lm = lsn
            if diagonal:
                m = diff > 0
                lm = jnp.where(m, lsn, 0.0)
            a = jnp.exp(lsp + _split_dot(lm, u_after))
            if diagonal:
                a = jnp.where(m, a, 0.0)
            g = a * lax.dot_general(dos, vb, NT_DIMS, preferred_element_type=f32)
            bb = jnp.exp(lsp)
            aa = g * jnp.exp(lsn) - _split_dot(g, u_before) * bb
            if diagonal:
                aa = jnp.where(m, aa, 0.0)
                bb = jnp.where(m, bb, 0.0)
            cl = cl + jnp.sum(lm, axis=-1, keepdims=True)
            f = jnp.exp(tots - cl)
            dz = (f * aa - cg * bb).astype(bf16)
            cg = cg + f * jnp.sum(g, axis=-1, keepdims=True)
            dq = dq + jnp.dot(dz, kb, preferred_element_type=f32)
            dk_ref[pl.ds(ks, BK), :] += lax.dot_general(dz, qs, TN_DIMS, preferred_element_type=f32)
            dv_ref[pl.ds(ks, BK), :] += lax.dot_general((f * a).astype(bf16), dos, TN_DIMS, preferred_element_type=f32)
            return dq, cl, cg

        zc = jnp.zeros((2 * BQ, 1), f32)
        carry = lax.fori_loop(0, i, lambda jb, cr: step(jb, cr, False), (jnp.zeros((2 * BQ, 128), f32), zc, zc))
        dq, _, _ = step(i, carry, True)
        dq_ref[...] = _unstack_heads(dq)
        if n_s:
            @pl.when(jnp.logical_and(p == 3, i == nq - 1))
            def _():
                for cp in copies():
                    cp.wait()

    blk = pl.BlockSpec((BQ, 128), lambda p, i: (i, p))
    full = pl.BlockSpec((T, 128), lambda p, i: (0, p))
    shp = jax.ShapeDtypeStruct((T, 512), f32)
    s_list = [] if scatter is None else list(scatter[0])
    bufs = chip_buffers(s_list, modes) if n_s else []
    res = pl.pallas_call(
        body, name="attn_bwd_scatter" if n_s else "attn_bwd", grid=(4, nq),
        in_specs=[blk, full, full, blk, blk] + [_ANY] * (2 * n_s), out_specs=[blk, full, full] + [_ANY] * n_s,
        out_shape=[shp, shp, shp] + [jax.ShapeDtypeStruct(b.shape, b.dtype) for b in bufs],
        input_output_aliases={5 + n_s + k: 3 + k for k in range(n_s)},
        scratch_shapes=_sems(n_s, 3) if n_s else [], compiler_params=_cparams(2),
    )(q, k, v, do, tot, *s_list, *bufs)
    return res[0], res[1], res[2], list(res[3:])


def loss_head(y, target, T, tm=256):
    D = y.shape[1]

    def body(y_ref, t_ref, dy_ref, l_ref):
        i = pl.program_id(0)
        err = y_ref[...] - t_ref[...]
        dy_ref[...] = err * (1.0 / D)
        part = 0.5 * jnp.sum(jnp.sum(err * err, axis=-1, keepdims=True) * (1.0 / D), axis=0, keepdims=True)

        @pl.when(i == 0)
        def _():
            l_ref[...] = jnp.zeros_like(l_ref)

        l_ref[...] += jnp.broadcast_to(part, l_ref.shape)

    spec = pl.BlockSpec((tm, D), lambda i: (i, 0))
    return pl.pallas_call(
        body, name="loss_head", grid=(T // tm,), in_specs=[spec, spec],
        out_specs=[spec, pl.BlockSpec((1, 128), lambda i: (0, 0))],
        out_shape=[jax.ShapeDtypeStruct((T, D), f32), jax.ShapeDtypeStruct((1, 128), f32)],
        compiler_params=_cparams(1),
    )(y, target)


def _row_tile(rows, cols, offset=0, max_elems=128 * 1024):
    best = None
    for t in range(16, rows + 1, 16):
        if rows % t == 0 and offset % t == 0 and t * cols <= max_elems:
            best = t
    return best if best is not None else rows


def adamw(name, w, m, v, parts, layer, prev=None, row_off=0):
    L, R, C = w.shape
    n_parts = parts.shape[0]
    tr = _row_tile(R, C, row_off, max_elems=256 * 1024)
    assert R % tr == 0 and row_off % tr == 0, (name, R, row_off, tr)

    def body(w_ref, m_ref, v_ref, p_ref, *rest):
        g_ref, d_ref, nm_ref, nv_ref = rest[-4:]
        g = p_ref[0].astype(f32)
        for s in range(1, n_parts):
            g = g + p_ref[s].astype(f32)
        wv = w_ref[...]
        mn = ADAM_B1 * m_ref[...] + (1.0 - ADAM_B1) * g
        vn = ADAM_B2 * v_ref[...] + (1.0 - ADAM_B2) * jnp.square(g)
        m_hat = mn / (1.0 - ADAM_B1 ** ADAM_STEP)
        v_hat = vn / (1.0 - ADAM_B2 ** ADAM_STEP)
        g_ref[...] = g
        d_ref[...] = -ADAM_LR * (m_hat / (jnp.sqrt(v_hat) + ADAM_EPS) + ADAM_WD * wv)
        nm_ref[...] = mn
        nv_ref[...] = vn

    spec = pl.BlockSpec((None, tr, C), lambda i: (layer, i, 0))
    shp = jax.ShapeDtypeStruct((L, R, C), f32)
    n_prev = 0 if prev is None else 4
    return pl.pallas_call(
        body, name=name, grid=(R // tr,),
        in_specs=[spec, spec, spec, pl.BlockSpec((n_parts, tr, C), lambda i: (0, row_off // tr + i, 0))] + [_ANY] * n_prev,
        out_specs=[spec] * 4, out_shape=[shp] * 4, input_output_aliases={4 + k: k for k in range(n_prev)},
        compiler_params=_cparams(1),
    )(w, m, v, parts, *(prev or ()))


REPL = ["ln1_g", "sg_ln_g", "sg_ln_b", "sg_w", "sg_b", "cv_b", "cv_ln_g", "cv_ln_b", "q_norm_g", "k_norm_g", "ln2_g",
        "ffn_conv_b"]
SMALL_SHARDED = ["b_gate", "cv_w", "ffn_conv_w"]
BIG = ["w_in", "w_a_out", "w_b_out", "w_c_out", "w_up", "w_out", "w_down"]
COL_SHARDED = ["w_a_out", "w_b_out", "w_c_out"]
TRANSPOSED = ["w_in", "w_up"]


def _pack(arrs, rows):
    flat = jnp.concatenate([a.reshape(-1) for a in arrs])
    return jnp.pad(flat, (0, rows * 128 - flat.shape[0])).reshape(rows, 128)


def _pack_layers(arrs, rows):
    flat = jnp.concatenate([a.reshape(a.shape[0], -1) for a in arrs], axis=1)
    return jnp.pad(flat, ((0, 0), (0, rows * 128 - flat.shape[1]))).reshape(flat.shape[0], rows, 128)


def _unpack_layers(packed, shapes):
    flat = packed.reshape(packed.shape[0], -1)
    out, pos = [], 0
    for s in shapes:
        n = 1
        for d in s[1:]:
            n *= d
        out.append(flat[:, pos:pos + n].reshape(s))
        pos += n
    return out


def _rows_for(shapes, mult):
    n = 0
    for s in shapes:
        k = 1
        for d in s:
            k *= d
        n += k
    rows = -(-n // 128)
    return -(-rows // mult) * mult


W_NAMES = ['ln1_g', 'w_in', 'b_gate', 'sg_ln_g', 'sg_ln_b', 'sg_w', 'sg_b', 'w_a_out', 'cv_w', 'cv_b', 'cv_ln_g', 'cv_ln_b',
           'w_b_out', 'q_norm_g', 'k_norm_g', 'w_c_out', 'w_out', 'ln2_g', 'w_up', 'ffn_conv_w', 'ffn_conv_b', 'w_down']


def _forward_layer(x, P, T, late=None):
    D = x.shape[1]
    sv = {"x0": x}
    (h1,) = rowwise("rms_fwd", f_rms, [(x, D, 0)], [(P["ln1_g"], (1, D), None)], [(D, D, 0, bf16)], T, 256)
    z = mm_nt("in_proj", h1, P["w_in_t"], bf16, tko=1664, tc=D)
    sv["h1"], sv["z"] = h1, z
    (ya_in,) = rowwise("sgu_fwd", f_sgu, [(z, 512, 0), (z, 512, 1)],
                       [(P["sg_ln_g"], (1, 512), None), (P["sg_ln_b"], (1, 512), None),
                        (P["sg_w"], (8, CHUNK, CHUNK), None), (P["sg_b_t"], (CHUNK, 8), None)],
                       [(512, 512, 0, bf16)], T, CHUNK)
    (c0,) = rowwise("glu_fwd", f_glu, [(z, 512, 2), (z, 512, 3)], [], [(512, 512, 0, f32)], T, 256)
    c1 = conv31_fwd(c0, P["cv_w"], P["cv_b"], T)
    (c3,) = rowwise("lnsilu_fwd", f_lnsilu, [(c1, 512, 0)], [(P["cv_ln_g"], (1, 512), None), (P["cv_ln_b"], (1, 512), None)],
                    [(512, 512, 0, bf16)], T, 256)
    q8, kn, vb = rowwise("qkv_fwd", f_qkv, [(z, 512, 4), (z, 512, 5), (z, 512, 6)],
                         [(P["q_norm_g"], (1, HEAD_DIM), None), (P["k_norm_g"], (1, HEAD_DIM), None)],
                         [(512, 512, 0, bf16)] * 3, T, 256)
    o, tot, _ = attn_fwd(q8, kn, vb, T)
    if late is not None:
        P.update(late(o))
    ya, yb, yc = mm_many("branch_out", [ya_in, c3, o], [P["w_a_out"], P["w_b_out"], P["w_c_out"]], f32)
    (merged,) = rowwise("merge_fwd", f_merge,
                        [(z, 512, lambda c: 7 + c), (z, 512, lambda c: 9 + c), (z, 512, lambda c: 11 + c),
                         (ya, 512, lambda c: c), (yb, 512, lambda c: c), (yc, 512, lambda c: c)],
                        [(P["b_gate"][k], (1, 512), lambda c: c) for k in range(3)],
                        [(D, 512, lambda c: c, bf16)], T, 256, ncol=2)
    x1 = mm_nn("out_proj", merged, P["w_out"], f32, add=x)
    sv.update(ya_in=ya_in, ya=ya, c0=c0, c1=c1, c3=c3, yb=yb, q8=q8, kn=kn, vb=vb, o=o, tot=tot, yc=yc, merged=merged, x1=x1)
    (h2,) = rowwise("rms_fwd", f_rms, [(x1, D, 0)], [(P["ln2_g"], (1, D), None)], [(D, D, 0, bf16)], T, 256)
    up = mm_nt("up_proj", h2, P["w_up_t"], bf16, tko=1408, tc=D)
    act = ffn_act_fwd(up, P["ffn_conv_w"], P["ffn_conv_b"], T)
    x2 = mm_nn("down_proj", act, P["w_down"], f32, add=x1)
    sv.update(h2=h2, up=up, act=act)
    return x2, sv


def _backward_layer(dx2, P, sv, T, scatter=None):
    D = dx2.shape[1]
    G = {}
    G["w_down"] = mm_tn("dw_down", sv["act"], dx2, tk=1408)
    dact = mm_nt("d_act", dx2, P["w_down"], f32, tko=1408, tc=D)
    dup, dcw0, dcw1, dcw2, G["ffn_conv_b"] = ffn_act_bwd(sv["up"], P["ffn_conv_w"], P["ffn_conv_b"], dact, T)
    G["ffn_conv_w"] = jnp.concatenate([dcw0, dcw1, dcw2], axis=0)
    G["w_up"] = mm_tn("dw_up", dup, sv["h2"], tk=1408)
    dh2 = mm_nn("d_h2", dup, P["w_up_t"], f32, tm=512)
    dx1, G["ln2_g"] = rowwise_bwd("rms_bwd", f_rms, [(sv["x1"], D, 0)], [(P["ln2_g"], (1, D), None)], [(dh2, D, 0)],
                                  [f32], T, 256, adds=[(dx2, D, 0)])
    G["w_out"] = mm_tn("dw_out", sv["merged"], dx1, tk=1024)
    dmerged = mm_nt("d_merged", dx1, P["w_out"], f32)
    z = sv["z"]
    dg0, dg1, dg2, dya, dyb, dyc, db0, db1, db2 = rowwise_bwd(
        "merge_bwd", f_merge,
        [(z, 512, lambda c: 7 + c), (z, 512, lambda c: 9 + c), (z, 512, lambda c: 11 + c),
         (sv["ya"], 512, lambda c: c), (sv["yb"], 512, lambda c: c), (sv["yc"], 512, lambda c: c)],
        [(P["b_gate"][k], (1, 512), lambda c: c) for k in range(3)],
        [(dmerged, 512, lambda c: c)], [bf16] * 6, T, 256, ncol=2)
    G["b_gate"] = jnp.concatenate([db0, db1, db2], axis=0)
    dya_in, dc3, do = mm_many("d_branch_in", [dya, dyb, dyc], [P["w_a_out"], P["w_b_out"], P["w_c_out"]], f32,
                              transpose_w=True)
    G["w_c_out"] = mm_tn("dw_c_out", sv["o"], dyc, n_slab=CHUNK)
    dq8, dkn, dvb, received = attn_bwd(sv["q8"], sv["kn"], sv["vb"], do, sv["tot"], T, scatter=scatter)
    dzq, dzk, dzv, G["q_norm_g"], G["k_norm_g"] = rowwise_bwd(
        "qkv_bwd", f_qkv, [(z, 512, 4), (z, 512, 5), (z, 512, 6)],
        [(P["q_norm_g"], (1, HEAD_DIM), None), (P["k_norm_g"], (1, HEAD_DIM), None)],
        [(dq8, 512, 0), (dkn, 512, 0), (dvb, 512, 0)], [bf16] * 3, T, 256)
    G["w_b_out"] = mm_tn("dw_b_out", sv["c3"], dyb, n_slab=CHUNK)
    dc1, G["cv_ln_g"], G["cv_ln_b"] = rowwise_bwd(
        "lnsilu_bwd", f_lnsilu, [(sv["c1"], 512, 0)], [(P["cv_ln_g"], (1, 512), None), (P["cv_ln_b"], (1, 512), None)],
        [(dc3, 512, 0)], [f32], T, 256)
    dc0, G["cv_w"], G["cv_b"] = conv31_bwd(sv["c0"], P["cv_w"], dc1, T)
    dzp, dzgl = rowwise_bwd("glu_bwd", f_glu, [(z, 512, 2), (z, 512, 3)], [], [(dc0, 512, 0)], [bf16] * 2, T, 256)
    G["w_a_out"] = mm_tn("dw_a_out", sv["ya_in"], dya, n_slab=CHUNK)
    dzu, dzv_a, G["sg_ln_g"], G["sg_ln_b"], G["sg_w"], dsbt = rowwise_bwd(
        "sgu_bwd", f_sgu, [(z, 512, 0), (z, 512, 1)],
        [(P["sg_ln_g"], (1, 512), None), (P["sg_ln_b"], (1, 512), None), (P["sg_w"], (8, CHUNK, CHUNK), None),
         (P["sg_b_t"], (CHUNK, 8), None)],
        [(dya_in, 512, 0)], [bf16] * 2, T, CHUNK)
    G["sg_b"] = dsbt.T
    dz = concat_cols("dz_concat", [dzu, dzv_a, dzp, dzgl, dzq, dzk, dzv, dg0, dg1, dg2], T)
    G["w_in"] = mm_tn("dw_in", dz, sv["h1"], tk=1664)
    dh1 = mm_nn("d_h1", dz, P["w_in_t"], f32, tm=512)
    dx0, G["ln1_g"] = rowwise_bwd("rms_bwd", f_rms, [(sv["x0"], D, 0)], [(P["ln1_g"], (1, D), None)], [(dh1, D, 0)],
                                  [f32], T, 256, adds=[(dx1, D, 0)])
    return dx0, G, received


def kernel(x, ln1_g, w_in, b_gate, sg_ln_g, sg_ln_b, sg_w, sg_b, w_a_out, cv_w, cv_b, cv_ln_g, cv_ln_b, w_b_out, q_norm_g, k_norm_g, w_c_out, w_out, ln2_g, w_up, ffn_conv_w, ffn_conv_b, w_down, loss_target, m_ln1_g, m_w_in, m_b_gate, m_sg_ln_g, m_sg_ln_b, m_sg_w, m_sg_b, m_w_a_out, m_cv_w, m_cv_b, m_cv_ln_g, m_cv_ln_b, m_w_b_out, m_q_norm_g, m_k_norm_g, m_w_c_out, m_w_out, m_ln2_g, m_w_up, m_ffn_conv_w, m_ffn_conv_b, m_w_down, v_ln1_g, v_w_in, v_b_gate, v_sg_ln_g, v_sg_ln_b, v_sg_w, v_sg_b, v_w_a_out, v_cv_w, v_cv_b, v_cv_ln_g, v_cv_ln_b, v_w_b_out, v_q_norm_g, v_k_norm_g, v_w_c_out, v_w_out, v_ln2_g, v_w_up, v_ffn_conv_w, v_ffn_conv_b, v_w_down):
    W = dict(ln1_g=ln1_g, w_in=w_in, b_gate=b_gate, sg_ln_g=sg_ln_g, sg_ln_b=sg_ln_b, sg_w=sg_w, sg_b=sg_b, w_a_out=w_a_out,
             cv_w=cv_w, cv_b=cv_b, cv_ln_g=cv_ln_g, cv_ln_b=cv_ln_b, w_b_out=w_b_out, q_norm_g=q_norm_g, k_norm_g=k_norm_g,
             w_c_out=w_c_out, w_out=w_out, ln2_g=ln2_g, w_up=w_up, ffn_conv_w=ffn_conv_w, ffn_conv_b=ffn_conv_b, w_down=w_down)
    M = dict(ln1_g=m_ln1_g, w_in=m_w_in, b_gate=m_b_gate, sg_ln_g=m_sg_ln_g, sg_ln_b=m_sg_ln_b, sg_w=m_sg_w, sg_b=m_sg_b,
             w_a_out=m_w_a_out, cv_w=m_cv_w, cv_b=m_cv_b, cv_ln_g=m_cv_ln_g, cv_ln_b=m_cv_ln_b, w_b_out=m_w_b_out,
             q_norm_g=m_q_norm_g, k_norm_g=m_k_norm_g, w_c_out=m_w_c_out, w_out=m_w_out, ln2_g=m_ln2_g, w_up=m_w_up,
             ffn_conv_w=m_ffn_conv_w, ffn_conv_b=m_ffn_conv_b, w_down=m_w_down)
    V = dict(ln1_g=v_ln1_g, w_in=v_w_in, b_gate=v_b_gate, sg_ln_g=v_sg_ln_g, sg_ln_b=v_sg_ln_b, sg_w=v_sg_w, sg_b=v_sg_b,
             w_a_out=v_w_a_out, cv_w=v_cv_w, cv_b=v_cv_b, cv_ln_g=v_cv_ln_g, cv_ln_b=v_cv_ln_b, w_b_out=v_w_b_out,
             q_norm_g=v_q_norm_g, k_norm_g=v_k_norm_g, w_c_out=v_w_c_out, w_out=v_w_out, ln2_g=v_ln2_g, w_up=v_w_up,
             ffn_conv_w=v_ffn_conv_w, ffn_conv_b=v_ffn_conv_b, w_down=v_w_down)
    T, D = x.shape[1], x.shape[2]
    L = DEPTH
    xs = x.reshape(T, D)
    target = loss_target.reshape(T, D)

    ss_shapes = [W[n].shape for n in SMALL_SHARDED]
    ss_rows = _rows_for(ss_shapes, 8)
    (ss_all,), small_token = exchange("gather_small", [_pack([W[n] for n in SMALL_SHARDED], ss_rows)], ["gather"])
    full_small = {}
    pos = 0
    for n in SMALL_SHARDED:
        s = W[n].shape
        cnt = s[0] * s[1] * s[2]
        part = ss_all.reshape(N_DEV, -1)[:, pos:pos + cnt].reshape((N_DEV,) + s)
        full_small[n] = jnp.transpose(part, (1, 2, 0, 3)).reshape(s[0], s[1], N_DEV * s[2])
        pos += cnt

    params, saved = [], []
    cur = xs
    Wt = {n: (jnp.transpose(W[n], (0, 2, 1)) if n in TRANSPOSED else W[n]) for n in BIG}

    def local_slabs(l):
        return [Wt[n][l].astype(bf16) for n in BIG]

    def weights_of(names, slabs):
        out = {}
        for n, s in zip(names, slabs):
            if n in TRANSPOSED:
                out[n + "_t"] = s.reshape(-1, s.shape[-1])
            else:
                out[n] = assemble("assemble_" + n, s) if n in COL_SHARDED else s.reshape(-1, s.shape[-1])
        return out

    LATE = [n for n in BIG if n != "w_in"]
    def behind(token, blocks):
        return [blocks[0] + token[0, 0].astype(blocks[0].dtype)] + list(blocks[1:])

    first, token = gather2("gather_w_in", behind(small_token, [local_slabs(0)[0]]))
    early_part, late_part = {}, {}
    for l in range(L):
        if l > 0:
            early_part[l] = gather_start("gather_start_in_%d" % l, behind(token, local_slabs(l)[:1]))
            token = early_part[l][3]
        late_part[l] = gather_start("gather_start_rest_%d" % l, behind(token, local_slabs(l)[1:]))
        token = late_part[l][3]
    start_token = token[0, 0]

    def late_weights(l):
        send_sems, recv_sems, bufs, _ = late_part[l]
        return lambda o: weights_of(LATE, gather_wait("gather_wait_rest_%d" % l, send_sems, recv_sems, bufs, o))

    for l in range(L):
        if l == 0:
            P = weights_of(["w_in"], first)
        else:
            send_sems, recv_sems, bufs, _ = early_part[l]
            P = weights_of(["w_in"], gather_wait("gather_wait_in_%d" % l, send_sems, recv_sems, bufs, cur))
        late = late_weights(l)
        for n in REPL:
            P[n] = W[n][l]
        for n in ("ln1_g", "sg_ln_g", "sg_ln_b", "cv_b", "cv_ln_g", "cv_ln_b", "q_norm_g", "k_norm_g", "ln2_g", "ffn_conv_b"):
            P[n] = P[n].reshape(1, -1)
        P["sg_b_t"] = P["sg_b"].T
        P["cv_w"] = full_small["cv_w"][l]
        P["b_gate"] = [full_small["b_gate"][l][k:k + 1] for k in range(3)]
        P["ffn_conv_w"] = [full_small["ffn_conv_w"][l][k:k + 1] for k in range(3)]
        if l == 0:
            P["ln1_g"] = P["ln1_g"] + start_token
        cur, sv = _forward_layer(cur, P, T, late=late)
        params.append(P)
        saved.append(sv)

    dy, loss_row = loss_head(cur, target, T)
    loss = lax.psum(loss_row[0, 0], ("x", "y", "c"))

    repl_shapes = [W[n].shape for n in REPL]
    repl_rows = _rows_for([s[1:] for s in repl_shapes], 16)
    ssl_rows = _rows_for([s[1:] for s in ss_shapes], 16)
    state = {"repl": [_pack_layers([X[n] for n in REPL], repl_rows) for X in (W, M, V)],
             "ss": [_pack_layers([X[n] for n in SMALL_SHARDED], ssl_rows) for X in (W, M, V)]}
    big_rows, big_off = {}, {}
    for n in BIG:
        big_off[n] = sum(big_rows.values())
        big_rows[n] = W[n].shape[1] * W[n].shape[2] // D
        state[n] = [(jnp.transpose(X[n], (0, 2, 1)) if n in TRANSPOSED else X[n]).reshape(L, big_rows[n], D)
                    for X in (W, M, V)]
    done = {n: None for n in state}

    def update(layer, received):
        for n in BIG:
            done[n] = adamw("adamw_" + n, *state[n], received[0], layer, done[n], row_off=big_off[n])
        done["repl"] = adamw("adamw_repl", *state["repl"], received[1], layer, done["repl"])
        done["ss"] = adamw("adamw_ss", *state["ss"], received[2], layer, done["ss"])

    def reduce_in_chip(G):
        big = [G[n].reshape(4, 2, big_rows[n], D) for n in BIG]
        ss_parts = []
        for n in SMALL_SHARDED:
            k, c = W[n].shape[1:]
            ss_parts.append(jnp.transpose(G[n].reshape(k, N_DEV, c), (1, 0, 2)).reshape(N_DEV, k * c))
        ss_send = jnp.concatenate(ss_parts, axis=1)
        ss_send = jnp.pad(ss_send, ((0, 0), (0, ssl_rows * 128 - ss_send.shape[1]))).reshape(4, 2, ssl_rows, 128)
        repl = _pack([G[n] for n in REPL], repl_rows)
        core = lax.axis_index("c")
        kept = jnp.concatenate([lax.dynamic_index_in_dim(b, core, 1, keepdims=False) for b in big], axis=1)
        ss_kept = lax.dynamic_index_in_dim(ss_send, core, 1, keepdims=False)
        got, repl_got, ss_got = pair_exchange("pair_exchange", big, [repl, ss_send], ["gather", "scatter"])
        return [add2("pair_add_big", kept, got), add2("pair_add_repl", repl, repl_got), add2("pair_add_ss", ss_kept, ss_got)]

    chip_modes = ["scatter", "gather", "scatter"]
    dcur = dy
    send = None
    for l in reversed(range(L)):
        dcur, G, recv = _backward_layer(dcur, params[l], saved[l], T, scatter=send)
        if send is not None:
            update(l + 1, recv)
        send = (reduce_in_chip(G), chip_modes)
    update(0, chip_exchange("chip_exchange_last", send[0], send[1]))

    results = [{}, {}, {}, {}]
    for k in range(4):
        for n, a in zip(REPL, _unpack_layers(done["repl"][k], repl_shapes)):
            results[k][n] = a
        for n, a in zip(SMALL_SHARDED, _unpack_layers(done["ss"][k], ss_shapes)):
            results[k][n] = a
        for n in BIG:
            results[k][n] = jnp.transpose(done[n][k], (0, 2, 1)) if n in TRANSPOSED else done[n][k].reshape(W[n].shape)
    out = [loss, dcur.reshape(1, T, D)]
    for k in range(4):
        out += [results[k][n] for n in W_NAMES]
    return tuple(out)
```

```python
import functools

import jax
import jax.numpy as jnp
from jax import lax
from jax.experimental import pallas as pl
from jax.experimental.pallas import tpu as pltpu

f32 = jnp.float32
bf16 = jnp.bfloat16

EPS = 1e-6
N_DEV = 8
DEPTH = 4
CHUNK = 128
HEAD_DIM = 64
N_HEADS = 8
CV_KERNEL = 31
VMEM_LIMIT_BYTES = 56 * 2 ** 20

ADAM_LR = 0.001
ADAM_B1 = 0.9
ADAM_B2 = 0.999
ADAM_EPS = 1e-08
ADAM_WD = 0.01
ADAM_STEP = 10

MESH = pl.DeviceIdType.MESH


def _cparams(n_grid):
    return pltpu.CompilerParams(dimension_semantics=("arbitrary",) * n_grid, vmem_limit_bytes=VMEM_LIMIT_BYTES)


def exchange(name, arrays, modes):
    n = len(arrays)

    def body(*refs):
        copies = _direct_copies(refs[:n], refs[n:2 * n], modes, *refs[2 * n + 1:])
        for cp in copies:
            cp.start()
        for cp in copies:
            cp.wait()
        refs[2 * n][...] = jnp.zeros_like(refs[2 * n])

    res = pl.pallas_call(
        body, name=name, out_shape=_exchange_out_shapes(arrays, modes) + [jax.ShapeDtypeStruct((8, 128), f32)],
        in_specs=[_ANY] * n, out_specs=[_ANY] * n + [pl.BlockSpec(memory_space=pltpu.VMEM)],
        scratch_shapes=_exchange_sems(n),
    )(*arrays)
    return list(res[:n]), res[n]


_ANY = pl.BlockSpec(memory_space=pl.ANY)


def _exchange_out_shapes(arrays, modes):
    return [jax.ShapeDtypeStruct((N_DEV,) + tuple(a.shape) if m == "gather" else tuple(a.shape), a.dtype)
            for a, m in zip(arrays, modes)]


def _exchange_sems(n):
    return [pltpu.SemaphoreType.DMA((n, N_DEV - 1)), pltpu.SemaphoreType.DMA((n, N_DEV - 1)), pltpu.SemaphoreType.DMA((n,))]


def _direct_copies(ins, outs, modes, send_sems, recv_sems, local_sems):
    x, y, c = lax.axis_index("x"), lax.axis_index("y"), lax.axis_index("c")
    me = 4 * x + 2 * y + c
    copies = []
    for k in range(len(ins)):
        src_mine = ins[k] if modes[k] == "gather" else ins[k].at[me]
        copies.append(pltpu.make_async_copy(src_mine, outs[k].at[me], local_sems.at[k]))
        for r in range(1, N_DEV):
            px = 1 - x if r & 4 else x
            py = 1 - y if r & 2 else y
            pc = 1 - c if r & 1 else c
            src = ins[k] if modes[k] == "gather" else ins[k].at[4 * px + 2 * py + pc]
            copies.append(pltpu.make_async_remote_copy(
                src_ref=src, dst_ref=outs[k].at[me], send_sem=send_sems.at[k, r - 1], recv_sem=recv_sems.at[k, r - 1],
                device_id=(px, py, pc), device_id_type=MESH))
    return copies


def own_slot_buffer(block, n_slots, index):
    return lax.dynamic_update_slice(lax.empty((n_slots,) + tuple(block.shape), block.dtype), block[None],
                                    (index,) + (0,) * block.ndim)


def _sems(n, m):
    return [pltpu.SemaphoreType.DMA((n, m)), pltpu.SemaphoreType.DMA((n, m))]


def _two_level_gather(bufs, send_sems, recv_sems):
    x, y, c = lax.axis_index("x"), lax.axis_index("y"), lax.axis_index("c")
    me = 4 * x + 2 * y + c
    sibling = (x, y, 1 - c)
    chips = [(1 - x, y), (x, 1 - y), (1 - x, 1 - y)]

    def slot(px, py, pc):
        return 4 * px + 2 * py + pc

    def copy(k, sem, block, to):
        rows = bufs[k].at[block]
        return pltpu.make_async_remote_copy(src_ref=rows, dst_ref=rows, send_sem=send_sems.at[k, sem],
                                            recv_sem=recv_sems.at[k, sem], device_id=to, device_id_type=MESH)

    def first(k):
        return [copy(k, 0, me, sibling)] + [copy(k, 1 + j, me, (*chip, c)) for j, chip in enumerate(chips)]

    def passed_on(k, j):
        return copy(k, 4 + j, slot(*chips[j], c), sibling)

    def start():
        for k in range(len(bufs)):
            for cp in first(k):
                cp.start()

    def forward():
        for k in range(len(bufs)):
            for j in range(3):
                copy(k, 1 + j, slot(*chips[j], c), sibling).wait_recv()
                passed_on(k, j).start()

    def finish():
        for k in range(len(bufs)):
            copy(k, 0, slot(x, y, 1 - c), sibling).wait_recv()
            for j, chip in enumerate(chips):
                copy(k, 4 + j, slot(*chip, 1 - c), sibling).wait_recv()
            for cp in first(k) + [passed_on(k, j) for j in range(3)]:
                cp.wait_send()

    return start, forward, finish


def pair_exchange(name, big, smalls, modes):
    nb, ns = len(big), len(smalls)
    rows = [a.shape[2] for a in big]
    total, width = sum(rows), big[0].shape[3]
    n = nb + ns

    def body(*refs):
        ins, outs = refs[:n], refs[n:n + 1 + ns]
        send_sems, recv_sems = refs[-2:]
        x, y, c = lax.axis_index("x"), lax.axis_index("y"), lax.axis_index("c")
        copies = []

        def remote(k, src, dst):
            copies.append(pltpu.make_async_remote_copy(src_ref=src, dst_ref=dst, send_sem=send_sems.at[k, 0],
                                                       recv_sem=recv_sems.at[k, 0], device_id=(x, y, 1 - c),
                                                       device_id_type=MESH))

        off = 0
        for k in range(nb):
            remote(k, ins[k].at[:, 1 - c], outs[0].at[:, pl.ds(off, rows[k]), :])
            off += rows[k]
        for j in range(ns):
            remote(nb + j, ins[nb + j].at[:, 1 - c] if modes[j] == "scatter" else ins[nb + j], outs[1 + j])
        for cp in copies:
            cp.start()
        for cp in copies:
            cp.wait()

    out_shape = [jax.ShapeDtypeStruct((4, total, width), big[0].dtype)]
    for a, m in zip(smalls, modes):
        out_shape.append(jax.ShapeDtypeStruct((4,) + tuple(a.shape[2:]) if m == "scatter" else tuple(a.shape), a.dtype))
    return pl.pallas_call(
        body, name=name, out_shape=out_shape, in_specs=[_ANY] * n, out_specs=[_ANY] * len(out_shape),
        scratch_shapes=_sems(n, 1),
    )(*big, *smalls)


def add2(name, a, b):
    shape = a.shape
    a2, b2 = a.reshape(-1, shape[-1]), b.reshape(-1, shape[-1])
    R, C = a2.shape
    tr = _row_tile(R, C, max_elems=1200 * 1024)

    def body(a_ref, b_ref, o_ref):
        o_ref[...] = (a_ref[...].astype(f32) + b_ref[...].astype(f32)).astype(o_ref.dtype)

    spec = pl.BlockSpec((tr, C), lambda i: (i, 0))
    return pl.pallas_call(
        body, name=name, grid=(R // tr,), in_specs=[spec, spec], out_specs=spec,
        out_shape=jax.ShapeDtypeStruct((R, C), a.dtype), compiler_params=_cparams(1),
    )(a2, b2).reshape(shape)


def chip_buffers(arrays, modes):
    mine = 2 * lax.axis_index("x") + lax.axis_index("y")
    return [own_slot_buffer(lax.dynamic_index_in_dim(a, mine, 0, keepdims=False) if m == "scatter" else a, 4, mine)
            for a, m in zip(arrays, modes)]


def _chip_copies(ins, bufs, modes, send_sems, recv_sems):
    x, y, c = lax.axis_index("x"), lax.axis_index("y"), lax.axis_index("c")
    mine = 2 * x + y
    copies = []
    for k in range(len(ins)):
        for r in range(1, 4):
            px = 1 - x if r & 2 else x
            py = 1 - y if r & 1 else y
            src = ins[k].at[2 * px + py] if modes[k] == "scatter" else ins[k]
            copies.append(pltpu.make_async_remote_copy(
                src_ref=src, dst_ref=bufs[k].at[mine], send_sem=send_sems.at[k, r - 1], recv_sem=recv_sems.at[k, r - 1],
                device_id=(px, py, c), device_id_type=MESH))
    return copies


def chip_exchange(name, arrays, modes):
    n = len(arrays)

    def body(*refs):
        copies = _chip_copies(refs[:n], refs[2 * n:3 * n], modes, *refs[3 * n:])
        for cp in copies:
            cp.start()
        for cp in copies:
            cp.wait()

    bufs = chip_buffers(arrays, modes)
    return pl.pallas_call(
        body, name=name, out_shape=[jax.ShapeDtypeStruct(b.shape, b.dtype) for b in bufs],
        in_specs=[_ANY] * (2 * n), out_specs=[_ANY] * n, input_output_aliases={n + k: k for k in range(n)},
        scratch_shapes=_sems(n, 3),
    )(*arrays, *bufs)


def gather_buffers(blocks):
    me = 4 * lax.axis_index("x") + 2 * lax.axis_index("y") + lax.axis_index("c")
    return [own_slot_buffer(b, N_DEV, me) for b in blocks]


_HBM = pl.BlockSpec(memory_space=pltpu.HBM)
_SEM = pl.BlockSpec(memory_space=pltpu.SEMAPHORE)
_SPLIT_PARAMS = pltpu.CompilerParams(has_side_effects=pltpu.SideEffectType.DATAFLOW_SIDE_EFFECTING)


def _split_gather_copies(bufs, send_sems, recv_sems):
    x, y, c = lax.axis_index("x"), lax.axis_index("y"), lax.axis_index("c")
    me = 4 * x + 2 * y + c
    copies = []
    for k in range(len(bufs)):
        rows = bufs[k].at[me]
        for r in range(1, N_DEV):
            peer = (1 - x if r & 4 else x, 1 - y if r & 2 else y, 1 - c if r & 1 else c)
            copies.append(pltpu.make_async_remote_copy(
                src_ref=rows, dst_ref=rows, send_sem=send_sems.at[(N_DEV - 1) * k + r - 1],
                recv_sem=recv_sems.at[(N_DEV - 1) * k + r - 1], device_id=peer, device_id_type=MESH))
    return copies


def gather_start(name, blocks):
    n = len(blocks)
    bufs = gather_buffers(blocks)

    def body(*refs):
        for cp in _split_gather_copies(refs[n + 2:2 * n + 2], refs[n], refs[n + 1]):
            cp.start()
        refs[2 * n + 2][...] = jnp.zeros_like(refs[2 * n + 2])

    n_sem = n * (N_DEV - 1)
    res = pl.pallas_call(
        body, name=name,
        out_shape=(pltpu.SemaphoreType.DMA((n_sem,)), pltpu.SemaphoreType.DMA((n_sem,)),
                   *[pltpu.HBM(b.shape, b.dtype) for b in bufs], jax.ShapeDtypeStruct((8, 128), f32)),
        in_specs=[_HBM] * n, out_specs=(_SEM, _SEM, *[_HBM] * n, pl.BlockSpec(memory_space=pltpu.VMEM)),
        input_output_aliases={k: 2 + k for k in range(n)}, compiler_params=_SPLIT_PARAMS,
    )(*[pltpu.with_memory_space_constraint(b, pltpu.HBM) for b in bufs])
    return res[0], res[1], list(res[2:2 + n]), res[2 + n]


def gather_wait(name, send_sems, recv_sems, bufs, after):
    n = len(bufs)

    def body(*refs):
        for cp in _split_gather_copies(refs[:n], refs[n], refs[n + 1]):
            cp.wait_send()
            cp.wait_recv()

    return list(pl.pallas_call(
        body, name=name, out_shape=tuple(pltpu.HBM(b.shape, b.dtype) for b in bufs),
        in_specs=[_HBM] * n + [_SEM, _SEM, _ANY], out_specs=tuple([_HBM] * n),
        input_output_aliases={k: k for k in range(n)}, compiler_params=_SPLIT_PARAMS,
    )(*bufs, send_sems, recv_sems, after))


def _split_chip_copies(srcs, bufs, modes, send_sems, recv_sems):
    x, y, c = lax.axis_index("x"), lax.axis_index("y"), lax.axis_index("c")
    mine = 2 * x + y
    copies = []
    for k in range(len(srcs)):
        for r in range(1, 4):
            px = 1 - x if r & 2 else x
            py = 1 - y if r & 1 else y
            src = srcs[k].at[2 * px + py] if modes[k] == "scatter" else srcs[k]
            copies.append(pltpu.make_async_remote_copy(
                src_ref=src, dst_ref=bufs[k].at[mine], send_sem=send_sems.at[3 * k + r - 1],
                recv_sem=recv_sems.at[3 * k + r - 1], device_id=(px, py, c), device_id_type=MESH))
    return copies


def chip_start(name, arrays, modes):
    n = len(arrays)
    bufs = chip_buffers(arrays, modes)

    def body(*refs):
        for cp in _split_chip_copies(refs[2 * n + 2:3 * n + 2], refs[3 * n + 2:4 * n + 2], modes, refs[2 * n], refs[2 * n + 1]):
            cp.start()
        refs[4 * n + 2][...] = jnp.zeros_like(refs[4 * n + 2])

    both = list(arrays) + list(bufs)
    res = pl.pallas_call(
        body, name=name,
        out_shape=(pltpu.SemaphoreType.DMA((3 * n,)), pltpu.SemaphoreType.DMA((3 * n,)),
                   *[pltpu.HBM(b.shape, b.dtype) for b in both], jax.ShapeDtypeStruct((8, 128), f32)),
        in_specs=[_HBM] * (2 * n), out_specs=(_SEM, _SEM, *[_HBM] * (2 * n), pl.BlockSpec(memory_space=pltpu.VMEM)),
        input_output_aliases={k: 2 + k for k in range(2 * n)}, compiler_params=_SPLIT_PARAMS,
    )(*[pltpu.with_memory_space_constraint(b, pltpu.HBM) for b in both])
    return res[0], res[1], list(res[2:2 + n]), list(res[2 + n:2 + 2 * n]), res[2 + 2 * n]


def chip_wait(name, send_sems, recv_sems, srcs, bufs, modes, after):
    n = len(bufs)

    def body(*refs):
        for cp in _split_chip_copies(refs[:n], refs[n:2 * n], modes, refs[2 * n], refs[2 * n + 1]):
            cp.wait_send()
            cp.wait_recv()

    both = list(srcs) + list(bufs)
    res = pl.pallas_call(
        body, name=name, out_shape=tuple(pltpu.HBM(b.shape, b.dtype) for b in both),
        in_specs=[_HBM] * (2 * n) + [_SEM, _SEM, _ANY], out_specs=tuple([_HBM] * (2 * n)),
        input_output_aliases={k: k for k in range(2 * n)}, compiler_params=_SPLIT_PARAMS,
    )(*both, send_sems, recv_sems, after)
    return list(res[n:])


def gather2(name, arrays):
    n = len(arrays)

    def body(*refs):
        start, forward, finish = _two_level_gather(refs[n:2 * n], *refs[2 * n + 1:])
        start()
        forward()
        finish()
        refs[2 * n][...] = jnp.zeros_like(refs[2 * n])

    bufs = gather_buffers(arrays)
    res = pl.pallas_call(
        body, name=name, out_shape=[jax.ShapeDtypeStruct(b.shape, b.dtype) for b in bufs] + [jax.ShapeDtypeStruct((8, 128), f32)],
        in_specs=[_ANY] * n, out_specs=[_ANY] * n + [pl.BlockSpec(memory_space=pltpu.VMEM)],
        input_output_aliases={k: k for k in range(n)}, scratch_shapes=_sems(n, N_DEV - 1),
    )(*bufs)
    return list(res[:n]), res[n]


def assemble(name, slabs):
    _, K, Ns = slabs.shape
    g = N_DEV if Ns % 128 == 0 else 2

    def body(w_ref, o_ref):
        o_ref[...] = jnp.concatenate([w_ref[s] for s in range(g)], axis=1)

    return pl.pallas_call(
        body, name=name, grid=(N_DEV // g,),
        in_specs=[pl.BlockSpec((g, K, Ns), lambda m: (m, 0, 0))],
        out_specs=pl.BlockSpec((K, g * Ns), lambda m: (0, m)),
        out_shape=jax.ShapeDtypeStruct((K, N_DEV * Ns), slabs.dtype),
        compiler_params=_cparams(1),
    )(slabs)


def mm_nn(name, a, b, out_dtype, add=None, tm=1024, tn=512, tk=None):
    M, K = a.shape
    N = b.shape[1]
    tm, tn = min(tm, M), min(tn, N)
    tk = K if tk is None else tk
    nk = K // tk
    has_add = add is not None

    def body(a_ref, b_ref, *rest):
        o_ref = rest[1] if has_add else rest[0]
        part = jnp.dot(a_ref[...].astype(bf16), b_ref[...].astype(bf16), preferred_element_type=f32)
        if nk == 1:
            o_ref[...] = (part + rest[0][...] if has_add else part).astype(o_ref.dtype)
            return
        acc_ref = rest[-1]
        k = pl.program_id(2)

        @pl.when(k == 0)
        def _():
            acc_ref[...] = part + rest[0][...] if has_add else part

        @pl.when(k > 0)
        def _():
            acc_ref[...] += part

        @pl.when(k == nk - 1)
        def _():
            o_ref[...] = acc_ref[...].astype(o_ref.dtype)

    in_specs = [pl.BlockSpec((tm, tk), lambda i, j, k: (i, k)), pl.BlockSpec((tk, tn), lambda i, j, k: (k, j))]
    ops = [a, b]
    if has_add:
        in_specs.append(pl.BlockSpec((tm, tn), lambda i, j, k: (i, j)))
        ops.append(add)
    return pl.pallas_call(
        body, name=name, grid=(M // tm, N // tn, nk), in_specs=in_specs,
        out_specs=pl.BlockSpec((tm, tn), lambda i, j, k: (i, j)),
        out_shape=jax.ShapeDtypeStruct((M, N), out_dtype),
        scratch_shapes=[pltpu.VMEM((tm, tn), f32)] if nk > 1 else [], compiler_params=_cparams(3),
    )(*ops)


def mm_many(name, xs, ws, out_dtype, transpose_w=False, tm=512):
    n = len(xs)
    M = xs[0].shape[0]
    tm = min(tm, M)
    dims = NT_DIMS if transpose_w else (((1,), (0,)), ((), ()))

    def body(*refs):
        for k in range(n):
            refs[2 * n + k][...] = lax.dot_general(refs[k][...].astype(bf16), refs[n + k][...].astype(bf16), dims,
                                                   preferred_element_type=f32).astype(out_dtype)

    n_out = [w.shape[0] if transpose_w else w.shape[1] for w in ws]
    return pl.pallas_call(
        body, name=name, grid=(M // tm,),
        in_specs=[pl.BlockSpec((tm, x.shape[1]), lambda i: (i, 0)) for x in xs]
        + [pl.BlockSpec(w.shape, lambda i: (0, 0)) for w in ws],
        out_specs=[pl.BlockSpec((tm, c), lambda i: (i, 0)) for c in n_out],
        out_shape=[jax.ShapeDtypeStruct((M, c), out_dtype) for c in n_out], compiler_params=_cparams(1),
    )(*xs, *ws)


def concat_cols(name, pieces, T, tm=512):
    widths = [p.shape[1] for p in pieces]
    total = sum(widths)
    tm = min(tm, T)

    def body(*refs):
        o_ref = refs[-1]
        off = 0
        for r, w in zip(refs[:-1], widths):
            o_ref[:, off:off + w] = r[...]
            off += w

    return pl.pallas_call(
        body, name=name, grid=(T // tm,),
        in_specs=[pl.BlockSpec((tm, w), lambda i: (i, 0)) for w in widths],
        out_specs=pl.BlockSpec((tm, total), lambda i: (i, 0)),
        out_shape=jax.ShapeDtypeStruct((T, total), pieces[0].dtype), compiler_params=_cparams(1),
    )(*pieces)


def mm_nt(name, a, b, out_dtype, tm=1024, tko=None, tc=None):
    M, C = a.shape
    Ko = b.shape[0]
    tm = min(tm, M)
    tc = C if tc is None else min(tc, C)
    tko = Ko if tko is None else tko
    nc = C // tc

    def body(a_ref, b_ref, o_ref, *scratch):
        part = lax.dot_general(a_ref[...].astype(bf16), b_ref[...].astype(bf16), (((1,), (1,)), ((), ())),
                               preferred_element_type=f32)
        if nc == 1:
            o_ref[...] = part.astype(o_ref.dtype)
            return
        acc_ref = scratch[0]
        c = pl.program_id(2)

        @pl.when(c == 0)
        def _():
            acc_ref[...] = part

        @pl.when(c > 0)
        def _():
            acc_ref[...] += part

        @pl.when(c == nc - 1)
        def _():
            o_ref[...] = acc_ref[...].astype(o_ref.dtype)

    return pl.pallas_call(
        body, name=name, grid=(M // tm, Ko // tko, nc),
        in_specs=[pl.BlockSpec((tm, tc), lambda i, j, c: (i, c)), pl.BlockSpec((tko, tc), lambda i, j, c: (j, c))],
        out_specs=pl.BlockSpec((tm, tko), lambda i, j, c: (i, j)),
        out_shape=jax.ShapeDtypeStruct((M, Ko), out_dtype),
        scratch_shapes=[pltpu.VMEM((tm, tko), f32)] if nc > 1 else [], compiler_params=_cparams(3),
    )(a, b)


def mm_tn(name, a, b, n_slab=None, tk=512, tn=512):
    T, K = a.shape
    N = b.shape[1]
    tk = min(tk, K)
    if n_slab is None:
        tn = min(tn, N)

        def body(a_ref, b_ref, o_ref):
            o_ref[...] = lax.dot_general(a_ref[...].astype(bf16), b_ref[...].astype(bf16), (((0,), (0,)), ((), ())),
                                         preferred_element_type=f32).astype(o_ref.dtype)

        return pl.pallas_call(
            body, name=name, grid=(N // tn, K // tk),
            in_specs=[pl.BlockSpec((T, tk), lambda j, i: (0, i)), pl.BlockSpec((T, tn), lambda j, i: (0, j))],
            out_specs=pl.BlockSpec((tk, tn), lambda j, i: (i, j)),
            out_shape=jax.ShapeDtypeStruct((K, N), bf16), compiler_params=_cparams(2),
        )(a, b)

    Ns = n_slab
    g = N_DEV if Ns % 128 == 0 else 2
    tn = g * Ns

    def body(a_ref, b_ref, o_ref):
        val = lax.dot_general(a_ref[...].astype(bf16), b_ref[...].astype(bf16), (((0,), (0,)), ((), ())),
                              preferred_element_type=f32)
        for s in range(g):
            o_ref[s] = val[:, s * Ns:(s + 1) * Ns].astype(o_ref.dtype)

    return pl.pallas_call(
        body, name=name, grid=(K // tk, N_DEV // g),
        in_specs=[pl.BlockSpec((T, tk), lambda i, j: (0, i)), pl.BlockSpec((T, tn), lambda i, j: (0, j))],
        out_specs=pl.BlockSpec((g, tk, Ns), lambda i, j: (j, i, 0)),
        out_shape=jax.ShapeDtypeStruct((N_DEV, K, Ns), bf16), compiler_params=_cparams(2),
    )(a, b)


def _tile_spec(w, tm, cb):
    if callable(cb):
        return pl.BlockSpec((tm, w), lambda c, i: (i, cb(c)))
    return pl.BlockSpec((tm, w), lambda c, i: (i, cb))


def _param_spec(block, cb):
    nd = len(block)
    if cb is None:
        return pl.BlockSpec(block, lambda c, i: (0,) * nd)
    return pl.BlockSpec(block, lambda c, i: (0,) * (nd - 1) + (cb(c),))


def rowwise(name, fn, tiled, params, outs, T, tm, ncol=1):
    n_in = len(tiled) + len(params)
    n_t = len(tiled)

    def body(*refs):
        ins = [r[...].astype(f32) for r in refs[:n_t]] + [r[...] for r in refs[n_t:n_in]]
        res = fn(*ins)
        for r, o in zip(refs[n_in:], res):
            r[...] = o.astype(r.dtype)

    return pl.pallas_call(
        body, name=name, grid=(ncol, T // tm),
        in_specs=[_tile_spec(w, tm, cb) for _, w, cb in tiled] + [_param_spec(blk, cb) for _, blk, cb in params],
        out_specs=[_tile_spec(w, tm, cb) for _, w, cb, _ in outs],
        out_shape=[jax.ShapeDtypeStruct((T, cols), dt) for cols, _, _, dt in outs],
        compiler_params=_cparams(2),
    )(*[a for a, _, _ in tiled], *[a for a, _, _ in params])


def rowwise_bwd(name, fn, tiled, params, cts, grads, T, tm, ncol=1, adds=None):
    n_t, n_p, n_c = len(tiled), len(params), len(cts)
    adds = adds or [None] * n_t
    add_list = [(k, a) for k, a in enumerate(adds) if a is not None]
    want = [k for k, g in enumerate(grads) if g is not None]
    n_a = len(add_list)

    def body(*refs):
        pos = 0
        t_refs = refs[pos:pos + n_t]; pos += n_t
        p_refs = refs[pos:pos + n_p]; pos += n_p
        c_refs = refs[pos:pos + n_c]; pos += n_c
        a_refs = refs[pos:pos + n_a]; pos += n_a
        g_refs = refs[pos:pos + len(want)]; pos += len(want)
        pg_refs = refs[pos:pos + n_p]
        primals = [r[...].astype(f32) for r in t_refs] + [r[...] for r in p_refs]
        _, vjp = jax.vjp(fn, *primals)
        g = vjp(tuple(r[...].astype(f32) for r in c_refs))
        add_of = {k: a_refs[n][...] for n, (k, _) in enumerate(add_list)}
        for n, k in enumerate(want):
            val = g[k]
            if k in add_of:
                val = val + add_of[k]
            g_refs[n][...] = val.astype(g_refs[n].dtype)
        i = pl.program_id(1)
        for k in range(n_p):
            @pl.when(i == 0)
            def _(k=k):
                pg_refs[k][...] = g[n_t + k]

            @pl.when(i > 0)
            def _(k=k):
                pg_refs[k][...] += g[n_t + k]

    in_specs = ([_tile_spec(w, tm, cb) for _, w, cb in tiled] + [_param_spec(blk, cb) for _, blk, cb in params]
                + [_tile_spec(w, tm, cb) for _, w, cb in cts] + [_tile_spec(w, tm, cb) for _, (_, w, cb) in add_list])
    ops = ([a for a, _, _ in tiled] + [a for a, _, _ in params] + [a for a, _, _ in cts]
           + [a for _, (a, _, _) in add_list])
    out_specs, out_shape = [], []
    for k in want:
        w = tiled[k][1]
        out_specs.append(_tile_spec(w, tm, lambda c: c))
        out_shape.append(jax.ShapeDtypeStruct((T, ncol * w), grads[k]))
    for a, blk, cb in params:
        out_specs.append(_param_spec(blk, cb))
        out_shape.append(jax.ShapeDtypeStruct(a.shape, f32))
    return pl.pallas_call(
        body, name=name, grid=(ncol, T // tm), in_specs=in_specs, out_specs=out_specs, out_shape=out_shape,
        compiler_params=_cparams(2),
    )(*ops)


@jax.custom_vjp
def _bdot(a, b):
    return jnp.dot(a.astype(bf16), b.astype(bf16), preferred_element_type=f32)


def _bdot_fwd(a, b):
    return _bdot(a, b), (a, b)


def _bdot_bwd(res, ct):
    a, b = res
    ctb = ct.astype(bf16)
    da = lax.dot_general(ctb, b.astype(bf16), (((1,), (1,)), ((), ())), preferred_element_type=f32)
    db = lax.dot_general(a.astype(bf16), ctb, (((0,), (0,)), ((), ())), preferred_element_type=f32)
    return da, db


_bdot.defvjp(_bdot_fwd, _bdot_bwd)


def _layer_norm(x, g, b):
    mu = jnp.mean(x, axis=-1, keepdims=True)
    xc = x - mu
    y = xc * lax.rsqrt(jnp.mean(xc * xc, axis=-1, keepdims=True) + EPS)
    return y * g + b


def f_rms(x, g):
    y = x * lax.rsqrt(jnp.mean(x * x, axis=-1, keepdims=True) + EPS)
    return (y * g,)


def f_sgu(zu, zv, ln_g, ln_b, wm, sgb_t):
    u = jax.nn.gelu(zu)
    vn = _layer_norm(jax.nn.gelu(zv), ln_g, ln_b)
    row = lax.broadcasted_iota(jnp.int32, (CHUNK, CHUNK), 0)
    col = lax.broadcasted_iota(jnp.int32, (CHUNK, CHUNK), 1)
    tril = col <= row
    low = col < HEAD_DIM
    parts = []
    for p in range(4):
        vp = vn[:, CHUNK * p:CHUNK * (p + 1)]
        w0 = jnp.where(tril, wm[2 * p], 0.0)
        w1 = jnp.where(tril, wm[2 * p + 1], 0.0)
        parts.append(_bdot(w0, jnp.where(low, vp, 0.0)) + _bdot(w1, jnp.where(low, 0.0, vp)))
    s = jnp.concatenate(parts, axis=1)
    lane_g = lax.shift_right_logical(lax.broadcasted_iota(jnp.int32, s.shape, 1), 6)
    bias = jnp.zeros_like(s)
    for g in range(8):
        bias = jnp.where(lane_g == g, sgb_t[:, g:g + 1], bias)
    return (u * (s + bias),)


def f_glu(p, gl):
    return (p * jax.nn.sigmoid(gl),)


def f_lnsilu(c1, g, b):
    return (jax.nn.silu(_layer_norm(c1, g, b)),)


def _head_norm(x, g64):
    g = jnp.concatenate([g64] * N_HEADS, axis=1)
    lane_h = lax.shift_right_logical(lax.broadcasted_iota(jnp.int32, x.shape, 1), 6)
    x2 = x * x
    r = jnp.zeros_like(x)
    for h in range(N_HEADS):
        mh = lane_h == h
        ms = jnp.sum(jnp.where(mh, x2, 0.0), axis=-1, keepdims=True) * (1.0 / HEAD_DIM)
        r = jnp.where(mh, lax.rsqrt(ms + EPS), r)
    return (x * r) * g


def f_qkv(zq, zk, zv, qg, kg):
    return (_head_norm(zq, qg) * 0.125, _head_norm(zk, kg), zv)


def f_merge(g0, g1, g2, ya, yb, yc, b0, b1, b2):
    return (jax.nn.sigmoid(g0 + b0) * ya + jax.nn.sigmoid(g1 + b1) * yb + jax.nn.sigmoid(g2 + b2) * yc,)


HALO = 32


def conv31_fwd(x, w, b, T, tt=256):
    C = x.shape[1]
    r = tt // HALO

    def body(x_ref, h_ref, w_ref, b_ref, y_ref, buf):
        i = pl.program_id(0)
        halo = h_ref[...]
        buf[0:HALO, :] = jnp.where(i > 0, halo, jnp.zeros_like(halo))
        buf[HALO:HALO + tt, :] = x_ref[...]
        acc = jnp.zeros((tt, C), f32) + b_ref[...]
        for k in range(CV_KERNEL):
            acc = acc + w_ref[k:k + 1, :] * buf[pl.ds(HALO - (CV_KERNEL - 1) + k, tt), :]
        y_ref[...] = acc

    return pl.pallas_call(
        body, name="conv31_fwd", grid=(T // tt,),
        in_specs=[pl.BlockSpec((tt, C), lambda i: (i, 0)),
                  pl.BlockSpec((HALO, C), lambda i: (jnp.maximum(i * r - 1, 0), 0)),
                  pl.BlockSpec((CV_KERNEL, C), lambda i: (0, 0)), pl.BlockSpec((1, C), lambda i: (0, 0))],
        out_specs=pl.BlockSpec((tt, C), lambda i: (i, 0)),
        out_shape=jax.ShapeDtypeStruct((T, C), f32),
        scratch_shapes=[pltpu.VMEM((HALO + tt, C), f32)], compiler_params=_cparams(1),
    )(x, x, w, b)


def conv31_bwd(x, w, dy, T, tt=256):
    C = x.shape[1]
    r = tt // HALO
    n = T // tt

    def body(x_ref, h_ref, w_ref, dy_ref, dyn_ref, dx_ref, dw_ref, db_ref, xbuf, dbuf):
        i = pl.program_id(0)
        halo = h_ref[...]
        xbuf[0:HALO, :] = jnp.where(i > 0, halo, jnp.zeros_like(halo))
        xbuf[HALO:HALO + tt, :] = x_ref[...]
        nxt = dyn_ref[...]
        dy = dy_ref[...]
        dbuf[0:tt, :] = dy
        dbuf[tt:tt + HALO, :] = jnp.where(i < n - 1, nxt, jnp.zeros_like(nxt))

        @pl.when(i == 0)
        def _():
            dw_ref[...] = jnp.zeros_like(dw_ref)
            db_ref[...] = jnp.zeros_like(db_ref)

        acc = jnp.zeros((tt, C), f32)
        for k in range(CV_KERNEL):
            acc = acc + w_ref[k:k + 1, :] * dbuf[pl.ds(CV_KERNEL - 1 - k, tt), :]
            xs = xbuf[pl.ds(HALO - (CV_KERNEL - 1) + k, tt), :]
            dw_ref[k:k + 1, :] += jnp.sum(dy * xs, axis=0, keepdims=True)
        dx_ref[...] = acc
        db_ref[...] += jnp.sum(dy, axis=0, keepdims=True)

    return pl.pallas_call(
        body, name="conv31_bwd", grid=(n,),
        in_specs=[pl.BlockSpec((tt, C), lambda i: (i, 0)),
                  pl.BlockSpec((HALO, C), lambda i: (jnp.maximum(i * r - 1, 0), 0)),
                  pl.BlockSpec((CV_KERNEL, C), lambda i: (0, 0)),
                  pl.BlockSpec((tt, C), lambda i: (i, 0)),
                  pl.BlockSpec((HALO, C), lambda i: (jnp.minimum((i + 1) * r, n * r - 1), 0))],
        out_specs=[pl.BlockSpec((tt, C), lambda i: (i, 0)), pl.BlockSpec((CV_KERNEL, C), lambda i: (0, 0)),
                   pl.BlockSpec((1, C), lambda i: (0, 0))],
        out_shape=[jax.ShapeDtypeStruct((T, C), f32), jax.ShapeDtypeStruct((CV_KERNEL, C), f32),
                   jax.ShapeDtypeStruct((1, C), f32)],
        scratch_shapes=[pltpu.VMEM((HALO + tt, C), f32), pltpu.VMEM((HALO + tt, C), f32)], compiler_params=_cparams(1),
    )(x, x, w, dy, dy)


FFN_TC = 128
FFN_PAD = 8


def ffn_act_fwd(up, cw, cb, T):
    F = up.shape[1] // 2
    nj = F // FFN_TC
    rc = min(128, T)

    def body(g_ref, v_ref, g0, g1, g2, gb, v0, v1, v2, vb, o_ref, gp, vp):
        zeros = jnp.zeros((FFN_PAD, FFN_TC), f32)
        for p, x_ref in ((gp, g_ref), (vp, v_ref)):
            p[0:FFN_PAD, :] = zeros
            p[FFN_PAD:FFN_PAD + T, :] = x_ref[...].astype(f32)
        wg = (g0[...], g1[...], g2[...], gb[...])
        wv = (v0[...], v1[...], v2[...], vb[...])

        def conv(p, w, r):
            return (w[0] * p[pl.ds(FFN_PAD + r - 2, rc), :] + w[1] * p[pl.ds(FFN_PAD + r - 1, rc), :]
                    + w[2] * p[pl.ds(FFN_PAD + r, rc), :] + w[3])

        for r in range(0, T, rc):
            o_ref[pl.ds(r, rc), :] = (jax.nn.silu(conv(gp, wg, r)) * conv(vp, wv, r)).astype(o_ref.dtype)

    gspec = pl.BlockSpec((T, FFN_TC), lambda j: (0, j))
    vspec = pl.BlockSpec((T, FFN_TC), lambda j: (0, j + nj))
    pg = pl.BlockSpec((1, FFN_TC), lambda j: (0, j))
    pv = pl.BlockSpec((1, FFN_TC), lambda j: (0, j + nj))
    return pl.pallas_call(
        body, name="ffn_act_fwd", grid=(nj,),
        in_specs=[gspec, vspec, pg, pg, pg, pg, pv, pv, pv, pv], out_specs=gspec,
        out_shape=jax.ShapeDtypeStruct((T, F), bf16),
        scratch_shapes=[pltpu.VMEM((FFN_PAD + T, FFN_TC), f32)] * 2, compiler_params=_cparams(1),
    )(up, up, cw[0], cw[1], cw[2], cb, cw[0], cw[1], cw[2], cb)


def ffn_act_bwd(up, cw, cb, dact, T):
    F = up.shape[1] // 2
    nj = F // FFN_TC

    rc = min(128, T)
    ext = rc + FFN_PAD

    def body(g_ref, v_ref, g0, g1, g2, gb, v0, v1, v2, vb, d_ref, dupg_ref, dupv_ref, *rest):
        p_refs, (gp, vp, dp, dgs, dvs) = rest[:8], rest[8:]
        zeros = jnp.zeros((FFN_PAD, FFN_TC), f32)
        for p, x_ref in ((gp, g_ref), (vp, v_ref)):
            p[0:FFN_PAD, :] = zeros
            p[FFN_PAD:FFN_PAD + T, :] = x_ref[...].astype(f32)
            p[FFN_PAD + T:FFN_PAD + T + FFN_PAD, :] = zeros
        dp[0:T, :] = d_ref[...]
        dp[T:T + FFN_PAD, :] = zeros
        wg = (g0[...], g1[...], g2[...], gb[...])
        wv = (v0[...], v1[...], v2[...], vb[...])
        acc = [jnp.zeros((1, FFN_TC), f32) for _ in range(8)]

        def taps(p, r):
            return tuple(p[pl.ds(FFN_PAD + r - s, ext), :] for s in (2, 1, 0))

        for r in range(0, T, rc):
            xg, xv = taps(gp, r), taps(vp, r)
            gc = wg[0] * xg[0] + wg[1] * xg[1] + wg[2] * xg[2] + wg[3]
            vc = wv[0] * xv[0] + wv[1] * xv[1] + wv[2] * xv[2] + wv[3]
            d = dp[pl.ds(r, ext), :]
            sg = jax.nn.sigmoid(gc)
            sides = ((d * vc * (sg * (1.0 + gc * (1.0 - sg))), xg, wg, dgs, dupg_ref), (d * (gc * sg), xv, wv, dvs, dupv_ref))
            for side, (dc, x, w, buf, dup_ref) in enumerate(sides):
                buf[...] = dc
                dc0 = dc[0:rc]
                dup = w[2] * dc0 + w[1] * buf[pl.ds(1, rc), :] + w[0] * buf[pl.ds(2, rc), :]
                dup_ref[pl.ds(r, rc), :] = dup.astype(dup_ref.dtype)
                for k in range(3):
                    acc[4 * side + k] = acc[4 * side + k] + jnp.sum(dc0 * x[k][0:rc], axis=0, keepdims=True)
                acc[4 * side + 3] = acc[4 * side + 3] + jnp.sum(dc0, axis=0, keepdims=True)
        for k in range(8):
            p_refs[k][...] = acc[k]

    gspec = pl.BlockSpec((T, FFN_TC), lambda j: (0, j))
    vspec = pl.BlockSpec((T, FFN_TC), lambda j: (0, j + nj))
    pg = pl.BlockSpec((1, FFN_TC), lambda j: (0, j))
    pv = pl.BlockSpec((1, FFN_TC), lambda j: (0, j + nj))
    res = pl.pallas_call(
        body, name="ffn_act_bwd", grid=(nj,),
        in_specs=[gspec, vspec, pg, pg, pg, pg, pv, pv, pv, pv, gspec],
        out_specs=[gspec, gspec] + [pg] * 8,
        out_shape=[jax.ShapeDtypeStruct((T, F), bf16)] * 2 + [jax.ShapeDtypeStruct((1, F), f32)] * 8,
        scratch_shapes=[pltpu.VMEM((FFN_PAD + T + FFN_PAD, FFN_TC), f32)] * 2 + [pltpu.VMEM((T + FFN_PAD, FFN_TC), f32)]
        + [pltpu.VMEM((ext, FFN_TC), f32)] * 2,
        compiler_params=_cparams(1),
    )(up, up, cw[0], cw[1], cw[2], cb, cw[0], cw[1], cw[2], cb, dact)
    dup = concat_cols("dup_concat", [res[0], res[1]], T)
    return (dup,) + tuple(jnp.concatenate([res[2 + k], res[6 + k]], axis=1) for k in range(4))


BQ = 256
BK = 256
assert BQ == BK
NT_DIMS = (((1,), (1,)), ((), ()))
TN_DIMS = (((0,), (0,)), ((), ()))


def _split_dot(x, u):
    x1 = x.astype(bf16)
    x2 = (x - x1.astype(f32)).astype(bf16)
    n = x.shape[0]
    y = jnp.dot(jnp.concatenate([x1, x2], axis=0), u, preferred_element_type=f32)
    return y[0:n] + y[n:2 * n]


def _log_sigmoids(z):
    sp = jnp.log(1.0 + jnp.exp(-jnp.abs(z)))
    lsp = jnp.minimum(z, 0.0) - sp
    return lsp, lsp - z


def _stack_heads(x):
    head1 = lax.broadcasted_iota(jnp.int32, x.shape, 1) >= HEAD_DIM
    zero = jnp.zeros_like(x)
    return jnp.concatenate([jnp.where(head1, zero, x), jnp.where(head1, x, zero)], axis=0)


def _unstack_heads(y):
    head1 = lax.broadcasted_iota(jnp.int32, (BQ, y.shape[1]), 1) >= HEAD_DIM
    return jnp.where(head1, y[BQ:2 * BQ], y[0:BQ])


def _attn_masks():
    row = lax.broadcasted_iota(jnp.int32, (2 * BQ, BK), 0)
    col = lax.broadcasted_iota(jnp.int32, (2 * BQ, BK), 1)
    ur = lax.broadcasted_iota(jnp.int32, (BK, BK), 0)
    uc = lax.broadcasted_iota(jnp.int32, (BK, BK), 1)
    return (row & (BQ - 1)) - col, (ur > uc).astype(bf16), (ur < uc).astype(bf16)


def attn_fwd(q, k, v, T, gather=None):
    nq = T // BQ
    n_g = 0 if gather is None else len(gather)

    def body(*refs):
        q_ref, k_ref, v_ref = refs[:3]
        o_ref, tot_ref = refs[3 + n_g:5 + n_g]
        p, i = pl.program_id(0), pl.program_id(1)
        if n_g:
            start, forward, finish = _two_level_gather(refs[5 + n_g:5 + 2 * n_g], *refs[5 + 2 * n_g:])
            pl.when(jnp.logical_and(p == 0, i == 0))(start)
        qs = _stack_heads(q_ref[...])
        diff, u_after, _ = _attn_masks()

        def step(jb, carry, diagonal):
            acc, c = carry
            ks = pl.multiple_of(jb * BK, BK)
            kb = k_ref[pl.ds(ks, BK), :]
            vb = v_ref[pl.ds(ks, BK), :]
            z = lax.dot_general(qs, kb, NT_DIMS, preferred_element_type=f32)
            lsp, lm = _log_sigmoids(z)
            if diagonal:
                m = diff > 0
                lm = jnp.where(m, lm, 0.0)
            a = jnp.exp(lsp + _split_dot(lm, u_after))
            if diagonal:
                a = jnp.where(m, a, 0.0)
            acc = acc + jnp.exp(c) * jnp.dot(a.astype(bf16), vb, preferred_element_type=f32)
            return acc, c + jnp.sum(lm, axis=-1, keepdims=True)

        carry = step(i, (jnp.zeros((2 * BQ, 128), f32), jnp.zeros((2 * BQ, 1), f32)), True)
        acc, c = lax.fori_loop(0, i, lambda t, cr: step(i - 1 - t, cr, False), carry)
        o_ref[...] = _unstack_heads(acc).astype(o_ref.dtype)
        tot_ref[...] = _unstack_heads(jnp.broadcast_to(c, (2 * BQ, 128)))
        if n_g:
            @pl.when(jnp.logical_and(p == 3, i == nq - 1))
            def _():
                forward()
                finish()

    blk = pl.BlockSpec((BQ, 128), lambda p, i: (i, p))
    full = pl.BlockSpec((T, 128), lambda p, i: (0, p))
    bufs = [] if gather is None else gather_buffers(gather)
    res = pl.pallas_call(
        body, name="attn_fwd_gather" if n_g else "attn_fwd", grid=(4, nq),
        in_specs=[blk, full, full] + [_ANY] * n_g, out_specs=[blk, blk] + [_ANY] * n_g,
        out_shape=[jax.ShapeDtypeStruct((T, 512), bf16), jax.ShapeDtypeStruct((T, 512), f32)]
        + [jax.ShapeDtypeStruct(b.shape, b.dtype) for b in bufs],
        input_output_aliases={3 + k: 2 + k for k in range(n_g)},
        scratch_shapes=_sems(n_g, N_DEV - 1) if n_g else [], compiler_params=_cparams(2),
    )(q, k, v, *bufs)
    return res[0], res[1], list(res[2:])


def attn_bwd(q, k, v, do, tot, T, scatter=None):
    nq = T // BQ
    n_s = 0 if scatter is None else len(scatter[0])
    modes = [] if scatter is None else list(scatter[1])

    def body(*refs):
        q_ref, k_ref, v_ref, do_ref, tot_ref = refs[:5]
        dq_ref, dk_ref, dv_ref = refs[5 + 2 * n_s:8 + 2 * n_s]
        p, i = pl.program_id(0), pl.program_id(1)
        if n_s:
            def copies():
                return _chip_copies(refs[5:5 + n_s], refs[8 + 2 * n_s:8 + 3 * n_s], modes, *refs[8 + 3 * n_s:])

            @pl.when(jnp.logical_and(p == 0, i == 0))
            def _():
                for cp in copies():
                    cp.start()

        @pl.when(i == 0)
        def _():
            dk_ref[...] = jnp.zeros_like(dk_ref)
            dv_ref[...] = jnp.zeros_like(dv_ref)

        qs = _stack_heads(q_ref[...])
        dos = _stack_heads(do_ref[...].astype(bf16))
        totv = tot_ref[...]
        tots = jnp.concatenate([totv[:, 0:1], totv[:, HEAD_DIM:HEAD_DIM + 1]], axis=0)
        diff, u_after, u_before = _attn_masks()

        def step(jb, carry, diagonal):
            dq, cl, cg = carry
            ks = pl.multiple_of(jb * BK, BK)
            kb = k_ref[pl.ds(ks, BK), :]
            vb = v_ref[pl.ds(ks, BK), :]
            z = lax.dot_general(qs, kb, NT_DIMS, preferred_element_type=f32)
            lsp, lsn = _log_sigmoids(z)
            lm = lsn
            if diagonal:
                m = diff > 0
                lm = jnp.where(m, lsn, 0.0)
            a = jnp.exp(lsp + _split_dot(lm, u_after))
            if diagonal:
                a = jnp.where(m, a, 0.0)
            g = a * lax.dot_general(dos, vb, NT_DIMS, preferred_element_type=f32)
            bb = jnp.exp(lsp)
            aa = g * jnp.exp(lsn) - _split_dot(g, u_before) * bb
            if diagonal:
                aa = jnp.where(m, aa, 0.0)
                bb = jnp.where(m, bb, 0.0)
            cl = cl + jnp.sum(lm, axis=-1, keepdims=True)
            f = jnp.exp(tots - cl)
            dz = (f * aa - cg * bb).astype(bf16)
            cg = cg + f * jnp.sum(g, axis=-1, keepdims=True)
            dq = dq + jnp.dot(dz, kb, preferred_element_type=f32)
            dk_ref[pl.ds(ks, BK), :] += lax.dot_general(dz, qs, TN_DIMS, preferred_element_type=f32)
            dv_ref[pl.ds(ks, BK), :] += lax.dot_general((f * a).astype(bf16), dos, TN_DIMS, preferred_element_type=f32)
            return dq, cl, cg

        zc = jnp.zeros((2 * BQ, 1), f32)
        carry = lax.fori_loop(0, i, lambda jb, cr: step(jb, cr, False), (jnp.zeros((2 * BQ, 128), f32), zc, zc))
        dq, _, _ = step(i, carry, True)
        dq_ref[...] = _unstack_heads(dq)
        if n_s:
            @pl.when(jnp.logical_and(p == 3, i == nq - 1))
            def _():
                for cp in copies():
                    cp.wait()

    blk = pl.BlockSpec((BQ, 128), lambda p, i: (i, p))
    full = pl.BlockSpec((T, 128), lambda p, i: (0, p))
    shp = jax.ShapeDtypeStruct((T, 512), f32)
    s_list = [] if scatter is None else list(scatter[0])
    bufs = chip_buffers(s_list, modes) if n_s else []
    res = pl.pallas_call(
        body, name="attn_bwd_scatter" if n_s else "attn_bwd", grid=(4, nq),
        in_specs=[blk, full, full, blk, blk] + [_ANY] * (2 * n_s), out_specs=[blk, full, full] + [_ANY] * n_s,
        out_shape=[shp, shp, shp] + [jax.ShapeDtypeStruct(b.shape, b.dtype) for b in bufs],
        input_output_aliases={5 + n_s + k: 3 + k for k in range(n_s)},
        scratch_shapes=_sems(n_s, 3) if n_s else [], compiler_params=_cparams(2),
    )(q, k, v, do, tot, *s_list, *bufs)
    return res[0], res[1], res[2], list(res[3:])


def loss_head(y, target, T, tm=256):
    D = y.shape[1]

    def body(y_ref, t_ref, dy_ref, l_ref):
        i = pl.program_id(0)
        err = y_ref[...] - t_ref[...]
        dy_ref[...] = err * (1.0 / D)
        part = 0.5 * jnp.sum(jnp.sum(err * err, axis=-1, keepdims=True) * (1.0 / D), axis=0, keepdims=True)

        @pl.when(i == 0)
        def _():
            l_ref[...] = jnp.zeros_like(l_ref)

        l_ref[...] += jnp.broadcast_to(part, l_ref.shape)

    spec = pl.BlockSpec((tm, D), lambda i: (i, 0))
    return pl.pallas_call(
        body, name="loss_head", grid=(T // tm,), in_specs=[spec, spec],
        out_specs=[spec, pl.BlockSpec((1, 128), lambda i: (0, 0))],
        out_shape=[jax.ShapeDtypeStruct((T, D), f32), jax.ShapeDtypeStruct((1, 128), f32)],
        compiler_params=_cparams(1),
    )(y, target)


def _row_tile(rows, cols, offset=0, max_elems=128 * 1024):
    best = None
    for t in range(16, rows + 1, 16):
        if rows % t == 0 and offset % t == 0 and t * cols <= max_elems:
            best = t
    return best if best is not None else rows


def adamw(name, w, m, v, parts, layer, prev=None, row_off=0):
    L, R, C = w.shape
    n_parts = parts.shape[0]
    tr = _row_tile(R, C, row_off, max_elems=256 * 1024)
    assert R % tr == 0 and row_off % tr == 0, (name, R, row_off, tr)

    def body(w_ref, m_ref, v_ref, p_ref, *rest):
        g_ref, d_ref, nm_ref, nv_ref = rest[-4:]
        g = p_ref[0].astype(f32)
        for s in range(1, n_parts):
            g = g + p_ref[s].astype(f32)
        wv = w_ref[...]
        mn = ADAM_B1 * m_ref[...] + (1.0 - ADAM_B1) * g
        vn = ADAM_B2 * v_ref[...] + (1.0 - ADAM_B2) * jnp.square(g)
        m_hat = mn / (1.0 - ADAM_B1 ** ADAM_STEP)
        v_hat = vn / (1.0 - ADAM_B2 ** ADAM_STEP)
        g_ref[...] = g
        d_ref[...] = -ADAM_LR * (m_hat / (jnp.sqrt(v_hat) + ADAM_EPS) + ADAM_WD * wv)
        nm_ref[...] = mn
        nv_ref[...] = vn

    spec = pl.BlockSpec((None, tr, C), lambda i: (layer, i, 0))
    shp = jax.ShapeDtypeStruct((L, R, C), f32)
    n_prev = 0 if prev is None else 4
    return pl.pallas_call(
        body, name=name, grid=(R // tr,),
        in_specs=[spec, spec, spec, pl.BlockSpec((n_parts, tr, C), lambda i: (0, row_off // tr + i, 0))] + [_ANY] * n_prev,
        out_specs=[spec] * 4, out_shape=[shp] * 4, input_output_aliases={4 + k: k for k in range(n_prev)},
        compiler_params=_cparams(1),
    )(w, m, v, parts, *(prev or ()))


REPL = ["ln1_g", "sg_ln_g", "sg_ln_b", "sg_w", "sg_b", "cv_b", "cv_ln_g", "cv_ln_b", "q_norm_g", "k_norm_g", "ln2_g",
        "ffn_conv_b"]
SMALL_SHARDED = ["b_gate", "cv_w", "ffn_conv_w"]
BIG = ["w_in", "w_a_out", "w_b_out", "w_c_out", "w_up", "w_out", "w_down"]
COL_SHARDED = ["w_a_out", "w_b_out", "w_c_out"]
TRANSPOSED = ["w_in", "w_up"]


def _pack(arrs, rows):
    flat = jnp.concatenate([a.reshape(-1) for a in arrs])
    return jnp.pad(flat, (0, rows * 128 - flat.shape[0])).reshape(rows, 128)


def _pack_layers(arrs, rows):
    flat = jnp.concatenate([a.reshape(a.shape[0], -1) for a in arrs], axis=1)
    return jnp.pad(flat, ((0, 0), (0, rows * 128 - flat.shape[1]))).reshape(flat.shape[0], rows, 128)


def _unpack_layers(packed, shapes):
    flat = packed.reshape(packed.shape[0], -1)
    out, pos = [], 0
    for s in shapes:
        n = 1
        for d in s[1:]:
            n *= d
        out.append(flat[:, pos:pos + n].reshape(s))
        pos += n
    return out


def _rows_for(shapes, mult):
    n = 0
    for s in shapes:
        k = 1
        for d in s:
            k *= d
        n += k
    rows = -(-n // 128)
    return -(-rows // mult) * mult


W_NAMES = ['ln1_g', 'w_in', 'b_gate', 'sg_ln_g', 'sg_ln_b', 'sg_w', 'sg_b', 'w_a_out', 'cv_w', 'cv_b', 'cv_ln_g', 'cv_ln_b',
           'w_b_out', 'q_norm_g', 'k_norm_g', 'w_c_out', 'w_out', 'ln2_g', 'w_up', 'ffn_conv_w', 'ffn_conv_b', 'w_down']


def _forward_layer(x, P, T, late=None):
    D = x.shape[1]
    sv = {"x0": x}
    (h1,) = rowwise("rms_fwd", f_rms, [(x, D, 0)], [(P["ln1_g"], (1, D), None)], [(D, D, 0, bf16)], T, 256)
    z = mm_nt("in_proj", h1, P["w_in_t"], bf16, tko=1664, tc=D)
    sv["h1"], sv["z"] = h1, z
    (ya_in,) = rowwise("sgu_fwd", f_sgu, [(z, 512, 0), (z, 512, 1)],
                       [(P["sg_ln_g"], (1, 512), None), (P["sg_ln_b"], (1, 512), None),
                        (P["sg_w"], (8, CHUNK, CHUNK), None), (P["sg_b_t"], (CHUNK, 8), None)],
                       [(512, 512, 0, bf16)], T, CHUNK)
    (c0,) = rowwise("glu_fwd", f_glu, [(z, 512, 2), (z, 512, 3)], [], [(512, 512, 0, f32)], T, 256)
    c1 = conv31_fwd(c0, P["cv_w"], P["cv_b"], T)
    (c3,) = rowwise("lnsilu_fwd", f_lnsilu, [(c1, 512, 0)], [(P["cv_ln_g"], (1, 512), None), (P["cv_ln_b"], (1, 512), None)],
                    [(512, 512, 0, bf16)], T, 256)
    q8, kn, vb = rowwise("qkv_fwd", f_qkv, [(z, 512, 4), (z, 512, 5), (z, 512, 6)],
                         [(P["q_norm_g"], (1, HEAD_DIM), None), (P["k_norm_g"], (1, HEAD_DIM), None)],
                         [(512, 512, 0, bf16)] * 3, T, 256)
    o, tot, _ = attn_fwd(q8, kn, vb, T)
    if late is not None:
        P.update(late(o))
    ya, yb, yc = mm_many("branch_out", [ya_in, c3, o], [P["w_a_out"], P["w_b_out"], P["w_c_out"]], f32)
    (merged,) = rowwise("merge_fwd", f_merge,
                        [(z, 512, lambda c: 7 + c), (z, 512, lambda c: 9 + c), (z, 512, lambda c: 11 + c),
                         (ya, 512, lambda c: c), (yb, 512, lambda c: c), (yc, 512, lambda c: c)],
                        [(P["b_gate"][k], (1, 512), lambda c: c) for k in range(3)],
                        [(D, 512, lambda c: c, bf16)], T, 256, ncol=2)
    x1 = mm_nn("out_proj", merged, P["w_out"], f32, add=x)
    sv.update(ya_in=ya_in, ya=ya, c0=c0, c1=c1, c3=c3, yb=yb, q8=q8, kn=kn, vb=vb, o=o, tot=tot, yc=yc, merged=merged, x1=x1)
    (h2,) = rowwise("rms_fwd", f_rms, [(x1, D, 0)], [(P["ln2_g"], (1, D), None)], [(D, D, 0, bf16)], T, 256)
    up = mm_nt("up_proj", h2, P["w_up_t"], bf16, tko=1408, tc=D)
    act = ffn_act_fwd(up, P["ffn_conv_w"], P["ffn_conv_b"], T)
    x2 = mm_nn("down_proj", act, P["w_down"], f32, add=x1)
    sv.update(h2=h2, up=up, act=act)
    return x2, sv


def _backward_layer(dx2, P, sv, T, scatter=None):
    D = dx2.shape[1]
    G = {}
    G["w_down"] = mm_tn("dw_down", sv["act"], dx2, tk=1408)
    dact = mm_nt("d_act", dx2, P["w_down"], f32, tko=1408, tc=D)
    dup, dcw0, dcw1, dcw2, G["ffn_conv_b"] = ffn_act_bwd(sv["up"], P["ffn_conv_w"], P["ffn_conv_b"], dact, T)
    G["ffn_conv_w"] = jnp.concatenate([dcw0, dcw1, dcw2], axis=0)
    G["w_up"] = mm_tn("dw_up", dup, sv["h2"], tk=1408)
    dh2 = mm_nn("d_h2", dup, P["w_up_t"], f32, tm=512)
    dx1, G["ln2_g"] = rowwise_bwd("rms_bwd", f_rms, [(sv["x1"], D, 0)], [(P["ln2_g"], (1, D), None)], [(dh2, D, 0)],
                                  [f32], T, 256, adds=[(dx2, D, 0)])
    G["w_out"] = mm_tn("dw_out", sv["merged"], dx1, tk=1024)
    dmerged = mm_nt("d_merged", dx1, P["w_out"], f32)
    z = sv["z"]
    dg0, dg1, dg2, dya, dyb, dyc, db0, db1, db2 = rowwise_bwd(
        "merge_bwd", f_merge,
        [(z, 512, lambda c: 7 + c), (z, 512, lambda c: 9 + c), (z, 512, lambda c: 11 + c),
         (sv["ya"], 512, lambda c: c), (sv["yb"], 512, lambda c: c), (sv["yc"], 512, lambda c: c)],
        [(P["b_gate"][k], (1, 512), lambda c: c) for k in range(3)],
        [(dmerged, 512, lambda c: c)], [bf16] * 6, T, 256, ncol=2)
    G["b_gate"] = jnp.concatenate([db0, db1, db2], axis=0)
    dya_in, dc3, do = mm_many("d_branch_in", [dya, dyb, dyc], [P["w_a_out"], P["w_b_out"], P["w_c_out"]], f32,
                              transpose_w=True)
    G["w_c_out"] = mm_tn("dw_c_out", sv["o"], dyc, n_slab=CHUNK)
    dq8, dkn, dvb, received = attn_bwd(sv["q8"], sv["kn"], sv["vb"], do, sv["tot"], T, scatter=scatter)
    dzq, dzk, dzv, G["q_norm_g"], G["k_norm_g"] = rowwise_bwd(
        "qkv_bwd", f_qkv, [(z, 512, 4), (z, 512, 5), (z, 512, 6)],
        [(P["q_norm_g"], (1, HEAD_DIM), None), (P["k_norm_g"], (1, HEAD_DIM), None)],
        [(dq8, 512, 0), (dkn, 512, 0), (dvb, 512, 0)], [bf16] * 3, T, 256)
    G["w_b_out"] = mm_tn("dw_b_out", sv["c3"], dyb, n_slab=CHUNK)
    dc1, G["cv_ln_g"], G["cv_ln_b"] = rowwise_bwd(
        "lnsilu_bwd", f_lnsilu, [(sv["c1"], 512, 0)], [(P["cv_ln_g"], (1, 512), None), (P["cv_ln_b"], (1, 512), None)],
        [(dc3, 512, 0)], [f32], T, 256)
    dc0, G["cv_w"], G["cv_b"] = conv31_bwd(sv["c0"], P["cv_w"], dc1, T)
    dzp, dzgl = rowwise_bwd("glu_bwd", f_glu, [(z, 512, 2), (z, 512, 3)], [], [(dc0, 512, 0)], [bf16] * 2, T, 256)
    G["w_a_out"] = mm_tn("dw_a_out", sv["ya_in"], dya, n_slab=CHUNK)
    dzu, dzv_a, G["sg_ln_g"], G["sg_ln_b"], G["sg_w"], dsbt = rowwise_bwd(
        "sgu_bwd", f_sgu, [(z, 512, 0), (z, 512, 1)],
        [(P["sg_ln_g"], (1, 512), None), (P["sg_ln_b"], (1, 512), None), (P["sg_w"], (8, CHUNK, CHUNK), None),
         (P["sg_b_t"], (CHUNK, 8), None)],
        [(dya_in, 512, 0)], [bf16] * 2, T, CHUNK)
    G["sg_b"] = dsbt.T
    dz = concat_cols("dz_concat", [dzu, dzv_a, dzp, dzgl, dzq, dzk, dzv, dg0, dg1, dg2], T)
    G["w_in"] = mm_tn("dw_in", dz, sv["h1"], tk=1664)
    dh1 = mm_nn("d_h1", dz, P["w_in_t"], f32, tm=512)
    dx0, G["ln1_g"] = rowwise_bwd("rms_bwd", f_rms, [(sv["x0"], D, 0)], [(P["ln1_g"], (1, D), None)], [(dh1, D, 0)],
                                  [f32], T, 256, adds=[(dx1, D, 0)])
    return dx0, G, received


def kernel(x, ln1_g, w_in, b_gate, sg_ln_g, sg_ln_b, sg_w, sg_b, w_a_out, cv_w, cv_b, cv_ln_g, cv_ln_b, w_b_out, q_norm_g, k_norm_g, w_c_out, w_out, ln2_g, w_up, ffn_conv_w, ffn_conv_b, w_down, loss_target, m_ln1_g, m_w_in, m_b_gate, m_sg_ln_g, m_sg_ln_b, m_sg_w, m_sg_b, m_w_a_out, m_cv_w, m_cv_b, m_cv_ln_g, m_cv_ln_b, m_w_b_out, m_q_norm_g, m_k_norm_g, m_w_c_out, m_w_out, m_ln2_g, m_w_up, m_ffn_conv_w, m_ffn_conv_b, m_w_down, v_ln1_g, v_w_in, v_b_gate, v_sg_ln_g, v_sg_ln_b, v_sg_w, v_sg_b, v_w_a_out, v_cv_w, v_cv_b, v_cv_ln_g, v_cv_ln_b, v_w_b_out, v_q_norm_g, v_k_norm_g, v_w_c_out, v_w_out, v_ln2_g, v_w_up, v_ffn_conv_w, v_ffn_conv_b, v_w_down):
    W = dict(ln1_g=ln1_g, w_in=w_in, b_gate=b_gate, sg_ln_g=sg_ln_g, sg_ln_b=sg_ln_b, sg_w=sg_w, sg_b=sg_b, w_a_out=w_a_out,
             cv_w=cv_w, cv_b=cv_b, cv_ln_g=cv_ln_g, cv_ln_b=cv_ln_b, w_b_out=w_b_out, q_norm_g=q_norm_g, k_norm_g=k_norm_g,
             w_c_out=w_c_out, w_out=w_out, ln2_g=ln2_g, w_up=w_up, ffn_conv_w=ffn_conv_w, ffn_conv_b=ffn_conv_b, w_down=w_down)
    M = dict(ln1_g=m_ln1_g, w_in=m_w_in, b_gate=m_b_gate, sg_ln_g=m_sg_ln_g, sg_ln_b=m_sg_ln_b, sg_w=m_sg_w, sg_b=m_sg_b,
             w_a_out=m_w_a_out, cv_w=m_cv_w, cv_b=m_cv_b, cv_ln_g=m_cv_ln_g, cv_ln_b=m_cv_ln_b, w_b_out=m_w_b_out,
             q_norm_g=m_q_norm_g, k_norm_g=m_k_norm_g, w_c_out=m_w_c_out, w_out=m_w_out, ln2_g=m_ln2_g, w_up=m_w_up,
             ffn_conv_w=m_ffn_conv_w, ffn_conv_b=m_ffn_conv_b, w_down=m_w_down)
    V = dict(ln1_g=v_ln1_g, w_in=v_w_in, b_gate=v_b_gate, sg_ln_g=v_sg_ln_g, sg_ln_b=v_sg_ln_b, sg_w=v_sg_w, sg_b=v_sg_b,
             w_a_out=v_w_a_out, cv_w=v_cv_w, cv_b=v_cv_b, cv_ln_g=v_cv_ln_g, cv_ln_b=v_cv_ln_b, w_b_out=v_w_b_out,
             q_norm_g=v_q_norm_g, k_norm_g=v_k_norm_g, w_c_out=v_w_c_out, w_out=v_w_out, ln2_g=v_ln2_g, w_up=v_w_up,
             ffn_conv_w=v_ffn_conv_w, ffn_conv_b=v_ffn_conv_b, w_down=v_w_down)
    T, D = x.shape[1], x.shape[2]
    L = DEPTH
    xs = x.reshape(T, D)
    target = loss_target.reshape(T, D)

    ss_shapes = [W[n].shape for n in SMALL_SHARDED]
    ss_rows = _rows_for(ss_shapes, 8)
    (ss_all,), small_token = exchange("gather_small", [_pack([W[n] for n in SMALL_SHARDED], ss_rows)], ["gather"])
    full_small = {}
    pos = 0
    for n in SMALL_SHARDED:
        s = W[n].shape
        cnt = s[0] * s[1] * s[2]
        part = ss_all.reshape(N_DEV, -1)[:, pos:pos + cnt].reshape((N_DEV,) + s)
        full_small[n] = jnp.transpose(part, (1, 2, 0, 3)).reshape(s[0], s[1], N_DEV * s[2])
        pos += cnt

    params, saved = [], []
    cur = xs
    Wt = {n: (jnp.transpose(W[n], (0, 2, 1)) if n in TRANSPOSED else W[n]) for n in BIG}

    def local_slabs(l):
        return [Wt[n][l].astype(bf16) for n in BIG]

    def weights_of(names, slabs):
        out = {}
        for n, s in zip(names, slabs):
            if n in TRANSPOSED:
                out[n + "_t"] = s.reshape(-1, s.shape[-1])
            else:
                out[n] = assemble("assemble_" + n, s) if n in COL_SHARDED else s.reshape(-1, s.shape[-1])
        return out

    LATE = [n for n in BIG if n != "w_in"]
    def behind(token, blocks):
        return [blocks[0] + token[0, 0].astype(blocks[0].dtype)] + list(blocks[1:])

    first, token = gather2("gather_w_in", behind(small_token, [local_slabs(0)[0]]))
    early_part, late_part = {}, {}
    for l in range(L):
        if l > 0:
            early_part[l] = gather_start("gather_start_in_%d" % l, behind(token, local_slabs(l)[:1]))
            token = early_part[l][3]
        late_part[l] = gather_start("gather_start_rest_%d" % l, behind(token, local_slabs(l)[1:]))
        token = late_part[l][3]
    start_token = token[0, 0]

    def late_weights(l):
        send_sems, recv_sems, bufs, _ = late_part[l]
        return lambda o: weights_of(LATE, gather_wait("gather_wait_rest_%d" % l, send_sems, recv_sems, bufs, o))

    for l in range(L):
        if l == 0:
            P = weights_of(["w_in"], first)
        else:
            send_sems, recv_sems, bufs, _ = early_part[l]
            P = weights_of(["w_in"], gather_wait("gather_wait_in_%d" % l, send_sems, recv_sems, bufs, cur))
        late = late_weights(l)
        for n in REPL:
            P[n] = W[n][l]
        for n in ("ln1_g", "sg_ln_g", "sg_ln_b", "cv_b", "cv_ln_g", "cv_ln_b", "q_norm_g", "k_norm_g", "ln2_g", "ffn_conv_b"):
            P[n] = P[n].reshape(1, -1)
        P["sg_b_t"] = P["sg_b"].T
        P["cv_w"] = full_small["cv_w"][l]
        P["b_gate"] = [full_small["b_gate"][l][k:k + 1] for k in range(3)]
        P["ffn_conv_w"] = [full_small["ffn_conv_w"][l][k:k + 1] for k in range(3)]
        if l == 0:
            P["ln1_g"] = P["ln1_g"] + start_token
        cur, sv = _forward_layer(cur, P, T, late=late)
        params.append(P)
        saved.append(sv)

    dy, loss_row = loss_head(cur, target, T)
    loss = lax.psum(loss_row[0, 0], ("x", "y", "c"))

    repl_shapes = [W[n].shape for n in REPL]
    repl_rows = _rows_for([s[1:] for s in repl_shapes], 16)
    ssl_rows = _rows_for([s[1:] for s in ss_shapes], 16)
    state = {"repl": [_pack_layers([X[n] for n in REPL], repl_rows) for X in (W, M, V)],
             "ss": [_pack_layers([X[n] for n in SMALL_SHARDED], ssl_rows) for X in (W, M, V)]}
    big_rows, big_off = {}, {}
    for n in BIG:
        big_off[n] = sum(big_rows.values())
        big_rows[n] = W[n].shape[1] * W[n].shape[2] // D
        state[n] = [(jnp.transpose(X[n], (0, 2, 1)) if n in TRANSPOSED else X[n]).reshape(L, big_rows[n], D)
                    for X in (W, M, V)]
    done = {n: None for n in state}

    def update(layer, received):
        for n in BIG:
            done[n] = adamw("adamw_" + n, *state[n], received[0], layer, done[n], row_off=big_off[n])
        done["repl"] = adamw("adamw_repl", *state["repl"], received[1], layer, done["repl"])
        done["ss"] = adamw("adamw_ss", *state["ss"], received[2], layer, done["ss"])

    def reduce_in_chip(G):
        big = [G[n].reshape(4, 2, big_rows[n], D) for n in BIG]
        ss_parts = []
        for n in SMALL_SHARDED:
            k, c = W[n].shape[1:]
            ss_parts.append(jnp.transpose(G[n].reshape(k, N_DEV, c), (1, 0, 2)).reshape(N_DEV, k * c))
        ss_send = jnp.concatenate(ss_parts, axis=1)
        ss_send = jnp.pad(ss_send, ((0, 0), (0, ssl_rows * 128 - ss_send.shape[1]))).reshape(4, 2, ssl_rows, 128)
        repl = _pack([G[n] for n in REPL], repl_rows)
        core = lax.axis_index("c")
        kept = jnp.concatenate([lax.dynamic_index_in_dim(b, core, 1, keepdims=False) for b in big], axis=1)
        ss_kept = lax.dynamic_index_in_dim(ss_send, core, 1, keepdims=False)
        got, repl_got, ss_got = pair_exchange("pair_exchange", big, [repl, ss_send], ["gather", "scatter"])
        return [add2("pair_add_big", kept, got), add2("pair_add_repl", repl, repl_got), add2("pair_add_ss", ss_kept, ss_got)]

    chip_modes = ["scatter", "gather", "scatter"]
    dcur = dy
    pending = None
    for l in reversed(range(L)):
        P = params[l]
        if pending is not None:
            P = dict(P, ffn_conv_b=P["ffn_conv_b"] + pending[4][0, 0])
        dcur, G, _ = _backward_layer(dcur, P, saved[l], T)
        started = chip_start("chip_start_%d" % l, reduce_in_chip(G), chip_modes)
        if pending is not None:
            after = started[4] + dcur[0:8, 0:128]
            update(l + 1, chip_wait("chip_wait_%d" % (l + 1), *pending[:4], chip_modes, after))
        pending = started
    after = sum(done[n][0][1, 0:1, 0:1] for n in state)
    update(0, chip_wait("chip_wait_0", *pending[:4], chip_modes, after))

    results = [{}, {}, {}, {}]
    for k in range(4):
        for n, a in zip(REPL, _unpack_layers(done["repl"][k], repl_shapes)):
            results[k][n] = a
        for n, a in zip(SMALL_SHARDED, _unpack_layers(done["ss"][k], ss_shapes)):
            results[k][n] = a
        for n in BIG:
            results[k][n] = jnp.transpose(done[n][k], (0, 2, 1)) if n in TRANSPOSED else done[n][k].reshape(W[n].shape)
    out = [loss, dcur.reshape(1, T, D)]
    for k in range(4):
        out += [results[k][n] for n in W_NAMES]
    return tuple(out)
```

```python
import functools

import jax
import jax.numpy as jnp
from jax import lax
from jax.experimental import pallas as pl
from jax.experimental.pallas import tpu as pltpu

f32 = jnp.float32
bf16 = jnp.bfloat16

EPS = 1e-6
N_DEV = 8
DEPTH = 4
CHUNK = 128
HEAD_DIM = 64
N_HEADS = 8
CV_KERNEL = 31
VMEM_LIMIT_BYTES = 56 * 2 ** 20

ADAM_LR = 0.001
ADAM_B1 = 0.9
ADAM_B2 = 0.999
ADAM_EPS = 1e-08
ADAM_WD = 0.01
ADAM_STEP = 10

MESH = pl.DeviceIdType.MESH


def _cparams(n_grid):
    return pltpu.CompilerParams(dimension_semantics=("arbitrary",) * n_grid, vmem_limit_bytes=VMEM_LIMIT_BYTES)


def exchange(name, arrays, modes):
    n = len(arrays)

    def body(*refs):
        copies = _direct_copies(refs[:n], refs[n:2 * n], modes, *refs[2 * n + 1:])
        for cp in copies:
            cp.start()
        for cp in copies:
            cp.wait()
        refs[2 * n][...] = jnp.zeros_like(refs[2 * n])

    res = pl.pallas_call(
        body, name=name, out_shape=_exchange_out_shapes(arrays, modes) + [jax.ShapeDtypeStruct((8, 128), f32)],
        in_specs=[_ANY] * n, out_specs=[_ANY] * n + [pl.BlockSpec(memory_space=pltpu.VMEM)],
        scratch_shapes=_exchange_sems(n),
    )(*arrays)
    return list(res[:n]), res[n]


_ANY = pl.BlockSpec(memory_space=pl.ANY)


def _exchange_out_shapes(arrays, modes):
    return [jax.ShapeDtypeStruct((N_DEV,) + tuple(a.shape) if m == "gather" else tuple(a.shape), a.dtype)
            for a, m in zip(arrays, modes)]


def _exchange_sems(n):
    return [pltpu.SemaphoreType.DMA((n, N_DEV - 1)), pltpu.SemaphoreType.DMA((n, N_DEV - 1)), pltpu.SemaphoreType.DMA((n,))]


def _direct_copies(ins, outs, modes, send_sems, recv_sems, local_sems):
    x, y, c = lax.axis_index("x"), lax.axis_index("y"), lax.axis_index("c")
    me = 4 * x + 2 * y + c
    copies = []
    for k in range(len(ins)):
        src_mine = ins[k] if modes[k] == "gather" else ins[k].at[me]
        copies.append(pltpu.make_async_copy(src_mine, outs[k].at[me], local_sems.at[k]))
        for r in range(1, N_DEV):
            px = 1 - x if r & 4 else x
            py = 1 - y if r & 2 else y
            pc = 1 - c if r & 1 else c
            src = ins[k] if modes[k] == "gather" else ins[k].at[4 * px + 2 * py + pc]
            copies.append(pltpu.make_async_remote_copy(
                src_ref=src, dst_ref=outs[k].at[me], send_sem=send_sems.at[k, r - 1], recv_sem=recv_sems.at[k, r - 1],
                device_id=(px, py, pc), device_id_type=MESH))
    return copies


def own_slot_buffer(block, n_slots, index):
    return lax.dynamic_update_slice(lax.empty((n_slots,) + tuple(block.shape), block.dtype), block[None],
                                    (index,) + (0,) * block.ndim)


def _sems(n, m):
    return [pltpu.SemaphoreType.DMA((n, m)), pltpu.SemaphoreType.DMA((n, m))]


def _two_level_gather(bufs, send_sems, recv_sems):
    x, y, c = lax.axis_index("x"), lax.axis_index("y"), lax.axis_index("c")
    me = 4 * x + 2 * y + c
    sibling = (x, y, 1 - c)
    chips = [(1 - x, y), (x, 1 - y), (1 - x, 1 - y)]

    def slot(px, py, pc):
        return 4 * px + 2 * py + pc

    def copy(k, sem, block, to):
        rows = bufs[k].at[block]
        return pltpu.make_async_remote_copy(src_ref=rows, dst_ref=rows, send_sem=send_sems.at[k, sem],
                                            recv_sem=recv_sems.at[k, sem], device_id=to, device_id_type=MESH)

    def first(k):
        return [copy(k, 0, me, sibling)] + [copy(k, 1 + j, me, (*chip, c)) for j, chip in enumerate(chips)]

    def passed_on(k, j):
        return copy(k, 4 + j, slot(*chips[j], c), sibling)

    def start():
        for k in range(len(bufs)):
            for cp in first(k):
                cp.start()

    def forward():
        for k in range(len(bufs)):
            for j in range(3):
                copy(k, 1 + j, slot(*chips[j], c), sibling).wait_recv()
                passed_on(k, j).start()

    def finish():
        for k in range(len(bufs)):
            copy(k, 0, slot(x, y, 1 - c), sibling).wait_recv()
            for j, chip in enumerate(chips):
                copy(k, 4 + j, slot(*chip, 1 - c), sibling).wait_recv()
            for cp in first(k) + [passed_on(k, j) for j in range(3)]:
                cp.wait_send()

    return start, forward, finish


def pair_exchange(name, big, smalls, modes):
    nb, ns = len(big), len(smalls)
    rows = [a.shape[2] for a in big]
    total, width = sum(rows), big[0].shape[3]
    n = nb + ns

    def body(*refs):
        ins, outs = refs[:n], refs[n:n + 1 + ns]
        send_sems, recv_sems = refs[-2:]
        x, y, c = lax.axis_index("x"), lax.axis_index("y"), lax.axis_index("c")
        copies = []

        def remote(k, src, dst):
            copies.append(pltpu.make_async_remote_copy(src_ref=src, dst_ref=dst, send_sem=send_sems.at[k, 0],
                                                       recv_sem=recv_sems.at[k, 0], device_id=(x, y, 1 - c),
                                                       device_id_type=MESH))

        off = 0
        for k in range(nb):
            remote(k, ins[k].at[:, 1 - c], outs[0].at[:, pl.ds(off, rows[k]), :])
            off += rows[k]
        for j in range(ns):
            remote(nb + j, ins[nb + j].at[:, 1 - c] if modes[j] == "scatter" else ins[nb + j], outs[1 + j])
        for cp in copies:
            cp.start()
        for cp in copies:
            cp.wait()

    out_shape = [jax.ShapeDtypeStruct((4, total, width), big[0].dtype)]
    for a, m in zip(smalls, modes):
        out_shape.append(jax.ShapeDtypeStruct((4,) + tuple(a.shape[2:]) if m == "scatter" else tuple(a.shape), a.dtype))
    return pl.pallas_call(
        body, name=name, out_shape=out_shape, in_specs=[_ANY] * n, out_specs=[_ANY] * len(out_shape),
        scratch_shapes=_sems(n, 1),
    )(*big, *smalls)


def add2(name, a, b):
    shape = a.shape
    a2, b2 = a.reshape(-1, shape[-1]), b.reshape(-1, shape[-1])
    R, C = a2.shape
    tr = _row_tile(R, C, max_elems=1200 * 1024)

    def body(a_ref, b_ref, o_ref):
        o_ref[...] = (a_ref[...].astype(f32) + b_ref[...].astype(f32)).astype(o_ref.dtype)

    spec = pl.BlockSpec((tr, C), lambda i: (i, 0))
    return pl.pallas_call(
        body, name=name, grid=(R // tr,), in_specs=[spec, spec], out_specs=spec,
        out_shape=jax.ShapeDtypeStruct((R, C), a.dtype), compiler_params=_cparams(1),
    )(a2, b2).reshape(shape)


def chip_buffers(arrays, modes):
    mine = 2 * lax.axis_index("x") + lax.axis_index("y")
    return [own_slot_buffer(lax.dynamic_index_in_dim(a, mine, 0, keepdims=False) if m == "scatter" else a, 4, mine)
            for a, m in zip(arrays, modes)]


def _chip_copies(ins, bufs, modes, send_sems, recv_sems):
    x, y, c = lax.axis_index("x"), lax.axis_index("y"), lax.axis_index("c")
    mine = 2 * x + y
    copies = []
    for k in range(len(ins)):
        for r in range(1, 4):
            px = 1 - x if r & 2 else x
            py = 1 - y if r & 1 else y
            src = ins[k].at[2 * px + py] if modes[k] == "scatter" else ins[k]
            copies.append(pltpu.make_async_remote_copy(
                src_ref=src, dst_ref=bufs[k].at[mine], send_sem=send_sems.at[k, r - 1], recv_sem=recv_sems.at[k, r - 1],
                device_id=(px, py, c), device_id_type=MESH))
    return copies


def chip_exchange(name, arrays, modes):
    n = len(arrays)

    def body(*refs):
        copies = _chip_copies(refs[:n], refs[2 * n:3 * n], modes, *refs[3 * n:])
        for cp in copies:
            cp.start()
        for cp in copies:
            cp.wait()

    bufs = chip_buffers(arrays, modes)
    return pl.pallas_call(
        body, name=name, out_shape=[jax.ShapeDtypeStruct(b.shape, b.dtype) for b in bufs],
        in_specs=[_ANY] * (2 * n), out_specs=[_ANY] * n, input_output_aliases={n + k: k for k in range(n)},
        scratch_shapes=_sems(n, 3),
    )(*arrays, *bufs)


def gather_buffers(blocks):
    me = 4 * lax.axis_index("x") + 2 * lax.axis_index("y") + lax.axis_index("c")
    return [own_slot_buffer(b, N_DEV, me) for b in blocks]


_HBM = pl.BlockSpec(memory_space=pltpu.HBM)
_SEM = pl.BlockSpec(memory_space=pltpu.SEMAPHORE)
_SPLIT_PARAMS = pltpu.CompilerParams(has_side_effects=pltpu.SideEffectType.DATAFLOW_SIDE_EFFECTING)


def _split_gather_copies(bufs, send_sems, recv_sems):
    x, y, c = lax.axis_index("x"), lax.axis_index("y"), lax.axis_index("c")
    me = 4 * x + 2 * y + c
    copies = []
    for k in range(len(bufs)):
        rows = bufs[k].at[me]
        for r in range(1, N_DEV):
            peer = (1 - x if r & 4 else x, 1 - y if r & 2 else y, 1 - c if r & 1 else c)
            copies.append(pltpu.make_async_remote_copy(
                src_ref=rows, dst_ref=rows, send_sem=send_sems.at[(N_DEV - 1) * k + r - 1],
                recv_sem=recv_sems.at[(N_DEV - 1) * k + r - 1], device_id=peer, device_id_type=MESH))
    return copies


def gather_start(name, blocks):
    n = len(blocks)
    bufs = gather_buffers(blocks)

    def body(*refs):
        for cp in _split_gather_copies(refs[n + 2:2 * n + 2], refs[n], refs[n + 1]):
            cp.start()
        refs[2 * n + 2][...] = jnp.zeros_like(refs[2 * n + 2])

    n_sem = n * (N_DEV - 1)
    res = pl.pallas_call(
        body, name=name,
        out_shape=(pltpu.SemaphoreType.DMA((n_sem,)), pltpu.SemaphoreType.DMA((n_sem,)),
                   *[pltpu.HBM(b.shape, b.dtype) for b in bufs], jax.ShapeDtypeStruct((8, 128), f32)),
        in_specs=[_HBM] * n, out_specs=(_SEM, _SEM, *[_HBM] * n, pl.BlockSpec(memory_space=pltpu.VMEM)),
        input_output_aliases={k: 2 + k for k in range(n)}, compiler_params=_SPLIT_PARAMS,
    )(*[pltpu.with_memory_space_constraint(b, pltpu.HBM) for b in bufs])
    return res[0], res[1], list(res[2:2 + n]), res[2 + n]


def gather_wait(name, send_sems, recv_sems, bufs, after):
    n = len(bufs)

    def body(*refs):
        for cp in _split_gather_copies(refs[:n], refs[n], refs[n + 1]):
            cp.wait_send()
            cp.wait_recv()

    return list(pl.pallas_call(
        body, name=name, out_shape=tuple(pltpu.HBM(b.shape, b.dtype) for b in bufs),
        in_specs=[_HBM] * n + [_SEM, _SEM, _ANY], out_specs=tuple([_HBM] * n),
        input_output_aliases={k: k for k in range(n)}, compiler_params=_SPLIT_PARAMS,
    )(*bufs, send_sems, recv_sems, after))


def _split_chip_copies(srcs, bufs, modes, send_sems, recv_sems):
    x, y, c = lax.axis_index("x"), lax.axis_index("y"), lax.axis_index("c")
    mine = 2 * x + y
    copies = []
    for k in range(len(srcs)):
        for r in range(1, 4):
            px = 1 - x if r & 2 else x
            py = 1 - y if r & 1 else y
            src = srcs[k].at[2 * px + py] if modes[k] == "scatter" else srcs[k]
            copies.append(pltpu.make_async_remote_copy(
                src_ref=src, dst_ref=bufs[k].at[mine], send_sem=send_sems.at[3 * k + r - 1],
                recv_sem=recv_sems.at[3 * k + r - 1], device_id=(px, py, c), device_id_type=MESH))
    return copies


def chip_start(name, arrays, modes):
    n = len(arrays)
    bufs = chip_buffers(arrays, modes)

    def body(*refs):
        for cp in _split_chip_copies(refs[2 * n + 2:3 * n + 2], refs[3 * n + 2:4 * n + 2], modes, refs[2 * n], refs[2 * n + 1]):
            cp.start()
        refs[4 * n + 2][...] = jnp.zeros_like(refs[4 * n + 2])

    both = list(arrays) + list(bufs)
    res = pl.pallas_call(
        body, name=name,
        out_shape=(pltpu.SemaphoreType.DMA((3 * n,)), pltpu.SemaphoreType.DMA((3 * n,)),
                   *[pltpu.HBM(b.shape, b.dtype) for b in both], jax.ShapeDtypeStruct((8, 128), f32)),
        in_specs=[_HBM] * (2 * n), out_specs=(_SEM, _SEM, *[_HBM] * (2 * n), pl.BlockSpec(memory_space=pltpu.VMEM)),
        input_output_aliases={k: 2 + k for k in range(2 * n)}, compiler_params=_SPLIT_PARAMS,
    )(*[pltpu.with_memory_space_constraint(b, pltpu.HBM) for b in both])
    return res[0], res[1], list(res[2:2 + n]), list(res[2 + n:2 + 2 * n]), res[2 + 2 * n]


def chip_wait(name, send_sems, recv_sems, srcs, bufs, modes, after):
    n = len(bufs)

    def body(*refs):
        for cp in _split_chip_copies(refs[:n], refs[n:2 * n], modes, refs[2 * n], refs[2 * n + 1]):
            cp.wait_send()
            cp.wait_recv()

    both = list(srcs) + list(bufs)
    res = pl.pallas_call(
        body, name=name, out_shape=tuple(pltpu.HBM(b.shape, b.dtype) for b in both),
        in_specs=[_HBM] * (2 * n) + [_SEM, _SEM, _ANY], out_specs=tuple([_HBM] * (2 * n)),
        input_output_aliases={k: k for k in range(2 * n)}, compiler_params=_SPLIT_PARAMS,
    )(*both, send_sems, recv_sems, after)
    return list(res[n:])


def gather2(name, arrays):
    n = len(arrays)

    def body(*refs):
        start, forward, finish = _two_level_gather(refs[n:2 * n], *refs[2 * n + 1:])
        start()
        forward()
        finish()
        refs[2 * n][...] = jnp.zeros_like(refs[2 * n])

    bufs = gather_buffers(arrays)
    res = pl.pallas_call(
        body, name=name, out_shape=[jax.ShapeDtypeStruct(b.shape, b.dtype) for b in bufs] + [jax.ShapeDtypeStruct((8, 128), f32)],
        in_specs=[_ANY] * n, out_specs=[_ANY] * n + [pl.BlockSpec(memory_space=pltpu.VMEM)],
        input_output_aliases={k: k for k in range(n)}, scratch_shapes=_sems(n, N_DEV - 1),
    )(*bufs)
    return list(res[:n]), res[n]


def assemble(name, slabs):
    _, K, Ns = slabs.shape
    g = N_DEV if Ns % 128 == 0 else 2

    def body(w_ref, o_ref):
        o_ref[...] = jnp.concatenate([w_ref[s] for s in range(g)], axis=1)

    return pl.pallas_call(
        body, name=name, grid=(N_DEV // g,),
        in_specs=[pl.BlockSpec((g, K, Ns), lambda m: (m, 0, 0))],
        out_specs=pl.BlockSpec((K, g * Ns), lambda m: (0, m)),
        out_shape=jax.ShapeDtypeStruct((K, N_DEV * Ns), slabs.dtype),
        compiler_params=_cparams(1),
    )(slabs)


def mm_nn(name, a, b, out_dtype, add=None, tm=1024, tn=512, tk=None):
    M, K = a.shape
    N = b.shape[1]
    tm, tn = min(tm, M), min(tn, N)
    tk = K if tk is None else tk
    nk = K // tk
    has_add = add is not None

    def body(a_ref, b_ref, *rest):
        o_ref = rest[1] if has_add else rest[0]
        part = jnp.dot(a_ref[...].astype(bf16), b_ref[...].astype(bf16), preferred_element_type=f32)
        if nk == 1:
            o_ref[...] = (part + rest[0][...] if has_add else part).astype(o_ref.dtype)
            return
        acc_ref = rest[-1]
        k = pl.program_id(2)

        @pl.when(k == 0)
        def _():
            acc_ref[...] = part + rest[0][...] if has_add else part

        @pl.when(k > 0)
        def _():
            acc_ref[...] += part

        @pl.when(k == nk - 1)
        def _():
            o_ref[...] = acc_ref[...].astype(o_ref.dtype)

    in_specs = [pl.BlockSpec((tm, tk), lambda i, j, k: (i, k)), pl.BlockSpec((tk, tn), lambda i, j, k: (k, j))]
    ops = [a, b]
    if has_add:
        in_specs.append(pl.BlockSpec((tm, tn), lambda i, j, k: (i, j)))
        ops.append(add)
    return pl.pallas_call(
        body, name=name, grid=(M // tm, N // tn, nk), in_specs=in_specs,
        out_specs=pl.BlockSpec((tm, tn), lambda i, j, k: (i, j)),
        out_shape=jax.ShapeDtypeStruct((M, N), out_dtype),
        scratch_shapes=[pltpu.VMEM((tm, tn), f32)] if nk > 1 else [], compiler_params=_cparams(3),
    )(*ops)


def mm_many(name, xs, ws, out_dtype, transpose_w=False, tm=512):
    n = len(xs)
    M = xs[0].shape[0]
    tm = min(tm, M)
    dims = NT_DIMS if transpose_w else (((1,), (0,)), ((), ()))

    def body(*refs):
        for k in range(n):
            refs[2 * n + k][...] = lax.dot_general(refs[k][...].astype(bf16), refs[n + k][...].astype(bf16), dims,
                                                   preferred_element_type=f32).astype(out_dtype)

    n_out = [w.shape[0] if transpose_w else w.shape[1] for w in ws]
    return pl.pallas_call(
        body, name=name, grid=(M // tm,),
        in_specs=[pl.BlockSpec((tm, x.shape[1]), lambda i: (i, 0)) for x in xs]
        + [pl.BlockSpec(w.shape, lambda i: (0, 0)) for w in ws],
        out_specs=[pl.BlockSpec((tm, c), lambda i: (i, 0)) for c in n_out],
        out_shape=[jax.ShapeDtypeStruct((M, c), out_dtype) for c in n_out], compiler_params=_cparams(1),
    )(*xs, *ws)


def concat_cols(name, pieces, T, tm=512):
    widths = [p.shape[1] for p in pieces]
    total = sum(widths)
    tm = min(tm, T)

    def body(*refs):
        o_ref = refs[-1]
        off = 0
        for r, w in zip(refs[:-1], widths):
            o_ref[:, off:off + w] = r[...]
            off += w

    return pl.pallas_call(
        body, name=name, grid=(T // tm,),
        in_specs=[pl.BlockSpec((tm, w), lambda i: (i, 0)) for w in widths],
        out_specs=pl.BlockSpec((tm, total), lambda i: (i, 0)),
        out_shape=jax.ShapeDtypeStruct((T, total), pieces[0].dtype), compiler_params=_cparams(1),
    )(*pieces)


def mm_nt(name, a, b, out_dtype, tm=1024, tko=None, tc=None):
    M, C = a.shape
    Ko = b.shape[0]
    tm = min(tm, M)
    tc = C if tc is None else min(tc, C)
    tko = Ko if tko is None else tko
    nc = C // tc

    def body(a_ref, b_ref, o_ref, *scratch):
        part = lax.dot_general(a_ref[...].astype(bf16), b_ref[...].astype(bf16), (((1,), (1,)), ((), ())),
                               preferred_element_type=f32)
        if nc == 1:
            o_ref[...] = part.astype(o_ref.dtype)
            return
        acc_ref = scratch[0]
        c = pl.program_id(2)

        @pl.when(c == 0)
        def _():
            acc_ref[...] = part

        @pl.when(c > 0)
        def _():
            acc_ref[...] += part

        @pl.when(c == nc - 1)
        def _():
            o_ref[...] = acc_ref[...].astype(o_ref.dtype)

    return pl.pallas_call(
        body, name=name, grid=(M // tm, Ko // tko, nc),
        in_specs=[pl.BlockSpec((tm, tc), lambda i, j, c: (i, c)), pl.BlockSpec((tko, tc), lambda i, j, c: (j, c))],
        out_specs=pl.BlockSpec((tm, tko), lambda i, j, c: (i, j)),
        out_shape=jax.ShapeDtypeStruct((M, Ko), out_dtype),
        scratch_shapes=[pltpu.VMEM((tm, tko), f32)] if nc > 1 else [], compiler_params=_cparams(3),
    )(a, b)


def mm_tn(name, a, b, n_slab=None, tk=512, tn=512):
    T, K = a.shape
    N = b.shape[1]
    tk = min(tk, K)
    if n_slab is None:
        tn = min(tn, N)

        def body(a_ref, b_ref, o_ref):
            o_ref[...] = lax.dot_general(a_ref[...].astype(bf16), b_ref[...].astype(bf16), (((0,), (0,)), ((), ())),
                                         preferred_element_type=f32).astype(o_ref.dtype)

        return pl.pallas_call(
            body, name=name, grid=(N // tn, K // tk),
            in_specs=[pl.BlockSpec((T, tk), lambda j, i: (0, i)), pl.BlockSpec((T, tn), lambda j, i: (0, j))],
            out_specs=pl.BlockSpec((tk, tn), lambda j, i: (i, j)),
            out_shape=jax.ShapeDtypeStruct((K, N), bf16), compiler_params=_cparams(2),
        )(a, b)

    Ns = n_slab
    g = N_DEV if Ns % 128 == 0 else 2
    tn = g * Ns

    def body(a_ref, b_ref, o_ref):
        val = lax.dot_general(a_ref[...].astype(bf16), b_ref[...].astype(bf16), (((0,), (0,)), ((), ())),
                              preferred_element_type=f32)
        for s in range(g):
            o_ref[s] = val[:, s * Ns:(s + 1) * Ns].astype(o_ref.dtype)

    return pl.pallas_call(
        body, name=name, grid=(K // tk, N_DEV // g),
        in_specs=[pl.BlockSpec((T, tk), lambda i, j: (0, i)), pl.BlockSpec((T, tn), lambda i, j: (0, j))],
        out_specs=pl.BlockSpec((g, tk, Ns), lambda i, j: (j, i, 0)),
        out_shape=jax.ShapeDtypeStruct((N_DEV, K, Ns), bf16), compiler_params=_cparams(2),
    )(a, b)


def _tile_spec(w, tm, cb):
    if callable(cb):
        return pl.BlockSpec((tm, w), lambda c, i: (i, cb(c)))
    return pl.BlockSpec((tm, w), lambda c, i: (i, cb))


def _param_spec(block, cb):
    nd = len(block)
    if cb is None:
        return pl.BlockSpec(block, lambda c, i: (0,) * nd)
    return pl.BlockSpec(block, lambda c, i: (0,) * (nd - 1) + (cb(c),))


ROW_TILE = 512


def rowwise(name, fn, tiled, params, outs, T, tm, ncol=1):
    tm = min(tm, T)
    n_in = len(tiled) + len(params)
    n_t = len(tiled)

    def body(*refs):
        ins = [r[...].astype(f32) for r in refs[:n_t]] + [r[...] for r in refs[n_t:n_in]]
        res = fn(*ins)
        for r, o in zip(refs[n_in:], res):
            r[...] = o.astype(r.dtype)

    return pl.pallas_call(
        body, name=name, grid=(ncol, T // tm),
        in_specs=[_tile_spec(w, tm, cb) for _, w, cb in tiled] + [_param_spec(blk, cb) for _, blk, cb in params],
        out_specs=[_tile_spec(w, tm, cb) for _, w, cb, _ in outs],
        out_shape=[jax.ShapeDtypeStruct((T, cols), dt) for cols, _, _, dt in outs],
        compiler_params=_cparams(2),
    )(*[a for a, _, _ in tiled], *[a for a, _, _ in params])


def rowwise_bwd(name, fn, tiled, params, cts, grads, T, tm, ncol=1, adds=None):
    tm = min(tm, T)
    n_t, n_p, n_c = len(tiled), len(params), len(cts)
    adds = adds or [None] * n_t
    add_list = [(k, a) for k, a in enumerate(adds) if a is not None]
    want = [k for k, g in enumerate(grads) if g is not None]
    n_a = len(add_list)

    def body(*refs):
        pos = 0
        t_refs = refs[pos:pos + n_t]; pos += n_t
        p_refs = refs[pos:pos + n_p]; pos += n_p
        c_refs = refs[pos:pos + n_c]; pos += n_c
        a_refs = refs[pos:pos + n_a]; pos += n_a
        g_refs = refs[pos:pos + len(want)]; pos += len(want)
        pg_refs = refs[pos:pos + n_p]
        primals = [r[...].astype(f32) for r in t_refs] + [r[...] for r in p_refs]
        _, vjp = jax.vjp(fn, *primals)
        g = vjp(tuple(r[...].astype(f32) for r in c_refs))
        add_of = {k: a_refs[n][...] for n, (k, _) in enumerate(add_list)}
        for n, k in enumerate(want):
            val = g[k]
            if k in add_of:
                val = val + add_of[k]
            g_refs[n][...] = val.astype(g_refs[n].dtype)
        i = pl.program_id(1)
        for k in range(n_p):
            @pl.when(i == 0)
            def _(k=k):
                pg_refs[k][...] = g[n_t + k]

            @pl.when(i > 0)
            def _(k=k):
                pg_refs[k][...] += g[n_t + k]

    in_specs = ([_tile_spec(w, tm, cb) for _, w, cb in tiled] + [_param_spec(blk, cb) for _, blk, cb in params]
                + [_tile_spec(w, tm, cb) for _, w, cb in cts] + [_tile_spec(w, tm, cb) for _, (_, w, cb) in add_list])
    ops = ([a for a, _, _ in tiled] + [a for a, _, _ in params] + [a for a, _, _ in cts]
           + [a for _, (a, _, _) in add_list])
    out_specs, out_shape = [], []
    for k in want:
        w = tiled[k][1]
        out_specs.append(_tile_spec(w, tm, lambda c: c))
        out_shape.append(jax.ShapeDtypeStruct((T, ncol * w), grads[k]))
    for a, blk, cb in params:
        out_specs.append(_param_spec(blk, cb))
        out_shape.append(jax.ShapeDtypeStruct(a.shape, f32))
    return pl.pallas_call(
        body, name=name, grid=(ncol, T // tm), in_specs=in_specs, out_specs=out_specs, out_shape=out_shape,
        compiler_params=_cparams(2),
    )(*ops)


@jax.custom_vjp
def _bdot(a, b):
    return jnp.dot(a.astype(bf16), b.astype(bf16), preferred_element_type=f32)


def _bdot_fwd(a, b):
    return _bdot(a, b), (a, b)


def _bdot_bwd(res, ct):
    a, b = res
    ctb = ct.astype(bf16)
    da = lax.dot_general(ctb, b.astype(bf16), (((1,), (1,)), ((), ())), preferred_element_type=f32)
    db = lax.dot_general(a.astype(bf16), ctb, (((0,), (0,)), ((), ())), preferred_element_type=f32)
    return da, db


_bdot.defvjp(_bdot_fwd, _bdot_bwd)


def _layer_norm(x, g, b):
    mu = jnp.mean(x, axis=-1, keepdims=True)
    xc = x - mu
    y = xc * lax.rsqrt(jnp.mean(xc * xc, axis=-1, keepdims=True) + EPS)
    return y * g + b


def f_rms(x, g):
    y = x * lax.rsqrt(jnp.mean(x * x, axis=-1, keepdims=True) + EPS)
    return (y * g,)


def f_sgu(zu, zv, ln_g, ln_b, wm, sgb_t):
    u = jax.nn.gelu(zu)
    vn = _layer_norm(jax.nn.gelu(zv), ln_g, ln_b)
    row = lax.broadcasted_iota(jnp.int32, (CHUNK, CHUNK), 0)
    col = lax.broadcasted_iota(jnp.int32, (CHUNK, CHUNK), 1)
    tril = col <= row
    low = col < HEAD_DIM
    parts = []
    for p in range(4):
        vp = vn[:, CHUNK * p:CHUNK * (p + 1)]
        w0 = jnp.where(tril, wm[2 * p], 0.0)
        w1 = jnp.where(tril, wm[2 * p + 1], 0.0)
        parts.append(_bdot(w0, jnp.where(low, vp, 0.0)) + _bdot(w1, jnp.where(low, 0.0, vp)))
    s = jnp.concatenate(parts, axis=1)
    lane_g = lax.shift_right_logical(lax.broadcasted_iota(jnp.int32, s.shape, 1), 6)
    bias = jnp.zeros_like(s)
    for g in range(8):
        bias = jnp.where(lane_g == g, sgb_t[:, g:g + 1], bias)
    return (u * (s + bias),)


def f_glu(p, gl):
    return (p * jax.nn.sigmoid(gl),)


def f_lnsilu(c1, g, b):
    return (jax.nn.silu(_layer_norm(c1, g, b)),)


def _head_norm(x, g64):
    g = jnp.concatenate([g64] * N_HEADS, axis=1)
    lane_h = lax.shift_right_logical(lax.broadcasted_iota(jnp.int32, x.shape, 1), 6)
    x2 = x * x
    r = jnp.zeros_like(x)
    for h in range(N_HEADS):
        mh = lane_h == h
        ms = jnp.sum(jnp.where(mh, x2, 0.0), axis=-1, keepdims=True) * (1.0 / HEAD_DIM)
        r = jnp.where(mh, lax.rsqrt(ms + EPS), r)
    return (x * r) * g


def f_qkv(zq, zk, zv, qg, kg):
    return (_head_norm(zq, qg) * 0.125, _head_norm(zk, kg), zv)


def f_merge(g0, g1, g2, ya, yb, yc, b0, b1, b2):
    return (jax.nn.sigmoid(g0 + b0) * ya + jax.nn.sigmoid(g1 + b1) * yb + jax.nn.sigmoid(g2 + b2) * yc,)


HALO = 32


def conv31_fwd(x, w, b, T, tt=256):
    C = x.shape[1]
    r = tt // HALO

    def body(x_ref, h_ref, w_ref, b_ref, y_ref, buf):
        i = pl.program_id(0)
        halo = h_ref[...]
        buf[0:HALO, :] = jnp.where(i > 0, halo, jnp.zeros_like(halo))
        buf[HALO:HALO + tt, :] = x_ref[...]
        acc = jnp.zeros((tt, C), f32) + b_ref[...]
        for k in range(CV_KERNEL):
            acc = acc + w_ref[k:k + 1, :] * buf[pl.ds(HALO - (CV_KERNEL - 1) + k, tt), :]
        y_ref[...] = acc

    return pl.pallas_call(
        body, name="conv31_fwd", grid=(T // tt,),
        in_specs=[pl.BlockSpec((tt, C), lambda i: (i, 0)),
                  pl.BlockSpec((HALO, C), lambda i: (jnp.maximum(i * r - 1, 0), 0)),
                  pl.BlockSpec((CV_KERNEL, C), lambda i: (0, 0)), pl.BlockSpec((1, C), lambda i: (0, 0))],
        out_specs=pl.BlockSpec((tt, C), lambda i: (i, 0)),
        out_shape=jax.ShapeDtypeStruct((T, C), f32),
        scratch_shapes=[pltpu.VMEM((HALO + tt, C), f32)], compiler_params=_cparams(1),
    )(x, x, w, b)


def conv31_bwd(x, w, dy, T, tt=256):
    C = x.shape[1]
    r = tt // HALO
    n = T // tt

    def body(x_ref, h_ref, w_ref, dy_ref, dyn_ref, dx_ref, dw_ref, db_ref, xbuf, dbuf):
        i = pl.program_id(0)
        halo = h_ref[...]
        xbuf[0:HALO, :] = jnp.where(i > 0, halo, jnp.zeros_like(halo))
        xbuf[HALO:HALO + tt, :] = x_ref[...]
        nxt = dyn_ref[...]
        dy = dy_ref[...]
        dbuf[0:tt, :] = dy
        dbuf[tt:tt + HALO, :] = jnp.where(i < n - 1, nxt, jnp.zeros_like(nxt))

        @pl.when(i == 0)
        def _():
            dw_ref[...] = jnp.zeros_like(dw_ref)
            db_ref[...] = jnp.zeros_like(db_ref)

        acc = jnp.zeros((tt, C), f32)
        for k in range(CV_KERNEL):
            acc = acc + w_ref[k:k + 1, :] * dbuf[pl.ds(CV_KERNEL - 1 - k, tt), :]
            xs = xbuf[pl.ds(HALO - (CV_KERNEL - 1) + k, tt), :]
            dw_ref[k:k + 1, :] += jnp.sum(dy * xs, axis=0, keepdims=True)
        dx_ref[...] = acc
        db_ref[...] += jnp.sum(dy, axis=0, keepdims=True)

    return pl.pallas_call(
        body, name="conv31_bwd", grid=(n,),
        in_specs=[pl.BlockSpec((tt, C), lambda i: (i, 0)),
                  pl.BlockSpec((HALO, C), lambda i: (jnp.maximum(i * r - 1, 0), 0)),
                  pl.BlockSpec((CV_KERNEL, C), lambda i: (0, 0)),
                  pl.BlockSpec((tt, C), lambda i: (i, 0)),
                  pl.BlockSpec((HALO, C), lambda i: (jnp.minimum((i + 1) * r, n * r - 1), 0))],
        out_specs=[pl.BlockSpec((tt, C), lambda i: (i, 0)), pl.BlockSpec((CV_KERNEL, C), lambda i: (0, 0)),
                   pl.BlockSpec((1, C), lambda i: (0, 0))],
        out_shape=[jax.ShapeDtypeStruct((T, C), f32), jax.ShapeDtypeStruct((CV_KERNEL, C), f32),
                   jax.ShapeDtypeStruct((1, C), f32)],
        scratch_shapes=[pltpu.VMEM((HALO + tt, C), f32), pltpu.VMEM((HALO + tt, C), f32)], compiler_params=_cparams(1),
    )(x, x, w, dy, dy)


FFN_TC = 128
FFN_PAD = 8


def ffn_act_fwd(up, cw, cb, T):
    F = up.shape[1] // 2
    nj = F // FFN_TC
    rc = min(128, T)

    def body(g_ref, v_ref, g0, g1, g2, gb, v0, v1, v2, vb, o_ref, gp, vp):
        zeros = jnp.zeros((FFN_PAD, FFN_TC), f32)
        for p, x_ref in ((gp, g_ref), (vp, v_ref)):
            p[0:FFN_PAD, :] = zeros
            p[FFN_PAD:FFN_PAD + T, :] = x_ref[...].astype(f32)
        wg = (g0[...], g1[...], g2[...], gb[...])
        wv = (v0[...], v1[...], v2[...], vb[...])

        def conv(p, w, r):
            return (w[0] * p[pl.ds(FFN_PAD + r - 2, rc), :] + w[1] * p[pl.ds(FFN_PAD + r - 1, rc), :]
                    + w[2] * p[pl.ds(FFN_PAD + r, rc), :] + w[3])

        for r in range(0, T, rc):
            o_ref[pl.ds(r, rc), :] = (jax.nn.silu(conv(gp, wg, r)) * conv(vp, wv, r)).astype(o_ref.dtype)

    gspec = pl.BlockSpec((T, FFN_TC), lambda j: (0, j))
    vspec = pl.BlockSpec((T, FFN_TC), lambda j: (0, j + nj))
    pg = pl.BlockSpec((1, FFN_TC), lambda j: (0, j))
    pv = pl.BlockSpec((1, FFN_TC), lambda j: (0, j + nj))
    return pl.pallas_call(
        body, name="ffn_act_fwd", grid=(nj,),
        in_specs=[gspec, vspec, pg, pg, pg, pg, pv, pv, pv, pv], out_specs=gspec,
        out_shape=jax.ShapeDtypeStruct((T, F), bf16),
        scratch_shapes=[pltpu.VMEM((FFN_PAD + T, FFN_TC), f32)] * 2, compiler_params=_cparams(1),
    )(up, up, cw[0], cw[1], cw[2], cb, cw[0], cw[1], cw[2], cb)


def ffn_act_bwd(up, cw, cb, dact, T):
    F = up.shape[1] // 2
    nj = F // FFN_TC

    rc = min(128, T)
    ext = rc + FFN_PAD

    def body(g_ref, v_ref, g0, g1, g2, gb, v0, v1, v2, vb, d_ref, dupg_ref, dupv_ref, *rest):
        p_refs, (gp, vp, dp, dgs, dvs) = rest[:8], rest[8:]
        zeros = jnp.zeros((FFN_PAD, FFN_TC), f32)
        for p, x_ref in ((gp, g_ref), (vp, v_ref)):
            p[0:FFN_PAD, :] = zeros
            p[FFN_PAD:FFN_PAD + T, :] = x_ref[...].astype(f32)
            p[FFN_PAD + T:FFN_PAD + T + FFN_PAD, :] = zeros
        dp[0:T, :] = d_ref[...]
        dp[T:T + FFN_PAD, :] = zeros
        wg = (g0[...], g1[...], g2[...], gb[...])
        wv = (v0[...], v1[...], v2[...], vb[...])
        acc = [jnp.zeros((1, FFN_TC), f32) for _ in range(8)]

        def taps(p, r):
            return tuple(p[pl.ds(FFN_PAD + r - s, ext), :] for s in (2, 1, 0))

        for r in range(0, T, rc):
            xg, xv = taps(gp, r), taps(vp, r)
            gc = wg[0] * xg[0] + wg[1] * xg[1] + wg[2] * xg[2] + wg[3]
            vc = wv[0] * xv[0] + wv[1] * xv[1] + wv[2] * xv[2] + wv[3]
            d = dp[pl.ds(r, ext), :]
            sg = jax.nn.sigmoid(gc)
            sides = ((d * vc * (sg * (1.0 + gc * (1.0 - sg))), xg, wg, dgs, dupg_ref), (d * (gc * sg), xv, wv, dvs, dupv_ref))
            for side, (dc, x, w, buf, dup_ref) in enumerate(sides):
                buf[...] = dc
                dc0 = dc[0:rc]
                dup = w[2] * dc0 + w[1] * buf[pl.ds(1, rc), :] + w[0] * buf[pl.ds(2, rc), :]
                dup_ref[pl.ds(r, rc), :] = dup.astype(dup_ref.dtype)
                for k in range(3):
                    acc[4 * side + k] = acc[4 * side + k] + jnp.sum(dc0 * x[k][0:rc], axis=0, keepdims=True)
                acc[4 * side + 3] = acc[4 * side + 3] + jnp.sum(dc0, axis=0, keepdims=True)
        for k in range(8):
            p_refs[k][...] = acc[k]

    gspec = pl.BlockSpec((T, FFN_TC), lambda j: (0, j))
    vspec = pl.BlockSpec((T, FFN_TC), lambda j: (0, j + nj))
    pg = pl.BlockSpec((1, FFN_TC), lambda j: (0, j))
    pv = pl.BlockSpec((1, FFN_TC), lambda j: (0, j + nj))
    res = pl.pallas_call(
        body, name="ffn_act_bwd", grid=(nj,),
        in_specs=[gspec, vspec, pg, pg, pg, pg, pv, pv, pv, pv, gspec],
        out_specs=[gspec, gspec] + [pg] * 8,
        out_shape=[jax.ShapeDtypeStruct((T, F), bf16)] * 2 + [jax.ShapeDtypeStruct((1, F), f32)] * 8,
        scratch_shapes=[pltpu.VMEM((FFN_PAD + T + FFN_PAD, FFN_TC), f32)] * 2 + [pltpu.VMEM((T + FFN_PAD, FFN_TC), f32)]
        + [pltpu.VMEM((ext, FFN_TC), f32)] * 2,
        compiler_params=_cparams(1),
    )(up, up, cw[0], cw[1], cw[2], cb, cw[0], cw[1], cw[2], cb, dact)
    dup = concat_cols("dup_concat", [res[0], res[1]], T)
    return (dup,) + tuple(jnp.concatenate([res[2 + k], res[6 + k]], axis=1) for k in range(4))


BQ = 256
BK = 256
assert BQ == BK
NT_DIMS = (((1,), (1,)), ((), ()))
TN_DIMS = (((0,), (0,)), ((), ()))


def _split_dot(x, u):
    x1 = x.astype(bf16)
    x2 = (x - x1.astype(f32)).astype(bf16)
    n = x.shape[0]
    y = jnp.dot(jnp.concatenate([x1, x2], axis=0), u, preferred_element_type=f32)
    return y[0:n] + y[n:2 * n]


def _log_sigmoids(z):
    sp = jnp.log(1.0 + jnp.exp(-jnp.abs(z)))
    lsp = jnp.minimum(z, 0.0) - sp
    return lsp, lsp - z


def _stack_heads(x):
    head1 = lax.broadcasted_iota(jnp.int32, x.shape, 1) >= HEAD_DIM
    zero = jnp.zeros_like(x)
    return jnp.concatenate([jnp.where(head1, zero, x), jnp.where(head1, x, zero)], axis=0)


def _unstack_heads(y):
    head1 = lax.broadcasted_iota(jnp.int32, (BQ, y.shape[1]), 1) >= HEAD_DIM
    return jnp.where(head1, y[BQ:2 * BQ], y[0:BQ])


def _attn_masks():
    row = lax.broadcasted_iota(jnp.int32, (2 * BQ, BK), 0)
    col = lax.broadcasted_iota(jnp.int32, (2 * BQ, BK), 1)
    ur = lax.broadcasted_iota(jnp.int32, (BK, BK), 0)
    uc = lax.broadcasted_iota(jnp.int32, (BK, BK), 1)
    return (row & (BQ - 1)) - col, (ur > uc).astype(bf16), (ur < uc).astype(bf16)


def attn_fwd(q, k, v, T, gather=None):
    nq = T // BQ
    n_g = 0 if gather is None else len(gather)

    def body(*refs):
        q_ref, k_ref, v_ref = refs[:3]
        o_ref, tot_ref = refs[3 + n_g:5 + n_g]
        p, i = pl.program_id(0), pl.program_id(1)
        if n_g:
            start, forward, finish = _two_level_gather(refs[5 + n_g:5 + 2 * n_g], *refs[5 + 2 * n_g:])
            pl.when(jnp.logical_and(p == 0, i == 0))(start)
        qs = _stack_heads(q_ref[...])
        diff, u_after, _ = _attn_masks()

        def step(jb, carry, diagonal):
            acc, c = carry
            ks = pl.multiple_of(jb * BK, BK)
            kb = k_ref[pl.ds(ks, BK), :]
            vb = v_ref[pl.ds(ks, BK), :]
            z = lax.dot_general(qs, kb, NT_DIMS, preferred_element_type=f32)
            lsp, lm = _log_sigmoids(z)
            if diagonal:
                m = diff > 0
                lm = jnp.where(m, lm, 0.0)
            a = jnp.exp(lsp + _split_dot(lm, u_after))
            if diagonal:
                a = jnp.where(m, a, 0.0)
            acc = acc + jnp.exp(c) * jnp.dot(a.astype(bf16), vb, preferred_element_type=f32)
            return acc, c + jnp.sum(lm, axis=-1, keepdims=True)

        carry = step(i, (jnp.zeros((2 * BQ, 128), f32), jnp.zeros((2 * BQ, 1), f32)), True)
        acc, c = lax.fori_loop(0, i, lambda t, cr: step(i - 1 - t, cr, False), carry)
        o_ref[...] = _unstack_heads(acc).astype(o_ref.dtype)
        tot_ref[...] = _unstack_heads(jnp.broadcast_to(c, (2 * BQ, 128)))
        if n_g:
            @pl.when(jnp.logical_and(p == 3, i == nq - 1))
            def _():
                forward()
                finish()

    blk = pl.BlockSpec((BQ, 128), lambda p, i: (i, p))
    full = pl.BlockSpec((T, 128), lambda p, i: (0, p))
    bufs = [] if gather is None else gather_buffers(gather)
    res = pl.pallas_call(
        body, name="attn_fwd_gather" if n_g else "attn_fwd", grid=(4, nq),
        in_specs=[blk, full, full] + [_ANY] * n_g, out_specs=[blk, blk] + [_ANY] * n_g,
        out_shape=[jax.ShapeDtypeStruct((T, 512), bf16), jax.ShapeDtypeStruct((T, 512), f32)]
        + [jax.ShapeDtypeStruct(b.shape, b.dtype) for b in bufs],
        input_output_aliases={3 + k: 2 + k for k in range(n_g)},
        scratch_shapes=_sems(n_g, N_DEV - 1) if n_g else [], compiler_params=_cparams(2),
    )(q, k, v, *bufs)
    return res[0], res[1], list(res[2:])


def attn_bwd(q, k, v, do, tot, T, scatter=None):
    nq = T // BQ
    n_s = 0 if scatter is None else len(scatter[0])
    modes = [] if scatter is None else list(scatter[1])

    def body(*refs):
        q_ref, k_ref, v_ref, do_ref, tot_ref = refs[:5]
        dq_ref, dk_ref, dv_ref = refs[5 + 2 * n_s:8 + 2 * n_s]
        p, i = pl.program_id(0), pl.program_id(1)
        if n_s:
            def copies():
                return _chip_copies(refs[5:5 + n_s], refs[8 + 2 * n_s:8 + 3 * n_s], modes, *refs[8 + 3 * n_s:])

            @pl.when(jnp.logical_and(p == 0, i == 0))
            def _():
                for cp in copies():
                    cp.start()

        @pl.when(i == 0)
        def _():
            dk_ref[...] = jnp.zeros_like(dk_ref)
            dv_ref[...] = jnp.zeros_like(dv_ref)

        qs = _stack_heads(q_ref[...])
        dos = _stack_heads(do_ref[...].astype(bf16))
        totv = tot_ref[...]
        tots = jnp.concatenate([totv[:, 0:1], totv[:, HEAD_DIM:HEAD_DIM + 1]], axis=0)
        diff, u_after, u_before = _attn_masks()

        def step(jb, carry, diagonal):
            dq, cl, cg = carry
            ks = pl.multiple_of(jb * BK, BK)
            kb = k_ref[pl.ds(ks, BK), :]
            vb = v_ref[pl.ds(ks, BK), :]
            z = lax.dot_general(qs, kb, NT_DIMS, preferred_element_type=f32)
            lsp, lsn = _log_sigmoids(z)
            lm = lsn
            if diagonal:
                m = diff > 0
                lm = jnp.where(m, lsn, 0.0)
            a = jnp.exp(lsp + _split_dot(lm, u_after))
            if diagonal:
                a = jnp.where(m, a, 0.0)
            g = a * lax.dot_general(dos, vb, NT_DIMS, preferred_element_type=f32)
            bb = jnp.exp(lsp)
            aa = g * jnp.exp(lsn) - _split_dot(g, u_before) * bb
            if diagonal:
                aa = jnp.where(m, aa, 0.0)
                bb = jnp.where(m, bb, 0.0)
            cl = cl + jnp.sum(lm, axis=-1, keepdims=True)
            f = jnp.exp(tots - cl)
            dz = (f * aa - cg * bb).astype(bf16)
            cg = cg + f * jnp.sum(g, axis=-1, keepdims=True)
            dq = dq + jnp.dot(dz, kb, preferred_element_type=f32)
            dk_ref[pl.ds(ks, BK), :] += lax.dot_general(dz, qs, TN_DIMS, preferred_element_type=f32)
            dv_ref[pl.ds(ks, BK), :] += lax.dot_general((f * a).astype(bf16), dos, TN_DIMS, preferred_element_type=f32)
            return dq, cl, cg

        zc = jnp.zeros((2 * BQ, 1), f32)
        carry = lax.fori_loop(0, i, lambda jb, cr: step(jb, cr, False), (jnp.zeros((2 * BQ, 128), f32), zc, zc))
        dq, _, _ = step(i, carry, True)
        dq_ref[...] = _unstack_heads(dq)
        if n_s:
            @pl.when(jnp.logical_and(p == 3, i == nq - 1))
            def _():
                for cp in copies():
                    cp.wait()

    blk = pl.BlockSpec((BQ, 128), lambda p, i: (i, p))
    full = pl.BlockSpec((T, 128), lambda p, i: (0, p))
    shp = jax.ShapeDtypeStruct((T, 512), f32)
    s_list = [] if scatter is None else list(scatter[0])
    bufs = chip_buffers(s_list, modes) if n_s else []
    res = pl.pallas_call(
        body, name="attn_bwd_scatter" if n_s else "attn_bwd", grid=(4, nq),
        in_specs=[blk, full, full, blk, blk] + [_ANY] * (2 * n_s), out_specs=[blk, full, full] + [_ANY] * n_s,
        out_shape=[shp, shp, shp] + [jax.ShapeDtypeStruct(b.shape, b.dtype) for b in bufs],
        input_output_aliases={5 + n_s + k: 3 + k for k in range(n_s)},
        scratch_shapes=_sems(n_s, 3) if n_s else [], compiler_params=_cparams(2),
    )(q, k, v, do, tot, *s_list, *bufs)
    return res[0], res[1], res[2], list(res[3:])


def loss_head(y, target, T, tm=256):
    D = y.shape[1]

    def body(y_ref, t_ref, dy_ref, l_ref):
        i = pl.program_id(0)
        err = y_ref[...] - t_ref[...]
        dy_ref[...] = err * (1.0 / D)
        part = 0.5 * jnp.sum(jnp.sum(err * err, axis=-1, keepdims=True) * (1.0 / D), axis=0, keepdims=True)

        @pl.when(i == 0)
        def _():
            l_ref[...] = jnp.zeros_like(l_ref)

        l_ref[...] += jnp.broadcast_to(part, l_ref.shape)

    spec = pl.BlockSpec((tm, D), lambda i: (i, 0))
    return pl.pallas_call(
        body, name="loss_head", grid=(T // tm,), in_specs=[spec, spec],
        out_specs=[spec, pl.BlockSpec((1, 128), lambda i: (0, 0))],
        out_shape=[jax.ShapeDtypeStruct((T, D), f32), jax.ShapeDtypeStruct((1, 128), f32)],
        compiler_params=_cparams(1),
    )(y, target)


def _row_tile(rows, cols, offset=0, max_elems=128 * 1024):
    best = None
    for t in range(16, rows + 1, 16):
        if rows % t == 0 and offset % t == 0 and t * cols <= max_elems:
            best = t
    return best if best is not None else rows


def adamw(name, w, m, v, parts, layer, prev=None, row_off=0):
    L, R, C = w.shape
    n_parts = parts.shape[0]
    tr = _row_tile(R, C, row_off, max_elems=256 * 1024)
    assert R % tr == 0 and row_off % tr == 0, (name, R, row_off, tr)

    def body(w_ref, m_ref, v_ref, p_ref, *rest):
        g_ref, d_ref, nm_ref, nv_ref = rest[-4:]
        g = p_ref[0].astype(f32)
        for s in range(1, n_parts):
            g = g + p_ref[s].astype(f32)
        wv = w_ref[...]
        mn = ADAM_B1 * m_ref[...] + (1.0 - ADAM_B1) * g
        vn = ADAM_B2 * v_ref[...] + (1.0 - ADAM_B2) * jnp.square(g)
        m_hat = mn / (1.0 - ADAM_B1 ** ADAM_STEP)
        v_hat = vn / (1.0 - ADAM_B2 ** ADAM_STEP)
        g_ref[...] = g
        d_ref[...] = -ADAM_LR * (m_hat / (jnp.sqrt(v_hat) + ADAM_EPS) + ADAM_WD * wv)
        nm_ref[...] = mn
        nv_ref[...] = vn

    spec = pl.BlockSpec((None, tr, C), lambda i: (layer, i, 0))
    shp = jax.ShapeDtypeStruct((L, R, C), f32)
    n_prev = 0 if prev is None else 4
    return pl.pallas_call(
        body, name=name, grid=(R // tr,),
        in_specs=[spec, spec, spec, pl.BlockSpec((n_parts, tr, C), lambda i: (0, row_off // tr + i, 0))] + [_ANY] * n_prev,
        out_specs=[spec] * 4, out_shape=[shp] * 4, input_output_aliases={4 + k: k for k in range(n_prev)},
        compiler_params=_cparams(1),
    )(w, m, v, parts, *(prev or ()))


REPL = ["ln1_g", "sg_ln_g", "sg_ln_b", "sg_w", "sg_b", "cv_b", "cv_ln_g", "cv_ln_b", "q_norm_g", "k_norm_g", "ln2_g",
        "ffn_conv_b"]
SMALL_SHARDED = ["b_gate", "cv_w", "ffn_conv_w"]
BIG = ["w_in", "w_a_out", "w_b_out", "w_c_out", "w_up", "w_out", "w_down"]
COL_SHARDED = ["w_a_out", "w_b_out", "w_c_out"]
TRANSPOSED = ["w_in", "w_up"]


def _pack(arrs, rows):
    flat = jnp.concatenate([a.reshape(-1) for a in arrs])
    return jnp.pad(flat, (0, rows * 128 - flat.shape[0])).reshape(rows, 128)


def _pack_layers(arrs, rows):
    flat = jnp.concatenate([a.reshape(a.shape[0], -1) for a in arrs], axis=1)
    return jnp.pad(flat, ((0, 0), (0, rows * 128 - flat.shape[1]))).reshape(flat.shape[0], rows, 128)


def _unpack_layers(packed, shapes):
    flat = packed.reshape(packed.shape[0], -1)
    out, pos = [], 0
    for s in shapes:
        n = 1
        for d in s[1:]:
            n *= d
        out.append(flat[:, pos:pos + n].reshape(s))
        pos += n
    return out


def _rows_for(shapes, mult):
    n = 0
    for s in shapes:
        k = 1
        for d in s:
            k *= d
        n += k
    rows = -(-n // 128)
    return -(-rows // mult) * mult


W_NAMES = ['ln1_g', 'w_in', 'b_gate', 'sg_ln_g', 'sg_ln_b', 'sg_w', 'sg_b', 'w_a_out', 'cv_w', 'cv_b', 'cv_ln_g', 'cv_ln_b',
           'w_b_out', 'q_norm_g', 'k_norm_g', 'w_c_out', 'w_out', 'ln2_g', 'w_up', 'ffn_conv_w', 'ffn_conv_b', 'w_down']


def _forward_layer(x, P, T, late=None):
    D = x.shape[1]
    sv = {"x0": x}
    (h1,) = rowwise("rms_fwd", f_rms, [(x, D, 0)], [(P["ln1_g"], (1, D), None)], [(D, D, 0, bf16)], T, ROW_TILE)
    z = mm_nt("in_proj", h1, P["w_in_t"], bf16, tko=1664, tc=D)
    sv["h1"], sv["z"] = h1, z
    (ya_in,) = rowwise("sgu_fwd", f_sgu, [(z, 512, 0), (z, 512, 1)],
                       [(P["sg_ln_g"], (1, 512), None), (P["sg_ln_b"], (1, 512), None),
                        (P["sg_w"], (8, CHUNK, CHUNK), None), (P["sg_b_t"], (CHUNK, 8), None)],
                       [(512, 512, 0, bf16)], T, CHUNK)
    (c0,) = rowwise("glu_fwd", f_glu, [(z, 512, 2), (z, 512, 3)], [], [(512, 512, 0, f32)], T, ROW_TILE)
    c1 = conv31_fwd(c0, P["cv_w"], P["cv_b"], T)
    (c3,) = rowwise("lnsilu_fwd", f_lnsilu, [(c1, 512, 0)], [(P["cv_ln_g"], (1, 512), None), (P["cv_ln_b"], (1, 512), None)],
                    [(512, 512, 0, bf16)], T, ROW_TILE)
    q8, kn, vb = rowwise("qkv_fwd", f_qkv, [(z, 512, 4), (z, 512, 5), (z, 512, 6)],
                         [(P["q_norm_g"], (1, HEAD_DIM), None), (P["k_norm_g"], (1, HEAD_DIM), None)],
                         [(512, 512, 0, bf16)] * 3, T, ROW_TILE)
    o, tot, _ = attn_fwd(q8, kn, vb, T)
    if late is not None:
        P.update(late(o))
    ya, yb, yc = mm_many("branch_out", [ya_in, c3, o], [P["w_a_out"], P["w_b_out"], P["w_c_out"]], f32)
    (merged,) = rowwise("merge_fwd", f_merge,
                        [(z, 512, lambda c: 7 + c), (z, 512, lambda c: 9 + c), (z, 512, lambda c: 11 + c),
                         (ya, 512, lambda c: c), (yb, 512, lambda c: c), (yc, 512, lambda c: c)],
                        [(P["b_gate"][k], (1, 512), lambda c: c) for k in range(3)],
                        [(D, 512, lambda c: c, bf16)], T, 256, ncol=2)
    x1 = mm_nn("out_proj", merged, P["w_out"], f32, add=x)
    sv.update(ya_in=ya_in, ya=ya, c0=c0, c1=c1, c3=c3, yb=yb, q8=q8, kn=kn, vb=vb, o=o, tot=tot, yc=yc, merged=merged, x1=x1)
    (h2,) = rowwise("rms_fwd", f_rms, [(x1, D, 0)], [(P["ln2_g"], (1, D), None)], [(D, D, 0, bf16)], T, ROW_TILE)
    up = mm_nt("up_proj", h2, P["w_up_t"], bf16, tko=1408, tc=D)
    act = ffn_act_fwd(up, P["ffn_conv_w"], P["ffn_conv_b"], T)
    x2 = mm_nn("down_proj", act, P["w_down"], f32, add=x1)
    sv.update(h2=h2, up=up, act=act)
    return x2, sv


def _backward_layer(dx2, P, sv, T, scatter=None):
    D = dx2.shape[1]
    G = {}
    G["w_down"] = mm_tn("dw_down", sv["act"], dx2, tk=1408)
    dact = mm_nt("d_act", dx2, P["w_down"], f32, tko=1408, tc=D)
    dup, dcw0, dcw1, dcw2, G["ffn_conv_b"] = ffn_act_bwd(sv["up"], P["ffn_conv_w"], P["ffn_conv_b"], dact, T)
    G["ffn_conv_w"] = jnp.concatenate([dcw0, dcw1, dcw2], axis=0)
    G["w_up"] = mm_tn("dw_up", dup, sv["h2"], tk=1408)
    dh2 = mm_nn("d_h2", dup, P["w_up_t"], f32, tm=512)
    dx1, G["ln2_g"] = rowwise_bwd("rms_bwd", f_rms, [(sv["x1"], D, 0)], [(P["ln2_g"], (1, D), None)], [(dh2, D, 0)],
                                  [f32], T, ROW_TILE, adds=[(dx2, D, 0)])
    G["w_out"] = mm_tn("dw_out", sv["merged"], dx1, tk=1024)
    dmerged = mm_nt("d_merged", dx1, P["w_out"], f32)
    z = sv["z"]
    dg0, dg1, dg2, dya, dyb, dyc, db0, db1, db2 = rowwise_bwd(
        "merge_bwd", f_merge,
        [(z, 512, lambda c: 7 + c), (z, 512, lambda c: 9 + c), (z, 512, lambda c: 11 + c),
         (sv["ya"], 512, lambda c: c), (sv["yb"], 512, lambda c: c), (sv["yc"], 512, lambda c: c)],
        [(P["b_gate"][k], (1, 512), lambda c: c) for k in range(3)],
        [(dmerged, 512, lambda c: c)], [bf16] * 6, T, 256, ncol=2)
    G["b_gate"] = jnp.concatenate([db0, db1, db2], axis=0)
    dya_in, dc3, do = mm_many("d_branch_in", [dya, dyb, dyc], [P["w_a_out"], P["w_b_out"], P["w_c_out"]], f32,
                              transpose_w=True)
    G["w_c_out"] = mm_tn("dw_c_out", sv["o"], dyc, n_slab=CHUNK)
    dq8, dkn, dvb, received = attn_bwd(sv["q8"], sv["kn"], sv["vb"], do, sv["tot"], T, scatter=scatter)
    dzq, dzk, dzv, G["q_norm_g"], G["k_norm_g"] = rowwise_bwd(
        "qkv_bwd", f_qkv, [(z, 512, 4), (z, 512, 5), (z, 512, 6)],
        [(P["q_norm_g"], (1, HEAD_DIM), None), (P["k_norm_g"], (1, HEAD_DIM), None)],
        [(dq8, 512, 0), (dkn, 512, 0), (dvb, 512, 0)], [bf16] * 3, T, ROW_TILE)
    G["w_b_out"] = mm_tn("dw_b_out", sv["c3"], dyb, n_slab=CHUNK)
    dc1, G["cv_ln_g"], G["cv_ln_b"] = rowwise_bwd(
        "lnsilu_bwd", f_lnsilu, [(sv["c1"], 512, 0)], [(P["cv_ln_g"], (1, 512), None), (P["cv_ln_b"], (1, 512), None)],
        [(dc3, 512, 0)], [f32], T, ROW_TILE)
    dc0, G["cv_w"], G["cv_b"] = conv31_bwd(sv["c0"], P["cv_w"], dc1, T)
    dzp, dzgl = rowwise_bwd("glu_bwd", f_glu, [(z, 512, 2), (z, 512, 3)], [], [(dc0, 512, 0)], [bf16] * 2, T, ROW_TILE)
    G["w_a_out"] = mm_tn("dw_a_out", sv["ya_in"], dya, n_slab=CHUNK)
    dzu, dzv_a, G["sg_ln_g"], G["sg_ln_b"], G["sg_w"], dsbt = rowwise_bwd(
        "sgu_bwd", f_sgu, [(z, 512, 0), (z, 512, 1)],
        [(P["sg_ln_g"], (1, 512), None), (P["sg_ln_b"], (1, 512), None), (P["sg_w"], (8, CHUNK, CHUNK), None),
         (P["sg_b_t"], (CHUNK, 8), None)],
        [(dya_in, 512, 0)], [bf16] * 2, T, CHUNK)
    G["sg_b"] = dsbt.T
    dz = concat_cols("dz_concat", [dzu, dzv_a, dzp, dzgl, dzq, dzk, dzv, dg0, dg1, dg2], T)
    G["w_in"] = mm_tn("dw_in", dz, sv["h1"], tk=1664)
    dh1 = mm_nn("d_h1", dz, P["w_in_t"], f32, tm=512)
    dx0, G["ln1_g"] = rowwise_bwd("rms_bwd", f_rms, [(sv["x0"], D, 0)], [(P["ln1_g"], (1, D), None)], [(dh1, D, 0)],
                                  [f32], T, ROW_TILE, adds=[(dx1, D, 0)])
    return dx0, G, received


def kernel(x, ln1_g, w_in, b_gate, sg_ln_g, sg_ln_b, sg_w, sg_b, w_a_out, cv_w, cv_b, cv_ln_g, cv_ln_b, w_b_out, q_norm_g, k_norm_g, w_c_out, w_out, ln2_g, w_up, ffn_conv_w, ffn_conv_b, w_down, loss_target, m_ln1_g, m_w_in, m_b_gate, m_sg_ln_g, m_sg_ln_b, m_sg_w, m_sg_b, m_w_a_out, m_cv_w, m_cv_b, m_cv_ln_g, m_cv_ln_b, m_w_b_out, m_q_norm_g, m_k_norm_g, m_w_c_out, m_w_out, m_ln2_g, m_w_up, m_ffn_conv_w, m_ffn_conv_b, m_w_down, v_ln1_g, v_w_in, v_b_gate, v_sg_ln_g, v_sg_ln_b, v_sg_w, v_sg_b, v_w_a_out, v_cv_w, v_cv_b, v_cv_ln_g, v_cv_ln_b, v_w_b_out, v_q_norm_g, v_k_norm_g, v_w_c_out, v_w_out, v_ln2_g, v_w_up, v_ffn_conv_w, v_ffn_conv_b, v_w_down):
    W = dict(ln1_g=ln1_g, w_in=w_in, b_gate=b_gate, sg_ln_g=sg_ln_g, sg_ln_b=sg_ln_b, sg_w=sg_w, sg_b=sg_b, w_a_out=w_a_out,
             cv_w=cv_w, cv_b=cv_b, cv_ln_g=cv_ln_g, cv_ln_b=cv_ln_b, w_b_out=w_b_out, q_norm_g=q_norm_g, k_norm_g=k_norm_g,
             w_c_out=w_c_out, w_out=w_out, ln2_g=ln2_g, w_up=w_up, ffn_conv_w=ffn_conv_w, ffn_conv_b=ffn_conv_b, w_down=w_down)
    M = dict(ln1_g=m_ln1_g, w_in=m_w_in, b_gate=m_b_gate, sg_ln_g=m_sg_ln_g, sg_ln_b=m_sg_ln_b, sg_w=m_sg_w, sg_b=m_sg_b,
             w_a_out=m_w_a_out, cv_w=m_cv_w, cv_b=m_cv_b, cv_ln_g=m_cv_ln_g, cv_ln_b=m_cv_ln_b, w_b_out=m_w_b_out,
             q_norm_g=m_q_norm_g, k_norm_g=m_k_norm_g, w_c_out=m_w_c_out, w_out=m_w_out, ln2_g=m_ln2_g, w_up=m_w_up,
             ffn_conv_w=m_ffn_conv_w, ffn_conv_b=m_ffn_conv_b, w_down=m_w_down)
    V = dict(ln1_g=v_ln1_g, w_in=v_w_in, b_gate=v_b_gate, sg_ln_g=v_sg_ln_g, sg_ln_b=v_sg_ln_b, sg_w=v_sg_w, sg_b=v_sg_b,
             w_a_out=v_w_a_out, cv_w=v_cv_w, cv_b=v_cv_b, cv_ln_g=v_cv_ln_g, cv_ln_b=v_cv_ln_b, w_b_out=v_w_b_out,
             q_norm_g=v_q_norm_g, k_norm_g=v_k_norm_g, w_c_out=v_w_c_out, w_out=v_w_out, ln2_g=v_ln2_g, w_up=v_w_up,
             ffn_conv_w=v_ffn_conv_w, ffn_conv_b=v_ffn_conv_b, w_down=v_w_down)
    T, D = x.shape[1], x.shape[2]
    L = DEPTH
    xs = x.reshape(T, D)
    target = loss_target.reshape(T, D)

    ss_shapes = [W[n].shape for n in SMALL_SHARDED]
    ss_rows = _rows_for(ss_shapes, 8)
    (ss_all,), small_token = exchange("gather_small", [_pack([W[n] for n in SMALL_SHARDED], ss_rows)], ["gather"])
    full_small = {}
    pos = 0
    for n in SMALL_SHARDED:
        s = W[n].shape
        cnt = s[0] * s[1] * s[2]
        part = ss_all.reshape(N_DEV, -1)[:, pos:pos + cnt].reshape((N_DEV,) + s)
        full_small[n] = jnp.transpose(part, (1, 2, 0, 3)).reshape(s[0], s[1], N_DEV * s[2])
        pos += cnt

    params, saved = [], []
    cur = xs
    Wt = {n: (jnp.transpose(W[n], (0, 2, 1)) if n in TRANSPOSED else W[n]) for n in BIG}

    def local_slabs(l):
        return [Wt[n][l].astype(bf16) for n in BIG]

    def weights_of(names, slabs):
        out = {}
        for n, s in zip(names, slabs):
            if n in TRANSPOSED:
                out[n + "_t"] = s.reshape(-1, s.shape[-1])
            else:
                out[n] = assemble("assemble_" + n, s) if n in COL_SHARDED else s.reshape(-1, s.shape[-1])
        return out

    LATE = [n for n in BIG if n != "w_in"]
    def behind(token, blocks):
        return [blocks[0] + token[0, 0].astype(blocks[0].dtype)] + list(blocks[1:])

    first, token = gather2("gather_w_in", behind(small_token, [local_slabs(0)[0]]))
    early_part, late_part = {}, {}
    for l in range(L):
        if l > 0:
            early_part[l] = gather_start("gather_start_in_%d" % l, behind(token, local_slabs(l)[:1]))
            token = early_part[l][3]
        late_part[l] = gather_start("gather_start_rest_%d" % l, behind(token, local_slabs(l)[1:]))
        token = late_part[l][3]
    start_token = token[0, 0]

    def late_weights(l):
        send_sems, recv_sems, bufs, _ = late_part[l]
        return lambda o: weights_of(LATE, gather_wait("gather_wait_rest_%d" % l, send_sems, recv_sems, bufs, o))

    for l in range(L):
        if l == 0:
            P = weights_of(["w_in"], first)
        else:
            send_sems, recv_sems, bufs, _ = early_part[l]
            P = weights_of(["w_in"], gather_wait("gather_wait_in_%d" % l, send_sems, recv_sems, bufs, cur))
        late = late_weights(l)
        for n in REPL:
            P[n] = W[n][l]
        for n in ("ln1_g", "sg_ln_g", "sg_ln_b", "cv_b", "cv_ln_g", "cv_ln_b", "q_norm_g", "k_norm_g", "ln2_g", "ffn_conv_b"):
            P[n] = P[n].reshape(1, -1)
        P["sg_b_t"] = P["sg_b"].T
        P["cv_w"] = full_small["cv_w"][l]
        P["b_gate"] = [full_small["b_gate"][l][k:k + 1] for k in range(3)]
        P["ffn_conv_w"] = [full_small["ffn_conv_w"][l][k:k + 1] for k in range(3)]
        if l == 0:
            P["ln1_g"] = P["ln1_g"] + start_token
        cur, sv = _forward_layer(cur, P, T, late=late)
        params.append(P)
        saved.append(sv)

    dy, loss_row = loss_head(cur, target, T)
    loss = lax.psum(loss_row[0, 0], ("x", "y", "c"))

    repl_shapes = [W[n].shape for n in REPL]
    repl_rows = _rows_for([s[1:] for s in repl_shapes], 16)
    ssl_rows = _rows_for([s[1:] for s in ss_shapes], 16)
    state = {"repl": [_pack_layers([X[n] for n in REPL], repl_rows) for X in (W, M, V)],
             "ss": [_pack_layers([X[n] for n in SMALL_SHARDED], ssl_rows) for X in (W, M, V)]}
    WIDE = [n for n in BIG if n not in COL_SHARDED]
    big_rows, big_off = {}, {}
    for n in WIDE:
        big_off[n] = sum(big_rows.values())
        big_rows[n] = W[n].shape[1] * W[n].shape[2] // D
        state[n] = [(jnp.transpose(X[n], (0, 2, 1)) if n in TRANSPOSED else X[n]).reshape(L, big_rows[n], D)
                    for X in (W, M, V)]
    for n in COL_SHARDED:
        state[n] = [W[n], M[n], V[n]]
    done = {n: None for n in state}

    def update(layer, received):
        for n in WIDE:
            done[n] = adamw("adamw_" + n, *state[n], received[0], layer, done[n], row_off=big_off[n])
        done["repl"] = adamw("adamw_repl", *state["repl"], received[1], layer, done["repl"])
        done["ss"] = adamw("adamw_ss", *state["ss"], received[2], layer, done["ss"])
        for k, n in enumerate(COL_SHARDED):
            done[n] = adamw("adamw_" + n, *state[n], received[3], layer, done[n], row_off=k * W[n].shape[1])

    def reduce_in_chip(G):
        big = [G[n].reshape(4, 2, big_rows[n], D) for n in WIDE]
        narrow = jnp.concatenate([G[n].reshape((4, 2) + G[n].shape[1:]) for n in COL_SHARDED], axis=2)
        ss_parts = []
        for n in SMALL_SHARDED:
            k, c = W[n].shape[1:]
            ss_parts.append(jnp.transpose(G[n].reshape(k, N_DEV, c), (1, 0, 2)).reshape(N_DEV, k * c))
        ss_send = jnp.concatenate(ss_parts, axis=1)
        ss_send = jnp.pad(ss_send, ((0, 0), (0, ssl_rows * 128 - ss_send.shape[1]))).reshape(4, 2, ssl_rows, 128)
        repl = _pack([G[n] for n in REPL], repl_rows)
        core = lax.axis_index("c")
        kept = jnp.concatenate([lax.dynamic_index_in_dim(b, core, 1, keepdims=False) for b in big], axis=1)
        ss_kept = lax.dynamic_index_in_dim(ss_send, core, 1, keepdims=False)
        narrow_kept = lax.dynamic_index_in_dim(narrow, core, 1, keepdims=False)
        got, repl_got, ss_got, narrow_got = pair_exchange("pair_exchange", big, [repl, ss_send, narrow],
                                                          ["gather", "scatter", "scatter"])
        return [add2("pair_add_big", kept, got), add2("pair_add_repl", repl, repl_got), add2("pair_add_ss", ss_kept, ss_got),
                add2("pair_add_narrow", narrow_kept, narrow_got)]

    chip_modes = ["scatter", "gather", "scatter", "scatter"]
    dcur = dy
    pending = None
    for l in reversed(range(L)):
        P = params[l]
        if pending is not None:
            P = dict(P, ffn_conv_b=P["ffn_conv_b"] + pending[4][0, 0])
        dcur, G, _ = _backward_layer(dcur, P, saved[l], T)
        started = chip_start("chip_start_%d" % l, reduce_in_chip(G), chip_modes)
        if pending is not None:
            after = started[4] + dcur[0:8, 0:128]
            update(l + 1, chip_wait("chip_wait_%d" % (l + 1), *pending[:4], chip_modes, after))
        pending = started
    after = sum(done[n][0][1, 0:1, 0:1] for n in state)
    update(0, chip_wait("chip_wait_0", *pending[:4], chip_modes, after))

    results = [{}, {}, {}, {}]
    for k in range(4):
        for n, a in zip(REPL, _unpack_layers(done["repl"][k], repl_shapes)):
            results[k][n] = a
        for n, a in zip(SMALL_SHARDED, _unpack_layers(done["ss"][k], ss_shapes)):
            results[k][n] = a
        for n in BIG:
            results[k][n] = jnp.transpose(done[n][k], (0, 2, 1)) if n in TRANSPOSED else done[n][k].reshape(W[n].shape)
    out = [loss, dcur.reshape(1, T, D)]
    for k in range(4):
        out += [results[k][n] for n in W_NAMES]
    return tuple(out)
```

```python
import functools

import jax
import jax.numpy as jnp
from jax import lax
from jax.experimental import pallas as pl
from jax.experimental.pallas import tpu as pltpu

f32 = jnp.float32
bf16 = jnp.bfloat16

EPS = 1e-6
N_DEV = 8
DEPTH = 4
CHUNK = 128
HEAD_DIM = 64
N_HEADS = 8
CV_KERNEL = 31
VMEM_LIMIT_BYTES = 56 * 2 ** 20

ADAM_LR = 0.001
ADAM_B1 = 0.9
ADAM_B2 = 0.999
ADAM_EPS = 1e-08
ADAM_WD = 0.01
ADAM_STEP = 10

MESH = pl.DeviceIdType.MESH


def _cparams(n_grid):
    return pltpu.CompilerParams(dimension_semantics=("arbitrary",) * n_grid, vmem_limit_bytes=VMEM_LIMIT_BYTES)


def exchange(name, arrays, modes):
    n = len(arrays)

    def body(*refs):
        copies = _direct_copies(refs[:n], refs[n:2 * n], modes, *refs[2 * n + 1:])
        for cp in copies:
            cp.start()
        for cp in copies:
            cp.wait()
        refs[2 * n][...] = jnp.zeros_like(refs[2 * n])

    res = pl.pallas_call(
        body, name=name, out_shape=_exchange_out_shapes(arrays, modes) + [jax.ShapeDtypeStruct((8, 128), f32)],
        in_specs=[_ANY] * n, out_specs=[_ANY] * n + [pl.BlockSpec(memory_space=pltpu.VMEM)],
        scratch_shapes=_exchange_sems(n),
    )(*arrays)
    return list(res[:n]), res[n]


_ANY = pl.BlockSpec(memory_space=pl.ANY)


def _exchange_out_shapes(arrays, modes):
    return [jax.ShapeDtypeStruct((N_DEV,) + tuple(a.shape) if m == "gather" else tuple(a.shape), a.dtype)
            for a, m in zip(arrays, modes)]


def _exchange_sems(n):
    return [pltpu.SemaphoreType.DMA((n, N_DEV - 1)), pltpu.SemaphoreType.DMA((n, N_DEV - 1)), pltpu.SemaphoreType.DMA((n,))]


def _direct_copies(ins, outs, modes, send_sems, recv_sems, local_sems):
    x, y, c = lax.axis_index("x"), lax.axis_index("y"), lax.axis_index("c")
    me = 4 * x + 2 * y + c
    copies = []
    for k in range(len(ins)):
        src_mine = ins[k] if modes[k] == "gather" else ins[k].at[me]
        copies.append(pltpu.make_async_copy(src_mine, outs[k].at[me], local_sems.at[k]))
        for r in range(1, N_DEV):
            px = 1 - x if r & 4 else x
            py = 1 - y if r & 2 else y
            pc = 1 - c if r & 1 else c
            src = ins[k] if modes[k] == "gather" else ins[k].at[4 * px + 2 * py + pc]
            copies.append(pltpu.make_async_remote_copy(
                src_ref=src, dst_ref=outs[k].at[me], send_sem=send_sems.at[k, r - 1], recv_sem=recv_sems.at[k, r - 1],
                device_id=(px, py, pc), device_id_type=MESH))
    return copies


def own_slot_buffer(block, n_slots, index):
    return lax.dynamic_update_slice(lax.empty((n_slots,) + tuple(block.shape), block.dtype), block[None],
                                    (index,) + (0,) * block.ndim)


def _sems(n, m):
    return [pltpu.SemaphoreType.DMA((n, m)), pltpu.SemaphoreType.DMA((n, m))]


def _two_level_gather(bufs, send_sems, recv_sems):
    x, y, c = lax.axis_index("x"), lax.axis_index("y"), lax.axis_index("c")
    me = 4 * x + 2 * y + c
    sibling = (x, y, 1 - c)
    chips = [(1 - x, y), (x, 1 - y), (1 - x, 1 - y)]

    def slot(px, py, pc):
        return 4 * px + 2 * py + pc

    def copy(k, sem, block, to):
        rows = bufs[k].at[block]
        return pltpu.make_async_remote_copy(src_ref=rows, dst_ref=rows, send_sem=send_sems.at[k, sem],
                                            recv_sem=recv_sems.at[k, sem], device_id=to, device_id_type=MESH)

    def first(k):
        return [copy(k, 0, me, sibling)] + [copy(k, 1 + j, me, (*chip, c)) for j, chip in enumerate(chips)]

    def passed_on(k, j):
        return copy(k, 4 + j, slot(*chips[j], c), sibling)

    def start():
        for k in range(len(bufs)):
            for cp in first(k):
                cp.start()

    def forward():
        for k in range(len(bufs)):
            for j in range(3):
                copy(k, 1 + j, slot(*chips[j], c), sibling).wait_recv()
                passed_on(k, j).start()

    def finish():
        for k in range(len(bufs)):
            copy(k, 0, slot(x, y, 1 - c), sibling).wait_recv()
            for j, chip in enumerate(chips):
                copy(k, 4 + j, slot(*chip, 1 - c), sibling).wait_recv()
            for cp in first(k) + [passed_on(k, j) for j in range(3)]:
                cp.wait_send()

    return start, forward, finish


def pair_exchange(name, big, smalls, modes):
    nb, ns = len(big), len(smalls)
    rows = [a.shape[2] for a in big]
    total, width = sum(rows), big[0].shape[3]
    n = nb + ns

    def body(*refs):
        ins, outs = refs[:n], refs[n:n + 1 + ns]
        send_sems, recv_sems = refs[-2:]
        x, y, c = lax.axis_index("x"), lax.axis_index("y"), lax.axis_index("c")
        copies = []

        def remote(k, src, dst):
            copies.append(pltpu.make_async_remote_copy(src_ref=src, dst_ref=dst, send_sem=send_sems.at[k, 0],
                                                       recv_sem=recv_sems.at[k, 0], device_id=(x, y, 1 - c),
                                                       device_id_type=MESH))

        off = 0
        for k in range(nb):
            remote(k, ins[k].at[:, 1 - c], outs[0].at[:, pl.ds(off, rows[k]), :])
            off += rows[k]
        for j in range(ns):
            remote(nb + j, ins[nb + j].at[:, 1 - c] if modes[j] == "scatter" else ins[nb + j], outs[1 + j])
        for cp in copies:
            cp.start()
        for cp in copies:
            cp.wait()

    out_shape = [jax.ShapeDtypeStruct((4, total, width), big[0].dtype)]
    for a, m in zip(smalls, modes):
        out_shape.append(jax.ShapeDtypeStruct((4,) + tuple(a.shape[2:]) if m == "scatter" else tuple(a.shape), a.dtype))
    return pl.pallas_call(
        body, name=name, out_shape=out_shape, in_specs=[_ANY] * n, out_specs=[_ANY] * len(out_shape),
        scratch_shapes=_sems(n, 1),
    )(*big, *smalls)


def add2(name, a, b):
    shape = a.shape
    a2, b2 = a.reshape(-1, shape[-1]), b.reshape(-1, shape[-1])
    R, C = a2.shape
    tr = _row_tile(R, C, max_elems=1200 * 1024)

    def body(a_ref, b_ref, o_ref):
        o_ref[...] = (a_ref[...].astype(f32) + b_ref[...].astype(f32)).astype(o_ref.dtype)

    spec = pl.BlockSpec((tr, C), lambda i: (i, 0))
    return pl.pallas_call(
        body, name=name, grid=(R // tr,), in_specs=[spec, spec], out_specs=spec,
        out_shape=jax.ShapeDtypeStruct((R, C), a.dtype), compiler_params=_cparams(1),
    )(a2, b2).reshape(shape)


def pair_add_wide(name, big, got):
    nb = len(big)
    rows = [a.shape[2] for a in big]
    total, width = sum(rows), big[0].shape[3]
    core = lax.axis_index("c").astype(jnp.int32).reshape(1)

    def body(core_ref, *refs):
        got_ref, o_ref = refs[nb], refs[nb + 1]
        off = 0
        for k in range(nb):
            part = slice(off, off + rows[k])
            o_ref[part, :] = (refs[k][...].astype(f32) + got_ref[part, :].astype(f32)).astype(o_ref.dtype)
            off += rows[k]

    whole = pl.BlockSpec((None, total, width), lambda i, core_ref: (i, 0, 0))
    grid_spec = pltpu.PrefetchScalarGridSpec(
        num_scalar_prefetch=1, grid=(4,),
        in_specs=[pl.BlockSpec((None, None, r, width), lambda i, core_ref: (i, core_ref[0], 0, 0)) for r in rows] + [whole],
        out_specs=whole)
    return pl.pallas_call(
        body, name=name, grid_spec=grid_spec, out_shape=jax.ShapeDtypeStruct((4, total, width), got.dtype),
        compiler_params=_cparams(1),
    )(core, *big, got)


def chip_buffers(arrays, modes):
    mine = 2 * lax.axis_index("x") + lax.axis_index("y")
    return [own_slot_buffer(lax.dynamic_index_in_dim(a, mine, 0, keepdims=False) if m == "scatter" else a, 4, mine)
            for a, m in zip(arrays, modes)]


def _chip_copies(ins, bufs, modes, send_sems, recv_sems):
    x, y, c = lax.axis_index("x"), lax.axis_index("y"), lax.axis_index("c")
    mine = 2 * x + y
    copies = []
    for k in range(len(ins)):
        for r in range(1, 4):
            px = 1 - x if r & 2 else x
            py = 1 - y if r & 1 else y
            src = ins[k].at[2 * px + py] if modes[k] == "scatter" else ins[k]
            copies.append(pltpu.make_async_remote_copy(
                src_ref=src, dst_ref=bufs[k].at[mine], send_sem=send_sems.at[k, r - 1], recv_sem=recv_sems.at[k, r - 1],
                device_id=(px, py, c), device_id_type=MESH))
    return copies


def chip_exchange(name, arrays, modes):
    n = len(arrays)

    def body(*refs):
        copies = _chip_copies(refs[:n], refs[2 * n:3 * n], modes, *refs[3 * n:])
        for cp in copies:
            cp.start()
        for cp in copies:
            cp.wait()

    bufs = chip_buffers(arrays, modes)
    return pl.pallas_call(
        body, name=name, out_shape=[jax.ShapeDtypeStruct(b.shape, b.dtype) for b in bufs],
        in_specs=[_ANY] * (2 * n), out_specs=[_ANY] * n, input_output_aliases={n + k: k for k in range(n)},
        scratch_shapes=_sems(n, 3),
    )(*arrays, *bufs)


def gather_buffers(blocks):
    me = 4 * lax.axis_index("x") + 2 * lax.axis_index("y") + lax.axis_index("c")
    return [own_slot_buffer(b, N_DEV, me) for b in blocks]


_HBM = pl.BlockSpec(memory_space=pltpu.HBM)
_SEM = pl.BlockSpec(memory_space=pltpu.SEMAPHORE)
_SPLIT_PARAMS = pltpu.CompilerParams(has_side_effects=pltpu.SideEffectType.DATAFLOW_SIDE_EFFECTING)


def _split_gather_copies(bufs, send_sems, recv_sems):
    x, y, c = lax.axis_index("x"), lax.axis_index("y"), lax.axis_index("c")
    me = 4 * x + 2 * y + c
    copies = []
    for k in range(len(bufs)):
        rows = bufs[k].at[me]
        for r in range(1, N_DEV):
            peer = (1 - x if r & 4 else x, 1 - y if r & 2 else y, 1 - c if r & 1 else c)
            copies.append(pltpu.make_async_remote_copy(
                src_ref=rows, dst_ref=rows, send_sem=send_sems.at[(N_DEV - 1) * k + r - 1],
                recv_sem=recv_sems.at[(N_DEV - 1) * k + r - 1], device_id=peer, device_id_type=MESH))
    return copies


def gather_start(name, blocks):
    n = len(blocks)
    bufs = gather_buffers(blocks)

    def body(*refs):
        for cp in _split_gather_copies(refs[n + 2:2 * n + 2], refs[n], refs[n + 1]):
            cp.start()
        refs[2 * n + 2][...] = jnp.zeros_like(refs[2 * n + 2])

    n_sem = n * (N_DEV - 1)
    res = pl.pallas_call(
        body, name=name,
        out_shape=(pltpu.SemaphoreType.DMA((n_sem,)), pltpu.SemaphoreType.DMA((n_sem,)),
                   *[pltpu.HBM(b.shape, b.dtype) for b in bufs], jax.ShapeDtypeStruct((8, 128), f32)),
        in_specs=[_HBM] * n, out_specs=(_SEM, _SEM, *[_HBM] * n, pl.BlockSpec(memory_space=pltpu.VMEM)),
        input_output_aliases={k: 2 + k for k in range(n)}, compiler_params=_SPLIT_PARAMS,
    )(*[pltpu.with_memory_space_constraint(b, pltpu.HBM) for b in bufs])
    return res[0], res[1], list(res[2:2 + n]), res[2 + n]


def gather_wait(name, send_sems, recv_sems, bufs, after):
    n = len(bufs)

    def body(*refs):
        for cp in _split_gather_copies(refs[:n], refs[n], refs[n + 1]):
            cp.wait_send()
            cp.wait_recv()

    return list(pl.pallas_call(
        body, name=name, out_shape=tuple(pltpu.HBM(b.shape, b.dtype) for b in bufs),
        in_specs=[_HBM] * n + [_SEM, _SEM, _ANY], out_specs=tuple([_HBM] * n),
        input_output_aliases={k: k for k in range(n)}, compiler_params=_SPLIT_PARAMS,
    )(*bufs, send_sems, recv_sems, after))


def _split_chip_copies(srcs, bufs, modes, send_sems, recv_sems):
    x, y, c = lax.axis_index("x"), lax.axis_index("y"), lax.axis_index("c")
    mine = 2 * x + y
    copies = []
    for k in range(len(srcs)):
        for r in range(1, 4):
            px = 1 - x if r & 2 else x
            py = 1 - y if r & 1 else y
            src = srcs[k].at[2 * px + py] if modes[k] == "scatter" else srcs[k]
            copies.append(pltpu.make_async_remote_copy(
                src_ref=src, dst_ref=bufs[k].at[mine], send_sem=send_sems.at[3 * k + r - 1],
                recv_sem=recv_sems.at[3 * k + r - 1], device_id=(px, py, c), device_id_type=MESH))
    return copies


def chip_start(name, arrays, modes):
    n = len(arrays)
    bufs = chip_buffers(arrays, modes)

    def body(*refs):
        for cp in _split_chip_copies(refs[2 * n + 2:3 * n + 2], refs[3 * n + 2:4 * n + 2], modes, refs[2 * n], refs[2 * n + 1]):
            cp.start()
        refs[4 * n + 2][...] = jnp.zeros_like(refs[4 * n + 2])

    both = list(arrays) + list(bufs)
    res = pl.pallas_call(
        body, name=name,
        out_shape=(pltpu.SemaphoreType.DMA((3 * n,)), pltpu.SemaphoreType.DMA((3 * n,)),
                   *[pltpu.HBM(b.shape, b.dtype) for b in both], jax.ShapeDtypeStruct((8, 128), f32)),
        in_specs=[_HBM] * (2 * n), out_specs=(_SEM, _SEM, *[_HBM] * (2 * n), pl.BlockSpec(memory_space=pltpu.VMEM)),
        input_output_aliases={k: 2 + k for k in range(2 * n)}, compiler_params=_SPLIT_PARAMS,
    )(*[pltpu.with_memory_space_constraint(b, pltpu.HBM) for b in both])
    return res[0], res[1], list(res[2:2 + n]), list(res[2 + n:2 + 2 * n]), res[2 + 2 * n]


def chip_wait(name, send_sems, recv_sems, srcs, bufs, modes, after):
    n = len(bufs)

    def body(*refs):
        for cp in _split_chip_copies(refs[:n], refs[n:2 * n], modes, refs[2 * n], refs[2 * n + 1]):
            cp.wait_send()
            cp.wait_recv()

    both = list(srcs) + list(bufs)
    res = pl.pallas_call(
        body, name=name, out_shape=tuple(pltpu.HBM(b.shape, b.dtype) for b in both),
        in_specs=[_HBM] * (2 * n) + [_SEM, _SEM, _ANY], out_specs=tuple([_HBM] * (2 * n)),
        input_output_aliases={k: k for k in range(2 * n)}, compiler_params=_SPLIT_PARAMS,
    )(*both, send_sems, recv_sems, after)
    return list(res[n:])


def gather2(name, arrays):
    n = len(arrays)

    def body(*refs):
        start, forward, finish = _two_level_gather(refs[n:2 * n], *refs[2 * n + 1:])
        start()
        forward()
        finish()
        refs[2 * n][...] = jnp.zeros_like(refs[2 * n])

    bufs = gather_buffers(arrays)
    res = pl.pallas_call(
        body, name=name, out_shape=[jax.ShapeDtypeStruct(b.shape, b.dtype) for b in bufs] + [jax.ShapeDtypeStruct((8, 128), f32)],
        in_specs=[_ANY] * n, out_specs=[_ANY] * n + [pl.BlockSpec(memory_space=pltpu.VMEM)],
        input_output_aliases={k: k for k in range(n)}, scratch_shapes=_sems(n, N_DEV - 1),
    )(*bufs)
    return list(res[:n]), res[n]


def assemble(name, slabs):
    _, K, Ns = slabs.shape
    g = N_DEV if Ns % 128 == 0 else 2

    def body(w_ref, o_ref):
        o_ref[...] = jnp.concatenate([w_ref[s] for s in range(g)], axis=1)

    return pl.pallas_call(
        body, name=name, grid=(N_DEV // g,),
        in_specs=[pl.BlockSpec((g, K, Ns), lambda m: (m, 0, 0))],
        out_specs=pl.BlockSpec((K, g * Ns), lambda m: (0, m)),
        out_shape=jax.ShapeDtypeStruct((K, N_DEV * Ns), slabs.dtype),
        compiler_params=_cparams(1),
    )(slabs)


def mm_nn(name, a, b, out_dtype, add=None, tm=1024, tn=512, tk=None):
    M, K = a.shape
    N = b.shape[1]
    tm, tn = min(tm, M), min(tn, N)
    tk = K if tk is None else tk
    nk = K // tk
    has_add = add is not None

    def body(a_ref, b_ref, *rest):
        o_ref = rest[1] if has_add else rest[0]
        part = jnp.dot(a_ref[...].astype(bf16), b_ref[...].astype(bf16), preferred_element_type=f32)
        if nk == 1:
            o_ref[...] = (part + rest[0][...] if has_add else part).astype(o_ref.dtype)
            return
        acc_ref = rest[-1]
        k = pl.program_id(2)

        @pl.when(k == 0)
        def _():
            acc_ref[...] = part + rest[0][...] if has_add else part

        @pl.when(k > 0)
        def _():
            acc_ref[...] += part

        @pl.when(k == nk - 1)
        def _():
            o_ref[...] = acc_ref[...].astype(o_ref.dtype)

    in_specs = [pl.BlockSpec((tm, tk), lambda i, j, k: (i, k)), pl.BlockSpec((tk, tn), lambda i, j, k: (k, j))]
    ops = [a, b]
    if has_add:
        in_specs.append(pl.BlockSpec((tm, tn), lambda i, j, k: (i, j)))
        ops.append(add)
    return pl.pallas_call(
        body, name=name, grid=(M // tm, N // tn, nk), in_specs=in_specs,
        out_specs=pl.BlockSpec((tm, tn), lambda i, j, k: (i, j)),
        out_shape=jax.ShapeDtypeStruct((M, N), out_dtype),
        scratch_shapes=[pltpu.VMEM((tm, tn), f32)] if nk > 1 else [], compiler_params=_cparams(3),
    )(*ops)


def mm_many(name, xs, ws, out_dtype, transpose_w=False, tm=512):
    n = len(xs)
    M = xs[0].shape[0]
    tm = min(tm, M)
    dims = NT_DIMS if transpose_w else (((1,), (0,)), ((), ()))

    def body(*refs):
        for k in range(n):
            refs[2 * n + k][...] = lax.dot_general(refs[k][...].astype(bf16), refs[n + k][...].astype(bf16), dims,
                                                   preferred_element_type=f32).astype(out_dtype)

    n_out = [w.shape[0] if transpose_w else w.shape[1] for w in ws]
    return pl.pallas_call(
        body, name=name, grid=(M // tm,),
        in_specs=[pl.BlockSpec((tm, x.shape[1]), lambda i: (i, 0)) for x in xs]
        + [pl.BlockSpec(w.shape, lambda i: (0, 0)) for w in ws],
        out_specs=[pl.BlockSpec((tm, c), lambda i: (i, 0)) for c in n_out],
        out_shape=[jax.ShapeDtypeStruct((M, c), out_dtype) for c in n_out], compiler_params=_cparams(1),
    )(*xs, *ws)


def concat_cols(name, pieces, T, tm=512):
    widths = [p.shape[1] for p in pieces]
    total = sum(widths)
    tm = min(tm, T)

    def body(*refs):
        o_ref = refs[-1]
        off = 0
        for r, w in zip(refs[:-1], widths):
            o_ref[:, off:off + w] = r[...]
            off += w

    return pl.pallas_call(
        body, name=name, grid=(T // tm,),
        in_specs=[pl.BlockSpec((tm, w), lambda i: (i, 0)) for w in widths],
        out_specs=pl.BlockSpec((tm, total), lambda i: (i, 0)),
        out_shape=jax.ShapeDtypeStruct((T, total), pieces[0].dtype), compiler_params=_cparams(1),
    )(*pieces)


def mm_nt(name, a, b, out_dtype, tm=1024, tko=None, tc=None):
    M, C = a.shape
    Ko = b.shape[0]
    tm = min(tm, M)
    tc = C if tc is None else min(tc, C)
    tko = Ko if tko is None else tko
    nc = C // tc

    def body(a_ref, b_ref, o_ref, *scratch):
        part = lax.dot_general(a_ref[...].astype(bf16), b_ref[...].astype(bf16), (((1,), (1,)), ((), ())),
                               preferred_element_type=f32)
        if nc == 1:
            o_ref[...] = part.astype(o_ref.dtype)
            return
        acc_ref = scratch[0]
        c = pl.program_id(2)

        @pl.when(c == 0)
        def _():
            acc_ref[...] = part

        @pl.when(c > 0)
        def _():
            acc_ref[...] += part

        @pl.when(c == nc - 1)
        def _():
            o_ref[...] = acc_ref[...].astype(o_ref.dtype)

    return pl.pallas_call(
        body, name=name, grid=(M // tm, Ko // tko, nc),
        in_specs=[pl.BlockSpec((tm, tc), lambda i, j, c: (i, c)), pl.BlockSpec((tko, tc), lambda i, j, c: (j, c))],
        out_specs=pl.BlockSpec((tm, tko), lambda i, j, c: (i, j)),
        out_shape=jax.ShapeDtypeStruct((M, Ko), out_dtype),
        scratch_shapes=[pltpu.VMEM((tm, tko), f32)] if nc > 1 else [], compiler_params=_cparams(3),
    )(a, b)


def mm_tn(name, a, b, n_slab=None, tk=512, tn=512):
    T, K = a.shape
    N = b.shape[1]
    tk = min(tk, K)
    if n_slab is None:
        tn = min(tn, N)

        def body(a_ref, b_ref, o_ref):
            o_ref[...] = lax.dot_general(a_ref[...].astype(bf16), b_ref[...].astype(bf16), (((0,), (0,)), ((), ())),
                                         preferred_element_type=f32).astype(o_ref.dtype)

        return pl.pallas_call(
            body, name=name, grid=(N // tn, K // tk),
            in_specs=[pl.BlockSpec((T, tk), lambda j, i: (0, i)), pl.BlockSpec((T, tn), lambda j, i: (0, j))],
            out_specs=pl.BlockSpec((tk, tn), lambda j, i: (i, j)),
            out_shape=jax.ShapeDtypeStruct((K, N), bf16), compiler_params=_cparams(2),
        )(a, b)

    Ns = n_slab
    g = N_DEV if Ns % 128 == 0 else 2
    tn = g * Ns

    def body(a_ref, b_ref, o_ref):
        val = lax.dot_general(a_ref[...].astype(bf16), b_ref[...].astype(bf16), (((0,), (0,)), ((), ())),
                              preferred_element_type=f32)
        for s in range(g):
            o_ref[s] = val[:, s * Ns:(s + 1) * Ns].astype(o_ref.dtype)

    return pl.pallas_call(
        body, name=name, grid=(K // tk, N_DEV // g),
        in_specs=[pl.BlockSpec((T, tk), lambda i, j: (0, i)), pl.BlockSpec((T, tn), lambda i, j: (0, j))],
        out_specs=pl.BlockSpec((g, tk, Ns), lambda i, j: (j, i, 0)),
        out_shape=jax.ShapeDtypeStruct((N_DEV, K, Ns), bf16), compiler_params=_cparams(2),
    )(a, b)


def _tile_spec(w, tm, cb):
    if callable(cb):
        return pl.BlockSpec((tm, w), lambda c, i: (i, cb(c)))
    return pl.BlockSpec((tm, w), lambda c, i: (i, cb))


def _param_spec(block, cb):
    nd = len(block)
    if cb is None:
        return pl.BlockSpec(block, lambda c, i: (0,) * nd)
    return pl.BlockSpec(block, lambda c, i: (0,) * (nd - 1) + (cb(c),))


ROW_TILE = 512


def rowwise(name, fn, tiled, params, outs, T, tm, ncol=1):
    tm = min(tm, T)
    n_in = len(tiled) + len(params)
    n_t = len(tiled)

    def body(*refs):
        ins = [r[...].astype(f32) for r in refs[:n_t]] + [r[...] for r in refs[n_t:n_in]]
        res = fn(*ins)
        for r, o in zip(refs[n_in:], res):
            r[...] = o.astype(r.dtype)

    return pl.pallas_call(
        body, name=name, grid=(ncol, T // tm),
        in_specs=[_tile_spec(w, tm, cb) for _, w, cb in tiled] + [_param_spec(blk, cb) for _, blk, cb in params],
        out_specs=[_tile_spec(w, tm, cb) for _, w, cb, _ in outs],
        out_shape=[jax.ShapeDtypeStruct((T, cols), dt) for cols, _, _, dt in outs],
        compiler_params=_cparams(2),
    )(*[a for a, _, _ in tiled], *[a for a, _, _ in params])


def rowwise_bwd(name, fn, tiled, params, cts, grads, T, tm, ncol=1, adds=None):
    tm = min(tm, T)
    n_t, n_p, n_c = len(tiled), len(params), len(cts)
    adds = adds or [None] * n_t
    add_list = [(k, a) for k, a in enumerate(adds) if a is not None]
    want = [k for k, g in enumerate(grads) if g is not None]
    n_a = len(add_list)

    def body(*refs):
        pos = 0
        t_refs = refs[pos:pos + n_t]; pos += n_t
        p_refs = refs[pos:pos + n_p]; pos += n_p
        c_refs = refs[pos:pos + n_c]; pos += n_c
        a_refs = refs[pos:pos + n_a]; pos += n_a
        g_refs = refs[pos:pos + len(want)]; pos += len(want)
        pg_refs = refs[pos:pos + n_p]
        primals = [r[...].astype(f32) for r in t_refs] + [r[...] for r in p_refs]
        _, vjp = jax.vjp(fn, *primals)
        g = vjp(tuple(r[...].astype(f32) for r in c_refs))
        add_of = {k: a_refs[n][...] for n, (k, _) in enumerate(add_list)}
        for n, k in enumerate(want):
            val = g[k]
            if k in add_of:
                val = val + add_of[k]
            g_refs[n][...] = val.astype(g_refs[n].dtype)
        i = pl.program_id(1)
        for k in range(n_p):
            @pl.when(i == 0)
            def _(k=k):
                pg_refs[k][...] = g[n_t + k]

            @pl.when(i > 0)
            def _(k=k):
                pg_refs[k][...] += g[n_t + k]

    in_specs = ([_tile_spec(w, tm, cb) for _, w, cb in tiled] + [_param_spec(blk, cb) for _, blk, cb in params]
                + [_tile_spec(w, tm, cb) for _, w, cb in cts] + [_tile_spec(w, tm, cb) for _, (_, w, cb) in add_list])
    ops = ([a for a, _, _ in tiled] + [a for a, _, _ in params] + [a for a, _, _ in cts]
           + [a for _, (a, _, _) in add_list])
    out_specs, out_shape = [], []
    for k in want:
        w = tiled[k][1]
        out_specs.append(_tile_spec(w, tm, lambda c: c))
        out_shape.append(jax.ShapeDtypeStruct((T, ncol * w), grads[k]))
    for a, blk, cb in params:
        out_specs.append(_param_spec(blk, cb))
        out_shape.append(jax.ShapeDtypeStruct(a.shape, f32))
    return pl.pallas_call(
        body, name=name, grid=(ncol, T // tm), in_specs=in_specs, out_specs=out_specs, out_shape=out_shape,
        compiler_params=_cparams(2),
    )(*ops)


@jax.custom_vjp
def _bdot(a, b):
    return jnp.dot(a.astype(bf16), b.astype(bf16), preferred_element_type=f32)


def _bdot_fwd(a, b):
    return _bdot(a, b), (a, b)


def _bdot_bwd(res, ct):
    a, b = res
    ctb = ct.astype(bf16)
    da = lax.dot_general(ctb, b.astype(bf16), (((1,), (1,)), ((), ())), preferred_element_type=f32)
    db = lax.dot_general(a.astype(bf16), ctb, (((0,), (0,)), ((), ())), preferred_element_type=f32)
    return da, db


_bdot.defvjp(_bdot_fwd, _bdot_bwd)


def _layer_norm(x, g, b):
    mu = jnp.mean(x, axis=-1, keepdims=True)
    xc = x - mu
    y = xc * lax.rsqrt(jnp.mean(xc * xc, axis=-1, keepdims=True) + EPS)
    return y * g + b


def f_rms(x, g):
    y = x * lax.rsqrt(jnp.mean(x * x, axis=-1, keepdims=True) + EPS)
    return (y * g,)


def f_sgu(zu, zv, ln_g, ln_b, wm, sgb_t):
    u = jax.nn.gelu(zu)
    vn = _layer_norm(jax.nn.gelu(zv), ln_g, ln_b)
    row = lax.broadcasted_iota(jnp.int32, (CHUNK, CHUNK), 0)
    col = lax.broadcasted_iota(jnp.int32, (CHUNK, CHUNK), 1)
    tril = col <= row
    low = col < HEAD_DIM
    parts = []
    for p in range(4):
        vp = vn[:, CHUNK * p:CHUNK * (p + 1)]
        w0 = jnp.where(tril, wm[2 * p], 0.0)
        w1 = jnp.where(tril, wm[2 * p + 1], 0.0)
        parts.append(_bdot(w0, jnp.where(low, vp, 0.0)) + _bdot(w1, jnp.where(low, 0.0, vp)))
    s = jnp.concatenate(parts, axis=1)
    lane_g = lax.shift_right_logical(lax.broadcasted_iota(jnp.int32, s.shape, 1), 6)
    bias = jnp.zeros_like(s)
    for g in range(8):
        bias = jnp.where(lane_g == g, sgb_t[:, g:g + 1], bias)
    return (u * (s + bias),)


def f_glu(p, gl):
    return (p * jax.nn.sigmoid(gl),)


def f_lnsilu(c1, g, b):
    return (jax.nn.silu(_layer_norm(c1, g, b)),)


def _head_norm(x, g64):
    g = jnp.concatenate([g64] * N_HEADS, axis=1)
    lane_h = lax.shift_right_logical(lax.broadcasted_iota(jnp.int32, x.shape, 1), 6)
    x2 = x * x
    r = jnp.zeros_like(x)
    for h in range(N_HEADS):
        mh = lane_h == h
        ms = jnp.sum(jnp.where(mh, x2, 0.0), axis=-1, keepdims=True) * (1.0 / HEAD_DIM)
        r = jnp.where(mh, lax.rsqrt(ms + EPS), r)
    return (x * r) * g


def f_qkv(zq, zk, zv, qg, kg):
    return (_head_norm(zq, qg) * 0.125, _head_norm(zk, kg), zv)


def f_merge(g0, g1, g2, ya, yb, yc, b0, b1, b2):
    return (jax.nn.sigmoid(g0 + b0) * ya + jax.nn.sigmoid(g1 + b1) * yb + jax.nn.sigmoid(g2 + b2) * yc,)


HALO = 32


def conv31_fwd(x, w, b, T, tt=256):
    C = x.shape[1]
    r = tt // HALO

    def body(x_ref, h_ref, w_ref, b_ref, y_ref, buf):
        i = pl.program_id(0)
        halo = h_ref[...]
        buf[0:HALO, :] = jnp.where(i > 0, halo, jnp.zeros_like(halo))
        buf[HALO:HALO + tt, :] = x_ref[...]
        acc = jnp.zeros((tt, C), f32) + b_ref[...]
        for k in range(CV_KERNEL):
            acc = acc + w_ref[k:k + 1, :] * buf[pl.ds(HALO - (CV_KERNEL - 1) + k, tt), :]
        y_ref[...] = acc

    return pl.pallas_call(
        body, name="conv31_fwd", grid=(T // tt,),
        in_specs=[pl.BlockSpec((tt, C), lambda i: (i, 0)),
                  pl.BlockSpec((HALO, C), lambda i: (jnp.maximum(i * r - 1, 0), 0)),
                  pl.BlockSpec((CV_KERNEL, C), lambda i: (0, 0)), pl.BlockSpec((1, C), lambda i: (0, 0))],
        out_specs=pl.BlockSpec((tt, C), lambda i: (i, 0)),
        out_shape=jax.ShapeDtypeStruct((T, C), f32),
        scratch_shapes=[pltpu.VMEM((HALO + tt, C), f32)], compiler_params=_cparams(1),
    )(x, x, w, b)


def conv31_bwd(x, w, dy, T, tt=256):
    C = x.shape[1]
    r = tt // HALO
    n = T // tt

    def body(x_ref, h_ref, w_ref, dy_ref, dyn_ref, dx_ref, dw_ref, db_ref, xbuf, dbuf):
        i = pl.program_id(0)
        halo = h_ref[...]
        xbuf[0:HALO, :] = jnp.where(i > 0, halo, jnp.zeros_like(halo))
        xbuf[HALO:HALO + tt, :] = x_ref[...]
        nxt = dyn_ref[...]
        dy = dy_ref[...]
        dbuf[0:tt, :] = dy
        dbuf[tt:tt + HALO, :] = jnp.where(i < n - 1, nxt, jnp.zeros_like(nxt))

        @pl.when(i == 0)
        def _():
            dw_ref[...] = jnp.zeros_like(dw_ref)
            db_ref[...] = jnp.zeros_like(db_ref)

        acc = jnp.zeros((tt, C), f32)
        for k in range(CV_KERNEL):
            acc = acc + w_ref[k:k + 1, :] * dbuf[pl.ds(CV_KERNEL - 1 - k, tt), :]
            xs = xbuf[pl.ds(HALO - (CV_KERNEL - 1) + k, tt), :]
            dw_ref[k:k + 1, :] += jnp.sum(dy * xs, axis=0, keepdims=True)
        dx_ref[...] = acc
        db_ref[...] += jnp.sum(dy, axis=0, keepdims=True)

    return pl.pallas_call(
        body, name="conv31_bwd", grid=(n,),
        in_specs=[pl.BlockSpec((tt, C), lambda i: (i, 0)),
                  pl.BlockSpec((HALO, C), lambda i: (jnp.maximum(i * r - 1, 0), 0)),
                  pl.BlockSpec((CV_KERNEL, C), lambda i: (0, 0)),
                  pl.BlockSpec((tt, C), lambda i: (i, 0)),
                  pl.BlockSpec((HALO, C), lambda i: (jnp.minimum((i + 1) * r, n * r - 1), 0))],
        out_specs=[pl.BlockSpec((tt, C), lambda i: (i, 0)), pl.BlockSpec((CV_KERNEL, C), lambda i: (0, 0)),
                   pl.BlockSpec((1, C), lambda i: (0, 0))],
        out_shape=[jax.ShapeDtypeStruct((T, C), f32), jax.ShapeDtypeStruct((CV_KERNEL, C), f32),
                   jax.ShapeDtypeStruct((1, C), f32)],
        scratch_shapes=[pltpu.VMEM((HALO + tt, C), f32), pltpu.VMEM((HALO + tt, C), f32)], compiler_params=_cparams(1),
    )(x, x, w, dy, dy)


FFN_TC = 128
FFN_PAD = 8


def ffn_act_fwd(up, cw, cb, T):
    F = up.shape[1] // 2
    nj = F // FFN_TC
    rc = min(128, T)

    def body(g_ref, v_ref, g0, g1, g2, gb, v0, v1, v2, vb, o_ref, gp, vp):
        zeros = jnp.zeros((FFN_PAD, FFN_TC), f32)
        for p, x_ref in ((gp, g_ref), (vp, v_ref)):
            p[0:FFN_PAD, :] = zeros
            p[FFN_PAD:FFN_PAD + T, :] = x_ref[...].astype(f32)
        wg = (g0[...], g1[...], g2[...], gb[...])
        wv = (v0[...], v1[...], v2[...], vb[...])

        def conv(p, w, r):
            return (w[0] * p[pl.ds(FFN_PAD + r - 2, rc), :] + w[1] * p[pl.ds(FFN_PAD + r - 1, rc), :]
                    + w[2] * p[pl.ds(FFN_PAD + r, rc), :] + w[3])

        for r in range(0, T, rc):
            o_ref[pl.ds(r, rc), :] = (jax.nn.silu(conv(gp, wg, r)) * conv(vp, wv, r)).astype(o_ref.dtype)

    gspec = pl.BlockSpec((T, FFN_TC), lambda j: (0, j))
    vspec = pl.BlockSpec((T, FFN_TC), lambda j: (0, j + nj))
    pg = pl.BlockSpec((1, FFN_TC), lambda j: (0, j))
    pv = pl.BlockSpec((1, FFN_TC), lambda j: (0, j + nj))
    return pl.pallas_call(
        body, name="ffn_act_fwd", grid=(nj,),
        in_specs=[gspec, vspec, pg, pg, pg, pg, pv, pv, pv, pv], out_specs=gspec,
        out_shape=jax.ShapeDtypeStruct((T, F), bf16),
        scratch_shapes=[pltpu.VMEM((FFN_PAD + T, FFN_TC), f32)] * 2, compiler_params=_cparams(1),
    )(up, up, cw[0], cw[1], cw[2], cb, cw[0], cw[1], cw[2], cb)


def ffn_act_bwd(up, cw, cb, dact, T):
    F = up.shape[1] // 2
    nj = F // FFN_TC

    rc = min(128, T)
    ext = rc + FFN_PAD

    def body(g_ref, v_ref, g0, g1, g2, gb, v0, v1, v2, vb, d_ref, dupg_ref, dupv_ref, *rest):
        p_refs, (gp, vp, dp, dgs, dvs) = rest[:8], rest[8:]
        zeros = jnp.zeros((FFN_PAD, FFN_TC), f32)
        for p, x_ref in ((gp, g_ref), (vp, v_ref)):
            p[0:FFN_PAD, :] = zeros
            p[FFN_PAD:FFN_PAD + T, :] = x_ref[...].astype(f32)
            p[FFN_PAD + T:FFN_PAD + T + FFN_PAD, :] = zeros
        dp[0:T, :] = d_ref[...]
        dp[T:T + FFN_PAD, :] = zeros
        wg = (g0[...], g1[...], g2[...], gb[...])
        wv = (v0[...], v1[...], v2[...], vb[...])
        acc = [jnp.zeros((1, FFN_TC), f32) for _ in range(8)]

        def taps(p, r):
            return tuple(p[pl.ds(FFN_PAD + r - s, ext), :] for s in (2, 1, 0))

        for r in range(0, T, rc):
            xg, xv = taps(gp, r), taps(vp, r)
            gc = wg[0] * xg[0] + wg[1] * xg[1] + wg[2] * xg[2] + wg[3]
            vc = wv[0] * xv[0] + wv[1] * xv[1] + wv[2] * xv[2] + wv[3]
            d = dp[pl.ds(r, ext), :]
            sg = jax.nn.sigmoid(gc)
            sides = ((d * vc * (sg * (1.0 + gc * (1.0 - sg))), xg, wg, dgs, dupg_ref), (d * (gc * sg), xv, wv, dvs, dupv_ref))
            for side, (dc, x, w, buf, dup_ref) in enumerate(sides):
                buf[...] = dc
                dc0 = dc[0:rc]
                dup = w[2] * dc0 + w[1] * buf[pl.ds(1, rc), :] + w[0] * buf[pl.ds(2, rc), :]
                dup_ref[pl.ds(r, rc), :] = dup.astype(dup_ref.dtype)
                for k in range(3):
                    acc[4 * side + k] = acc[4 * side + k] + jnp.sum(dc0 * x[k][0:rc], axis=0, keepdims=True)
                acc[4 * side + 3] = acc[4 * side + 3] + jnp.sum(dc0, axis=0, keepdims=True)
        for k in range(8):
            p_refs[k][...] = acc[k]

    gspec = pl.BlockSpec((T, FFN_TC), lambda j: (0, j))
    vspec = pl.BlockSpec((T, FFN_TC), lambda j: (0, j + nj))
    pg = pl.BlockSpec((1, FFN_TC), lambda j: (0, j))
    pv = pl.BlockSpec((1, FFN_TC), lambda j: (0, j + nj))
    res = pl.pallas_call(
        body, name="ffn_act_bwd", grid=(nj,),
        in_specs=[gspec, vspec, pg, pg, pg, pg, pv, pv, pv, pv, gspec],
        out_specs=[gspec, gspec] + [pg] * 8,
        out_shape=[jax.ShapeDtypeStruct((T, F), bf16)] * 2 + [jax.ShapeDtypeStruct((1, F), f32)] * 8,
        scratch_shapes=[pltpu.VMEM((FFN_PAD + T + FFN_PAD, FFN_TC), f32)] * 2 + [pltpu.VMEM((T + FFN_PAD, FFN_TC), f32)]
        + [pltpu.VMEM((ext, FFN_TC), f32)] * 2,
        compiler_params=_cparams(1),
    )(up, up, cw[0], cw[1], cw[2], cb, cw[0], cw[1], cw[2], cb, dact)
    dup = concat_cols("dup_concat", [res[0], res[1]], T)
    return (dup,) + tuple(jnp.concatenate([res[2 + k], res[6 + k]], axis=1) for k in range(4))


BQ = 256
BK = 256
assert BQ == BK
NT_DIMS = (((1,), (1,)), ((), ()))
TN_DIMS = (((0,), (0,)), ((), ()))


def _split_dot(x, u):
    x1 = x.astype(bf16)
    x2 = (x - x1.astype(f32)).astype(bf16)
    n = x.shape[0]
    y = jnp.dot(jnp.concatenate([x1, x2], axis=0), u, preferred_element_type=f32)
    return y[0:n] + y[n:2 * n]


def _log_sigmoids(z):
    sp = jnp.log(1.0 + jnp.exp(-jnp.abs(z)))
    lsp = jnp.minimum(z, 0.0) - sp
    return lsp, lsp - z


def _stack_heads(x):
    head1 = lax.broadcasted_iota(jnp.int32, x.shape, 1) >= HEAD_DIM
    zero = jnp.zeros_like(x)
    return jnp.concatenate([jnp.where(head1, zero, x), jnp.where(head1, x, zero)], axis=0)


def _unstack_heads(y):
    head1 = lax.broadcasted_iota(jnp.int32, (BQ, y.shape[1]), 1) >= HEAD_DIM
    return jnp.where(head1, y[BQ:2 * BQ], y[0:BQ])


def _attn_masks():
    row = lax.broadcasted_iota(jnp.int32, (2 * BQ, BK), 0)
    col = lax.broadcasted_iota(jnp.int32, (2 * BQ, BK), 1)
    ur = lax.broadcasted_iota(jnp.int32, (BK, BK), 0)
    uc = lax.broadcasted_iota(jnp.int32, (BK, BK), 1)
    return (row & (BQ - 1)) - col, (ur > uc).astype(bf16), (ur < uc).astype(bf16)


def attn_fwd(q, k, v, T, gather=None):
    nq = T // BQ
    n_g = 0 if gather is None else len(gather)

    def body(*refs):
        q_ref, k_ref, v_ref = refs[:3]
        o_ref, tot_ref = refs[3 + n_g:5 + n_g]
        p, i = pl.program_id(0), pl.program_id(1)
        if n_g:
            start, forward, finish = _two_level_gather(refs[5 + n_g:5 + 2 * n_g], *refs[5 + 2 * n_g:])
            pl.when(jnp.logical_and(p == 0, i == 0))(start)
        qs = _stack_heads(q_ref[...])
        diff, u_after, _ = _attn_masks()

        def step(jb, carry, diagonal):
            acc, c = carry
            ks = pl.multiple_of(jb * BK, BK)
            kb = k_ref[pl.ds(ks, BK), :]
            vb = v_ref[pl.ds(ks, BK), :]
            z = lax.dot_general(qs, kb, NT_DIMS, preferred_element_type=f32)
            lsp, lm = _log_sigmoids(z)
            if diagonal:
                m = diff > 0
                lm = jnp.where(m, lm, 0.0)
            a = jnp.exp(lsp + _split_dot(lm, u_after))
            if diagonal:
                a = jnp.where(m, a, 0.0)
            acc = acc + jnp.exp(c) * jnp.dot(a.astype(bf16), vb, preferred_element_type=f32)
            return acc, c + jnp.sum(lm, axis=-1, keepdims=True)

        carry = step(i, (jnp.zeros((2 * BQ, 128), f32), jnp.zeros((2 * BQ, 1), f32)), True)
        acc, c = lax.fori_loop(0, i, lambda t, cr: step(i - 1 - t, cr, False), carry)
        o_ref[...] = _unstack_heads(acc).astype(o_ref.dtype)
        tot_ref[...] = _unstack_heads(jnp.broadcast_to(c, (2 * BQ, 128)))
        if n_g:
            @pl.when(jnp.logical_and(p == 3, i == nq - 1))
            def _():
                forward()
                finish()

    blk = pl.BlockSpec((BQ, 128), lambda p, i: (i, p))
    full = pl.BlockSpec((T, 128), lambda p, i: (0, p))
    bufs = [] if gather is None else gather_buffers(gather)
    res = pl.pallas_call(
        body, name="attn_fwd_gather" if n_g else "attn_fwd", grid=(4, nq),
        in_specs=[blk, full, full] + [_ANY] * n_g, out_specs=[blk, blk] + [_ANY] * n_g,
        out_shape=[jax.ShapeDtypeStruct((T, 512), bf16), jax.ShapeDtypeStruct((T, 512), f32)]
        + [jax.ShapeDtypeStruct(b.shape, b.dtype) for b in bufs],
        input_output_aliases={3 + k: 2 + k for k in range(n_g)},
        scratch_shapes=_sems(n_g, N_DEV - 1) if n_g else [], compiler_params=_cparams(2),
    )(q, k, v, *bufs)
    return res[0], res[1], list(res[2:])


def attn_bwd(q, k, v, do, tot, T, scatter=None):
    nq = T // BQ
    n_s = 0 if scatter is None else len(scatter[0])
    modes = [] if scatter is None else list(scatter[1])

    def body(*refs):
        q_ref, k_ref, v_ref, do_ref, tot_ref = refs[:5]
        dq_ref, dk_ref, dv_ref = refs[5 + 2 * n_s:8 + 2 * n_s]
        p, i = pl.program_id(0), pl.program_id(1)
        if n_s:
            def copies():
                return _chip_copies(refs[5:5 + n_s], refs[8 + 2 * n_s:8 + 3 * n_s], modes, *refs[8 + 3 * n_s:])

            @pl.when(jnp.logical_and(p == 0, i == 0))
            def _():
                for cp in copies():
                    cp.start()

        @pl.when(i == 0)
        def _():
            dk_ref[...] = jnp.zeros_like(dk_ref)
            dv_ref[...] = jnp.zeros_like(dv_ref)

        qs = _stack_heads(q_ref[...])
        dos = _stack_heads(do_ref[...].astype(bf16))
        totv = tot_ref[...]
        tots = jnp.concatenate([totv[:, 0:1], totv[:, HEAD_DIM:HEAD_DIM + 1]], axis=0)
        diff, u_after, u_before = _attn_masks()

        def step(jb, carry, diagonal):
            dq, cl, cg = carry
            ks = pl.multiple_of(jb * BK, BK)
            kb = k_ref[pl.ds(ks, BK), :]
            vb = v_ref[pl.ds(ks, BK), :]
            z = lax.dot_general(qs, kb, NT_DIMS, preferred_element_type=f32)
            lsp, lsn = _log_sigmoids(z)
            lm = lsn
            if diagonal:
                m = diff > 0
                lm = jnp.where(m, lsn, 0.0)
            a = jnp.exp(lsp + _split_dot(lm, u_after))
            if diagonal:
                a = jnp.where(m, a, 0.0)
            g = a * lax.dot_general(dos, vb, NT_DIMS, preferred_element_type=f32)
            bb = jnp.exp(lsp)
            aa = g * jnp.exp(lsn) - _split_dot(g, u_before) * bb
            if diagonal:
                aa = jnp.where(m, aa, 0.0)
                bb = jnp.where(m, bb, 0.0)
            cl = cl + jnp.sum(lm, axis=-1, keepdims=True)
            f = jnp.exp(tots - cl)
            dz = (f * aa - cg * bb).astype(bf16)
            cg = cg + f * jnp.sum(g, axis=-1, keepdims=True)
            dq = dq + jnp.dot(dz, kb, preferred_element_type=f32)
            dk_ref[pl.ds(ks, BK), :] += lax.dot_general(dz, qs, TN_DIMS, preferred_element_type=f32)
            dv_ref[pl.ds(ks, BK), :] += lax.dot_general((f * a).astype(bf16), dos, TN_DIMS, preferred_element_type=f32)
            return dq, cl, cg

        zc = jnp.zeros((2 * BQ, 1), f32)
        carry = lax.fori_loop(0, i, lambda jb, cr: step(jb, cr, False), (jnp.zeros((2 * BQ, 128), f32), zc, zc))
        dq, _, _ = step(i, carry, True)
        dq_ref[...] = _unstack_heads(dq)
        if n_s:
            @pl.when(jnp.logical_and(p == 3, i == nq - 1))
            def _():
                for cp in copies():
                    cp.wait()

    blk = pl.BlockSpec((BQ, 128), lambda p, i: (i, p))
    full = pl.BlockSpec((T, 128), lambda p, i: (0, p))
    shp = jax.ShapeDtypeStruct((T, 512), f32)
    s_list = [] if scatter is None else list(scatter[0])
    bufs = chip_buffers(s_list, modes) if n_s else []
    res = pl.pallas_call(
        body, name="attn_bwd_scatter" if n_s else "attn_bwd", grid=(4, nq),
        in_specs=[blk, full, full, blk, blk] + [_ANY] * (2 * n_s), out_specs=[blk, full, full] + [_ANY] * n_s,
        out_shape=[shp, shp, shp] + [jax.ShapeDtypeStruct(b.shape, b.dtype) for b in bufs],
        input_output_aliases={5 + n_s + k: 3 + k for k in range(n_s)},
        scratch_shapes=_sems(n_s, 3) if n_s else [], compiler_params=_cparams(2),
    )(q, k, v, do, tot, *s_list, *bufs)
    return res[0], res[1], res[2], list(res[3:])


def loss_head(y, target, T, tm=256):
    D = y.shape[1]

    def body(y_ref, t_ref, dy_ref, l_ref):
        i = pl.program_id(0)
        err = y_ref[...] - t_ref[...]
        dy_ref[...] = err * (1.0 / D)
        part = 0.5 * jnp.sum(jnp.sum(err * err, axis=-1, keepdims=True) * (1.0 / D), axis=0, keepdims=True)

        @pl.when(i == 0)
        def _():
            l_ref[...] = jnp.zeros_like(l_ref)

        l_ref[...] += jnp.broadcast_to(part, l_ref.shape)

    spec = pl.BlockSpec((tm, D), lambda i: (i, 0))
    return pl.pallas_call(
        body, name="loss_head", grid=(T // tm,), in_specs=[spec, spec],
        out_specs=[spec, pl.BlockSpec((1, 128), lambda i: (0, 0))],
        out_shape=[jax.ShapeDtypeStruct((T, D), f32), jax.ShapeDtypeStruct((1, 128), f32)],
        compiler_params=_cparams(1),
    )(y, target)


def _row_tile(rows, cols, offset=0, max_elems=128 * 1024):
    best = None
    for t in range(16, rows + 1, 16):
        if rows % t == 0 and offset % t == 0 and t * cols <= max_elems:
            best = t
    return best if best is not None else rows


def adamw(name, w, m, v, parts, layer, prev=None, row_off=0):
    L, R, C = w.shape
    n_parts = parts.shape[0]
    tr = _row_tile(R, C, row_off, max_elems=256 * 1024)
    assert R % tr == 0 and row_off % tr == 0, (name, R, row_off, tr)

    def body(w_ref, m_ref, v_ref, p_ref, *rest):
        g_ref, d_ref, nm_ref, nv_ref = rest[-4:]
        g = p_ref[0].astype(f32)
        for s in range(1, n_parts):
            g = g + p_ref[s].astype(f32)
        wv = w_ref[...]
        mn = ADAM_B1 * m_ref[...] + (1.0 - ADAM_B1) * g
        vn = ADAM_B2 * v_ref[...] + (1.0 - ADAM_B2) * jnp.square(g)
        m_hat = mn / (1.0 - ADAM_B1 ** ADAM_STEP)
        v_hat = vn / (1.0 - ADAM_B2 ** ADAM_STEP)
        g_ref[...] = g
        d_ref[...] = -ADAM_LR * (m_hat / (jnp.sqrt(v_hat) + ADAM_EPS) + ADAM_WD * wv)
        nm_ref[...] = mn
        nv_ref[...] = vn

    spec = pl.BlockSpec((None, tr, C), lambda i: (layer, i, 0))
    shp = jax.ShapeDtypeStruct((L, R, C), f32)
    n_prev = 0 if prev is None else 4
    return pl.pallas_call(
        body, name=name, grid=(R // tr,),
        in_specs=[spec, spec, spec, pl.BlockSpec((n_parts, tr, C), lambda i: (0, row_off // tr + i, 0))] + [_ANY] * n_prev,
        out_specs=[spec] * 4, out_shape=[shp] * 4, input_output_aliases={4 + k: k for k in range(n_prev)},
        compiler_params=_cparams(1),
    )(w, m, v, parts, *(prev or ()))


REPL = ["ln1_g", "sg_ln_g", "sg_ln_b", "sg_w", "sg_b", "cv_b", "cv_ln_g", "cv_ln_b", "q_norm_g", "k_norm_g", "ln2_g",
        "ffn_conv_b"]
SMALL_SHARDED = ["b_gate", "cv_w", "ffn_conv_w"]
BIG = ["w_in", "w_a_out", "w_b_out", "w_c_out", "w_up", "w_out", "w_down"]
COL_SHARDED = ["w_a_out", "w_b_out", "w_c_out"]
TRANSPOSED = ["w_in", "w_up"]


def _pack(arrs, rows):
    flat = jnp.concatenate([a.reshape(-1) for a in arrs])
    return jnp.pad(flat, (0, rows * 128 - flat.shape[0])).reshape(rows, 128)


def _pack_layers(arrs, rows):
    flat = jnp.concatenate([a.reshape(a.shape[0], -1) for a in arrs], axis=1)
    return jnp.pad(flat, ((0, 0), (0, rows * 128 - flat.shape[1]))).reshape(flat.shape[0], rows, 128)


def _unpack_layers(packed, shapes):
    flat = packed.reshape(packed.shape[0], -1)
    out, pos = [], 0
    for s in shapes:
        n = 1
        for d in s[1:]:
            n *= d
        out.append(flat[:, pos:pos + n].reshape(s))
        pos += n
    return out


def _rows_for(shapes, mult):
    n = 0
    for s in shapes:
        k = 1
        for d in s:
            k *= d
        n += k
    rows = -(-n // 128)
    return -(-rows // mult) * mult


W_NAMES = ['ln1_g', 'w_in', 'b_gate', 'sg_ln_g', 'sg_ln_b', 'sg_w', 'sg_b', 'w_a_out', 'cv_w', 'cv_b', 'cv_ln_g', 'cv_ln_b',
           'w_b_out', 'q_norm_g', 'k_norm_g', 'w_c_out', 'w_out', 'ln2_g', 'w_up', 'ffn_conv_w', 'ffn_conv_b', 'w_down']


def _forward_layer(x, P, T, late=None):
    D = x.shape[1]
    sv = {"x0": x}
    (h1,) = rowwise("rms_fwd", f_rms, [(x, D, 0)], [(P["ln1_g"], (1, D), None)], [(D, D, 0, bf16)], T, ROW_TILE)
    z = mm_nt("in_proj", h1, P["w_in_t"], bf16, tko=1664, tc=D)
    sv["h1"], sv["z"] = h1, z
    (ya_in,) = rowwise("sgu_fwd", f_sgu, [(z, 512, 0), (z, 512, 1)],
                       [(P["sg_ln_g"], (1, 512), None), (P["sg_ln_b"], (1, 512), None),
                        (P["sg_w"], (8, CHUNK, CHUNK), None), (P["sg_b_t"], (CHUNK, 8), None)],
                       [(512, 512, 0, bf16)], T, CHUNK)
    (c0,) = rowwise("glu_fwd", f_glu, [(z, 512, 2), (z, 512, 3)], [], [(512, 512, 0, f32)], T, ROW_TILE)
    c1 = conv31_fwd(c0, P["cv_w"], P["cv_b"], T)
    (c3,) = rowwise("lnsilu_fwd", f_lnsilu, [(c1, 512, 0)], [(P["cv_ln_g"], (1, 512), None), (P["cv_ln_b"], (1, 512), None)],
                    [(512, 512, 0, bf16)], T, ROW_TILE)
    q8, kn, vb = rowwise("qkv_fwd", f_qkv, [(z, 512, 4), (z, 512, 5), (z, 512, 6)],
                         [(P["q_norm_g"], (1, HEAD_DIM), None), (P["k_norm_g"], (1, HEAD_DIM), None)],
                         [(512, 512, 0, bf16)] * 3, T, ROW_TILE)
    o, tot, _ = attn_fwd(q8, kn, vb, T)
    if late is not None:
        P.update(late(o))
    ya, yb, yc = mm_many("branch_out", [ya_in, c3, o], [P["w_a_out"], P["w_b_out"], P["w_c_out"]], f32)
    (merged,) = rowwise("merge_fwd", f_merge,
                        [(z, 512, lambda c: 7 + c), (z, 512, lambda c: 9 + c), (z, 512, lambda c: 11 + c),
                         (ya, 512, lambda c: c), (yb, 512, lambda c: c), (yc, 512, lambda c: c)],
                        [(P["b_gate"][k], (1, 512), lambda c: c) for k in range(3)],
                        [(D, 512, lambda c: c, bf16)], T, 256, ncol=2)
    x1 = mm_nn("out_proj", merged, P["w_out"], f32, add=x)
    sv.update(ya_in=ya_in, ya=ya, c0=c0, c1=c1, c3=c3, yb=yb, q8=q8, kn=kn, vb=vb, o=o, tot=tot, yc=yc, merged=merged, x1=x1)
    (h2,) = rowwise("rms_fwd", f_rms, [(x1, D, 0)], [(P["ln2_g"], (1, D), None)], [(D, D, 0, bf16)], T, ROW_TILE)
    up = mm_nt("up_proj", h2, P["w_up_t"], bf16, tko=1408, tc=D)
    act = ffn_act_fwd(up, P["ffn_conv_w"], P["ffn_conv_b"], T)
    x2 = mm_nn("down_proj", act, P["w_down"], f32, add=x1)
    sv.update(h2=h2, up=up, act=act)
    return x2, sv


def _backward_layer(dx2, P, sv, T, scatter=None):
    D = dx2.shape[1]
    G = {}
    G["w_down"] = mm_tn("dw_down", sv["act"], dx2, tk=1408)
    dact = mm_nt("d_act", dx2, P["w_down"], f32, tko=1408, tc=D)
    dup, dcw0, dcw1, dcw2, G["ffn_conv_b"] = ffn_act_bwd(sv["up"], P["ffn_conv_w"], P["ffn_conv_b"], dact, T)
    G["ffn_conv_w"] = jnp.concatenate([dcw0, dcw1, dcw2], axis=0)
    G["w_up"] = mm_tn("dw_up", dup, sv["h2"], tk=1408)
    dh2 = mm_nn("d_h2", dup, P["w_up_t"], f32, tm=512)
    dx1, G["ln2_g"] = rowwise_bwd("rms_bwd", f_rms, [(sv["x1"], D, 0)], [(P["ln2_g"], (1, D), None)], [(dh2, D, 0)],
                                  [f32], T, ROW_TILE, adds=[(dx2, D, 0)])
    G["w_out"] = mm_tn("dw_out", sv["merged"], dx1, tk=1024)
    dmerged = mm_nt("d_merged", dx1, P["w_out"], f32)
    z = sv["z"]
    dg0, dg1, dg2, dya, dyb, dyc, db0, db1, db2 = rowwise_bwd(
        "merge_bwd", f_merge,
        [(z, 512, lambda c: 7 + c), (z, 512, lambda c: 9 + c), (z, 512, lambda c: 11 + c),
         (sv["ya"], 512, lambda c: c), (sv["yb"], 512, lambda c: c), (sv["yc"], 512, lambda c: c)],
        [(P["b_gate"][k], (1, 512), lambda c: c) for k in range(3)],
        [(dmerged, 512, lambda c: c)], [bf16] * 6, T, 256, ncol=2)
    G["b_gate"] = jnp.concatenate([db0, db1, db2], axis=0)
    dya_in, dc3, do = mm_many("d_branch_in", [dya, dyb, dyc], [P["w_a_out"], P["w_b_out"], P["w_c_out"]], f32,
                              transpose_w=True)
    G["w_c_out"] = mm_tn("dw_c_out", sv["o"], dyc, n_slab=CHUNK)
    dq8, dkn, dvb, received = attn_bwd(sv["q8"], sv["kn"], sv["vb"], do, sv["tot"], T, scatter=scatter)
    dzq, dzk, dzv, G["q_norm_g"], G["k_norm_g"] = rowwise_bwd(
        "qkv_bwd", f_qkv, [(z, 512, 4), (z, 512, 5), (z, 512, 6)],
        [(P["q_norm_g"], (1, HEAD_DIM), None), (P["k_norm_g"], (1, HEAD_DIM), None)],
        [(dq8, 512, 0), (dkn, 512, 0), (dvb, 512, 0)], [bf16] * 3, T, ROW_TILE)
    G["w_b_out"] = mm_tn("dw_b_out", sv["c3"], dyb, n_slab=CHUNK)
    dc1, G["cv_ln_g"], G["cv_ln_b"] = rowwise_bwd(
        "lnsilu_bwd", f_lnsilu, [(sv["c1"], 512, 0)], [(P["cv_ln_g"], (1, 512), None), (P["cv_ln_b"], (1, 512), None)],
        [(dc3, 512, 0)], [f32], T, ROW_TILE)
    dc0, G["cv_w"], G["cv_b"] = conv31_bwd(sv["c0"], P["cv_w"], dc1, T)
    dzp, dzgl = rowwise_bwd("glu_bwd", f_glu, [(z, 512, 2), (z, 512, 3)], [], [(dc0, 512, 0)], [bf16] * 2, T, ROW_TILE)
    G["w_a_out"] = mm_tn("dw_a_out", sv["ya_in"], dya, n_slab=CHUNK)
    dzu, dzv_a, G["sg_ln_g"], G["sg_ln_b"], G["sg_w"], dsbt = rowwise_bwd(
        "sgu_bwd", f_sgu, [(z, 512, 0), (z, 512, 1)],
        [(P["sg_ln_g"], (1, 512), None), (P["sg_ln_b"], (1, 512), None), (P["sg_w"], (8, CHUNK, CHUNK), None),
         (P["sg_b_t"], (CHUNK, 8), None)],
        [(dya_in, 512, 0)], [bf16] * 2, T, CHUNK)
    G["sg_b"] = dsbt.T
    dz = concat_cols("dz_concat", [dzu, dzv_a, dzp, dzgl, dzq, dzk, dzv, dg0, dg1, dg2], T)
    G["w_in"] = mm_tn("dw_in", dz, sv["h1"], tk=1664)
    dh1 = mm_nn("d_h1", dz, P["w_in_t"], f32, tm=512)
    dx0, G["ln1_g"] = rowwise_bwd("rms_bwd", f_rms, [(sv["x0"], D, 0)], [(P["ln1_g"], (1, D), None)], [(dh1, D, 0)],
                                  [f32], T, ROW_TILE, adds=[(dx1, D, 0)])
    return dx0, G, received


def kernel(x, ln1_g, w_in, b_gate, sg_ln_g, sg_ln_b, sg_w, sg_b, w_a_out, cv_w, cv_b, cv_ln_g, cv_ln_b, w_b_out, q_norm_g, k_norm_g, w_c_out, w_out, ln2_g, w_up, ffn_conv_w, ffn_conv_b, w_down, loss_target, m_ln1_g, m_w_in, m_b_gate, m_sg_ln_g, m_sg_ln_b, m_sg_w, m_sg_b, m_w_a_out, m_cv_w, m_cv_b, m_cv_ln_g, m_cv_ln_b, m_w_b_out, m_q_norm_g, m_k_norm_g, m_w_c_out, m_w_out, m_ln2_g, m_w_up, m_ffn_conv_w, m_ffn_conv_b, m_w_down, v_ln1_g, v_w_in, v_b_gate, v_sg_ln_g, v_sg_ln_b, v_sg_w, v_sg_b, v_w_a_out, v_cv_w, v_cv_b, v_cv_ln_g, v_cv_ln_b, v_w_b_out, v_q_norm_g, v_k_norm_g, v_w_c_out, v_w_out, v_ln2_g, v_w_up, v_ffn_conv_w, v_ffn_conv_b, v_w_down):
    W = dict(ln1_g=ln1_g, w_in=w_in, b_gate=b_gate, sg_ln_g=sg_ln_g, sg_ln_b=sg_ln_b, sg_w=sg_w, sg_b=sg_b, w_a_out=w_a_out,
             cv_w=cv_w, cv_b=cv_b, cv_ln_g=cv_ln_g, cv_ln_b=cv_ln_b, w_b_out=w_b_out, q_norm_g=q_norm_g, k_norm_g=k_norm_g,
             w_c_out=w_c_out, w_out=w_out, ln2_g=ln2_g, w_up=w_up, ffn_conv_w=ffn_conv_w, ffn_conv_b=ffn_conv_b, w_down=w_down)
    M = dict(ln1_g=m_ln1_g, w_in=m_w_in, b_gate=m_b_gate, sg_ln_g=m_sg_ln_g, sg_ln_b=m_sg_ln_b, sg_w=m_sg_w, sg_b=m_sg_b,
             w_a_out=m_w_a_out, cv_w=m_cv_w, cv_b=m_cv_b, cv_ln_g=m_cv_ln_g, cv_ln_b=m_cv_ln_b, w_b_out=m_w_b_out,
             q_norm_g=m_q_norm_g, k_norm_g=m_k_norm_g, w_c_out=m_w_c_out, w_out=m_w_out, ln2_g=m_ln2_g, w_up=m_w_up,
             ffn_conv_w=m_ffn_conv_w, ffn_conv_b=m_ffn_conv_b, w_down=m_w_down)
    V = dict(ln1_g=v_ln1_g, w_in=v_w_in, b_gate=v_b_gate, sg_ln_g=v_sg_ln_g, sg_ln_b=v_sg_ln_b, sg_w=v_sg_w, sg_b=v_sg_b,
             w_a_out=v_w_a_out, cv_w=v_cv_w, cv_b=v_cv_b, cv_ln_g=v_cv_ln_g, cv_ln_b=v_cv_ln_b, w_b_out=v_w_b_out,
             q_norm_g=v_q_norm_g, k_norm_g=v_k_norm_g, w_c_out=v_w_c_out, w_out=v_w_out, ln2_g=v_ln2_g, w_up=v_w_up,
             ffn_conv_w=v_ffn_conv_w, ffn_conv_b=v_ffn_conv_b, w_down=v_w_down)
    T, D = x.shape[1], x.shape[2]
    L = DEPTH
    xs = x.reshape(T, D)
    target = loss_target.reshape(T, D)

    ss_shapes = [W[n].shape for n in SMALL_SHARDED]
    ss_rows = _rows_for(ss_shapes, 8)
    (ss_all,), small_token = exchange("gather_small", [_pack([W[n] for n in SMALL_SHARDED], ss_rows)], ["gather"])
    full_small = {}
    pos = 0
    for n in SMALL_SHARDED:
        s = W[n].shape
        cnt = s[0] * s[1] * s[2]
        part = ss_all.reshape(N_DEV, -1)[:, pos:pos + cnt].reshape((N_DEV,) + s)
        full_small[n] = jnp.transpose(part, (1, 2, 0, 3)).reshape(s[0], s[1], N_DEV * s[2])
        pos += cnt

    params, saved = [], []
    cur = xs
    Wt = {n: (jnp.transpose(W[n], (0, 2, 1)) if n in TRANSPOSED else W[n]) for n in BIG}

    def local_slabs(l):
        return [Wt[n][l].astype(bf16) for n in BIG]

    def weights_of(names, slabs):
        out = {}
        for n, s in zip(names, slabs):
            if n in TRANSPOSED:
                out[n + "_t"] = s.reshape(-1, s.shape[-1])
            else:
                out[n] = assemble("assemble_" + n, s) if n in COL_SHARDED else s.reshape(-1, s.shape[-1])
        return out

    LATE = [n for n in BIG if n != "w_in"]
    def behind(token, blocks):
        return [blocks[0] + token[0, 0].astype(blocks[0].dtype)] + list(blocks[1:])

    first, token = gather2("gather_w_in", behind(small_token, [local_slabs(0)[0]]))
    early_part, late_part = {}, {}
    for l in range(L):
        if l > 0:
            early_part[l] = gather_start("gather_start_in_%d" % l, behind(token, local_slabs(l)[:1]))
            token = early_part[l][3]
        late_part[l] = gather_start("gather_start_rest_%d" % l, behind(token, local_slabs(l)[1:]))
        token = late_part[l][3]
    start_token = token[0, 0]

    def late_weights(l):
        send_sems, recv_sems, bufs, _ = late_part[l]
        return lambda o: weights_of(LATE, gather_wait("gather_wait_rest_%d" % l, send_sems, recv_sems, bufs, o))

    for l in range(L):
        if l == 0:
            P = weights_of(["w_in"], first)
        else:
            send_sems, recv_sems, bufs, _ = early_part[l]
            P = weights_of(["w_in"], gather_wait("gather_wait_in_%d" % l, send_sems, recv_sems, bufs, cur))
        late = late_weights(l)
        for n in REPL:
            P[n] = W[n][l]
        for n in ("ln1_g", "sg_ln_g", "sg_ln_b", "cv_b", "cv_ln_g", "cv_ln_b", "q_norm_g", "k_norm_g", "ln2_g", "ffn_conv_b"):
            P[n] = P[n].reshape(1, -1)
        P["sg_b_t"] = P["sg_b"].T
        P["cv_w"] = full_small["cv_w"][l]
        P["b_gate"] = [full_small["b_gate"][l][k:k + 1] for k in range(3)]
        P["ffn_conv_w"] = [full_small["ffn_conv_w"][l][k:k + 1] for k in range(3)]
        if l == 0:
            P["ln1_g"] = P["ln1_g"] + start_token
        cur, sv = _forward_layer(cur, P, T, late=late)
        params.append(P)
        saved.append(sv)

    dy, loss_row = loss_head(cur, target, T)
    loss = lax.psum(loss_row[0, 0], ("x", "y", "c"))

    repl_shapes = [W[n].shape for n in REPL]
    repl_rows = _rows_for([s[1:] for s in repl_shapes], 16)
    ssl_rows = _rows_for([s[1:] for s in ss_shapes], 16)
    state = {"repl": [_pack_layers([X[n] for n in REPL], repl_rows) for X in (W, M, V)],
             "ss": [_pack_layers([X[n] for n in SMALL_SHARDED], ssl_rows) for X in (W, M, V)]}
    WIDE = [n for n in BIG if n not in COL_SHARDED]
    big_rows, big_off = {}, {}
    for n in WIDE:
        big_off[n] = sum(big_rows.values())
        big_rows[n] = W[n].shape[1] * W[n].shape[2] // D
        state[n] = [(jnp.transpose(X[n], (0, 2, 1)) if n in TRANSPOSED else X[n]).reshape(L, big_rows[n], D)
                    for X in (W, M, V)]
    for n in COL_SHARDED:
        state[n] = [W[n], M[n], V[n]]
    done = {n: None for n in state}

    def update(layer, received):
        for n in WIDE:
            done[n] = adamw("adamw_" + n, *state[n], received[0], layer, done[n], row_off=big_off[n])
        done["repl"] = adamw("adamw_repl", *state["repl"], received[1], layer, done["repl"])
        done["ss"] = adamw("adamw_ss", *state["ss"], received[2], layer, done["ss"])
        for k, n in enumerate(COL_SHARDED):
            done[n] = adamw("adamw_" + n, *state[n], received[3], layer, done[n], row_off=k * W[n].shape[1])

    def reduce_in_chip(G):
        big = [G[n].reshape(4, 2, big_rows[n], D) for n in WIDE]
        narrow = jnp.concatenate([G[n].reshape((4, 2) + G[n].shape[1:]) for n in COL_SHARDED], axis=2)
        ss_parts = []
        for n in SMALL_SHARDED:
            k, c = W[n].shape[1:]
            ss_parts.append(jnp.transpose(G[n].reshape(k, N_DEV, c), (1, 0, 2)).reshape(N_DEV, k * c))
        ss_send = jnp.concatenate(ss_parts, axis=1)
        ss_send = jnp.pad(ss_send, ((0, 0), (0, ssl_rows * 128 - ss_send.shape[1]))).reshape(4, 2, ssl_rows, 128)
        repl = _pack([G[n] for n in REPL], repl_rows)
        core = lax.axis_index("c")
        ss_kept = lax.dynamic_index_in_dim(ss_send, core, 1, keepdims=False)
        narrow_kept = lax.dynamic_index_in_dim(narrow, core, 1, keepdims=False)
        got, repl_got, ss_got, narrow_got = pair_exchange("pair_exchange", big, [repl, ss_send, narrow],
                                                          ["gather", "scatter", "scatter"])
        return [pair_add_wide("pair_add_big", big, got), add2("pair_add_repl", repl, repl_got), add2("pair_add_ss", ss_kept, ss_got),
                add2("pair_add_narrow", narrow_kept, narrow_got)]

    chip_modes = ["scatter", "gather", "scatter", "scatter"]
    dcur = dy
    pending = None
    for l in reversed(range(L)):
        P = params[l]
        if pending is not None:
            P = dict(P, ffn_conv_b=P["ffn_conv_b"] + pending[4][0, 0])
        dcur, G, _ = _backward_layer(dcur, P, saved[l], T)
        started = chip_start("chip_start_%d" % l, reduce_in_chip(G), chip_modes)
        if pending is not None:
            after = started[4] + dcur[0:8, 0:128]
            update(l + 1, chip_wait("chip_wait_%d" % (l + 1), *pending[:4], chip_modes, after))
        pending = started
    after = sum(done[n][0][1, 0:1, 0:1] for n in state)
    update(0, chip_wait("chip_wait_0", *pending[:4], chip_modes, after))

    results = [{}, {}, {}, {}]
    for k in range(4):
        for n, a in zip(REPL, _unpack_layers(done["repl"][k], repl_shapes)):
            results[k][n] = a
        for n, a in zip(SMALL_SHARDED, _unpack_layers(done["ss"][k], ss_shapes)):
            results[k][n] = a
        for n in BIG:
            results[k][n] = jnp.transpose(done[n][k], (0, 2, 1)) if n in TRANSPOSED else done[n][k].reshape(W[n].shape)
    out = [loss, dcur.reshape(1, T, D)]
    for k in range(4):
        out += [results[k][n] for n in W_NAMES]
    return tuple(out)
```

```python
import functools

import jax
import jax.numpy as jnp
from jax import lax
from jax.experimental import pallas as pl
from jax.experimental.pallas import tpu as pltpu

f32 = jnp.float32
bf16 = jnp.bfloat16

EPS = 1e-6
N_DEV = 8
DEPTH = 4
CHUNK = 128
HEAD_DIM = 64
N_HEADS = 8
CV_KERNEL = 31
VMEM_LIMIT_BYTES = 56 * 2 ** 20

ADAM_LR = 0.001
ADAM_B1 = 0.9
ADAM_B2 = 0.999
ADAM_EPS = 1e-08
ADAM_WD = 0.01
ADAM_STEP = 10

MESH = pl.DeviceIdType.MESH


def _cparams(n_grid):
    return pltpu.CompilerParams(dimension_semantics=("arbitrary",) * n_grid, vmem_limit_bytes=VMEM_LIMIT_BYTES)


def exchange(name, arrays, modes):
    n = len(arrays)

    def body(*refs):
        copies = _direct_copies(refs[:n], refs[n:2 * n], modes, *refs[2 * n + 1:])
        for cp in copies:
            cp.start()
        for cp in copies:
            cp.wait()
        refs[2 * n][...] = jnp.zeros_like(refs[2 * n])

    res = pl.pallas_call(
        body, name=name, out_shape=_exchange_out_shapes(arrays, modes) + [jax.ShapeDtypeStruct((8, 128), f32)],
        in_specs=[_ANY] * n, out_specs=[_ANY] * n + [pl.BlockSpec(memory_space=pltpu.VMEM)],
        scratch_shapes=_exchange_sems(n),
    )(*arrays)
    return list(res[:n]), res[n]


_ANY = pl.BlockSpec(memory_space=pl.ANY)


def _exchange_out_shapes(arrays, modes):
    return [jax.ShapeDtypeStruct((N_DEV,) + tuple(a.shape) if m == "gather" else tuple(a.shape), a.dtype)
            for a, m in zip(arrays, modes)]


def _exchange_sems(n):
    return [pltpu.SemaphoreType.DMA((n, N_DEV - 1)), pltpu.SemaphoreType.DMA((n, N_DEV - 1)), pltpu.SemaphoreType.DMA((n,))]


def _direct_copies(ins, outs, modes, send_sems, recv_sems, local_sems):
    x, y, c = lax.axis_index("x"), lax.axis_index("y"), lax.axis_index("c")
    me = 4 * x + 2 * y + c
    copies = []
    for k in range(len(ins)):
        src_mine = ins[k] if modes[k] == "gather" else ins[k].at[me]
        copies.append(pltpu.make_async_copy(src_mine, outs[k].at[me], local_sems.at[k]))
        for r in range(1, N_DEV):
            px = 1 - x if r & 4 else x
            py = 1 - y if r & 2 else y
            pc = 1 - c if r & 1 else c
            src = ins[k] if modes[k] == "gather" else ins[k].at[4 * px + 2 * py + pc]
            copies.append(pltpu.make_async_remote_copy(
                src_ref=src, dst_ref=outs[k].at[me], send_sem=send_sems.at[k, r - 1], recv_sem=recv_sems.at[k, r - 1],
                device_id=(px, py, pc), device_id_type=MESH))
    return copies


def own_slot_buffer(block, n_slots, index):
    return lax.dynamic_update_slice(lax.empty((n_slots,) + tuple(block.shape), block.dtype), block[None],
                                    (index,) + (0,) * block.ndim)


def _sems(n, m):
    return [pltpu.SemaphoreType.DMA((n, m)), pltpu.SemaphoreType.DMA((n, m))]


def _two_level_gather(bufs, send_sems, recv_sems):
    x, y, c = lax.axis_index("x"), lax.axis_index("y"), lax.axis_index("c")
    me = 4 * x + 2 * y + c
    sibling = (x, y, 1 - c)
    chips = [(1 - x, y), (x, 1 - y), (1 - x, 1 - y)]

    def slot(px, py, pc):
        return 4 * px + 2 * py + pc

    def copy(k, sem, block, to):
        rows = bufs[k].at[block]
        return pltpu.make_async_remote_copy(src_ref=rows, dst_ref=rows, send_sem=send_sems.at[k, sem],
                                            recv_sem=recv_sems.at[k, sem], device_id=to, device_id_type=MESH)

    def first(k):
        return [copy(k, 0, me, sibling)] + [copy(k, 1 + j, me, (*chip, c)) for j, chip in enumerate(chips)]

    def passed_on(k, j):
        return copy(k, 4 + j, slot(*chips[j], c), sibling)

    def start():
        for k in range(len(bufs)):
            for cp in first(k):
                cp.start()

    def forward():
        for k in range(len(bufs)):
            for j in range(3):
                copy(k, 1 + j, slot(*chips[j], c), sibling).wait_recv()
                passed_on(k, j).start()

    def finish():
        for k in range(len(bufs)):
            copy(k, 0, slot(x, y, 1 - c), sibling).wait_recv()
            for j, chip in enumerate(chips):
                copy(k, 4 + j, slot(*chip, 1 - c), sibling).wait_recv()
            for cp in first(k) + [passed_on(k, j) for j in range(3)]:
                cp.wait_send()

    return start, forward, finish


def pair_exchange(name, big, smalls, modes):
    nb, ns = len(big), len(smalls)
    rows = [a.shape[2] for a in big]
    total, width = sum(rows), big[0].shape[3]
    n = nb + ns

    def body(*refs):
        ins, outs = refs[:n], refs[n:n + 1 + ns]
        send_sems, recv_sems = refs[-2:]
        x, y, c = lax.axis_index("x"), lax.axis_index("y"), lax.axis_index("c")
        copies = []

        def remote(k, src, dst):
            copies.append(pltpu.make_async_remote_copy(src_ref=src, dst_ref=dst, send_sem=send_sems.at[k, 0],
                                                       recv_sem=recv_sems.at[k, 0], device_id=(x, y, 1 - c),
                                                       device_id_type=MESH))

        off = 0
        for k in range(nb):
            remote(k, ins[k].at[:, 1 - c], outs[0].at[:, pl.ds(off, rows[k]), :])
            off += rows[k]
        for j in range(ns):
            remote(nb + j, ins[nb + j].at[:, 1 - c] if modes[j] == "scatter" else ins[nb + j], outs[1 + j])
        for cp in copies:
            cp.start()
        for cp in copies:
            cp.wait()

    out_shape = [jax.ShapeDtypeStruct((4, total, width), big[0].dtype)]
    for a, m in zip(smalls, modes):
        out_shape.append(jax.ShapeDtypeStruct((4,) + tuple(a.shape[2:]) if m == "scatter" else tuple(a.shape), a.dtype))
    return pl.pallas_call(
        body, name=name, out_shape=out_shape, in_specs=[_ANY] * n, out_specs=[_ANY] * len(out_shape),
        scratch_shapes=_sems(n, 1),
    )(*big, *smalls)


def add2(name, a, b):
    shape = a.shape
    a2, b2 = a.reshape(-1, shape[-1]), b.reshape(-1, shape[-1])
    R, C = a2.shape
    tr = _row_tile(R, C, max_elems=1200 * 1024)

    def body(a_ref, b_ref, o_ref):
        o_ref[...] = (a_ref[...].astype(f32) + b_ref[...].astype(f32)).astype(o_ref.dtype)

    spec = pl.BlockSpec((tr, C), lambda i: (i, 0))
    return pl.pallas_call(
        body, name=name, grid=(R // tr,), in_specs=[spec, spec], out_specs=spec,
        out_shape=jax.ShapeDtypeStruct((R, C), a.dtype), compiler_params=_cparams(1),
    )(a2, b2).reshape(shape)


def pair_add_wide(name, big, got):
    nb = len(big)
    rows = [a.shape[2] for a in big]
    total, width = sum(rows), big[0].shape[3]
    core = lax.axis_index("c").astype(jnp.int32).reshape(1)

    def body(core_ref, *refs):
        got_ref, o_ref = refs[nb], refs[nb + 1]
        off = 0
        for k in range(nb):
            part = slice(off, off + rows[k])
            o_ref[part, :] = (refs[k][...].astype(f32) + got_ref[part, :].astype(f32)).astype(o_ref.dtype)
            off += rows[k]

    whole = pl.BlockSpec((None, total, width), lambda i, core_ref: (i, 0, 0))
    grid_spec = pltpu.PrefetchScalarGridSpec(
        num_scalar_prefetch=1, grid=(4,),
        in_specs=[pl.BlockSpec((None, None, r, width), lambda i, core_ref: (i, core_ref[0], 0, 0)) for r in rows] + [whole],
        out_specs=whole)
    return pl.pallas_call(
        body, name=name, grid_spec=grid_spec, out_shape=jax.ShapeDtypeStruct((4, total, width), got.dtype),
        compiler_params=_cparams(1),
    )(core, *big, got)


def chip_buffers(arrays, modes):
    mine = 2 * lax.axis_index("x") + lax.axis_index("y")
    return [own_slot_buffer(lax.dynamic_index_in_dim(a, mine, 0, keepdims=False) if m == "scatter" else a, 4, mine)
            for a, m in zip(arrays, modes)]


def _chip_copies(ins, bufs, modes, send_sems, recv_sems):
    x, y, c = lax.axis_index("x"), lax.axis_index("y"), lax.axis_index("c")
    mine = 2 * x + y
    copies = []
    for k in range(len(ins)):
        for r in range(1, 4):
            px = 1 - x if r & 2 else x
            py = 1 - y if r & 1 else y
            src = ins[k].at[2 * px + py] if modes[k] == "scatter" else ins[k]
            copies.append(pltpu.make_async_remote_copy(
                src_ref=src, dst_ref=bufs[k].at[mine], send_sem=send_sems.at[k, r - 1], recv_sem=recv_sems.at[k, r - 1],
                device_id=(px, py, c), device_id_type=MESH))
    return copies


def chip_exchange(name, arrays, modes):
    n = len(arrays)

    def body(*refs):
        copies = _chip_copies(refs[:n], refs[2 * n:3 * n], modes, *refs[3 * n:])
        for cp in copies:
            cp.start()
        for cp in copies:
            cp.wait()

    bufs = chip_buffers(arrays, modes)
    return pl.pallas_call(
        body, name=name, out_shape=[jax.ShapeDtypeStruct(b.shape, b.dtype) for b in bufs],
        in_specs=[_ANY] * (2 * n), out_specs=[_ANY] * n, input_output_aliases={n + k: k for k in range(n)},
        scratch_shapes=_sems(n, 3),
    )(*arrays, *bufs)


def gather_buffers(blocks):
    me = 4 * lax.axis_index("x") + 2 * lax.axis_index("y") + lax.axis_index("c")
    return [own_slot_buffer(b, N_DEV, me) for b in blocks]


_HBM = pl.BlockSpec(memory_space=pltpu.HBM)
_SEM = pl.BlockSpec(memory_space=pltpu.SEMAPHORE)
_SPLIT_PARAMS = pltpu.CompilerParams(has_side_effects=pltpu.SideEffectType.DATAFLOW_SIDE_EFFECTING)


def _split_gather_copies(bufs, send_sems, recv_sems):
    x, y, c = lax.axis_index("x"), lax.axis_index("y"), lax.axis_index("c")
    me = 4 * x + 2 * y + c
    copies = []
    for k in range(len(bufs)):
        rows = bufs[k].at[me]
        for r in range(1, N_DEV):
            peer = (1 - x if r & 4 else x, 1 - y if r & 2 else y, 1 - c if r & 1 else c)
            copies.append(pltpu.make_async_remote_copy(
                src_ref=rows, dst_ref=rows, send_sem=send_sems.at[(N_DEV - 1) * k + r - 1],
                recv_sem=recv_sems.at[(N_DEV - 1) * k + r - 1], device_id=peer, device_id_type=MESH))
    return copies


def gather_start(name, blocks):
    n = len(blocks)
    bufs = gather_buffers(blocks)

    def body(*refs):
        for cp in _split_gather_copies(refs[n + 2:2 * n + 2], refs[n], refs[n + 1]):
            cp.start()
        refs[2 * n + 2][...] = jnp.zeros_like(refs[2 * n + 2])

    n_sem = n * (N_DEV - 1)
    res = pl.pallas_call(
        body, name=name,
        out_shape=(pltpu.SemaphoreType.DMA((n_sem,)), pltpu.SemaphoreType.DMA((n_sem,)),
                   *[pltpu.HBM(b.shape, b.dtype) for b in bufs], jax.ShapeDtypeStruct((8, 128), f32)),
        in_specs=[_HBM] * n, out_specs=(_SEM, _SEM, *[_HBM] * n, pl.BlockSpec(memory_space=pltpu.VMEM)),
        input_output_aliases={k: 2 + k for k in range(n)}, compiler_params=_SPLIT_PARAMS,
    )(*[pltpu.with_memory_space_constraint(b, pltpu.HBM) for b in bufs])
    return res[0], res[1], list(res[2:2 + n]), res[2 + n]


def gather_wait(name, send_sems, recv_sems, bufs, after):
    n = len(bufs)

    def body(*refs):
        for cp in _split_gather_copies(refs[:n], refs[n], refs[n + 1]):
            cp.wait_send()
            cp.wait_recv()

    return list(pl.pallas_call(
        body, name=name, out_shape=tuple(pltpu.HBM(b.shape, b.dtype) for b in bufs),
        in_specs=[_HBM] * n + [_SEM, _SEM, _ANY], out_specs=tuple([_HBM] * n),
        input_output_aliases={k: k for k in range(n)}, compiler_params=_SPLIT_PARAMS,
    )(*bufs, send_sems, recv_sems, after))


def _split_chip_copies(srcs, bufs, modes, send_sems, recv_sems):
    x, y, c = lax.axis_index("x"), lax.axis_index("y"), lax.axis_index("c")
    mine = 2 * x + y
    copies = []
    for k in range(len(srcs)):
        for r in range(1, 4):
            px = 1 - x if r & 2 else x
            py = 1 - y if r & 1 else y
            src = srcs[k].at[2 * px + py] if modes[k] == "scatter" else srcs[k]
            copies.append(pltpu.make_async_remote_copy(
                src_ref=src, dst_ref=bufs[k].at[mine], send_sem=send_sems.at[3 * k + r - 1],
                recv_sem=recv_sems.at[3 * k + r - 1], device_id=(px, py, c), device_id_type=MESH))
    return copies


def chip_start(name, arrays, modes):
    n = len(arrays)
    bufs = chip_buffers(arrays, modes)

    def body(*refs):
        for cp in _split_chip_copies(refs[2 * n + 2:3 * n + 2], refs[3 * n + 2:4 * n + 2], modes, refs[2 * n], refs[2 * n + 1]):
            cp.start()
        refs[4 * n + 2][...] = jnp.zeros_like(refs[4 * n + 2])

    both = list(arrays) + list(bufs)
    res = pl.pallas_call(
        body, name=name,
        out_shape=(pltpu.SemaphoreType.DMA((3 * n,)), pltpu.SemaphoreType.DMA((3 * n,)),
                   *[pltpu.HBM(b.shape, b.dtype) for b in both], jax.ShapeDtypeStruct((8, 128), f32)),
        in_specs=[_HBM] * (2 * n), out_specs=(_SEM, _SEM, *[_HBM] * (2 * n), pl.BlockSpec(memory_space=pltpu.VMEM)),
        input_output_aliases={k: 2 + k for k in range(2 * n)}, compiler_params=_SPLIT_PARAMS,
    )(*[pltpu.with_memory_space_constraint(b, pltpu.HBM) for b in both])
    return res[0], res[1], list(res[2:2 + n]), list(res[2 + n:2 + 2 * n]), res[2 + 2 * n]


def chip_wait(name, send_sems, recv_sems, srcs, bufs, modes, after):
    n = len(bufs)

    def body(*refs):
        for cp in _split_chip_copies(refs[:n], refs[n:2 * n], modes, refs[2 * n], refs[2 * n + 1]):
            cp.wait_send()
            cp.wait_recv()

    both = list(srcs) + list(bufs)
    res = pl.pallas_call(
        body, name=name, out_shape=tuple(pltpu.HBM(b.shape, b.dtype) for b in both),
        in_specs=[_HBM] * (2 * n) + [_SEM, _SEM, _ANY], out_specs=tuple([_HBM] * (2 * n)),
        input_output_aliases={k: k for k in range(2 * n)}, compiler_params=_SPLIT_PARAMS,
    )(*both, send_sems, recv_sems, after)
    return list(res[n:])


def gather2(name, arrays):
    n = len(arrays)

    def body(*refs):
        start, forward, finish = _two_level_gather(refs[n:2 * n], *refs[2 * n + 1:])
        start()
        forward()
        finish()
        refs[2 * n][...] = jnp.zeros_like(refs[2 * n])

    bufs = gather_buffers(arrays)
    res = pl.pallas_call(
        body, name=name, out_shape=[jax.ShapeDtypeStruct(b.shape, b.dtype) for b in bufs] + [jax.ShapeDtypeStruct((8, 128), f32)],
        in_specs=[_ANY] * n, out_specs=[_ANY] * n + [pl.BlockSpec(memory_space=pltpu.VMEM)],
        input_output_aliases={k: k for k in range(n)}, scratch_shapes=_sems(n, N_DEV - 1),
    )(*bufs)
    return list(res[:n]), res[n]


def assemble(name, slabs):
    _, K, Ns = slabs.shape
    g = N_DEV if Ns % 128 == 0 else 2

    def body(w_ref, o_ref):
        o_ref[...] = jnp.concatenate([w_ref[s] for s in range(g)], axis=1)

    return pl.pallas_call(
        body, name=name, grid=(N_DEV // g,),
        in_specs=[pl.BlockSpec((g, K, Ns), lambda m: (m, 0, 0))],
        out_specs=pl.BlockSpec((K, g * Ns), lambda m: (0, m)),
        out_shape=jax.ShapeDtypeStruct((K, N_DEV * Ns), slabs.dtype),
        compiler_params=_cparams(1),
    )(slabs)


def mm_nn(name, a, b, out_dtype, add=None, tm=1024, tn=512, tk=None):
    M, K = a.shape
    N = b.shape[1]
    tm, tn = min(tm, M), min(tn, N)
    tk = K if tk is None else tk
    nk = K // tk
    has_add = add is not None

    def body(a_ref, b_ref, *rest):
        o_ref = rest[1] if has_add else rest[0]
        part = jnp.dot(a_ref[...].astype(bf16), b_ref[...].astype(bf16), preferred_element_type=f32)
        if nk == 1:
            o_ref[...] = (part + rest[0][...] if has_add else part).astype(o_ref.dtype)
            return
        acc_ref = rest[-1]
        k = pl.program_id(2)

        @pl.when(k == 0)
        def _():
            acc_ref[...] = part + rest[0][...] if has_add else part

        @pl.when(k > 0)
        def _():
            acc_ref[...] += part

        @pl.when(k == nk - 1)
        def _():
            o_ref[...] = acc_ref[...].astype(o_ref.dtype)

    in_specs = [pl.BlockSpec((tm, tk), lambda i, j, k: (i, k)), pl.BlockSpec((tk, tn), lambda i, j, k: (k, j))]
    ops = [a, b]
    if has_add:
        in_specs.append(pl.BlockSpec((tm, tn), lambda i, j, k: (i, j)))
        ops.append(add)
    return pl.pallas_call(
        body, name=name, grid=(M // tm, N // tn, nk), in_specs=in_specs,
        out_specs=pl.BlockSpec((tm, tn), lambda i, j, k: (i, j)),
        out_shape=jax.ShapeDtypeStruct((M, N), out_dtype),
        scratch_shapes=[pltpu.VMEM((tm, tn), f32)] if nk > 1 else [], compiler_params=_cparams(3),
    )(*ops)


def mm_many(name, xs, ws, out_dtype, transpose_w=False, tm=512):
    n = len(xs)
    M = xs[0].shape[0]
    tm = min(tm, M)
    dims = NT_DIMS if transpose_w else (((1,), (0,)), ((), ()))

    def body(*refs):
        for k in range(n):
            refs[2 * n + k][...] = lax.dot_general(refs[k][...].astype(bf16), refs[n + k][...].astype(bf16), dims,
                                                   preferred_element_type=f32).astype(out_dtype)

    n_out = [w.shape[0] if transpose_w else w.shape[1] for w in ws]
    return pl.pallas_call(
        body, name=name, grid=(M // tm,),
        in_specs=[pl.BlockSpec((tm, x.shape[1]), lambda i: (i, 0)) for x in xs]
        + [pl.BlockSpec(w.shape, lambda i: (0, 0)) for w in ws],
        out_specs=[pl.BlockSpec((tm, c), lambda i: (i, 0)) for c in n_out],
        out_shape=[jax.ShapeDtypeStruct((M, c), out_dtype) for c in n_out], compiler_params=_cparams(1),
    )(*xs, *ws)


def concat_cols(name, pieces, T, tm=512):
    widths = [p.shape[1] for p in pieces]
    total = sum(widths)
    tm = min(tm, T)

    def body(*refs):
        o_ref = refs[-1]
        off = 0
        for r, w in zip(refs[:-1], widths):
            o_ref[:, off:off + w] = r[...]
            off += w

    return pl.pallas_call(
        body, name=name, grid=(T // tm,),
        in_specs=[pl.BlockSpec((tm, w), lambda i: (i, 0)) for w in widths],
        out_specs=pl.BlockSpec((tm, total), lambda i: (i, 0)),
        out_shape=jax.ShapeDtypeStruct((T, total), pieces[0].dtype), compiler_params=_cparams(1),
    )(*pieces)


def mm_nt(name, a, b, out_dtype, tm=1024, tko=None, tc=None):
    M, C = a.shape
    Ko = b.shape[0]
    tm = min(tm, M)
    tc = C if tc is None else min(tc, C)
    tko = Ko if tko is None else tko
    nc = C // tc

    def body(a_ref, b_ref, o_ref, *scratch):
        part = lax.dot_general(a_ref[...].astype(bf16), b_ref[...].astype(bf16), (((1,), (1,)), ((), ())),
                               preferred_element_type=f32)
        if nc == 1:
            o_ref[...] = part.astype(o_ref.dtype)
            return
        acc_ref = scratch[0]
        c = pl.program_id(2)

        @pl.when(c == 0)
        def _():
            acc_ref[...] = part

        @pl.when(c > 0)
        def _():
            acc_ref[...] += part

        @pl.when(c == nc - 1)
        def _():
            o_ref[...] = acc_ref[...].astype(o_ref.dtype)

    return pl.pallas_call(
        body, name=name, grid=(M // tm, Ko // tko, nc),
        in_specs=[pl.BlockSpec((tm, tc), lambda i, j, c: (i, c)), pl.BlockSpec((tko, tc), lambda i, j, c: (j, c))],
        out_specs=pl.BlockSpec((tm, tko), lambda i, j, c: (i, j)),
        out_shape=jax.ShapeDtypeStruct((M, Ko), out_dtype),
        scratch_shapes=[pltpu.VMEM((tm, tko), f32)] if nc > 1 else [], compiler_params=_cparams(3),
    )(a, b)


def mm_tn(name, a, b, n_slab=None, tk=512, tn=512):
    T, K = a.shape
    N = b.shape[1]
    tk = min(tk, K)
    if n_slab is None:
        tn = min(tn, N)

        def body(a_ref, b_ref, o_ref):
            o_ref[...] = lax.dot_general(a_ref[...].astype(bf16), b_ref[...].astype(bf16), (((0,), (0,)), ((), ())),
                                         preferred_element_type=f32).astype(o_ref.dtype)

        return pl.pallas_call(
            body, name=name, grid=(N // tn, K // tk),
            in_specs=[pl.BlockSpec((T, tk), lambda j, i: (0, i)), pl.BlockSpec((T, tn), lambda j, i: (0, j))],
            out_specs=pl.BlockSpec((tk, tn), lambda j, i: (i, j)),
            out_shape=jax.ShapeDtypeStruct((K, N), bf16), compiler_params=_cparams(2),
        )(a, b)

    Ns = n_slab
    g = N_DEV if Ns % 128 == 0 else 2
    tn = g * Ns

    def body(a_ref, b_ref, o_ref):
        val = lax.dot_general(a_ref[...].astype(bf16), b_ref[...].astype(bf16), (((0,), (0,)), ((), ())),
                              preferred_element_type=f32)
        for s in range(g):
            o_ref[s] = val[:, s * Ns:(s + 1) * Ns].astype(o_ref.dtype)

    return pl.pallas_call(
        body, name=name, grid=(K // tk, N_DEV // g),
        in_specs=[pl.BlockSpec((T, tk), lambda i, j: (0, i)), pl.BlockSpec((T, tn), lambda i, j: (0, j))],
        out_specs=pl.BlockSpec((g, tk, Ns), lambda i, j: (j, i, 0)),
        out_shape=jax.ShapeDtypeStruct((N_DEV, K, Ns), bf16), compiler_params=_cparams(2),
    )(a, b)


def _tile_spec(w, tm, cb):
    if callable(cb):
        return pl.BlockSpec((tm, w), lambda c, i: (i, cb(c)))
    return pl.BlockSpec((tm, w), lambda c, i: (i, cb))


def _param_spec(block, cb):
    nd = len(block)
    if cb is None:
        return pl.BlockSpec(block, lambda c, i: (0,) * nd)
    return pl.BlockSpec(block, lambda c, i: (0,) * (nd - 1) + (cb(c),))


ROW_TILE = 512


def rowwise(name, fn, tiled, params, outs, T, tm, ncol=1):
    tm = min(tm, T)
    n_in = len(tiled) + len(params)
    n_t = len(tiled)

    def body(*refs):
        ins = [r[...].astype(f32) for r in refs[:n_t]] + [r[...] for r in refs[n_t:n_in]]
        res = fn(*ins)
        for r, o in zip(refs[n_in:], res):
            r[...] = o.astype(r.dtype)

    return pl.pallas_call(
        body, name=name, grid=(ncol, T // tm),
        in_specs=[_tile_spec(w, tm, cb) for _, w, cb in tiled] + [_param_spec(blk, cb) for _, blk, cb in params],
        out_specs=[_tile_spec(w, tm, cb) for _, w, cb, _ in outs],
        out_shape=[jax.ShapeDtypeStruct((T, cols), dt) for cols, _, _, dt in outs],
        compiler_params=_cparams(2),
    )(*[a for a, _, _ in tiled], *[a for a, _, _ in params])


def rowwise_bwd(name, fn, tiled, params, cts, grads, T, tm, ncol=1, adds=None):
    tm = min(tm, T)
    n_t, n_p, n_c = len(tiled), len(params), len(cts)
    adds = adds or [None] * n_t
    add_list = [(k, a) for k, a in enumerate(adds) if a is not None]
    want = [k for k, g in enumerate(grads) if g is not None]
    n_a = len(add_list)

    def body(*refs):
        pos = 0
        t_refs = refs[pos:pos + n_t]; pos += n_t
        p_refs = refs[pos:pos + n_p]; pos += n_p
        c_refs = refs[pos:pos + n_c]; pos += n_c
        a_refs = refs[pos:pos + n_a]; pos += n_a
        g_refs = refs[pos:pos + len(want)]; pos += len(want)
        pg_refs = refs[pos:pos + n_p]
        primals = [r[...].astype(f32) for r in t_refs] + [r[...] for r in p_refs]
        _, vjp = jax.vjp(fn, *primals)
        g = vjp(tuple(r[...].astype(f32) for r in c_refs))
        add_of = {k: a_refs[n][...] for n, (k, _) in enumerate(add_list)}
        for n, k in enumerate(want):
            val = g[k]
            if k in add_of:
                val = val + add_of[k]
            g_refs[n][...] = val.astype(g_refs[n].dtype)
        i = pl.program_id(1)
        for k in range(n_p):
            @pl.when(i == 0)
            def _(k=k):
                pg_refs[k][...] = g[n_t + k]

            @pl.when(i > 0)
            def _(k=k):
                pg_refs[k][...] += g[n_t + k]

    in_specs = ([_tile_spec(w, tm, cb) for _, w, cb in tiled] + [_param_spec(blk, cb) for _, blk, cb in params]
                + [_tile_spec(w, tm, cb) for _, w, cb in cts] + [_tile_spec(w, tm, cb) for _, (_, w, cb) in add_list])
    ops = ([a for a, _, _ in tiled] + [a for a, _, _ in params] + [a for a, _, _ in cts]
           + [a for _, (a, _, _) in add_list])
    out_specs, out_shape = [], []
    for k in want:
        w = tiled[k][1]
        out_specs.append(_tile_spec(w, tm, lambda c: c))
        out_shape.append(jax.ShapeDtypeStruct((T, ncol * w), grads[k]))
    for a, blk, cb in params:
        out_specs.append(_param_spec(blk, cb))
        out_shape.append(jax.ShapeDtypeStruct(a.shape, f32))
    return pl.pallas_call(
        body, name=name, grid=(ncol, T // tm), in_specs=in_specs, out_specs=out_specs, out_shape=out_shape,
        compiler_params=_cparams(2),
    )(*ops)


@jax.custom_vjp
def _bdot(a, b):
    return jnp.dot(a.astype(bf16), b.astype(bf16), preferred_element_type=f32)


def _bdot_fwd(a, b):
    return _bdot(a, b), (a, b)


def _bdot_bwd(res, ct):
    a, b = res
    ctb = ct.astype(bf16)
    da = lax.dot_general(ctb, b.astype(bf16), (((1,), (1,)), ((), ())), preferred_element_type=f32)
    db = lax.dot_general(a.astype(bf16), ctb, (((0,), (0,)), ((), ())), preferred_element_type=f32)
    return da, db


_bdot.defvjp(_bdot_fwd, _bdot_bwd)


def _layer_norm(x, g, b):
    mu = jnp.mean(x, axis=-1, keepdims=True)
    xc = x - mu
    y = xc * lax.rsqrt(jnp.mean(xc * xc, axis=-1, keepdims=True) + EPS)
    return y * g + b


def f_rms(x, g):
    y = x * lax.rsqrt(jnp.mean(x * x, axis=-1, keepdims=True) + EPS)
    return (y * g,)


def f_sgu(zu, zv, ln_g, ln_b, wm, sgb_t):
    u = jax.nn.gelu(zu)
    vn = _layer_norm(jax.nn.gelu(zv), ln_g, ln_b)
    row = lax.broadcasted_iota(jnp.int32, (CHUNK, CHUNK), 0)
    col = lax.broadcasted_iota(jnp.int32, (CHUNK, CHUNK), 1)
    tril = col <= row
    low = col < HEAD_DIM
    parts = []
    for p in range(4):
        vp = vn[:, CHUNK * p:CHUNK * (p + 1)]
        w0 = jnp.where(tril, wm[2 * p], 0.0)
        w1 = jnp.where(tril, wm[2 * p + 1], 0.0)
        parts.append(_bdot(w0, jnp.where(low, vp, 0.0)) + _bdot(w1, jnp.where(low, 0.0, vp)))
    s = jnp.concatenate(parts, axis=1)
    lane_g = lax.shift_right_logical(lax.broadcasted_iota(jnp.int32, s.shape, 1), 6)
    bias = jnp.zeros_like(s)
    for g in range(8):
        bias = jnp.where(lane_g == g, sgb_t[:, g:g + 1], bias)
    return (u * (s + bias),)


def f_glu(p, gl):
    return (p * jax.nn.sigmoid(gl),)


def f_lnsilu(c1, g, b):
    return (jax.nn.silu(_layer_norm(c1, g, b)),)


def _head_norm(x, g64):
    g = jnp.concatenate([g64] * N_HEADS, axis=1)
    lane_h = lax.shift_right_logical(lax.broadcasted_iota(jnp.int32, x.shape, 1), 6)
    x2 = x * x
    r = jnp.zeros_like(x)
    for h in range(N_HEADS):
        mh = lane_h == h
        ms = jnp.sum(jnp.where(mh, x2, 0.0), axis=-1, keepdims=True) * (1.0 / HEAD_DIM)
        r = jnp.where(mh, lax.rsqrt(ms + EPS), r)
    return (x * r) * g


def f_qkv(zq, zk, zv, qg, kg):
    return (_head_norm(zq, qg) * 0.125, _head_norm(zk, kg), zv)


def f_merge(g0, g1, g2, ya, yb, yc, b0, b1, b2):
    return (jax.nn.sigmoid(g0 + b0) * ya + jax.nn.sigmoid(g1 + b1) * yb + jax.nn.sigmoid(g2 + b2) * yc,)


HALO = 32


def conv31_fwd(x, w, b, T, tt=256):
    C = x.shape[1]
    r = tt // HALO

    def body(x_ref, h_ref, w_ref, b_ref, y_ref, buf):
        i = pl.program_id(0)
        halo = h_ref[...]
        buf[0:HALO, :] = jnp.where(i > 0, halo, jnp.zeros_like(halo))
        buf[HALO:HALO + tt, :] = x_ref[...]
        acc = jnp.zeros((tt, C), f32) + b_ref[...]
        for k in range(CV_KERNEL):
            acc = acc + w_ref[k:k + 1, :] * buf[pl.ds(HALO - (CV_KERNEL - 1) + k, tt), :]
        y_ref[...] = acc

    return pl.pallas_call(
        body, name="conv31_fwd", grid=(T // tt,),
        in_specs=[pl.BlockSpec((tt, C), lambda i: (i, 0)),
                  pl.BlockSpec((HALO, C), lambda i: (jnp.maximum(i * r - 1, 0), 0)),
                  pl.BlockSpec((CV_KERNEL, C), lambda i: (0, 0)), pl.BlockSpec((1, C), lambda i: (0, 0))],
        out_specs=pl.BlockSpec((tt, C), lambda i: (i, 0)),
        out_shape=jax.ShapeDtypeStruct((T, C), f32),
        scratch_shapes=[pltpu.VMEM((HALO + tt, C), f32)], compiler_params=_cparams(1),
    )(x, x, w, b)


def conv31_bwd(x, w, dy, T, tt=256):
    C = x.shape[1]
    r = tt // HALO
    n = T // tt

    def body(x_ref, h_ref, w_ref, dy_ref, dyn_ref, dx_ref, dw_ref, db_ref, xbuf, dbuf):
        i = pl.program_id(0)
        halo = h_ref[...]
        xbuf[0:HALO, :] = jnp.where(i > 0, halo, jnp.zeros_like(halo))
        xbuf[HALO:HALO + tt, :] = x_ref[...]
        nxt = dyn_ref[...]
        dy = dy_ref[...]
        dbuf[0:tt, :] = dy
        dbuf[tt:tt + HALO, :] = jnp.where(i < n - 1, nxt, jnp.zeros_like(nxt))

        @pl.when(i == 0)
        def _():
            dw_ref[...] = jnp.zeros_like(dw_ref)
            db_ref[...] = jnp.zeros_like(db_ref)

        acc = jnp.zeros((tt, C), f32)
        for k in range(CV_KERNEL):
            acc = acc + w_ref[k:k + 1, :] * dbuf[pl.ds(CV_KERNEL - 1 - k, tt), :]
            xs = xbuf[pl.ds(HALO - (CV_KERNEL - 1) + k, tt), :]
            dw_ref[k:k + 1, :] += jnp.sum(dy * xs, axis=0, keepdims=True)
        dx_ref[...] = acc
        db_ref[...] += jnp.sum(dy, axis=0, keepdims=True)

    return pl.pallas_call(
        body, name="conv31_bwd", grid=(n,),
        in_specs=[pl.BlockSpec((tt, C), lambda i: (i, 0)),
                  pl.BlockSpec((HALO, C), lambda i: (jnp.maximum(i * r - 1, 0), 0)),
                  pl.BlockSpec((CV_KERNEL, C), lambda i: (0, 0)),
                  pl.BlockSpec((tt, C), lambda i: (i, 0)),
                  pl.BlockSpec((HALO, C), lambda i: (jnp.minimum((i + 1) * r, n * r - 1), 0))],
        out_specs=[pl.BlockSpec((tt, C), lambda i: (i, 0)), pl.BlockSpec((CV_KERNEL, C), lambda i: (0, 0)),
                   pl.BlockSpec((1, C), lambda i: (0, 0))],
        out_shape=[jax.ShapeDtypeStruct((T, C), f32), jax.ShapeDtypeStruct((CV_KERNEL, C), f32),
                   jax.ShapeDtypeStruct((1, C), f32)],
        scratch_shapes=[pltpu.VMEM((HALO + tt, C), f32), pltpu.VMEM((HALO + tt, C), f32)], compiler_params=_cparams(1),
    )(x, x, w, dy, dy)


FFN_TC = 128
FFN_PAD = 8


def ffn_act_fwd(up, cw, cb, T):
    F = up.shape[1] // 2
    nj = F // FFN_TC
    rc = min(128, T)

    def body(g_ref, v_ref, g0, g1, g2, gb, v0, v1, v2, vb, o_ref, gp, vp):
        zeros = jnp.zeros((FFN_PAD, FFN_TC), f32)
        for p, x_ref in ((gp, g_ref), (vp, v_ref)):
            p[0:FFN_PAD, :] = zeros
            p[FFN_PAD:FFN_PAD + T, :] = x_ref[...].astype(f32)
        wg = (g0[...], g1[...], g2[...], gb[...])
        wv = (v0[...], v1[...], v2[...], vb[...])

        def conv(p, w, r):
            return (w[0] * p[pl.ds(FFN_PAD + r - 2, rc), :] + w[1] * p[pl.ds(FFN_PAD + r - 1, rc), :]
                    + w[2] * p[pl.ds(FFN_PAD + r, rc), :] + w[3])

        for r in range(0, T, rc):
            o_ref[pl.ds(r, rc), :] = (jax.nn.silu(conv(gp, wg, r)) * conv(vp, wv, r)).astype(o_ref.dtype)

    gspec = pl.BlockSpec((T, FFN_TC), lambda j: (0, j))
    vspec = pl.BlockSpec((T, FFN_TC), lambda j: (0, j + nj))
    pg = pl.BlockSpec((1, FFN_TC), lambda j: (0, j))
    pv = pl.BlockSpec((1, FFN_TC), lambda j: (0, j + nj))
    return pl.pallas_call(
        body, name="ffn_act_fwd", grid=(nj,),
        in_specs=[gspec, vspec, pg, pg, pg, pg, pv, pv, pv, pv], out_specs=gspec,
        out_shape=jax.ShapeDtypeStruct((T, F), bf16),
        scratch_shapes=[pltpu.VMEM((FFN_PAD + T, FFN_TC), f32)] * 2, compiler_params=_cparams(1),
    )(up, up, cw[0], cw[1], cw[2], cb, cw[0], cw[1], cw[2], cb)


def ffn_act_bwd(up, cw, cb, dact, T):
    F = up.shape[1] // 2
    nj = F // FFN_TC

    rc = min(128, T)
    ext = rc + FFN_PAD

    def body(g_ref, v_ref, g0, g1, g2, gb, v0, v1, v2, vb, d_ref, dupg_ref, dupv_ref, *rest):
        p_refs, (gp, vp, dp, dgs, dvs) = rest[:8], rest[8:]
        zeros = jnp.zeros((FFN_PAD, FFN_TC), f32)
        for p, x_ref in ((gp, g_ref), (vp, v_ref)):
            p[0:FFN_PAD, :] = zeros
            p[FFN_PAD:FFN_PAD + T, :] = x_ref[...].astype(f32)
            p[FFN_PAD + T:FFN_PAD + T + FFN_PAD, :] = zeros
        dp[0:T, :] = d_ref[...]
        dp[T:T + FFN_PAD, :] = zeros
        wg = (g0[...], g1[...], g2[...], gb[...])
        wv = (v0[...], v1[...], v2[...], vb[...])
        acc = [jnp.zeros((1, FFN_TC), f32) for _ in range(8)]

        def taps(p, r):
            return tuple(p[pl.ds(FFN_PAD + r - s, ext), :] for s in (2, 1, 0))

        for r in range(0, T, rc):
            xg, xv = taps(gp, r), taps(vp, r)
            gc = wg[0] * xg[0] + wg[1] * xg[1] + wg[2] * xg[2] + wg[3]
            vc = wv[0] * xv[0] + wv[1] * xv[1] + wv[2] * xv[2] + wv[3]
            d = dp[pl.ds(r, ext), :]
            sg = jax.nn.sigmoid(gc)
            sides = ((d * vc * (sg * (1.0 + gc * (1.0 - sg))), xg, wg, dgs, dupg_ref), (d * (gc * sg), xv, wv, dvs, dupv_ref))
            for side, (dc, x, w, buf, dup_ref) in enumerate(sides):
                buf[...] = dc
                dc0 = dc[0:rc]
                dup = w[2] * dc0 + w[1] * buf[pl.ds(1, rc), :] + w[0] * buf[pl.ds(2, rc), :]
                dup_ref[pl.ds(r, rc), :] = dup.astype(dup_ref.dtype)
                for k in range(3):
                    acc[4 * side + k] = acc[4 * side + k] + jnp.sum(dc0 * x[k][0:rc], axis=0, keepdims=True)
                acc[4 * side + 3] = acc[4 * side + 3] + jnp.sum(dc0, axis=0, keepdims=True)
        for k in range(8):
            p_refs[k][...] = acc[k]

    gspec = pl.BlockSpec((T, FFN_TC), lambda j: (0, j))
    vspec = pl.BlockSpec((T, FFN_TC), lambda j: (0, j + nj))
    pg = pl.BlockSpec((1, FFN_TC), lambda j: (0, j))
    pv = pl.BlockSpec((1, FFN_TC), lambda j: (0, j + nj))
    res = pl.pallas_call(
        body, name="ffn_act_bwd", grid=(nj,),
        in_specs=[gspec, vspec, pg, pg, pg, pg, pv, pv, pv, pv, gspec],
        out_specs=[gspec, gspec] + [pg] * 8,
        out_shape=[jax.ShapeDtypeStruct((T, F), bf16)] * 2 + [jax.ShapeDtypeStruct((1, F), f32)] * 8,
        scratch_shapes=[pltpu.VMEM((FFN_PAD + T + FFN_PAD, FFN_TC), f32)] * 2 + [pltpu.VMEM((T + FFN_PAD, FFN_TC), f32)]
        + [pltpu.VMEM((ext, FFN_TC), f32)] * 2,
        compiler_params=_cparams(1),
    )(up, up, cw[0], cw[1], cw[2], cb, cw[0], cw[1], cw[2], cb, dact)
    dup = concat_cols("dup_concat", [res[0], res[1]], T)
    return (dup,) + tuple(jnp.concatenate([res[2 + k], res[6 + k]], axis=1) for k in range(4))


BQ = 256
BK = 256
assert BQ == BK
NT_DIMS = (((1,), (1,)), ((), ()))
TN_DIMS = (((0,), (0,)), ((), ()))


def _split_dot(x, u):
    x1 = x.astype(bf16)
    x2 = (x - x1.astype(f32)).astype(bf16)
    n = x.shape[0]
    y = jnp.dot(jnp.concatenate([x1, x2], axis=0), u, preferred_element_type=f32)
    return y[0:n] + y[n:2 * n]


def _log_sigmoids(z):
    sp = jnp.log(1.0 + jnp.exp(-jnp.abs(z)))
    lsp = jnp.minimum(z, 0.0) - sp
    return lsp, lsp - z


def _stack_heads(x):
    head1 = lax.broadcasted_iota(jnp.int32, x.shape, 1) >= HEAD_DIM
    zero = jnp.zeros_like(x)
    return jnp.concatenate([jnp.where(head1, zero, x), jnp.where(head1, x, zero)], axis=0)


def _unstack_heads(y):
    head1 = lax.broadcasted_iota(jnp.int32, (BQ, y.shape[1]), 1) >= HEAD_DIM
    return jnp.where(head1, y[BQ:2 * BQ], y[0:BQ])


def _attn_masks():
    row = lax.broadcasted_iota(jnp.int32, (2 * BQ, BK), 0)
    col = lax.broadcasted_iota(jnp.int32, (2 * BQ, BK), 1)
    ur = lax.broadcasted_iota(jnp.int32, (BK, BK), 0)
    uc = lax.broadcasted_iota(jnp.int32, (BK, BK), 1)
    return (row & (BQ - 1)) - col, (ur > uc).astype(bf16), (ur < uc).astype(bf16)


def attn_fwd(q, k, v, T, gather=None):
    nq = T // BQ
    n_g = 0 if gather is None else len(gather)

    def body(*refs):
        q_ref, k_ref, v_ref = refs[:3]
        o_ref, tot_ref = refs[3 + n_g:5 + n_g]
        p, i = pl.program_id(0), pl.program_id(1)
        if n_g:
            start, forward, finish = _two_level_gather(refs[5 + n_g:5 + 2 * n_g], *refs[5 + 2 * n_g:])
            pl.when(jnp.logical_and(p == 0, i == 0))(start)
        qs = _stack_heads(q_ref[...])
        diff, u_after, _ = _attn_masks()

        def step(jb, carry, diagonal):
            acc, c = carry
            ks = pl.multiple_of(jb * BK, BK)
            kb = k_ref[pl.ds(ks, BK), :]
            vb = v_ref[pl.ds(ks, BK), :]
            z = lax.dot_general(qs, kb, NT_DIMS, preferred_element_type=f32)
            lsp, lm = _log_sigmoids(z)
            if diagonal:
                m = diff > 0
                lm = jnp.where(m, lm, 0.0)
            a = jnp.exp(lsp + _split_dot(lm, u_after))
            if diagonal:
                a = jnp.where(m, a, 0.0)
            acc = acc + jnp.exp(c) * jnp.dot(a.astype(bf16), vb, preferred_element_type=f32)
            return acc, c + jnp.sum(lm, axis=-1, keepdims=True)

        carry = step(i, (jnp.zeros((2 * BQ, 128), f32), jnp.zeros((2 * BQ, 1), f32)), True)
        acc, c = lax.fori_loop(0, i, lambda t, cr: step(i - 1 - t, cr, False), carry)
        o_ref[...] = _unstack_heads(acc).astype(o_ref.dtype)
        tot_ref[...] = _unstack_heads(jnp.broadcast_to(c, (2 * BQ, 128)))
        if n_g:
            @pl.when(jnp.logical_and(p == 3, i == nq - 1))
            def _():
                forward()
                finish()

    blk = pl.BlockSpec((BQ, 128), lambda p, i: (i, p))
    full = pl.BlockSpec((T, 128), lambda p, i: (0, p))
    bufs = [] if gather is None else gather_buffers(gather)
    res = pl.pallas_call(
        body, name="attn_fwd_gather" if n_g else "attn_fwd", grid=(4, nq),
        in_specs=[blk, full, full] + [_ANY] * n_g, out_specs=[blk, blk] + [_ANY] * n_g,
        out_shape=[jax.ShapeDtypeStruct((T, 512), bf16), jax.ShapeDtypeStruct((T, 512), f32)]
        + [jax.ShapeDtypeStruct(b.shape, b.dtype) for b in bufs],
        input_output_aliases={3 + k: 2 + k for k in range(n_g)},
        scratch_shapes=_sems(n_g, N_DEV - 1) if n_g else [], compiler_params=_cparams(2),
    )(q, k, v, *bufs)
    return res[0], res[1], list(res[2:])


def attn_bwd(q, k, v, do, tot, T, scatter=None):
    nq = T // BQ
    n_s = 0 if scatter is None else len(scatter[0])
    modes = [] if scatter is None else list(scatter[1])

    def body(*refs):
        q_ref, k_ref, v_ref, do_ref, tot_ref = refs[:5]
        dq_ref, dk_ref, dv_ref = refs[5 + 2 * n_s:8 + 2 * n_s]
        p, i = pl.program_id(0), pl.program_id(1)
        if n_s:
            def copies():
                return _chip_copies(refs[5:5 + n_s], refs[8 + 2 * n_s:8 + 3 * n_s], modes, *refs[8 + 3 * n_s:])

            @pl.when(jnp.logical_and(p == 0, i == 0))
            def _():
                for cp in copies():
                    cp.start()

        @pl.when(i == 0)
        def _():
            dk_ref[...] = jnp.zeros_like(dk_ref)
            dv_ref[...] = jnp.zeros_like(dv_ref)

        qs = _stack_heads(q_ref[...])
        dos = _stack_heads(do_ref[...].astype(bf16))
        totv = tot_ref[...]
        tots = jnp.concatenate([totv[:, 0:1], totv[:, HEAD_DIM:HEAD_DIM + 1]], axis=0)
        diff, u_after, u_before = _attn_masks()

        def step(jb, carry, diagonal):
            dq, cl, cg = carry
            ks = pl.multiple_of(jb * BK, BK)
            kb = k_ref[pl.ds(ks, BK), :]
            vb = v_ref[pl.ds(ks, BK), :]
            z = lax.dot_general(qs, kb, NT_DIMS, preferred_element_type=f32)
            lsp, lsn = _log_sigmoids(z)
            lm = lsn
            if diagonal:
                m = diff > 0
                lm = jnp.where(m, lsn, 0.0)
            a = jnp.exp(lsp + _split_dot(lm, u_after))
            if diagonal:
                a = jnp.where(m, a, 0.0)
            g = a * lax.dot_general(dos, vb, NT_DIMS, preferred_element_type=f32)
            bb = jnp.exp(lsp)
            aa = g * jnp.exp(lsn) - _split_dot(g, u_before) * bb
            if diagonal:
                aa = jnp.where(m, aa, 0.0)
                bb = jnp.where(m, bb, 0.0)
            cl = cl + jnp.sum(lm, axis=-1, keepdims=True)
            f = jnp.exp(tots - cl)
            dz = (f * aa - cg * bb).astype(bf16)
            cg = cg + f * jnp.sum(g, axis=-1, keepdims=True)
            dq = dq + jnp.dot(dz, kb, preferred_element_type=f32)
            dk_ref[pl.ds(ks, BK), :] += lax.dot_general(dz, qs, TN_DIMS, preferred_element_type=f32)
            dv_ref[pl.ds(ks, BK), :] += lax.dot_general((f * a).astype(bf16), dos, TN_DIMS, preferred_element_type=f32)
            return dq, cl, cg

        zc = jnp.zeros((2 * BQ, 1), f32)
        carry = lax.fori_loop(0, i, lambda jb, cr: step(jb, cr, False), (jnp.zeros((2 * BQ, 128), f32), zc, zc))
        dq, _, _ = step(i, carry, True)
        dq_ref[...] = _unstack_heads(dq)
        if n_s:
            @pl.when(jnp.logical_and(p == 3, i == nq - 1))
            def _():
                for cp in copies():
                    cp.wait()

    blk = pl.BlockSpec((BQ, 128), lambda p, i: (i, p))
    full = pl.BlockSpec((T, 128), lambda p, i: (0, p))
    shp = jax.ShapeDtypeStruct((T, 512), f32)
    s_list = [] if scatter is None else list(scatter[0])
    bufs = chip_buffers(s_list, modes) if n_s else []
    res = pl.pallas_call(
        body, name="attn_bwd_scatter" if n_s else "attn_bwd", grid=(4, nq),
        in_specs=[blk, full, full, blk, blk] + [_ANY] * (2 * n_s), out_specs=[blk, full, full] + [_ANY] * n_s,
        out_shape=[shp, shp, shp] + [jax.ShapeDtypeStruct(b.shape, b.dtype) for b in bufs],
        input_output_aliases={5 + n_s + k: 3 + k for k in range(n_s)},
        scratch_shapes=_sems(n_s, 3) if n_s else [], compiler_params=_cparams(2),
    )(q, k, v, do, tot, *s_list, *bufs)
    return res[0], res[1], res[2], list(res[3:])


def loss_head(y, target, T, tm=256):
    D = y.shape[1]

    def body(y_ref, t_ref, dy_ref, l_ref):
        i = pl.program_id(0)
        err = y_ref[...] - t_ref[...]
        dy_ref[...] = err * (1.0 / D)
        part = 0.5 * jnp.sum(jnp.sum(err * err, axis=-1, keepdims=True) * (1.0 / D), axis=0, keepdims=True)

        @pl.when(i == 0)
        def _():
            l_ref[...] = jnp.zeros_like(l_ref)

        l_ref[...] += jnp.broadcast_to(part, l_ref.shape)

    spec = pl.BlockSpec((tm, D), lambda i: (i, 0))
    return pl.pallas_call(
        body, name="loss_head", grid=(T // tm,), in_specs=[spec, spec],
        out_specs=[spec, pl.BlockSpec((1, 128), lambda i: (0, 0))],
        out_shape=[jax.ShapeDtypeStruct((T, D), f32), jax.ShapeDtypeStruct((1, 128), f32)],
        compiler_params=_cparams(1),
    )(y, target)


def _row_tile(rows, cols, offset=0, max_elems=128 * 1024):
    best = None
    for t in range(16, rows + 1, 16):
        if rows % t == 0 and offset % t == 0 and t * cols <= max_elems:
            best = t
    return best if best is not None else rows


def adamw(name, w, m, v, parts, layer, prev=None, row_off=0):
    L, R, C = w.shape
    n_parts = parts.shape[0]
    tr = _row_tile(R, C, row_off, max_elems=256 * 1024)
    assert R % tr == 0 and row_off % tr == 0, (name, R, row_off, tr)

    def body(w_ref, m_ref, v_ref, p_ref, *rest):
        g_ref, d_ref, nm_ref, nv_ref = rest[-4:]
        g = p_ref[0].astype(f32)
        for s in range(1, n_parts):
            g = g + p_ref[s].astype(f32)
        wv = w_ref[...]
        mn = ADAM_B1 * m_ref[...] + (1.0 - ADAM_B1) * g
        vn = ADAM_B2 * v_ref[...] + (1.0 - ADAM_B2) * jnp.square(g)
        m_hat = mn / (1.0 - ADAM_B1 ** ADAM_STEP)
        v_hat = vn / (1.0 - ADAM_B2 ** ADAM_STEP)
        g_ref[...] = g
        d_ref[...] = -ADAM_LR * (m_hat / (jnp.sqrt(v_hat) + ADAM_EPS) + ADAM_WD * wv)
        nm_ref[...] = mn
        nv_ref[...] = vn

    spec = pl.BlockSpec((None, tr, C), lambda i: (layer, i, 0))
    shp = jax.ShapeDtypeStruct((L, R, C), f32)
    n_prev = 0 if prev is None else 4
    return pl.pallas_call(
        body, name=name, grid=(R // tr,),
        in_specs=[spec, spec, spec, pl.BlockSpec((n_parts, tr, C), lambda i: (0, row_off // tr + i, 0))] + [_ANY] * n_prev,
        out_specs=[spec] * 4, out_shape=[shp] * 4, input_output_aliases={4 + k: k for k in range(n_prev)},
        compiler_params=_cparams(1),
    )(w, m, v, parts, *(prev or ()))


REPL = ["ln1_g", "sg_ln_g", "sg_ln_b", "sg_w", "sg_b", "cv_b", "cv_ln_g", "cv_ln_b", "q_norm_g", "k_norm_g", "ln2_g",
        "ffn_conv_b"]
SMALL_SHARDED = ["b_gate", "cv_w", "ffn_conv_w"]
BIG = ["w_in", "w_a_out", "w_b_out", "w_c_out", "w_up", "w_out", "w_down"]
COL_SHARDED = ["w_a_out", "w_b_out", "w_c_out"]
TRANSPOSED = ["w_in", "w_up"]


def _pack(arrs, rows):
    flat = jnp.concatenate([a.reshape(-1) for a in arrs])
    return jnp.pad(flat, (0, rows * 128 - flat.shape[0])).reshape(rows, 128)


def _pack_layers(arrs, rows):
    flat = jnp.concatenate([a.reshape(a.shape[0], -1) for a in arrs], axis=1)
    return jnp.pad(flat, ((0, 0), (0, rows * 128 - flat.shape[1]))).reshape(flat.shape[0], rows, 128)


def _unpack_layers(packed, shapes):
    flat = packed.reshape(packed.shape[0], -1)
    out, pos = [], 0
    for s in shapes:
        n = 1
        for d in s[1:]:
            n *= d
        out.append(flat[:, pos:pos + n].reshape(s))
        pos += n
    return out


def _rows_for(shapes, mult):
    n = 0
    for s in shapes:
        k = 1
        for d in s:
            k *= d
        n += k
    rows = -(-n // 128)
    return -(-rows // mult) * mult


W_NAMES = ['ln1_g', 'w_in', 'b_gate', 'sg_ln_g', 'sg_ln_b', 'sg_w', 'sg_b', 'w_a_out', 'cv_w', 'cv_b', 'cv_ln_g', 'cv_ln_b',
           'w_b_out', 'q_norm_g', 'k_norm_g', 'w_c_out', 'w_out', 'ln2_g', 'w_up', 'ffn_conv_w', 'ffn_conv_b', 'w_down']


def _forward_layer(x, P, T, late=None):
    D = x.shape[1]
    sv = {"x0": x}
    (h1,) = rowwise("rms_fwd", f_rms, [(x, D, 0)], [(P["ln1_g"], (1, D), None)], [(D, D, 0, bf16)], T, ROW_TILE)
    z = mm_nt("in_proj", h1, P["w_in_t"], bf16, tm=512, tko=3328, tc=D)
    sv["h1"], sv["z"] = h1, z
    (ya_in,) = rowwise("sgu_fwd", f_sgu, [(z, 512, 0), (z, 512, 1)],
                       [(P["sg_ln_g"], (1, 512), None), (P["sg_ln_b"], (1, 512), None),
                        (P["sg_w"], (8, CHUNK, CHUNK), None), (P["sg_b_t"], (CHUNK, 8), None)],
                       [(512, 512, 0, bf16)], T, CHUNK)
    (c0,) = rowwise("glu_fwd", f_glu, [(z, 512, 2), (z, 512, 3)], [], [(512, 512, 0, f32)], T, ROW_TILE)
    c1 = conv31_fwd(c0, P["cv_w"], P["cv_b"], T)
    (c3,) = rowwise("lnsilu_fwd", f_lnsilu, [(c1, 512, 0)], [(P["cv_ln_g"], (1, 512), None), (P["cv_ln_b"], (1, 512), None)],
                    [(512, 512, 0, bf16)], T, ROW_TILE)
    q8, kn, vb = rowwise("qkv_fwd", f_qkv, [(z, 512, 4), (z, 512, 5), (z, 512, 6)],
                         [(P["q_norm_g"], (1, HEAD_DIM), None), (P["k_norm_g"], (1, HEAD_DIM), None)],
                         [(512, 512, 0, bf16)] * 3, T, ROW_TILE)
    o, tot, _ = attn_fwd(q8, kn, vb, T)
    if late is not None:
        P.update(late(o))
    ya, yb, yc = mm_many("branch_out", [ya_in, c3, o], [P["w_a_out"], P["w_b_out"], P["w_c_out"]], bf16)
    (merged,) = rowwise("merge_fwd", f_merge,
                        [(z, 512, lambda c: 7 + c), (z, 512, lambda c: 9 + c), (z, 512, lambda c: 11 + c),
                         (ya, 512, lambda c: c), (yb, 512, lambda c: c), (yc, 512, lambda c: c)],
                        [(P["b_gate"][k], (1, 512), lambda c: c) for k in range(3)],
                        [(D, 512, lambda c: c, bf16)], T, 256, ncol=2)
    x1 = mm_nn("out_proj", merged, P["w_out"], f32, add=x)
    sv.update(ya_in=ya_in, ya=ya, c0=c0, c1=c1, c3=c3, yb=yb, q8=q8, kn=kn, vb=vb, o=o, tot=tot, yc=yc, merged=merged, x1=x1)
    (h2,) = rowwise("rms_fwd", f_rms, [(x1, D, 0)], [(P["ln2_g"], (1, D), None)], [(D, D, 0, bf16)], T, ROW_TILE)
    up = mm_nt("up_proj", h2, P["w_up_t"], bf16, tm=512, tko=2816, tc=D)
    act = ffn_act_fwd(up, P["ffn_conv_w"], P["ffn_conv_b"], T)
    x2 = mm_nn("down_proj", act, P["w_down"], f32, add=x1)
    sv.update(h2=h2, up=up, act=act)
    return x2, sv


def _backward_layer(dx2, P, sv, T, scatter=None):
    D = dx2.shape[1]
    G = {}
    G["w_down"] = mm_tn("dw_down", sv["act"], dx2, tk=1408)
    dact = mm_nt("d_act", dx2, P["w_down"], f32, tko=1408, tc=D)
    dup, dcw0, dcw1, dcw2, G["ffn_conv_b"] = ffn_act_bwd(sv["up"], P["ffn_conv_w"], P["ffn_conv_b"], dact, T)
    G["ffn_conv_w"] = jnp.concatenate([dcw0, dcw1, dcw2], axis=0)
    G["w_up"] = mm_tn("dw_up", dup, sv["h2"], tk=1408)
    dh2 = mm_nn("d_h2", dup, P["w_up_t"], f32, tm=512)
    dx1, G["ln2_g"] = rowwise_bwd("rms_bwd", f_rms, [(sv["x1"], D, 0)], [(P["ln2_g"], (1, D), None)], [(dh2, D, 0)],
                                  [f32], T, ROW_TILE, adds=[(dx2, D, 0)])
    G["w_out"] = mm_tn("dw_out", sv["merged"], dx1, tk=1024)
    dmerged = mm_nt("d_merged", dx1, P["w_out"], f32)
    z = sv["z"]
    dg0, dg1, dg2, dya, dyb, dyc, db0, db1, db2 = rowwise_bwd(
        "merge_bwd", f_merge,
        [(z, 512, lambda c: 7 + c), (z, 512, lambda c: 9 + c), (z, 512, lambda c: 11 + c),
         (sv["ya"], 512, lambda c: c), (sv["yb"], 512, lambda c: c), (sv["yc"], 512, lambda c: c)],
        [(P["b_gate"][k], (1, 512), lambda c: c) for k in range(3)],
        [(dmerged, 512, lambda c: c)], [bf16] * 6, T, 256, ncol=2)
    G["b_gate"] = jnp.concatenate([db0, db1, db2], axis=0)
    dya_in, dc3, do = mm_many("d_branch_in", [dya, dyb, dyc], [P["w_a_out"], P["w_b_out"], P["w_c_out"]], f32,
                              transpose_w=True)
    G["w_c_out"] = mm_tn("dw_c_out", sv["o"], dyc, n_slab=CHUNK)
    dq8, dkn, dvb, received = attn_bwd(sv["q8"], sv["kn"], sv["vb"], do, sv["tot"], T, scatter=scatter)
    dzq, dzk, dzv, G["q_norm_g"], G["k_norm_g"] = rowwise_bwd(
        "qkv_bwd", f_qkv, [(z, 512, 4), (z, 512, 5), (z, 512, 6)],
        [(P["q_norm_g"], (1, HEAD_DIM), None), (P["k_norm_g"], (1, HEAD_DIM), None)],
        [(dq8, 512, 0), (dkn, 512, 0), (dvb, 512, 0)], [bf16] * 3, T, ROW_TILE)
    G["w_b_out"] = mm_tn("dw_b_out", sv["c3"], dyb, n_slab=CHUNK)
    dc1, G["cv_ln_g"], G["cv_ln_b"] = rowwise_bwd(
        "lnsilu_bwd", f_lnsilu, [(sv["c1"], 512, 0)], [(P["cv_ln_g"], (1, 512), None), (P["cv_ln_b"], (1, 512), None)],
        [(dc3, 512, 0)], [f32], T, ROW_TILE)
    dc0, G["cv_w"], G["cv_b"] = conv31_bwd(sv["c0"], P["cv_w"], dc1, T)
    dzp, dzgl = rowwise_bwd("glu_bwd", f_glu, [(z, 512, 2), (z, 512, 3)], [], [(dc0, 512, 0)], [bf16] * 2, T, ROW_TILE)
    G["w_a_out"] = mm_tn("dw_a_out", sv["ya_in"], dya, n_slab=CHUNK)
    dzu, dzv_a, G["sg_ln_g"], G["sg_ln_b"], G["sg_w"], dsbt = rowwise_bwd(
        "sgu_bwd", f_sgu, [(z, 512, 0), (z, 512, 1)],
        [(P["sg_ln_g"], (1, 512), None), (P["sg_ln_b"], (1, 512), None), (P["sg_w"], (8, CHUNK, CHUNK), None),
         (P["sg_b_t"], (CHUNK, 8), None)],
        [(dya_in, 512, 0)], [bf16] * 2, T, CHUNK)
    G["sg_b"] = dsbt.T
    dz = concat_cols("dz_concat", [dzu, dzv_a, dzp, dzgl, dzq, dzk, dzv, dg0, dg1, dg2], T)
    G["w_in"] = mm_tn("dw_in", dz, sv["h1"], tk=1664)
    dh1 = mm_nn("d_h1", dz, P["w_in_t"], f32, tm=512)
    dx0, G["ln1_g"] = rowwise_bwd("rms_bwd", f_rms, [(sv["x0"], D, 0)], [(P["ln1_g"], (1, D), None)], [(dh1, D, 0)],
                                  [f32], T, ROW_TILE, adds=[(dx1, D, 0)])
    return dx0, G, received


def kernel(x, ln1_g, w_in, b_gate, sg_ln_g, sg_ln_b, sg_w, sg_b, w_a_out, cv_w, cv_b, cv_ln_g, cv_ln_b, w_b_out, q_norm_g, k_norm_g, w_c_out, w_out, ln2_g, w_up, ffn_conv_w, ffn_conv_b, w_down, loss_target, m_ln1_g, m_w_in, m_b_gate, m_sg_ln_g, m_sg_ln_b, m_sg_w, m_sg_b, m_w_a_out, m_cv_w, m_cv_b, m_cv_ln_g, m_cv_ln_b, m_w_b_out, m_q_norm_g, m_k_norm_g, m_w_c_out, m_w_out, m_ln2_g, m_w_up, m_ffn_conv_w, m_ffn_conv_b, m_w_down, v_ln1_g, v_w_in, v_b_gate, v_sg_ln_g, v_sg_ln_b, v_sg_w, v_sg_b, v_w_a_out, v_cv_w, v_cv_b, v_cv_ln_g, v_cv_ln_b, v_w_b_out, v_q_norm_g, v_k_norm_g, v_w_c_out, v_w_out, v_ln2_g, v_w_up, v_ffn_conv_w, v_ffn_conv_b, v_w_down):
    W = dict(ln1_g=ln1_g, w_in=w_in, b_gate=b_gate, sg_ln_g=sg_ln_g, sg_ln_b=sg_ln_b, sg_w=sg_w, sg_b=sg_b, w_a_out=w_a_out,
             cv_w=cv_w, cv_b=cv_b, cv_ln_g=cv_ln_g, cv_ln_b=cv_ln_b, w_b_out=w_b_out, q_norm_g=q_norm_g, k_norm_g=k_norm_g,
             w_c_out=w_c_out, w_out=w_out, ln2_g=ln2_g, w_up=w_up, ffn_conv_w=ffn_conv_w, ffn_conv_b=ffn_conv_b, w_down=w_down)
    M = dict(ln1_g=m_ln1_g, w_in=m_w_in, b_gate=m_b_gate, sg_ln_g=m_sg_ln_g, sg_ln_b=m_sg_ln_b, sg_w=m_sg_w, sg_b=m_sg_b,
             w_a_out=m_w_a_out, cv_w=m_cv_w, cv_b=m_cv_b, cv_ln_g=m_cv_ln_g, cv_ln_b=m_cv_ln_b, w_b_out=m_w_b_out,
             q_norm_g=m_q_norm_g, k_norm_g=m_k_norm_g, w_c_out=m_w_c_out, w_out=m_w_out, ln2_g=m_ln2_g, w_up=m_w_up,
             ffn_conv_w=m_ffn_conv_w, ffn_conv_b=m_ffn_conv_b, w_down=m_w_down)
    V = dict(ln1_g=v_ln1_g, w_in=v_w_in, b_gate=v_b_gate, sg_ln_g=v_sg_ln_g, sg_ln_b=v_sg_ln_b, sg_w=v_sg_w, sg_b=v_sg_b,
             w_a_out=v_w_a_out, cv_w=v_cv_w, cv_b=v_cv_b, cv_ln_g=v_cv_ln_g, cv_ln_b=v_cv_ln_b, w_b_out=v_w_b_out,
             q_norm_g=v_q_norm_g, k_norm_g=v_k_norm_g, w_c_out=v_w_c_out, w_out=v_w_out, ln2_g=v_ln2_g, w_up=v_w_up,
             ffn_conv_w=v_ffn_conv_w, ffn_conv_b=v_ffn_conv_b, w_down=v_w_down)
    T, D = x.shape[1], x.shape[2]
    L = DEPTH
    xs = x.reshape(T, D)
    target = loss_target.reshape(T, D)

    ss_shapes = [W[n].shape for n in SMALL_SHARDED]
    ss_rows = _rows_for(ss_shapes, 8)
    (ss_all,), small_token = exchange("gather_small", [_pack([W[n] for n in SMALL_SHARDED], ss_rows)], ["gather"])
    full_small = {}
    pos = 0
    for n in SMALL_SHARDED:
        s = W[n].shape
        cnt = s[0] * s[1] * s[2]
        part = ss_all.reshape(N_DEV, -1)[:, pos:pos + cnt].reshape((N_DEV,) + s)
        full_small[n] = jnp.transpose(part, (1, 2, 0, 3)).reshape(s[0], s[1], N_DEV * s[2])
        pos += cnt

    params, saved = [], []
    cur = xs
    Wt = {n: (jnp.transpose(W[n], (0, 2, 1)) if n in TRANSPOSED else W[n]) for n in BIG}

    def local_slabs(l):
        return [Wt[n][l].astype(bf16) for n in BIG]

    def weights_of(names, slabs):
        out = {}
        for n, s in zip(names, slabs):
            if n in TRANSPOSED:
                out[n + "_t"] = s.reshape(-1, s.shape[-1])
            else:
                out[n] = assemble("assemble_" + n, s) if n in COL_SHARDED else s.reshape(-1, s.shape[-1])
        return out

    LATE = [n for n in BIG if n != "w_in"]
    def behind(token, blocks):
        return [blocks[0] + token[0, 0].astype(blocks[0].dtype)] + list(blocks[1:])

    first, token = gather2("gather_w_in", behind(small_token, [local_slabs(0)[0]]))
    early_part, late_part = {}, {}
    for l in range(L):
        if l > 0:
            early_part[l] = gather_start("gather_start_in_%d" % l, behind(token, local_slabs(l)[:1]))
            token = early_part[l][3]
        late_part[l] = gather_start("gather_start_rest_%d" % l, behind(token, local_slabs(l)[1:]))
        token = late_part[l][3]
    start_token = token[0, 0]

    def late_weights(l):
        send_sems, recv_sems, bufs, _ = late_part[l]
        return lambda o: weights_of(LATE, gather_wait("gather_wait_rest_%d" % l, send_sems, recv_sems, bufs, o))

    for l in range(L):
        if l == 0:
            P = weights_of(["w_in"], first)
        else:
            send_sems, recv_sems, bufs, _ = early_part[l]
            P = weights_of(["w_in"], gather_wait("gather_wait_in_%d" % l, send_sems, recv_sems, bufs, cur))
        late = late_weights(l)
        for n in REPL:
            P[n] = W[n][l]
        for n in ("ln1_g", "sg_ln_g", "sg_ln_b", "cv_b", "cv_ln_g", "cv_ln_b", "q_norm_g", "k_norm_g", "ln2_g", "ffn_conv_b"):
            P[n] = P[n].reshape(1, -1)
        P["sg_b_t"] = P["sg_b"].T
        P["cv_w"] = full_small["cv_w"][l]
        P["b_gate"] = [full_small["b_gate"][l][k:k + 1] for k in range(3)]
        P["ffn_conv_w"] = [full_small["ffn_conv_w"][l][k:k + 1] for k in range(3)]
        if l == 0:
            P["ln1_g"] = P["ln1_g"] + start_token
        cur, sv = _forward_layer(cur, P, T, late=late)
        params.append(P)
        saved.append(sv)

    dy, loss_row = loss_head(cur, target, T)
    loss = lax.psum(loss_row[0, 0], ("x", "y", "c"))

    repl_shapes = [W[n].shape for n in REPL]
    repl_rows = _rows_for([s[1:] for s in repl_shapes], 16)
    ssl_rows = _rows_for([s[1:] for s in ss_shapes], 16)
    state = {"repl": [_pack_layers([X[n] for n in REPL], repl_rows) for X in (W, M, V)],
             "ss": [_pack_layers([X[n] for n in SMALL_SHARDED], ssl_rows) for X in (W, M, V)]}
    WIDE = [n for n in BIG if n not in COL_SHARDED]
    big_rows, big_off = {}, {}
    for n in WIDE:
        big_off[n] = sum(big_rows.values())
        big_rows[n] = W[n].shape[1] * W[n].shape[2] // D
        state[n] = [(jnp.transpose(X[n], (0, 2, 1)) if n in TRANSPOSED else X[n]).reshape(L, big_rows[n], D)
                    for X in (W, M, V)]
    for n in COL_SHARDED:
        state[n] = [W[n], M[n], V[n]]
    done = {n: None for n in state}

    def update(layer, received):
        for n in WIDE:
            done[n] = adamw("adamw_" + n, *state[n], received[0], layer, done[n], row_off=big_off[n])
        done["repl"] = adamw("adamw_repl", *state["repl"], received[1], layer, done["repl"])
        done["ss"] = adamw("adamw_ss", *state["ss"], received[2], layer, done["ss"])
        for k, n in enumerate(COL_SHARDED):
            done[n] = adamw("adamw_" + n, *state[n], received[3], layer, done[n], row_off=k * W[n].shape[1])

    def reduce_in_chip(G):
        big = [G[n].reshape(4, 2, big_rows[n], D) for n in WIDE]
        narrow = jnp.concatenate([G[n].reshape((4, 2) + G[n].shape[1:]) for n in COL_SHARDED], axis=2)
        ss_parts = []
        for n in SMALL_SHARDED:
            k, c = W[n].shape[1:]
            ss_parts.append(jnp.transpose(G[n].reshape(k, N_DEV, c), (1, 0, 2)).reshape(N_DEV, k * c))
        ss_send = jnp.concatenate(ss_parts, axis=1)
        ss_send = jnp.pad(ss_send, ((0, 0), (0, ssl_rows * 128 - ss_send.shape[1]))).reshape(4, 2, ssl_rows, 128)
        repl = _pack([G[n] for n in REPL], repl_rows)
        core = lax.axis_index("c")
        ss_kept = lax.dynamic_index_in_dim(ss_send, core, 1, keepdims=False)
        narrow_kept = lax.dynamic_index_in_dim(narrow, core, 1, keepdims=False)
        got, repl_got, ss_got, narrow_got = pair_exchange("pair_exchange", big, [repl, ss_send, narrow],
                                                          ["gather", "scatter", "scatter"])
        return [pair_add_wide("pair_add_big", big, got), add2("pair_add_repl", repl, repl_got), add2("pair_add_ss", ss_kept, ss_got),
                add2("pair_add_narrow", narrow_kept, narrow_got)]

    chip_modes = ["scatter", "gather", "scatter", "scatter"]
    dcur = dy
    pending = None
    for l in reversed(range(L)):
        P = params[l]
        if pending is not None:
            P = dict(P, ffn_conv_b=P["ffn_conv_b"] + pending[4][0, 0])
        dcur, G, _ = _backward_layer(dcur, P, saved[l], T)
        started = chip_start("chip_start_%d" % l, reduce_in_chip(G), chip_modes)
        if pending is not None:
            after = started[4] + dcur[0:8, 0:128]
            update(l + 1, chip_wait("chip_wait_%d" % (l + 1), *pending[:4], chip_modes, after))
        pending = started
    after = sum(done[n][0][1, 0:1, 0:1] for n in state)
    update(0, chip_wait("chip_wait_0", *pending[:4], chip_modes, after))

    results = [{}, {}, {}, {}]
    for k in range(4):
        for n, a in zip(REPL, _unpack_layers(done["repl"][k], repl_shapes)):
            results[k][n] = a
        for n, a in zip(SMALL_SHARDED, _unpack_layers(done["ss"][k], ss_shapes)):
            results[k][n] = a
        for n in BIG:
            results[k][n] = jnp.transpose(done[n][k], (0, 2, 1)) if n in TRANSPOSED else done[n][k].reshape(W[n].shape)
    out = [loss, dcur.reshape(1, T, D)]
    for k in range(4):
        out += [results[k][n] for n in W_NAMES]
    return tuple(out)
```

```python
import functools

import jax
import jax.numpy as jnp
from jax import lax
from jax.experimental import pallas as pl
from jax.experimental.pallas import tpu as pltpu

f32 = jnp.float32
bf16 = jnp.bfloat16

EPS = 1e-6
N_DEV = 8
DEPTH = 4
CHUNK = 128
HEAD_DIM = 64
N_HEADS = 8
CV_KERNEL = 31
VMEM_LIMIT_BYTES = 56 * 2 ** 20

ADAM_LR = 0.001
ADAM_B1 = 0.9
ADAM_B2 = 0.999
ADAM_EPS = 1e-08
ADAM_WD = 0.01
ADAM_STEP = 10

MESH = pl.DeviceIdType.MESH


def _cparams(n_grid):
    return pltpu.CompilerParams(dimension_semantics=("arbitrary",) * n_grid, vmem_limit_bytes=VMEM_LIMIT_BYTES)


def exchange(name, arrays, modes):
    n = len(arrays)

    def body(*refs):
        copies = _direct_copies(refs[:n], refs[n:2 * n], modes, *refs[2 * n + 1:])
        for cp in copies:
            cp.start()
        for cp in copies:
            cp.wait()
        refs[2 * n][...] = jnp.zeros_like(refs[2 * n])

    res = pl.pallas_call(
        body, name=name, out_shape=_exchange_out_shapes(arrays, modes) + [jax.ShapeDtypeStruct((8, 128), f32)],
        in_specs=[_ANY] * n, out_specs=[_ANY] * n + [pl.BlockSpec(memory_space=pltpu.VMEM)],
        scratch_shapes=_exchange_sems(n),
    )(*arrays)
    return list(res[:n]), res[n]


_ANY = pl.BlockSpec(memory_space=pl.ANY)


def _exchange_out_shapes(arrays, modes):
    return [jax.ShapeDtypeStruct((N_DEV,) + tuple(a.shape) if m == "gather" else tuple(a.shape), a.dtype)
            for a, m in zip(arrays, modes)]


def _exchange_sems(n):
    return [pltpu.SemaphoreType.DMA((n, N_DEV - 1)), pltpu.SemaphoreType.DMA((n, N_DEV - 1)), pltpu.SemaphoreType.DMA((n,))]


def _direct_copies(ins, outs, modes, send_sems, recv_sems, local_sems):
    x, y, c = lax.axis_index("x"), lax.axis_index("y"), lax.axis_index("c")
    me = 4 * x + 2 * y + c
    copies = []
    for k in range(len(ins)):
        src_mine = ins[k] if modes[k] == "gather" else ins[k].at[me]
        copies.append(pltpu.make_async_copy(src_mine, outs[k].at[me], local_sems.at[k]))
        for r in range(1, N_DEV):
            px = 1 - x if r & 4 else x
            py = 1 - y if r & 2 else y
            pc = 1 - c if r & 1 else c
            src = ins[k] if modes[k] == "gather" else ins[k].at[4 * px + 2 * py + pc]
            copies.append(pltpu.make_async_remote_copy(
                src_ref=src, dst_ref=outs[k].at[me], send_sem=send_sems.at[k, r - 1], recv_sem=recv_sems.at[k, r - 1],
                device_id=(px, py, pc), device_id_type=MESH))
    return copies


def own_slot_buffer(block, n_slots, index):
    return lax.dynamic_update_slice(lax.empty((n_slots,) + tuple(block.shape), block.dtype), block[None],
                                    (index,) + (0,) * block.ndim)


def _sems(n, m):
    return [pltpu.SemaphoreType.DMA((n, m)), pltpu.SemaphoreType.DMA((n, m))]


def _two_level_gather(bufs, send_sems, recv_sems):
    x, y, c = lax.axis_index("x"), lax.axis_index("y"), lax.axis_index("c")
    me = 4 * x + 2 * y + c
    sibling = (x, y, 1 - c)
    chips = [(1 - x, y), (x, 1 - y), (1 - x, 1 - y)]

    def slot(px, py, pc):
        return 4 * px + 2 * py + pc

    def copy(k, sem, block, to):
        rows = bufs[k].at[block]
        return pltpu.make_async_remote_copy(src_ref=rows, dst_ref=rows, send_sem=send_sems.at[k, sem],
                                            recv_sem=recv_sems.at[k, sem], device_id=to, device_id_type=MESH)

    def first(k):
        return [copy(k, 0, me, sibling)] + [copy(k, 1 + j, me, (*chip, c)) for j, chip in enumerate(chips)]

    def passed_on(k, j):
        return copy(k, 4 + j, slot(*chips[j], c), sibling)

    def start():
        for k in range(len(bufs)):
            for cp in first(k):
                cp.start()

    def forward():
        for k in range(len(bufs)):
            for j in range(3):
                copy(k, 1 + j, slot(*chips[j], c), sibling).wait_recv()
                passed_on(k, j).start()

    def finish():
        for k in range(len(bufs)):
            copy(k, 0, slot(x, y, 1 - c), sibling).wait_recv()
            for j, chip in enumerate(chips):
                copy(k, 4 + j, slot(*chip, 1 - c), sibling).wait_recv()
            for cp in first(k) + [passed_on(k, j) for j in range(3)]:
                cp.wait_send()

    return start, forward, finish


def pair_exchange(name, big, smalls, modes):
    nb, ns = len(big), len(smalls)
    rows = [a.shape[2] for a in big]
    total, width = sum(rows), big[0].shape[3]
    n = nb + ns

    def body(*refs):
        ins, outs = refs[:n], refs[n:n + 1 + ns]
        send_sems, recv_sems = refs[-2:]
        x, y, c = lax.axis_index("x"), lax.axis_index("y"), lax.axis_index("c")
        copies = []

        def remote(k, src, dst):
            copies.append(pltpu.make_async_remote_copy(src_ref=src, dst_ref=dst, send_sem=send_sems.at[k, 0],
                                                       recv_sem=recv_sems.at[k, 0], device_id=(x, y, 1 - c),
                                                       device_id_type=MESH))

        off = 0
        for k in range(nb):
            remote(k, ins[k].at[:, 1 - c], outs[0].at[:, pl.ds(off, rows[k]), :])
            off += rows[k]
        for j in range(ns):
            remote(nb + j, ins[nb + j].at[:, 1 - c] if modes[j] == "scatter" else ins[nb + j], outs[1 + j])
        for cp in copies:
            cp.start()
        for cp in copies:
            cp.wait()

    out_shape = [jax.ShapeDtypeStruct((4, total, width), big[0].dtype)]
    for a, m in zip(smalls, modes):
        out_shape.append(jax.ShapeDtypeStruct((4,) + tuple(a.shape[2:]) if m == "scatter" else tuple(a.shape), a.dtype))
    return pl.pallas_call(
        body, name=name, out_shape=out_shape, in_specs=[_ANY] * n, out_specs=[_ANY] * len(out_shape),
        scratch_shapes=_sems(n, 1),
    )(*big, *smalls)


def add2(name, a, b):
    shape = a.shape
    a2, b2 = a.reshape(-1, shape[-1]), b.reshape(-1, shape[-1])
    R, C = a2.shape
    tr = _row_tile(R, C, max_elems=1200 * 1024)

    def body(a_ref, b_ref, o_ref):
        o_ref[...] = (a_ref[...].astype(f32) + b_ref[...].astype(f32)).astype(o_ref.dtype)

    spec = pl.BlockSpec((tr, C), lambda i: (i, 0))
    return pl.pallas_call(
        body, name=name, grid=(R // tr,), in_specs=[spec, spec], out_specs=spec,
        out_shape=jax.ShapeDtypeStruct((R, C), a.dtype), compiler_params=_cparams(1),
    )(a2, b2).reshape(shape)


def pair_add_wide(name, big, got):
    nb = len(big)
    rows = [a.shape[2] for a in big]
    total, width = sum(rows), big[0].shape[3]
    core = lax.axis_index("c").astype(jnp.int32).reshape(1)

    def body(core_ref, *refs):
        got_ref, o_ref = refs[nb], refs[nb + 1]
        off = 0
        for k in range(nb):
            part = slice(off, off + rows[k])
            o_ref[part, :] = (refs[k][...].astype(f32) + got_ref[part, :].astype(f32)).astype(o_ref.dtype)
            off += rows[k]

    whole = pl.BlockSpec((None, total, width), lambda i, core_ref: (i, 0, 0))
    grid_spec = pltpu.PrefetchScalarGridSpec(
        num_scalar_prefetch=1, grid=(4,),
        in_specs=[pl.BlockSpec((None, None, r, width), lambda i, core_ref: (i, core_ref[0], 0, 0)) for r in rows] + [whole],
        out_specs=whole)
    return pl.pallas_call(
        body, name=name, grid_spec=grid_spec, out_shape=jax.ShapeDtypeStruct((4, total, width), got.dtype),
        compiler_params=_cparams(1),
    )(core, *big, got)


def chip_buffers(arrays, modes):
    mine = 2 * lax.axis_index("x") + lax.axis_index("y")
    return [own_slot_buffer(lax.dynamic_index_in_dim(a, mine, 0, keepdims=False) if m == "scatter" else a, 4, mine)
            for a, m in zip(arrays, modes)]


def _chip_copies(ins, bufs, modes, send_sems, recv_sems):
    x, y, c = lax.axis_index("x"), lax.axis_index("y"), lax.axis_index("c")
    mine = 2 * x + y
    copies = []
    for k in range(len(ins)):
        for r in range(1, 4):
            px = 1 - x if r & 2 else x
            py = 1 - y if r & 1 else y
            src = ins[k].at[2 * px + py] if modes[k] == "scatter" else ins[k]
            copies.append(pltpu.make_async_remote_copy(
                src_ref=src, dst_ref=bufs[k].at[mine], send_sem=send_sems.at[k, r - 1], recv_sem=recv_sems.at[k, r - 1],
                device_id=(px, py, c), device_id_type=MESH))
    return copies


def chip_exchange(name, arrays, modes):
    n = len(arrays)

    def body(*refs):
        copies = _chip_copies(refs[:n], refs[2 * n:3 * n], modes, *refs[3 * n:])
        for cp in copies:
            cp.start()
        for cp in copies:
            cp.wait()

    bufs = chip_buffers(arrays, modes)
    return pl.pallas_call(
        body, name=name, out_shape=[jax.ShapeDtypeStruct(b.shape, b.dtype) for b in bufs],
        in_specs=[_ANY] * (2 * n), out_specs=[_ANY] * n, input_output_aliases={n + k: k for k in range(n)},
        scratch_shapes=_sems(n, 3),
    )(*arrays, *bufs)


def gather_buffers(blocks):
    me = 4 * lax.axis_index("x") + 2 * lax.axis_index("y") + lax.axis_index("c")
    return [own_slot_buffer(b, N_DEV, me) for b in blocks]


_HBM = pl.BlockSpec(memory_space=pltpu.HBM)
_SEM = pl.BlockSpec(memory_space=pltpu.SEMAPHORE)
_SPLIT_PARAMS = pltpu.CompilerParams(has_side_effects=pltpu.SideEffectType.DATAFLOW_SIDE_EFFECTING)


def _split_gather_copies(bufs, send_sems, recv_sems):
    x, y, c = lax.axis_index("x"), lax.axis_index("y"), lax.axis_index("c")
    me = 4 * x + 2 * y + c
    copies = []
    for k in range(len(bufs)):
        rows = bufs[k].at[me]
        for r in range(1, N_DEV):
            peer = (1 - x if r & 4 else x, 1 - y if r & 2 else y, 1 - c if r & 1 else c)
            copies.append(pltpu.make_async_remote_copy(
                src_ref=rows, dst_ref=rows, send_sem=send_sems.at[(N_DEV - 1) * k + r - 1],
                recv_sem=recv_sems.at[(N_DEV - 1) * k + r - 1], device_id=peer, device_id_type=MESH))
    return copies


def gather_start(name, blocks):
    n = len(blocks)
    bufs = gather_buffers(blocks)

    def body(*refs):
        for cp in _split_gather_copies(refs[n + 2:2 * n + 2], refs[n], refs[n + 1]):
            cp.start()
        refs[2 * n + 2][...] = jnp.zeros_like(refs[2 * n + 2])

    n_sem = n * (N_DEV - 1)
    res = pl.pallas_call(
        body, name=name,
        out_shape=(pltpu.SemaphoreType.DMA((n_sem,)), pltpu.SemaphoreType.DMA((n_sem,)),
                   *[pltpu.HBM(b.shape, b.dtype) for b in bufs], jax.ShapeDtypeStruct((8, 128), f32)),
        in_specs=[_HBM] * n, out_specs=(_SEM, _SEM, *[_HBM] * n, pl.BlockSpec(memory_space=pltpu.VMEM)),
        input_output_aliases={k: 2 + k for k in range(n)}, compiler_params=_SPLIT_PARAMS,
    )(*[pltpu.with_memory_space_constraint(b, pltpu.HBM) for b in bufs])
    return res[0], res[1], list(res[2:2 + n]), res[2 + n]


def gather_wait(name, send_sems, recv_sems, bufs, after):
    n = len(bufs)

    def body(*refs):
        for cp in _split_gather_copies(refs[:n], refs[n], refs[n + 1]):
            cp.wait_send()
            cp.wait_recv()

    return list(pl.pallas_call(
        body, name=name, out_shape=tuple(pltpu.HBM(b.shape, b.dtype) for b in bufs),
        in_specs=[_HBM] * n + [_SEM, _SEM, _ANY], out_specs=tuple([_HBM] * n),
        input_output_aliases={k: k for k in range(n)}, compiler_params=_SPLIT_PARAMS,
    )(*bufs, send_sems, recv_sems, after))


def _split_chip_copies(srcs, bufs, modes, send_sems, recv_sems):
    x, y, c = lax.axis_index("x"), lax.axis_index("y"), lax.axis_index("c")
    mine = 2 * x + y
    copies = []
    for k in range(len(srcs)):
        for r in range(1, 4):
            px = 1 - x if r & 2 else x
            py = 1 - y if r & 1 else y
            src = srcs[k].at[2 * px + py] if modes[k] == "scatter" else srcs[k]
            copies.append(pltpu.make_async_remote_copy(
                src_ref=src, dst_ref=bufs[k].at[mine], send_sem=send_sems.at[3 * k + r - 1],
                recv_sem=recv_sems.at[3 * k + r - 1], device_id=(px, py, c), device_id_type=MESH))
    return copies


def chip_start(name, arrays, modes):
    n = len(arrays)
    bufs = chip_buffers(arrays, modes)

    def body(*refs):
        for cp in _split_chip_copies(refs[2 * n + 2:3 * n + 2], refs[3 * n + 2:4 * n + 2], modes, refs[2 * n], refs[2 * n + 1]):
            cp.start()
        refs[4 * n + 2][...] = jnp.zeros_like(refs[4 * n + 2])

    both = list(arrays) + list(bufs)
    res = pl.pallas_call(
        body, name=name,
        out_shape=(pltpu.SemaphoreType.DMA((3 * n,)), pltpu.SemaphoreType.DMA((3 * n,)),
                   *[pltpu.HBM(b.shape, b.dtype) for b in both], jax.ShapeDtypeStruct((8, 128), f32)),
        in_specs=[_HBM] * (2 * n), out_specs=(_SEM, _SEM, *[_HBM] * (2 * n), pl.BlockSpec(memory_space=pltpu.VMEM)),
        input_output_aliases={k: 2 + k for k in range(2 * n)}, compiler_params=_SPLIT_PARAMS,
    )(*[pltpu.with_memory_space_constraint(b, pltpu.HBM) for b in both])
    return res[0], res[1], list(res[2:2 + n]), list(res[2 + n:2 + 2 * n]), res[2 + 2 * n]


def chip_wait(name, send_sems, recv_sems, srcs, bufs, modes, after):
    n = len(bufs)

    def body(*refs):
        for cp in _split_chip_copies(refs[:n], refs[n:2 * n], modes, refs[2 * n], refs[2 * n + 1]):
            cp.wait_send()
            cp.wait_recv()

    both = list(srcs) + list(bufs)
    res = pl.pallas_call(
        body, name=name, out_shape=tuple(pltpu.HBM(b.shape, b.dtype) for b in both),
        in_specs=[_HBM] * (2 * n) + [_SEM, _SEM, _ANY], out_specs=tuple([_HBM] * (2 * n)),
        input_output_aliases={k: k for k in range(2 * n)}, compiler_params=_SPLIT_PARAMS,
    )(*both, send_sems, recv_sems, after)
    return list(res[n:])


def gather2(name, arrays):
    n = len(arrays)

    def body(*refs):
        start, forward, finish = _two_level_gather(refs[n:2 * n], *refs[2 * n + 1:])
        start()
        forward()
        finish()
        refs[2 * n][...] = jnp.zeros_like(refs[2 * n])

    bufs = gather_buffers(arrays)
    res = pl.pallas_call(
        body, name=name, out_shape=[jax.ShapeDtypeStruct(b.shape, b.dtype) for b in bufs] + [jax.ShapeDtypeStruct((8, 128), f32)],
        in_specs=[_ANY] * n, out_specs=[_ANY] * n + [pl.BlockSpec(memory_space=pltpu.VMEM)],
        input_output_aliases={k: k for k in range(n)}, scratch_shapes=_sems(n, N_DEV - 1),
    )(*bufs)
    return list(res[:n]), res[n]


def assemble(name, slabs):
    _, K, Ns = slabs.shape
    g = N_DEV if Ns % 128 == 0 else 2

    def body(w_ref, o_ref):
        o_ref[...] = jnp.concatenate([w_ref[s] for s in range(g)], axis=1)

    return pl.pallas_call(
        body, name=name, grid=(N_DEV // g,),
        in_specs=[pl.BlockSpec((g, K, Ns), lambda m: (m, 0, 0))],
        out_specs=pl.BlockSpec((K, g * Ns), lambda m: (0, m)),
        out_shape=jax.ShapeDtypeStruct((K, N_DEV * Ns), slabs.dtype),
        compiler_params=_cparams(1),
    )(slabs)


def mm_nn(name, a, b, out_dtype, add=None, tm=1024, tn=512, tk=None):
    M, K = a.shape
    N = b.shape[1]
    tm, tn = min(tm, M), min(tn, N)
    tk = K if tk is None else tk
    nk = K // tk
    has_add = add is not None

    def body(a_ref, b_ref, *rest):
        o_ref = rest[1] if has_add else rest[0]
        part = jnp.dot(a_ref[...].astype(bf16), b_ref[...].astype(bf16), preferred_element_type=f32)
        if nk == 1:
            o_ref[...] = (part + rest[0][...] if has_add else part).astype(o_ref.dtype)
            return
        acc_ref = rest[-1]
        k = pl.program_id(2)

        @pl.when(k == 0)
        def _():
            acc_ref[...] = part + rest[0][...] if has_add else part

        @pl.when(k > 0)
        def _():
            acc_ref[...] += part

        @pl.when(k == nk - 1)
        def _():
            o_ref[...] = acc_ref[...].astype(o_ref.dtype)

    in_specs = [pl.BlockSpec((tm, tk), lambda i, j, k: (i, k)), pl.BlockSpec((tk, tn), lambda i, j, k: (k, j))]
    ops = [a, b]
    if has_add:
        in_specs.append(pl.BlockSpec((tm, tn), lambda i, j, k: (i, j)))
        ops.append(add)
    return pl.pallas_call(
        body, name=name, grid=(M // tm, N // tn, nk), in_specs=in_specs,
        out_specs=pl.BlockSpec((tm, tn), lambda i, j, k: (i, j)),
        out_shape=jax.ShapeDtypeStruct((M, N), out_dtype),
        scratch_shapes=[pltpu.VMEM((tm, tn), f32)] if nk > 1 else [], compiler_params=_cparams(3),
    )(*ops)


def mm_many(name, xs, ws, out_dtype, transpose_w=False, tm=512):
    n = len(xs)
    M = xs[0].shape[0]
    tm = min(tm, M)
    dims = NT_DIMS if transpose_w else (((1,), (0,)), ((), ()))

    def body(*refs):
        for k in range(n):
            refs[2 * n + k][...] = lax.dot_general(refs[k][...].astype(bf16), refs[n + k][...].astype(bf16), dims,
                                                   preferred_element_type=f32).astype(out_dtype)

    n_out = [w.shape[0] if transpose_w else w.shape[1] for w in ws]
    return pl.pallas_call(
        body, name=name, grid=(M // tm,),
        in_specs=[pl.BlockSpec((tm, x.shape[1]), lambda i: (i, 0)) for x in xs]
        + [pl.BlockSpec(w.shape, lambda i: (0, 0)) for w in ws],
        out_specs=[pl.BlockSpec((tm, c), lambda i: (i, 0)) for c in n_out],
        out_shape=[jax.ShapeDtypeStruct((M, c), out_dtype) for c in n_out], compiler_params=_cparams(1),
    )(*xs, *ws)


def concat_cols(name, pieces, T, tm=512):
    widths = [p.shape[1] for p in pieces]
    total = sum(widths)
    tm = min(tm, T)

    def body(*refs):
        o_ref = refs[-1]
        off = 0
        for r, w in zip(refs[:-1], widths):
            o_ref[:, off:off + w] = r[...]
            off += w

    return pl.pallas_call(
        body, name=name, grid=(T // tm,),
        in_specs=[pl.BlockSpec((tm, w), lambda i: (i, 0)) for w in widths],
        out_specs=pl.BlockSpec((tm, total), lambda i: (i, 0)),
        out_shape=jax.ShapeDtypeStruct((T, total), pieces[0].dtype), compiler_params=_cparams(1),
    )(*pieces)


def mm_nt(name, a, b, out_dtype, tm=1024, tko=None, tc=None):
    M, C = a.shape
    Ko = b.shape[0]
    tm = min(tm, M)
    tc = C if tc is None else min(tc, C)
    tko = Ko if tko is None else tko
    nc = C // tc

    def body(a_ref, b_ref, o_ref, *scratch):
        part = lax.dot_general(a_ref[...].astype(bf16), b_ref[...].astype(bf16), (((1,), (1,)), ((), ())),
                               preferred_element_type=f32)
        if nc == 1:
            o_ref[...] = part.astype(o_ref.dtype)
            return
        acc_ref = scratch[0]
        c = pl.program_id(2)

        @pl.when(c == 0)
        def _():
            acc_ref[...] = part

        @pl.when(c > 0)
        def _():
            acc_ref[...] += part

        @pl.when(c == nc - 1)
        def _():
            o_ref[...] = acc_ref[...].astype(o_ref.dtype)

    return pl.pallas_call(
        body, name=name, grid=(M // tm, Ko // tko, nc),
        in_specs=[pl.BlockSpec((tm, tc), lambda i, j, c: (i, c)), pl.BlockSpec((tko, tc), lambda i, j, c: (j, c))],
        out_specs=pl.BlockSpec((tm, tko), lambda i, j, c: (i, j)),
        out_shape=jax.ShapeDtypeStruct((M, Ko), out_dtype),
        scratch_shapes=[pltpu.VMEM((tm, tko), f32)] if nc > 1 else [], compiler_params=_cparams(3),
    )(a, b)


def mm_tn(name, a, b, n_slab=None, tk=512, tn=512):
    T, K = a.shape
    N = b.shape[1]
    tk = min(tk, K)
    if n_slab is None:
        tn = min(tn, N)

        def body(a_ref, b_ref, o_ref):
            o_ref[...] = lax.dot_general(a_ref[...].astype(bf16), b_ref[...].astype(bf16), (((0,), (0,)), ((), ())),
                                         preferred_element_type=f32).astype(o_ref.dtype)

        return pl.pallas_call(
            body, name=name, grid=(N // tn, K // tk),
            in_specs=[pl.BlockSpec((T, tk), lambda j, i: (0, i)), pl.BlockSpec((T, tn), lambda j, i: (0, j))],
            out_specs=pl.BlockSpec((tk, tn), lambda j, i: (i, j)),
            out_shape=jax.ShapeDtypeStruct((K, N), bf16), compiler_params=_cparams(2),
        )(a, b)

    Ns = n_slab
    g = N_DEV if Ns % 128 == 0 else 2
    tn = g * Ns

    def body(a_ref, b_ref, o_ref):
        val = lax.dot_general(a_ref[...].astype(bf16), b_ref[...].astype(bf16), (((0,), (0,)), ((), ())),
                              preferred_element_type=f32)
        for s in range(g):
            o_ref[s] = val[:, s * Ns:(s + 1) * Ns].astype(o_ref.dtype)

    return pl.pallas_call(
        body, name=name, grid=(K // tk, N_DEV // g),
        in_specs=[pl.BlockSpec((T, tk), lambda i, j: (0, i)), pl.BlockSpec((T, tn), lambda i, j: (0, j))],
        out_specs=pl.BlockSpec((g, tk, Ns), lambda i, j: (j, i, 0)),
        out_shape=jax.ShapeDtypeStruct((N_DEV, K, Ns), bf16), compiler_params=_cparams(2),
    )(a, b)


def _tile_spec(w, tm, cb):
    if callable(cb):
        return pl.BlockSpec((tm, w), lambda c, i: (i, cb(c)))
    return pl.BlockSpec((tm, w), lambda c, i: (i, cb))


def _param_spec(block, cb):
    nd = len(block)
    if cb is None:
        return pl.BlockSpec(block, lambda c, i: (0,) * nd)
    return pl.BlockSpec(block, lambda c, i: (0,) * (nd - 1) + (cb(c),))


ROW_TILE = 512


def rowwise(name, fn, tiled, params, outs, T, tm, ncol=1):
    tm = min(tm, T)
    n_in = len(tiled) + len(params)
    n_t = len(tiled)

    def body(*refs):
        ins = [r[...].astype(f32) for r in refs[:n_t]] + [r[...] for r in refs[n_t:n_in]]
        res = fn(*ins)
        for r, o in zip(refs[n_in:], res):
            r[...] = o.astype(r.dtype)

    return pl.pallas_call(
        body, name=name, grid=(ncol, T // tm),
        in_specs=[_tile_spec(w, tm, cb) for _, w, cb in tiled] + [_param_spec(blk, cb) for _, blk, cb in params],
        out_specs=[_tile_spec(w, tm, cb) for _, w, cb, _ in outs],
        out_shape=[jax.ShapeDtypeStruct((T, cols), dt) for cols, _, _, dt in outs],
        compiler_params=_cparams(2),
    )(*[a for a, _, _ in tiled], *[a for a, _, _ in params])


def rowwise_bwd(name, fn, tiled, params, cts, grads, T, tm, ncol=1, adds=None):
    tm = min(tm, T)
    n_t, n_p, n_c = len(tiled), len(params), len(cts)
    adds = adds or [None] * n_t
    add_list = [(k, a) for k, a in enumerate(adds) if a is not None]
    want = [k for k, g in enumerate(grads) if g is not None]
    n_a = len(add_list)

    def body(*refs):
        pos = 0
        t_refs = refs[pos:pos + n_t]; pos += n_t
        p_refs = refs[pos:pos + n_p]; pos += n_p
        c_refs = refs[pos:pos + n_c]; pos += n_c
        a_refs = refs[pos:pos + n_a]; pos += n_a
        g_refs = refs[pos:pos + len(want)]; pos += len(want)
        pg_refs = refs[pos:pos + n_p]
        primals = [r[...].astype(f32) for r in t_refs] + [r[...] for r in p_refs]
        _, vjp = jax.vjp(fn, *primals)
        g = vjp(tuple(r[...].astype(f32) for r in c_refs))
        add_of = {k: a_refs[n][...] for n, (k, _) in enumerate(add_list)}
        for n, k in enumerate(want):
            val = g[k]
            if k in add_of:
                val = val + add_of[k]
            g_refs[n][...] = val.astype(g_refs[n].dtype)
        i = pl.program_id(1)
        for k in range(n_p):
            @pl.when(i == 0)
            def _(k=k):
                pg_refs[k][...] = g[n_t + k]

            @pl.when(i > 0)
            def _(k=k):
                pg_refs[k][...] += g[n_t + k]

    in_specs = ([_tile_spec(w, tm, cb) for _, w, cb in tiled] + [_param_spec(blk, cb) for _, blk, cb in params]
                + [_tile_spec(w, tm, cb) for _, w, cb in cts] + [_tile_spec(w, tm, cb) for _, (_, w, cb) in add_list])
    ops = ([a for a, _, _ in tiled] + [a for a, _, _ in params] + [a for a, _, _ in cts]
           + [a for _, (a, _, _) in add_list])
    out_specs, out_shape = [], []
    for k in want:
        w = tiled[k][1]
        out_specs.append(_tile_spec(w, tm, lambda c: c))
        out_shape.append(jax.ShapeDtypeStruct((T, ncol * w), grads[k]))
    for a, blk, cb in params:
        out_specs.append(_param_spec(blk, cb))
        out_shape.append(jax.ShapeDtypeStruct(a.shape, f32))
    return pl.pallas_call(
        body, name=name, grid=(ncol, T // tm), in_specs=in_specs, out_specs=out_specs, out_shape=out_shape,
        compiler_params=_cparams(2),
    )(*ops)


@jax.custom_vjp
def _bdot(a, b):
    return jnp.dot(a.astype(bf16), b.astype(bf16), preferred_element_type=f32)


def _bdot_fwd(a, b):
    return _bdot(a, b), (a, b)


def _bdot_bwd(res, ct):
    a, b = res
    ctb = ct.astype(bf16)
    da = lax.dot_general(ctb, b.astype(bf16), (((1,), (1,)), ((), ())), preferred_element_type=f32)
    db = lax.dot_general(a.astype(bf16), ctb, (((0,), (0,)), ((), ())), preferred_element_type=f32)
    return da, db


_bdot.defvjp(_bdot_fwd, _bdot_bwd)


def _layer_norm(x, g, b):
    mu = jnp.mean(x, axis=-1, keepdims=True)
    xc = x - mu
    y = xc * lax.rsqrt(jnp.mean(xc * xc, axis=-1, keepdims=True) + EPS)
    return y * g + b


def f_rms(x, g):
    y = x * lax.rsqrt(jnp.mean(x * x, axis=-1, keepdims=True) + EPS)
    return (y * g,)


def f_sgu(zu, zv, ln_g, ln_b, wm, sgb_t):
    u = jax.nn.gelu(zu)
    vn = _layer_norm(jax.nn.gelu(zv), ln_g, ln_b)
    row = lax.broadcasted_iota(jnp.int32, (CHUNK, CHUNK), 0)
    col = lax.broadcasted_iota(jnp.int32, (CHUNK, CHUNK), 1)
    tril = col <= row
    low = col < HEAD_DIM
    parts = []
    for p in range(4):
        vp = vn[:, CHUNK * p:CHUNK * (p + 1)]
        w0 = jnp.where(tril, wm[2 * p], 0.0)
        w1 = jnp.where(tril, wm[2 * p + 1], 0.0)
        parts.append(_bdot(w0, jnp.where(low, vp, 0.0)) + _bdot(w1, jnp.where(low, 0.0, vp)))
    s = jnp.concatenate(parts, axis=1)
    lane_g = lax.shift_right_logical(lax.broadcasted_iota(jnp.int32, s.shape, 1), 6)
    bias = jnp.zeros_like(s)
    for g in range(8):
        bias = jnp.where(lane_g == g, sgb_t[:, g:g + 1], bias)
    return (u * (s + bias),)


def f_glu(p, gl):
    return (p * jax.nn.sigmoid(gl),)


def f_lnsilu(c1, g, b):
    return (jax.nn.silu(_layer_norm(c1, g, b)),)


def _head_norm(x, g64):
    g = jnp.concatenate([g64] * N_HEADS, axis=1)
    lane_h = lax.shift_right_logical(lax.broadcasted_iota(jnp.int32, x.shape, 1), 6)
    x2 = x * x
    r = jnp.zeros_like(x)
    for h in range(N_HEADS):
        mh = lane_h == h
        ms = jnp.sum(jnp.where(mh, x2, 0.0), axis=-1, keepdims=True) * (1.0 / HEAD_DIM)
        r = jnp.where(mh, lax.rsqrt(ms + EPS), r)
    return (x * r) * g


def f_qkv(zq, zk, zv, qg, kg):
    return (_head_norm(zq, qg) * 0.125, _head_norm(zk, kg), zv)


def f_merge(g0, g1, g2, ya, yb, yc, b0, b1, b2):
    return (jax.nn.sigmoid(g0 + b0) * ya + jax.nn.sigmoid(g1 + b1) * yb + jax.nn.sigmoid(g2 + b2) * yc,)


HALO = 32


def conv31_fwd(x, w, b, T, tt=256):
    C = x.shape[1]
    r = tt // HALO

    def body(x_ref, h_ref, w_ref, b_ref, y_ref, buf):
        i = pl.program_id(0)
        halo = h_ref[...]
        buf[0:HALO, :] = jnp.where(i > 0, halo, jnp.zeros_like(halo))
        buf[HALO:HALO + tt, :] = x_ref[...]
        acc = jnp.zeros((tt, C), f32) + b_ref[...]
        for k in range(CV_KERNEL):
            acc = acc + w_ref[k:k + 1, :] * buf[pl.ds(HALO - (CV_KERNEL - 1) + k, tt), :]
        y_ref[...] = acc

    return pl.pallas_call(
        body, name="conv31_fwd", grid=(T // tt,),
        in_specs=[pl.BlockSpec((tt, C), lambda i: (i, 0)),
                  pl.BlockSpec((HALO, C), lambda i: (jnp.maximum(i * r - 1, 0), 0)),
                  pl.BlockSpec((CV_KERNEL, C), lambda i: (0, 0)), pl.BlockSpec((1, C), lambda i: (0, 0))],
        out_specs=pl.BlockSpec((tt, C), lambda i: (i, 0)),
        out_shape=jax.ShapeDtypeStruct((T, C), f32),
        scratch_shapes=[pltpu.VMEM((HALO + tt, C), f32)], compiler_params=_cparams(1),
    )(x, x, w, b)


def conv31_bwd(x, w, dy, T, tt=256):
    C = x.shape[1]
    r = tt // HALO
    n = T // tt

    def body(x_ref, h_ref, w_ref, dy_ref, dyn_ref, dx_ref, dw_ref, db_ref, xbuf, dbuf):
        i = pl.program_id(0)
        halo = h_ref[...]
        xbuf[0:HALO, :] = jnp.where(i > 0, halo, jnp.zeros_like(halo))
        xbuf[HALO:HALO + tt, :] = x_ref[...]
        nxt = dyn_ref[...]
        dy = dy_ref[...]
        dbuf[0:tt, :] = dy
        dbuf[tt:tt + HALO, :] = jnp.where(i < n - 1, nxt, jnp.zeros_like(nxt))

        @pl.when(i == 0)
        def _():
            dw_ref[...] = jnp.zeros_like(dw_ref)
            db_ref[...] = jnp.zeros_like(db_ref)

        acc = jnp.zeros((tt, C), f32)
        for k in range(CV_KERNEL):
            acc = acc + w_ref[k:k + 1, :] * dbuf[pl.ds(CV_KERNEL - 1 - k, tt), :]
            xs = xbuf[pl.ds(HALO - (CV_KERNEL - 1) + k, tt), :]
            dw_ref[k:k + 1, :] += jnp.sum(dy * xs, axis=0, keepdims=True)
        dx_ref[...] = acc
        db_ref[...] += jnp.sum(dy, axis=0, keepdims=True)

    return pl.pallas_call(
        body, name="conv31_bwd", grid=(n,),
        in_specs=[pl.BlockSpec((tt, C), lambda i: (i, 0)),
                  pl.BlockSpec((HALO, C), lambda i: (jnp.maximum(i * r - 1, 0), 0)),
                  pl.BlockSpec((CV_KERNEL, C), lambda i: (0, 0)),
                  pl.BlockSpec((tt, C), lambda i: (i, 0)),
                  pl.BlockSpec((HALO, C), lambda i: (jnp.minimum((i + 1) * r, n * r - 1), 0))],
        out_specs=[pl.BlockSpec((tt, C), lambda i: (i, 0)), pl.BlockSpec((CV_KERNEL, C), lambda i: (0, 0)),
                   pl.BlockSpec((1, C), lambda i: (0, 0))],
        out_shape=[jax.ShapeDtypeStruct((T, C), f32), jax.ShapeDtypeStruct((CV_KERNEL, C), f32),
                   jax.ShapeDtypeStruct((1, C), f32)],
        scratch_shapes=[pltpu.VMEM((HALO + tt, C), f32), pltpu.VMEM((HALO + tt, C), f32)], compiler_params=_cparams(1),
    )(x, x, w, dy, dy)


FFN_TC = 128
FFN_PAD = 8


def ffn_act_fwd(up, cw, cb, T):
    F = up.shape[1] // 2
    nj = F // FFN_TC
    rc = min(128, T)

    def body(g_ref, v_ref, g0, g1, g2, gb, v0, v1, v2, vb, o_ref, gp, vp):
        zeros = jnp.zeros((FFN_PAD, FFN_TC), f32)
        for p, x_ref in ((gp, g_ref), (vp, v_ref)):
            p[0:FFN_PAD, :] = zeros
            p[FFN_PAD:FFN_PAD + T, :] = x_ref[...].astype(f32)
        wg = (g0[...], g1[...], g2[...], gb[...])
        wv = (v0[...], v1[...], v2[...], vb[...])

        def conv(p, w, r):
            return (w[0] * p[pl.ds(FFN_PAD + r - 2, rc), :] + w[1] * p[pl.ds(FFN_PAD + r - 1, rc), :]
                    + w[2] * p[pl.ds(FFN_PAD + r, rc), :] + w[3])

        for r in range(0, T, rc):
            o_ref[pl.ds(r, rc), :] = (jax.nn.silu(conv(gp, wg, r)) * conv(vp, wv, r)).astype(o_ref.dtype)

    gspec = pl.BlockSpec((T, FFN_TC), lambda j: (0, j))
    vspec = pl.BlockSpec((T, FFN_TC), lambda j: (0, j + nj))
    pg = pl.BlockSpec((1, FFN_TC), lambda j: (0, j))
    pv = pl.BlockSpec((1, FFN_TC), lambda j: (0, j + nj))
    return pl.pallas_call(
        body, name="ffn_act_fwd", grid=(nj,),
        in_specs=[gspec, vspec, pg, pg, pg, pg, pv, pv, pv, pv], out_specs=gspec,
        out_shape=jax.ShapeDtypeStruct((T, F), bf16),
        scratch_shapes=[pltpu.VMEM((FFN_PAD + T, FFN_TC), f32)] * 2, compiler_params=_cparams(1),
    )(up, up, cw[0], cw[1], cw[2], cb, cw[0], cw[1], cw[2], cb)


def ffn_act_bwd(up, cw, cb, dact, T):
    F = up.shape[1] // 2
    nj = F // FFN_TC

    rc = min(128, T)
    ext = rc + FFN_PAD

    def body(g_ref, v_ref, g0, g1, g2, gb, v0, v1, v2, vb, d_ref, dupg_ref, dupv_ref, *rest):
        p_refs, (gp, vp, dp, dgs, dvs) = rest[:8], rest[8:]
        zeros = jnp.zeros((FFN_PAD, FFN_TC), f32)
        for p, x_ref in ((gp, g_ref), (vp, v_ref)):
            p[0:FFN_PAD, :] = zeros
            p[FFN_PAD:FFN_PAD + T, :] = x_ref[...].astype(f32)
            p[FFN_PAD + T:FFN_PAD + T + FFN_PAD, :] = zeros
        dp[0:T, :] = d_ref[...]
        dp[T:T + FFN_PAD, :] = zeros
        wg = (g0[...], g1[...], g2[...], gb[...])
        wv = (v0[...], v1[...], v2[...], vb[...])
        acc = [jnp.zeros((1, FFN_TC), f32) for _ in range(8)]

        def taps(p, r):
            return tuple(p[pl.ds(FFN_PAD + r - s, ext), :] for s in (2, 1, 0))

        for r in range(0, T, rc):
            xg, xv = taps(gp, r), taps(vp, r)
            gc = wg[0] * xg[0] + wg[1] * xg[1] + wg[2] * xg[2] + wg[3]
            vc = wv[0] * xv[0] + wv[1] * xv[1] + wv[2] * xv[2] + wv[3]
            d = dp[pl.ds(r, ext), :]
            sg = jax.nn.sigmoid(gc)
            sides = ((d * vc * (sg * (1.0 + gc * (1.0 - sg))), xg, wg, dgs, dupg_ref), (d * (gc * sg), xv, wv, dvs, dupv_ref))
            for side, (dc, x, w, buf, dup_ref) in enumerate(sides):
                buf[...] = dc
                dc0 = dc[0:rc]
                dup = w[2] * dc0 + w[1] * buf[pl.ds(1, rc), :] + w[0] * buf[pl.ds(2, rc), :]
                dup_ref[pl.ds(r, rc), :] = dup.astype(dup_ref.dtype)
                for k in range(3):
                    acc[4 * side + k] = acc[4 * side + k] + jnp.sum(dc0 * x[k][0:rc], axis=0, keepdims=True)
                acc[4 * side + 3] = acc[4 * side + 3] + jnp.sum(dc0, axis=0, keepdims=True)
        for k in range(8):
            p_refs[k][...] = acc[k]

    gspec = pl.BlockSpec((T, FFN_TC), lambda j: (0, j))
    vspec = pl.BlockSpec((T, FFN_TC), lambda j: (0, j + nj))
    pg = pl.BlockSpec((1, FFN_TC), lambda j: (0, j))
    pv = pl.BlockSpec((1, FFN_TC), lambda j: (0, j + nj))
    res = pl.pallas_call(
        body, name="ffn_act_bwd", grid=(nj,),
        in_specs=[gspec, vspec, pg, pg, pg, pg, pv, pv, pv, pv, gspec],
        out_specs=[gspec, gspec] + [pg] * 8,
        out_shape=[jax.ShapeDtypeStruct((T, F), bf16)] * 2 + [jax.ShapeDtypeStruct((1, F), f32)] * 8,
        scratch_shapes=[pltpu.VMEM((FFN_PAD + T + FFN_PAD, FFN_TC), f32)] * 2 + [pltpu.VMEM((T + FFN_PAD, FFN_TC), f32)]
        + [pltpu.VMEM((ext, FFN_TC), f32)] * 2,
        compiler_params=_cparams(1),
    )(up, up, cw[0], cw[1], cw[2], cb, cw[0], cw[1], cw[2], cb, dact)
    dup = concat_cols("dup_concat", [res[0], res[1]], T)
    return (dup,) + tuple(jnp.concatenate([res[2 + k], res[6 + k]], axis=1) for k in range(4))


BQ = 256
BK = 256
assert BQ == BK
NT_DIMS = (((1,), (1,)), ((), ()))
TN_DIMS = (((0,), (0,)), ((), ()))


def _split_dot(x, u):
    x1 = x.astype(bf16)
    x2 = (x - x1.astype(f32)).astype(bf16)
    n = x.shape[0]
    y = jnp.dot(jnp.concatenate([x1, x2], axis=0), u, preferred_element_type=f32)
    return y[0:n] + y[n:2 * n]


def _log_sigmoids(z):
    sp = jnp.log(1.0 + jnp.exp(-jnp.abs(z)))
    lsp = jnp.minimum(z, 0.0) - sp
    return lsp, lsp - z


def _stack_heads(x):
    head1 = lax.broadcasted_iota(jnp.int32, x.shape, 1) >= HEAD_DIM
    zero = jnp.zeros_like(x)
    return jnp.concatenate([jnp.where(head1, zero, x), jnp.where(head1, x, zero)], axis=0)


def _unstack_heads(y):
    head1 = lax.broadcasted_iota(jnp.int32, (BQ, y.shape[1]), 1) >= HEAD_DIM
    return jnp.where(head1, y[BQ:2 * BQ], y[0:BQ])


def _attn_masks():
    row = lax.broadcasted_iota(jnp.int32, (2 * BQ, BK), 0)
    col = lax.broadcasted_iota(jnp.int32, (2 * BQ, BK), 1)
    ur = lax.broadcasted_iota(jnp.int32, (BK, BK), 0)
    uc = lax.broadcasted_iota(jnp.int32, (BK, BK), 1)
    return (row & (BQ - 1)) - col, (ur > uc).astype(bf16), (ur < uc).astype(bf16)


def attn_fwd(q, k, v, T, gather=None):
    nq = T // BQ
    n_g = 0 if gather is None else len(gather)

    def body(*refs):
        q_ref, k_ref, v_ref = refs[:3]
        o_ref, tot_ref = refs[3 + n_g:5 + n_g]
        p, i = pl.program_id(0), pl.program_id(1)
        if n_g:
            start, forward, finish = _two_level_gather(refs[5 + n_g:5 + 2 * n_g], *refs[5 + 2 * n_g:])
            pl.when(jnp.logical_and(p == 0, i == 0))(start)
        qs = _stack_heads(q_ref[...])
        diff, u_after, _ = _attn_masks()

        def step(jb, carry, diagonal):
            acc, c = carry
            ks = pl.multiple_of(jb * BK, BK)
            kb = k_ref[pl.ds(ks, BK), :]
            vb = v_ref[pl.ds(ks, BK), :]
            z = lax.dot_general(qs, kb, NT_DIMS, preferred_element_type=f32)
            lsp, lm = _log_sigmoids(z)
            if diagonal:
                m = diff > 0
                lm = jnp.where(m, lm, 0.0)
            a = jnp.exp(lsp + _split_dot(lm, u_after))
            if diagonal:
                a = jnp.where(m, a, 0.0)
            acc = acc + jnp.exp(c) * jnp.dot(a.astype(bf16), vb, preferred_element_type=f32)
            return acc, c + jnp.sum(lm, axis=-1, keepdims=True)

        carry = step(i, (jnp.zeros((2 * BQ, 128), f32), jnp.zeros((2 * BQ, 1), f32)), True)
        acc, c = lax.fori_loop(0, i, lambda t, cr: step(i - 1 - t, cr, False), carry)
        o_ref[...] = _unstack_heads(acc).astype(o_ref.dtype)
        tot_ref[...] = _unstack_heads(jnp.broadcast_to(c, (2 * BQ, 128)))
        if n_g:
            @pl.when(jnp.logical_and(p == 3, i == nq - 1))
            def _():
                forward()
                finish()

    blk = pl.BlockSpec((BQ, 128), lambda p, i: (i, p))
    full = pl.BlockSpec((T, 128), lambda p, i: (0, p))
    bufs = [] if gather is None else gather_buffers(gather)
    res = pl.pallas_call(
        body, name="attn_fwd_gather" if n_g else "attn_fwd", grid=(4, nq),
        in_specs=[blk, full, full] + [_ANY] * n_g, out_specs=[blk, blk] + [_ANY] * n_g,
        out_shape=[jax.ShapeDtypeStruct((T, 512), bf16), jax.ShapeDtypeStruct((T, 512), f32)]
        + [jax.ShapeDtypeStruct(b.shape, b.dtype) for b in bufs],
        input_output_aliases={3 + k: 2 + k for k in range(n_g)},
        scratch_shapes=_sems(n_g, N_DEV - 1) if n_g else [], compiler_params=_cparams(2),
    )(q, k, v, *bufs)
    return res[0], res[1], list(res[2:])


def attn_bwd(q, k, v, do, tot, T, scatter=None):
    nq = T // BQ
    n_s = 0 if scatter is None else len(scatter[0])
    modes = [] if scatter is None else list(scatter[1])

    def body(*refs):
        q_ref, k_ref, v_ref, do_ref, tot_ref = refs[:5]
        dq_ref, dk_ref, dv_ref = refs[5 + 2 * n_s:8 + 2 * n_s]
        p, i = pl.program_id(0), pl.program_id(1)
        if n_s:
            def copies():
                return _chip_copies(refs[5:5 + n_s], refs[8 + 2 * n_s:8 + 3 * n_s], modes, *refs[8 + 3 * n_s:])

            @pl.when(jnp.logical_and(p == 0, i == 0))
            def _():
                for cp in copies():
                    cp.start()

        @pl.when(i == 0)
        def _():
            dk_ref[...] = jnp.zeros_like(dk_ref)
            dv_ref[...] = jnp.zeros_like(dv_ref)

        qs = _stack_heads(q_ref[...])
        dos = _stack_heads(do_ref[...].astype(bf16))
        totv = tot_ref[...]
        tots = jnp.concatenate([totv[:, 0:1], totv[:, HEAD_DIM:HEAD_DIM + 1]], axis=0)
        diff, u_after, u_before = _attn_masks()

        def step(jb, carry, diagonal):
            dq, cl, cg = carry
            ks = pl.multiple_of(jb * BK, BK)
            kb = k_ref[pl.ds(ks, BK), :]
            vb = v_ref[pl.ds(ks, BK), :]
            z = lax.dot_general(qs, kb, NT_DIMS, preferred_element_type=f32)
            lsp, lsn = _log_sigmoids(z)
            lm = lsn
            if diagonal:
                m = diff > 0
                lm = jnp.where(m, lsn, 0.0)
            a = jnp.exp(lsp + _split_dot(lm, u_after))
            if diagonal:
                a = jnp.where(m, a, 0.0)
            g = a * lax.dot_general(dos, vb, NT_DIMS, preferred_element_type=f32)
            bb = jnp.exp(lsp)
            aa = g * jnp.exp(lsn) - _split_dot(g, u_before) * bb
            if diagonal:
                aa = jnp.where(m, aa, 0.0)
                bb = jnp.where(m, bb, 0.0)
            cl = cl + jnp.sum(lm, axis=-1, keepdims=True)
            f = jnp.exp(tots - cl)
            dz = (f * aa - cg * bb).astype(bf16)
            cg = cg + f * jnp.sum(g, axis=-1, keepdims=True)
            dq = dq + jnp.dot(dz, kb, preferred_element_type=f32)
            dk_ref[pl.ds(ks, BK), :] += lax.dot_general(dz, qs, TN_DIMS, preferred_element_type=f32)
            dv_ref[pl.ds(ks, BK), :] += lax.dot_general((f * a).astype(bf16), dos, TN_DIMS, preferred_element_type=f32)
            return dq, cl, cg

        zc = jnp.zeros((2 * BQ, 1), f32)
        carry = lax.fori_loop(0, i, lambda jb, cr: step(jb, cr, False), (jnp.zeros((2 * BQ, 128), f32), zc, zc))
        dq, _, _ = step(i, carry, True)
        dq_ref[...] = _unstack_heads(dq)
        if n_s:
            @pl.when(jnp.logical_and(p == 3, i == nq - 1))
            def _():
                for cp in copies():
                    cp.wait()

    blk = pl.BlockSpec((BQ, 128), lambda p, i: (i, p))
    full = pl.BlockSpec((T, 128), lambda p, i: (0, p))
    shp = jax.ShapeDtypeStruct((T, 512), f32)
    s_list = [] if scatter is None else list(scatter[0])
    bufs = chip_buffers(s_list, modes) if n_s else []
    res = pl.pallas_call(
        body, name="attn_bwd_scatter" if n_s else "attn_bwd", grid=(4, nq),
        in_specs=[blk, full, full, blk, blk] + [_ANY] * (2 * n_s), out_specs=[blk, full, full] + [_ANY] * n_s,
        out_shape=[shp, shp, shp] + [jax.ShapeDtypeStruct(b.shape, b.dtype) for b in bufs],
        input_output_aliases={5 + n_s + k: 3 + k for k in range(n_s)},
        scratch_shapes=_sems(n_s, 3) if n_s else [], compiler_params=_cparams(2),
    )(q, k, v, do, tot, *s_list, *bufs)
    return res[0], res[1], res[2], list(res[3:])


def loss_head(y, target, T, tm=256):
    D = y.shape[1]

    def body(y_ref, t_ref, dy_ref, l_ref):
        i = pl.program_id(0)
        err = y_ref[...] - t_ref[...]
        dy_ref[...] = err * (1.0 / D)
        part = 0.5 * jnp.sum(jnp.sum(err * err, axis=-1, keepdims=True) * (1.0 / D), axis=0, keepdims=True)

        @pl.when(i == 0)
        def _():
            l_ref[...] = jnp.zeros_like(l_ref)

        l_ref[...] += jnp.broadcast_to(part, l_ref.shape)

    spec = pl.BlockSpec((tm, D), lambda i: (i, 0))
    return pl.pallas_call(
        body, name="loss_head", grid=(T // tm,), in_specs=[spec, spec],
        out_specs=[spec, pl.BlockSpec((1, 128), lambda i: (0, 0))],
        out_shape=[jax.ShapeDtypeStruct((T, D), f32), jax.ShapeDtypeStruct((1, 128), f32)],
        compiler_params=_cparams(1),
    )(y, target)


def _row_tile(rows, cols, offset=0, max_elems=128 * 1024):
    best = None
    for t in range(16, rows + 1, 16):
        if rows % t == 0 and offset % t == 0 and t * cols <= max_elems:
            best = t
    return best if best is not None else rows


def adamw(name, w, m, v, parts, layer, prev=None, row_off=0):
    L, R, C = w.shape
    n_parts = parts.shape[0]
    tr = _row_tile(R, C, row_off, max_elems=256 * 1024)
    assert R % tr == 0 and row_off % tr == 0, (name, R, row_off, tr)

    def body(w_ref, m_ref, v_ref, p_ref, *rest):
        g_ref, d_ref, nm_ref, nv_ref = rest[-4:]
        g = p_ref[0].astype(f32)
        for s in range(1, n_parts):
            g = g + p_ref[s].astype(f32)
        wv = w_ref[...]
        mn = ADAM_B1 * m_ref[...] + (1.0 - ADAM_B1) * g
        vn = ADAM_B2 * v_ref[...] + (1.0 - ADAM_B2) * jnp.square(g)
        m_hat = mn / (1.0 - ADAM_B1 ** ADAM_STEP)
        v_hat = vn / (1.0 - ADAM_B2 ** ADAM_STEP)
        g_ref[...] = g
        d_ref[...] = -ADAM_LR * (m_hat / (jnp.sqrt(v_hat) + ADAM_EPS) + ADAM_WD * wv)
        nm_ref[...] = mn
        nv_ref[...] = vn

    spec = pl.BlockSpec((None, tr, C), lambda i: (layer, i, 0))
    shp = jax.ShapeDtypeStruct((L, R, C), f32)
    n_prev = 0 if prev is None else 4
    return pl.pallas_call(
        body, name=name, grid=(R // tr,),
        in_specs=[spec, spec, spec, pl.BlockSpec((n_parts, tr, C), lambda i: (0, row_off // tr + i, 0))] + [_ANY] * n_prev,
        out_specs=[spec] * 4, out_shape=[shp] * 4, input_output_aliases={4 + k: k for k in range(n_prev)},
        compiler_params=_cparams(1),
    )(w, m, v, parts, *(prev or ()))


REPL = ["ln1_g", "sg_ln_g", "sg_ln_b", "sg_w", "sg_b", "cv_b", "cv_ln_g", "cv_ln_b", "q_norm_g", "k_norm_g", "ln2_g",
        "ffn_conv_b"]
SMALL_SHARDED = ["b_gate", "cv_w", "ffn_conv_w"]
BIG = ["w_in", "w_a_out", "w_b_out", "w_c_out", "w_up", "w_out", "w_down"]
COL_SHARDED = ["w_a_out", "w_b_out", "w_c_out"]
TRANSPOSED = ["w_in", "w_up"]


def _pack(arrs, rows):
    flat = jnp.concatenate([a.reshape(-1) for a in arrs])
    return jnp.pad(flat, (0, rows * 128 - flat.shape[0])).reshape(rows, 128)


def _pack_layers(arrs, rows):
    flat = jnp.concatenate([a.reshape(a.shape[0], -1) for a in arrs], axis=1)
    return jnp.pad(flat, ((0, 0), (0, rows * 128 - flat.shape[1]))).reshape(flat.shape[0], rows, 128)


def _unpack_layers(packed, shapes):
    flat = packed.reshape(packed.shape[0], -1)
    out, pos = [], 0
    for s in shapes:
        n = 1
        for d in s[1:]:
            n *= d
        out.append(flat[:, pos:pos + n].reshape(s))
        pos += n
    return out


def _rows_for(shapes, mult):
    n = 0
    for s in shapes:
        k = 1
        for d in s:
            k *= d
        n += k
    rows = -(-n // 128)
    return -(-rows // mult) * mult


W_NAMES = ['ln1_g', 'w_in', 'b_gate', 'sg_ln_g', 'sg_ln_b', 'sg_w', 'sg_b', 'w_a_out', 'cv_w', 'cv_b', 'cv_ln_g', 'cv_ln_b',
           'w_b_out', 'q_norm_g', 'k_norm_g', 'w_c_out', 'w_out', 'ln2_g', 'w_up', 'ffn_conv_w', 'ffn_conv_b', 'w_down']


def _forward_layer(x, P, T, late=None):
    D = x.shape[1]
    sv = {"x0": x}
    (h1,) = rowwise("rms_fwd", f_rms, [(x, D, 0)], [(P["ln1_g"], (1, D), None)], [(D, D, 0, bf16)], T, ROW_TILE)
    z = mm_nt("in_proj", h1, P["w_in_t"], bf16, tko=1664, tc=D)
    sv["h1"], sv["z"] = h1, z
    (ya_in,) = rowwise("sgu_fwd", f_sgu, [(z, 512, 0), (z, 512, 1)],
                       [(P["sg_ln_g"], (1, 512), None), (P["sg_ln_b"], (1, 512), None),
                        (P["sg_w"], (8, CHUNK, CHUNK), None), (P["sg_b_t"], (CHUNK, 8), None)],
                       [(512, 512, 0, bf16)], T, CHUNK)
    (c0,) = rowwise("glu_fwd", f_glu, [(z, 512, 2), (z, 512, 3)], [], [(512, 512, 0, f32)], T, ROW_TILE)
    c1 = conv31_fwd(c0, P["cv_w"], P["cv_b"], T)
    (c3,) = rowwise("lnsilu_fwd", f_lnsilu, [(c1, 512, 0)], [(P["cv_ln_g"], (1, 512), None), (P["cv_ln_b"], (1, 512), None)],
                    [(512, 512, 0, bf16)], T, ROW_TILE)
    q8, kn, vb = rowwise("qkv_fwd", f_qkv, [(z, 512, 4), (z, 512, 5), (z, 512, 6)],
                         [(P["q_norm_g"], (1, HEAD_DIM), None), (P["k_norm_g"], (1, HEAD_DIM), None)],
                         [(512, 512, 0, bf16)] * 3, T, ROW_TILE)
    o, tot, _ = attn_fwd(q8, kn, vb, T)
    if late is not None:
        P.update(late(o))
    ya, yb, yc = mm_many("branch_out", [ya_in, c3, o], [P["w_a_out"], P["w_b_out"], P["w_c_out"]], bf16)
    (merged,) = rowwise("merge_fwd", f_merge,
                        [(z, 512, lambda c: 7 + c), (z, 512, lambda c: 9 + c), (z, 512, lambda c: 11 + c),
                         (ya, 512, lambda c: c), (yb, 512, lambda c: c), (yc, 512, lambda c: c)],
                        [(P["b_gate"][k], (1, 512), lambda c: c) for k in range(3)],
                        [(D, 512, lambda c: c, bf16)], T, 256, ncol=2)
    x1 = mm_nn("out_proj", merged, P["w_out"], f32, add=x)
    sv.update(ya_in=ya_in, ya=ya, c0=c0, c1=c1, c3=c3, yb=yb, q8=q8, kn=kn, vb=vb, o=o, tot=tot, yc=yc, merged=merged, x1=x1)
    (h2,) = rowwise("rms_fwd", f_rms, [(x1, D, 0)], [(P["ln2_g"], (1, D), None)], [(D, D, 0, bf16)], T, ROW_TILE)
    up = mm_nt("up_proj", h2, P["w_up_t"], bf16, tko=1408, tc=D)
    act = ffn_act_fwd(up, P["ffn_conv_w"], P["ffn_conv_b"], T)
    x2 = mm_nn("down_proj", act, P["w_down"], f32, add=x1)
    sv.update(h2=h2, up=up, act=act)
    return x2, sv


def _backward_layer(dx2, P, sv, T, scatter=None):
    D = dx2.shape[1]
    G = {}
    G["w_down"] = mm_tn("dw_down", sv["act"], dx2, tk=1408)
    dact = mm_nt("d_act", dx2, P["w_down"], f32, tko=1408, tc=D)
    dup, dcw0, dcw1, dcw2, G["ffn_conv_b"] = ffn_act_bwd(sv["up"], P["ffn_conv_w"], P["ffn_conv_b"], dact, T)
    G["ffn_conv_w"] = jnp.concatenate([dcw0, dcw1, dcw2], axis=0)
    G["w_up"] = mm_tn("dw_up", dup, sv["h2"], tk=1408)
    dh2 = mm_nn("d_h2", dup, P["w_up_t"], f32, tm=256, tn=D)
    dx1, G["ln2_g"] = rowwise_bwd("rms_bwd", f_rms, [(sv["x1"], D, 0)], [(P["ln2_g"], (1, D), None)], [(dh2, D, 0)],
                                  [f32], T, ROW_TILE, adds=[(dx2, D, 0)])
    G["w_out"] = mm_tn("dw_out", sv["merged"], dx1, tk=1024)
    dmerged = mm_nt("d_merged", dx1, P["w_out"], f32)
    z = sv["z"]
    dg0, dg1, dg2, dya, dyb, dyc, db0, db1, db2 = rowwise_bwd(
        "merge_bwd", f_merge,
        [(z, 512, lambda c: 7 + c), (z, 512, lambda c: 9 + c), (z, 512, lambda c: 11 + c),
         (sv["ya"], 512, lambda c: c), (sv["yb"], 512, lambda c: c), (sv["yc"], 512, lambda c: c)],
        [(P["b_gate"][k], (1, 512), lambda c: c) for k in range(3)],
        [(dmerged, 512, lambda c: c)], [bf16] * 6, T, 256, ncol=2)
    G["b_gate"] = jnp.concatenate([db0, db1, db2], axis=0)
    dya_in, dc3, do = mm_many("d_branch_in", [dya, dyb, dyc], [P["w_a_out"], P["w_b_out"], P["w_c_out"]], f32,
                              transpose_w=True)
    G["w_c_out"] = mm_tn("dw_c_out", sv["o"], dyc, n_slab=CHUNK)
    dq8, dkn, dvb, received = attn_bwd(sv["q8"], sv["kn"], sv["vb"], do, sv["tot"], T, scatter=scatter)
    dzq, dzk, dzv, G["q_norm_g"], G["k_norm_g"] = rowwise_bwd(
        "qkv_bwd", f_qkv, [(z, 512, 4), (z, 512, 5), (z, 512, 6)],
        [(P["q_norm_g"], (1, HEAD_DIM), None), (P["k_norm_g"], (1, HEAD_DIM), None)],
        [(dq8, 512, 0), (dkn, 512, 0), (dvb, 512, 0)], [bf16] * 3, T, ROW_TILE)
    G["w_b_out"] = mm_tn("dw_b_out", sv["c3"], dyb, n_slab=CHUNK)
    dc1, G["cv_ln_g"], G["cv_ln_b"] = rowwise_bwd(
        "lnsilu_bwd", f_lnsilu, [(sv["c1"], 512, 0)], [(P["cv_ln_g"], (1, 512), None), (P["cv_ln_b"], (1, 512), None)],
        [(dc3, 512, 0)], [f32], T, ROW_TILE)
    dc0, G["cv_w"], G["cv_b"] = conv31_bwd(sv["c0"], P["cv_w"], dc1, T)
    dzp, dzgl = rowwise_bwd("glu_bwd", f_glu, [(z, 512, 2), (z, 512, 3)], [], [(dc0, 512, 0)], [bf16] * 2, T, ROW_TILE)
    G["w_a_out"] = mm_tn("dw_a_out", sv["ya_in"], dya, n_slab=CHUNK)
    dzu, dzv_a, G["sg_ln_g"], G["sg_ln_b"], G["sg_w"], dsbt = rowwise_bwd(
        "sgu_bwd", f_sgu, [(z, 512, 0), (z, 512, 1)],
        [(P["sg_ln_g"], (1, 512), None), (P["sg_ln_b"], (1, 512), None), (P["sg_w"], (8, CHUNK, CHUNK), None),
         (P["sg_b_t"], (CHUNK, 8), None)],
        [(dya_in, 512, 0)], [bf16] * 2, T, CHUNK)
    G["sg_b"] = dsbt.T
    dz = concat_cols("dz_concat", [dzu, dzv_a, dzp, dzgl, dzq, dzk, dzv, dg0, dg1, dg2], T)
    G["w_in"] = mm_tn("dw_in", dz, sv["h1"], tk=1664)
    dh1 = mm_nn("d_h1", dz, P["w_in_t"], f32, tm=256, tn=D)
    dx0, G["ln1_g"] = rowwise_bwd("rms_bwd", f_rms, [(sv["x0"], D, 0)], [(P["ln1_g"], (1, D), None)], [(dh1, D, 0)],
                                  [f32], T, ROW_TILE, adds=[(dx1, D, 0)])
    return dx0, G, received


def kernel(x, ln1_g, w_in, b_gate, sg_ln_g, sg_ln_b, sg_w, sg_b, w_a_out, cv_w, cv_b, cv_ln_g, cv_ln_b, w_b_out, q_norm_g, k_norm_g, w_c_out, w_out, ln2_g, w_up, ffn_conv_w, ffn_conv_b, w_down, loss_target, m_ln1_g, m_w_in, m_b_gate, m_sg_ln_g, m_sg_ln_b, m_sg_w, m_sg_b, m_w_a_out, m_cv_w, m_cv_b, m_cv_ln_g, m_cv_ln_b, m_w_b_out, m_q_norm_g, m_k_norm_g, m_w_c_out, m_w_out, m_ln2_g, m_w_up, m_ffn_conv_w, m_ffn_conv_b, m_w_down, v_ln1_g, v_w_in, v_b_gate, v_sg_ln_g, v_sg_ln_b, v_sg_w, v_sg_b, v_w_a_out, v_cv_w, v_cv_b, v_cv_ln_g, v_cv_ln_b, v_w_b_out, v_q_norm_g, v_k_norm_g, v_w_c_out, v_w_out, v_ln2_g, v_w_up, v_ffn_conv_w, v_ffn_conv_b, v_w_down):
    W = dict(ln1_g=ln1_g, w_in=w_in, b_gate=b_gate, sg_ln_g=sg_ln_g, sg_ln_b=sg_ln_b, sg_w=sg_w, sg_b=sg_b, w_a_out=w_a_out,
             cv_w=cv_w, cv_b=cv_b, cv_ln_g=cv_ln_g, cv_ln_b=cv_ln_b, w_b_out=w_b_out, q_norm_g=q_norm_g, k_norm_g=k_norm_g,
             w_c_out=w_c_out, w_out=w_out, ln2_g=ln2_g, w_up=w_up, ffn_conv_w=ffn_conv_w, ffn_conv_b=ffn_conv_b, w_down=w_down)
    M = dict(ln1_g=m_ln1_g, w_in=m_w_in, b_gate=m_b_gate, sg_ln_g=m_sg_ln_g, sg_ln_b=m_sg_ln_b, sg_w=m_sg_w, sg_b=m_sg_b,
             w_a_out=m_w_a_out, cv_w=m_cv_w, cv_b=m_cv_b, cv_ln_g=m_cv_ln_g, cv_ln_b=m_cv_ln_b, w_b_out=m_w_b_out,
             q_norm_g=m_q_norm_g, k_norm_g=m_k_norm_g, w_c_out=m_w_c_out, w_out=m_w_out, ln2_g=m_ln2_g, w_up=m_w_up,
             ffn_conv_w=m_ffn_conv_w, ffn_conv_b=m_ffn_conv_b, w_down=m_w_down)
    V = dict(ln1_g=v_ln1_g, w_in=v_w_in, b_gate=v_b_gate, sg_ln_g=v_sg_ln_g, sg_ln_b=v_sg_ln_b, sg_w=v_sg_w, sg_b=v_sg_b,
             w_a_out=v_w_a_out, cv_w=v_cv_w, cv_b=v_cv_b, cv_ln_g=v_cv_ln_g, cv_ln_b=v_cv_ln_b, w_b_out=v_w_b_out,
             q_norm_g=v_q_norm_g, k_norm_g=v_k_norm_g, w_c_out=v_w_c_out, w_out=v_w_out, ln2_g=v_ln2_g, w_up=v_w_up,
             ffn_conv_w=v_ffn_conv_w, ffn_conv_b=v_ffn_conv_b, w_down=v_w_down)
    T, D = x.shape[1], x.shape[2]
    L = DEPTH
    xs = x.reshape(T, D)
    target = loss_target.reshape(T, D)

    ss_shapes = [W[n].shape for n in SMALL_SHARDED]
    ss_rows = _rows_for(ss_shapes, 8)
    (ss_all,), small_token = exchange("gather_small", [_pack([W[n] for n in SMALL_SHARDED], ss_rows)], ["gather"])
    full_small = {}
    pos = 0
    for n in SMALL_SHARDED:
        s = W[n].shape
        cnt = s[0] * s[1] * s[2]
        part = ss_all.reshape(N_DEV, -1)[:, pos:pos + cnt].reshape((N_DEV,) + s)
        full_small[n] = jnp.transpose(part, (1, 2, 0, 3)).reshape(s[0], s[1], N_DEV * s[2])
        pos += cnt

    params, saved = [], []
    cur = xs
    Wt = {n: (jnp.transpose(W[n], (0, 2, 1)) if n in TRANSPOSED else W[n]) for n in BIG}

    def local_slabs(l):
        return [Wt[n][l].astype(bf16) for n in BIG]

    def weights_of(names, slabs):
        out = {}
        for n, s in zip(names, slabs):
            if n in TRANSPOSED:
                out[n + "_t"] = s.reshape(-1, s.shape[-1])
            else:
                out[n] = assemble("assemble_" + n, s) if n in COL_SHARDED else s.reshape(-1, s.shape[-1])
        return out

    LATE = [n for n in BIG if n != "w_in"]
    def behind(token, blocks):
        return [blocks[0] + token[0, 0].astype(blocks[0].dtype)] + list(blocks[1:])

    first, token = gather2("gather_w_in", behind(small_token, [local_slabs(0)[0]]))
    early_part, late_part = {}, {}
    for l in range(L):
        if l > 0:
            early_part[l] = gather_start("gather_start_in_%d" % l, behind(token, local_slabs(l)[:1]))
            token = early_part[l][3]
        late_part[l] = gather_start("gather_start_rest_%d" % l, behind(token, local_slabs(l)[1:]))
        token = late_part[l][3]
    start_token = token[0, 0]

    def late_weights(l):
        send_sems, recv_sems, bufs, _ = late_part[l]
        return lambda o: weights_of(LATE, gather_wait("gather_wait_rest_%d" % l, send_sems, recv_sems, bufs, o))

    for l in range(L):
        if l == 0:
            P = weights_of(["w_in"], first)
        else:
            send_sems, recv_sems, bufs, _ = early_part[l]
            P = weights_of(["w_in"], gather_wait("gather_wait_in_%d" % l, send_sems, recv_sems, bufs, cur))
        late = late_weights(l)
        for n in REPL:
            P[n] = W[n][l]
        for n in ("ln1_g", "sg_ln_g", "sg_ln_b", "cv_b", "cv_ln_g", "cv_ln_b", "q_norm_g", "k_norm_g", "ln2_g", "ffn_conv_b"):
            P[n] = P[n].reshape(1, -1)
        P["sg_b_t"] = P["sg_b"].T
        P["cv_w"] = full_small["cv_w"][l]
        P["b_gate"] = [full_small["b_gate"][l][k:k + 1] for k in range(3)]
        P["ffn_conv_w"] = [full_small["ffn_conv_w"][l][k:k + 1] for k in range(3)]
        if l == 0:
            P["ln1_g"] = P["ln1_g"] + start_token
        cur, sv = _forward_layer(cur, P, T, late=late)
        params.append(P)
        saved.append(sv)

    dy, loss_row = loss_head(cur, target, T)
    loss = lax.psum(loss_row[0, 0], ("x", "y", "c"))

    repl_shapes = [W[n].shape for n in REPL]
    repl_rows = _rows_for([s[1:] for s in repl_shapes], 16)
    ssl_rows = _rows_for([s[1:] for s in ss_shapes], 16)
    state = {"repl": [_pack_layers([X[n] for n in REPL], repl_rows) for X in (W, M, V)],
             "ss": [_pack_layers([X[n] for n in SMALL_SHARDED], ssl_rows) for X in (W, M, V)]}
    WIDE = [n for n in BIG if n not in COL_SHARDED]
    big_rows, big_off = {}, {}
    for n in WIDE:
        big_off[n] = sum(big_rows.values())
        big_rows[n] = W[n].shape[1] * W[n].shape[2] // D
        state[n] = [(jnp.transpose(X[n], (0, 2, 1)) if n in TRANSPOSED else X[n]).reshape(L, big_rows[n], D)
                    for X in (W, M, V)]
    for n in COL_SHARDED:
        state[n] = [W[n], M[n], V[n]]
    done = {n: None for n in state}

    def update(layer, received):
        for n in WIDE:
            done[n] = adamw("adamw_" + n, *state[n], received[0], layer, done[n], row_off=big_off[n])
        done["repl"] = adamw("adamw_repl", *state["repl"], received[1], layer, done["repl"])
        done["ss"] = adamw("adamw_ss", *state["ss"], received[2], layer, done["ss"])
        for k, n in enumerate(COL_SHARDED):
            done[n] = adamw("adamw_" + n, *state[n], received[3], layer, done[n], row_off=k * W[n].shape[1])

    def reduce_in_chip(G):
        big = [G[n].reshape(4, 2, big_rows[n], D) for n in WIDE]
        narrow = jnp.concatenate([G[n].reshape((4, 2) + G[n].shape[1:]) for n in COL_SHARDED], axis=2)
        ss_parts = []
        for n in SMALL_SHARDED:
            k, c = W[n].shape[1:]
            ss_parts.append(jnp.transpose(G[n].reshape(k, N_DEV, c), (1, 0, 2)).reshape(N_DEV, k * c))
        ss_send = jnp.concatenate(ss_parts, axis=1)
        ss_send = jnp.pad(ss_send, ((0, 0), (0, ssl_rows * 128 - ss_send.shape[1]))).reshape(4, 2, ssl_rows, 128)
        repl = _pack([G[n] for n in REPL], repl_rows)
        core = lax.axis_index("c")
        ss_kept = lax.dynamic_index_in_dim(ss_send, core, 1, keepdims=False)
        narrow_kept = lax.dynamic_index_in_dim(narrow, core, 1, keepdims=False)
        got, repl_got, ss_got, narrow_got = pair_exchange("pair_exchange", big, [repl, ss_send, narrow],
                                                          ["gather", "scatter", "scatter"])
        return [pair_add_wide("pair_add_big", big, got), add2("pair_add_repl", repl, repl_got), add2("pair_add_ss", ss_kept, ss_got),
                add2("pair_add_narrow", narrow_kept, narrow_got)]

    chip_modes = ["scatter", "gather", "scatter", "scatter"]
    dcur = dy
    pending = None
    for l in reversed(range(L)):
        P = params[l]
        if pending is not None:
            P = dict(P, ffn_conv_b=P["ffn_conv_b"] + pending[4][0, 0])
        dcur, G, _ = _backward_layer(dcur, P, saved[l], T)
        started = chip_start("chip_start_%d" % l, reduce_in_chip(G), chip_modes)
        if pending is not None:
            after = started[4] + dcur[0:8, 0:128]
            update(l + 1, chip_wait("chip_wait_%d" % (l + 1), *pending[:4], chip_modes, after))
        pending = started
    after = sum(done[n][0][1, 0:1, 0:1] for n in state)
    update(0, chip_wait("chip_wait_0", *pending[:4], chip_modes, after))

    results = [{}, {}, {}, {}]
    for k in range(4):
        for n, a in zip(REPL, _unpack_layers(done["repl"][k], repl_shapes)):
            results[k][n] = a
        for n, a in zip(SMALL_SHARDED, _unpack_layers(done["ss"][k], ss_shapes)):
            results[k][n] = a
        for n in BIG:
            results[k][n] = jnp.transpose(done[n][k], (0, 2, 1)) if n in TRANSPOSED else done[n][k].reshape(W[n].shape)
    out = [loss, dcur.reshape(1, T, D)]
    for k in range(4):
        out += [results[k][n] for n in W_NAMES]
    return tuple(out)
```

```python
import functools

import jax
import jax.numpy as jnp
from jax import lax
from jax.experimental import pallas as pl
from jax.experimental.pallas import tpu as pltpu

f32 = jnp.float32
bf16 = jnp.bfloat16

EPS = 1e-6
N_DEV = 8
DEPTH = 4
CHUNK = 128
HEAD_DIM = 64
N_HEADS = 8
CV_KERNEL = 31
VMEM_LIMIT_BYTES = 56 * 2 ** 20

ADAM_LR = 0.001
ADAM_B1 = 0.9
ADAM_B2 = 0.999
ADAM_EPS = 1e-08
ADAM_WD = 0.01
ADAM_STEP = 10

MESH = pl.DeviceIdType.MESH


def _cparams(n_grid):
    return pltpu.CompilerParams(dimension_semantics=("arbitrary",) * n_grid, vmem_limit_bytes=VMEM_LIMIT_BYTES)


def exchange(name, arrays, modes):
    n = len(arrays)

    def body(*refs):
        copies = _direct_copies(refs[:n], refs[n:2 * n], modes, *refs[2 * n + 1:])
        for cp in copies:
            cp.start()
        for cp in copies:
            cp.wait()
        refs[2 * n][...] = jnp.zeros_like(refs[2 * n])

    res = pl.pallas_call(
        body, name=name, out_shape=_exchange_out_shapes(arrays, modes) + [jax.ShapeDtypeStruct((8, 128), f32)],
        in_specs=[_ANY] * n, out_specs=[_ANY] * n + [pl.BlockSpec(memory_space=pltpu.VMEM)],
        scratch_shapes=_exchange_sems(n),
    )(*arrays)
    return list(res[:n]), res[n]


_ANY = pl.BlockSpec(memory_space=pl.ANY)


def _exchange_out_shapes(arrays, modes):
    return [jax.ShapeDtypeStruct((N_DEV,) + tuple(a.shape) if m == "gather" else tuple(a.shape), a.dtype)
            for a, m in zip(arrays, modes)]


def _exchange_sems(n):
    return [pltpu.SemaphoreType.DMA((n, N_DEV - 1)), pltpu.SemaphoreType.DMA((n, N_DEV - 1)), pltpu.SemaphoreType.DMA((n,))]


def _direct_copies(ins, outs, modes, send_sems, recv_sems, local_sems):
    x, y, c = lax.axis_index("x"), lax.axis_index("y"), lax.axis_index("c")
    me = 4 * x + 2 * y + c
    copies = []
    for k in range(len(ins)):
        src_mine = ins[k] if modes[k] == "gather" else ins[k].at[me]
        copies.append(pltpu.make_async_copy(src_mine, outs[k].at[me], local_sems.at[k]))
        for r in range(1, N_DEV):
            px = 1 - x if r & 4 else x
            py = 1 - y if r & 2 else y
            pc = 1 - c if r & 1 else c
            src = ins[k] if modes[k] == "gather" else ins[k].at[4 * px + 2 * py + pc]
            copies.append(pltpu.make_async_remote_copy(
                src_ref=src, dst_ref=outs[k].at[me], send_sem=send_sems.at[k, r - 1], recv_sem=recv_sems.at[k, r - 1],
                device_id=(px, py, pc), device_id_type=MESH))
    return copies


def own_slot_buffer(block, n_slots, index):
    return lax.dynamic_update_slice(lax.empty((n_slots,) + tuple(block.shape), block.dtype), block[None],
                                    (index,) + (0,) * block.ndim)


def _sems(n, m):
    return [pltpu.SemaphoreType.DMA((n, m)), pltpu.SemaphoreType.DMA((n, m))]


def _two_level_gather(bufs, send_sems, recv_sems):
    x, y, c = lax.axis_index("x"), lax.axis_index("y"), lax.axis_index("c")
    me = 4 * x + 2 * y + c
    sibling = (x, y, 1 - c)
    chips = [(1 - x, y), (x, 1 - y), (1 - x, 1 - y)]

    def slot(px, py, pc):
        return 4 * px + 2 * py + pc

    def copy(k, sem, block, to):
        rows = bufs[k].at[block]
        return pltpu.make_async_remote_copy(src_ref=rows, dst_ref=rows, send_sem=send_sems.at[k, sem],
                                            recv_sem=recv_sems.at[k, sem], device_id=to, device_id_type=MESH)

    def first(k):
        return [copy(k, 0, me, sibling)] + [copy(k, 1 + j, me, (*chip, c)) for j, chip in enumerate(chips)]

    def passed_on(k, j):
        return copy(k, 4 + j, slot(*chips[j], c), sibling)

    def start():
        for k in range(len(bufs)):
            for cp in first(k):
                cp.start()

    def forward():
        for k in range(len(bufs)):
            for j in range(3):
                copy(k, 1 + j, slot(*chips[j], c), sibling).wait_recv()
                passed_on(k, j).start()

    def finish():
        for k in range(len(bufs)):
            copy(k, 0, slot(x, y, 1 - c), sibling).wait_recv()
            for j, chip in enumerate(chips):
                copy(k, 4 + j, slot(*chip, 1 - c), sibling).wait_recv()
            for cp in first(k) + [passed_on(k, j) for j in range(3)]:
                cp.wait_send()

    return start, forward, finish


def pair_exchange(name, big, smalls, modes):
    nb, ns = len(big), len(smalls)
    rows = [a.shape[2] for a in big]
    total, width = sum(rows), big[0].shape[3]
    n = nb + ns

    def body(*refs):
        ins, outs = refs[:n], refs[n:n + 1 + ns]
        send_sems, recv_sems = refs[-2:]
        x, y, c = lax.axis_index("x"), lax.axis_index("y"), lax.axis_index("c")
        copies = []

        def remote(k, src, dst):
            copies.append(pltpu.make_async_remote_copy(src_ref=src, dst_ref=dst, send_sem=send_sems.at[k, 0],
                                                       recv_sem=recv_sems.at[k, 0], device_id=(x, y, 1 - c),
                                                       device_id_type=MESH))

        off = 0
        for k in range(nb):
            remote(k, ins[k].at[:, 1 - c], outs[0].at[:, pl.ds(off, rows[k]), :])
            off += rows[k]
        for j in range(ns):
            remote(nb + j, ins[nb + j].at[:, 1 - c] if modes[j] == "scatter" else ins[nb + j], outs[1 + j])
        for cp in copies:
            cp.start()
        for cp in copies:
            cp.wait()

    out_shape = [jax.ShapeDtypeStruct((4, total, width), big[0].dtype)]
    for a, m in zip(smalls, modes):
        out_shape.append(jax.ShapeDtypeStruct((4,) + tuple(a.shape[2:]) if m == "scatter" else tuple(a.shape), a.dtype))
    return pl.pallas_call(
        body, name=name, out_shape=out_shape, in_specs=[_ANY] * n, out_specs=[_ANY] * len(out_shape),
        scratch_shapes=_sems(n, 1),
    )(*big, *smalls)


def add2(name, a, b):
    shape = a.shape
    a2, b2 = a.reshape(-1, shape[-1]), b.reshape(-1, shape[-1])
    R, C = a2.shape
    tr = _row_tile(R, C, max_elems=1200 * 1024)

    def body(a_ref, b_ref, o_ref):
        o_ref[...] = (a_ref[...].astype(f32) + b_ref[...].astype(f32)).astype(o_ref.dtype)

    spec = pl.BlockSpec((tr, C), lambda i: (i, 0))
    return pl.pallas_call(
        body, name=name, grid=(R // tr,), in_specs=[spec, spec], out_specs=spec,
        out_shape=jax.ShapeDtypeStruct((R, C), a.dtype), compiler_params=_cparams(1),
    )(a2, b2).reshape(shape)


def pair_add_wide(name, big, got):
    nb = len(big)
    rows = [a.shape[2] for a in big]
    total, width = sum(rows), big[0].shape[3]
    core = lax.axis_index("c").astype(jnp.int32).reshape(1)

    def body(core_ref, *refs):
        got_ref, o_ref = refs[nb], refs[nb + 1]
        off = 0
        for k in range(nb):
            part = slice(off, off + rows[k])
            o_ref[part, :] = (refs[k][...].astype(f32) + got_ref[part, :].astype(f32)).astype(o_ref.dtype)
            off += rows[k]

    whole = pl.BlockSpec((None, total, width), lambda i, core_ref: (i, 0, 0))
    grid_spec = pltpu.PrefetchScalarGridSpec(
        num_scalar_prefetch=1, grid=(4,),
        in_specs=[pl.BlockSpec((None, None, r, width), lambda i, core_ref: (i, core_ref[0], 0, 0)) for r in rows] + [whole],
        out_specs=whole)
    return pl.pallas_call(
        body, name=name, grid_spec=grid_spec, out_shape=jax.ShapeDtypeStruct((4, total, width), got.dtype),
        compiler_params=_cparams(1),
    )(core, *big, got)


def chip_buffers(arrays, modes):
    mine = 2 * lax.axis_index("x") + lax.axis_index("y")
    return [own_slot_buffer(lax.dynamic_index_in_dim(a, mine, 0, keepdims=False) if m == "scatter" else a, 4, mine)
            for a, m in zip(arrays, modes)]


def _chip_copies(ins, bufs, modes, send_sems, recv_sems):
    x, y, c = lax.axis_index("x"), lax.axis_index("y"), lax.axis_index("c")
    mine = 2 * x + y
    copies = []
    for k in range(len(ins)):
        for r in range(1, 4):
            px = 1 - x if r & 2 else x
            py = 1 - y if r & 1 else y
            src = ins[k].at[2 * px + py] if modes[k] == "scatter" else ins[k]
            copies.append(pltpu.make_async_remote_copy(
                src_ref=src, dst_ref=bufs[k].at[mine], send_sem=send_sems.at[k, r - 1], recv_sem=recv_sems.at[k, r - 1],
                device_id=(px, py, c), device_id_type=MESH))
    return copies


def chip_exchange(name, arrays, modes):
    n = len(arrays)

    def body(*refs):
        copies = _chip_copies(refs[:n], refs[2 * n:3 * n], modes, *refs[3 * n:])
        for cp in copies:
            cp.start()
        for cp in copies:
            cp.wait()

    bufs = chip_buffers(arrays, modes)
    return pl.pallas_call(
        body, name=name, out_shape=[jax.ShapeDtypeStruct(b.shape, b.dtype) for b in bufs],
        in_specs=[_ANY] * (2 * n), out_specs=[_ANY] * n, input_output_aliases={n + k: k for k in range(n)},
        scratch_shapes=_sems(n, 3),
    )(*arrays, *bufs)


def gather_buffers(blocks):
    me = 4 * lax.axis_index("x") + 2 * lax.axis_index("y") + lax.axis_index("c")
    return [own_slot_buffer(b, N_DEV, me) for b in blocks]


_HBM = pl.BlockSpec(memory_space=pltpu.HBM)
_SEM = pl.BlockSpec(memory_space=pltpu.SEMAPHORE)
_SPLIT_PARAMS = pltpu.CompilerParams(has_side_effects=pltpu.SideEffectType.DATAFLOW_SIDE_EFFECTING)


def _split_gather_copies(bufs, send_sems, recv_sems):
    x, y, c = lax.axis_index("x"), lax.axis_index("y"), lax.axis_index("c")
    me = 4 * x + 2 * y + c
    copies = []
    for k in range(len(bufs)):
        rows = bufs[k].at[me]
        for r in range(1, N_DEV):
            peer = (1 - x if r & 4 else x, 1 - y if r & 2 else y, 1 - c if r & 1 else c)
            copies.append(pltpu.make_async_remote_copy(
                src_ref=rows, dst_ref=rows, send_sem=send_sems.at[(N_DEV - 1) * k + r - 1],
                recv_sem=recv_sems.at[(N_DEV - 1) * k + r - 1], device_id=peer, device_id_type=MESH))
    return copies


def gather_start(name, blocks):
    n = len(blocks)
    bufs = gather_buffers(blocks)

    def body(*refs):
        for cp in _split_gather_copies(refs[n + 2:2 * n + 2], refs[n], refs[n + 1]):
            cp.start()
        refs[2 * n + 2][...] = jnp.zeros_like(refs[2 * n + 2])

    n_sem = n * (N_DEV - 1)
    res = pl.pallas_call(
        body, name=name,
        out_shape=(pltpu.SemaphoreType.DMA((n_sem,)), pltpu.SemaphoreType.DMA((n_sem,)),
                   *[pltpu.HBM(b.shape, b.dtype) for b in bufs], jax.ShapeDtypeStruct((8, 128), f32)),
        in_specs=[_HBM] * n, out_specs=(_SEM, _SEM, *[_HBM] * n, pl.BlockSpec(memory_space=pltpu.VMEM)),
        input_output_aliases={k: 2 + k for k in range(n)}, compiler_params=_SPLIT_PARAMS,
    )(*[pltpu.with_memory_space_constraint(b, pltpu.HBM) for b in bufs])
    return res[0], res[1], list(res[2:2 + n]), res[2 + n]


def gather_wait(name, send_sems, recv_sems, bufs, after):
    n = len(bufs)

    def body(*refs):
        for cp in _split_gather_copies(refs[:n], refs[n], refs[n + 1]):
            cp.wait_send()
            cp.wait_recv()

    return list(pl.pallas_call(
        body, name=name, out_shape=tuple(pltpu.HBM(b.shape, b.dtype) for b in bufs),
        in_specs=[_HBM] * n + [_SEM, _SEM, _ANY], out_specs=tuple([_HBM] * n),
        input_output_aliases={k: k for k in range(n)}, compiler_params=_SPLIT_PARAMS,
    )(*bufs, send_sems, recv_sems, after))


def _split_chip_copies(srcs, bufs, modes, send_sems, recv_sems):
    x, y, c = lax.axis_index("x"), lax.axis_index("y"), lax.axis_index("c")
    mine = 2 * x + y
    copies = []
    for k in range(len(srcs)):
        for r in range(1, 4):
            px = 1 - x if r & 2 else x
            py = 1 - y if r & 1 else y
            src = srcs[k].at[2 * px + py] if modes[k] == "scatter" else srcs[k]
            copies.append(pltpu.make_async_remote_copy(
                src_ref=src, dst_ref=bufs[k].at[mine], send_sem=send_sems.at[3 * k + r - 1],
                recv_sem=recv_sems.at[3 * k + r - 1], device_id=(px, py, c), device_id_type=MESH))
    return copies


def chip_start(name, arrays, modes):
    n = len(arrays)
    bufs = chip_buffers(arrays, modes)

    def body(*refs):
        for cp in _split_chip_copies(refs[2 * n + 2:3 * n + 2], refs[3 * n + 2:4 * n + 2], modes, refs[2 * n], refs[2 * n + 1]):
            cp.start()
        refs[4 * n + 2][...] = jnp.zeros_like(refs[4 * n + 2])

    both = list(arrays) + list(bufs)
    res = pl.pallas_call(
        body, name=name,
        out_shape=(pltpu.SemaphoreType.DMA((3 * n,)), pltpu.SemaphoreType.DMA((3 * n,)),
                   *[pltpu.HBM(b.shape, b.dtype) for b in both], jax.ShapeDtypeStruct((8, 128), f32)),
        in_specs=[_HBM] * (2 * n), out_specs=(_SEM, _SEM, *[_HBM] * (2 * n), pl.BlockSpec(memory_space=pltpu.VMEM)),
        input_output_aliases={k: 2 + k for k in range(2 * n)}, compiler_params=_SPLIT_PARAMS,
    )(*[pltpu.with_memory_space_constraint(b, pltpu.HBM) for b in both])
    return res[0], res[1], list(res[2:2 + n]), list(res[2 + n:2 + 2 * n]), res[2 + 2 * n]


def chip_wait(name, send_sems, recv_sems, srcs, bufs, modes, after):
    n = len(bufs)

    def body(*refs):
        for cp in _split_chip_copies(refs[:n], refs[n:2 * n], modes, refs[2 * n], refs[2 * n + 1]):
            cp.wait_send()
            cp.wait_recv()

    both = list(srcs) + list(bufs)
    res = pl.pallas_call(
        body, name=name, out_shape=tuple(pltpu.HBM(b.shape, b.dtype) for b in both),
        in_specs=[_HBM] * (2 * n) + [_SEM, _SEM, _ANY], out_specs=tuple([_HBM] * (2 * n)),
        input_output_aliases={k: k for k in range(2 * n)}, compiler_params=_SPLIT_PARAMS,
    )(*both, send_sems, recv_sems, after)
    return list(res[n:])


def gather2(name, arrays):
    n = len(arrays)

    def body(*refs):
        start, forward, finish = _two_level_gather(refs[n:2 * n], *refs[2 * n + 1:])
        start()
        forward()
        finish()
        refs[2 * n][...] = jnp.zeros_like(refs[2 * n])

    bufs = gather_buffers(arrays)
    res = pl.pallas_call(
        body, name=name, out_shape=[jax.ShapeDtypeStruct(b.shape, b.dtype) for b in bufs] + [jax.ShapeDtypeStruct((8, 128), f32)],
        in_specs=[_ANY] * n, out_specs=[_ANY] * n + [pl.BlockSpec(memory_space=pltpu.VMEM)],
        input_output_aliases={k: k for k in range(n)}, scratch_shapes=_sems(n, N_DEV - 1),
    )(*bufs)
    return list(res[:n]), res[n]


def assemble(name, slabs):
    _, K, Ns = slabs.shape
    g = N_DEV if Ns % 128 == 0 else 2

    def body(w_ref, o_ref):
        o_ref[...] = jnp.concatenate([w_ref[s] for s in range(g)], axis=1)

    return pl.pallas_call(
        body, name=name, grid=(N_DEV // g,),
        in_specs=[pl.BlockSpec((g, K, Ns), lambda m: (m, 0, 0))],
        out_specs=pl.BlockSpec((K, g * Ns), lambda m: (0, m)),
        out_shape=jax.ShapeDtypeStruct((K, N_DEV * Ns), slabs.dtype),
        compiler_params=_cparams(1),
    )(slabs)


def mm_nn(name, a, b, out_dtype, add=None, tm=1024, tn=512, tk=None):
    M, K = a.shape
    N = b.shape[1]
    tm, tn = min(tm, M), min(tn, N)
    tk = K if tk is None else tk
    nk = K // tk
    has_add = add is not None

    def body(a_ref, b_ref, *rest):
        o_ref = rest[1] if has_add else rest[0]
        part = jnp.dot(a_ref[...].astype(bf16), b_ref[...].astype(bf16), preferred_element_type=f32)
        if nk == 1:
            o_ref[...] = (part + rest[0][...] if has_add else part).astype(o_ref.dtype)
            return
        acc_ref = rest[-1]
        k = pl.program_id(2)

        @pl.when(k == 0)
        def _():
            acc_ref[...] = part + rest[0][...] if has_add else part

        @pl.when(k > 0)
        def _():
            acc_ref[...] += part

        @pl.when(k == nk - 1)
        def _():
            o_ref[...] = acc_ref[...].astype(o_ref.dtype)

    in_specs = [pl.BlockSpec((tm, tk), lambda i, j, k: (i, k)), pl.BlockSpec((tk, tn), lambda i, j, k: (k, j))]
    ops = [a, b]
    if has_add:
        in_specs.append(pl.BlockSpec((tm, tn), lambda i, j, k: (i, j)))
        ops.append(add)
    return pl.pallas_call(
        body, name=name, grid=(M // tm, N // tn, nk), in_specs=in_specs,
        out_specs=pl.BlockSpec((tm, tn), lambda i, j, k: (i, j)),
        out_shape=jax.ShapeDtypeStruct((M, N), out_dtype),
        scratch_shapes=[pltpu.VMEM((tm, tn), f32)] if nk > 1 else [], compiler_params=_cparams(3),
    )(*ops)


def mm_many(name, xs, ws, out_dtype, transpose_w=False, tm=512):
    n = len(xs)
    M = xs[0].shape[0]
    tm = min(tm, M)
    dims = NT_DIMS if transpose_w else (((1,), (0,)), ((), ()))

    def body(*refs):
        for k in range(n):
            refs[2 * n + k][...] = lax.dot_general(refs[k][...].astype(bf16), refs[n + k][...].astype(bf16), dims,
                                                   preferred_element_type=f32).astype(out_dtype)

    n_out = [w.shape[0] if transpose_w else w.shape[1] for w in ws]
    return pl.pallas_call(
        body, name=name, grid=(M // tm,),
        in_specs=[pl.BlockSpec((tm, x.shape[1]), lambda i: (i, 0)) for x in xs]
        + [pl.BlockSpec(w.shape, lambda i: (0, 0)) for w in ws],
        out_specs=[pl.BlockSpec((tm, c), lambda i: (i, 0)) for c in n_out],
        out_shape=[jax.ShapeDtypeStruct((M, c), out_dtype) for c in n_out], compiler_params=_cparams(1),
    )(*xs, *ws)


def concat_cols(name, pieces, T, tm=512):
    widths = [p.shape[1] for p in pieces]
    total = sum(widths)
    tm = min(tm, T)

    def body(*refs):
        o_ref = refs[-1]
        off = 0
        for r, w in zip(refs[:-1], widths):
            o_ref[:, off:off + w] = r[...]
            off += w

    return pl.pallas_call(
        body, name=name, grid=(T // tm,),
        in_specs=[pl.BlockSpec((tm, w), lambda i: (i, 0)) for w in widths],
        out_specs=pl.BlockSpec((tm, total), lambda i: (i, 0)),
        out_shape=jax.ShapeDtypeStruct((T, total), pieces[0].dtype), compiler_params=_cparams(1),
    )(*pieces)


def mm_nt(name, a, b, out_dtype, tm=1024, tko=None, tc=None):
    M, C = a.shape
    Ko = b.shape[0]
    tm = min(tm, M)
    tc = C if tc is None else min(tc, C)
    tko = Ko if tko is None else tko
    nc = C // tc

    def body(a_ref, b_ref, o_ref, *scratch):
        part = lax.dot_general(a_ref[...].astype(bf16), b_ref[...].astype(bf16), (((1,), (1,)), ((), ())),
                               preferred_element_type=f32)
        if nc == 1:
            o_ref[...] = part.astype(o_ref.dtype)
            return
        acc_ref = scratch[0]
        c = pl.program_id(2)

        @pl.when(c == 0)
        def _():
            acc_ref[...] = part

        @pl.when(c > 0)
        def _():
            acc_ref[...] += part

        @pl.when(c == nc - 1)
        def _():
            o_ref[...] = acc_ref[...].astype(o_ref.dtype)

    return pl.pallas_call(
        body, name=name, grid=(M // tm, Ko // tko, nc),
        in_specs=[pl.BlockSpec((tm, tc), lambda i, j, c: (i, c)), pl.BlockSpec((tko, tc), lambda i, j, c: (j, c))],
        out_specs=pl.BlockSpec((tm, tko), lambda i, j, c: (i, j)),
        out_shape=jax.ShapeDtypeStruct((M, Ko), out_dtype),
        scratch_shapes=[pltpu.VMEM((tm, tko), f32)] if nc > 1 else [], compiler_params=_cparams(3),
    )(a, b)


def mm_tn(name, a, b, n_slab=None, tk=512, tn=512):
    T, K = a.shape
    N = b.shape[1]
    tk = min(tk, K)
    if n_slab is None:
        tn = min(tn, N)

        def body(a_ref, b_ref, o_ref):
            o_ref[...] = lax.dot_general(a_ref[...].astype(bf16), b_ref[...].astype(bf16), (((0,), (0,)), ((), ())),
                                         preferred_element_type=f32).astype(o_ref.dtype)

        return pl.pallas_call(
            body, name=name, grid=(N // tn, K // tk),
            in_specs=[pl.BlockSpec((T, tk), lambda j, i: (0, i)), pl.BlockSpec((T, tn), lambda j, i: (0, j))],
            out_specs=pl.BlockSpec((tk, tn), lambda j, i: (i, j)),
            out_shape=jax.ShapeDtypeStruct((K, N), bf16), compiler_params=_cparams(2),
        )(a, b)

    Ns = n_slab
    g = N_DEV if Ns % 128 == 0 else 2
    tn = g * Ns

    def body(a_ref, b_ref, o_ref):
        val = lax.dot_general(a_ref[...].astype(bf16), b_ref[...].astype(bf16), (((0,), (0,)), ((), ())),
                              preferred_element_type=f32)
        for s in range(g):
            o_ref[s] = val[:, s * Ns:(s + 1) * Ns].astype(o_ref.dtype)

    return pl.pallas_call(
        body, name=name, grid=(K // tk, N_DEV // g),
        in_specs=[pl.BlockSpec((T, tk), lambda i, j: (0, i)), pl.BlockSpec((T, tn), lambda i, j: (0, j))],
        out_specs=pl.BlockSpec((g, tk, Ns), lambda i, j: (j, i, 0)),
        out_shape=jax.ShapeDtypeStruct((N_DEV, K, Ns), bf16), compiler_params=_cparams(2),
    )(a, b)


def _tile_spec(w, tm, cb):
    if callable(cb):
        return pl.BlockSpec((tm, w), lambda c, i: (i, cb(c)))
    return pl.BlockSpec((tm, w), lambda c, i: (i, cb))


def _param_spec(block, cb):
    nd = len(block)
    if cb is None:
        return pl.BlockSpec(block, lambda c, i: (0,) * nd)
    return pl.BlockSpec(block, lambda c, i: (0,) * (nd - 1) + (cb(c),))


ROW_TILE = 512


def rowwise(name, fn, tiled, params, outs, T, tm, ncol=1):
    tm = min(tm, T)
    n_in = len(tiled) + len(params)
    n_t = len(tiled)

    def body(*refs):
        ins = [r[...].astype(f32) for r in refs[:n_t]] + [r[...] for r in refs[n_t:n_in]]
        res = fn(*ins)
        for r, o in zip(refs[n_in:], res):
            r[...] = o.astype(r.dtype)

    return pl.pallas_call(
        body, name=name, grid=(ncol, T // tm),
        in_specs=[_tile_spec(w, tm, cb) for _, w, cb in tiled] + [_param_spec(blk, cb) for _, blk, cb in params],
        out_specs=[_tile_spec(w, tm, cb) for _, w, cb, _ in outs],
        out_shape=[jax.ShapeDtypeStruct((T, cols), dt) for cols, _, _, dt in outs],
        compiler_params=_cparams(2),
    )(*[a for a, _, _ in tiled], *[a for a, _, _ in params])


def rowwise_bwd(name, fn, tiled, params, cts, grads, T, tm, ncol=1, adds=None):
    tm = min(tm, T)
    n_t, n_p, n_c = len(tiled), len(params), len(cts)
    adds = adds or [None] * n_t
    add_list = [(k, a) for k, a in enumerate(adds) if a is not None]
    want = [k for k, g in enumerate(grads) if g is not None]
    n_a = len(add_list)

    def body(*refs):
        pos = 0
        t_refs = refs[pos:pos + n_t]; pos += n_t
        p_refs = refs[pos:pos + n_p]; pos += n_p
        c_refs = refs[pos:pos + n_c]; pos += n_c
        a_refs = refs[pos:pos + n_a]; pos += n_a
        g_refs = refs[pos:pos + len(want)]; pos += len(want)
        pg_refs = refs[pos:pos + n_p]
        primals = [r[...].astype(f32) for r in t_refs] + [r[...] for r in p_refs]
        _, vjp = jax.vjp(fn, *primals)
        g = vjp(tuple(r[...].astype(f32) for r in c_refs))
        add_of = {k: a_refs[n][...] for n, (k, _) in enumerate(add_list)}
        for n, k in enumerate(want):
            val = g[k]
            if k in add_of:
                val = val + add_of[k]
            g_refs[n][...] = val.astype(g_refs[n].dtype)
        i = pl.program_id(1)
        for k in range(n_p):
            @pl.when(i == 0)
            def _(k=k):
                pg_refs[k][...] = g[n_t + k]

            @pl.when(i > 0)
            def _(k=k):
                pg_refs[k][...] += g[n_t + k]

    in_specs = ([_tile_spec(w, tm, cb) for _, w, cb in tiled] + [_param_spec(blk, cb) for _, blk, cb in params]
                + [_tile_spec(w, tm, cb) for _, w, cb in cts] + [_tile_spec(w, tm, cb) for _, (_, w, cb) in add_list])
    ops = ([a for a, _, _ in tiled] + [a for a, _, _ in params] + [a for a, _, _ in cts]
           + [a for _, (a, _, _) in add_list])
    out_specs, out_shape = [], []
    for k in want:
        w = tiled[k][1]
        out_specs.append(_tile_spec(w, tm, lambda c: c))
        out_shape.append(jax.ShapeDtypeStruct((T, ncol * w), grads[k]))
    for a, blk, cb in params:
        out_specs.append(_param_spec(blk, cb))
        out_shape.append(jax.ShapeDtypeStruct(a.shape, f32))
    return pl.pallas_call(
        body, name=name, grid=(ncol, T // tm), in_specs=in_specs, out_specs=out_specs, out_shape=out_shape,
        compiler_params=_cparams(2),
    )(*ops)


@jax.custom_vjp
def _bdot(a, b):
    return jnp.dot(a.astype(bf16), b.astype(bf16), preferred_element_type=f32)


def _bdot_fwd(a, b):
    return _bdot(a, b), (a, b)


def _bdot_bwd(res, ct):
    a, b = res
    ctb = ct.astype(bf16)
    da = lax.dot_general(ctb, b.astype(bf16), (((1,), (1,)), ((), ())), preferred_element_type=f32)
    db = lax.dot_general(a.astype(bf16), ctb, (((0,), (0,)), ((), ())), preferred_element_type=f32)
    return da, db


_bdot.defvjp(_bdot_fwd, _bdot_bwd)


def _layer_norm(x, g, b):
    mu = jnp.mean(x, axis=-1, keepdims=True)
    xc = x - mu
    y = xc * lax.rsqrt(jnp.mean(xc * xc, axis=-1, keepdims=True) + EPS)
    return y * g + b


def f_rms(x, g):
    y = x * lax.rsqrt(jnp.mean(x * x, axis=-1, keepdims=True) + EPS)
    return (y * g,)


def f_sgu(zu, zv, ln_g, ln_b, wm, sgb_t):
    u = jax.nn.gelu(zu)
    vn = _layer_norm(jax.nn.gelu(zv), ln_g, ln_b)
    row = lax.broadcasted_iota(jnp.int32, (CHUNK, CHUNK), 0)
    col = lax.broadcasted_iota(jnp.int32, (CHUNK, CHUNK), 1)
    tril = col <= row
    low = col < HEAD_DIM
    parts = []
    for p in range(4):
        vp = vn[:, CHUNK * p:CHUNK * (p + 1)]
        w0 = jnp.where(tril, wm[2 * p], 0.0)
        w1 = jnp.where(tril, wm[2 * p + 1], 0.0)
        parts.append(_bdot(w0, jnp.where(low, vp, 0.0)) + _bdot(w1, jnp.where(low, 0.0, vp)))
    s = jnp.concatenate(parts, axis=1)
    lane_g = lax.shift_right_logical(lax.broadcasted_iota(jnp.int32, s.shape, 1), 6)
    bias = jnp.zeros_like(s)
    for g in range(8):
        bias = jnp.where(lane_g == g, sgb_t[:, g:g + 1], bias)
    return (u * (s + bias),)


def f_glu(p, gl):
    return (p * jax.nn.sigmoid(gl),)


def f_lnsilu(c1, g, b):
    return (jax.nn.silu(_layer_norm(c1, g, b)),)


def _head_norm(x, g64):
    g = jnp.concatenate([g64] * N_HEADS, axis=1)
    lane_h = lax.shift_right_logical(lax.broadcasted_iota(jnp.int32, x.shape, 1), 6)
    x2 = x * x
    r = jnp.zeros_like(x)
    for h in range(N_HEADS):
        mh = lane_h == h
        ms = jnp.sum(jnp.where(mh, x2, 0.0), axis=-1, keepdims=True) * (1.0 / HEAD_DIM)
        r = jnp.where(mh, lax.rsqrt(ms + EPS), r)
    return (x * r) * g


def f_qkv(zq, zk, zv, qg, kg):
    return (_head_norm(zq, qg) * 0.125, _head_norm(zk, kg), zv)


def f_merge(g0, g1, g2, ya, yb, yc, b0, b1, b2):
    return (jax.nn.sigmoid(g0 + b0) * ya + jax.nn.sigmoid(g1 + b1) * yb + jax.nn.sigmoid(g2 + b2) * yc,)


HALO = 32


def conv31_fwd(x, w, b, T, tt=256):
    C = x.shape[1]
    r = tt // HALO

    def body(x_ref, h_ref, w_ref, b_ref, y_ref, buf):
        i = pl.program_id(0)
        halo = h_ref[...]
        buf[0:HALO, :] = jnp.where(i > 0, halo, jnp.zeros_like(halo))
        buf[HALO:HALO + tt, :] = x_ref[...]
        acc = jnp.zeros((tt, C), f32) + b_ref[...]
        for k in range(CV_KERNEL):
            acc = acc + w_ref[k:k + 1, :] * buf[pl.ds(HALO - (CV_KERNEL - 1) + k, tt), :]
        y_ref[...] = acc

    return pl.pallas_call(
        body, name="conv31_fwd", grid=(T // tt,),
        in_specs=[pl.BlockSpec((tt, C), lambda i: (i, 0)),
                  pl.BlockSpec((HALO, C), lambda i: (jnp.maximum(i * r - 1, 0), 0)),
                  pl.BlockSpec((CV_KERNEL, C), lambda i: (0, 0)), pl.BlockSpec((1, C), lambda i: (0, 0))],
        out_specs=pl.BlockSpec((tt, C), lambda i: (i, 0)),
        out_shape=jax.ShapeDtypeStruct((T, C), f32),
        scratch_shapes=[pltpu.VMEM((HALO + tt, C), f32)], compiler_params=_cparams(1),
    )(x, x, w, b)


def conv31_bwd(x, w, dy, T, tt=256):
    C = x.shape[1]
    r = tt // HALO
    n = T // tt

    def body(x_ref, h_ref, w_ref, dy_ref, dyn_ref, dx_ref, dw_ref, db_ref, xbuf, dbuf):
        i = pl.program_id(0)
        halo = h_ref[...]
        xbuf[0:HALO, :] = jnp.where(i > 0, halo, jnp.zeros_like(halo))
        xbuf[HALO:HALO + tt, :] = x_ref[...]
        nxt = dyn_ref[...]
        dy = dy_ref[...]
        dbuf[0:tt, :] = dy
        dbuf[tt:tt + HALO, :] = jnp.where(i < n - 1, nxt, jnp.zeros_like(nxt))

        @pl.when(i == 0)
        def _():
            dw_ref[...] = jnp.zeros_like(dw_ref)
            db_ref[...] = jnp.zeros_like(db_ref)

        acc = jnp.zeros((tt, C), f32)
        for k in range(CV_KERNEL):
            acc = acc + w_ref[k:k + 1, :] * dbuf[pl.ds(CV_KERNEL - 1 - k, tt), :]
            xs = xbuf[pl.ds(HALO - (CV_KERNEL - 1) + k, tt), :]
            dw_ref[k:k + 1, :] += jnp.sum(dy * xs, axis=0, keepdims=True)
        dx_ref[...] = acc
        db_ref[...] += jnp.sum(dy, axis=0, keepdims=True)

    return pl.pallas_call(
        body, name="conv31_bwd", grid=(n,),
        in_specs=[pl.BlockSpec((tt, C), lambda i: (i, 0)),
                  pl.BlockSpec((HALO, C), lambda i: (jnp.maximum(i * r - 1, 0), 0)),
                  pl.BlockSpec((CV_KERNEL, C), lambda i: (0, 0)),
                  pl.BlockSpec((tt, C), lambda i: (i, 0)),
                  pl.BlockSpec((HALO, C), lambda i: (jnp.minimum((i + 1) * r, n * r - 1), 0))],
        out_specs=[pl.BlockSpec((tt, C), lambda i: (i, 0)), pl.BlockSpec((CV_KERNEL, C), lambda i: (0, 0)),
                   pl.BlockSpec((1, C), lambda i: (0, 0))],
        out_shape=[jax.ShapeDtypeStruct((T, C), f32), jax.ShapeDtypeStruct((CV_KERNEL, C), f32),
                   jax.ShapeDtypeStruct((1, C), f32)],
        scratch_shapes=[pltpu.VMEM((HALO + tt, C), f32), pltpu.VMEM((HALO + tt, C), f32)], compiler_params=_cparams(1),
    )(x, x, w, dy, dy)


FFN_TC = 128
FFN_PAD = 8


def ffn_act_fwd(up, cw, cb, T):
    F = up.shape[1] // 2
    nj = F // FFN_TC
    rc = min(128, T)

    def body(g_ref, v_ref, g0, g1, g2, gb, v0, v1, v2, vb, o_ref, gp, vp):
        zeros = jnp.zeros((FFN_PAD, FFN_TC), f32)
        for p, x_ref in ((gp, g_ref), (vp, v_ref)):
            p[0:FFN_PAD, :] = zeros
            p[FFN_PAD:FFN_PAD + T, :] = x_ref[...].astype(f32)
        wg = (g0[...], g1[...], g2[...], gb[...])
        wv = (v0[...], v1[...], v2[...], vb[...])

        def conv(p, w, r):
            return (w[0] * p[pl.ds(FFN_PAD + r - 2, rc), :] + w[1] * p[pl.ds(FFN_PAD + r - 1, rc), :]
                    + w[2] * p[pl.ds(FFN_PAD + r, rc), :] + w[3])

        for r in range(0, T, rc):
            o_ref[pl.ds(r, rc), :] = (jax.nn.silu(conv(gp, wg, r)) * conv(vp, wv, r)).astype(o_ref.dtype)

    gspec = pl.BlockSpec((T, FFN_TC), lambda j: (0, j))
    vspec = pl.BlockSpec((T, FFN_TC), lambda j: (0, j + nj))
    pg = pl.BlockSpec((1, FFN_TC), lambda j: (0, j))
    pv = pl.BlockSpec((1, FFN_TC), lambda j: (0, j + nj))
    return pl.pallas_call(
        body, name="ffn_act_fwd", grid=(nj,),
        in_specs=[gspec, vspec, pg, pg, pg, pg, pv, pv, pv, pv], out_specs=gspec,
        out_shape=jax.ShapeDtypeStruct((T, F), bf16),
        scratch_shapes=[pltpu.VMEM((FFN_PAD + T, FFN_TC), f32)] * 2, compiler_params=_cparams(1),
    )(up, up, cw[0], cw[1], cw[2], cb, cw[0], cw[1], cw[2], cb)


def ffn_act_bwd(up, cw, cb, dact, T):
    F = up.shape[1] // 2
    nj = F // FFN_TC

    rc = min(128, T)
    ext = rc + FFN_PAD

    def body(g_ref, v_ref, g0, g1, g2, gb, v0, v1, v2, vb, d_ref, dupg_ref, dupv_ref, *rest):
        p_refs, (gp, vp, dp, dgs, dvs) = rest[:8], rest[8:]
        zeros = jnp.zeros((FFN_PAD, FFN_TC), f32)
        for p, x_ref in ((gp, g_ref), (vp, v_ref)):
            p[0:FFN_PAD, :] = zeros
            p[FFN_PAD:FFN_PAD + T, :] = x_ref[...].astype(f32)
            p[FFN_PAD + T:FFN_PAD + T + FFN_PAD, :] = zeros
        dp[0:T, :] = d_ref[...]
        dp[T:T + FFN_PAD, :] = zeros
        wg = (g0[...], g1[...], g2[...], gb[...])
        wv = (v0[...], v1[...], v2[...], vb[...])
        acc = [jnp.zeros((1, FFN_TC), f32) for _ in range(8)]

        def taps(p, r):
            return tuple(p[pl.ds(FFN_PAD + r - s, ext), :] for s in (2, 1, 0))

        for r in range(0, T, rc):
            xg, xv = taps(gp, r), taps(vp, r)
            gc = wg[0] * xg[0] + wg[1] * xg[1] + wg[2] * xg[2] + wg[3]
            vc = wv[0] * xv[0] + wv[1] * xv[1] + wv[2] * xv[2] + wv[3]
            d = dp[pl.ds(r, ext), :]
            sg = jax.nn.sigmoid(gc)
            sides = ((d * vc * (sg * (1.0 + gc * (1.0 - sg))), xg, wg, dgs, dupg_ref), (d * (gc * sg), xv, wv, dvs, dupv_ref))
            for side, (dc, x, w, buf, dup_ref) in enumerate(sides):
                buf[...] = dc
                dc0 = dc[0:rc]
                dup = w[2] * dc0 + w[1] * buf[pl.ds(1, rc), :] + w[0] * buf[pl.ds(2, rc), :]
                dup_ref[pl.ds(r, rc), :] = dup.astype(dup_ref.dtype)
                for k in range(3):
                    acc[4 * side + k] = acc[4 * side + k] + jnp.sum(dc0 * x[k][0:rc], axis=0, keepdims=True)
                acc[4 * side + 3] = acc[4 * side + 3] + jnp.sum(dc0, axis=0, keepdims=True)
        for k in range(8):
            p_refs[k][...] = acc[k]

    gspec = pl.BlockSpec((T, FFN_TC), lambda j: (0, j))
    vspec = pl.BlockSpec((T, FFN_TC), lambda j: (0, j + nj))
    pg = pl.BlockSpec((1, FFN_TC), lambda j: (0, j))
    pv = pl.BlockSpec((1, FFN_TC), lambda j: (0, j + nj))
    res = pl.pallas_call(
        body, name="ffn_act_bwd", grid=(nj,),
        in_specs=[gspec, vspec, pg, pg, pg, pg, pv, pv, pv, pv, gspec],
        out_specs=[gspec, gspec] + [pg] * 8,
        out_shape=[jax.ShapeDtypeStruct((T, F), bf16)] * 2 + [jax.ShapeDtypeStruct((1, F), f32)] * 8,
        scratch_shapes=[pltpu.VMEM((FFN_PAD + T + FFN_PAD, FFN_TC), f32)] * 2 + [pltpu.VMEM((T + FFN_PAD, FFN_TC), f32)]
        + [pltpu.VMEM((ext, FFN_TC), f32)] * 2,
        compiler_params=_cparams(1),
    )(up, up, cw[0], cw[1], cw[2], cb, cw[0], cw[1], cw[2], cb, dact)
    dup = concat_cols("dup_concat", [res[0], res[1]], T)
    return (dup,) + tuple(jnp.concatenate([res[2 + k], res[6 + k]], axis=1) for k in range(4))


BQ = 256
BK = 256
assert BQ == BK
NT_DIMS = (((1,), (1,)), ((), ()))
TN_DIMS = (((0,), (0,)), ((), ()))


def _split_dot(x, u):
    x1 = x.astype(bf16)
    x2 = (x - x1.astype(f32)).astype(bf16)
    n = x.shape[0]
    y = jnp.dot(jnp.concatenate([x1, x2], axis=0), u, preferred_element_type=f32)
    return y[0:n] + y[n:2 * n]


def _log_sigmoids(z):
    sp = jnp.log(1.0 + jnp.exp(-jnp.abs(z)))
    lsp = jnp.minimum(z, 0.0) - sp
    return lsp, lsp - z


def _stack_heads(x):
    head1 = lax.broadcasted_iota(jnp.int32, x.shape, 1) >= HEAD_DIM
    zero = jnp.zeros_like(x)
    return jnp.concatenate([jnp.where(head1, zero, x), jnp.where(head1, x, zero)], axis=0)


def _unstack_heads(y):
    head1 = lax.broadcasted_iota(jnp.int32, (BQ, y.shape[1]), 1) >= HEAD_DIM
    return jnp.where(head1, y[BQ:2 * BQ], y[0:BQ])


def _attn_masks():
    row = lax.broadcasted_iota(jnp.int32, (2 * BQ, BK), 0)
    col = lax.broadcasted_iota(jnp.int32, (2 * BQ, BK), 1)
    ur = lax.broadcasted_iota(jnp.int32, (BK, BK), 0)
    uc = lax.broadcasted_iota(jnp.int32, (BK, BK), 1)
    return (row & (BQ - 1)) - col, (ur > uc).astype(bf16), (ur < uc).astype(bf16)


def attn_fwd(q, k, v, T, gather=None):
    nq = T // BQ
    n_g = 0 if gather is None else len(gather)

    def body(*refs):
        q_ref, k_ref, v_ref = refs[:3]
        o_ref, tot_ref = refs[3 + n_g:5 + n_g]
        p, i = pl.program_id(0), pl.program_id(1)
        if n_g:
            start, forward, finish = _two_level_gather(refs[5 + n_g:5 + 2 * n_g], *refs[5 + 2 * n_g:])
            pl.when(jnp.logical_and(p == 0, i == 0))(start)
        qs = _stack_heads(q_ref[...])
        diff, u_after, _ = _attn_masks()

        def step(jb, carry, diagonal):
            acc, c = carry
            ks = pl.multiple_of(jb * BK, BK)
            kb = k_ref[pl.ds(ks, BK), :]
            vb = v_ref[pl.ds(ks, BK), :]
            z = lax.dot_general(qs, kb, NT_DIMS, preferred_element_type=f32)
            lsp, lm = _log_sigmoids(z)
            if diagonal:
                m = diff > 0
                lm = jnp.where(m, lm, 0.0)
            a = jnp.exp(lsp + _split_dot(lm, u_after))
            if diagonal:
                a = jnp.where(m, a, 0.0)
            acc = acc + jnp.exp(c) * jnp.dot(a.astype(bf16), vb, preferred_element_type=f32)
            return acc, c + jnp.sum(lm, axis=-1, keepdims=True)

        carry = step(i, (jnp.zeros((2 * BQ, 128), f32), jnp.zeros((2 * BQ, 1), f32)), True)
        acc, c = lax.fori_loop(0, i, lambda t, cr: step(i - 1 - t, cr, False), carry)
        o_ref[...] = _unstack_heads(acc).astype(o_ref.dtype)
        tot_ref[...] = _unstack_heads(jnp.broadcast_to(c, (2 * BQ, 128)))
        if n_g:
            @pl.when(jnp.logical_and(p == 3, i == nq - 1))
            def _():
                forward()
                finish()

    blk = pl.BlockSpec((BQ, 128), lambda p, i: (i, p))
    full = pl.BlockSpec((T, 128), lambda p, i: (0, p))
    bufs = [] if gather is None else gather_buffers(gather)
    res = pl.pallas_call(
        body, name="attn_fwd_gather" if n_g else "attn_fwd", grid=(4, nq),
        in_specs=[blk, full, full] + [_ANY] * n_g, out_specs=[blk, blk] + [_ANY] * n_g,
        out_shape=[jax.ShapeDtypeStruct((T, 512), bf16), jax.ShapeDtypeStruct((T, 512), f32)]
        + [jax.ShapeDtypeStruct(b.shape, b.dtype) for b in bufs],
        input_output_aliases={3 + k: 2 + k for k in range(n_g)},
        scratch_shapes=_sems(n_g, N_DEV - 1) if n_g else [], compiler_params=_cparams(2),
    )(q, k, v, *bufs)
    return res[0], res[1], list(res[2:])


def attn_bwd(q, k, v, do, tot, T, scatter=None):
    nq = T // BQ
    n_s = 0 if scatter is None else len(scatter[0])
    modes = [] if scatter is None else list(scatter[1])

    def body(*refs):
        q_ref, k_ref, v_ref, do_ref, tot_ref = refs[:5]
        dq_ref, dk_ref, dv_ref = refs[5 + 2 * n_s:8 + 2 * n_s]
        p, i = pl.program_id(0), pl.program_id(1)
        if n_s:
            def copies():
                return _chip_copies(refs[5:5 + n_s], refs[8 + 2 * n_s:8 + 3 * n_s], modes, *refs[8 + 3 * n_s:])

            @pl.when(jnp.logical_and(p == 0, i == 0))
            def _():
                for cp in copies():
                    cp.start()

        @pl.when(i == 0)
        def _():
            dk_ref[...] = jnp.zeros_like(dk_ref)
            dv_ref[...] = jnp.zeros_like(dv_ref)

        qs = _stack_heads(q_ref[...])
        dos = _stack_heads(do_ref[...].astype(bf16))
        totv = tot_ref[...]
        tots = jnp.concatenate([totv[:, 0:1], totv[:, HEAD_DIM:HEAD_DIM + 1]], axis=0)
        diff, u_after, u_before = _attn_masks()

        def step(jb, carry, diagonal):
            dq, cl, cg = carry
            ks = pl.multiple_of(jb * BK, BK)
            kb = k_ref[pl.ds(ks, BK), :]
            vb = v_ref[pl.ds(ks, BK), :]
            z = lax.dot_general(qs, kb, NT_DIMS, preferred_element_type=f32)
            lsp, lsn = _log_sigmoids(z)
            lm = lsn
            if diagonal:
                m = diff > 0
                lm = jnp.where(m, lsn, 0.0)
            a = jnp.exp(lsp + _split_dot(lm, u_after))
            if diagonal:
                a = jnp.where(m, a, 0.0)
            g = a * lax.dot_general(dos, vb, NT_DIMS, preferred_element_type=f32)
            bb = jnp.exp(lsp)
            aa = g * jnp.exp(lsn) - _split_dot(g, u_before) * bb
            if diagonal:
                aa = jnp.where(m, aa, 0.0)
                bb = jnp.where(m, bb, 0.0)
            cl = cl + jnp.sum(lm, axis=-1, keepdims=True)
            f = jnp.exp(tots - cl)
            dz = (f * aa - cg * bb).astype(bf16)
            cg = cg + f * jnp.sum(g, axis=-1, keepdims=True)
            dq = dq + jnp.dot(dz, kb, preferred_element_type=f32)
            dk_ref[pl.ds(ks, BK), :] += lax.dot_general(dz, qs, TN_DIMS, preferred_element_type=f32)
            dv_ref[pl.ds(ks, BK), :] += lax.dot_general((f * a).astype(bf16), dos, TN_DIMS, preferred_element_type=f32)
            return dq, cl, cg

        zc = jnp.zeros((2 * BQ, 1), f32)
        carry = lax.fori_loop(0, i, lambda jb, cr: step(jb, cr, False), (jnp.zeros((2 * BQ, 128), f32), zc, zc))
        dq, _, _ = step(i, carry, True)
        dq_ref[...] = _unstack_heads(dq)
        if n_s:
            @pl.when(jnp.logical_and(p == 3, i == nq - 1))
            def _():
                for cp in copies():
                    cp.wait()

    blk = pl.BlockSpec((BQ, 128), lambda p, i: (i, p))
    full = pl.BlockSpec((T, 128), lambda p, i: (0, p))
    shp = jax.ShapeDtypeStruct((T, 512), f32)
    s_list = [] if scatter is None else list(scatter[0])
    bufs = chip_buffers(s_list, modes) if n_s else []
    res = pl.pallas_call(
        body, name="attn_bwd_scatter" if n_s else "attn_bwd", grid=(4, nq),
        in_specs=[blk, full, full, blk, blk] + [_ANY] * (2 * n_s), out_specs=[blk, full, full] + [_ANY] * n_s,
        out_shape=[shp, shp, shp] + [jax.ShapeDtypeStruct(b.shape, b.dtype) for b in bufs],
        input_output_aliases={5 + n_s + k: 3 + k for k in range(n_s)},
        scratch_shapes=_sems(n_s, 3) if n_s else [], compiler_params=_cparams(2),
    )(q, k, v, do, tot, *s_list, *bufs)
    return res[0], res[1], res[2], list(res[3:])


def loss_head(y, target, T, tm=256):
    D = y.shape[1]

    def body(y_ref, t_ref, dy_ref, l_ref):
        i = pl.program_id(0)
        err = y_ref[...] - t_ref[...]
        dy_ref[...] = err * (1.0 / D)
        part = 0.5 * jnp.sum(jnp.sum(err * err, axis=-1, keepdims=True) * (1.0 / D), axis=0, keepdims=True)

        @pl.when(i == 0)
        def _():
            l_ref[...] = jnp.zeros_like(l_ref)

        l_ref[...] += jnp.broadcast_to(part, l_ref.shape)

    spec = pl.BlockSpec((tm, D), lambda i: (i, 0))
    return pl.pallas_call(
        body, name="loss_head", grid=(T // tm,), in_specs=[spec, spec],
        out_specs=[spec, pl.BlockSpec((1, 128), lambda i: (0, 0))],
        out_shape=[jax.ShapeDtypeStruct((T, D), f32), jax.ShapeDtypeStruct((1, 128), f32)],
        compiler_params=_cparams(1),
    )(y, target)


def _row_tile(rows, cols, offset=0, max_elems=128 * 1024):
    best = None
    for t in range(16, rows + 1, 16):
        if rows % t == 0 and offset % t == 0 and t * cols <= max_elems:
            best = t
    return best if best is not None else rows


def adamw(name, w, m, v, parts, layer, prev=None, row_off=0):
    L, R, C = w.shape
    n_parts = parts.shape[0]
    tr = _row_tile(R, C, row_off, max_elems=256 * 1024)
    assert R % tr == 0 and row_off % tr == 0, (name, R, row_off, tr)

    def body(w_ref, m_ref, v_ref, p_ref, *rest):
        g_ref, d_ref, nm_ref, nv_ref = rest[-4:]
        g = p_ref[0].astype(f32)
        for s in range(1, n_parts):
            g = g + p_ref[s].astype(f32)
        wv = w_ref[...]
        mn = ADAM_B1 * m_ref[...] + (1.0 - ADAM_B1) * g
        vn = ADAM_B2 * v_ref[...] + (1.0 - ADAM_B2) * jnp.square(g)
        m_hat = mn / (1.0 - ADAM_B1 ** ADAM_STEP)
        v_hat = vn / (1.0 - ADAM_B2 ** ADAM_STEP)
        g_ref[...] = g
        d_ref[...] = -ADAM_LR * (m_hat / (jnp.sqrt(v_hat) + ADAM_EPS) + ADAM_WD * wv)
        nm_ref[...] = mn
        nv_ref[...] = vn

    spec = pl.BlockSpec((None, tr, C), lambda i: (layer, i, 0))
    shp = jax.ShapeDtypeStruct((L, R, C), f32)
    n_prev = 0 if prev is None else 4
    return pl.pallas_call(
        body, name=name, grid=(R // tr,),
        in_specs=[spec, spec, spec, pl.BlockSpec((n_parts, tr, C), lambda i: (0, row_off // tr + i, 0))] + [_ANY] * n_prev,
        out_specs=[spec] * 4, out_shape=[shp] * 4, input_output_aliases={4 + k: k for k in range(n_prev)},
        compiler_params=_cparams(1),
    )(w, m, v, parts, *(prev or ()))


REPL = ["ln1_g", "sg_ln_g", "sg_ln_b", "sg_w", "sg_b", "cv_b", "cv_ln_g", "cv_ln_b", "q_norm_g", "k_norm_g", "ln2_g",
        "ffn_conv_b"]
SMALL_SHARDED = ["b_gate", "cv_w", "ffn_conv_w"]
BIG = ["w_in", "w_a_out", "w_b_out", "w_c_out", "w_up", "w_out", "w_down"]
COL_SHARDED = ["w_a_out", "w_b_out", "w_c_out"]
TRANSPOSED = ["w_in", "w_up"]


def _pack(arrs, rows):
    flat = jnp.concatenate([a.reshape(-1) for a in arrs])
    return jnp.pad(flat, (0, rows * 128 - flat.shape[0])).reshape(rows, 128)


def _pack_layers(arrs, rows):
    flat = jnp.concatenate([a.reshape(a.shape[0], -1) for a in arrs], axis=1)
    return jnp.pad(flat, ((0, 0), (0, rows * 128 - flat.shape[1]))).reshape(flat.shape[0], rows, 128)


def _unpack_layers(packed, shapes):
    flat = packed.reshape(packed.shape[0], -1)
    out, pos = [], 0
    for s in shapes:
        n = 1
        for d in s[1:]:
            n *= d
        out.append(flat[:, pos:pos + n].reshape(s))
        pos += n
    return out


def _rows_for(shapes, mult):
    n = 0
    for s in shapes:
        k = 1
        for d in s:
            k *= d
        n += k
    rows = -(-n // 128)
    return -(-rows // mult) * mult


W_NAMES = ['ln1_g', 'w_in', 'b_gate', 'sg_ln_g', 'sg_ln_b', 'sg_w', 'sg_b', 'w_a_out', 'cv_w', 'cv_b', 'cv_ln_g', 'cv_ln_b',
           'w_b_out', 'q_norm_g', 'k_norm_g', 'w_c_out', 'w_out', 'ln2_g', 'w_up', 'ffn_conv_w', 'ffn_conv_b', 'w_down']


def _forward_layer(x, P, T, late=None):
    D = x.shape[1]
    sv = {"x0": x}
    (h1,) = rowwise("rms_fwd", f_rms, [(x, D, 0)], [(P["ln1_g"], (1, D), None)], [(D, D, 0, bf16)], T, ROW_TILE)
    z = mm_nt("in_proj", h1, P["w_in_t"], bf16, tko=1664, tc=D)
    sv["h1"], sv["z"] = h1, z
    (ya_in,) = rowwise("sgu_fwd", f_sgu, [(z, 512, 0), (z, 512, 1)],
                       [(P["sg_ln_g"], (1, 512), None), (P["sg_ln_b"], (1, 512), None),
                        (P["sg_w"], (8, CHUNK, CHUNK), None), (P["sg_b_t"], (CHUNK, 8), None)],
                       [(512, 512, 0, bf16)], T, CHUNK)
    (c0,) = rowwise("glu_fwd", f_glu, [(z, 512, 2), (z, 512, 3)], [], [(512, 512, 0, f32)], T, ROW_TILE)
    c1 = conv31_fwd(c0, P["cv_w"], P["cv_b"], T)
    (c3,) = rowwise("lnsilu_fwd", f_lnsilu, [(c1, 512, 0)], [(P["cv_ln_g"], (1, 512), None), (P["cv_ln_b"], (1, 512), None)],
                    [(512, 512, 0, bf16)], T, ROW_TILE)
    q8, kn, vb = rowwise("qkv_fwd", f_qkv, [(z, 512, 4), (z, 512, 5), (z, 512, 6)],
                         [(P["q_norm_g"], (1, HEAD_DIM), None), (P["k_norm_g"], (1, HEAD_DIM), None)],
                         [(512, 512, 0, bf16)] * 3, T, ROW_TILE)
    o, tot, _ = attn_fwd(q8, kn, vb, T)
    if late is not None:
        P.update(late(o))
    ya, yb, yc = mm_many("branch_out", [ya_in, c3, o], [P["w_a_out"], P["w_b_out"], P["w_c_out"]], bf16)
    (merged,) = rowwise("merge_fwd", f_merge,
                        [(z, 512, lambda c: 7 + c), (z, 512, lambda c: 9 + c), (z, 512, lambda c: 11 + c),
                         (ya, 512, lambda c: c), (yb, 512, lambda c: c), (yc, 512, lambda c: c)],
                        [(P["b_gate"][k], (1, 512), lambda c: c) for k in range(3)],
                        [(D, 512, lambda c: c, bf16)], T, 256, ncol=2)
    x1 = mm_nn("out_proj", merged, P["w_out"], f32, add=x, tm=256, tn=D)
    sv.update(ya_in=ya_in, ya=ya, c0=c0, c1=c1, c3=c3, yb=yb, q8=q8, kn=kn, vb=vb, o=o, tot=tot, yc=yc, merged=merged, x1=x1)
    (h2,) = rowwise("rms_fwd", f_rms, [(x1, D, 0)], [(P["ln2_g"], (1, D), None)], [(D, D, 0, bf16)], T, ROW_TILE)
    up = mm_nt("up_proj", h2, P["w_up_t"], bf16, tm=256, tc=D)
    act = ffn_act_fwd(up, P["ffn_conv_w"], P["ffn_conv_b"], T)
    x2 = mm_nn("down_proj", act, P["w_down"], f32, add=x1, tm=256, tn=D)
    sv.update(h2=h2, up=up, act=act)
    return x2, sv


def _backward_layer(dx2, P, sv, T, scatter=None):
    D = dx2.shape[1]
    G = {}
    G["w_down"] = mm_tn("dw_down", sv["act"], dx2, tk=1408)
    dact = mm_nt("d_act", dx2, P["w_down"], f32, tm=256, tc=D)
    dup, dcw0, dcw1, dcw2, G["ffn_conv_b"] = ffn_act_bwd(sv["up"], P["ffn_conv_w"], P["ffn_conv_b"], dact, T)
    G["ffn_conv_w"] = jnp.concatenate([dcw0, dcw1, dcw2], axis=0)
    G["w_up"] = mm_tn("dw_up", dup, sv["h2"], tk=1408)
    dh2 = mm_nn("d_h2", dup, P["w_up_t"], f32, tm=256, tn=D)
    dx1, G["ln2_g"] = rowwise_bwd("rms_bwd", f_rms, [(sv["x1"], D, 0)], [(P["ln2_g"], (1, D), None)], [(dh2, D, 0)],
                                  [f32], T, ROW_TILE, adds=[(dx2, D, 0)])
    G["w_out"] = mm_tn("dw_out", sv["merged"], dx1, tk=1024)
    dmerged = mm_nt("d_merged", dx1, P["w_out"], f32)
    z = sv["z"]
    dg0, dg1, dg2, dya, dyb, dyc, db0, db1, db2 = rowwise_bwd(
        "merge_bwd", f_merge,
        [(z, 512, lambda c: 7 + c), (z, 512, lambda c: 9 + c), (z, 512, lambda c: 11 + c),
         (sv["ya"], 512, lambda c: c), (sv["yb"], 512, lambda c: c), (sv["yc"], 512, lambda c: c)],
        [(P["b_gate"][k], (1, 512), lambda c: c) for k in range(3)],
        [(dmerged, 512, lambda c: c)], [bf16] * 6, T, 256, ncol=2)
    G["b_gate"] = jnp.concatenate([db0, db1, db2], axis=0)
    dya_in, dc3, do = mm_many("d_branch_in", [dya, dyb, dyc], [P["w_a_out"], P["w_b_out"], P["w_c_out"]], f32,
                              transpose_w=True)
    G["w_c_out"] = mm_tn("dw_c_out", sv["o"], dyc, n_slab=CHUNK)
    dq8, dkn, dvb, received = attn_bwd(sv["q8"], sv["kn"], sv["vb"], do, sv["tot"], T, scatter=scatter)
    dzq, dzk, dzv, G["q_norm_g"], G["k_norm_g"] = rowwise_bwd(
        "qkv_bwd", f_qkv, [(z, 512, 4), (z, 512, 5), (z, 512, 6)],
        [(P["q_norm_g"], (1, HEAD_DIM), None), (P["k_norm_g"], (1, HEAD_DIM), None)],
        [(dq8, 512, 0), (dkn, 512, 0), (dvb, 512, 0)], [bf16] * 3, T, ROW_TILE)
    G["w_b_out"] = mm_tn("dw_b_out", sv["c3"], dyb, n_slab=CHUNK)
    dc1, G["cv_ln_g"], G["cv_ln_b"] = rowwise_bwd(
        "lnsilu_bwd", f_lnsilu, [(sv["c1"], 512, 0)], [(P["cv_ln_g"], (1, 512), None), (P["cv_ln_b"], (1, 512), None)],
        [(dc3, 512, 0)], [f32], T, ROW_TILE)
    dc0, G["cv_w"], G["cv_b"] = conv31_bwd(sv["c0"], P["cv_w"], dc1, T)
    dzp, dzgl = rowwise_bwd("glu_bwd", f_glu, [(z, 512, 2), (z, 512, 3)], [], [(dc0, 512, 0)], [bf16] * 2, T, ROW_TILE)
    G["w_a_out"] = mm_tn("dw_a_out", sv["ya_in"], dya, n_slab=CHUNK)
    dzu, dzv_a, G["sg_ln_g"], G["sg_ln_b"], G["sg_w"], dsbt = rowwise_bwd(
        "sgu_bwd", f_sgu, [(z, 512, 0), (z, 512, 1)],
        [(P["sg_ln_g"], (1, 512), None), (P["sg_ln_b"], (1, 512), None), (P["sg_w"], (8, CHUNK, CHUNK), None),
         (P["sg_b_t"], (CHUNK, 8), None)],
        [(dya_in, 512, 0)], [bf16] * 2, T, CHUNK)
    G["sg_b"] = dsbt.T
    dz = concat_cols("dz_concat", [dzu, dzv_a, dzp, dzgl, dzq, dzk, dzv, dg0, dg1, dg2], T)
    G["w_in"] = mm_tn("dw_in", dz, sv["h1"], tk=1664)
    dh1 = mm_nn("d_h1", dz, P["w_in_t"], f32, tm=256, tn=D)
    dx0, G["ln1_g"] = rowwise_bwd("rms_bwd", f_rms, [(sv["x0"], D, 0)], [(P["ln1_g"], (1, D), None)], [(dh1, D, 0)],
                                  [f32], T, ROW_TILE, adds=[(dx1, D, 0)])
    return dx0, G, received


def kernel(x, ln1_g, w_in, b_gate, sg_ln_g, sg_ln_b, sg_w, sg_b, w_a_out, cv_w, cv_b, cv_ln_g, cv_ln_b, w_b_out, q_norm_g, k_norm_g, w_c_out, w_out, ln2_g, w_up, ffn_conv_w, ffn_conv_b, w_down, loss_target, m_ln1_g, m_w_in, m_b_gate, m_sg_ln_g, m_sg_ln_b, m_sg_w, m_sg_b, m_w_a_out, m_cv_w, m_cv_b, m_cv_ln_g, m_cv_ln_b, m_w_b_out, m_q_norm_g, m_k_norm_g, m_w_c_out, m_w_out, m_ln2_g, m_w_up, m_ffn_conv_w, m_ffn_conv_b, m_w_down, v_ln1_g, v_w_in, v_b_gate, v_sg_ln_g, v_sg_ln_b, v_sg_w, v_sg_b, v_w_a_out, v_cv_w, v_cv_b, v_cv_ln_g, v_cv_ln_b, v_w_b_out, v_q_norm_g, v_k_norm_g, v_w_c_out, v_w_out, v_ln2_g, v_w_up, v_ffn_conv_w, v_ffn_conv_b, v_w_down):
    W = dict(ln1_g=ln1_g, w_in=w_in, b_gate=b_gate, sg_ln_g=sg_ln_g, sg_ln_b=sg_ln_b, sg_w=sg_w, sg_b=sg_b, w_a_out=w_a_out,
             cv_w=cv_w, cv_b=cv_b, cv_ln_g=cv_ln_g, cv_ln_b=cv_ln_b, w_b_out=w_b_out, q_norm_g=q_norm_g, k_norm_g=k_norm_g,
             w_c_out=w_c_out, w_out=w_out, ln2_g=ln2_g, w_up=w_up, ffn_conv_w=ffn_conv_w, ffn_conv_b=ffn_conv_b, w_down=w_down)
    M = dict(ln1_g=m_ln1_g, w_in=m_w_in, b_gate=m_b_gate, sg_ln_g=m_sg_ln_g, sg_ln_b=m_sg_ln_b, sg_w=m_sg_w, sg_b=m_sg_b,
             w_a_out=m_w_a_out, cv_w=m_cv_w, cv_b=m_cv_b, cv_ln_g=m_cv_ln_g, cv_ln_b=m_cv_ln_b, w_b_out=m_w_b_out,
             q_norm_g=m_q_norm_g, k_norm_g=m_k_norm_g, w_c_out=m_w_c_out, w_out=m_w_out, ln2_g=m_ln2_g, w_up=m_w_up,
             ffn_conv_w=m_ffn_conv_w, ffn_conv_b=m_ffn_conv_b, w_down=m_w_down)
    V = dict(ln1_g=v_ln1_g, w_in=v_w_in, b_gate=v_b_gate, sg_ln_g=v_sg_ln_g, sg_ln_b=v_sg_ln_b, sg_w=v_sg_w, sg_b=v_sg_b,
             w_a_out=v_w_a_out, cv_w=v_cv_w, cv_b=v_cv_b, cv_ln_g=v_cv_ln_g, cv_ln_b=v_cv_ln_b, w_b_out=v_w_b_out,
             q_norm_g=v_q_norm_g, k_norm_g=v_k_norm_g, w_c_out=v_w_c_out, w_out=v_w_out, ln2_g=v_ln2_g, w_up=v_w_up,
             ffn_conv_w=v_ffn_conv_w, ffn_conv_b=v_ffn_conv_b, w_down=v_w_down)
    T, D = x.shape[1], x.shape[2]
    L = DEPTH
    xs = x.reshape(T, D)
    target = loss_target.reshape(T, D)

    ss_shapes = [W[n].shape for n in SMALL_SHARDED]
    ss_rows = _rows_for(ss_shapes, 8)
    (ss_all,), small_token = exchange("gather_small", [_pack([W[n] for n in SMALL_SHARDED], ss_rows)], ["gather"])
    full_small = {}
    pos = 0
    for n in SMALL_SHARDED:
        s = W[n].shape
        cnt = s[0] * s[1] * s[2]
        part = ss_all.reshape(N_DEV, -1)[:, pos:pos + cnt].reshape((N_DEV,) + s)
        full_small[n] = jnp.transpose(part, (1, 2, 0, 3)).reshape(s[0], s[1], N_DEV * s[2])
        pos += cnt

    params, saved = [], []
    cur = xs
    Wt = {n: (jnp.transpose(W[n], (0, 2, 1)) if n in TRANSPOSED else W[n]) for n in BIG}

    def local_slabs(l):
        return [Wt[n][l].astype(bf16) for n in BIG]

    def weights_of(names, slabs):
        out = {}
        for n, s in zip(names, slabs):
            if n in TRANSPOSED:
                out[n + "_t"] = s.reshape(-1, s.shape[-1])
            else:
                out[n] = assemble("assemble_" + n, s) if n in COL_SHARDED else s.reshape(-1, s.shape[-1])
        return out

    LATE = [n for n in BIG if n != "w_in"]
    def behind(token, blocks):
        return [blocks[0] + token[0, 0].astype(blocks[0].dtype)] + list(blocks[1:])

    first, token = gather2("gather_w_in", behind(small_token, [local_slabs(0)[0]]))
    early_part, late_part = {}, {}
    for l in range(L):
        if l > 0:
            early_part[l] = gather_start("gather_start_in_%d" % l, behind(token, local_slabs(l)[:1]))
            token = early_part[l][3]
        late_part[l] = gather_start("gather_start_rest_%d" % l, behind(token, local_slabs(l)[1:]))
        token = late_part[l][3]
    start_token = token[0, 0]

    def late_weights(l):
        send_sems, recv_sems, bufs, _ = late_part[l]
        return lambda o: weights_of(LATE, gather_wait("gather_wait_rest_%d" % l, send_sems, recv_sems, bufs, o))

    for l in range(L):
        if l == 0:
            P = weights_of(["w_in"], first)
        else:
            send_sems, recv_sems, bufs, _ = early_part[l]
            P = weights_of(["w_in"], gather_wait("gather_wait_in_%d" % l, send_sems, recv_sems, bufs, cur))
        late = late_weights(l)
        for n in REPL:
            P[n] = W[n][l]
        for n in ("ln1_g", "sg_ln_g", "sg_ln_b", "cv_b", "cv_ln_g", "cv_ln_b", "q_norm_g", "k_norm_g", "ln2_g", "ffn_conv_b"):
            P[n] = P[n].reshape(1, -1)
        P["sg_b_t"] = P["sg_b"].T
        P["cv_w"] = full_small["cv_w"][l]
        P["b_gate"] = [full_small["b_gate"][l][k:k + 1] for k in range(3)]
        P["ffn_conv_w"] = [full_small["ffn_conv_w"][l][k:k + 1] for k in range(3)]
        if l == 0:
            P["ln1_g"] = P["ln1_g"] + start_token
        cur, sv = _forward_layer(cur, P, T, late=late)
        params.append(P)
        saved.append(sv)

    dy, loss_row = loss_head(cur, target, T)
    loss = lax.psum(loss_row[0, 0], ("x", "y", "c"))

    repl_shapes = [W[n].shape for n in REPL]
    repl_rows = _rows_for([s[1:] for s in repl_shapes], 16)
    ssl_rows = _rows_for([s[1:] for s in ss_shapes], 16)
    state = {"repl": [_pack_layers([X[n] for n in REPL], repl_rows) for X in (W, M, V)],
             "ss": [_pack_layers([X[n] for n in SMALL_SHARDED], ssl_rows) for X in (W, M, V)]}
    WIDE = [n for n in BIG if n not in COL_SHARDED]
    big_rows, big_off = {}, {}
    for n in WIDE:
        big_off[n] = sum(big_rows.values())
        big_rows[n] = W[n].shape[1] * W[n].shape[2] // D
        state[n] = [(jnp.transpose(X[n], (0, 2, 1)) if n in TRANSPOSED else X[n]).reshape(L, big_rows[n], D)
                    for X in (W, M, V)]
    for n in COL_SHARDED:
        state[n] = [W[n], M[n], V[n]]
    done = {n: None for n in state}

    def update(layer, received):
        for n in WIDE:
            done[n] = adamw("adamw_" + n, *state[n], received[0], layer, done[n], row_off=big_off[n])
        done["repl"] = adamw("adamw_repl", *state["repl"], received[1], layer, done["repl"])
        done["ss"] = adamw("adamw_ss", *state["ss"], received[2], layer, done["ss"])
        for k, n in enumerate(COL_SHARDED):
            done[n] = adamw("adamw_" + n, *state[n], received[3], layer, done[n], row_off=k * W[n].shape[1])

    def reduce_in_chip(G):
        big = [G[n].reshape(4, 2, big_rows[n], D) for n in WIDE]
        narrow = jnp.concatenate([G[n].reshape((4, 2) + G[n].shape[1:]) for n in COL_SHARDED], axis=2)
        ss_parts = []
        for n in SMALL_SHARDED:
            k, c = W[n].shape[1:]
            ss_parts.append(jnp.transpose(G[n].reshape(k, N_DEV, c), (1, 0, 2)).reshape(N_DEV, k * c))
        ss_send = jnp.concatenate(ss_parts, axis=1)
        ss_send = jnp.pad(ss_send, ((0, 0), (0, ssl_rows * 128 - ss_send.shape[1]))).reshape(4, 2, ssl_rows, 128)
        repl = _pack([G[n] for n in REPL], repl_rows)
        core = lax.axis_index("c")
        ss_kept = lax.dynamic_index_in_dim(ss_send, core, 1, keepdims=False)
        narrow_kept = lax.dynamic_index_in_dim(narrow, core, 1, keepdims=False)
        got, repl_got, ss_got, narrow_got = pair_exchange("pair_exchange", big, [repl, ss_send, narrow],
                                                          ["gather", "scatter", "scatter"])
        return [pair_add_wide("pair_add_big", big, got), add2("pair_add_repl", repl, repl_got), add2("pair_add_ss", ss_kept, ss_got),
                add2("pair_add_narrow", narrow_kept, narrow_got)]

    chip_modes = ["scatter", "gather", "scatter", "scatter"]
    dcur = dy
    pending = None
    for l in reversed(range(L)):
        P = params[l]
        if pending is not None:
            P = dict(P, ffn_conv_b=P["ffn_conv_b"] + pending[4][0, 0])
        dcur, G, _ = _backward_layer(dcur, P, saved[l], T)
        started = chip_start("chip_start_%d" % l, reduce_in_chip(G), chip_modes)
        if pending is not None:
            after = started[4] + dcur[0:8, 0:128]
            update(l + 1, chip_wait("chip_wait_%d" % (l + 1), *pending[:4], chip_modes, after))
        pending = started
    after = sum(done[n][0][1, 0:1, 0:1] for n in state)
    update(0, chip_wait("chip_wait_0", *pending[:4], chip_modes, after))

    results = [{}, {}, {}, {}]
    for k in range(4):
        for n, a in zip(REPL, _unpack_layers(done["repl"][k], repl_shapes)):
            results[k][n] = a
        for n, a in zip(SMALL_SHARDED, _unpack_layers(done["ss"][k], ss_shapes)):
            results[k][n] = a
        for n in BIG:
            results[k][n] = jnp.transpose(done[n][k], (0, 2, 1)) if n in TRANSPOSED else done[n][k].reshape(W[n].shape)
    out = [loss, dcur.reshape(1, T, D)]
    for k in range(4):
        out += [results[k][n] for n in W_NAMES]
    return tuple(out)
```

```python
import functools

import jax
import jax.numpy as jnp
from jax import lax
from jax.experimental import pallas as pl
from jax.experimental.pallas import tpu as pltpu

f32 = jnp.float32
bf16 = jnp.bfloat16

EPS = 1e-6
N_DEV = 8
DEPTH = 4
CHUNK = 128
HEAD_DIM = 64
N_HEADS = 8
CV_KERNEL = 31
VMEM_LIMIT_BYTES = 56 * 2 ** 20

ADAM_LR = 0.001
ADAM_B1 = 0.9
ADAM_B2 = 0.999
ADAM_EPS = 1e-08
ADAM_WD = 0.01
ADAM_STEP = 10

MESH = pl.DeviceIdType.MESH


def _cparams(n_grid):
    return pltpu.CompilerParams(dimension_semantics=("arbitrary",) * n_grid, vmem_limit_bytes=VMEM_LIMIT_BYTES)


def exchange(name, arrays, modes):
    n = len(arrays)

    def body(*refs):
        copies = _direct_copies(refs[:n], refs[n:2 * n], modes, *refs[2 * n + 1:])
        for cp in copies:
            cp.start()
        for cp in copies:
            cp.wait()
        refs[2 * n][...] = jnp.zeros_like(refs[2 * n])

    res = pl.pallas_call(
        body, name=name, out_shape=_exchange_out_shapes(arrays, modes) + [jax.ShapeDtypeStruct((8, 128), f32)],
        in_specs=[_ANY] * n, out_specs=[_ANY] * n + [pl.BlockSpec(memory_space=pltpu.VMEM)],
        scratch_shapes=_exchange_sems(n),
    )(*arrays)
    return list(res[:n]), res[n]


_ANY = pl.BlockSpec(memory_space=pl.ANY)


def _exchange_out_shapes(arrays, modes):
    return [jax.ShapeDtypeStruct((N_DEV,) + tuple(a.shape) if m == "gather" else tuple(a.shape), a.dtype)
            for a, m in zip(arrays, modes)]


def _exchange_sems(n):
    return [pltpu.SemaphoreType.DMA((n, N_DEV - 1)), pltpu.SemaphoreType.DMA((n, N_DEV - 1)), pltpu.SemaphoreType.DMA((n,))]


def _direct_copies(ins, outs, modes, send_sems, recv_sems, local_sems):
    x, y, c = lax.axis_index("x"), lax.axis_index("y"), lax.axis_index("c")
    me = 4 * x + 2 * y + c
    copies = []
    for k in range(len(ins)):
        src_mine = ins[k] if modes[k] == "gather" else ins[k].at[me]
        copies.append(pltpu.make_async_copy(src_mine, outs[k].at[me], local_sems.at[k]))
        for r in range(1, N_DEV):
            px = 1 - x if r & 4 else x
            py = 1 - y if r & 2 else y
            pc = 1 - c if r & 1 else c
            src = ins[k] if modes[k] == "gather" else ins[k].at[4 * px + 2 * py + pc]
            copies.append(pltpu.make_async_remote_copy(
                src_ref=src, dst_ref=outs[k].at[me], send_sem=send_sems.at[k, r - 1], recv_sem=recv_sems.at[k, r - 1],
                device_id=(px, py, pc), device_id_type=MESH))
    return copies


def own_slot_buffer(block, n_slots, index):
    return lax.dynamic_update_slice(lax.empty((n_slots,) + tuple(block.shape), block.dtype), block[None],
                                    (index,) + (0,) * block.ndim)


def _sems(n, m):
    return [pltpu.SemaphoreType.DMA((n, m)), pltpu.SemaphoreType.DMA((n, m))]


def _two_level_gather(bufs, send_sems, recv_sems):
    x, y, c = lax.axis_index("x"), lax.axis_index("y"), lax.axis_index("c")
    me = 4 * x + 2 * y + c
    sibling = (x, y, 1 - c)
    chips = [(1 - x, y), (x, 1 - y), (1 - x, 1 - y)]

    def slot(px, py, pc):
        return 4 * px + 2 * py + pc

    def copy(k, sem, block, to):
        rows = bufs[k].at[block]
        return pltpu.make_async_remote_copy(src_ref=rows, dst_ref=rows, send_sem=send_sems.at[k, sem],
                                            recv_sem=recv_sems.at[k, sem], device_id=to, device_id_type=MESH)

    def first(k):
        return [copy(k, 0, me, sibling)] + [copy(k, 1 + j, me, (*chip, c)) for j, chip in enumerate(chips)]

    def passed_on(k, j):
        return copy(k, 4 + j, slot(*chips[j], c), sibling)

    def start():
        for k in range(len(bufs)):
            for cp in first(k):
                cp.start()

    def forward():
        for k in range(len(bufs)):
            for j in range(3):
                copy(k, 1 + j, slot(*chips[j], c), sibling).wait_recv()
                passed_on(k, j).start()

    def finish():
        for k in range(len(bufs)):
            copy(k, 0, slot(x, y, 1 - c), sibling).wait_recv()
            for j, chip in enumerate(chips):
                copy(k, 4 + j, slot(*chip, 1 - c), sibling).wait_recv()
            for cp in first(k) + [passed_on(k, j) for j in range(3)]:
                cp.wait_send()

    return start, forward, finish


def pair_exchange(name, big, smalls, modes):
    nb, ns = len(big), len(smalls)
    rows = [a.shape[2] for a in big]
    total, width = sum(rows), big[0].shape[3]
    n = nb + ns

    def body(*refs):
        ins, outs = refs[:n], refs[n:n + 1 + ns]
        send_sems, recv_sems = refs[-2:]
        x, y, c = lax.axis_index("x"), lax.axis_index("y"), lax.axis_index("c")
        copies = []

        def remote(k, src, dst):
            copies.append(pltpu.make_async_remote_copy(src_ref=src, dst_ref=dst, send_sem=send_sems.at[k, 0],
                                                       recv_sem=recv_sems.at[k, 0], device_id=(x, y, 1 - c),
                                                       device_id_type=MESH))

        off = 0
        for k in range(nb):
            remote(k, ins[k].at[:, 1 - c], outs[0].at[:, pl.ds(off, rows[k]), :])
            off += rows[k]
        for j in range(ns):
            remote(nb + j, ins[nb + j].at[:, 1 - c] if modes[j] == "scatter" else ins[nb + j], outs[1 + j])
        for cp in copies:
            cp.start()
        for cp in copies:
            cp.wait()

    out_shape = [jax.ShapeDtypeStruct((4, total, width), big[0].dtype)]
    for a, m in zip(smalls, modes):
        out_shape.append(jax.ShapeDtypeStruct((4,) + tuple(a.shape[2:]) if m == "scatter" else tuple(a.shape), a.dtype))
    return pl.pallas_call(
        body, name=name, out_shape=out_shape, in_specs=[_ANY] * n, out_specs=[_ANY] * len(out_shape),
        scratch_shapes=_sems(n, 1),
    )(*big, *smalls)


def add2(name, a, b):
    shape = a.shape
    a2, b2 = a.reshape(-1, shape[-1]), b.reshape(-1, shape[-1])
    R, C = a2.shape
    tr = _row_tile(R, C, max_elems=1200 * 1024)

    def body(a_ref, b_ref, o_ref):
        o_ref[...] = (a_ref[...].astype(f32) + b_ref[...].astype(f32)).astype(o_ref.dtype)

    spec = pl.BlockSpec((tr, C), lambda i: (i, 0))
    return pl.pallas_call(
        body, name=name, grid=(R // tr,), in_specs=[spec, spec], out_specs=spec,
        out_shape=jax.ShapeDtypeStruct((R, C), a.dtype), compiler_params=_cparams(1),
    )(a2, b2).reshape(shape)


def pair_add_wide(name, big, got):
    nb = len(big)
    rows = [a.shape[2] for a in big]
    total, width = sum(rows), big[0].shape[3]
    core = lax.axis_index("c").astype(jnp.int32).reshape(1)

    def body(core_ref, *refs):
        got_ref, o_ref = refs[nb], refs[nb + 1]
        off = 0
        for k in range(nb):
            part = slice(off, off + rows[k])
            o_ref[part, :] = (refs[k][...].astype(f32) + got_ref[part, :].astype(f32)).astype(o_ref.dtype)
            off += rows[k]

    whole = pl.BlockSpec((None, total, width), lambda i, core_ref: (i, 0, 0))
    grid_spec = pltpu.PrefetchScalarGridSpec(
        num_scalar_prefetch=1, grid=(4,),
        in_specs=[pl.BlockSpec((None, None, r, width), lambda i, core_ref: (i, core_ref[0], 0, 0)) for r in rows] + [whole],
        out_specs=whole)
    return pl.pallas_call(
        body, name=name, grid_spec=grid_spec, out_shape=jax.ShapeDtypeStruct((4, total, width), got.dtype),
        compiler_params=_cparams(1),
    )(core, *big, got)


def chip_buffers(arrays, modes):
    mine = 2 * lax.axis_index("x") + lax.axis_index("y")
    return [own_slot_buffer(lax.dynamic_index_in_dim(a, mine, 0, keepdims=False) if m == "scatter" else a, 4, mine)
            for a, m in zip(arrays, modes)]


def _chip_copies(ins, bufs, modes, send_sems, recv_sems):
    x, y, c = lax.axis_index("x"), lax.axis_index("y"), lax.axis_index("c")
    mine = 2 * x + y
    copies = []
    for k in range(len(ins)):
        for r in range(1, 4):
            px = 1 - x if r & 2 else x
            py = 1 - y if r & 1 else y
            src = ins[k].at[2 * px + py] if modes[k] == "scatter" else ins[k]
            copies.append(pltpu.make_async_remote_copy(
                src_ref=src, dst_ref=bufs[k].at[mine], send_sem=send_sems.at[k, r - 1], recv_sem=recv_sems.at[k, r - 1],
                device_id=(px, py, c), device_id_type=MESH))
    return copies


def chip_exchange(name, arrays, modes):
    n = len(arrays)

    def body(*refs):
        copies = _chip_copies(refs[:n], refs[2 * n:3 * n], modes, *refs[3 * n:])
        for cp in copies:
            cp.start()
        for cp in copies:
            cp.wait()

    bufs = chip_buffers(arrays, modes)
    return pl.pallas_call(
        body, name=name, out_shape=[jax.ShapeDtypeStruct(b.shape, b.dtype) for b in bufs],
        in_specs=[_ANY] * (2 * n), out_specs=[_ANY] * n, input_output_aliases={n + k: k for k in range(n)},
        scratch_shapes=_sems(n, 3),
    )(*arrays, *bufs)


def gather_buffers(blocks):
    me = 4 * lax.axis_index("x") + 2 * lax.axis_index("y") + lax.axis_index("c")
    return [own_slot_buffer(b, N_DEV, me) for b in blocks]


_HBM = pl.BlockSpec(memory_space=pltpu.HBM)
_SEM = pl.BlockSpec(memory_space=pltpu.SEMAPHORE)
_SPLIT_PARAMS = pltpu.CompilerParams(has_side_effects=pltpu.SideEffectType.DATAFLOW_SIDE_EFFECTING)


def _split_gather_copies(bufs, send_sems, recv_sems):
    x, y, c = lax.axis_index("x"), lax.axis_index("y"), lax.axis_index("c")
    me = 4 * x + 2 * y + c
    copies = []
    for k in range(len(bufs)):
        rows = bufs[k].at[me]
        for r in range(1, N_DEV):
            peer = (1 - x if r & 4 else x, 1 - y if r & 2 else y, 1 - c if r & 1 else c)
            copies.append(pltpu.make_async_remote_copy(
                src_ref=rows, dst_ref=rows, send_sem=send_sems.at[(N_DEV - 1) * k + r - 1],
                recv_sem=recv_sems.at[(N_DEV - 1) * k + r - 1], device_id=peer, device_id_type=MESH))
    return copies


def gather_start(name, blocks):
    n = len(blocks)
    bufs = gather_buffers(blocks)

    def body(*refs):
        for cp in _split_gather_copies(refs[n + 2:2 * n + 2], refs[n], refs[n + 1]):
            cp.start()
        refs[2 * n + 2][...] = jnp.zeros_like(refs[2 * n + 2])

    n_sem = n * (N_DEV - 1)
    res = pl.pallas_call(
        body, name=name,
        out_shape=(pltpu.SemaphoreType.DMA((n_sem,)), pltpu.SemaphoreType.DMA((n_sem,)),
                   *[pltpu.HBM(b.shape, b.dtype) for b in bufs], jax.ShapeDtypeStruct((8, 128), f32)),
        in_specs=[_HBM] * n, out_specs=(_SEM, _SEM, *[_HBM] * n, pl.BlockSpec(memory_space=pltpu.VMEM)),
        input_output_aliases={k: 2 + k for k in range(n)}, compiler_params=_SPLIT_PARAMS,
    )(*[pltpu.with_memory_space_constraint(b, pltpu.HBM) for b in bufs])
    return res[0], res[1], list(res[2:2 + n]), res[2 + n]


def gather_wait(name, send_sems, recv_sems, bufs, after):
    n = len(bufs)

    def body(*refs):
        for cp in _split_gather_copies(refs[:n], refs[n], refs[n + 1]):
            cp.wait_send()
            cp.wait_recv()

    return list(pl.pallas_call(
        body, name=name, out_shape=tuple(pltpu.HBM(b.shape, b.dtype) for b in bufs),
        in_specs=[_HBM] * n + [_SEM, _SEM, _ANY], out_specs=tuple([_HBM] * n),
        input_output_aliases={k: k for k in range(n)}, compiler_params=_SPLIT_PARAMS,
    )(*bufs, send_sems, recv_sems, after))


def _split_chip_copies(srcs, bufs, modes, send_sems, recv_sems):
    x, y, c = lax.axis_index("x"), lax.axis_index("y"), lax.axis_index("c")
    mine = 2 * x + y
    copies = []
    for k in range(len(srcs)):
        for r in range(1, 4):
            px = 1 - x if r & 2 else x
            py = 1 - y if r & 1 else y
            src = srcs[k].at[2 * px + py] if modes[k] == "scatter" else srcs[k]
            copies.append(pltpu.make_async_remote_copy(
                src_ref=src, dst_ref=bufs[k].at[mine], send_sem=send_sems.at[3 * k + r - 1],
                recv_sem=recv_sems.at[3 * k + r - 1], device_id=(px, py, c), device_id_type=MESH))
    return copies


def chip_start(name, arrays, modes):
    n = len(arrays)
    bufs = chip_buffers(arrays, modes)

    def body(*refs):
        for cp in _split_chip_copies(refs[2 * n + 2:3 * n + 2], refs[3 * n + 2:4 * n + 2], modes, refs[2 * n], refs[2 * n + 1]):
            cp.start()
        refs[4 * n + 2][...] = jnp.zeros_like(refs[4 * n + 2])

    both = list(arrays) + list(bufs)
    res = pl.pallas_call(
        body, name=name,
        out_shape=(pltpu.SemaphoreType.DMA((3 * n,)), pltpu.SemaphoreType.DMA((3 * n,)),
                   *[pltpu.HBM(b.shape, b.dtype) for b in both], jax.ShapeDtypeStruct((8, 128), f32)),
        in_specs=[_HBM] * (2 * n), out_specs=(_SEM, _SEM, *[_HBM] * (2 * n), pl.BlockSpec(memory_space=pltpu.VMEM)),
        input_output_aliases={k: 2 + k for k in range(2 * n)}, compiler_params=_SPLIT_PARAMS,
    )(*[pltpu.with_memory_space_constraint(b, pltpu.HBM) for b in both])
    return res[0], res[1], list(res[2:2 + n]), list(res[2 + n:2 + 2 * n]), res[2 + 2 * n]


def chip_wait(name, send_sems, recv_sems, srcs, bufs, modes, after):
    n = len(bufs)

    def body(*refs):
        for cp in _split_chip_copies(refs[:n], refs[n:2 * n], modes, refs[2 * n], refs[2 * n + 1]):
            cp.wait_send()
            cp.wait_recv()

    both = list(srcs) + list(bufs)
    res = pl.pallas_call(
        body, name=name, out_shape=tuple(pltpu.HBM(b.shape, b.dtype) for b in both),
        in_specs=[_HBM] * (2 * n) + [_SEM, _SEM, _ANY], out_specs=tuple([_HBM] * (2 * n)),
        input_output_aliases={k: k for k in range(2 * n)}, compiler_params=_SPLIT_PARAMS,
    )(*both, send_sems, recv_sems, after)
    return list(res[n:])


def gather2(name, arrays):
    n = len(arrays)

    def body(*refs):
        start, forward, finish = _two_level_gather(refs[n:2 * n], *refs[2 * n + 1:])
        start()
        forward()
        finish()
        refs[2 * n][...] = jnp.zeros_like(refs[2 * n])

    bufs = gather_buffers(arrays)
    res = pl.pallas_call(
        body, name=name, out_shape=[jax.ShapeDtypeStruct(b.shape, b.dtype) for b in bufs] + [jax.ShapeDtypeStruct((8, 128), f32)],
        in_specs=[_ANY] * n, out_specs=[_ANY] * n + [pl.BlockSpec(memory_space=pltpu.VMEM)],
        input_output_aliases={k: k for k in range(n)}, scratch_shapes=_sems(n, N_DEV - 1),
    )(*bufs)
    return list(res[:n]), res[n]


def assemble(name, slabs):
    _, K, Ns = slabs.shape
    g = N_DEV if Ns % 128 == 0 else 2

    def body(w_ref, o_ref):
        o_ref[...] = jnp.concatenate([w_ref[s] for s in range(g)], axis=1)

    return pl.pallas_call(
        body, name=name, grid=(N_DEV // g,),
        in_specs=[pl.BlockSpec((g, K, Ns), lambda m: (m, 0, 0))],
        out_specs=pl.BlockSpec((K, g * Ns), lambda m: (0, m)),
        out_shape=jax.ShapeDtypeStruct((K, N_DEV * Ns), slabs.dtype),
        compiler_params=_cparams(1),
    )(slabs)


def mm_nn(name, a, b, out_dtype, add=None, tm=1024, tn=512, tk=None):
    M, K = a.shape
    N = b.shape[1]
    tm, tn = min(tm, M), min(tn, N)
    tk = K if tk is None else tk
    nk = K // tk
    has_add = add is not None

    def body(a_ref, b_ref, *rest):
        o_ref = rest[1] if has_add else rest[0]
        part = jnp.dot(a_ref[...].astype(bf16), b_ref[...].astype(bf16), preferred_element_type=f32)
        if nk == 1:
            o_ref[...] = (part + rest[0][...] if has_add else part).astype(o_ref.dtype)
            return
        acc_ref = rest[-1]
        k = pl.program_id(2)

        @pl.when(k == 0)
        def _():
            acc_ref[...] = part + rest[0][...] if has_add else part

        @pl.when(k > 0)
        def _():
            acc_ref[...] += part

        @pl.when(k == nk - 1)
        def _():
            o_ref[...] = acc_ref[...].astype(o_ref.dtype)

    in_specs = [pl.BlockSpec((tm, tk), lambda i, j, k: (i, k)), pl.BlockSpec((tk, tn), lambda i, j, k: (k, j))]
    ops = [a, b]
    if has_add:
        in_specs.append(pl.BlockSpec((tm, tn), lambda i, j, k: (i, j)))
        ops.append(add)
    return pl.pallas_call(
        body, name=name, grid=(M // tm, N // tn, nk), in_specs=in_specs,
        out_specs=pl.BlockSpec((tm, tn), lambda i, j, k: (i, j)),
        out_shape=jax.ShapeDtypeStruct((M, N), out_dtype),
        scratch_shapes=[pltpu.VMEM((tm, tn), f32)] if nk > 1 else [], compiler_params=_cparams(3),
    )(*ops)


def mm_many(name, xs, ws, out_dtype, transpose_w=False, tm=512):
    n = len(xs)
    M = xs[0].shape[0]
    tm = min(tm, M)
    dims = NT_DIMS if transpose_w else (((1,), (0,)), ((), ()))

    def body(*refs):
        for k in range(n):
            refs[2 * n + k][...] = lax.dot_general(refs[k][...].astype(bf16), refs[n + k][...].astype(bf16), dims,
                                                   preferred_element_type=f32).astype(out_dtype)

    n_out = [w.shape[0] if transpose_w else w.shape[1] for w in ws]
    return pl.pallas_call(
        body, name=name, grid=(M // tm,),
        in_specs=[pl.BlockSpec((tm, x.shape[1]), lambda i: (i, 0)) for x in xs]
        + [pl.BlockSpec(w.shape, lambda i: (0, 0)) for w in ws],
        out_specs=[pl.BlockSpec((tm, c), lambda i: (i, 0)) for c in n_out],
        out_shape=[jax.ShapeDtypeStruct((M, c), out_dtype) for c in n_out], compiler_params=_cparams(1),
    )(*xs, *ws)


def concat_cols(name, pieces, T, tm=512):
    widths = [p.shape[1] for p in pieces]
    total = sum(widths)
    tm = min(tm, T)

    def body(*refs):
        o_ref = refs[-1]
        off = 0
        for r, w in zip(refs[:-1], widths):
            o_ref[:, off:off + w] = r[...]
            off += w

    return pl.pallas_call(
        body, name=name, grid=(T // tm,),
        in_specs=[pl.BlockSpec((tm, w), lambda i: (i, 0)) for w in widths],
        out_specs=pl.BlockSpec((tm, total), lambda i: (i, 0)),
        out_shape=jax.ShapeDtypeStruct((T, total), pieces[0].dtype), compiler_params=_cparams(1),
    )(*pieces)


def mm_nt(name, a, b, out_dtype, tm=1024, tko=None, tc=None):
    M, C = a.shape
    Ko = b.shape[0]
    tm = min(tm, M)
    tc = C if tc is None else min(tc, C)
    tko = Ko if tko is None else tko
    nc = C // tc

    def body(a_ref, b_ref, o_ref, *scratch):
        part = lax.dot_general(a_ref[...].astype(bf16), b_ref[...].astype(bf16), (((1,), (1,)), ((), ())),
                               preferred_element_type=f32)
        if nc == 1:
            o_ref[...] = part.astype(o_ref.dtype)
            return
        acc_ref = scratch[0]
        c = pl.program_id(2)

        @pl.when(c == 0)
        def _():
            acc_ref[...] = part

        @pl.when(c > 0)
        def _():
            acc_ref[...] += part

        @pl.when(c == nc - 1)
        def _():
            o_ref[...] = acc_ref[...].astype(o_ref.dtype)

    return pl.pallas_call(
        body, name=name, grid=(M // tm, Ko // tko, nc),
        in_specs=[pl.BlockSpec((tm, tc), lambda i, j, c: (i, c)), pl.BlockSpec((tko, tc), lambda i, j, c: (j, c))],
        out_specs=pl.BlockSpec((tm, tko), lambda i, j, c: (i, j)),
        out_shape=jax.ShapeDtypeStruct((M, Ko), out_dtype),
        scratch_shapes=[pltpu.VMEM((tm, tko), f32)] if nc > 1 else [], compiler_params=_cparams(3),
    )(a, b)


def mm_tn(name, a, b, n_slab=None, tk=512, tn=512):
    T, K = a.shape
    N = b.shape[1]
    tk = min(tk, K)
    if n_slab is None:
        tn = min(tn, N)

        def body(a_ref, b_ref, o_ref):
            o_ref[...] = lax.dot_general(a_ref[...].astype(bf16), b_ref[...].astype(bf16), (((0,), (0,)), ((), ())),
                                         preferred_element_type=f32).astype(o_ref.dtype)

        return pl.pallas_call(
            body, name=name, grid=(N // tn, K // tk),
            in_specs=[pl.BlockSpec((T, tk), lambda j, i: (0, i)), pl.BlockSpec((T, tn), lambda j, i: (0, j))],
            out_specs=pl.BlockSpec((tk, tn), lambda j, i: (i, j)),
            out_shape=jax.ShapeDtypeStruct((K, N), bf16), compiler_params=_cparams(2),
        )(a, b)

    Ns = n_slab
    g = N_DEV if Ns % 128 == 0 else 2
    tn = g * Ns

    def body(a_ref, b_ref, o_ref):
        val = lax.dot_general(a_ref[...].astype(bf16), b_ref[...].astype(bf16), (((0,), (0,)), ((), ())),
                              preferred_element_type=f32)
        for s in range(g):
            o_ref[s] = val[:, s * Ns:(s + 1) * Ns].astype(o_ref.dtype)

    return pl.pallas_call(
        body, name=name, grid=(K // tk, N_DEV // g),
        in_specs=[pl.BlockSpec((T, tk), lambda i, j: (0, i)), pl.BlockSpec((T, tn), lambda i, j: (0, j))],
        out_specs=pl.BlockSpec((g, tk, Ns), lambda i, j: (j, i, 0)),
        out_shape=jax.ShapeDtypeStruct((N_DEV, K, Ns), bf16), compiler_params=_cparams(2),
    )(a, b)


def _tile_spec(w, tm, cb):
    if callable(cb):
        return pl.BlockSpec((tm, w), lambda c, i: (i, cb(c)))
    return pl.BlockSpec((tm, w), lambda c, i: (i, cb))


def _param_spec(block, cb):
    nd = len(block)
    if cb is None:
        return pl.BlockSpec(block, lambda c, i: (0,) * nd)
    return pl.BlockSpec(block, lambda c, i: (0,) * (nd - 1) + (cb(c),))


ROW_TILE = 512


def rowwise(name, fn, tiled, params, outs, T, tm, ncol=1):
    tm = min(tm, T)
    n_in = len(tiled) + len(params)
    n_t = len(tiled)

    def body(*refs):
        ins = [r[...].astype(f32) for r in refs[:n_t]] + [r[...] for r in refs[n_t:n_in]]
        res = fn(*ins)
        for r, o in zip(refs[n_in:], res):
            r[...] = o.astype(r.dtype)

    return pl.pallas_call(
        body, name=name, grid=(ncol, T // tm),
        in_specs=[_tile_spec(w, tm, cb) for _, w, cb in tiled] + [_param_spec(blk, cb) for _, blk, cb in params],
        out_specs=[_tile_spec(w, tm, cb) for _, w, cb, _ in outs],
        out_shape=[jax.ShapeDtypeStruct((T, cols), dt) for cols, _, _, dt in outs],
        compiler_params=_cparams(2),
    )(*[a for a, _, _ in tiled], *[a for a, _, _ in params])


def rowwise_bwd(name, fn, tiled, params, cts, grads, T, tm, ncol=1, adds=None):
    tm = min(tm, T)
    n_t, n_p, n_c = len(tiled), len(params), len(cts)
    adds = adds or [None] * n_t
    add_list = [(k, a) for k, a in enumerate(adds) if a is not None]
    want = [k for k, g in enumerate(grads) if g is not None]
    n_a = len(add_list)

    def body(*refs):
        pos = 0
        t_refs = refs[pos:pos + n_t]; pos += n_t
        p_refs = refs[pos:pos + n_p]; pos += n_p
        c_refs = refs[pos:pos + n_c]; pos += n_c
        a_refs = refs[pos:pos + n_a]; pos += n_a
        g_refs = refs[pos:pos + len(want)]; pos += len(want)
        pg_refs = refs[pos:pos + n_p]
        primals = [r[...].astype(f32) for r in t_refs] + [r[...] for r in p_refs]
        _, vjp = jax.vjp(fn, *primals)
        g = vjp(tuple(r[...].astype(f32) for r in c_refs))
        add_of = {k: a_refs[n][...] for n, (k, _) in enumerate(add_list)}
        for n, k in enumerate(want):
            val = g[k]
            if k in add_of:
                val = val + add_of[k]
            g_refs[n][...] = val.astype(g_refs[n].dtype)
        i = pl.program_id(1)
        for k in range(n_p):
            @pl.when(i == 0)
            def _(k=k):
                pg_refs[k][...] = g[n_t + k]

            @pl.when(i > 0)
            def _(k=k):
                pg_refs[k][...] += g[n_t + k]

    in_specs = ([_tile_spec(w, tm, cb) for _, w, cb in tiled] + [_param_spec(blk, cb) for _, blk, cb in params]
                + [_tile_spec(w, tm, cb) for _, w, cb in cts] + [_tile_spec(w, tm, cb) for _, (_, w, cb) in add_list])
    ops = ([a for a, _, _ in tiled] + [a for a, _, _ in params] + [a for a, _, _ in cts]
           + [a for _, (a, _, _) in add_list])
    out_specs, out_shape = [], []
    for k in want:
        w = tiled[k][1]
        out_specs.append(_tile_spec(w, tm, lambda c: c))
        out_shape.append(jax.ShapeDtypeStruct((T, ncol * w), grads[k]))
    for a, blk, cb in params:
        out_specs.append(_param_spec(blk, cb))
        out_shape.append(jax.ShapeDtypeStruct(a.shape, f32))
    return pl.pallas_call(
        body, name=name, grid=(ncol, T // tm), in_specs=in_specs, out_specs=out_specs, out_shape=out_shape,
        compiler_params=_cparams(2),
    )(*ops)


@jax.custom_vjp
def _bdot(a, b):
    return jnp.dot(a.astype(bf16), b.astype(bf16), preferred_element_type=f32)


def _bdot_fwd(a, b):
    return _bdot(a, b), (a, b)


def _bdot_bwd(res, ct):
    a, b = res
    ctb = ct.astype(bf16)
    da = lax.dot_general(ctb, b.astype(bf16), (((1,), (1,)), ((), ())), preferred_element_type=f32)
    db = lax.dot_general(a.astype(bf16), ctb, (((0,), (0,)), ((), ())), preferred_element_type=f32)
    return da, db


_bdot.defvjp(_bdot_fwd, _bdot_bwd)


def _layer_norm(x, g, b):
    mu = jnp.mean(x, axis=-1, keepdims=True)
    xc = x - mu
    y = xc * lax.rsqrt(jnp.mean(xc * xc, axis=-1, keepdims=True) + EPS)
    return y * g + b


def f_rms(x, g):
    y = x * lax.rsqrt(jnp.mean(x * x, axis=-1, keepdims=True) + EPS)
    return (y * g,)


def f_sgu(zu, zv, ln_g, ln_b, wm, sgb_t):
    u = jax.nn.gelu(zu)
    vn = _layer_norm(jax.nn.gelu(zv), ln_g, ln_b)
    row = lax.broadcasted_iota(jnp.int32, (CHUNK, CHUNK), 0)
    col = lax.broadcasted_iota(jnp.int32, (CHUNK, CHUNK), 1)
    tril = col <= row
    low = col < HEAD_DIM
    parts = []
    for p in range(4):
        vp = vn[:, CHUNK * p:CHUNK * (p + 1)]
        w0 = jnp.where(tril, wm[2 * p], 0.0)
        w1 = jnp.where(tril, wm[2 * p + 1], 0.0)
        parts.append(_bdot(w0, jnp.where(low, vp, 0.0)) + _bdot(w1, jnp.where(low, 0.0, vp)))
    s = jnp.concatenate(parts, axis=1)
    lane_g = lax.shift_right_logical(lax.broadcasted_iota(jnp.int32, s.shape, 1), 6)
    bias = jnp.zeros_like(s)
    for g in range(8):
        bias = jnp.where(lane_g == g, sgb_t[:, g:g + 1], bias)
    return (u * (s + bias),)


def f_glu(p, gl):
    return (p * jax.nn.sigmoid(gl),)


def f_lnsilu(c1, g, b):
    return (jax.nn.silu(_layer_norm(c1, g, b)),)


def _head_norm(x, g64):
    g = jnp.concatenate([g64] * N_HEADS, axis=1)
    lane_h = lax.shift_right_logical(lax.broadcasted_iota(jnp.int32, x.shape, 1), 6)
    x2 = x * x
    r = jnp.zeros_like(x)
    for h in range(N_HEADS):
        mh = lane_h == h
        ms = jnp.sum(jnp.where(mh, x2, 0.0), axis=-1, keepdims=True) * (1.0 / HEAD_DIM)
        r = jnp.where(mh, lax.rsqrt(ms + EPS), r)
    return (x * r) * g


def f_qkv(zq, zk, zv, qg, kg):
    return (_head_norm(zq, qg) * 0.125, _head_norm(zk, kg), zv)


def f_merge(g0, g1, g2, ya, yb, yc, b0, b1, b2):
    return (jax.nn.sigmoid(g0 + b0) * ya + jax.nn.sigmoid(g1 + b1) * yb + jax.nn.sigmoid(g2 + b2) * yc,)


HALO = 32


def conv31_fwd(x, w, b, T, tt=256):
    C = x.shape[1]
    r = tt // HALO

    def body(x_ref, h_ref, w_ref, b_ref, y_ref, buf):
        i = pl.program_id(0)
        halo = h_ref[...]
        buf[0:HALO, :] = jnp.where(i > 0, halo, jnp.zeros_like(halo))
        buf[HALO:HALO + tt, :] = x_ref[...]
        acc = jnp.zeros((tt, C), f32) + b_ref[...]
        for k in range(CV_KERNEL):
            acc = acc + w_ref[k:k + 1, :] * buf[pl.ds(HALO - (CV_KERNEL - 1) + k, tt), :]
        y_ref[...] = acc

    return pl.pallas_call(
        body, name="conv31_fwd", grid=(T // tt,),
        in_specs=[pl.BlockSpec((tt, C), lambda i: (i, 0)),
                  pl.BlockSpec((HALO, C), lambda i: (jnp.maximum(i * r - 1, 0), 0)),
                  pl.BlockSpec((CV_KERNEL, C), lambda i: (0, 0)), pl.BlockSpec((1, C), lambda i: (0, 0))],
        out_specs=pl.BlockSpec((tt, C), lambda i: (i, 0)),
        out_shape=jax.ShapeDtypeStruct((T, C), f32),
        scratch_shapes=[pltpu.VMEM((HALO + tt, C), f32)], compiler_params=_cparams(1),
    )(x, x, w, b)


def conv31_bwd(x, w, dy, T, tt=256):
    C = x.shape[1]
    r = tt // HALO
    n = T // tt

    def body(x_ref, h_ref, w_ref, dy_ref, dyn_ref, dx_ref, dw_ref, db_ref, xbuf, dbuf):
        i = pl.program_id(0)
        halo = h_ref[...]
        xbuf[0:HALO, :] = jnp.where(i > 0, halo, jnp.zeros_like(halo))
        xbuf[HALO:HALO + tt, :] = x_ref[...]
        nxt = dyn_ref[...]
        dy = dy_ref[...]
        dbuf[0:tt, :] = dy
        dbuf[tt:tt + HALO, :] = jnp.where(i < n - 1, nxt, jnp.zeros_like(nxt))

        @pl.when(i == 0)
        def _():
            dw_ref[...] = jnp.zeros_like(dw_ref)
            db_ref[...] = jnp.zeros_like(db_ref)

        acc = jnp.zeros((tt, C), f32)
        for k in range(CV_KERNEL):
            acc = acc + w_ref[k:k + 1, :] * dbuf[pl.ds(CV_KERNEL - 1 - k, tt), :]
            xs = xbuf[pl.ds(HALO - (CV_KERNEL - 1) + k, tt), :]
            dw_ref[k:k + 1, :] += jnp.sum(dy * xs, axis=0, keepdims=True)
        dx_ref[...] = acc
        db_ref[...] += jnp.sum(dy, axis=0, keepdims=True)

    return pl.pallas_call(
        body, name="conv31_bwd", grid=(n,),
        in_specs=[pl.BlockSpec((tt, C), lambda i: (i, 0)),
                  pl.BlockSpec((HALO, C), lambda i: (jnp.maximum(i * r - 1, 0), 0)),
                  pl.BlockSpec((CV_KERNEL, C), lambda i: (0, 0)),
                  pl.BlockSpec((tt, C), lambda i: (i, 0)),
                  pl.BlockSpec((HALO, C), lambda i: (jnp.minimum((i + 1) * r, n * r - 1), 0))],
        out_specs=[pl.BlockSpec((tt, C), lambda i: (i, 0)), pl.BlockSpec((CV_KERNEL, C), lambda i: (0, 0)),
                   pl.BlockSpec((1, C), lambda i: (0, 0))],
        out_shape=[jax.ShapeDtypeStruct((T, C), f32), jax.ShapeDtypeStruct((CV_KERNEL, C), f32),
                   jax.ShapeDtypeStruct((1, C), f32)],
        scratch_shapes=[pltpu.VMEM((HALO + tt, C), f32), pltpu.VMEM((HALO + tt, C), f32)], compiler_params=_cparams(1),
    )(x, x, w, dy, dy)


FFN_TC = 128
FFN_PAD = 8


def ffn_act_fwd(up, cw, cb, T):
    F = up.shape[1] // 2
    nj = F // FFN_TC
    rc = min(128, T)

    def body(g_ref, v_ref, g0, g1, g2, gb, v0, v1, v2, vb, o_ref, gp, vp):
        zeros = jnp.zeros((FFN_PAD, FFN_TC), f32)
        for p, x_ref in ((gp, g_ref), (vp, v_ref)):
            p[0:FFN_PAD, :] = zeros
            p[FFN_PAD:FFN_PAD + T, :] = x_ref[...].astype(f32)
        wg = (g0[...], g1[...], g2[...], gb[...])
        wv = (v0[...], v1[...], v2[...], vb[...])

        def conv(p, w, r):
            return (w[0] * p[pl.ds(FFN_PAD + r - 2, rc), :] + w[1] * p[pl.ds(FFN_PAD + r - 1, rc), :]
                    + w[2] * p[pl.ds(FFN_PAD + r, rc), :] + w[3])

        for r in range(0, T, rc):
            o_ref[pl.ds(r, rc), :] = (jax.nn.silu(conv(gp, wg, r)) * conv(vp, wv, r)).astype(o_ref.dtype)

    gspec = pl.BlockSpec((T, FFN_TC), lambda j: (0, j))
    vspec = pl.BlockSpec((T, FFN_TC), lambda j: (0, j + nj))
    pg = pl.BlockSpec((1, FFN_TC), lambda j: (0, j))
    pv = pl.BlockSpec((1, FFN_TC), lambda j: (0, j + nj))
    return pl.pallas_call(
        body, name="ffn_act_fwd", grid=(nj,),
        in_specs=[gspec, vspec, pg, pg, pg, pg, pv, pv, pv, pv], out_specs=gspec,
        out_shape=jax.ShapeDtypeStruct((T, F), bf16),
        scratch_shapes=[pltpu.VMEM((FFN_PAD + T, FFN_TC), f32)] * 2, compiler_params=_cparams(1),
    )(up, up, cw[0], cw[1], cw[2], cb, cw[0], cw[1], cw[2], cb)


def ffn_act_bwd(up, cw, cb, dact, T):
    F = up.shape[1] // 2
    nj = F // FFN_TC

    rc = min(128, T)
    ext = rc + FFN_PAD

    def body(g_ref, v_ref, g0, g1, g2, gb, v0, v1, v2, vb, d_ref, dupg_ref, dupv_ref, *rest):
        p_refs, (gp, vp, dp, dgs, dvs) = rest[:8], rest[8:]
        zeros = jnp.zeros((FFN_PAD, FFN_TC), f32)
        for p, x_ref in ((gp, g_ref), (vp, v_ref)):
            p[0:FFN_PAD, :] = zeros
            p[FFN_PAD:FFN_PAD + T, :] = x_ref[...].astype(f32)
            p[FFN_PAD + T:FFN_PAD + T + FFN_PAD, :] = zeros
        dp[0:T, :] = d_ref[...]
        dp[T:T + FFN_PAD, :] = zeros
        wg = (g0[...], g1[...], g2[...], gb[...])
        wv = (v0[...], v1[...], v2[...], vb[...])
        acc = [jnp.zeros((1, FFN_TC), f32) for _ in range(8)]

        def taps(p, r):
            return tuple(p[pl.ds(FFN_PAD + r - s, ext), :] for s in (2, 1, 0))

        for r in range(0, T, rc):
            xg, xv = taps(gp, r), taps(vp, r)
            gc = wg[0] * xg[0] + wg[1] * xg[1] + wg[2] * xg[2] + wg[3]
            vc = wv[0] * xv[0] + wv[1] * xv[1] + wv[2] * xv[2] + wv[3]
            d = dp[pl.ds(r, ext), :]
            sg = jax.nn.sigmoid(gc)
            sides = ((d * vc * (sg * (1.0 + gc * (1.0 - sg))), xg, wg, dgs, dupg_ref), (d * (gc * sg), xv, wv, dvs, dupv_ref))
            for side, (dc, x, w, buf, dup_ref) in enumerate(sides):
                buf[...] = dc
                dc0 = dc[0:rc]
                dup = w[2] * dc0 + w[1] * buf[pl.ds(1, rc), :] + w[0] * buf[pl.ds(2, rc), :]
                dup_ref[pl.ds(r, rc), :] = dup.astype(dup_ref.dtype)
                for k in range(3):
                    acc[4 * side + k] = acc[4 * side + k] + jnp.sum(dc0 * x[k][0:rc], axis=0, keepdims=True)
                acc[4 * side + 3] = acc[4 * side + 3] + jnp.sum(dc0, axis=0, keepdims=True)
        for k in range(8):
            p_refs[k][...] = acc[k]

    gspec = pl.BlockSpec((T, FFN_TC), lambda j: (0, j))
    vspec = pl.BlockSpec((T, FFN_TC), lambda j: (0, j + nj))
    pg = pl.BlockSpec((1, FFN_TC), lambda j: (0, j))
    pv = pl.BlockSpec((1, FFN_TC), lambda j: (0, j + nj))
    res = pl.pallas_call(
        body, name="ffn_act_bwd", grid=(nj,),
        in_specs=[gspec, vspec, pg, pg, pg, pg, pv, pv, pv, pv, gspec],
        out_specs=[gspec, gspec] + [pg] * 8,
        out_shape=[jax.ShapeDtypeStruct((T, F), bf16)] * 2 + [jax.ShapeDtypeStruct((1, F), f32)] * 8,
        scratch_shapes=[pltpu.VMEM((FFN_PAD + T + FFN_PAD, FFN_TC), f32)] * 2 + [pltpu.VMEM((T + FFN_PAD, FFN_TC), f32)]
        + [pltpu.VMEM((ext, FFN_TC), f32)] * 2,
        compiler_params=_cparams(1),
    )(up, up, cw[0], cw[1], cw[2], cb, cw[0], cw[1], cw[2], cb, dact)
    dup = concat_cols("dup_concat", [res[0], res[1]], T)
    return (dup,) + tuple(jnp.concatenate([res[2 + k], res[6 + k]], axis=1) for k in range(4))


BQ = 256
BK = 256
assert BQ == BK
NT_DIMS = (((1,), (1,)), ((), ()))
TN_DIMS = (((0,), (0,)), ((), ()))


def _split_dot(x, u):
    x1 = x.astype(bf16)
    x2 = (x - x1.astype(f32)).astype(bf16)
    n = x.shape[0]
    y = jnp.dot(jnp.concatenate([x1, x2], axis=0), u, preferred_element_type=f32)
    return y[0:n] + y[n:2 * n]


def _log_sigmoids(z):
    sp = jnp.log(1.0 + jnp.exp(-jnp.abs(z)))
    lsp = jnp.minimum(z, 0.0) - sp
    return lsp, lsp - z


def _stack_heads(x):
    head1 = lax.broadcasted_iota(jnp.int32, x.shape, 1) >= HEAD_DIM
    zero = jnp.zeros_like(x)
    return jnp.concatenate([jnp.where(head1, zero, x), jnp.where(head1, x, zero)], axis=0)


def _unstack_heads(y):
    head1 = lax.broadcasted_iota(jnp.int32, (BQ, y.shape[1]), 1) >= HEAD_DIM
    return jnp.where(head1, y[BQ:2 * BQ], y[0:BQ])


def _attn_masks():
    row = lax.broadcasted_iota(jnp.int32, (2 * BQ, BK), 0)
    col = lax.broadcasted_iota(jnp.int32, (2 * BQ, BK), 1)
    ur = lax.broadcasted_iota(jnp.int32, (BK, BK), 0)
    uc = lax.broadcasted_iota(jnp.int32, (BK, BK), 1)
    return (row & (BQ - 1)) - col, (ur > uc).astype(bf16), (ur < uc).astype(bf16)


def attn_fwd(q, k, v, T, gather=None):
    nq = T // BQ
    n_g = 0 if gather is None else len(gather)

    def body(*refs):
        q_ref, k_ref, v_ref = refs[:3]
        o_ref, tot_ref = refs[3 + n_g:5 + n_g]
        p, i = pl.program_id(0), pl.program_id(1)
        if n_g:
            start, forward, finish = _two_level_gather(refs[5 + n_g:5 + 2 * n_g], *refs[5 + 2 * n_g:])
            pl.when(jnp.logical_and(p == 0, i == 0))(start)
        qs = _stack_heads(q_ref[...])
        diff, u_after, _ = _attn_masks()

        def step(jb, carry, diagonal):
            acc, c = carry
            ks = pl.multiple_of(jb * BK, BK)
            kb = k_ref[pl.ds(ks, BK), :]
            vb = v_ref[pl.ds(ks, BK), :]
            z = lax.dot_general(qs, kb, NT_DIMS, preferred_element_type=f32)
            lsp, lm = _log_sigmoids(z)
            if diagonal:
                m = diff > 0
                lm = jnp.where(m, lm, 0.0)
            a = jnp.exp(lsp + _split_dot(lm, u_after))
            if diagonal:
                a = jnp.where(m, a, 0.0)
            acc = acc + jnp.exp(c) * jnp.dot(a.astype(bf16), vb, preferred_element_type=f32)
            return acc, c + jnp.sum(lm, axis=-1, keepdims=True)

        carry = step(i, (jnp.zeros((2 * BQ, 128), f32), jnp.zeros((2 * BQ, 1), f32)), True)
        acc, c = lax.fori_loop(0, i, lambda t, cr: step(i - 1 - t, cr, False), carry)
        o_ref[...] = _unstack_heads(acc).astype(o_ref.dtype)
        tot_ref[...] = _unstack_heads(jnp.broadcast_to(c, (2 * BQ, 128)))
        if n_g:
            @pl.when(jnp.logical_and(p == 3, i == nq - 1))
            def _():
                forward()
                finish()

    blk = pl.BlockSpec((BQ, 128), lambda p, i: (i, p))
    full = pl.BlockSpec((T, 128), lambda p, i: (0, p))
    bufs = [] if gather is None else gather_buffers(gather)
    res = pl.pallas_call(
        body, name="attn_fwd_gather" if n_g else "attn_fwd", grid=(4, nq),
        in_specs=[blk, full, full] + [_ANY] * n_g, out_specs=[blk, blk] + [_ANY] * n_g,
        out_shape=[jax.ShapeDtypeStruct((T, 512), bf16), jax.ShapeDtypeStruct((T, 512), f32)]
        + [jax.ShapeDtypeStruct(b.shape, b.dtype) for b in bufs],
        input_output_aliases={3 + k: 2 + k for k in range(n_g)},
        scratch_shapes=_sems(n_g, N_DEV - 1) if n_g else [], compiler_params=_cparams(2),
    )(q, k, v, *bufs)
    return res[0], res[1], list(res[2:])


def attn_bwd(q, k, v, do, tot, T, scatter=None):
    nq = T // BQ
    n_s = 0 if scatter is None else len(scatter[0])
    modes = [] if scatter is None else list(scatter[1])

    def body(*refs):
        q_ref, k_ref, v_ref, do_ref, tot_ref = refs[:5]
        dq_ref, dk_ref, dv_ref = refs[5 + 2 * n_s:8 + 2 * n_s]
        p, i = pl.program_id(0), pl.program_id(1)
        if n_s:
            def copies():
                return _chip_copies(refs[5:5 + n_s], refs[8 + 2 * n_s:8 + 3 * n_s], modes, *refs[8 + 3 * n_s:])

            @pl.when(jnp.logical_and(p == 0, i == 0))
            def _():
                for cp in copies():
                    cp.start()

        @pl.when(i == 0)
        def _():
            dk_ref[...] = jnp.zeros_like(dk_ref)
            dv_ref[...] = jnp.zeros_like(dv_ref)

        qs = _stack_heads(q_ref[...])
        dos = _stack_heads(do_ref[...].astype(bf16))
        totv = tot_ref[...]
        tots = jnp.concatenate([totv[:, 0:1], totv[:, HEAD_DIM:HEAD_DIM + 1]], axis=0)
        diff, u_after, u_before = _attn_masks()

        def step(jb, carry, diagonal):
            dq, cl, cg = carry
            ks = pl.multiple_of(jb * BK, BK)
            kb = k_ref[pl.ds(ks, BK), :]
            vb = v_ref[pl.ds(ks, BK), :]
            z = lax.dot_general(qs, kb, NT_DIMS, preferred_element_type=f32)
            lsp, lsn = _log_sigmoids(z)
            lm = lsn
            if diagonal:
                m = diff > 0
                lm = jnp.where(m, lsn, 0.0)
            a = jnp.exp(lsp + _split_dot(lm, u_after))
            if diagonal:
                a = jnp.where(m, a, 0.0)
            g = a * lax.dot_general(dos, vb, NT_DIMS, preferred_element_type=f32)
            bb = jnp.exp(lsp)
            aa = g * jnp.exp(lsn) - _split_dot(g, u_before) * bb
            if diagonal:
                aa = jnp.where(m, aa, 0.0)
                bb = jnp.where(m, bb, 0.0)
            cl = cl + jnp.sum(lm, axis=-1, keepdims=True)
            f = jnp.exp(tots - cl)
            dz = (f * aa - cg * bb).astype(bf16)
            cg = cg + f * jnp.sum(g, axis=-1, keepdims=True)
            dq = dq + jnp.dot(dz, kb, preferred_element_type=f32)
            dk_ref[pl.ds(ks, BK), :] += lax.dot_general(dz, qs, TN_DIMS, preferred_element_type=f32)
            dv_ref[pl.ds(ks, BK), :] += lax.dot_general((f * a).astype(bf16), dos, TN_DIMS, preferred_element_type=f32)
            return dq, cl, cg

        zc = jnp.zeros((2 * BQ, 1), f32)
        carry = lax.fori_loop(0, i, lambda jb, cr: step(jb, cr, False), (jnp.zeros((2 * BQ, 128), f32), zc, zc))
        dq, _, _ = step(i, carry, True)
        dq_ref[...] = _unstack_heads(dq)
        if n_s:
            @pl.when(jnp.logical_and(p == 3, i == nq - 1))
            def _():
                for cp in copies():
                    cp.wait()

    blk = pl.BlockSpec((BQ, 128), lambda p, i: (i, p))
    full = pl.BlockSpec((T, 128), lambda p, i: (0, p))
    shp = jax.ShapeDtypeStruct((T, 512), f32)
    s_list = [] if scatter is None else list(scatter[0])
    bufs = chip_buffers(s_list, modes) if n_s else []
    res = pl.pallas_call(
        body, name="attn_bwd_scatter" if n_s else "attn_bwd", grid=(4, nq),
        in_specs=[blk, full, full, blk, blk] + [_ANY] * (2 * n_s), out_specs=[blk, full, full] + [_ANY] * n_s,
        out_shape=[shp, shp, shp] + [jax.ShapeDtypeStruct(b.shape, b.dtype) for b in bufs],
        input_output_aliases={5 + n_s + k: 3 + k for k in range(n_s)},
        scratch_shapes=_sems(n_s, 3) if n_s else [], compiler_params=_cparams(2),
    )(q, k, v, do, tot, *s_list, *bufs)
    return res[0], res[1], res[2], list(res[3:])


def loss_head(y, target, T, tm=256):
    D = y.shape[1]

    def body(y_ref, t_ref, dy_ref, l_ref):
        i = pl.program_id(0)
        err = y_ref[...] - t_ref[...]
        dy_ref[...] = err * (1.0 / D)
        part = 0.5 * jnp.sum(jnp.sum(err * err, axis=-1, keepdims=True) * (1.0 / D), axis=0, keepdims=True)

        @pl.when(i == 0)
        def _():
            l_ref[...] = jnp.zeros_like(l_ref)

        l_ref[...] += jnp.broadcast_to(part, l_ref.shape)

    spec = pl.BlockSpec((tm, D), lambda i: (i, 0))
    return pl.pallas_call(
        body, name="loss_head", grid=(T // tm,), in_specs=[spec, spec],
        out_specs=[spec, pl.BlockSpec((1, 128), lambda i: (0, 0))],
        out_shape=[jax.ShapeDtypeStruct((T, D), f32), jax.ShapeDtypeStruct((1, 128), f32)],
        compiler_params=_cparams(1),
    )(y, target)


def _row_tile(rows, cols, offset=0, max_elems=128 * 1024):
    best = None
    for t in range(16, rows + 1, 16):
        if rows % t == 0 and offset % t == 0 and t * cols <= max_elems:
            best = t
    return best if best is not None else rows


def adamw(name, w, m, v, parts, layer, prev=None, row_off=0):
    L, R, C = w.shape
    n_parts = parts.shape[0]
    tr = _row_tile(R, C, row_off, max_elems=256 * 1024)
    assert R % tr == 0 and row_off % tr == 0, (name, R, row_off, tr)

    def body(w_ref, m_ref, v_ref, p_ref, *rest):
        g_ref, d_ref, nm_ref, nv_ref = rest[-4:]
        g = p_ref[0].astype(f32)
        for s in range(1, n_parts):
            g = g + p_ref[s].astype(f32)
        wv = w_ref[...]
        mn = ADAM_B1 * m_ref[...] + (1.0 - ADAM_B1) * g
        vn = ADAM_B2 * v_ref[...] + (1.0 - ADAM_B2) * jnp.square(g)
        m_hat = mn / (1.0 - ADAM_B1 ** ADAM_STEP)
        v_hat = vn / (1.0 - ADAM_B2 ** ADAM_STEP)
        g_ref[...] = g
        d_ref[...] = -ADAM_LR * (m_hat / (jnp.sqrt(v_hat) + ADAM_EPS) + ADAM_WD * wv)
        nm_ref[...] = mn
        nv_ref[...] = vn

    spec = pl.BlockSpec((None, tr, C), lambda i: (layer, i, 0))
    shp = jax.ShapeDtypeStruct((L, R, C), f32)
    n_prev = 0 if prev is None else 4
    return pl.pallas_call(
        body, name=name, grid=(R // tr,),
        in_specs=[spec, spec, spec, pl.BlockSpec((n_parts, tr, C), lambda i: (0, row_off // tr + i, 0))] + [_ANY] * n_prev,
        out_specs=[spec] * 4, out_shape=[shp] * 4, input_output_aliases={4 + k: k for k in range(n_prev)},
        compiler_params=_cparams(1),
    )(w, m, v, parts, *(prev or ()))


REPL = ["ln1_g", "sg_ln_g", "sg_ln_b", "sg_w", "sg_b", "cv_b", "cv_ln_g", "cv_ln_b", "q_norm_g", "k_norm_g", "ln2_g",
        "ffn_conv_b"]
SMALL_SHARDED = ["b_gate", "cv_w", "ffn_conv_w"]
BIG = ["w_in", "w_a_out", "w_b_out", "w_c_out", "w_up", "w_out", "w_down"]
COL_SHARDED = ["w_a_out", "w_b_out", "w_c_out"]
TRANSPOSED = ["w_in", "w_up"]


def _pack(arrs, rows):
    flat = jnp.concatenate([a.reshape(-1) for a in arrs])
    return jnp.pad(flat, (0, rows * 128 - flat.shape[0])).reshape(rows, 128)


def _pack_layers(arrs, rows):
    flat = jnp.concatenate([a.reshape(a.shape[0], -1) for a in arrs], axis=1)
    return jnp.pad(flat, ((0, 0), (0, rows * 128 - flat.shape[1]))).reshape(flat.shape[0], rows, 128)


def _unpack_layers(packed, shapes):
    flat = packed.reshape(packed.shape[0], -1)
    out, pos = [], 0
    for s in shapes:
        n = 1
        for d in s[1:]:
            n *= d
        out.append(flat[:, pos:pos + n].reshape(s))
        pos += n
    return out


def _rows_for(shapes, mult):
    n = 0
    for s in shapes:
        k = 1
        for d in s:
            k *= d
        n += k
    rows = -(-n // 128)
    return -(-rows // mult) * mult


W_NAMES = ['ln1_g', 'w_in', 'b_gate', 'sg_ln_g', 'sg_ln_b', 'sg_w', 'sg_b', 'w_a_out', 'cv_w', 'cv_b', 'cv_ln_g', 'cv_ln_b',
           'w_b_out', 'q_norm_g', 'k_norm_g', 'w_c_out', 'w_out', 'ln2_g', 'w_up', 'ffn_conv_w', 'ffn_conv_b', 'w_down']


def _forward_layer(x, P, T, late=None):
    D = x.shape[1]
    sv = {"x0": x}
    (h1,) = rowwise("rms_fwd", f_rms, [(x, D, 0)], [(P["ln1_g"], (1, D), None)], [(D, D, 0, bf16)], T, ROW_TILE)
    z = mm_nt("in_proj", h1, P["w_in_t"], bf16, tm=256, tc=D)
    sv["h1"], sv["z"] = h1, z
    (ya_in,) = rowwise("sgu_fwd", f_sgu, [(z, 512, 0), (z, 512, 1)],
                       [(P["sg_ln_g"], (1, 512), None), (P["sg_ln_b"], (1, 512), None),
                        (P["sg_w"], (8, CHUNK, CHUNK), None), (P["sg_b_t"], (CHUNK, 8), None)],
                       [(512, 512, 0, bf16)], T, CHUNK)
    (c0,) = rowwise("glu_fwd", f_glu, [(z, 512, 2), (z, 512, 3)], [], [(512, 512, 0, f32)], T, ROW_TILE)
    c1 = conv31_fwd(c0, P["cv_w"], P["cv_b"], T)
    (c3,) = rowwise("lnsilu_fwd", f_lnsilu, [(c1, 512, 0)], [(P["cv_ln_g"], (1, 512), None), (P["cv_ln_b"], (1, 512), None)],
                    [(512, 512, 0, bf16)], T, ROW_TILE)
    q8, kn, vb = rowwise("qkv_fwd", f_qkv, [(z, 512, 4), (z, 512, 5), (z, 512, 6)],
                         [(P["q_norm_g"], (1, HEAD_DIM), None), (P["k_norm_g"], (1, HEAD_DIM), None)],
                         [(512, 512, 0, bf16)] * 3, T, ROW_TILE)
    o, tot, _ = attn_fwd(q8, kn, vb, T)
    if late is not None:
        P.update(late(o))
    ya, yb, yc = mm_many("branch_out", [ya_in, c3, o], [P["w_a_out"], P["w_b_out"], P["w_c_out"]], bf16)
    (merged,) = rowwise("merge_fwd", f_merge,
                        [(z, 512, lambda c: 7 + c), (z, 512, lambda c: 9 + c), (z, 512, lambda c: 11 + c),
                         (ya, 512, lambda c: c), (yb, 512, lambda c: c), (yc, 512, lambda c: c)],
                        [(P["b_gate"][k], (1, 512), lambda c: c) for k in range(3)],
                        [(D, 512, lambda c: c, bf16)], T, 256, ncol=2)
    x1 = mm_nn("out_proj", merged, P["w_out"], f32, add=x, tm=256, tn=D)
    sv.update(ya_in=ya_in, ya=ya, c0=c0, c1=c1, c3=c3, yb=yb, q8=q8, kn=kn, vb=vb, o=o, tot=tot, yc=yc, merged=merged, x1=x1)
    (h2,) = rowwise("rms_fwd", f_rms, [(x1, D, 0)], [(P["ln2_g"], (1, D), None)], [(D, D, 0, bf16)], T, ROW_TILE)
    up = mm_nt("up_proj", h2, P["w_up_t"], bf16, tm=256, tc=D)
    act = ffn_act_fwd(up, P["ffn_conv_w"], P["ffn_conv_b"], T)
    x2 = mm_nn("down_proj", act, P["w_down"], f32, add=x1, tm=256, tn=D)
    sv.update(h2=h2, up=up, act=act)
    return x2, sv


def _backward_layer(dx2, P, sv, T, scatter=None):
    D = dx2.shape[1]
    G = {}
    G["w_down"] = mm_tn("dw_down", sv["act"], dx2, tk=1408)
    dact = mm_nt("d_act", dx2, P["w_down"], f32, tm=256, tc=D)
    dup, dcw0, dcw1, dcw2, G["ffn_conv_b"] = ffn_act_bwd(sv["up"], P["ffn_conv_w"], P["ffn_conv_b"], dact, T)
    G["ffn_conv_w"] = jnp.concatenate([dcw0, dcw1, dcw2], axis=0)
    G["w_up"] = mm_tn("dw_up", dup, sv["h2"], tk=1408)
    dh2 = mm_nn("d_h2", dup, P["w_up_t"], f32, tm=256, tn=D)
    dx1, G["ln2_g"] = rowwise_bwd("rms_bwd", f_rms, [(sv["x1"], D, 0)], [(P["ln2_g"], (1, D), None)], [(dh2, D, 0)],
                                  [f32], T, ROW_TILE, adds=[(dx2, D, 0)])
    G["w_out"] = mm_tn("dw_out", sv["merged"], dx1, tk=1024)
    dmerged = mm_nt("d_merged", dx1, P["w_out"], f32)
    z = sv["z"]
    dg0, dg1, dg2, dya, dyb, dyc, db0, db1, db2 = rowwise_bwd(
        "merge_bwd", f_merge,
        [(z, 512, lambda c: 7 + c), (z, 512, lambda c: 9 + c), (z, 512, lambda c: 11 + c),
         (sv["ya"], 512, lambda c: c), (sv["yb"], 512, lambda c: c), (sv["yc"], 512, lambda c: c)],
        [(P["b_gate"][k], (1, 512), lambda c: c) for k in range(3)],
        [(dmerged, 512, lambda c: c)], [bf16] * 6, T, 256, ncol=2)
    G["b_gate"] = jnp.concatenate([db0, db1, db2], axis=0)
    dya_in, dc3, do = mm_many("d_branch_in", [dya, dyb, dyc], [P["w_a_out"], P["w_b_out"], P["w_c_out"]], f32,
                              transpose_w=True)
    G["w_c_out"] = mm_tn("dw_c_out", sv["o"], dyc, n_slab=CHUNK)
    dq8, dkn, dvb, received = attn_bwd(sv["q8"], sv["kn"], sv["vb"], do, sv["tot"], T, scatter=scatter)
    dzq, dzk, dzv, G["q_norm_g"], G["k_norm_g"] = rowwise_bwd(
        "qkv_bwd", f_qkv, [(z, 512, 4), (z, 512, 5), (z, 512, 6)],
        [(P["q_norm_g"], (1, HEAD_DIM), None), (P["k_norm_g"], (1, HEAD_DIM), None)],
        [(dq8, 512, 0), (dkn, 512, 0), (dvb, 512, 0)], [bf16] * 3, T, ROW_TILE)
    G["w_b_out"] = mm_tn("dw_b_out", sv["c3"], dyb, n_slab=CHUNK)
    dc1, G["cv_ln_g"], G["cv_ln_b"] = rowwise_bwd(
        "lnsilu_bwd", f_lnsilu, [(sv["c1"], 512, 0)], [(P["cv_ln_g"], (1, 512), None), (P["cv_ln_b"], (1, 512), None)],
        [(dc3, 512, 0)], [f32], T, ROW_TILE)
    dc0, G["cv_w"], G["cv_b"] = conv31_bwd(sv["c0"], P["cv_w"], dc1, T)
    dzp, dzgl = rowwise_bwd("glu_bwd", f_glu, [(z, 512, 2), (z, 512, 3)], [], [(dc0, 512, 0)], [bf16] * 2, T, ROW_TILE)
    G["w_a_out"] = mm_tn("dw_a_out", sv["ya_in"], dya, n_slab=CHUNK)
    dzu, dzv_a, G["sg_ln_g"], G["sg_ln_b"], G["sg_w"], dsbt = rowwise_bwd(
        "sgu_bwd", f_sgu, [(z, 512, 0), (z, 512, 1)],
        [(P["sg_ln_g"], (1, 512), None), (P["sg_ln_b"], (1, 512), None), (P["sg_w"], (8, CHUNK, CHUNK), None),
         (P["sg_b_t"], (CHUNK, 8), None)],
        [(dya_in, 512, 0)], [bf16] * 2, T, CHUNK)
    G["sg_b"] = dsbt.T
    dz = concat_cols("dz_concat", [dzu, dzv_a, dzp, dzgl, dzq, dzk, dzv, dg0, dg1, dg2], T)
    G["w_in"] = mm_tn("dw_in", dz, sv["h1"], tk=1664)
    dh1 = mm_nn("d_h1", dz, P["w_in_t"], f32, tm=256, tn=D)
    dx0, G["ln1_g"] = rowwise_bwd("rms_bwd", f_rms, [(sv["x0"], D, 0)], [(P["ln1_g"], (1, D), None)], [(dh1, D, 0)],
                                  [f32], T, ROW_TILE, adds=[(dx1, D, 0)])
    return dx0, G, received


def kernel(x, ln1_g, w_in, b_gate, sg_ln_g, sg_ln_b, sg_w, sg_b, w_a_out, cv_w, cv_b, cv_ln_g, cv_ln_b, w_b_out, q_norm_g, k_norm_g, w_c_out, w_out, ln2_g, w_up, ffn_conv_w, ffn_conv_b, w_down, loss_target, m_ln1_g, m_w_in, m_b_gate, m_sg_ln_g, m_sg_ln_b, m_sg_w, m_sg_b, m_w_a_out, m_cv_w, m_cv_b, m_cv_ln_g, m_cv_ln_b, m_w_b_out, m_q_norm_g, m_k_norm_g, m_w_c_out, m_w_out, m_ln2_g, m_w_up, m_ffn_conv_w, m_ffn_conv_b, m_w_down, v_ln1_g, v_w_in, v_b_gate, v_sg_ln_g, v_sg_ln_b, v_sg_w, v_sg_b, v_w_a_out, v_cv_w, v_cv_b, v_cv_ln_g, v_cv_ln_b, v_w_b_out, v_q_norm_g, v_k_norm_g, v_w_c_out, v_w_out, v_ln2_g, v_w_up, v_ffn_conv_w, v_ffn_conv_b, v_w_down):
    W = dict(ln1_g=ln1_g, w_in=w_in, b_gate=b_gate, sg_ln_g=sg_ln_g, sg_ln_b=sg_ln_b, sg_w=sg_w, sg_b=sg_b, w_a_out=w_a_out,
             cv_w=cv_w, cv_b=cv_b, cv_ln_g=cv_ln_g, cv_ln_b=cv_ln_b, w_b_out=w_b_out, q_norm_g=q_norm_g, k_norm_g=k_norm_g,
             w_c_out=w_c_out, w_out=w_out, ln2_g=ln2_g, w_up=w_up, ffn_conv_w=ffn_conv_w, ffn_conv_b=ffn_conv_b, w_down=w_down)
    M = dict(ln1_g=m_ln1_g, w_in=m_w_in, b_gate=m_b_gate, sg_ln_g=m_sg_ln_g, sg_ln_b=m_sg_ln_b, sg_w=m_sg_w, sg_b=m_sg_b,
             w_a_out=m_w_a_out, cv_w=m_cv_w, cv_b=m_cv_b, cv_ln_g=m_cv_ln_g, cv_ln_b=m_cv_ln_b, w_b_out=m_w_b_out,
             q_norm_g=m_q_norm_g, k_norm_g=m_k_norm_g, w_c_out=m_w_c_out, w_out=m_w_out, ln2_g=m_ln2_g, w_up=m_w_up,
             ffn_conv_w=m_ffn_conv_w, ffn_conv_b=m_ffn_conv_b, w_down=m_w_down)
    V = dict(ln1_g=v_ln1_g, w_in=v_w_in, b_gate=v_b_gate, sg_ln_g=v_sg_ln_g, sg_ln_b=v_sg_ln_b, sg_w=v_sg_w, sg_b=v_sg_b,
             w_a_out=v_w_a_out, cv_w=v_cv_w, cv_b=v_cv_b, cv_ln_g=v_cv_ln_g, cv_ln_b=v_cv_ln_b, w_b_out=v_w_b_out,
             q_norm_g=v_q_norm_g, k_norm_g=v_k_norm_g, w_c_out=v_w_c_out, w_out=v_w_out, ln2_g=v_ln2_g, w_up=v_w_up,
             ffn_conv_w=v_ffn_conv_w, ffn_conv_b=v_ffn_conv_b, w_down=v_w_down)
    T, D = x.shape[1], x.shape[2]
    L = DEPTH
    xs = x.reshape(T, D)
    target = loss_target.reshape(T, D)

    ss_shapes = [W[n].shape for n in SMALL_SHARDED]
    ss_rows = _rows_for(ss_shapes, 8)
    (ss_all,), small_token = exchange("gather_small", [_pack([W[n] for n in SMALL_SHARDED], ss_rows)], ["gather"])
    full_small = {}
    pos = 0
    for n in SMALL_SHARDED:
        s = W[n].shape
        cnt = s[0] * s[1] * s[2]
        part = ss_all.reshape(N_DEV, -1)[:, pos:pos + cnt].reshape((N_DEV,) + s)
        full_small[n] = jnp.transpose(part, (1, 2, 0, 3)).reshape(s[0], s[1], N_DEV * s[2])
        pos += cnt

    params, saved = [], []
    cur = xs
    Wt = {n: (jnp.transpose(W[n], (0, 2, 1)) if n in TRANSPOSED else W[n]) for n in BIG}

    def local_slabs(l):
        return [Wt[n][l].astype(bf16) for n in BIG]

    def weights_of(names, slabs):
        out = {}
        for n, s in zip(names, slabs):
            if n in TRANSPOSED:
                out[n + "_t"] = s.reshape(-1, s.shape[-1])
            else:
                out[n] = assemble("assemble_" + n, s) if n in COL_SHARDED else s.reshape(-1, s.shape[-1])
        return out

    LATE = [n for n in BIG if n != "w_in"]
    def behind(token, blocks):
        return [blocks[0] + token[0, 0].astype(blocks[0].dtype)] + list(blocks[1:])

    first, token = gather2("gather_w_in", behind(small_token, [local_slabs(0)[0]]))
    early_part, late_part = {}, {}
    for l in range(L):
        if l > 0:
            early_part[l] = gather_start("gather_start_in_%d" % l, behind(token, local_slabs(l)[:1]))
            token = early_part[l][3]
        late_part[l] = gather_start("gather_start_rest_%d" % l, behind(token, local_slabs(l)[1:]))
        token = late_part[l][3]
    start_token = token[0, 0]

    def late_weights(l):
        send_sems, recv_sems, bufs, _ = late_part[l]
        return lambda o: weights_of(LATE, gather_wait("gather_wait_rest_%d" % l, send_sems, recv_sems, bufs, o))

    for l in range(L):
        if l == 0:
            P = weights_of(["w_in"], first)
        else:
            send_sems, recv_sems, bufs, _ = early_part[l]
            P = weights_of(["w_in"], gather_wait("gather_wait_in_%d" % l, send_sems, recv_sems, bufs, cur))
        late = late_weights(l)
        for n in REPL:
            P[n] = W[n][l]
        for n in ("ln1_g", "sg_ln_g", "sg_ln_b", "cv_b", "cv_ln_g", "cv_ln_b", "q_norm_g", "k_norm_g", "ln2_g", "ffn_conv_b"):
            P[n] = P[n].reshape(1, -1)
        P["sg_b_t"] = P["sg_b"].T
        P["cv_w"] = full_small["cv_w"][l]
        P["b_gate"] = [full_small["b_gate"][l][k:k + 1] for k in range(3)]
        P["ffn_conv_w"] = [full_small["ffn_conv_w"][l][k:k + 1] for k in range(3)]
        if l == 0:
            P["ln1_g"] = P["ln1_g"] + start_token
        cur, sv = _forward_layer(cur, P, T, late=late)
        params.append(P)
        saved.append(sv)

    dy, loss_row = loss_head(cur, target, T)
    loss = lax.psum(loss_row[0, 0], ("x", "y", "c"))

    repl_shapes = [W[n].shape for n in REPL]
    repl_rows = _rows_for([s[1:] for s in repl_shapes], 16)
    ssl_rows = _rows_for([s[1:] for s in ss_shapes], 16)
    state = {"repl": [_pack_layers([X[n] for n in REPL], repl_rows) for X in (W, M, V)],
             "ss": [_pack_layers([X[n] for n in SMALL_SHARDED], ssl_rows) for X in (W, M, V)]}
    WIDE = [n for n in BIG if n not in COL_SHARDED]
    big_rows, big_off = {}, {}
    for n in WIDE:
        big_off[n] = sum(big_rows.values())
        big_rows[n] = W[n].shape[1] * W[n].shape[2] // D
        state[n] = [(jnp.transpose(X[n], (0, 2, 1)) if n in TRANSPOSED else X[n]).reshape(L, big_rows[n], D)
                    for X in (W, M, V)]
    for n in COL_SHARDED:
        state[n] = [W[n], M[n], V[n]]
    done = {n: None for n in state}

    def update(layer, received):
        for n in WIDE:
            done[n] = adamw("adamw_" + n, *state[n], received[0], layer, done[n], row_off=big_off[n])
        done["repl"] = adamw("adamw_repl", *state["repl"], received[1], layer, done["repl"])
        done["ss"] = adamw("adamw_ss", *state["ss"], received[2], layer, done["ss"])
        for k, n in enumerate(COL_SHARDED):
            done[n] = adamw("adamw_" + n, *state[n], received[3], layer, done[n], row_off=k * W[n].shape[1])

    def reduce_in_chip(G):
        big = [G[n].reshape(4, 2, big_rows[n], D) for n in WIDE]
        narrow = jnp.concatenate([G[n].reshape((4, 2) + G[n].shape[1:]) for n in COL_SHARDED], axis=2)
        ss_parts = []
        for n in SMALL_SHARDED:
            k, c = W[n].shape[1:]
            ss_parts.append(jnp.transpose(G[n].reshape(k, N_DEV, c), (1, 0, 2)).reshape(N_DEV, k * c))
        ss_send = jnp.concatenate(ss_parts, axis=1)
        ss_send = jnp.pad(ss_send, ((0, 0), (0, ssl_rows * 128 - ss_send.shape[1]))).reshape(4, 2, ssl_rows, 128)
        repl = _pack([G[n] for n in REPL], repl_rows)
        core = lax.axis_index("c")
        ss_kept = lax.dynamic_index_in_dim(ss_send, core, 1, keepdims=False)
        narrow_kept = lax.dynamic_index_in_dim(narrow, core, 1, keepdims=False)
        got, repl_got, ss_got, narrow_got = pair_exchange("pair_exchange", big, [repl, ss_send, narrow],
                                                          ["gather", "scatter", "scatter"])
        return [pair_add_wide("pair_add_big", big, got), add2("pair_add_repl", repl, repl_got), add2("pair_add_ss", ss_kept, ss_got),
                add2("pair_add_narrow", narrow_kept, narrow_got)]

    chip_modes = ["scatter", "gather", "scatter", "scatter"]
    dcur = dy
    pending = None
    for l in reversed(range(L)):
        P = params[l]
        if pending is not None:
            P = dict(P, ffn_conv_b=P["ffn_conv_b"] + pending[4][0, 0])
        dcur, G, _ = _backward_layer(dcur, P, saved[l], T)
        started = chip_start("chip_start_%d" % l, reduce_in_chip(G), chip_modes)
        if pending is not None:
            after = started[4] + dcur[0:8, 0:128]
            update(l + 1, chip_wait("chip_wait_%d" % (l + 1), *pending[:4], chip_modes, after))
        pending = started
    after = sum(done[n][0][1, 0:1, 0:1] for n in state)
    update(0, chip_wait("chip_wait_0", *pending[:4], chip_modes, after))

    results = [{}, {}, {}, {}]
    for k in range(4):
        for n, a in zip(REPL, _unpack_layers(done["repl"][k], repl_shapes)):
            results[k][n] = a
        for n, a in zip(SMALL_SHARDED, _unpack_layers(done["ss"][k], ss_shapes)):
            results[k][n] = a
        for n in BIG:
            results[k][n] = jnp.transpose(done[n][k], (0, 2, 1)) if n in TRANSPOSED else done[n][k].reshape(W[n].shape)
    out = [loss, dcur.reshape(1, T, D)]
    for k in range(4):
        out += [results[k][n] for n in W_NAMES]
    return tuple(out)
```
